```python
import jax, jax.numpy as jnp
from jax import lax
import numpy as np

D_MODEL = 1024
BATCH = 8
SEQ = 4096
DEPTH = 4

N_MIXERS = 4
EPS = 1e-6
A_CHUNK = 128
A_WIDTH = 2048
A_GROUPS = 4
B_WIDTH = 1024
B_WINDOWS = (2, 4, 8, 16)
B_GROUPS = 4
C_WIDTH = 1024
C_HEADS = 4
C_CONV = 4
C_GATE_C = 8.0
D_WIDTH = 1024
D_CONV = 3
FFN_WIDTH = 2816
FFN_CONV = 3

kernel_name = "hybrid_interleaved_gmlp_pool_rglru_shortconv"


def rms_norm(x, g):
    xf = x.astype(jnp.float32)
    y = xf * lax.rsqrt(jnp.mean(xf * xf, axis=-1, keepdims=True) + EPS)
    return (y * g.astype(jnp.float32)).astype(x.dtype)


def causal_dwconv(x, w):
    k_taps = w.shape[0]
    s = x.shape[1]
    xp = jnp.pad(x, ((0, 0), (k_taps - 1, 0), (0, 0)))
    y = xp[:, 0:s] * w[0]
    for k in range(1, k_taps):
        y = y + xp[:, k:k + s] * w[k]
    return y


def mixer_gmlp(h, w_in, b_in, v_norm_g, w_s, b_s, w_out):
    bsz, s, _ = h.shape
    z = jax.nn.gelu(h @ w_in + b_in)
    u, v = jnp.split(z, 2, axis=-1)
    v = rms_norm(v, v_norm_g)
    gw = A_WIDTH // A_GROUPS
    v = v.reshape(bsz, s // A_CHUNK, A_CHUNK, A_GROUPS, gw)
    mask = jnp.tril(jnp.ones((A_CHUNK, A_CHUNK), dtype=bool))
    ws = jnp.where(mask, w_s, jnp.zeros_like(w_s))
    v = jnp.einsum('gts,bcsgd->bctgd', ws, v) + b_s.T[:, :, None]
    v = v.reshape(bsz, s, A_WIDTH)
    return (u * v) @ w_out


def mixer_pool(h, w_in, w_grp, b_grp, scale, w_out):
    bsz, s, _ = h.shape
    z = h @ w_in
    gw = B_WIDTH // B_GROUPS
    zf = z.astype(jnp.float32)
    csum = jnp.pad(jnp.cumsum(zf, axis=1), ((0, 0), (1, 0), (0, 0)))
    pos = jnp.arange(s)
    pooled = []
    for gi, win in enumerate(B_WINDOWS):
        c = csum[..., gi * gw:(gi + 1) * gw]
        upper = c[:, 1:]
        lower = jnp.pad(c[:, :s + 1 - win], ((0, 0), (win - 1, 0), (0, 0)))
        cnt = jnp.minimum(pos + 1, win).astype(jnp.float32)
        pooled.append((upper - lower) / cnt[None, :, None])
    p = (jnp.concatenate(pooled, axis=-1) - zf).astype(z.dtype)
    p = p.reshape(bsz, s, B_GROUPS, gw)
    y = jnp.einsum('bsgd,gde->bsge', p, w_grp) + b_grp
    y = y.reshape(bsz, s, B_WIDTH) * scale
    return y @ w_out


def mixer_rglru(h, w_in, b_in, conv_w, conv_b, w_a, b_a, w_i, b_i, lam, w_out):
    bsz, s, _ = h.shape
    z = h @ w_in + b_in
    gate, xr = jnp.split(z, 2, axis=-1)
    gate = jax.nn.gelu(gate)
    xr = causal_dwconv(xr, conv_w) + conv_b
    hw = C_WIDTH // C_HEADS
    xh = xr.reshape(bsz, s, C_HEADS, hw)
    r = jax.nn.sigmoid(jnp.einsum('bshd,hde->bshe', xh, w_a) + b_a).reshape(bsz, s, C_WIDTH)
    ig = jax.nn.sigmoid(jnp.einsum('bshd,hde->bshe', xh, w_i) + b_i).reshape(bsz, s, C_WIDTH)
    log_a = (-C_GATE_C * r.astype(jnp.float32)) * jax.nn.softplus(-lam.astype(jnp.float32))
    a = jnp.exp(log_a)
    mult = jnp.sqrt(-jnp.expm1(2.0 * log_a))
    bterm = mult * (ig * xr).astype(jnp.float32)

    def combine(left, right):
        a1, b1 = left
        a2, b2 = right
        return a1 * a2, a2 * b1 + b2

    _, hs = lax.associative_scan(combine, (a, bterm), axis=1)
    y = hs.astype(h.dtype) * gate
    return y @ w_out


def mixer_shortconv(h, w_in, conv_w, w_out):
    z = h @ w_in
    b_gate, c_gate, xv = jnp.split(z, 3, axis=-1)
    y = b_gate * causal_dwconv(c_gate * xv, conv_w)
    return y @ w_out


def conv_ffn(h, w_up, conv_w, conv_b, w_down):
    z = causal_dwconv(h @ w_up, conv_w) + conv_b
    g, v = jnp.split(z, 2, axis=-1)
    return (jax.nn.silu(g) * v) @ w_down


def _fwd_setup_inputs(seed: int = 0) -> dict:
    key = jax.random.key(seed)
    keys = iter(jax.random.split(key, 64))
    n_a, n_b, n_c, n_d = (len(range(m, DEPTH, N_MIXERS)) for m in range(N_MIXERS))
    d = D_MODEL

    def nrm(shape, scale):
        return scale * jax.random.normal(next(keys), shape, jnp.float32)

    def gain(shape):
        return 1.0 + nrm(shape, 0.02)

    inp = {}
    inp["x"] = nrm((BATCH, SEQ, d), 1.0)
    inp["a_norm_g"] = gain((n_a, d))
    inp["a_w_in"] = nrm((n_a, d, 2 * A_WIDTH), d ** -0.5)
    inp["a_b_in"] = nrm((n_a, 2 * A_WIDTH), 0.02)
    inp["a_v_norm_g"] = gain((n_a, A_WIDTH))
    inp["a_w_s"] = nrm((n_a, A_GROUPS, A_CHUNK, A_CHUNK), A_CHUNK ** -0.5)
    inp["a_b_s"] = 1.0 + nrm((n_a, A_GROUPS, A_CHUNK), 0.1)
    inp["a_w_out"] = nrm((n_a, A_WIDTH, d), 0.5 * A_WIDTH ** -0.5)
    gwb = B_WIDTH // B_GROUPS
    inp["b_norm_g"] = gain((n_b, d))
    inp["b_w_in"] = nrm((n_b, d, B_WIDTH), d ** -0.5)
    inp["b_w_grp"] = nrm((n_b, B_GROUPS, gwb, gwb), gwb ** -0.5)
    inp["b_b_grp"] = nrm((n_b, B_GROUPS, gwb), 0.02)
    inp["b_scale"] = 1.0 + nrm((n_b, B_WIDTH), 0.1)
    inp["b_w_out"] = nrm((n_b, B_WIDTH, d), 0.5 * B_WIDTH ** -0.5)
    hwc = C_WIDTH // C_HEADS
    inp["c_norm_g"] = gain((n_c, d))
    inp["c_w_in"] = nrm((n_c, d, 2 * C_WIDTH), d ** -0.5)
    inp["c_b_in"] = nrm((n_c, 2 * C_WIDTH), 0.02)
    inp["c_conv_w"] = nrm((n_c, C_CONV, C_WIDTH), C_CONV ** -0.5)
    inp["c_conv_b"] = nrm((n_c, C_WIDTH), 0.02)
    inp["c_w_a"] = nrm((n_c, C_HEADS, hwc, hwc), hwc ** -0.5)
    inp["c_b_a"] = nrm((n_c, C_HEADS, hwc), 0.02)
    inp["c_w_i"] = nrm((n_c, C_HEADS, hwc, hwc), hwc ** -0.5)
    inp["c_b_i"] = nrm((n_c, C_HEADS, hwc), 0.02)
    u = jax.random.uniform(next(keys), (n_c, C_WIDTH), jnp.float32, minval=0.9, maxval=0.999)
    sgm = u ** (1.0 / C_GATE_C)
    inp["c_lambda"] = jnp.log(sgm) - jnp.log1p(-sgm)
    inp["c_w_out"] = nrm((n_c, C_WIDTH, d), C_WIDTH ** -0.5)
    inp["d_norm_g"] = gain((n_d, d))
    inp["d_w_in"] = nrm((n_d, d, 3 * D_WIDTH), d ** -0.5)
    inp["d_conv_w"] = nrm((n_d, D_CONV, D_WIDTH), D_CONV ** -0.5)
    inp["d_w_out"] = nrm((n_d, D_WIDTH, d), 0.5 * D_WIDTH ** -0.5)
    inp["ffn_norm_g"] = gain((DEPTH, d))
    inp["ffn_w_up"] = nrm((DEPTH, d, 2 * FFN_WIDTH), d ** -0.5)
    inp["ffn_conv_w"] = nrm((DEPTH, FFN_CONV, 2 * FFN_WIDTH), FFN_CONV ** -0.5)
    inp["ffn_conv_b"] = nrm((DEPTH, 2 * FFN_WIDTH), 0.02)
    inp["ffn_w_down"] = nrm((DEPTH, FFN_WIDTH, d), FFN_WIDTH ** -0.5)
    inp["final_norm_g"] = gain((d,))
    return inp


def _fwd_reference(x,
              a_norm_g, a_w_in, a_b_in, a_v_norm_g, a_w_s, a_b_s, a_w_out,
              b_norm_g, b_w_in, b_w_grp, b_b_grp, b_scale, b_w_out,
              c_norm_g, c_w_in, c_b_in, c_conv_w, c_conv_b, c_w_a, c_b_a, c_w_i, c_b_i, c_lambda, c_w_out,
              d_norm_g, d_w_in, d_conv_w, d_w_out,
              ffn_norm_g, ffn_w_up, ffn_conv_w, ffn_conv_b, ffn_w_down,
              final_norm_g):
    for layer in range(DEPTH):
        m, j = layer % N_MIXERS, layer // N_MIXERS
        if m == 0:
            x = x + mixer_gmlp(rms_norm(x, a_norm_g[j]), a_w_in[j], a_b_in[j], a_v_norm_g[j],
                               a_w_s[j], a_b_s[j], a_w_out[j])
        elif m == 1:
            x = x + mixer_pool(rms_norm(x, b_norm_g[j]), b_w_in[j], b_w_grp[j], b_b_grp[j],
                               b_scale[j], b_w_out[j])
        elif m == 2:
            x = x + mixer_rglru(rms_norm(x, c_norm_g[j]), c_w_in[j], c_b_in[j], c_conv_w[j], c_conv_b[j],
                                c_w_a[j], c_b_a[j], c_w_i[j], c_b_i[j], c_lambda[j], c_w_out[j])
        else:
            x = x + mixer_shortconv(rms_norm(x, d_norm_g[j]), d_w_in[j], d_conv_w[j], d_w_out[j])
        x = x + conv_ffn(rms_norm(x, ffn_norm_g[layer]), ffn_w_up[layer], ffn_conv_w[layer],
                         ffn_conv_b[layer], ffn_w_down[layer])
    return rms_norm(x, final_norm_g)


import jax as _jax
import jax.numpy as _jnp

TWIN_FORMAT = 'train_step'
FWD_PARAMS = ['x', 'a_norm_g', 'a_w_in', 'a_b_in', 'a_v_norm_g', 'a_w_s', 'a_b_s', 'a_w_out', 'b_norm_g', 'b_w_in', 'b_w_grp', 'b_b_grp', 'b_scale', 'b_w_out', 'c_norm_g', 'c_w_in', 'c_b_in', 'c_conv_w', 'c_conv_b', 'c_w_a', 'c_b_a', 'c_w_i', 'c_b_i', 'c_lambda', 'c_w_out', 'd_norm_g', 'd_w_in', 'd_conv_w', 'd_w_out', 'ffn_norm_g', 'ffn_w_up', 'ffn_conv_w', 'ffn_conv_b', 'ffn_w_down', 'final_norm_g']
TWIN_WEIGHTS = ['a_norm_g', 'a_w_in', 'a_b_in', 'a_v_norm_g', 'a_w_s', 'a_b_s', 'a_w_out', 'b_norm_g', 'b_w_in', 'b_w_grp', 'b_b_grp', 'b_scale', 'b_w_out', 'c_norm_g', 'c_w_in', 'c_b_in', 'c_conv_w', 'c_conv_b', 'c_w_a', 'c_b_a', 'c_w_i', 'c_b_i', 'c_lambda', 'c_w_out', 'd_norm_g', 'd_w_in', 'd_conv_w', 'd_w_out', 'ffn_norm_g', 'ffn_w_up', 'ffn_conv_w', 'ffn_conv_b', 'ffn_w_down', 'final_norm_g']
TWIN_DIFF_INPUT = 'x'
TWIN_INPUTS = ['x', 'a_norm_g', 'a_w_in', 'a_b_in', 'a_v_norm_g', 'a_w_s', 'a_b_s', 'a_w_out', 'b_norm_g', 'b_w_in', 'b_w_grp', 'b_b_grp', 'b_scale', 'b_w_out', 'c_norm_g', 'c_w_in', 'c_b_in', 'c_conv_w', 'c_conv_b', 'c_w_a', 'c_b_a', 'c_w_i', 'c_b_i', 'c_lambda', 'c_w_out', 'd_norm_g', 'd_w_in', 'd_conv_w', 'd_w_out', 'ffn_norm_g', 'ffn_w_up', 'ffn_conv_w', 'ffn_conv_b', 'ffn_w_down', 'final_norm_g', 'loss_target', 'm_a_norm_g', 'm_a_w_in', 'm_a_b_in', 'm_a_v_norm_g', 'm_a_w_s', 'm_a_b_s', 'm_a_w_out', 'm_b_norm_g', 'm_b_w_in', 'm_b_w_grp', 'm_b_b_grp', 'm_b_scale', 'm_b_w_out', 'm_c_norm_g', 'm_c_w_in', 'm_c_b_in', 'm_c_conv_w', 'm_c_conv_b', 'm_c_w_a', 'm_c_b_a', 'm_c_w_i', 'm_c_b_i', 'm_c_lambda', 'm_c_w_out', 'm_d_norm_g', 'm_d_w_in', 'm_d_conv_w', 'm_d_w_out', 'm_ffn_norm_g', 'm_ffn_w_up', 'm_ffn_conv_w', 'm_ffn_conv_b', 'm_ffn_w_down', 'm_final_norm_g', 'v_a_norm_g', 'v_a_w_in', 'v_a_b_in', 'v_a_v_norm_g', 'v_a_w_s', 'v_a_b_s', 'v_a_w_out', 'v_b_norm_g', 'v_b_w_in', 'v_b_w_grp', 'v_b_b_grp', 'v_b_scale', 'v_b_w_out', 'v_c_norm_g', 'v_c_w_in', 'v_c_b_in', 'v_c_conv_w', 'v_c_conv_b', 'v_c_w_a', 'v_c_b_a', 'v_c_w_i', 'v_c_b_i', 'v_c_lambda', 'v_c_w_out', 'v_d_norm_g', 'v_d_w_in', 'v_d_conv_w', 'v_d_w_out', 'v_ffn_norm_g', 'v_ffn_w_up', 'v_ffn_conv_w', 'v_ffn_conv_b', 'v_ffn_w_down', 'v_final_norm_g']
TWIN_OUTPUTS = ['loss', 'grad_x', 'grad_a_norm_g', 'grad_a_w_in', 'grad_a_b_in', 'grad_a_v_norm_g', 'grad_a_w_s', 'grad_a_b_s', 'grad_a_w_out', 'grad_b_norm_g', 'grad_b_w_in', 'grad_b_w_grp', 'grad_b_b_grp', 'grad_b_scale', 'grad_b_w_out', 'grad_c_norm_g', 'grad_c_w_in', 'grad_c_b_in', 'grad_c_conv_w', 'grad_c_conv_b', 'grad_c_w_a', 'grad_c_b_a', 'grad_c_w_i', 'grad_c_b_i', 'grad_c_lambda', 'grad_c_w_out', 'grad_d_norm_g', 'grad_d_w_in', 'grad_d_conv_w', 'grad_d_w_out', 'grad_ffn_norm_g', 'grad_ffn_w_up', 'grad_ffn_conv_w', 'grad_ffn_conv_b', 'grad_ffn_w_down', 'grad_final_norm_g', 'delta_a_norm_g', 'delta_a_w_in', 'delta_a_b_in', 'delta_a_v_norm_g', 'delta_a_w_s', 'delta_a_b_s', 'delta_a_w_out', 'delta_b_norm_g', 'delta_b_w_in', 'delta_b_w_grp', 'delta_b_b_grp', 'delta_b_scale', 'delta_b_w_out', 'delta_c_norm_g', 'delta_c_w_in', 'delta_c_b_in', 'delta_c_conv_w', 'delta_c_conv_b', 'delta_c_w_a', 'delta_c_b_a', 'delta_c_w_i', 'delta_c_b_i', 'delta_c_lambda', 'delta_c_w_out', 'delta_d_norm_g', 'delta_d_w_in', 'delta_d_conv_w', 'delta_d_w_out', 'delta_ffn_norm_g', 'delta_ffn_w_up', 'delta_ffn_conv_w', 'delta_ffn_conv_b', 'delta_ffn_w_down', 'delta_final_norm_g', 'new_m_a_norm_g', 'new_m_a_w_in', 'new_m_a_b_in', 'new_m_a_v_norm_g', 'new_m_a_w_s', 'new_m_a_b_s', 'new_m_a_w_out', 'new_m_b_norm_g', 'new_m_b_w_in', 'new_m_b_w_grp', 'new_m_b_b_grp', 'new_m_b_scale', 'new_m_b_w_out', 'new_m_c_norm_g', 'new_m_c_w_in', 'new_m_c_b_in', 'new_m_c_conv_w', 'new_m_c_conv_b', 'new_m_c_w_a', 'new_m_c_b_a', 'new_m_c_w_i', 'new_m_c_b_i', 'new_m_c_lambda', 'new_m_c_w_out', 'new_m_d_norm_g', 'new_m_d_w_in', 'new_m_d_conv_w', 'new_m_d_w_out', 'new_m_ffn_norm_g', 'new_m_ffn_w_up', 'new_m_ffn_conv_w', 'new_m_ffn_conv_b', 'new_m_ffn_w_down', 'new_m_final_norm_g', 'new_v_a_norm_g', 'new_v_a_w_in', 'new_v_a_b_in', 'new_v_a_v_norm_g', 'new_v_a_w_s', 'new_v_a_b_s', 'new_v_a_w_out', 'new_v_b_norm_g', 'new_v_b_w_in', 'new_v_b_w_grp', 'new_v_b_b_grp', 'new_v_b_scale', 'new_v_b_w_out', 'new_v_c_norm_g', 'new_v_c_w_in', 'new_v_c_b_in', 'new_v_c_conv_w', 'new_v_c_conv_b', 'new_v_c_w_a', 'new_v_c_b_a', 'new_v_c_w_i', 'new_v_c_b_i', 'new_v_c_lambda', 'new_v_c_w_out', 'new_v_d_norm_g', 'new_v_d_w_in', 'new_v_d_conv_w', 'new_v_d_w_out', 'new_v_ffn_norm_g', 'new_v_ffn_w_up', 'new_v_ffn_conv_w', 'new_v_ffn_conv_b', 'new_v_ffn_w_down', 'new_v_final_norm_g']
TWIN_LEAF_KINDS = {'loss': 'loss', 'grad_x': 'grad_x', 'grad_a_norm_g': 'grad_w', 'grad_a_w_in': 'grad_w', 'grad_a_b_in': 'grad_w', 'grad_a_v_norm_g': 'grad_w', 'grad_a_w_s': 'grad_w', 'grad_a_b_s': 'grad_w', 'grad_a_w_out': 'grad_w', 'grad_b_norm_g': 'grad_w', 'grad_b_w_in': 'grad_w', 'grad_b_w_grp': 'grad_w', 'grad_b_b_grp': 'grad_w', 'grad_b_scale': 'grad_w', 'grad_b_w_out': 'grad_w', 'grad_c_norm_g': 'grad_w', 'grad_c_w_in': 'grad_w', 'grad_c_b_in': 'grad_w', 'grad_c_conv_w': 'grad_w', 'grad_c_conv_b': 'grad_w', 'grad_c_w_a': 'grad_w', 'grad_c_b_a': 'grad_w', 'grad_c_w_i': 'grad_w', 'grad_c_b_i': 'grad_w', 'grad_c_lambda': 'grad_w', 'grad_c_w_out': 'grad_w', 'grad_d_norm_g': 'grad_w', 'grad_d_w_in': 'grad_w', 'grad_d_conv_w': 'grad_w', 'grad_d_w_out': 'grad_w', 'grad_ffn_norm_g': 'grad_w', 'grad_ffn_w_up': 'grad_w', 'grad_ffn_conv_w': 'grad_w', 'grad_ffn_conv_b': 'grad_w', 'grad_ffn_w_down': 'grad_w', 'grad_final_norm_g': 'grad_w', 'delta_a_norm_g': 'delta_w', 'delta_a_w_in': 'delta_w', 'delta_a_b_in': 'delta_w', 'delta_a_v_norm_g': 'delta_w', 'delta_a_w_s': 'delta_w', 'delta_a_b_s': 'delta_w', 'delta_a_w_out': 'delta_w', 'delta_b_norm_g': 'delta_w', 'delta_b_w_in': 'delta_w', 'delta_b_w_grp': 'delta_w', 'delta_b_b_grp': 'delta_w', 'delta_b_scale': 'delta_w', 'delta_b_w_out': 'delta_w', 'delta_c_norm_g': 'delta_w', 'delta_c_w_in': 'delta_w', 'delta_c_b_in': 'delta_w', 'delta_c_conv_w': 'delta_w', 'delta_c_conv_b': 'delta_w', 'delta_c_w_a': 'delta_w', 'delta_c_b_a': 'delta_w', 'delta_c_w_i': 'delta_w', 'delta_c_b_i': 'delta_w', 'delta_c_lambda': 'delta_w', 'delta_c_w_out': 'delta_w', 'delta_d_norm_g': 'delta_w', 'delta_d_w_in': 'delta_w', 'delta_d_conv_w': 'delta_w', 'delta_d_w_out': 'delta_w', 'delta_ffn_norm_g': 'delta_w', 'delta_ffn_w_up': 'delta_w', 'delta_ffn_conv_w': 'delta_w', 'delta_ffn_conv_b': 'delta_w', 'delta_ffn_w_down': 'delta_w', 'delta_final_norm_g': 'delta_w', 'new_m_a_norm_g': 'new_m', 'new_m_a_w_in': 'new_m', 'new_m_a_b_in': 'new_m', 'new_m_a_v_norm_g': 'new_m', 'new_m_a_w_s': 'new_m', 'new_m_a_b_s': 'new_m', 'new_m_a_w_out': 'new_m', 'new_m_b_norm_g': 'new_m', 'new_m_b_w_in': 'new_m', 'new_m_b_w_grp': 'new_m', 'new_m_b_b_grp': 'new_m', 'new_m_b_scale': 'new_m', 'new_m_b_w_out': 'new_m', 'new_m_c_norm_g': 'new_m', 'new_m_c_w_in': 'new_m', 'new_m_c_b_in': 'new_m', 'new_m_c_conv_w': 'new_m', 'new_m_c_conv_b': 'new_m', 'new_m_c_w_a': 'new_m', 'new_m_c_b_a': 'new_m', 'new_m_c_w_i': 'new_m', 'new_m_c_b_i': 'new_m', 'new_m_c_lambda': 'new_m', 'new_m_c_w_out': 'new_m', 'new_m_d_norm_g': 'new_m', 'new_m_d_w_in': 'new_m', 'new_m_d_conv_w': 'new_m', 'new_m_d_w_out': 'new_m', 'new_m_ffn_norm_g': 'new_m', 'new_m_ffn_w_up': 'new_m', 'new_m_ffn_conv_w': 'new_m', 'new_m_ffn_conv_b': 'new_m', 'new_m_ffn_w_down': 'new_m', 'new_m_final_norm_g': 'new_m', 'new_v_a_norm_g': 'new_v', 'new_v_a_w_in': 'new_v', 'new_v_a_b_in': 'new_v', 'new_v_a_v_norm_g': 'new_v', 'new_v_a_w_s': 'new_v', 'new_v_a_b_s': 'new_v', 'new_v_a_w_out': 'new_v', 'new_v_b_norm_g': 'new_v', 'new_v_b_w_in': 'new_v', 'new_v_b_w_grp': 'new_v', 'new_v_b_b_grp': 'new_v', 'new_v_b_scale': 'new_v', 'new_v_b_w_out': 'new_v', 'new_v_c_norm_g': 'new_v', 'new_v_c_w_in': 'new_v', 'new_v_c_b_in': 'new_v', 'new_v_c_conv_w': 'new_v', 'new_v_c_conv_b': 'new_v', 'new_v_c_w_a': 'new_v', 'new_v_c_b_a': 'new_v', 'new_v_c_w_i': 'new_v', 'new_v_c_b_i': 'new_v', 'new_v_c_lambda': 'new_v', 'new_v_c_w_out': 'new_v', 'new_v_d_norm_g': 'new_v', 'new_v_d_w_in': 'new_v', 'new_v_d_conv_w': 'new_v', 'new_v_d_w_out': 'new_v', 'new_v_ffn_norm_g': 'new_v', 'new_v_ffn_w_up': 'new_v', 'new_v_ffn_conv_w': 'new_v', 'new_v_ffn_conv_b': 'new_v', 'new_v_ffn_w_down': 'new_v', 'new_v_final_norm_g': 'new_v'}


def _forward(args):
    return _fwd_reference(*[args[k] for k in FWD_PARAMS])


def _output_shape():
    out = _jax.eval_shape(lambda: _forward(_fwd_setup_inputs(0)))
    return out.shape, out.dtype

N_MICROBATCH = 1
ADAM_LR = 0.001
ADAM_B1 = 0.9
ADAM_B2 = 0.999
ADAM_EPS = 1e-08
ADAM_WD = 0.01
ADAM_STEP = 10
PER_EXAMPLE_BATCH_AXIS = {'x': 0, 'loss_target': 0}
SHARED_INPUTS = []
_WEIGHT_DTYPES = {'a_norm_g': _jnp.float32, 'a_w_in': _jnp.float32, 'a_b_in': _jnp.float32, 'a_v_norm_g': _jnp.float32, 'a_w_s': _jnp.float32, 'a_b_s': _jnp.float32, 'a_w_out': _jnp.float32, 'b_norm_g': _jnp.float32, 'b_w_in': _jnp.float32, 'b_w_grp': _jnp.float32, 'b_b_grp': _jnp.float32, 'b_scale': _jnp.float32, 'b_w_out': _jnp.float32, 'c_norm_g': _jnp.float32, 'c_w_in': _jnp.float32, 'c_b_in': _jnp.float32, 'c_conv_w': _jnp.float32, 'c_conv_b': _jnp.float32, 'c_w_a': _jnp.float32, 'c_b_a': _jnp.float32, 'c_w_i': _jnp.float32, 'c_b_i': _jnp.float32, 'c_lambda': _jnp.float32, 'c_w_out': _jnp.float32, 'd_norm_g': _jnp.float32, 'd_w_in': _jnp.float32, 'd_conv_w': _jnp.float32, 'd_w_out': _jnp.float32, 'ffn_norm_g': _jnp.float32, 'ffn_w_up': _jnp.float32, 'ffn_conv_w': _jnp.float32, 'ffn_conv_b': _jnp.float32, 'ffn_w_down': _jnp.float32, 'final_norm_g': _jnp.float32}
MOMENT_SCALE = {'a_norm_g': 1.188634e-01, 'a_w_in': 6.022351e-02, 'a_b_in': 8.306684e-02, 'a_v_norm_g': 4.051343e-02, 'a_w_s': 8.156757e-02, 'a_b_s': 1.149070e-01, 'a_w_out': 2.360216e-01, 'b_norm_g': 8.268402e-02, 'b_w_in': 8.134760e-02, 'b_w_grp': 8.095123e-02, 'b_b_grp': 2.549869e-01, 'b_scale': 7.925196e-02, 'b_w_out': 1.628057e-01, 'c_norm_g': 1.141892e-01, 'c_w_in': 8.733539e-02, 'c_b_in': 5.146309e-01, 'c_conv_w': 1.023820e-01, 'c_conv_b': 7.567097e-01, 'c_w_a': 1.718915e-02, 'c_b_a': 1.569993e-02, 'c_w_i': 3.029056e-02, 'c_b_i': 2.972665e-02, 'c_lambda': 3.622285e-02, 'c_w_out': 8.571881e-02, 'd_norm_g': 9.697384e-02, 'd_w_in': 5.495359e-02, 'd_conv_w': 5.477819e-02, 'd_w_out': 1.097308e-01, 'ffn_norm_g': 1.312236e-01, 'ffn_w_up': 5.550239e-02, 'ffn_conv_w': 5.596572e-02, 'ffn_conv_b': 6.177862e-02, 'ffn_w_down': 9.057728e-02, 'final_norm_g': 3.198680e+01}


def _to_microbatches(a, axis):
    t = _jnp.moveaxis(a, axis, 0)
    t = t.reshape((N_MICROBATCH, t.shape[0] // N_MICROBATCH) + t.shape[1:])
    return _jnp.moveaxis(t, 1, axis + 1)


def setup_inputs(seed: int = 0) -> dict:
    inp = _fwd_setup_inputs(seed)
    key = _jax.random.fold_in(_jax.random.key(seed), 7919)
    shape, _ = _output_shape()
    out = dict(inp)
    out["loss_target"] = _jax.random.normal(_jax.random.fold_in(key, 0), shape, _jnp.float32)
    for i, name in enumerate(TWIN_WEIGHTS):
        w = inp[name].astype(_jnp.float32)
        if MOMENT_SCALE is None:
            s = _jnp.sqrt(_jnp.mean(_jnp.square(w)) + 1e-30)
        else:
            s = MOMENT_SCALE[name]
        km, kv = _jax.random.split(_jax.random.fold_in(key, i + 1))
        out[name] = w
        out["m_" + name] = s * _jax.random.normal(km, w.shape, _jnp.float32)
        out["v_" + name] = (s * s) * _jax.random.uniform(kv, w.shape, _jnp.float32, 0.5, 1.5)
    if N_MICROBATCH > 1:
        for name, axis in PER_EXAMPLE_BATCH_AXIS.items():
            out[name] = _to_microbatches(out[name], axis)
    return {'x': out['x'], 'a_norm_g': out['a_norm_g'], 'a_w_in': out['a_w_in'], 'a_b_in': out['a_b_in'], 'a_v_norm_g': out['a_v_norm_g'], 'a_w_s': out['a_w_s'], 'a_b_s': out['a_b_s'], 'a_w_out': out['a_w_out'], 'b_norm_g': out['b_norm_g'], 'b_w_in': out['b_w_in'], 'b_w_grp': out['b_w_grp'], 'b_b_grp': out['b_b_grp'], 'b_scale': out['b_scale'], 'b_w_out': out['b_w_out'], 'c_norm_g': out['c_norm_g'], 'c_w_in': out['c_w_in'], 'c_b_in': out['c_b_in'], 'c_conv_w': out['c_conv_w'], 'c_conv_b': out['c_conv_b'], 'c_w_a': out['c_w_a'], 'c_b_a': out['c_b_a'], 'c_w_i': out['c_w_i'], 'c_b_i': out['c_b_i'], 'c_lambda': out['c_lambda'], 'c_w_out': out['c_w_out'], 'd_norm_g': out['d_norm_g'], 'd_w_in': out['d_w_in'], 'd_conv_w': out['d_conv_w'], 'd_w_out': out['d_w_out'], 'ffn_norm_g': out['ffn_norm_g'], 'ffn_w_up': out['ffn_w_up'], 'ffn_conv_w': out['ffn_conv_w'], 'ffn_conv_b': out['ffn_conv_b'], 'ffn_w_down': out['ffn_w_down'], 'final_norm_g': out['final_norm_g'], 'loss_target': out['loss_target'], 'm_a_norm_g': out['m_a_norm_g'], 'm_a_w_in': out['m_a_w_in'], 'm_a_b_in': out['m_a_b_in'], 'm_a_v_norm_g': out['m_a_v_norm_g'], 'm_a_w_s': out['m_a_w_s'], 'm_a_b_s': out['m_a_b_s'], 'm_a_w_out': out['m_a_w_out'], 'm_b_norm_g': out['m_b_norm_g'], 'm_b_w_in': out['m_b_w_in'], 'm_b_w_grp': out['m_b_w_grp'], 'm_b_b_grp': out['m_b_b_grp'], 'm_b_scale': out['m_b_scale'], 'm_b_w_out': out['m_b_w_out'], 'm_c_norm_g': out['m_c_norm_g'], 'm_c_w_in': out['m_c_w_in'], 'm_c_b_in': out['m_c_b_in'], 'm_c_conv_w': out['m_c_conv_w'], 'm_c_conv_b': out['m_c_conv_b'], 'm_c_w_a': out['m_c_w_a'], 'm_c_b_a': out['m_c_b_a'], 'm_c_w_i': out['m_c_w_i'], 'm_c_b_i': out['m_c_b_i'], 'm_c_lambda': out['m_c_lambda'], 'm_c_w_out': out['m_c_w_out'], 'm_d_norm_g': out['m_d_norm_g'], 'm_d_w_in': out['m_d_w_in'], 'm_d_conv_w': out['m_d_conv_w'], 'm_d_w_out': out['m_d_w_out'], 'm_ffn_norm_g': out['m_ffn_norm_g'], 'm_ffn_w_up': out['m_ffn_w_up'], 'm_ffn_conv_w': out['m_ffn_conv_w'], 'm_ffn_conv_b': out['m_ffn_conv_b'], 'm_ffn_w_down': out['m_ffn_w_down'], 'm_final_norm_g': out['m_final_norm_g'], 'v_a_norm_g': out['v_a_norm_g'], 'v_a_w_in': out['v_a_w_in'], 'v_a_b_in': out['v_a_b_in'], 'v_a_v_norm_g': out['v_a_v_norm_g'], 'v_a_w_s': out['v_a_w_s'], 'v_a_b_s': out['v_a_b_s'], 'v_a_w_out': out['v_a_w_out'], 'v_b_norm_g': out['v_b_norm_g'], 'v_b_w_in': out['v_b_w_in'], 'v_b_w_grp': out['v_b_w_grp'], 'v_b_b_grp': out['v_b_b_grp'], 'v_b_scale': out['v_b_scale'], 'v_b_w_out': out['v_b_w_out'], 'v_c_norm_g': out['v_c_norm_g'], 'v_c_w_in': out['v_c_w_in'], 'v_c_b_in': out['v_c_b_in'], 'v_c_conv_w': out['v_c_conv_w'], 'v_c_conv_b': out['v_c_conv_b'], 'v_c_w_a': out['v_c_w_a'], 'v_c_b_a': out['v_c_b_a'], 'v_c_w_i': out['v_c_w_i'], 'v_c_b_i': out['v_c_b_i'], 'v_c_lambda': out['v_c_lambda'], 'v_c_w_out': out['v_c_w_out'], 'v_d_norm_g': out['v_d_norm_g'], 'v_d_w_in': out['v_d_w_in'], 'v_d_conv_w': out['v_d_conv_w'], 'v_d_w_out': out['v_d_w_out'], 'v_ffn_norm_g': out['v_ffn_norm_g'], 'v_ffn_w_up': out['v_ffn_w_up'], 'v_ffn_conv_w': out['v_ffn_conv_w'], 'v_ffn_conv_b': out['v_ffn_conv_b'], 'v_ffn_w_down': out['v_ffn_w_down'], 'v_final_norm_g': out['v_final_norm_g']}


def _loss(weights, diff, rest, loss_target):
    with _jax.named_scope("forward"):
        args = {**rest, TWIN_DIFF_INPUT: diff, **{k: w.astype(_WEIGHT_DTYPES[k]) for k, w in weights.items()}}
        y = _forward(args)
    with _jax.named_scope("loss_head"):
        err = _jnp.square(y.astype(_jnp.float32) - loss_target)
        return 0.5 * _jnp.sum(_jnp.mean(err, axis=-1)) if err.ndim else 0.5 * err


def _adamw(w, g, m, v):
    m = ADAM_B1 * m + (1.0 - ADAM_B1) * g
    v = ADAM_B2 * v + (1.0 - ADAM_B2) * _jnp.square(g)
    m_hat = m / (1.0 - ADAM_B1 ** ADAM_STEP)
    v_hat = v / (1.0 - ADAM_B2 ** ADAM_STEP)
    delta = -ADAM_LR * (m_hat / (_jnp.sqrt(v_hat) + ADAM_EPS) + ADAM_WD * w)
    return delta, m, v


def reference(x, a_norm_g, a_w_in, a_b_in, a_v_norm_g, a_w_s, a_b_s, a_w_out, b_norm_g, b_w_in, b_w_grp, b_b_grp, b_scale, b_w_out, c_norm_g, c_w_in, c_b_in, c_conv_w, c_conv_b, c_w_a, c_b_a, c_w_i, c_b_i, c_lambda, c_w_out, d_norm_g, d_w_in, d_conv_w, d_w_out, ffn_norm_g, ffn_w_up, ffn_conv_w, ffn_conv_b, ffn_w_down, final_norm_g, loss_target, m_a_norm_g, m_a_w_in, m_a_b_in, m_a_v_norm_g, m_a_w_s, m_a_b_s, m_a_w_out, m_b_norm_g, m_b_w_in, m_b_w_grp, m_b_b_grp, m_b_scale, m_b_w_out, m_c_norm_g, m_c_w_in, m_c_b_in, m_c_conv_w, m_c_conv_b, m_c_w_a, m_c_b_a, m_c_w_i, m_c_b_i, m_c_lambda, m_c_w_out, m_d_norm_g, m_d_w_in, m_d_conv_w, m_d_w_out, m_ffn_norm_g, m_ffn_w_up, m_ffn_conv_w, m_ffn_conv_b, m_ffn_w_down, m_final_norm_g, v_a_norm_g, v_a_w_in, v_a_b_in, v_a_v_norm_g, v_a_w_s, v_a_b_s, v_a_w_out, v_b_norm_g, v_b_w_in, v_b_w_grp, v_b_b_grp, v_b_scale, v_b_w_out, v_c_norm_g, v_c_w_in, v_c_b_in, v_c_conv_w, v_c_conv_b, v_c_w_a, v_c_b_a, v_c_w_i, v_c_b_i, v_c_lambda, v_c_w_out, v_d_norm_g, v_d_w_in, v_d_conv_w, v_d_w_out, v_ffn_norm_g, v_ffn_w_up, v_ffn_conv_w, v_ffn_conv_b, v_ffn_w_down, v_final_norm_g):
    given = dict(x=x, a_norm_g=a_norm_g, a_w_in=a_w_in, a_b_in=a_b_in, a_v_norm_g=a_v_norm_g, a_w_s=a_w_s, a_b_s=a_b_s, a_w_out=a_w_out, b_norm_g=b_norm_g, b_w_in=b_w_in, b_w_grp=b_w_grp, b_b_grp=b_b_grp, b_scale=b_scale, b_w_out=b_w_out, c_norm_g=c_norm_g, c_w_in=c_w_in, c_b_in=c_b_in, c_conv_w=c_conv_w, c_conv_b=c_conv_b, c_w_a=c_w_a, c_b_a=c_b_a, c_w_i=c_w_i, c_b_i=c_b_i, c_lambda=c_lambda, c_w_out=c_w_out, d_norm_g=d_norm_g, d_w_in=d_w_in, d_conv_w=d_conv_w, d_w_out=d_w_out, ffn_norm_g=ffn_norm_g, ffn_w_up=ffn_w_up, ffn_conv_w=ffn_conv_w, ffn_conv_b=ffn_conv_b, ffn_w_down=ffn_w_down, final_norm_g=final_norm_g, loss_target=loss_target, m_a_norm_g=m_a_norm_g, m_a_w_in=m_a_w_in, m_a_b_in=m_a_b_in, m_a_v_norm_g=m_a_v_norm_g, m_a_w_s=m_a_w_s, m_a_b_s=m_a_b_s, m_a_w_out=m_a_w_out, m_b_norm_g=m_b_norm_g, m_b_w_in=m_b_w_in, m_b_w_grp=m_b_w_grp, m_b_b_grp=m_b_b_grp, m_b_scale=m_b_scale, m_b_w_out=m_b_w_out, m_c_norm_g=m_c_norm_g, m_c_w_in=m_c_w_in, m_c_b_in=m_c_b_in, m_c_conv_w=m_c_conv_w, m_c_conv_b=m_c_conv_b, m_c_w_a=m_c_w_a, m_c_b_a=m_c_b_a, m_c_w_i=m_c_w_i, m_c_b_i=m_c_b_i, m_c_lambda=m_c_lambda, m_c_w_out=m_c_w_out, m_d_norm_g=m_d_norm_g, m_d_w_in=m_d_w_in, m_d_conv_w=m_d_conv_w, m_d_w_out=m_d_w_out, m_ffn_norm_g=m_ffn_norm_g, m_ffn_w_up=m_ffn_w_up, m_ffn_conv_w=m_ffn_conv_w, m_ffn_conv_b=m_ffn_conv_b, m_ffn_w_down=m_ffn_w_down, m_final_norm_g=m_final_norm_g, v_a_norm_g=v_a_norm_g, v_a_w_in=v_a_w_in, v_a_b_in=v_a_b_in, v_a_v_norm_g=v_a_v_norm_g, v_a_w_s=v_a_w_s, v_a_b_s=v_a_b_s, v_a_w_out=v_a_w_out, v_b_norm_g=v_b_norm_g, v_b_w_in=v_b_w_in, v_b_w_grp=v_b_w_grp, v_b_b_grp=v_b_b_grp, v_b_scale=v_b_scale, v_b_w_out=v_b_w_out, v_c_norm_g=v_c_norm_g, v_c_w_in=v_c_w_in, v_c_b_in=v_c_b_in, v_c_conv_w=v_c_conv_w, v_c_conv_b=v_c_conv_b, v_c_w_a=v_c_w_a, v_c_b_a=v_c_b_a, v_c_w_i=v_c_w_i, v_c_b_i=v_c_b_i, v_c_lambda=v_c_lambda, v_c_w_out=v_c_w_out, v_d_norm_g=v_d_norm_g, v_d_w_in=v_d_w_in, v_d_conv_w=v_d_conv_w, v_d_w_out=v_d_w_out, v_ffn_norm_g=v_ffn_norm_g, v_ffn_w_up=v_ffn_w_up, v_ffn_conv_w=v_ffn_conv_w, v_ffn_conv_b=v_ffn_conv_b, v_ffn_w_down=v_ffn_w_down, v_final_norm_g=v_final_norm_g)
    weights = {n: given[n] for n in TWIN_WEIGHTS}
    shared = {n: given[n] for n in SHARED_INPUTS}
    per_example = {n: given[n] for n in ['x']}
    grad_fn = _jax.value_and_grad(_loss, argnums=(0, 1))

    def one_microbatch(ex, loss_target):
        ex = dict(ex)
        diff = ex.pop(TWIN_DIFF_INPUT)
        return grad_fn(weights, diff, {**shared, **ex}, loss_target)

    if N_MICROBATCH == 1:
        loss, (grad_w, grad_x) = one_microbatch(per_example, given["loss_target"])
    else:
        def body(carry, xs):
            loss_sum, grad_sum = carry
            l_k, (gw_k, gx_k) = one_microbatch(xs[0], xs[1])
            with _jax.named_scope("update"):
                return (loss_sum + l_k, _jax.tree.map(_jnp.add, grad_sum, gw_k)), gx_k

        init = (_jnp.zeros((), _jnp.float32), _jax.tree.map(_jnp.zeros_like, weights))
        (loss, grad_w), grad_x = _jax.lax.scan(body, init, (per_example, given["loss_target"]))
    with _jax.named_scope("update"):
        delta_w, new_m, new_v = {}, {}, {}
        for n in TWIN_WEIGHTS:
            delta_w[n], new_m[n], new_v[n] = _adamw(weights[n], grad_w[n], given["m_" + n], given["v_" + n])
    return (loss, grad_x, *[grad_w[n] for n in TWIN_WEIGHTS], *[delta_w[n] for n in TWIN_WEIGHTS],
            *[new_m[n] for n in TWIN_WEIGHTS], *[new_v[n] for n in TWIN_WEIGHTS])
```

```python
import functools

import jax
import jax.numpy as jnp
from jax import lax
from jax.experimental import pallas as pl
from jax.experimental.pallas import tpu as pltpu

F32, BF16 = jnp.float32, jnp.bfloat16
MESH_ID = pl.DeviceIdType.MESH
N_DEV = 8
V7X_VMEM_LIMIT_BYTES = 56 << 20
LANES = 128
HALO = 8
POOL_HALO = 16

EPS = 1e-6
A_CHUNK, A_GROUPS = 128, 4
B_WINDOWS = (2, 4, 8, 16)
C_GATE_C = 8.0
ADAM_LR, ADAM_B1, ADAM_B2, ADAM_EPS, ADAM_WD, ADAM_STEP = 0.001, 0.9, 0.999, 1e-08, 0.01, 10

WEIGHTS = ['a_norm_g', 'a_w_in', 'a_b_in', 'a_v_norm_g', 'a_w_s', 'a_b_s', 'a_w_out', 'b_norm_g', 'b_w_in', 'b_w_grp',
           'b_b_grp', 'b_scale', 'b_w_out', 'c_norm_g', 'c_w_in', 'c_b_in', 'c_conv_w', 'c_conv_b', 'c_w_a', 'c_b_a',
           'c_w_i', 'c_b_i', 'c_lambda', 'c_w_out', 'd_norm_g', 'd_w_in', 'd_conv_w', 'd_w_out', 'ffn_norm_g',
           'ffn_w_up', 'ffn_conv_w', 'ffn_conv_b', 'ffn_w_down', 'final_norm_g']
SMALL_SHARDED = {'b_norm_g': 1, 'b_w_grp': 2, 'b_b_grp': 2, 'b_scale': 1, 'c_norm_g': 1, 'c_b_in': 1, 'c_conv_w': 2,
                 'c_conv_b': 1, 'c_w_a': 2, 'c_b_a': 2, 'c_w_i': 2, 'c_b_i': 2, 'c_lambda': 1, 'd_norm_g': 1,
                 'd_conv_w': 2, 'ffn_conv_w': 2}
REPLICATED = ['a_norm_g', 'a_b_in', 'a_v_norm_g', 'a_w_s', 'a_b_s', 'ffn_norm_g', 'ffn_conv_b', 'final_norm_g']


_GELU_C0, _GELU_C1 = 0.7978845608028654, 0.044715


def _gelu(x):
    return 0.5 * x * (1.0 + jnp.tanh(_GELU_C0 * (x + _GELU_C1 * (x * x * x))))


def _gelu_grad(x):
    t = jnp.tanh(_GELU_C0 * (x + _GELU_C1 * (x * x * x)))
    return 0.5 * (1.0 + t) + 0.5 * x * (1.0 - t * t) * (_GELU_C0 * (1.0 + 3.0 * _GELU_C1 * (x * x)))


def _sigmoid(x):
    return jax.nn.sigmoid(x)


def _log1p(x):
    u = 1.0 + x
    return jnp.where(u == 1.0, x, jnp.log(u) * (x / (u - 1.0)))


def _softplus(x):
    return jnp.maximum(x, 0.0) + _log1p(jnp.exp(-jnp.abs(x)))


def _expm1(x):
    poly = x * (1.0 + x * (1 / 2) * (1.0 + x * (1 / 3) * (1.0 + x * (1 / 4) * (1.0 + x * (1 / 5) * (
        1.0 + x * (1 / 6) * (1.0 + x * (1 / 7) * (1.0 + x * (1 / 8))))))))
    return jnp.where(jnp.abs(x) < 0.35, poly, jnp.exp(x) - 1.0)


def _down(xe, s):
    return xe if s == 0 else pltpu.roll(xe, s, 0)


def _up(xe, s):
    return xe if s == 0 else pltpu.roll(xe, xe.shape[0] - s, 0)


def _conv_ext(xe, w, taps):
    y = xe * w[taps - 1:taps]
    for s in range(1, taps):
        y = y + _down(xe, s) * w[taps - 1 - s:taps - s]
    return y


def _acc(ref, val, first):
    @pl.when(first)
    def _():
        ref[...] = val

    @pl.when(jnp.logical_not(first))
    def _():
        ref[...] += val


def _colsum(v):
    return jnp.sum(v, axis=0, keepdims=True)


def _dot(a, b, dims=((1,), (0,))):
    return lax.dot_general(a.astype(BF16), b.astype(BF16), (dims, ((), ())), preferred_element_type=F32)


_NN, _NT, _TN = ((1,), (0,)), ((1,), (1,)), ((0,), (0,))


def _call(body, name, grid, in_specs, out_specs, out_shape, scratch=()):
    return pl.pallas_call(
        body, name=name, grid=grid, in_specs=in_specs, out_specs=out_specs, out_shape=out_shape,
        scratch_shapes=list(scratch),
        compiler_params=pltpu.CompilerParams(dimension_semantics=("arbitrary",) * len(grid),
                                             vmem_limit_bytes=V7X_VMEM_LIMIT_BYTES))


def _rows(m, t):
    t = min(m, t)
    assert m % t == 0, (m, t)
    return t


def _sds(shape, dtype=F32):
    return jax.ShapeDtypeStruct(tuple(shape), dtype)


def _prev_halo(tm, halo=HALO):
    return lambda i: jnp.maximum(i * (tm // halo) - 1, 0)


def _next_halo(tm, m, halo=HALO):
    return lambda i: jnp.minimum((i + 1) * (tm // halo), m // halo - 1)


def _matmul(name, a, b, out_shape, grid, a_spec, b_spec, o_spec, dims, reduce_last, extras=(), extra_specs=()):
    n_red = grid[-1] if reduce_last else 1
    n_grid = len(grid)

    def body(a_ref, b_ref, *rest):
        o_ref, ex = rest[-1], rest[:-1]
        p = _dot(a_ref[...], b_ref[...], dims)

        def first():
            q = p
            for e in ex:
                q = q + e[...]
            return q.astype(o_ref.dtype)

        if n_red == 1:
            o_ref[...] = first()
        else:
            r = pl.program_id(n_grid - 1)

            @pl.when(r == 0)
            def _():
                o_ref[...] = first()

            @pl.when(r > 0)
            def _():
                o_ref[...] += p

    return _call(body, name, grid, [a_spec, b_spec, *extra_specs], o_spec, out_shape)(a, b, *extras)


def mm_in(h, w_st, name, bias=None, stacked_out=False):
    m, k = h.shape
    nb, _, n = w_st.shape
    tm = _rows(m, 1024)
    a_spec = pl.BlockSpec((tm, k), lambda i, j: (i, 0))
    b_spec = pl.BlockSpec((None, k, n), lambda i, j: (j, 0, 0))
    if stacked_out:
        out, o_spec = _sds((nb, m, n)), pl.BlockSpec((None, tm, n), lambda i, j: (j, i, 0))
    else:
        out, o_spec = _sds((m, nb * n)), pl.BlockSpec((tm, n), lambda i, j: (i, j))
    extras, especs = (), ()
    if bias is not None:
        extras, especs = (bias,), (pl.BlockSpec((1, n), lambda i, j: (0, j)),)
    return _matmul(name, h, w_st, out, (m // tm, nb), a_spec, b_spec, o_spec, _NN, False, extras, especs)


def _split_rows(kf):
    g = max(1, kf // 1024)
    return g, kf // g


def mm_out(y, w, res, name):
    kf, n = w.shape
    if y.ndim == 3:
        g, m, k = y.shape
        a_spec_of = lambda tm: pl.BlockSpec((None, tm, k), lambda i, r: (r, i, 0))
    else:
        m = y.shape[0]
        g, k = _split_rows(kf)
        a_spec_of = lambda tm: pl.BlockSpec((tm, k), lambda i, r: (i, r))
    tm = _rows(m, 1024)
    b_spec = pl.BlockSpec((None, k, n), lambda i, r: (r, 0, 0))
    o_spec = pl.BlockSpec((tm, n), lambda i, r: (i, 0))
    return _matmul(name, y, w.reshape(g, k, n), _sds((m, n)), (m // tm, g), a_spec_of(tm), b_spec, o_spec, _NN, True,
                   (res,), (o_spec,))


def mm_dx_in(dz, w_st, name):
    nb, k, n = w_st.shape
    m = dz.shape[-2]
    tm = _rows(m, 1024)
    if dz.ndim == 3:
        a_spec = pl.BlockSpec((None, tm, n), lambda i, r: (r, i, 0))
    else:
        a_spec = pl.BlockSpec((tm, n), lambda i, r: (i, r))
    b_spec = pl.BlockSpec((None, k, n), lambda i, r: (r, 0, 0))
    o_spec = pl.BlockSpec((tm, k), lambda i, r: (i, 0))
    return _matmul(name, dz, w_st, _sds((m, k)), (m // tm, nb), a_spec, b_spec, o_spec, _NT, True)


def mm_dx_out(dout, w, name, groups=None):
    kf, n = w.shape
    m = dout.shape[0]
    tm = _rows(m, 1024)
    g, k = (groups, kf // groups) if groups else _split_rows(kf)
    a_spec = pl.BlockSpec((tm, n), lambda i, j: (i, 0))
    b_spec = pl.BlockSpec((None, k, n), lambda i, j: (j, 0, 0))
    if groups:
        out, o_spec = _sds((g, m, k)), pl.BlockSpec((None, tm, k), lambda i, j: (j, i, 0))
    else:
        out, o_spec = _sds((m, kf)), pl.BlockSpec((tm, k), lambda i, j: (i, j))
    return _matmul(name, dout, w.reshape(g, k, n), out, (m // tm, g), a_spec, b_spec, o_spec, _NT, False)


def mm_dw_in(h, dz, nb, name):
    m, k = h.shape
    tm = _rows(m, 1024)
    a_spec = pl.BlockSpec((tm, k), lambda j, r: (r, 0))
    if dz.ndim == 3:
        n = dz.shape[2]
        b_spec = pl.BlockSpec((None, tm, n), lambda j, r: (j, r, 0))
    else:
        n = dz.shape[1] // nb
        b_spec = pl.BlockSpec((tm, n), lambda j, r: (r, j))
    o_spec = pl.BlockSpec((None, k, n), lambda j, r: (j, 0, 0))
    return _matmul(name, h, dz, _sds((nb, k, n)), (nb, m // tm), a_spec, b_spec, o_spec, _TN, True)


def mm_dw_out(y, dout, name):
    m, n = dout.shape
    tm = _rows(m, 1024)
    if y.ndim == 3:
        g, _, k = y.shape
        a_spec = pl.BlockSpec((None, tm, k), lambda j, r: (j, r, 0))
    else:
        g, k = _split_rows(y.shape[1])
        a_spec = pl.BlockSpec((tm, k), lambda j, r: (r, j))
    b_spec = pl.BlockSpec((tm, n), lambda j, r: (r, 0))
    o_spec = pl.BlockSpec((None, k, n), lambda j, r: (j, 0, 0))
    out = _matmul(name, y, dout, _sds((g, k, n)), (g, m // tm), a_spec, b_spec, o_spec, _TN, True)
    return out.reshape(g * k, n)


def rms_fwd(x, g, name):
    m, d = x.shape
    tm = _rows(m, 512)

    def body(x_ref, g_ref, o_ref):
        xv = x_ref[...]
        rstd = lax.rsqrt(jnp.mean(xv * xv, axis=-1, keepdims=True) + EPS)
        o_ref[...] = (xv * rstd * g_ref[...]).astype(BF16)

    row = pl.BlockSpec((tm, d), lambda i: (i, 0))
    vec = pl.BlockSpec((1, d), lambda i: (0, 0))
    return _call(body, name, (m // tm,), [row, vec], row, _sds((m, d), BF16))(x, g)


def _rms_bwd_math(xv, g, dh):
    rstd = lax.rsqrt(jnp.mean(xv * xv, axis=-1, keepdims=True) + EPS)
    xhat = xv * rstd
    dxhat = dh * g
    dx = rstd * (dxhat - xhat * jnp.mean(dxhat * xhat, axis=-1, keepdims=True))
    return dx, _colsum(dh * xhat)


def rms_bwd(x, g, dh, dres, name):
    m, d = x.shape
    tm = _rows(m, 512)

    def body(x_ref, g_ref, dh_ref, dr_ref, dx_ref, dg_ref):
        dx, dg = _rms_bwd_math(x_ref[...], g_ref[...], dh_ref[...])
        dx_ref[...] = dr_ref[...] + dx
        _acc(dg_ref, dg, pl.program_id(0) == 0)

    row = pl.BlockSpec((tm, d), lambda i: (i, 0))
    vec = pl.BlockSpec((1, d), lambda i: (0, 0))
    return _call(body, name, (m // tm,), [row, vec, row, row], [row, vec], [_sds((m, d)), _sds((1, d))])(x, g, dh, dres)


def final_loss(x, g, target, name):
    m, d = x.shape
    tm = _rows(m, 512)

    def body(x_ref, g_ref, t_ref, l_ref, dx_ref, dg_ref):
        xv, gv = x_ref[...], g_ref[...]
        rstd = lax.rsqrt(jnp.mean(xv * xv, axis=-1, keepdims=True) + EPS)
        err = xv * rstd * gv - t_ref[...]
        part = 0.5 * jnp.sum(jnp.mean(err * err, axis=-1, keepdims=True), axis=0, keepdims=True)
        dx, dg = _rms_bwd_math(xv, gv, err * (1.0 / d))
        dx_ref[...] = dx
        first = pl.program_id(0) == 0
        _acc(l_ref, jnp.broadcast_to(part, l_ref.shape), first)
        _acc(dg_ref, dg, first)

    row = pl.BlockSpec((tm, d), lambda i: (i, 0))
    vec = pl.BlockSpec((1, d), lambda i: (0, 0))
    lsp = pl.BlockSpec((1, LANES), lambda i: (0, 0))
    return _call(body, name, (m // tm,), [row, vec, row], [lsp, row, vec],
                 [_sds((1, LANES)), _sds((m, d)), _sds((1, d))])(x, g, target)


def _a_common(z_ref, vg_ref, ws_ref, bst_ref, tm, width):
    gw = width // A_GROUPS
    zp = z_ref[...]
    z = _gelu(zp)
    u, v = z[:, :width], z[:, width:]
    rstd = lax.rsqrt(jnp.mean(v * v, axis=-1, keepdims=True) + EPS)
    vhat = v * rstd
    vn = vhat * vg_ref[...]
    t_i = lax.broadcasted_iota(jnp.int32, (A_CHUNK, A_CHUNK), 0)
    s_i = lax.broadcasted_iota(jnp.int32, (A_CHUNK, A_CHUNK), 1)
    wsm = [jnp.where(s_i <= t_i, ws_ref[g], 0.0).astype(BF16) for g in range(A_GROUPS)]
    bst = bst_ref[...]
    return zp, u, rstd, vhat, vn.astype(BF16), wsm, bst, gw


def a_mid_fwd(z, vg, ws, bst, name):
    m, w2 = z.shape
    width = w2 // 2
    tm = _rows(m, 256)

    def body(z_ref, vg_ref, ws_ref, bst_ref, y_ref):
        _, u, _, _, vnb, wsm, bst, gw = _a_common(z_ref, vg_ref, ws_ref, bst_ref, tm, width)
        for c in range(tm // A_CHUNK):
            r0 = c * A_CHUNK
            for g in range(A_GROUPS):
                c0 = g * gw
                vs = _dot(wsm[g], vnb[r0:r0 + A_CHUNK, c0:c0 + gw]) + bst[:, g:g + 1]
                y_ref[r0:r0 + A_CHUNK, c0:c0 + gw] = (u[r0:r0 + A_CHUNK, c0:c0 + gw] * vs).astype(BF16)

    in_specs = [pl.BlockSpec((tm, w2), lambda i: (i, 0)), pl.BlockSpec((1, width), lambda i: (0, 0)),
                pl.BlockSpec((A_GROUPS, A_CHUNK, A_CHUNK), lambda i: (0, 0, 0)),
                pl.BlockSpec((A_CHUNK, A_GROUPS), lambda i: (0, 0))]
    return _call(body, name, (m // tm,), in_specs, pl.BlockSpec((tm, width), lambda i: (i, 0)),
                 _sds((m, width), BF16))(z, vg, ws, bst)


def a_mid_bwd(z, dy, vg, ws, bst, name):
    m, w2 = z.shape
    width = w2 // 2
    tm = _rows(m, 256)

    def body(z_ref, dy_ref, vg_ref, ws_ref, bst_ref, dz_ref, dbin_ref, dvg_ref, dws_ref, dbs_ref, dvn_scr, du_scr):
        first = pl.program_id(0) == 0
        zp, u, rstd, vhat, vnb, wsm, bst, gw = _a_common(z_ref, vg_ref, ws_ref, bst_ref, tm, width)
        dy = dy_ref[...]
        dws = [jnp.zeros((A_CHUNK, A_CHUNK), F32) for _ in range(A_GROUPS)]
        dbs = [jnp.zeros((A_CHUNK, 1), F32) for _ in range(A_GROUPS)]
        for c in range(tm // A_CHUNK):
            r0 = c * A_CHUNK
            for g in range(A_GROUPS):
                c0 = g * gw
                vn_cg = vnb[r0:r0 + A_CHUNK, c0:c0 + gw]
                vs = _dot(wsm[g], vn_cg) + bst[:, g:g + 1]
                dy_cg = dy[r0:r0 + A_CHUNK, c0:c0 + gw]
                dvs = dy_cg * u[r0:r0 + A_CHUNK, c0:c0 + gw]
                du_scr[r0:r0 + A_CHUNK, c0:c0 + gw] = dy_cg * vs
                dws[g] = dws[g] + _dot(dvs, vn_cg, _NT)
                dbs[g] = dbs[g] + jnp.sum(dvs, axis=1, keepdims=True)
                dvn_scr[r0:r0 + A_CHUNK, c0:c0 + gw] = _dot(wsm[g], dvs, _TN)
        for g in range(A_GROUPS):
            _acc(dws_ref.at[g], dws[g], first)
            _acc(dbs_ref.at[:, g * LANES:(g + 1) * LANES], jnp.broadcast_to(dbs[g], (A_CHUNK, LANES)), first)
        dvn = dvn_scr[...]
        _acc(dvg_ref, _colsum(dvn * vhat), first)
        dvhat = dvn * vg_ref[...]
        dv = rstd * (dvhat - vhat * jnp.mean(dvhat * vhat, axis=-1, keepdims=True))
        gg = _gelu_grad(zp)
        dzu = du_scr[...] * gg[:, :width]
        dzv = dv * gg[:, width:]
        dz_ref[:, :width] = dzu.astype(BF16)
        dz_ref[:, width:] = dzv.astype(BF16)
        _acc(dbin_ref.at[:, :width], _colsum(dzu), first)
        _acc(dbin_ref.at[:, width:], _colsum(dzv), first)

    const2 = lambda i: (0, 0)
    in_specs = [pl.BlockSpec((tm, w2), lambda i: (i, 0)), pl.BlockSpec((tm, width), lambda i: (i, 0)),
                pl.BlockSpec((1, width), const2), pl.BlockSpec((A_GROUPS, A_CHUNK, A_CHUNK), lambda i: (0, 0, 0)),
                pl.BlockSpec((A_CHUNK, A_GROUPS), const2)]
    out_specs = [pl.BlockSpec((tm, w2), lambda i: (i, 0)), pl.BlockSpec((1, w2), const2), pl.BlockSpec((1, width), const2),
                 pl.BlockSpec((A_GROUPS, A_CHUNK, A_CHUNK), lambda i: (0, 0, 0)),
                 pl.BlockSpec((A_CHUNK, A_GROUPS * LANES), const2)]
    out_shape = [_sds((m, w2), BF16), _sds((1, w2)), _sds((1, width)), _sds((A_GROUPS, A_CHUNK, A_CHUNK)),
                 _sds((A_CHUNK, A_GROUPS * LANES))]
    scratch = [pltpu.VMEM((tm, width), F32), pltpu.VMEM((tm, width), F32)]
    return _call(body, name, (m // tm,), in_specs, out_specs, out_shape, scratch)(z, dy, vg, ws, bst)


def _pool_minus_id(ze, i, tm, gw):
    pos = i * tm + lax.broadcasted_iota(jnp.int32, (tm, 1), 0)
    out = []
    for gi, win in enumerate(B_WINDOWS):
        s = ze[:, gi * gw:(gi + 1) * gw]
        step = 1
        while step < win:
            s = s + _down(s, step)
            step *= 2
        inv = 1.0 / jnp.minimum(pos + 1, win).astype(F32)
        out.append(s[POOL_HALO:] * inv - ze[POOL_HALO:, gi * gw:(gi + 1) * gw])
    return out


def _b_specs(tm, width):
    return [pl.BlockSpec((POOL_HALO, width), lambda i: (_prev_halo(tm, POOL_HALO)(i), 0)),
            pl.BlockSpec((tm, width), lambda i: (i, 0))]


def b_mid_fwd(z, wgrp, bgrp, scale, name):
    m, width = z.shape
    ng = len(B_WINDOWS)
    gw = width // ng
    tm = _rows(m, 512)

    def body(zp_ref, zm_ref, w_ref, b_ref, s_ref, y_ref):
        i = pl.program_id(0)
        ze = jnp.concatenate([zp_ref[...] * (i > 0).astype(F32), zm_ref[...]], axis=0)
        p = _pool_minus_id(ze, i, tm, gw)
        for g in range(ng):
            cs = slice(g * gw, (g + 1) * gw)
            y = (_dot(p[g], w_ref[g]) + b_ref[:, cs]) * s_ref[:, cs]
            y_ref[:, cs] = y.astype(BF16)

    vec = pl.BlockSpec((1, width), lambda i: (0, 0))
    in_specs = _b_specs(tm, width) + [pl.BlockSpec((ng, gw, gw), lambda i: (0, 0, 0)), vec, vec]
    return _call(body, name, (m // tm,), in_specs, pl.BlockSpec((tm, width), lambda i: (i, 0)),
                 _sds((m, width), BF16))(z, z, wgrp, bgrp, scale)


def b_mid_bwd(z, dy, wgrp, bgrp, scale, name):
    m, width = z.shape
    ng = len(B_WINDOWS)
    gw = width // ng
    tm = _rows(m, 512)

    def body(zp_ref, zm_ref, dy_ref, w_ref, b_ref, s_ref, dp_ref, dw_ref, db_ref, ds_ref):
        i = pl.program_id(0)
        first = i == 0
        ze = jnp.concatenate([zp_ref[...] * (i > 0).astype(F32), zm_ref[...]], axis=0)
        p = _pool_minus_id(ze, i, tm, gw)
        for g in range(ng):
            cs = slice(g * gw, (g + 1) * gw)
            dyg = dy_ref[:, cs]
            ypre = _dot(p[g], w_ref[g]) + b_ref[:, cs]
            dyp = dyg * s_ref[:, cs]
            _acc(ds_ref.at[:, cs], _colsum(dyg * ypre), first)
            _acc(db_ref.at[:, cs], _colsum(dyp), first)
            _acc(dw_ref.at[g], _dot(p[g], dyp, _TN), first)
            dp_ref[:, cs] = _dot(dyp, w_ref[g], _NT)

    vec = pl.BlockSpec((1, width), lambda i: (0, 0))
    row = pl.BlockSpec((tm, width), lambda i: (i, 0))
    wsp = pl.BlockSpec((ng, gw, gw), lambda i: (0, 0, 0))
    return _call(body, name, (m // tm,), _b_specs(tm, width) + [row, wsp, vec, vec], [row, wsp, vec, vec],
                 [_sds((m, width)), _sds((ng, gw, gw)), _sds((1, width)), _sds((1, width))])(z, z, dy, wgrp, bgrp, scale)


def b_pool_bwd(dp, name):
    m, width = dp.shape
    gw = width // len(B_WINDOWS)
    tm = _rows(m, 512)
    n_i = m // tm

    def body(dm_ref, dn_ref, dz_ref):
        i = pl.program_id(0)
        de = jnp.concatenate([dm_ref[...], dn_ref[...] * (i < n_i - 1).astype(F32)], axis=0)
        pos = i * tm + lax.broadcasted_iota(jnp.int32, (tm + POOL_HALO, 1), 0)
        for gi, win in enumerate(B_WINDOWS):
            cs = slice(gi * gw, (gi + 1) * gw)
            d = de[:, cs]
            s = d * (1.0 / jnp.minimum(pos + 1, win).astype(F32))
            step = 1
            while step < win:
                s = s + _up(s, step)
                step *= 2
            dz_ref[:, cs] = (s[:tm] - d[:tm]).astype(BF16)

    in_specs = [pl.BlockSpec((tm, width), lambda i: (i, 0)),
                pl.BlockSpec((POOL_HALO, width), lambda i: (_next_halo(tm, m, POOL_HALO)(i), 0))]
    return _call(body, name, (n_i,), in_specs, pl.BlockSpec((tm, width), lambda i: (i, 0)), _sds((m, width), BF16))(dp, dp)


def _c_gates(xr, wa_ref, ba_ref, wi_ref, bi_ref, lam_ref, heads, hw):
    xb = xr.astype(BF16)
    ra = jnp.concatenate([_dot(xb[:, h * hw:(h + 1) * hw], wa_ref[h]) for h in range(heads)], axis=1) + ba_ref[...]
    ia = jnp.concatenate([_dot(xb[:, h * hw:(h + 1) * hw], wi_ref[h]) for h in range(heads)], axis=1) + bi_ref[...]
    r, ig = _sigmoid(ra), _sigmoid(ia)
    sp = _softplus(-lam_ref[...])
    log_a = (-C_GATE_C * r) * sp
    a = jnp.exp(log_a)
    mult = jnp.sqrt(-_expm1(2.0 * log_a))
    return xb, r, ig, sp, a, mult


def c_mid_fwd(z, cw, cb, wa, ba, wi, bi, lam, name):
    m, w2 = z.shape
    width = w2 // 2
    heads, hw = wa.shape[0], wa.shape[1]
    taps = cw.shape[0]
    tm = _rows(m, 512)

    def body(zp_ref, zm_ref, cw_ref, cb_ref, wa_ref, ba_ref, wi_ref, bi_ref, lam_ref, a_ref, b_ref, xr_ref):
        i = pl.program_id(0)
        xe = jnp.concatenate([zp_ref[...] * (i > 0).astype(F32), zm_ref[...]], axis=0)
        xr = _conv_ext(xe, cw_ref[...], taps)[HALO:] + cb_ref[...]
        _, _, ig, _, a, mult = _c_gates(xr, wa_ref, ba_ref, wi_ref, bi_ref, lam_ref, heads, hw)
        a_ref[...] = a
        b_ref[...] = mult * (ig * xr)
        xr_ref[...] = xr

    vec = pl.BlockSpec((1, width), lambda i: (0, 0))
    row = pl.BlockSpec((tm, width), lambda i: (i, 0))
    wsp = pl.BlockSpec((heads, hw, hw), lambda i: (0, 0, 0))
    in_specs = [pl.BlockSpec((HALO, width), lambda i: (_prev_halo(tm)(i), 1)), pl.BlockSpec((tm, width), lambda i: (i, 1)),
                pl.BlockSpec((taps, width), lambda i: (0, 0)), vec, wsp, vec, wsp, vec, vec]
    return _call(body, name, (m // tm,), in_specs, [row, row, row], [_sds((m, width))] * 3)(
        z, z, cw, cb, wa, ba, wi, bi, lam)


_SCAN_COLS = 128
_SCAN_CHUNK = 512


def c_scan_fwd(a, b, z, name):
    m, width = a.shape
    tc = min(width, _SCAN_COLS)
    ch = _rows(m, _SCAN_CHUNK)

    def body(a_ref, b_ref, g_ref, hs_ref, y_ref):
        def step(t, h):
            h = a_ref[pl.ds(t, 1), :] * h + b_ref[pl.ds(t, 1), :]
            hs_ref[pl.ds(t, 1), :] = h
            return h

        lax.fori_loop(0, m, step, jnp.zeros((1, tc), F32), unroll=8)

        def gate(c, carry):
            rs = pl.ds(pl.multiple_of(c * ch, ch), ch)
            y_ref[rs, :] = (hs_ref[rs, :] * _gelu(g_ref[rs, :])).astype(BF16)
            return carry

        lax.fori_loop(0, m // ch, gate, 0)

    col = pl.BlockSpec((m, tc), lambda j: (0, j))
    return _call(body, name, (width // tc,), [col, col, col], [col, col], [_sds((m, width)), _sds((m, width), BF16)])(a, b, z)


def c_scan_bwd(dy, z, hs, a, name):
    m, width = a.shape
    tc = min(width, _SCAN_COLS)
    ch = _rows(m, _SCAN_CHUNK)

    def body(dy_ref, g_ref, hs_ref, a_ref, lam_ref, da_ref, dg_ref, dgs_ref):
        def pre(c, total):
            rs = pl.ds(pl.multiple_of(c * ch, ch), ch)
            gp, dyv = g_ref[rs, :], dy_ref[rs, :]
            dgate = dyv * hs_ref[rs, :] * _gelu_grad(gp)
            dg_ref[rs, :] = dgate.astype(BF16)
            lam_ref[rs, :] = dyv * _gelu(gp)
            return total + _colsum(dgate)

        dgs_ref[...] = lax.fori_loop(0, m // ch, pre, jnp.zeros((1, tc), F32))

        def step(k, carry):
            lam_next, a_next = carry
            t = m - 1 - k
            lam_t = lam_ref[pl.ds(t, 1), :] + a_next * lam_next
            lam_ref[pl.ds(t, 1), :] = lam_t
            h_prev = hs_ref[pl.ds(jnp.maximum(t - 1, 0), 1), :] * (t > 0).astype(F32)
            da_ref[pl.ds(t, 1), :] = lam_t * h_prev
            return lam_t, a_ref[pl.ds(t, 1), :]

        zero = jnp.zeros((1, tc), F32)
        lax.fori_loop(0, m, step, (zero, zero), unroll=8)

    col = pl.BlockSpec((m, tc), lambda j: (0, j))
    return _call(body, name, (width // tc,), [col, col, col, col], [col, col, col, pl.BlockSpec((1, tc), lambda j: (0, j))],
                 [_sds((m, width)), _sds((m, width)), _sds((m, width), BF16), _sds((1, width))])(dy, z, hs, a)


def c_mid_bwd(lam_seq, da, xr, wa, ba, wi, bi, lam, name):
    m, width = xr.shape
    heads, hw = wa.shape[0], wa.shape[1]
    tm = _rows(m, 512)

    def body(l_ref, da_ref, xr_ref, wa_ref, ba_ref, wi_ref, bi_ref, lam_ref,
             dxr_ref, dwa_ref, dwi_ref, dba_ref, dbi_ref, dlam_ref):
        first = pl.program_id(0) == 0
        xr_v, lmb = xr_ref[...], l_ref[...]
        xb, r, ig, sp, a, mult = _c_gates(xr_v, wa_ref, ba_ref, wi_ref, bi_ref, lam_ref, heads, hw)
        dmult = lmb * (ig * xr_v)
        dig = lmb * mult * xr_v
        dxr = lmb * mult * ig
        dla = da_ref[...] * a - dmult * (a * a) / mult
        dr = dla * (-C_GATE_C * sp)
        dsp = _colsum(dla * (-C_GATE_C * r))
        _acc(dlam_ref, dsp * (-_sigmoid(-lam_ref[...])), first)
        dra = dr * r * (1.0 - r)
        dia = dig * ig * (1.0 - ig)
        _acc(dba_ref, _colsum(dra), first)
        _acc(dbi_ref, _colsum(dia), first)
        for h in range(heads):
            cs = slice(h * hw, (h + 1) * hw)
            _acc(dwa_ref.at[h], _dot(xb[:, cs], dra[:, cs], _TN), first)
            _acc(dwi_ref.at[h], _dot(xb[:, cs], dia[:, cs], _TN), first)
            dxr_ref[:, cs] = dxr[:, cs] + _dot(dra[:, cs], wa_ref[h], _NT) + _dot(dia[:, cs], wi_ref[h], _NT)

    vec = pl.BlockSpec((1, width), lambda i: (0, 0))
    row = pl.BlockSpec((tm, width), lambda i: (i, 0))
    wsp = pl.BlockSpec((heads, hw, hw), lambda i: (0, 0, 0))
    return _call(body, name, (m // tm,), [row, row, row, wsp, vec, wsp, vec, vec], [row, wsp, wsp, vec, vec, vec],
                 [_sds((m, width)), _sds((heads, hw, hw)), _sds((heads, hw, hw)), _sds((1, width)), _sds((1, width)),
                  _sds((1, width))])(lam_seq, da, xr, wa, ba, wi, bi, lam)


def conv_bwd(dy, x_src, col_block, cw, name):
    m, width = dy.shape
    taps = cw.shape[0]
    tm = _rows(m, 512)
    n_i = m // tm

    def body(dm_ref, dn_ref, xp_ref, xm_ref, cw_ref, dx_ref, dw_ref, db_ref, dxs_ref):
        i = pl.program_id(0)
        first = i == 0
        de = jnp.concatenate([jnp.zeros((HALO, width), F32), dm_ref[...], dn_ref[...] * (i < n_i - 1).astype(F32)], axis=0)
        xe = jnp.concatenate([xp_ref[...] * (i > 0).astype(F32), xm_ref[...], jnp.zeros((HALO, width), F32)], axis=0)
        w = cw_ref[...]
        dx = de * w[taps - 1:taps]
        for s in range(1, taps):
            dx = dx + _up(de, s) * w[taps - 1 - s:taps - s]
        dx_ref[...] = dx[HALO:HALO + tm].astype(BF16)
        _acc(dxs_ref, _colsum(dx[HALO:HALO + tm]), first)
        dm = dm_ref[...]
        for s in range(taps):
            _acc(dw_ref.at[taps - 1 - s:taps - s, :], _colsum(dm * _down(xe, s)[HALO:HALO + tm]), first)
        _acc(db_ref, _colsum(dm), first)

    row = pl.BlockSpec((tm, width), lambda i: (i, 0))
    vec = pl.BlockSpec((1, width), lambda i: (0, 0))
    tsp = pl.BlockSpec((taps, width), lambda i: (0, 0))
    in_specs = [row, pl.BlockSpec((HALO, width), lambda i: (_next_halo(tm, m)(i), 0)),
                pl.BlockSpec((HALO, width), lambda i: (_prev_halo(tm)(i), col_block)),
                pl.BlockSpec((tm, width), lambda i: (i, col_block)), tsp]
    return _call(body, name, (n_i,), in_specs, [row, tsp, vec, vec],
                 [_sds((m, width), BF16), _sds((taps, width)), _sds((1, width)), _sds((1, width))])(dy, dy, x_src, x_src, cw)


def d_mid_fwd(z, cw, name):
    m, w3 = z.shape
    width = w3 // 3
    taps = cw.shape[0]
    tm = _rows(m, 512)

    def body(bm_ref, cp_ref, cm_ref, xp_ref, xm_ref, cw_ref, y_ref):
        keep = (pl.program_id(0) > 0).astype(F32)
        qe = (jnp.concatenate([cp_ref[...] * keep, cm_ref[...]], axis=0)
              * jnp.concatenate([xp_ref[...], xm_ref[...]], axis=0))
        y_ref[...] = (bm_ref[...] * _conv_ext(qe, cw_ref[...], taps)[HALO:]).astype(BF16)

    main = lambda c: pl.BlockSpec((tm, width), lambda i: (i, c))
    prev = lambda c: pl.BlockSpec((HALO, width), lambda i: (_prev_halo(tm)(i), c))
    in_specs = [main(0), prev(1), main(1), prev(2), main(2), pl.BlockSpec((taps, width), lambda i: (0, 0))]
    return _call(body, name, (m // tm,), in_specs, pl.BlockSpec((tm, width), lambda i: (i, 0)),
                 _sds((m, width), BF16))(z, z, z, z, z, cw)


def d_mid_bwd(z, dy, cw, name):
    m, w3 = z.shape
    width = w3 // 3
    taps = cw.shape[0]
    tm = _rows(m, 512)
    n_i = m // tm

    def body(bm_ref, bn_ref, cp_ref, cm_ref, cn_ref, xp_ref, xm_ref, xn_ref, dm_ref, dn_ref, cw_ref, dz_ref, dw_ref):
        i = pl.program_id(0)
        first = i == 0
        kp, kn = (i > 0).astype(F32), (i < n_i - 1).astype(F32)
        zeros = jnp.zeros((HALO, width), F32)
        ce = jnp.concatenate([cp_ref[...] * kp, cm_ref[...], cn_ref[...] * kn], axis=0)
        xe = jnp.concatenate([xp_ref[...], xm_ref[...], xn_ref[...]], axis=0)
        qe = ce * xe
        be = jnp.concatenate([zeros, bm_ref[...], bn_ref[...]], axis=0)
        dye = jnp.concatenate([zeros, dm_ref[...], dn_ref[...] * kn], axis=0)
        w = cw_ref[...]
        cq = _conv_ext(qe, w, taps)
        dcq = dye * be
        dq = dcq * w[taps - 1:taps]
        for s in range(1, taps):
            dq = dq + _up(dcq, s) * w[taps - 1 - s:taps - s]
        ms = slice(HALO, HALO + tm)
        dz_ref[:, :width] = (dye * cq)[ms].astype(BF16)
        dz_ref[:, width:2 * width] = (dq * xe)[ms].astype(BF16)
        dz_ref[:, 2 * width:] = (dq * ce)[ms].astype(BF16)
        for s in range(taps):
            _acc(dw_ref.at[taps - 1 - s:taps - s, :], _colsum(dcq[ms] * _down(qe, s)[ms]), first)

    main = lambda c: pl.BlockSpec((tm, width), lambda i: (i, c))
    prev = lambda c: pl.BlockSpec((HALO, width), lambda i: (_prev_halo(tm)(i), c))
    nxt = lambda c: pl.BlockSpec((HALO, width), lambda i: (_next_halo(tm, m)(i), c))
    tsp = pl.BlockSpec((taps, width), lambda i: (0, 0))
    in_specs = [main(0), nxt(0), prev(1), main(1), nxt(1), prev(2), main(2), nxt(2), main(0), nxt(0), tsp]
    return _call(body, name, (n_i,), in_specs, [pl.BlockSpec((tm, w3), lambda i: (i, 0)), tsp],
                 [_sds((m, w3), BF16), _sds((taps, width))])(z, z, z, z, z, z, z, z, dy, dy, cw)


def ffn_mid_fwd(z, cw, cb, name):
    _, nj, m, c = z.shape
    taps = cw.shape[2]
    tm = _rows(m, 512)

    def body(zp_ref, zm_ref, cw_ref, cb_ref, o_ref):
        keep = (pl.program_id(1) > 0).astype(F32)
        zc = []
        for s in range(2):
            xe = jnp.concatenate([zp_ref[s] * keep, zm_ref[s]], axis=0)
            zc.append(_conv_ext(xe, cw_ref[s], taps)[HALO:] + cb_ref[s])
        o_ref[...] = (zc[0] * _sigmoid(zc[0]) * zc[1]).astype(BF16)

    in_specs = [pl.BlockSpec((2, None, HALO, c), lambda j, i: (0, j, _prev_halo(tm)(i), 0)),
                pl.BlockSpec((2, None, tm, c), lambda j, i: (0, j, i, 0)),
                pl.BlockSpec((2, None, taps, c), lambda j, i: (0, j, 0, 0)),
                pl.BlockSpec((2, None, 1, c), lambda j, i: (0, j, 0, 0))]
    return _call(body, name, (nj, m // tm), in_specs, pl.BlockSpec((None, tm, c), lambda j, i: (j, i, 0)),
                 _sds((nj, m, c), BF16))(z, z, cw, cb)


def ffn_mid_bwd(z, dact, cw, cb, name):
    _, nj, m, c = z.shape
    taps = cw.shape[2]
    tm = _rows(m, 512)
    n_i = m // tm

    def body(zp_ref, zm_ref, zn_ref, dm_ref, dn_ref, cw_ref, cb_ref, dz_ref, dw_ref, db_ref):
        i = pl.program_id(1)
        first = i == 0
        kp, kn = (i > 0).astype(F32), (i < n_i - 1).astype(F32)
        xe = [jnp.concatenate([zp_ref[s] * kp, zm_ref[s], zn_ref[s] * kn], axis=0) for s in range(2)]
        zc = [_conv_ext(xe[s], cw_ref[s], taps) + cb_ref[s] for s in range(2)]
        dae = jnp.concatenate([jnp.zeros((HALO, c), F32), dm_ref[...], dn_ref[...] * kn], axis=0)
        sg = _sigmoid(zc[0])
        dzc = [dae * zc[1] * (sg * (1.0 + zc[0] * (1.0 - sg))), dae * (zc[0] * sg)]
        ms = slice(HALO, HALO + tm)
        for s in range(2):
            w = cw_ref[s]
            dx = dzc[s] * w[taps - 1:taps]
            for u in range(1, taps):
                dx = dx + _up(dzc[s], u) * w[taps - 1 - u:taps - u]
            dz_ref[s] = dx[ms].astype(BF16)
            dm = dzc[s][ms]
            for u in range(taps):
                _acc(dw_ref.at[s, taps - 1 - u:taps - u, :], _colsum(dm * _down(xe[s], u)[ms]), first)
            _acc(db_ref.at[s], _colsum(dm), first)

    in_specs = [pl.BlockSpec((2, None, HALO, c), lambda j, i: (0, j, _prev_halo(tm)(i), 0)),
                pl.BlockSpec((2, None, tm, c), lambda j, i: (0, j, i, 0)),
                pl.BlockSpec((2, None, HALO, c), lambda j, i: (0, j, _next_halo(tm, m)(i), 0)),
                pl.BlockSpec((None, tm, c), lambda j, i: (j, i, 0)),
                pl.BlockSpec((None, HALO, c), lambda j, i: (j, _next_halo(tm, m)(i), 0)),
                pl.BlockSpec((2, None, taps, c), lambda j, i: (0, j, 0, 0)),
                pl.BlockSpec((2, None, 1, c), lambda j, i: (0, j, 0, 0))]
    out_specs = [pl.BlockSpec((2, None, tm, c), lambda j, i: (0, j, i, 0)),
                 pl.BlockSpec((2, None, taps, c), lambda j, i: (0, j, 0, 0)),
                 pl.BlockSpec((2, None, 1, c), lambda j, i: (0, j, 0, 0))]
    return _call(body, name, (nj, n_i), in_specs, out_specs,
                 [_sds((2, nj, m, c), BF16), _sds((2, nj, taps, c)), _sds((2, nj, 1, c))])(z, z, z, dact, dact, cw, cb)


def _as2d(a, lead):
    shape = a.shape
    return a.reshape((lead, -1, shape[-1]) if lead else (-1, shape[-1]))


def add_pairs(q, l1, name):
    shape = q.shape
    q3, l3 = _as2d(q, shape[0]), _as2d(l1, shape[0])
    _, r, c = q3.shape
    tr = _rows(r, 512) if r % 512 == 0 else r

    def body(a_ref, b_ref, o_ref):
        o_ref[...] = a_ref[...] + b_ref[...]

    spec = pl.BlockSpec((None, tr, c), lambda k, i: (k, i, 0))
    return _call(body, name, (shape[0], r // tr), [spec, spec], spec, _sds(q3.shape))(q3, l3).reshape(shape)


def _grad_sum(p_ref, l_ref):
    return ((p_ref[...] + l_ref[0]) + l_ref[1]) + l_ref[2]


def sum_parts(p, l2, name):
    _, r, c = p.shape

    def body(p_ref, l_ref, o_ref):
        o_ref[...] = _grad_sum(p_ref, l_ref)

    return _call(body, name, (1,), [pl.BlockSpec((None, r, c), lambda i: (0, 0, 0)), pl.BlockSpec((3, r, c), lambda i: (0, 0, 0))],
                 pl.BlockSpec((r, c), lambda i: (0, 0)), _sds((r, c)))(p, l2)


def _adamw_math(w, g, m, v):
    m = ADAM_B1 * m + (1.0 - ADAM_B1) * g
    v = ADAM_B2 * v + (1.0 - ADAM_B2) * (g * g)
    m_hat = m / (1.0 - ADAM_B1 ** ADAM_STEP)
    v_hat = v / (1.0 - ADAM_B2 ** ADAM_STEP)
    delta = -ADAM_LR * (m_hat / (jnp.sqrt(v_hat) + ADAM_EPS) + ADAM_WD * w)
    return delta, m, v


def adamw(w, m, v, name, g=None, p=None, l2=None):
    shape = w.shape
    w2, m2, v2 = (_as2d(t, 0) for t in (w, m, v))
    r, c = w2.shape
    tr = _rows(r, 512) if r % 512 == 0 else r
    row = pl.BlockSpec((tr, c), lambda i: (i, 0))
    if g is None:
        p3, l3 = _as2d(p, 4), _as2d(l2, 3)
        gin = (p3, l3)
        gspecs = [pl.BlockSpec((None, tr, c), lambda i: (0, i, 0)), pl.BlockSpec((3, tr, c), lambda i: (0, i, 0))]
    else:
        gin, gspecs = (_as2d(g, 0),), [row]

    def body(*refs):
        n_g = len(gin)
        w_ref, m_ref, v_ref, g_ref, d_ref, nm_ref, nv_ref = refs[n_g:]
        grad = refs[0][...] if n_g == 1 else _grad_sum(refs[0], refs[1])
        delta, nm, nv = _adamw_math(w_ref[...], grad, m_ref[...], v_ref[...])
        g_ref[...] = grad
        d_ref[...] = delta
        nm_ref[...] = nm
        nv_ref[...] = nv

    outs = _call(body, name, (r // tr,), gspecs + [row, row, row], [row] * 4, [_sds((r, c))] * 4)(*gin, w2, m2, v2)
    return tuple(o.reshape(shape) for o in outs)


def _comm_call(body, name, ins, out_shape, n_sems):
    any_spec = pl.BlockSpec(memory_space=pl.ANY)
    return pl.pallas_call(
        body, name=name, in_specs=[any_spec] * len(ins), out_specs=[any_spec] * len(out_shape), out_shape=out_shape,
        scratch_shapes=[pltpu.SemaphoreType.DMA((n,)) for n in n_sems],
        compiler_params=pltpu.CompilerParams(has_side_effects=True))(*ins)


def _place():
    return lax.axis_index("x"), lax.axis_index("y"), lax.axis_index("c")


def _dev_index(px, py, pc):
    return 4 * px + 2 * py + pc


def all_gather(blocks, name):
    n_t = len(blocks)

    def body(*refs):
        ins, outs = refs[:n_t], refs[n_t:2 * n_t]
        send_sems, recv_sems, local_sems = refs[2 * n_t:]
        x, y, c = _place()
        me, sibling = (x, y, c), (x, y, 1 - c)
        chips = [(1 - x, y), (x, 1 - y), (1 - x, 1 - y)]

        def copy(t, k, block, to, src=None):
            dst = outs[t].at[_dev_index(*block)]
            return pltpu.make_async_remote_copy(
                src_ref=dst if src is None else src, dst_ref=dst, send_sem=send_sems.at[t * 7 + k],
                recv_sem=recv_sems.at[t * 7 + k], device_id=to, device_id_type=MESH_ID)

        mine = [pltpu.make_async_copy(ins[t], outs[t].at[_dev_index(*me)], local_sems.at[t]) for t in range(n_t)]
        for cp in mine:
            cp.start()
        first = []
        for t in range(n_t):
            first.append(copy(t, 0, me, sibling, src=ins[t]))
            first += [copy(t, 1 + j, me, (*chip, c), src=ins[t]) for j, chip in enumerate(chips)]
        for cp in first:
            cp.start()
        passed = []
        for t in range(n_t):
            for j, chip in enumerate(chips):
                copy(t, 1 + j, (*chip, c), me).wait_recv()
                cp = copy(t, 4 + j, (*chip, c), sibling)
                cp.start()
                passed.append(cp)
        for t in range(n_t):
            copy(t, 0, sibling, me).wait_recv()
            for j, chip in enumerate(chips):
                copy(t, 4 + j, (*chip, 1 - c), me).wait_recv()
        for cp in first + passed:
            cp.wait_send()
        for cp in mine:
            cp.wait()

    out_shape = [_sds((N_DEV,) + b.shape, b.dtype) for b in blocks]
    return _comm_call(body, name, blocks, out_shape, (7 * n_t, 7 * n_t, n_t))


def _chip_of(x, y, k):
    return (x if k % 2 == 0 else 1 - x), (y if k // 2 == 0 else 1 - y)


def exchange_sibling(grads, name):
    n_t = len(grads)

    def body(*refs):
        ins, kept, got = refs[:n_t], refs[n_t:2 * n_t], refs[2 * n_t:3 * n_t]
        send_sems, recv_sems, local_sems = refs[3 * n_t:]
        x, y, c = _place()
        sends, keeps = [], []
        for t in range(n_t):
            for k in range(4):
                cx, cy = _chip_of(x, y, k)
                sends.append(pltpu.make_async_remote_copy(
                    src_ref=ins[t].at[_dev_index(cx, cy, 1 - c)], dst_ref=got[t].at[k], send_sem=send_sems.at[4 * t + k],
                    recv_sem=recv_sems.at[4 * t + k], device_id=(x, y, 1 - c), device_id_type=MESH_ID))
                keeps.append(pltpu.make_async_copy(ins[t].at[_dev_index(cx, cy, c)], kept[t].at[k], local_sems.at[4 * t + k]))
        for cp in sends + keeps:
            cp.start()
        for cp in sends:
            cp.wait_recv()
        for cp in sends:
            cp.wait_send()
        for cp in keeps:
            cp.wait()

    half = [_sds((4,) + g.shape[1:], g.dtype) for g in grads]
    outs = _comm_call(body, name, grads, half + half, (4 * n_t, 4 * n_t, 4 * n_t))
    return outs[:n_t], outs[n_t:]


def exchange_chips(parts, name):
    n_t = len(parts)

    def body(*refs):
        ins, got = refs[:n_t], refs[n_t:2 * n_t]
        send_sems, recv_sems = refs[2 * n_t:]
        x, y, c = _place()
        sends = []
        for t in range(n_t):
            for k in range(1, 4):
                cx, cy = _chip_of(x, y, k)
                sends.append(pltpu.make_async_remote_copy(
                    src_ref=ins[t].at[k], dst_ref=got[t].at[k - 1], send_sem=send_sems.at[3 * t + k - 1],
                    recv_sem=recv_sems.at[3 * t + k - 1], device_id=(cx, cy, c), device_id_type=MESH_ID))
        for cp in sends:
            cp.start()
        for cp in sends:
            cp.wait_recv()
        for cp in sends:
            cp.wait_send()

    out_shape = [_sds((3,) + p.shape[1:], p.dtype) for p in parts]
    return _comm_call(body, name, parts, out_shape, (3 * n_t, 3 * n_t))


def _pack(arrays, rows):
    flat = jnp.concatenate([a.reshape(-1) for a in arrays])
    return jnp.pad(flat, (0, rows * LANES - flat.shape[0])).reshape(rows, LANES)


def _pack_stacked(arrays, rows):
    flat = jnp.concatenate([a.reshape(N_DEV, -1) for a in arrays], axis=1)
    return jnp.pad(flat, ((0, 0), (0, rows * LANES - flat.shape[1]))).reshape(N_DEV, rows, LANES)


def _unpack(buf, shapes, lead=()):
    flat = buf.reshape(lead + (-1,))
    out, off = [], 0
    for s in shapes:
        n = 1
        for d in s:
            n *= d
        out.append(flat[..., off:off + n].reshape(lead + tuple(s)))
        off += n
    return out


def _padded_rows(shapes, multiple):
    n = sum(functools.reduce(lambda a, b: a * b, s, 1) for s in shapes)
    rows = -(-n // LANES)
    return -(-rows // multiple) * multiple


def _to_full(stacked, axis):
    t = jnp.moveaxis(stacked, 0, axis)
    return t.reshape(t.shape[:axis] + (t.shape[axis] * t.shape[axis + 1],) + t.shape[axis + 2:])


def _to_stacked(full, axis):
    s = full.shape
    t = full.reshape(s[:axis] + (N_DEV, s[axis] // N_DEV) + s[axis + 1:])
    return jnp.moveaxis(t, axis, 0)


def _ffn_forward(x, norm_g, w_up, cw, cb, w_down, tag):
    h = rms_fwd(x, norm_g, f"ffn{tag}_norm")
    z = mm_in(h, w_up, f"ffn{tag}_up", stacked_out=True)
    nb, m, c = z.shape
    z4 = z.reshape(2, nb // 2, m, c)
    act = ffn_mid_fwd(z4, cw, cb, f"ffn{tag}_mid")
    out = mm_out(act, w_down, x, f"ffn{tag}_down")
    return out, (x, h, z4, act)


def _ffn_backward(dx, saved, norm_g, w_up, cw, cb, w_down, tag):
    x, h, z4, act = saved
    nj = act.shape[0]
    dact = mm_dx_out(dx, w_down, f"ffn{tag}_down_dx", groups=nj)
    dw_down = mm_dw_out(act, dx, f"ffn{tag}_down_dw")
    dz4, dcw, dcb = ffn_mid_bwd(z4, dact, cw, cb, f"ffn{tag}_mid_bwd")
    dz = dz4.reshape((2 * nj,) + dz4.shape[2:])
    dh = mm_dx_in(dz, w_up, f"ffn{tag}_up_dx")
    dw_up = mm_dw_in(h, dz, 2 * nj, f"ffn{tag}_up_dw")
    dx, dg = rms_bwd(x, norm_g, dh, dx, f"ffn{tag}_norm_bwd")
    return dx, dict(norm_g=dg, w_up=dw_up, conv_w=dcw, conv_b=dcb, w_down=dw_down)


def kernel(x, a_norm_g, a_w_in, a_b_in, a_v_norm_g, a_w_s, a_b_s, a_w_out, b_norm_g, b_w_in, b_w_grp, b_b_grp, b_scale, b_w_out, c_norm_g, c_w_in, c_b_in, c_conv_w, c_conv_b, c_w_a, c_b_a, c_w_i, c_b_i, c_lambda, c_w_out, d_norm_g, d_w_in, d_conv_w, d_w_out, ffn_norm_g, ffn_w_up, ffn_conv_w, ffn_conv_b, ffn_w_down, final_norm_g, loss_target, m_a_norm_g, m_a_w_in, m_a_b_in, m_a_v_norm_g, m_a_w_s, m_a_b_s, m_a_w_out, m_b_norm_g, m_b_w_in, m_b_w_grp, m_b_b_grp, m_b_scale, m_b_w_out, m_c_norm_g, m_c_w_in, m_c_b_in, m_c_conv_w, m_c_conv_b, m_c_w_a, m_c_b_a, m_c_w_i, m_c_b_i, m_c_lambda, m_c_w_out, m_d_norm_g, m_d_w_in, m_d_conv_w, m_d_w_out, m_ffn_norm_g, m_ffn_w_up, m_ffn_conv_w, m_ffn_conv_b, m_ffn_w_down, m_final_norm_g, v_a_norm_g, v_a_w_in, v_a_b_in, v_a_v_norm_g, v_a_w_s, v_a_b_s, v_a_w_out, v_b_norm_g, v_b_w_in, v_b_w_grp, v_b_b_grp, v_b_scale, v_b_w_out, v_c_norm_g, v_c_w_in, v_c_b_in, v_c_conv_w, v_c_conv_b, v_c_w_a, v_c_b_a, v_c_w_i, v_c_b_i, v_c_lambda, v_c_w_out, v_d_norm_g, v_d_w_in, v_d_conv_w, v_d_w_out, v_ffn_norm_g, v_ffn_w_up, v_ffn_conv_w, v_ffn_conv_b, v_ffn_w_down, v_final_norm_g):
    args = locals()
    w_loc = {n: args[n] for n in WEIGHTS}
    m_loc = {n: args["m_" + n] for n in WEIGHTS}
    v_loc = {n: args["v_" + n] for n in WEIGHTS}
    depth = ffn_w_up.shape[0]
    xs = x[0]
    target = loss_target[0]

    big = [('a_w_in', a_w_in[0]), ('a_w_out', a_w_out[0]), ('b_w_in', b_w_in[0]), ('b_w_out', b_w_out[0]),
           ('c_w_in', c_w_in[0]), ('c_w_out', c_w_out[0]), ('d_w_in', d_w_in[0]), ('d_w_out', d_w_out[0])]
    big += [(f'ffn_w_up{l}', ffn_w_up[l]) for l in range(depth)] + [(f'ffn_w_down{l}', ffn_w_down[l]) for l in range(depth)]
    small_names = list(SMALL_SHARDED)
    small_shapes = [w_loc[n].shape for n in small_names]
    small_rows = _padded_rows(small_shapes, 8)
    small_packed = _pack([w_loc[n] for n in small_names], small_rows)
    gathered = all_gather([w.astype(BF16) for _, w in big] + [small_packed], "gather_weights")
    wg = {n: g for (n, _), g in zip(big, gathered[:-1])}
    small_st = _unpack(gathered[-1], small_shapes, (N_DEV,))
    sm = {n: _to_full(s, SMALL_SHARDED[n]) for n, s in zip(small_names, small_st)}

    def rows_full(st):
        return st.reshape((st.shape[0] * st.shape[1],) + st.shape[2:])

    wa_in, wc_in, wd_in = wg['a_w_in'], wg['c_w_in'], wg['d_w_in']
    wb_in = rows_full(wg['b_w_in'])[None]
    wa_out, wb_out, wc_out, wd_out = (rows_full(wg[n]) for n in ('a_w_out', 'b_w_out', 'c_w_out', 'd_w_out'))
    w_up = [wg[f'ffn_w_up{l}'] for l in range(depth)]
    w_down = [rows_full(wg[f'ffn_w_down{l}']) for l in range(depth)]
    nb = N_DEV
    ffn_cw = [sm['ffn_conv_w'][l].reshape(ffn_conv_w.shape[1], 2, nb // 2, -1).transpose(1, 2, 0, 3) for l in range(depth)]
    ffn_cb = [ffn_conv_b[l].reshape(2, nb // 2, 1, -1) for l in range(depth)]
    ffn_g = [ffn_norm_g[l:l + 1] for l in range(depth)]
    a_bst = a_b_s[0].T
    b_wgrp, c_wa, c_wi = (sm[n][0].astype(BF16) for n in ('b_w_grp', 'c_w_a', 'c_w_i'))
    b_bgrp, c_ba, c_bi = (sm[n][0].reshape(1, -1) for n in ('b_b_grp', 'c_b_a', 'c_b_i'))

    saved = {}
    h = rms_fwd(xs, a_norm_g, "a_norm")
    z = mm_in(h, wa_in, "a_in", bias=a_b_in)
    y = a_mid_fwd(z, a_v_norm_g, a_w_s[0], a_bst, "a_mid")
    x1 = mm_out(y, wa_out, xs, "a_out")
    saved['a'] = (xs, h, z, y)
    x1, saved['f0'] = _ffn_forward(x1, ffn_g[0], w_up[0], ffn_cw[0], ffn_cb[0], w_down[0], 0)

    h = rms_fwd(x1, sm['b_norm_g'], "b_norm")
    z = mm_in(h, wb_in, "b_in")
    y = b_mid_fwd(z, b_wgrp, b_bgrp, sm['b_scale'], "b_mid")
    x2 = mm_out(y, wb_out, x1, "b_out")
    saved['b'] = (x1, h, z, y)
    x2, saved['f1'] = _ffn_forward(x2, ffn_g[1], w_up[1], ffn_cw[1], ffn_cb[1], w_down[1], 1)

    h = rms_fwd(x2, sm['c_norm_g'], "c_norm")
    z = mm_in(h, wc_in, "c_in", bias=sm['c_b_in'])
    c_cw = sm['c_conv_w'][0]
    a_seq, b_seq, xr = c_mid_fwd(z, c_cw, sm['c_conv_b'], c_wa, c_ba, c_wi, c_bi, sm['c_lambda'], "c_mid")
    hs, y = c_scan_fwd(a_seq, b_seq, z, "c_scan")
    x3 = mm_out(y, wc_out, x2, "c_out")
    saved['c'] = (x2, h, z, y, a_seq, xr, hs)
    x3, saved['f2'] = _ffn_forward(x3, ffn_g[2], w_up[2], ffn_cw[2], ffn_cb[2], w_down[2], 2)

    h = rms_fwd(x3, sm['d_norm_g'], "d_norm")
    z = mm_in(h, wd_in, "d_in")
    d_cw = sm['d_conv_w'][0]
    y = d_mid_fwd(z, d_cw, "d_mid")
    x4 = mm_out(y, wd_out, x3, "d_out")
    saved['d'] = (x3, h, z, y)
    x4, saved['f3'] = _ffn_forward(x4, ffn_g[3], w_up[3], ffn_cw[3], ffn_cb[3], w_down[3], 3)

    loss_part, dx, d_final_g = final_loss(x4, final_norm_g.reshape(1, -1), target, "final_loss")
    loss = lax.psum(loss_part[0, 0], ("x", "y", "c"))

    gf = [None] * depth
    dx, gf[3] = _ffn_backward(dx, saved['f3'], ffn_g[3], w_up[3], ffn_cw[3], ffn_cb[3], w_down[3], 3)
    xin, h, z, y = saved['d']
    dy = mm_dx_out(dx, wd_out, "d_out_dx")
    g_d_w_out = mm_dw_out(y, dx, "d_out_dw")
    dz, g_d_conv_w = d_mid_bwd(z, dy, d_cw, "d_mid_bwd")
    dh = mm_dx_in(dz, wd_in, "d_in_dx")
    g_d_w_in = mm_dw_in(h, dz, nb, "d_in_dw")
    dx, g_d_norm_g = rms_bwd(xin, sm['d_norm_g'], dh, dx, "d_norm_bwd")

    dx, gf[2] = _ffn_backward(dx, saved['f2'], ffn_g[2], w_up[2], ffn_cw[2], ffn_cb[2], w_down[2], 2)
    xin, h, z, y, a_seq, xr, hs = saved['c']
    dy = mm_dx_out(dx, wc_out, "c_out_dx")
    g_c_w_out = mm_dw_out(y, dx, "c_out_dw")
    lam_seq, da_seq, dgate, dgate_sum = c_scan_bwd(dy, z, hs, a_seq, "c_scan_bwd")
    dxr, g_c_w_a, g_c_w_i, g_c_b_a, g_c_b_i, g_c_lambda = c_mid_bwd(
        lam_seq, da_seq, xr, c_wa, c_ba, c_wi, c_bi, sm['c_lambda'], "c_mid_bwd")
    dxr_pre, g_c_conv_w, g_c_conv_b, dxr_pre_sum = conv_bwd(dxr, z, 1, c_cw, "c_conv_bwd")
    dz = jnp.concatenate([dgate, dxr_pre], axis=1)
    g_c_b_in = jnp.concatenate([dgate_sum, dxr_pre_sum], axis=1)
    dh = mm_dx_in(dz, wc_in, "c_in_dx")
    g_c_w_in = mm_dw_in(h, dz, nb, "c_in_dw")
    dx, g_c_norm_g = rms_bwd(xin, sm['c_norm_g'], dh, dx, "c_norm_bwd")

    dx, gf[1] = _ffn_backward(dx, saved['f1'], ffn_g[1], w_up[1], ffn_cw[1], ffn_cb[1], w_down[1], 1)
    xin, h, z, y = saved['b']
    dy = mm_dx_out(dx, wb_out, "b_out_dx")
    g_b_w_out = mm_dw_out(y, dx, "b_out_dw")
    dp, g_b_w_grp, g_b_b_grp, g_b_scale = b_mid_bwd(z, dy, b_wgrp, b_bgrp, sm['b_scale'], "b_mid_bwd")
    dz = b_pool_bwd(dp, "b_pool_bwd")
    dh = mm_dx_in(dz, wb_in, "b_in_dx")
    g_b_w_in = mm_dw_in(h, dz, 1, "b_in_dw")
    dx, g_b_norm_g = rms_bwd(xin, sm['b_norm_g'], dh, dx, "b_norm_bwd")

    dx, gf[0] = _ffn_backward(dx, saved['f0'], ffn_g[0], w_up[0], ffn_cw[0], ffn_cb[0], w_down[0], 0)
    xin, h, z, y = saved['a']
    dy = mm_dx_out(dx, wa_out, "a_out_dx")
    g_a_w_out = mm_dw_out(y, dx, "a_out_dw")
    dz, g_a_b_in, g_a_v_norm_g, g_a_w_s, g_a_b_s = a_mid_bwd(z, dy, a_v_norm_g, a_w_s[0], a_bst, "a_mid_bwd")
    dh = mm_dx_in(dz, wa_in, "a_in_dx")
    g_a_w_in = mm_dw_in(h, dz, nb, "a_in_dw")
    dx, g_a_norm_g = rms_bwd(xin, a_norm_g, dh, dx, "a_norm_bwd")
    grad_x = dx[None]

    def rows_stacked(full):
        return full.reshape((N_DEV, full.shape[0] // N_DEV) + full.shape[1:])

    big_grads = [g_a_w_in, rows_stacked(g_a_w_out), rows_stacked(g_b_w_in[0]), rows_stacked(g_b_w_out), g_c_w_in,
                 rows_stacked(g_c_w_out), g_d_w_in, rows_stacked(g_d_w_out)]
    big_grads += [gf[l]['w_up'] for l in range(depth)] + [rows_stacked(gf[l]['w_down']) for l in range(depth)]
    tril = jnp.tril(jnp.ones((A_CHUNK, A_CHUNK), bool))
    full_small = {
        'b_norm_g': g_b_norm_g, 'b_w_grp': g_b_w_grp[None], 'b_b_grp': g_b_b_grp.reshape(b_b_grp.shape[:2] + (-1,)),
        'b_scale': g_b_scale, 'c_norm_g': g_c_norm_g, 'c_b_in': g_c_b_in, 'c_conv_w': g_c_conv_w[None],
        'c_conv_b': g_c_conv_b, 'c_w_a': g_c_w_a[None], 'c_b_a': g_c_b_a.reshape(c_b_a.shape[:2] + (-1,)),
        'c_w_i': g_c_w_i[None], 'c_b_i': g_c_b_i.reshape(c_b_i.shape[:2] + (-1,)), 'c_lambda': g_c_lambda,
        'd_norm_g': g_d_norm_g, 'd_conv_w': g_d_conv_w[None],
        'ffn_conv_w': jnp.stack([gf[l]['conv_w'].transpose(2, 0, 1, 3).reshape(ffn_conv_w.shape[1], -1) for l in range(depth)])}
    small_grads = _pack_stacked([_to_stacked(full_small[n], SMALL_SHARDED[n]) for n in small_names], small_rows)
    repl_full = {
        'a_norm_g': g_a_norm_g, 'a_b_in': g_a_b_in, 'a_v_norm_g': g_a_v_norm_g,
        'a_w_s': jnp.where(tril, g_a_w_s, 0.0)[None], 'a_b_s': g_a_b_s[:, ::LANES].T[None],
        'ffn_norm_g': jnp.concatenate([gf[l]['norm_g'] for l in range(depth)], axis=0),
        'ffn_conv_b': jnp.stack([gf[l]['conv_b'].reshape(-1) for l in range(depth)]), 'final_norm_g': d_final_g.reshape(-1)}
    repl_shapes = [w_loc[n].shape for n in REPLICATED]
    repl_rows = _padded_rows(repl_shapes, 8 * N_DEV)
    repl_grads = _pack([repl_full[n] for n in REPLICATED], repl_rows).reshape(N_DEV, repl_rows // N_DEV, LANES)

    grads = big_grads + [small_grads, repl_grads]
    kept, got = exchange_sibling(grads, "rs_sibling")
    parts = [add_pairs(q, l, f"rs_add{t}") for t, (q, l) in enumerate(zip(kept, got))]
    others = exchange_chips(parts, "rs_chips")

    out_g, out_d, out_m, out_v = {}, {}, {}, {}
    per_layer = {}
    for t, (n, _) in enumerate(big):
        if n.startswith('ffn_w'):
            base, l = n[:-1], int(n[-1])
            res = adamw(w_loc[base][l], m_loc[base][l], v_loc[base][l], f"adamw_{n}", p=parts[t], l2=others[t])
            per_layer.setdefault(base, []).append(res)
        else:
            res = adamw(w_loc[n][0], m_loc[n][0], v_loc[n][0], f"adamw_{n}", p=parts[t], l2=others[t])
            out_g[n], out_d[n], out_m[n], out_v[n] = (r[None] for r in res)
    for base, res in per_layer.items():
        out_g[base], out_d[base], out_m[base], out_v[base] = (jnp.stack([r[i] for r in res]) for i in range(4))

    t_small = len(big)
    res = adamw(small_packed, _pack([m_loc[n] for n in small_names], small_rows),
                _pack([v_loc[n] for n in small_names], small_rows), "adamw_small", p=parts[t_small], l2=others[t_small])
    for dst, r in zip((out_g, out_d, out_m, out_v), res):
        for n, a in zip(small_names, _unpack(r, small_shapes)):
            dst[n] = a

    t_repl = t_small + 1
    chunk = sum_parts(parts[t_repl], others[t_repl], "rs_sum_repl")
    repl_g = all_gather([chunk], "gather_repl")[0].reshape(repl_rows, LANES)
    res = adamw(_pack([w_loc[n] for n in REPLICATED], repl_rows), _pack([m_loc[n] for n in REPLICATED], repl_rows),
                _pack([v_loc[n] for n in REPLICATED], repl_rows), "adamw_repl", g=repl_g)
    for dst, r in zip((out_g, out_d, out_m, out_v), res):
        for n, a in zip(REPLICATED, _unpack(r, repl_shapes)):
            dst[n] = a

    return (loss, grad_x, *[out_g[n] for n in WEIGHTS], *[out_d[n] for n in WEIGHTS], *[out_m[n] for n in WEIGHTS],
            *[out_v[n] for n in WEIGHTS])
```

```python
import functools

import jax
import jax.numpy as jnp
from jax import lax
from jax.experimental import pallas as pl
from jax.experimental.pallas import tpu as pltpu

F32, BF16 = jnp.float32, jnp.bfloat16
MESH_ID = pl.DeviceIdType.MESH
N_DEV = 8
V7X_VMEM_LIMIT_BYTES = 56 << 20
LANES = 128
HALO = 8
POOL_HALO = 16

EPS = 1e-6
A_CHUNK, A_GROUPS = 128, 4
B_WINDOWS = (2, 4, 8, 16)
C_GATE_C = 8.0
ADAM_LR, ADAM_B1, ADAM_B2, ADAM_EPS, ADAM_WD, ADAM_STEP = 0.001, 0.9, 0.999, 1e-08, 0.01, 10

WEIGHTS = ['a_norm_g', 'a_w_in', 'a_b_in', 'a_v_norm_g', 'a_w_s', 'a_b_s', 'a_w_out', 'b_norm_g', 'b_w_in', 'b_w_grp',
           'b_b_grp', 'b_scale', 'b_w_out', 'c_norm_g', 'c_w_in', 'c_b_in', 'c_conv_w', 'c_conv_b', 'c_w_a', 'c_b_a',
           'c_w_i', 'c_b_i', 'c_lambda', 'c_w_out', 'd_norm_g', 'd_w_in', 'd_conv_w', 'd_w_out', 'ffn_norm_g',
           'ffn_w_up', 'ffn_conv_w', 'ffn_conv_b', 'ffn_w_down', 'final_norm_g']
SMALL_SHARDED = {'b_norm_g': 1, 'b_w_grp': 2, 'b_b_grp': 2, 'b_scale': 1, 'c_norm_g': 1, 'c_b_in': 1, 'c_conv_w': 2,
                 'c_conv_b': 1, 'c_w_a': 2, 'c_b_a': 2, 'c_w_i': 2, 'c_b_i': 2, 'c_lambda': 1, 'd_norm_g': 1,
                 'd_conv_w': 2, 'ffn_conv_w': 2}
REPLICATED = ['a_norm_g', 'a_b_in', 'a_v_norm_g', 'a_w_s', 'a_b_s', 'ffn_norm_g', 'ffn_conv_b', 'final_norm_g']


_GELU_C0, _GELU_C1 = 0.7978845608028654, 0.044715


def _gelu(x):
    return 0.5 * x * (1.0 + jnp.tanh(_GELU_C0 * (x + _GELU_C1 * (x * x * x))))


def _gelu_grad(x):
    t = jnp.tanh(_GELU_C0 * (x + _GELU_C1 * (x * x * x)))
    return 0.5 * (1.0 + t) + 0.5 * x * (1.0 - t * t) * (_GELU_C0 * (1.0 + 3.0 * _GELU_C1 * (x * x)))


def _sigmoid(x):
    return jax.nn.sigmoid(x)


def _log1p(x):
    u = 1.0 + x
    return jnp.where(u == 1.0, x, jnp.log(u) * (x / (u - 1.0)))


def _softplus(x):
    return jnp.maximum(x, 0.0) + _log1p(jnp.exp(-jnp.abs(x)))


def _expm1(x):
    poly = x * (1.0 + x * (1 / 2) * (1.0 + x * (1 / 3) * (1.0 + x * (1 / 4) * (1.0 + x * (1 / 5) * (
        1.0 + x * (1 / 6) * (1.0 + x * (1 / 7) * (1.0 + x * (1 / 8))))))))
    return jnp.where(jnp.abs(x) < 0.35, poly, jnp.exp(x) - 1.0)


def _down(xe, s):
    return xe if s == 0 else pltpu.roll(xe, s, 0)


def _up(xe, s):
    return xe if s == 0 else pltpu.roll(xe, xe.shape[0] - s, 0)


def _conv_ext(xe, w, taps):
    y = xe * w[taps - 1:taps]
    for s in range(1, taps):
        y = y + _down(xe, s) * w[taps - 1 - s:taps - s]
    return y


def _acc(ref, val, first):
    @pl.when(first)
    def _():
        ref[...] = val

    @pl.when(jnp.logical_not(first))
    def _():
        ref[...] += val


def _colsum(v):
    return jnp.sum(v, axis=0, keepdims=True)


def _dot(a, b, dims=((1,), (0,))):
    return lax.dot_general(a.astype(BF16), b.astype(BF16), (dims, ((), ())), preferred_element_type=F32)


_NN, _NT, _TN = ((1,), (0,)), ((1,), (1,)), ((0,), (0,))


def _call(body, name, grid, in_specs, out_specs, out_shape, scratch=()):
    return pl.pallas_call(
        body, name=name, grid=grid, in_specs=in_specs, out_specs=out_specs, out_shape=out_shape,
        scratch_shapes=list(scratch),
        compiler_params=pltpu.CompilerParams(dimension_semantics=("arbitrary",) * len(grid),
                                             vmem_limit_bytes=V7X_VMEM_LIMIT_BYTES))


def _rows(m, t):
    t = min(m, t)
    assert m % t == 0, (m, t)
    return t


def _sds(shape, dtype=F32):
    return jax.ShapeDtypeStruct(tuple(shape), dtype)


def _prev_halo(tm, halo=HALO):
    return lambda i: jnp.maximum(i * (tm // halo) - 1, 0)


def _next_halo(tm, m, halo=HALO):
    return lambda i: jnp.minimum((i + 1) * (tm // halo), m // halo - 1)


def _matmul(name, a, b, out_shape, grid, a_spec, b_spec, o_spec, dims, reduce_last, extras=(), extra_specs=()):
    n_red = grid[-1] if reduce_last else 1
    n_grid = len(grid)
    scratch_acc = n_red > 1 and out_shape.dtype != F32
    scratch = [pltpu.VMEM(tuple(d for d in o_spec.block_shape if d is not None), F32)] if scratch_acc else []

    def body(a_ref, b_ref, *rest):
        if scratch_acc:
            acc_ref, rest = rest[-1], rest[:-1]
        o_ref, ex = rest[-1], rest[:-1]
        if not scratch_acc:
            acc_ref = o_ref
        p = _dot(a_ref[...], b_ref[...], dims)

        def first():
            q = p
            for e in ex:
                q = q + e[...]
            return q

        if n_red == 1:
            o_ref[...] = first().astype(o_ref.dtype)
        else:
            r = pl.program_id(n_grid - 1)

            @pl.when(r == 0)
            def _():
                acc_ref[...] = first()

            @pl.when(r > 0)
            def _():
                acc_ref[...] += p

            if scratch_acc:
                @pl.when(r == n_red - 1)
                def _():
                    o_ref[...] = acc_ref[...].astype(o_ref.dtype)

    return _call(body, name, grid, [a_spec, b_spec, *extra_specs], o_spec, out_shape, scratch)(a, b, *extras)


def mm_in(h, w_st, name, bias=None, stacked_out=False):
    m, k = h.shape
    nb, _, n = w_st.shape
    tm = _rows(m, 1024)
    a_spec = pl.BlockSpec((tm, k), lambda i, j: (i, 0))
    b_spec = pl.BlockSpec((None, k, n), lambda i, j: (j, 0, 0))
    if stacked_out:
        out, o_spec = _sds((nb, m, n)), pl.BlockSpec((None, tm, n), lambda i, j: (j, i, 0))
    else:
        out, o_spec = _sds((m, nb * n)), pl.BlockSpec((tm, n), lambda i, j: (i, j))
    extras, especs = (), ()
    if bias is not None:
        extras, especs = (bias,), (pl.BlockSpec((1, n), lambda i, j: (0, j)),)
    return _matmul(name, h, w_st, out, (m // tm, nb), a_spec, b_spec, o_spec, _NN, False, extras, especs)


def _split_rows(kf):
    g = max(1, kf // 1024)
    return g, kf // g


def mm_out(y, w, res, name):
    kf, n = w.shape
    if y.ndim == 3:
        g, m, k = y.shape
        a_spec_of = lambda tm: pl.BlockSpec((None, tm, k), lambda i, r: (r, i, 0))
    else:
        m = y.shape[0]
        g, k = _split_rows(kf)
        a_spec_of = lambda tm: pl.BlockSpec((tm, k), lambda i, r: (i, r))
    tm = _rows(m, 1024)
    b_spec = pl.BlockSpec((None, k, n), lambda i, r: (r, 0, 0))
    o_spec = pl.BlockSpec((tm, n), lambda i, r: (i, 0))
    return _matmul(name, y, w.reshape(g, k, n), _sds((m, n)), (m // tm, g), a_spec_of(tm), b_spec, o_spec, _NN, True,
                   (res,), (o_spec,))


def mm_dx_in(dz, w_st, name):
    nb, k, n = w_st.shape
    m = dz.shape[-2]
    tm = _rows(m, 1024)
    if dz.ndim == 3:
        a_spec = pl.BlockSpec((None, tm, n), lambda i, r: (r, i, 0))
    else:
        a_spec = pl.BlockSpec((tm, n), lambda i, r: (i, r))
    b_spec = pl.BlockSpec((None, k, n), lambda i, r: (r, 0, 0))
    o_spec = pl.BlockSpec((tm, k), lambda i, r: (i, 0))
    return _matmul(name, dz, w_st, _sds((m, k)), (m // tm, nb), a_spec, b_spec, o_spec, _NT, True)


def mm_dx_out(dout, w, name, groups=None):
    kf, n = w.shape
    m = dout.shape[0]
    tm = _rows(m, 1024)
    g, k = (groups, kf // groups) if groups else _split_rows(kf)
    a_spec = pl.BlockSpec((tm, n), lambda i, j: (i, 0))
    b_spec = pl.BlockSpec((None, k, n), lambda i, j: (j, 0, 0))
    if groups:
        out, o_spec = _sds((g, m, k)), pl.BlockSpec((None, tm, k), lambda i, j: (j, i, 0))
    else:
        out, o_spec = _sds((m, kf)), pl.BlockSpec((tm, k), lambda i, j: (i, j))
    return _matmul(name, dout, w.reshape(g, k, n), out, (m // tm, g), a_spec, b_spec, o_spec, _NT, False)


def mm_dw_in(h, dz, nb, name):
    m, k = h.shape
    tm = _rows(m, 1024)
    a_spec = pl.BlockSpec((tm, k), lambda j, r: (r, 0))
    if dz.ndim == 3:
        n = dz.shape[2]
        b_spec = pl.BlockSpec((None, tm, n), lambda j, r: (j, r, 0))
    else:
        n = dz.shape[1] // nb
        b_spec = pl.BlockSpec((tm, n), lambda j, r: (r, j))
    o_spec = pl.BlockSpec((None, k, n), lambda j, r: (j, 0, 0))
    return _matmul(name, h, dz, _sds((nb, k, n), BF16), (nb, m // tm), a_spec, b_spec, o_spec, _TN, True)


def mm_dw_out(y, dout, name):
    m, n = dout.shape
    tm = _rows(m, 1024)
    if y.ndim == 3:
        g, _, k = y.shape
        a_spec = pl.BlockSpec((None, tm, k), lambda j, r: (j, r, 0))
    else:
        g, k = _split_rows(y.shape[1])
        a_spec = pl.BlockSpec((tm, k), lambda j, r: (r, j))
    b_spec = pl.BlockSpec((tm, n), lambda j, r: (r, 0))
    o_spec = pl.BlockSpec((None, k, n), lambda j, r: (j, 0, 0))
    out = _matmul(name, y, dout, _sds((g, k, n), BF16), (g, m // tm), a_spec, b_spec, o_spec, _TN, True)
    return out.reshape(g * k, n)


def rms_fwd(x, g, name):
    m, d = x.shape
    tm = _rows(m, 512)

    def body(x_ref, g_ref, o_ref):
        xv = x_ref[...]
        rstd = lax.rsqrt(jnp.mean(xv * xv, axis=-1, keepdims=True) + EPS)
        o_ref[...] = (xv * rstd * g_ref[...]).astype(BF16)

    row = pl.BlockSpec((tm, d), lambda i: (i, 0))
    vec = pl.BlockSpec((1, d), lambda i: (0, 0))
    return _call(body, name, (m // tm,), [row, vec], row, _sds((m, d), BF16))(x, g)


def _rms_bwd_math(xv, g, dh):
    rstd = lax.rsqrt(jnp.mean(xv * xv, axis=-1, keepdims=True) + EPS)
    xhat = xv * rstd
    dxhat = dh * g
    dx = rstd * (dxhat - xhat * jnp.mean(dxhat * xhat, axis=-1, keepdims=True))
    return dx, _colsum(dh * xhat)


def rms_bwd(x, g, dh, dres, name):
    m, d = x.shape
    tm = _rows(m, 512)

    def body(x_ref, g_ref, dh_ref, dr_ref, dx_ref, dg_ref):
        dx, dg = _rms_bwd_math(x_ref[...], g_ref[...], dh_ref[...])
        dx_ref[...] = dr_ref[...] + dx
        _acc(dg_ref, dg, pl.program_id(0) == 0)

    row = pl.BlockSpec((tm, d), lambda i: (i, 0))
    vec = pl.BlockSpec((1, d), lambda i: (0, 0))
    return _call(body, name, (m // tm,), [row, vec, row, row], [row, vec], [_sds((m, d)), _sds((1, d))])(x, g, dh, dres)


def final_loss(x, g, target, name):
    m, d = x.shape
    tm = _rows(m, 512)

    def body(x_ref, g_ref, t_ref, l_ref, dx_ref, dg_ref):
        xv, gv = x_ref[...], g_ref[...]
        rstd = lax.rsqrt(jnp.mean(xv * xv, axis=-1, keepdims=True) + EPS)
        err = xv * rstd * gv - t_ref[...]
        part = 0.5 * jnp.sum(jnp.mean(err * err, axis=-1, keepdims=True), axis=0, keepdims=True)
        dx, dg = _rms_bwd_math(xv, gv, err * (1.0 / d))
        dx_ref[...] = dx
        first = pl.program_id(0) == 0
        _acc(l_ref, jnp.broadcast_to(part, l_ref.shape), first)
        _acc(dg_ref, dg, first)

    row = pl.BlockSpec((tm, d), lambda i: (i, 0))
    vec = pl.BlockSpec((1, d), lambda i: (0, 0))
    lsp = pl.BlockSpec((1, LANES), lambda i: (0, 0))
    return _call(body, name, (m // tm,), [row, vec, row], [lsp, row, vec],
                 [_sds((1, LANES)), _sds((m, d)), _sds((1, d))])(x, g, target)


def _a_common(z_ref, vg_ref, ws_ref, bst_ref, tm, width):
    gw = width // A_GROUPS
    zp = z_ref[...]
    z = _gelu(zp)
    u, v = z[:, :width], z[:, width:]
    rstd = lax.rsqrt(jnp.mean(v * v, axis=-1, keepdims=True) + EPS)
    vhat = v * rstd
    vn = vhat * vg_ref[...]
    t_i = lax.broadcasted_iota(jnp.int32, (A_CHUNK, A_CHUNK), 0)
    s_i = lax.broadcasted_iota(jnp.int32, (A_CHUNK, A_CHUNK), 1)
    wsm = [jnp.where(s_i <= t_i, ws_ref[g], 0.0).astype(BF16) for g in range(A_GROUPS)]
    bst = bst_ref[...]
    return zp, u, rstd, vhat, vn.astype(BF16), wsm, bst, gw


def a_mid_fwd(z, vg, ws, bst, name):
    m, w2 = z.shape
    width = w2 // 2
    tm = _rows(m, 256)

    def body(z_ref, vg_ref, ws_ref, bst_ref, y_ref):
        _, u, _, _, vnb, wsm, bst, gw = _a_common(z_ref, vg_ref, ws_ref, bst_ref, tm, width)
        for c in range(tm // A_CHUNK):
            r0 = c * A_CHUNK
            for g in range(A_GROUPS):
                c0 = g * gw
                vs = _dot(wsm[g], vnb[r0:r0 + A_CHUNK, c0:c0 + gw]) + bst[:, g:g + 1]
                y_ref[r0:r0 + A_CHUNK, c0:c0 + gw] = (u[r0:r0 + A_CHUNK, c0:c0 + gw] * vs).astype(BF16)

    in_specs = [pl.BlockSpec((tm, w2), lambda i: (i, 0)), pl.BlockSpec((1, width), lambda i: (0, 0)),
                pl.BlockSpec((A_GROUPS, A_CHUNK, A_CHUNK), lambda i: (0, 0, 0)),
                pl.BlockSpec((A_CHUNK, A_GROUPS), lambda i: (0, 0))]
    return _call(body, name, (m // tm,), in_specs, pl.BlockSpec((tm, width), lambda i: (i, 0)),
                 _sds((m, width), BF16))(z, vg, ws, bst)


def a_mid_bwd(z, dy, vg, ws, bst, name):
    m, w2 = z.shape
    width = w2 // 2
    tm = _rows(m, 256)

    def body(z_ref, dy_ref, vg_ref, ws_ref, bst_ref, dz_ref, dbin_ref, dvg_ref, dws_ref, dbs_ref, dvn_scr, du_scr):
        first = pl.program_id(0) == 0
        zp, u, rstd, vhat, vnb, wsm, bst, gw = _a_common(z_ref, vg_ref, ws_ref, bst_ref, tm, width)
        dy = dy_ref[...]
        dws = [jnp.zeros((A_CHUNK, A_CHUNK), F32) for _ in range(A_GROUPS)]
        dbs = [jnp.zeros((A_CHUNK, 1), F32) for _ in range(A_GROUPS)]
        for c in range(tm // A_CHUNK):
            r0 = c * A_CHUNK
            for g in range(A_GROUPS):
                c0 = g * gw
                vn_cg = vnb[r0:r0 + A_CHUNK, c0:c0 + gw]
                vs = _dot(wsm[g], vn_cg) + bst[:, g:g + 1]
                dy_cg = dy[r0:r0 + A_CHUNK, c0:c0 + gw]
                dvs = dy_cg * u[r0:r0 + A_CHUNK, c0:c0 + gw]
                du_scr[r0:r0 + A_CHUNK, c0:c0 + gw] = dy_cg * vs
                dws[g] = dws[g] + _dot(dvs, vn_cg, _NT)
                dbs[g] = dbs[g] + jnp.sum(dvs, axis=1, keepdims=True)
                dvn_scr[r0:r0 + A_CHUNK, c0:c0 + gw] = _dot(wsm[g], dvs, _TN)
        for g in range(A_GROUPS):
            _acc(dws_ref.at[g], dws[g], first)
            _acc(dbs_ref.at[:, g * LANES:(g + 1) * LANES], jnp.broadcast_to(dbs[g], (A_CHUNK, LANES)), first)
        dvn = dvn_scr[...]
        _acc(dvg_ref, _colsum(dvn * vhat), first)
        dvhat = dvn * vg_ref[...]
        dv = rstd * (dvhat - vhat * jnp.mean(dvhat * vhat, axis=-1, keepdims=True))
        gg = _gelu_grad(zp)
        dzu = du_scr[...] * gg[:, :width]
        dzv = dv * gg[:, width:]
        dz_ref[:, :width] = dzu.astype(BF16)
        dz_ref[:, width:] = dzv.astype(BF16)
        _acc(dbin_ref.at[:, :width], _colsum(dzu), first)
        _acc(dbin_ref.at[:, width:], _colsum(dzv), first)

    const2 = lambda i: (0, 0)
    in_specs = [pl.BlockSpec((tm, w2), lambda i: (i, 0)), pl.BlockSpec((tm, width), lambda i: (i, 0)),
                pl.BlockSpec((1, width), const2), pl.BlockSpec((A_GROUPS, A_CHUNK, A_CHUNK), lambda i: (0, 0, 0)),
                pl.BlockSpec((A_CHUNK, A_GROUPS), const2)]
    out_specs = [pl.BlockSpec((tm, w2), lambda i: (i, 0)), pl.BlockSpec((1, w2), const2), pl.BlockSpec((1, width), const2),
                 pl.BlockSpec((A_GROUPS, A_CHUNK, A_CHUNK), lambda i: (0, 0, 0)),
                 pl.BlockSpec((A_CHUNK, A_GROUPS * LANES), const2)]
    out_shape = [_sds((m, w2), BF16), _sds((1, w2)), _sds((1, width)), _sds((A_GROUPS, A_CHUNK, A_CHUNK)),
                 _sds((A_CHUNK, A_GROUPS * LANES))]
    scratch = [pltpu.VMEM((tm, width), F32), pltpu.VMEM((tm, width), F32)]
    return _call(body, name, (m // tm,), in_specs, out_specs, out_shape, scratch)(z, dy, vg, ws, bst)


def _pool_minus_id(ze, i, tm, gw):
    pos = i * tm + lax.broadcasted_iota(jnp.int32, (tm, 1), 0)
    out = []
    for gi, win in enumerate(B_WINDOWS):
        s = ze[:, gi * gw:(gi + 1) * gw]
        step = 1
        while step < win:
            s = s + _down(s, step)
            step *= 2
        inv = 1.0 / jnp.minimum(pos + 1, win).astype(F32)
        out.append(s[POOL_HALO:] * inv - ze[POOL_HALO:, gi * gw:(gi + 1) * gw])
    return out


def _b_specs(tm, width):
    return [pl.BlockSpec((POOL_HALO, width), lambda i: (_prev_halo(tm, POOL_HALO)(i), 0)),
            pl.BlockSpec((tm, width), lambda i: (i, 0))]


def b_mid_fwd(z, wgrp, bgrp, scale, name):
    m, width = z.shape
    ng = len(B_WINDOWS)
    gw = width // ng
    tm = _rows(m, 512)

    def body(zp_ref, zm_ref, w_ref, b_ref, s_ref, y_ref):
        i = pl.program_id(0)
        ze = jnp.concatenate([zp_ref[...] * (i > 0).astype(F32), zm_ref[...]], axis=0)
        p = _pool_minus_id(ze, i, tm, gw)
        for g in range(ng):
            cs = slice(g * gw, (g + 1) * gw)
            y = (_dot(p[g], w_ref[g]) + b_ref[:, cs]) * s_ref[:, cs]
            y_ref[:, cs] = y.astype(BF16)

    vec = pl.BlockSpec((1, width), lambda i: (0, 0))
    in_specs = _b_specs(tm, width) + [pl.BlockSpec((ng, gw, gw), lambda i: (0, 0, 0)), vec, vec]
    return _call(body, name, (m // tm,), in_specs, pl.BlockSpec((tm, width), lambda i: (i, 0)),
                 _sds((m, width), BF16))(z, z, wgrp, bgrp, scale)


def b_mid_bwd(z, dy, wgrp, bgrp, scale, name):
    m, width = z.shape
    ng = len(B_WINDOWS)
    gw = width // ng
    tm = _rows(m, 512)

    def body(zp_ref, zm_ref, dy_ref, w_ref, b_ref, s_ref, dp_ref, dw_ref, db_ref, ds_ref):
        i = pl.program_id(0)
        first = i == 0
        ze = jnp.concatenate([zp_ref[...] * (i > 0).astype(F32), zm_ref[...]], axis=0)
        p = _pool_minus_id(ze, i, tm, gw)
        for g in range(ng):
            cs = slice(g * gw, (g + 1) * gw)
            dyg = dy_ref[:, cs]
            ypre = _dot(p[g], w_ref[g]) + b_ref[:, cs]
            dyp = dyg * s_ref[:, cs]
            _acc(ds_ref.at[:, cs], _colsum(dyg * ypre), first)
            _acc(db_ref.at[:, cs], _colsum(dyp), first)
            _acc(dw_ref.at[g], _dot(p[g], dyp, _TN), first)
            dp_ref[:, cs] = _dot(dyp, w_ref[g], _NT)

    vec = pl.BlockSpec((1, width), lambda i: (0, 0))
    row = pl.BlockSpec((tm, width), lambda i: (i, 0))
    wsp = pl.BlockSpec((ng, gw, gw), lambda i: (0, 0, 0))
    return _call(body, name, (m // tm,), _b_specs(tm, width) + [row, wsp, vec, vec], [row, wsp, vec, vec],
                 [_sds((m, width)), _sds((ng, gw, gw)), _sds((1, width)), _sds((1, width))])(z, z, dy, wgrp, bgrp, scale)


def b_pool_bwd(dp, name):
    m, width = dp.shape
    gw = width // len(B_WINDOWS)
    tm = _rows(m, 512)
    n_i = m // tm

    def body(dm_ref, dn_ref, dz_ref):
        i = pl.program_id(0)
        de = jnp.concatenate([dm_ref[...], dn_ref[...] * (i < n_i - 1).astype(F32)], axis=0)
        pos = i * tm + lax.broadcasted_iota(jnp.int32, (tm + POOL_HALO, 1), 0)
        for gi, win in enumerate(B_WINDOWS):
            cs = slice(gi * gw, (gi + 1) * gw)
            d = de[:, cs]
            s = d * (1.0 / jnp.minimum(pos + 1, win).astype(F32))
            step = 1
            while step < win:
                s = s + _up(s, step)
                step *= 2
            dz_ref[:, cs] = (s[:tm] - d[:tm]).astype(BF16)

    in_specs = [pl.BlockSpec((tm, width), lambda i: (i, 0)),
                pl.BlockSpec((POOL_HALO, width), lambda i: (_next_halo(tm, m, POOL_HALO)(i), 0))]
    return _call(body, name, (n_i,), in_specs, pl.BlockSpec((tm, width), lambda i: (i, 0)), _sds((m, width), BF16))(dp, dp)


def _c_gates(xr, wa_ref, ba_ref, wi_ref, bi_ref, lam_ref, heads, hw):
    xb = xr.astype(BF16)
    ra = jnp.concatenate([_dot(xb[:, h * hw:(h + 1) * hw], wa_ref[h]) for h in range(heads)], axis=1) + ba_ref[...]
    ia = jnp.concatenate([_dot(xb[:, h * hw:(h + 1) * hw], wi_ref[h]) for h in range(heads)], axis=1) + bi_ref[...]
    r, ig = _sigmoid(ra), _sigmoid(ia)
    sp = _softplus(-lam_ref[...])
    log_a = (-C_GATE_C * r) * sp
    a = jnp.exp(log_a)
    mult = jnp.sqrt(-_expm1(2.0 * log_a))
    return xb, r, ig, sp, a, mult


def c_mid_fwd(z, cw, cb, wa, ba, wi, bi, lam, name):
    m, w2 = z.shape
    width = w2 // 2
    heads, hw = wa.shape[0], wa.shape[1]
    taps = cw.shape[0]
    tm = _rows(m, 512)

    def body(zp_ref, zm_ref, cw_ref, cb_ref, wa_ref, ba_ref, wi_ref, bi_ref, lam_ref, a_ref, b_ref, xr_ref):
        i = pl.program_id(0)
        xe = jnp.concatenate([zp_ref[...] * (i > 0).astype(F32), zm_ref[...]], axis=0)
        xr = _conv_ext(xe, cw_ref[...], taps)[HALO:] + cb_ref[...]
        _, _, ig, _, a, mult = _c_gates(xr, wa_ref, ba_ref, wi_ref, bi_ref, lam_ref, heads, hw)
        a_ref[...] = a
        b_ref[...] = mult * (ig * xr)
        xr_ref[...] = xr

    vec = pl.BlockSpec((1, width), lambda i: (0, 0))
    row = pl.BlockSpec((tm, width), lambda i: (i, 0))
    wsp = pl.BlockSpec((heads, hw, hw), lambda i: (0, 0, 0))
    in_specs = [pl.BlockSpec((HALO, width), lambda i: (_prev_halo(tm)(i), 1)), pl.BlockSpec((tm, width), lambda i: (i, 1)),
                pl.BlockSpec((taps, width), lambda i: (0, 0)), vec, wsp, vec, wsp, vec, vec]
    return _call(body, name, (m // tm,), in_specs, [row, row, row], [_sds((m, width))] * 3)(
        z, z, cw, cb, wa, ba, wi, bi, lam)


_SCAN_ROWS = 512


def c_scan_fwd(a, b, z, name):
    m, width = a.shape
    tm = _rows(m, _SCAN_ROWS)

    def body(a_ref, b_ref, g_ref, hs_ref, y_ref, h_carry):
        @pl.when(pl.program_id(0) == 0)
        def _():
            h_carry[...] = jnp.zeros_like(h_carry)

        def step(t, h):
            h = a_ref[pl.ds(t, 1), :] * h + b_ref[pl.ds(t, 1), :]
            hs_ref[pl.ds(t, 1), :] = h
            return h

        h_carry[...] = lax.fori_loop(0, tm, step, h_carry[...], unroll=8)
        y_ref[...] = (hs_ref[...] * _gelu(g_ref[...])).astype(BF16)

    row = pl.BlockSpec((tm, width), lambda i: (i, 0))
    return _call(body, name, (m // tm,), [row, row, row], [row, row], [_sds((m, width)), _sds((m, width), BF16)],
                 [pltpu.VMEM((1, width), F32)])(a, b, z)


def c_scan_bwd(dy, z, hs, a, name):
    m, width = a.shape
    tm = _rows(m, _SCAN_ROWS)
    n_i = m // tm

    def body(dy_ref, g_ref, hs_ref, hp_ref, a_ref, lam_ref, da_ref, dg_ref, dgs_ref, lam_carry, a_carry):
        i = pl.program_id(0)
        first = i == 0

        @pl.when(first)
        def _():
            lam_carry[...] = jnp.zeros_like(lam_carry)
            a_carry[...] = jnp.zeros_like(a_carry)

        gp, dyv, hsv = g_ref[...], dy_ref[...], hs_ref[...]
        dgate = dyv * hsv * _gelu_grad(gp)
        dg_ref[...] = dgate.astype(BF16)
        _acc(dgs_ref, _colsum(dgate), first)
        lam_ref[...] = dyv * _gelu(gp)

        def step(k, carry):
            lam_next, a_next = carry
            t = tm - 1 - k
            lam_t = lam_ref[pl.ds(t, 1), :] + a_next * lam_next
            lam_ref[pl.ds(t, 1), :] = lam_t
            return lam_t, a_ref[pl.ds(t, 1), :]

        lam_c, a_c = lax.fori_loop(0, tm, step, (lam_carry[...], a_carry[...]), unroll=8)
        lam_carry[...] = lam_c
        a_carry[...] = a_c
        h_before = hp_ref[HALO - 1:HALO, :] * (i < n_i - 1).astype(F32)
        t_i = lax.broadcasted_iota(jnp.int32, (tm, 1), 0)
        da_ref[...] = lam_ref[...] * jnp.where(t_i == 0, h_before, _down(hsv, 1))

    row = pl.BlockSpec((tm, width), lambda i: (n_i - 1 - i, 0))
    halo = pl.BlockSpec((HALO, width), lambda i: (_prev_halo(tm)(n_i - 1 - i), 0))
    vec = pl.BlockSpec((1, width), lambda i: (0, 0))
    return _call(body, name, (n_i,), [row, row, row, halo, row], [row, row, row, vec],
                 [_sds((m, width)), _sds((m, width)), _sds((m, width), BF16), _sds((1, width))],
                 [pltpu.VMEM((1, width), F32), pltpu.VMEM((1, width), F32)])(dy, z, hs, hs, a)


def c_mid_bwd(lam_seq, da, xr, wa, ba, wi, bi, lam, name):
    m, width = xr.shape
    heads, hw = wa.shape[0], wa.shape[1]
    tm = _rows(m, 512)

    def body(l_ref, da_ref, xr_ref, wa_ref, ba_ref, wi_ref, bi_ref, lam_ref,
             dxr_ref, dwa_ref, dwi_ref, dba_ref, dbi_ref, dlam_ref):
        first = pl.program_id(0) == 0
        xr_v, lmb = xr_ref[...], l_ref[...]
        xb, r, ig, sp, a, mult = _c_gates(xr_v, wa_ref, ba_ref, wi_ref, bi_ref, lam_ref, heads, hw)
        dmult = lmb * (ig * xr_v)
        dig = lmb * mult * xr_v
        dxr = lmb * mult * ig
        dla = da_ref[...] * a - dmult * (a * a) / mult
        dr = dla * (-C_GATE_C * sp)
        dsp = _colsum(dla * (-C_GATE_C * r))
        _acc(dlam_ref, dsp * (-_sigmoid(-lam_ref[...])), first)
        dra = dr * r * (1.0 - r)
        dia = dig * ig * (1.0 - ig)
        _acc(dba_ref, _colsum(dra), first)
        _acc(dbi_ref, _colsum(dia), first)
        for h in range(heads):
            cs = slice(h * hw, (h + 1) * hw)
            _acc(dwa_ref.at[h], _dot(xb[:, cs], dra[:, cs], _TN), first)
            _acc(dwi_ref.at[h], _dot(xb[:, cs], dia[:, cs], _TN), first)
            dxr_ref[:, cs] = dxr[:, cs] + _dot(dra[:, cs], wa_ref[h], _NT) + _dot(dia[:, cs], wi_ref[h], _NT)

    vec = pl.BlockSpec((1, width), lambda i: (0, 0))
    row = pl.BlockSpec((tm, width), lambda i: (i, 0))
    wsp = pl.BlockSpec((heads, hw, hw), lambda i: (0, 0, 0))
    return _call(body, name, (m // tm,), [row, row, row, wsp, vec, wsp, vec, vec], [row, wsp, wsp, vec, vec, vec],
                 [_sds((m, width)), _sds((heads, hw, hw)), _sds((heads, hw, hw)), _sds((1, width)), _sds((1, width)),
                  _sds((1, width))])(lam_seq, da, xr, wa, ba, wi, bi, lam)


def conv_bwd(dy, x_src, col_block, cw, name):
    m, width = dy.shape
    taps = cw.shape[0]
    tm = _rows(m, 512)
    n_i = m // tm

    def body(dm_ref, dn_ref, xp_ref, xm_ref, cw_ref, dx_ref, dw_ref, db_ref, dxs_ref):
        i = pl.program_id(0)
        first = i == 0
        de = jnp.concatenate([jnp.zeros((HALO, width), F32), dm_ref[...], dn_ref[...] * (i < n_i - 1).astype(F32)], axis=0)
        xe = jnp.concatenate([xp_ref[...] * (i > 0).astype(F32), xm_ref[...], jnp.zeros((HALO, width), F32)], axis=0)
        w = cw_ref[...]
        dx = de * w[taps - 1:taps]
        for s in range(1, taps):
            dx = dx + _up(de, s) * w[taps - 1 - s:taps - s]
        dx_ref[...] = dx[HALO:HALO + tm].astype(BF16)
        _acc(dxs_ref, _colsum(dx[HALO:HALO + tm]), first)
        dm = dm_ref[...]
        for s in range(taps):
            _acc(dw_ref.at[taps - 1 - s:taps - s, :], _colsum(dm * _down(xe, s)[HALO:HALO + tm]), first)
        _acc(db_ref, _colsum(dm), first)

    row = pl.BlockSpec((tm, width), lambda i: (i, 0))
    vec = pl.BlockSpec((1, width), lambda i: (0, 0))
    tsp = pl.BlockSpec((taps, width), lambda i: (0, 0))
    in_specs = [row, pl.BlockSpec((HALO, width), lambda i: (_next_halo(tm, m)(i), 0)),
                pl.BlockSpec((HALO, width), lambda i: (_prev_halo(tm)(i), col_block)),
                pl.BlockSpec((tm, width), lambda i: (i, col_block)), tsp]
    return _call(body, name, (n_i,), in_specs, [row, tsp, vec, vec],
                 [_sds((m, width), BF16), _sds((taps, width)), _sds((1, width)), _sds((1, width))])(dy, dy, x_src, x_src, cw)


def d_mid_fwd(z, cw, name):
    m, w3 = z.shape
    width = w3 // 3
    taps = cw.shape[0]
    tm = _rows(m, 512)

    def body(bm_ref, cp_ref, cm_ref, xp_ref, xm_ref, cw_ref, y_ref):
        keep = (pl.program_id(0) > 0).astype(F32)
        qe = (jnp.concatenate([cp_ref[...] * keep, cm_ref[...]], axis=0)
              * jnp.concatenate([xp_ref[...], xm_ref[...]], axis=0))
        y_ref[...] = (bm_ref[...] * _conv_ext(qe, cw_ref[...], taps)[HALO:]).astype(BF16)

    main = lambda c: pl.BlockSpec((tm, width), lambda i: (i, c))
    prev = lambda c: pl.BlockSpec((HALO, width), lambda i: (_prev_halo(tm)(i), c))
    in_specs = [main(0), prev(1), main(1), prev(2), main(2), pl.BlockSpec((taps, width), lambda i: (0, 0))]
    return _call(body, name, (m // tm,), in_specs, pl.BlockSpec((tm, width), lambda i: (i, 0)),
                 _sds((m, width), BF16))(z, z, z, z, z, cw)


def d_mid_bwd(z, dy, cw, name):
    m, w3 = z.shape
    width = w3 // 3
    taps = cw.shape[0]
    tm = _rows(m, 512)
    n_i = m // tm

    def body(bm_ref, bn_ref, cp_ref, cm_ref, cn_ref, xp_ref, xm_ref, xn_ref, dm_ref, dn_ref, cw_ref, dz_ref, dw_ref):
        i = pl.program_id(0)
        first = i == 0
        kp, kn = (i > 0).astype(F32), (i < n_i - 1).astype(F32)
        zeros = jnp.zeros((HALO, width), F32)
        ce = jnp.concatenate([cp_ref[...] * kp, cm_ref[...], cn_ref[...] * kn], axis=0)
        xe = jnp.concatenate([xp_ref[...], xm_ref[...], xn_ref[...]], axis=0)
        qe = ce * xe
        be = jnp.concatenate([zeros, bm_ref[...], bn_ref[...]], axis=0)
        dye = jnp.concatenate([zeros, dm_ref[...], dn_ref[...] * kn], axis=0)
        w = cw_ref[...]
        cq = _conv_ext(qe, w, taps)
        dcq = dye * be
        dq = dcq * w[taps - 1:taps]
        for s in range(1, taps):
            dq = dq + _up(dcq, s) * w[taps - 1 - s:taps - s]
        ms = slice(HALO, HALO + tm)
        dz_ref[:, :width] = (dye * cq)[ms].astype(BF16)
        dz_ref[:, width:2 * width] = (dq * xe)[ms].astype(BF16)
        dz_ref[:, 2 * width:] = (dq * ce)[ms].astype(BF16)
        for s in range(taps):
            _acc(dw_ref.at[taps - 1 - s:taps - s, :], _colsum(dcq[ms] * _down(qe, s)[ms]), first)

    main = lambda c: pl.BlockSpec((tm, width), lambda i: (i, c))
    prev = lambda c: pl.BlockSpec((HALO, width), lambda i: (_prev_halo(tm)(i), c))
    nxt = lambda c: pl.BlockSpec((HALO, width), lambda i: (_next_halo(tm, m)(i), c))
    tsp = pl.BlockSpec((taps, width), lambda i: (0, 0))
    in_specs = [main(0), nxt(0), prev(1), main(1), nxt(1), prev(2), main(2), nxt(2), main(0), nxt(0), tsp]
    return _call(body, name, (n_i,), in_specs, [pl.BlockSpec((tm, w3), lambda i: (i, 0)), tsp],
                 [_sds((m, w3), BF16), _sds((taps, width))])(z, z, z, z, z, z, z, z, dy, dy, cw)


def ffn_mid_fwd(z, cw, cb, name):
    _, nj, m, c = z.shape
    taps = cw.shape[2]
    tm = _rows(m, 512)

    def body(zp_ref, zm_ref, cw_ref, cb_ref, o_ref):
        keep = (pl.program_id(1) > 0).astype(F32)
        zc = []
        for s in range(2):
            xe = jnp.concatenate([zp_ref[s] * keep, zm_ref[s]], axis=0)
            zc.append(_conv_ext(xe, cw_ref[s], taps)[HALO:] + cb_ref[s])
        o_ref[...] = (zc[0] * _sigmoid(zc[0]) * zc[1]).astype(BF16)

    in_specs = [pl.BlockSpec((2, None, HALO, c), lambda j, i: (0, j, _prev_halo(tm)(i), 0)),
                pl.BlockSpec((2, None, tm, c), lambda j, i: (0, j, i, 0)),
                pl.BlockSpec((2, None, taps, c), lambda j, i: (0, j, 0, 0)),
                pl.BlockSpec((2, None, 1, c), lambda j, i: (0, j, 0, 0))]
    return _call(body, name, (nj, m // tm), in_specs, pl.BlockSpec((None, tm, c), lambda j, i: (j, i, 0)),
                 _sds((nj, m, c), BF16))(z, z, cw, cb)


def ffn_mid_bwd(z, dact, cw, cb, name):
    _, nj, m, c = z.shape
    taps = cw.shape[2]
    tm = _rows(m, 512)
    n_i = m // tm

    def body(zp_ref, zm_ref, zn_ref, dm_ref, dn_ref, cw_ref, cb_ref, dz_ref, dw_ref, db_ref):
        i = pl.program_id(1)
        first = i == 0
        kp, kn = (i > 0).astype(F32), (i < n_i - 1).astype(F32)
        xe = [jnp.concatenate([zp_ref[s] * kp, zm_ref[s], zn_ref[s] * kn], axis=0) for s in range(2)]
        zc = [_conv_ext(xe[s], cw_ref[s], taps) + cb_ref[s] for s in range(2)]
        dae = jnp.concatenate([jnp.zeros((HALO, c), F32), dm_ref[...], dn_ref[...] * kn], axis=0)
        sg = _sigmoid(zc[0])
        dzc = [dae * zc[1] * (sg * (1.0 + zc[0] * (1.0 - sg))), dae * (zc[0] * sg)]
        ms = slice(HALO, HALO + tm)
        for s in range(2):
            w = cw_ref[s]
            dx = dzc[s] * w[taps - 1:taps]
            for u in range(1, taps):
                dx = dx + _up(dzc[s], u) * w[taps - 1 - u:taps - u]
            dz_ref[s] = dx[ms].astype(BF16)
            dm = dzc[s][ms]
            for u in range(taps):
                _acc(dw_ref.at[s, taps - 1 - u:taps - u, :], _colsum(dm * _down(xe[s], u)[ms]), first)
            _acc(db_ref.at[s], _colsum(dm), first)

    in_specs = [pl.BlockSpec((2, None, HALO, c), lambda j, i: (0, j, _prev_halo(tm)(i), 0)),
                pl.BlockSpec((2, None, tm, c), lambda j, i: (0, j, i, 0)),
                pl.BlockSpec((2, None, HALO, c), lambda j, i: (0, j, _next_halo(tm, m)(i), 0)),
                pl.BlockSpec((None, tm, c), lambda j, i: (j, i, 0)),
                pl.BlockSpec((None, HALO, c), lambda j, i: (j, _next_halo(tm, m)(i), 0)),
                pl.BlockSpec((2, None, taps, c), lambda j, i: (0, j, 0, 0)),
                pl.BlockSpec((2, None, 1, c), lambda j, i: (0, j, 0, 0))]
    out_specs = [pl.BlockSpec((2, None, tm, c), lambda j, i: (0, j, i, 0)),
                 pl.BlockSpec((2, None, taps, c), lambda j, i: (0, j, 0, 0)),
                 pl.BlockSpec((2, None, 1, c), lambda j, i: (0, j, 0, 0))]
    return _call(body, name, (nj, n_i), in_specs, out_specs,
                 [_sds((2, nj, m, c), BF16), _sds((2, nj, taps, c)), _sds((2, nj, 1, c))])(z, z, z, dact, dact, cw, cb)


def _as2d(a, lead):
    shape = a.shape
    return a.reshape((lead, -1, shape[-1]) if lead else (-1, shape[-1]))


def add_pairs(g, l1, own, name):
    shape = l1.shape
    g3, l3 = _as2d(g, N_DEV), _as2d(l1, 4)
    _, r, c = l3.shape
    tr = _rows(r, 512) if r % 512 == 0 else r

    def body(own_ref, a_ref, b_ref, o_ref):
        o_ref[...] = (a_ref[...].astype(F32) + b_ref[...].astype(F32)).astype(o_ref.dtype)

    spec = pl.BlockSpec((None, tr, c), lambda k, i, own_ref: (k, i, 0))
    grid_spec = pltpu.PrefetchScalarGridSpec(
        num_scalar_prefetch=1, grid=(4, r // tr),
        in_specs=[pl.BlockSpec((None, tr, c), lambda k, i, own_ref: (own_ref[k], i, 0)), spec], out_specs=spec)
    out = pl.pallas_call(
        body, name=name, grid_spec=grid_spec, out_shape=_sds(l3.shape, l1.dtype),
        compiler_params=pltpu.CompilerParams(dimension_semantics=("arbitrary", "arbitrary"),
                                             vmem_limit_bytes=V7X_VMEM_LIMIT_BYTES))(own, g3, l3)
    return out.reshape(shape)


def _grad_sum(p_ref, l_ref):
    return ((p_ref[...].astype(F32) + l_ref[0].astype(F32)) + l_ref[1].astype(F32)) + l_ref[2].astype(F32)


def sum_parts(p, l2, name):
    _, r, c = p.shape

    def body(p_ref, l_ref, o_ref):
        o_ref[...] = _grad_sum(p_ref, l_ref)

    return _call(body, name, (1,), [pl.BlockSpec((None, r, c), lambda i: (0, 0, 0)), pl.BlockSpec((3, r, c), lambda i: (0, 0, 0))],
                 pl.BlockSpec((r, c), lambda i: (0, 0)), _sds((r, c)))(p, l2)


def _adamw_math(w, g, m, v):
    m = ADAM_B1 * m + (1.0 - ADAM_B1) * g
    v = ADAM_B2 * v + (1.0 - ADAM_B2) * (g * g)
    m_hat = m / (1.0 - ADAM_B1 ** ADAM_STEP)
    v_hat = v / (1.0 - ADAM_B2 ** ADAM_STEP)
    delta = -ADAM_LR * (m_hat / (jnp.sqrt(v_hat) + ADAM_EPS) + ADAM_WD * w)
    return delta, m, v


def adamw(w, m, v, name, g=None, p=None, l2=None):
    shape = w.shape
    w2, m2, v2 = (_as2d(t, 0) for t in (w, m, v))
    r, c = w2.shape
    tr = _rows(r, 512) if r % 512 == 0 else r
    row = pl.BlockSpec((tr, c), lambda i: (i, 0))
    if g is None:
        p3, l3 = _as2d(p, 4), _as2d(l2, 3)
        gin = (p3, l3)
        gspecs = [pl.BlockSpec((None, tr, c), lambda i: (0, i, 0)), pl.BlockSpec((3, tr, c), lambda i: (0, i, 0))]
    else:
        gin, gspecs = (_as2d(g, 0),), [row]

    def body(*refs):
        n_g = len(gin)
        w_ref, m_ref, v_ref, g_ref, d_ref, nm_ref, nv_ref = refs[n_g:]
        grad = refs[0][...] if n_g == 1 else _grad_sum(refs[0], refs[1])
        delta, nm, nv = _adamw_math(w_ref[...], grad, m_ref[...], v_ref[...])
        g_ref[...] = grad
        d_ref[...] = delta
        nm_ref[...] = nm
        nv_ref[...] = nv

    outs = _call(body, name, (r // tr,), gspecs + [row, row, row], [row] * 4, [_sds((r, c))] * 4)(*gin, w2, m2, v2)
    return tuple(o.reshape(shape) for o in outs)


def _comm_call(body, name, ins, out_shape, n_sems):
    any_spec = pl.BlockSpec(memory_space=pl.ANY)
    return pl.pallas_call(
        body, name=name, in_specs=[any_spec] * len(ins), out_specs=[any_spec] * len(out_shape), out_shape=out_shape,
        scratch_shapes=[pltpu.SemaphoreType.DMA((n,)) for n in n_sems],
        compiler_params=pltpu.CompilerParams(has_side_effects=True))(*ins)


def _place():
    return lax.axis_index("x"), lax.axis_index("y"), lax.axis_index("c")


def _dev_index(px, py, pc):
    return 4 * px + 2 * py + pc


def all_gather(blocks, name):
    n_t = len(blocks)

    def body(*refs):
        ins, outs = refs[:n_t], refs[n_t:2 * n_t]
        send_sems, recv_sems, local_sems = refs[2 * n_t:]
        x, y, c = _place()
        me, sibling = (x, y, c), (x, y, 1 - c)
        chips = [(1 - x, y), (x, 1 - y), (1 - x, 1 - y)]

        def copy(t, k, block, to, src=None):
            dst = outs[t].at[_dev_index(*block)]
            return pltpu.make_async_remote_copy(
                src_ref=dst if src is None else src, dst_ref=dst, send_sem=send_sems.at[t * 7 + k],
                recv_sem=recv_sems.at[t * 7 + k], device_id=to, device_id_type=MESH_ID)

        mine = [pltpu.make_async_copy(ins[t], outs[t].at[_dev_index(*me)], local_sems.at[t]) for t in range(n_t)]
        for cp in mine:
            cp.start()
        first = []
        for t in range(n_t):
            first.append(copy(t, 0, me, sibling, src=ins[t]))
            first += [copy(t, 1 + j, me, (*chip, c), src=ins[t]) for j, chip in enumerate(chips)]
        for cp in first:
            cp.start()
        passed = []
        for t in range(n_t):
            for j, chip in enumerate(chips):
                copy(t, 1 + j, (*chip, c), me).wait_recv()
                cp = copy(t, 4 + j, (*chip, c), sibling)
                cp.start()
                passed.append(cp)
        for t in range(n_t):
            copy(t, 0, sibling, me).wait_recv()
            for j, chip in enumerate(chips):
                copy(t, 4 + j, (*chip, 1 - c), me).wait_recv()
        for cp in first + passed:
            cp.wait_send()
        for cp in mine:
            cp.wait()

    out_shape = [_sds((N_DEV,) + b.shape, b.dtype) for b in blocks]
    return _comm_call(body, name, blocks, out_shape, (7 * n_t, 7 * n_t, n_t))


def _chip_of(x, y, k):
    return (x if k % 2 == 0 else 1 - x), (y if k // 2 == 0 else 1 - y)


def exchange_sibling(grads, name):
    n_t = len(grads)

    def body(*refs):
        ins, got = refs[:n_t], refs[n_t:2 * n_t]
        send_sems, recv_sems = refs[2 * n_t:]
        x, y, c = _place()
        sends = []
        for t in range(n_t):
            for k in range(4):
                cx, cy = _chip_of(x, y, k)
                sends.append(pltpu.make_async_remote_copy(
                    src_ref=ins[t].at[_dev_index(cx, cy, 1 - c)], dst_ref=got[t].at[k], send_sem=send_sems.at[4 * t + k],
                    recv_sem=recv_sems.at[4 * t + k], device_id=(x, y, 1 - c), device_id_type=MESH_ID))
        for cp in sends:
            cp.start()
        for cp in sends:
            cp.wait_recv()
        for cp in sends:
            cp.wait_send()

    half = [_sds((4,) + g.shape[1:], g.dtype) for g in grads]
    return _comm_call(body, name, grads, half, (4 * n_t, 4 * n_t))


def exchange_chips(parts, name):
    n_t = len(parts)

    def body(*refs):
        ins, got = refs[:n_t], refs[n_t:2 * n_t]
        send_sems, recv_sems = refs[2 * n_t:]
        x, y, c = _place()
        sends = []
        for t in range(n_t):
            for k in range(1, 4):
                cx, cy = _chip_of(x, y, k)
                sends.append(pltpu.make_async_remote_copy(
                    src_ref=ins[t].at[k], dst_ref=got[t].at[k - 1], send_sem=send_sems.at[3 * t + k - 1],
                    recv_sem=recv_sems.at[3 * t + k - 1], device_id=(cx, cy, c), device_id_type=MESH_ID))
        for cp in sends:
            cp.start()
        for cp in sends:
            cp.wait_recv()
        for cp in sends:
            cp.wait_send()

    out_shape = [_sds((3,) + p.shape[1:], p.dtype) for p in parts]
    return _comm_call(body, name, parts, out_shape, (3 * n_t, 3 * n_t))


def _pack(arrays, rows):
    flat = jnp.concatenate([a.reshape(-1) for a in arrays])
    return jnp.pad(flat, (0, rows * LANES - flat.shape[0])).reshape(rows, LANES)


def _pack_stacked(arrays, rows):
    flat = jnp.concatenate([a.reshape(N_DEV, -1) for a in arrays], axis=1)
    return jnp.pad(flat, ((0, 0), (0, rows * LANES - flat.shape[1]))).reshape(N_DEV, rows, LANES)


def _unpack(buf, shapes, lead=()):
    flat = buf.reshape(lead + (-1,))
    out, off = [], 0
    for s in shapes:
        n = 1
        for d in s:
            n *= d
        out.append(flat[..., off:off + n].reshape(lead + tuple(s)))
        off += n
    return out


def _padded_rows(shapes, multiple):
    n = sum(functools.reduce(lambda a, b: a * b, s, 1) for s in shapes)
    rows = -(-n // LANES)
    return -(-rows // multiple) * multiple


def _to_full(stacked, axis):
    t = jnp.moveaxis(stacked, 0, axis)
    return t.reshape(t.shape[:axis] + (t.shape[axis] * t.shape[axis + 1],) + t.shape[axis + 2:])


def _to_stacked(full, axis):
    s = full.shape
    t = full.reshape(s[:axis] + (N_DEV, s[axis] // N_DEV) + s[axis + 1:])
    return jnp.moveaxis(t, axis, 0)


def _ffn_forward(x, norm_g, w_up, cw, cb, w_down, tag):
    h = rms_fwd(x, norm_g, f"ffn{tag}_norm")
    z = mm_in(h, w_up, f"ffn{tag}_up", stacked_out=True)
    nb, m, c = z.shape
    z4 = z.reshape(2, nb // 2, m, c)
    act = ffn_mid_fwd(z4, cw, cb, f"ffn{tag}_mid")
    out = mm_out(act, w_down, x, f"ffn{tag}_down")
    return out, (x, h, z4, act)


def _ffn_backward(dx, saved, norm_g, w_up, cw, cb, w_down, tag):
    x, h, z4, act = saved
    nj = act.shape[0]
    dact = mm_dx_out(dx, w_down, f"ffn{tag}_down_dx", groups=nj)
    dw_down = mm_dw_out(act, dx, f"ffn{tag}_down_dw")
    dz4, dcw, dcb = ffn_mid_bwd(z4, dact, cw, cb, f"ffn{tag}_mid_bwd")
    dz = dz4.reshape((2 * nj,) + dz4.shape[2:])
    dh = mm_dx_in(dz, w_up, f"ffn{tag}_up_dx")
    dw_up = mm_dw_in(h, dz, 2 * nj, f"ffn{tag}_up_dw")
    dx, dg = rms_bwd(x, norm_g, dh, dx, f"ffn{tag}_norm_bwd")
    return dx, dict(norm_g=dg, w_up=dw_up, conv_w=dcw, conv_b=dcb, w_down=dw_down)


def kernel(x, a_norm_g, a_w_in, a_b_in, a_v_norm_g, a_w_s, a_b_s, a_w_out, b_norm_g, b_w_in, b_w_grp, b_b_grp, b_scale, b_w_out, c_norm_g, c_w_in, c_b_in, c_conv_w, c_conv_b, c_w_a, c_b_a, c_w_i, c_b_i, c_lambda, c_w_out, d_norm_g, d_w_in, d_conv_w, d_w_out, ffn_norm_g, ffn_w_up, ffn_conv_w, ffn_conv_b, ffn_w_down, final_norm_g, loss_target, m_a_norm_g, m_a_w_in, m_a_b_in, m_a_v_norm_g, m_a_w_s, m_a_b_s, m_a_w_out, m_b_norm_g, m_b_w_in, m_b_w_grp, m_b_b_grp, m_b_scale, m_b_w_out, m_c_norm_g, m_c_w_in, m_c_b_in, m_c_conv_w, m_c_conv_b, m_c_w_a, m_c_b_a, m_c_w_i, m_c_b_i, m_c_lambda, m_c_w_out, m_d_norm_g, m_d_w_in, m_d_conv_w, m_d_w_out, m_ffn_norm_g, m_ffn_w_up, m_ffn_conv_w, m_ffn_conv_b, m_ffn_w_down, m_final_norm_g, v_a_norm_g, v_a_w_in, v_a_b_in, v_a_v_norm_g, v_a_w_s, v_a_b_s, v_a_w_out, v_b_norm_g, v_b_w_in, v_b_w_grp, v_b_b_grp, v_b_scale, v_b_w_out, v_c_norm_g, v_c_w_in, v_c_b_in, v_c_conv_w, v_c_conv_b, v_c_w_a, v_c_b_a, v_c_w_i, v_c_b_i, v_c_lambda, v_c_w_out, v_d_norm_g, v_d_w_in, v_d_conv_w, v_d_w_out, v_ffn_norm_g, v_ffn_w_up, v_ffn_conv_w, v_ffn_conv_b, v_ffn_w_down, v_final_norm_g):
    args = locals()
    w_loc = {n: args[n] for n in WEIGHTS}
    m_loc = {n: args["m_" + n] for n in WEIGHTS}
    v_loc = {n: args["v_" + n] for n in WEIGHTS}
    depth = ffn_w_up.shape[0]
    xs = x[0]
    target = loss_target[0]

    big = [('a_w_in', a_w_in[0]), ('a_w_out', a_w_out[0]), ('b_w_in', b_w_in[0]), ('b_w_out', b_w_out[0]),
           ('c_w_in', c_w_in[0]), ('c_w_out', c_w_out[0]), ('d_w_in', d_w_in[0]), ('d_w_out', d_w_out[0])]
    big += [(f'ffn_w_up{l}', ffn_w_up[l]) for l in range(depth)] + [(f'ffn_w_down{l}', ffn_w_down[l]) for l in range(depth)]
    small_names = list(SMALL_SHARDED)
    small_shapes = [w_loc[n].shape for n in small_names]
    small_rows = _padded_rows(small_shapes, 8)
    small_packed = _pack([w_loc[n] for n in small_names], small_rows)
    gathered = all_gather([w.astype(BF16) for _, w in big] + [small_packed], "gather_weights")
    wg = {n: g for (n, _), g in zip(big, gathered[:-1])}
    small_st = _unpack(gathered[-1], small_shapes, (N_DEV,))
    sm = {n: _to_full(s, SMALL_SHARDED[n]) for n, s in zip(small_names, small_st)}

    def rows_full(st):
        return st.reshape((st.shape[0] * st.shape[1],) + st.shape[2:])

    wa_in, wc_in, wd_in = wg['a_w_in'], wg['c_w_in'], wg['d_w_in']
    wb_in = rows_full(wg['b_w_in'])[None]
    wa_out, wb_out, wc_out, wd_out = (rows_full(wg[n]) for n in ('a_w_out', 'b_w_out', 'c_w_out', 'd_w_out'))
    w_up = [wg[f'ffn_w_up{l}'] for l in range(depth)]
    w_down = [rows_full(wg[f'ffn_w_down{l}']) for l in range(depth)]
    nb = N_DEV
    ffn_cw = [sm['ffn_conv_w'][l].reshape(ffn_conv_w.shape[1], 2, nb // 2, -1).transpose(1, 2, 0, 3) for l in range(depth)]
    ffn_cb = [ffn_conv_b[l].reshape(2, nb // 2, 1, -1) for l in range(depth)]
    ffn_g = [ffn_norm_g[l:l + 1] for l in range(depth)]
    a_bst = a_b_s[0].T
    b_wgrp, c_wa, c_wi = (sm[n][0].astype(BF16) for n in ('b_w_grp', 'c_w_a', 'c_w_i'))
    b_bgrp, c_ba, c_bi = (sm[n][0].reshape(1, -1) for n in ('b_b_grp', 'c_b_a', 'c_b_i'))

    saved = {}
    h = rms_fwd(xs, a_norm_g, "a_norm")
    z = mm_in(h, wa_in, "a_in", bias=a_b_in)
    y = a_mid_fwd(z, a_v_norm_g, a_w_s[0], a_bst, "a_mid")
    x1 = mm_out(y, wa_out, xs, "a_out")
    saved['a'] = (xs, h, z, y)
    x1, saved['f0'] = _ffn_forward(x1, ffn_g[0], w_up[0], ffn_cw[0], ffn_cb[0], w_down[0], 0)

    h = rms_fwd(x1, sm['b_norm_g'], "b_norm")
    z = mm_in(h, wb_in, "b_in")
    y = b_mid_fwd(z, b_wgrp, b_bgrp, sm['b_scale'], "b_mid")
    x2 = mm_out(y, wb_out, x1, "b_out")
    saved['b'] = (x1, h, z, y)
    x2, saved['f1'] = _ffn_forward(x2, ffn_g[1], w_up[1], ffn_cw[1], ffn_cb[1], w_down[1], 1)

    h = rms_fwd(x2, sm['c_norm_g'], "c_norm")
    z = mm_in(h, wc_in, "c_in", bias=sm['c_b_in'])
    c_cw = sm['c_conv_w'][0]
    a_seq, b_seq, xr = c_mid_fwd(z, c_cw, sm['c_conv_b'], c_wa, c_ba, c_wi, c_bi, sm['c_lambda'], "c_mid")
    hs, y = c_scan_fwd(a_seq, b_seq, z, "c_scan")
    x3 = mm_out(y, wc_out, x2, "c_out")
    saved['c'] = (x2, h, z, y, a_seq, xr, hs)
    x3, saved['f2'] = _ffn_forward(x3, ffn_g[2], w_up[2], ffn_cw[2], ffn_cb[2], w_down[2], 2)

    h = rms_fwd(x3, sm['d_norm_g'], "d_norm")
    z = mm_in(h, wd_in, "d_in")
    d_cw = sm['d_conv_w'][0]
    y = d_mid_fwd(z, d_cw, "d_mid")
    x4 = mm_out(y, wd_out, x3, "d_out")
    saved['d'] = (x3, h, z, y)
    x4, saved['f3'] = _ffn_forward(x4, ffn_g[3], w_up[3], ffn_cw[3], ffn_cb[3], w_down[3], 3)

    loss_part, dx, d_final_g = final_loss(x4, final_norm_g.reshape(1, -1), target, "final_loss")
    loss = lax.psum(loss_part[0, 0], ("x", "y", "c"))

    gf = [None] * depth
    dx, gf[3] = _ffn_backward(dx, saved['f3'], ffn_g[3], w_up[3], ffn_cw[3], ffn_cb[3], w_down[3], 3)
    xin, h, z, y = saved['d']
    dy = mm_dx_out(dx, wd_out, "d_out_dx")
    g_d_w_out = mm_dw_out(y, dx, "d_out_dw")
    dz, g_d_conv_w = d_mid_bwd(z, dy, d_cw, "d_mid_bwd")
    dh = mm_dx_in(dz, wd_in, "d_in_dx")
    g_d_w_in = mm_dw_in(h, dz, nb, "d_in_dw")
    dx, g_d_norm_g = rms_bwd(xin, sm['d_norm_g'], dh, dx, "d_norm_bwd")

    dx, gf[2] = _ffn_backward(dx, saved['f2'], ffn_g[2], w_up[2], ffn_cw[2], ffn_cb[2], w_down[2], 2)
    xin, h, z, y, a_seq, xr, hs = saved['c']
    dy = mm_dx_out(dx, wc_out, "c_out_dx")
    g_c_w_out = mm_dw_out(y, dx, "c_out_dw")
    lam_seq, da_seq, dgate, dgate_sum = c_scan_bwd(dy, z, hs, a_seq, "c_scan_bwd")
    dxr, g_c_w_a, g_c_w_i, g_c_b_a, g_c_b_i, g_c_lambda = c_mid_bwd(
        lam_seq, da_seq, xr, c_wa, c_ba, c_wi, c_bi, sm['c_lambda'], "c_mid_bwd")
    dxr_pre, g_c_conv_w, g_c_conv_b, dxr_pre_sum = conv_bwd(dxr, z, 1, c_cw, "c_conv_bwd")
    dz = jnp.concatenate([dgate, dxr_pre], axis=1)
    g_c_b_in = jnp.concatenate([dgate_sum, dxr_pre_sum], axis=1)
    dh = mm_dx_in(dz, wc_in, "c_in_dx")
    g_c_w_in = mm_dw_in(h, dz, nb, "c_in_dw")
    dx, g_c_norm_g = rms_bwd(xin, sm['c_norm_g'], dh, dx, "c_norm_bwd")

    dx, gf[1] = _ffn_backward(dx, saved['f1'], ffn_g[1], w_up[1], ffn_cw[1], ffn_cb[1], w_down[1], 1)
    xin, h, z, y = saved['b']
    dy = mm_dx_out(dx, wb_out, "b_out_dx")
    g_b_w_out = mm_dw_out(y, dx, "b_out_dw")
    dp, g_b_w_grp, g_b_b_grp, g_b_scale = b_mid_bwd(z, dy, b_wgrp, b_bgrp, sm['b_scale'], "b_mid_bwd")
    dz = b_pool_bwd(dp, "b_pool_bwd")
    dh = mm_dx_in(dz, wb_in, "b_in_dx")
    g_b_w_in = mm_dw_in(h, dz, 1, "b_in_dw")
    dx, g_b_norm_g = rms_bwd(xin, sm['b_norm_g'], dh, dx, "b_norm_bwd")

    dx, gf[0] = _ffn_backward(dx, saved['f0'], ffn_g[0], w_up[0], ffn_cw[0], ffn_cb[0], w_down[0], 0)
    xin, h, z, y = saved['a']
    dy = mm_dx_out(dx, wa_out, "a_out_dx")
    g_a_w_out = mm_dw_out(y, dx, "a_out_dw")
    dz, g_a_b_in, g_a_v_norm_g, g_a_w_s, g_a_b_s = a_mid_bwd(z, dy, a_v_norm_g, a_w_s[0], a_bst, "a_mid_bwd")
    dh = mm_dx_in(dz, wa_in, "a_in_dx")
    g_a_w_in = mm_dw_in(h, dz, nb, "a_in_dw")
    dx, g_a_norm_g = rms_bwd(xin, a_norm_g, dh, dx, "a_norm_bwd")
    grad_x = dx[None]

    def rows_stacked(full):
        return full.reshape((N_DEV, full.shape[0] // N_DEV) + full.shape[1:])

    big_grads = [g_a_w_in, rows_stacked(g_a_w_out), rows_stacked(g_b_w_in[0]), rows_stacked(g_b_w_out), g_c_w_in,
                 rows_stacked(g_c_w_out), g_d_w_in, rows_stacked(g_d_w_out)]
    big_grads += [gf[l]['w_up'] for l in range(depth)] + [rows_stacked(gf[l]['w_down']) for l in range(depth)]
    tril = jnp.tril(jnp.ones((A_CHUNK, A_CHUNK), bool))
    full_small = {
        'b_norm_g': g_b_norm_g, 'b_w_grp': g_b_w_grp[None], 'b_b_grp': g_b_b_grp.reshape(b_b_grp.shape[:2] + (-1,)),
        'b_scale': g_b_scale, 'c_norm_g': g_c_norm_g, 'c_b_in': g_c_b_in, 'c_conv_w': g_c_conv_w[None],
        'c_conv_b': g_c_conv_b, 'c_w_a': g_c_w_a[None], 'c_b_a': g_c_b_a.reshape(c_b_a.shape[:2] + (-1,)),
        'c_w_i': g_c_w_i[None], 'c_b_i': g_c_b_i.reshape(c_b_i.shape[:2] + (-1,)), 'c_lambda': g_c_lambda,
        'd_norm_g': g_d_norm_g, 'd_conv_w': g_d_conv_w[None],
        'ffn_conv_w': jnp.stack([gf[l]['conv_w'].transpose(2, 0, 1, 3).reshape(ffn_conv_w.shape[1], -1) for l in range(depth)])}
    small_grads = _pack_stacked([_to_stacked(full_small[n], SMALL_SHARDED[n]) for n in small_names], small_rows)
    repl_full = {
        'a_norm_g': g_a_norm_g, 'a_b_in': g_a_b_in, 'a_v_norm_g': g_a_v_norm_g,
        'a_w_s': jnp.where(tril, g_a_w_s, 0.0)[None], 'a_b_s': g_a_b_s[:, ::LANES].T[None],
        'ffn_norm_g': jnp.concatenate([gf[l]['norm_g'] for l in range(depth)], axis=0),
        'ffn_conv_b': jnp.stack([gf[l]['conv_b'].reshape(-1) for l in range(depth)]), 'final_norm_g': d_final_g.reshape(-1)}
    repl_shapes = [w_loc[n].shape for n in REPLICATED]
    repl_rows = _padded_rows(repl_shapes, 8 * N_DEV)
    repl_grads = _pack([repl_full[n] for n in REPLICATED], repl_rows).reshape(N_DEV, repl_rows // N_DEV, LANES)

    grads = big_grads + [small_grads, repl_grads]
    got = exchange_sibling(grads, "rs_sibling")
    mx, my, mc = _place()
    own = jnp.stack([_dev_index(*_chip_of(mx, my, k), mc) for k in range(4)]).astype(jnp.int32)
    parts = [add_pairs(g, l, own, f"rs_add{t}") for t, (g, l) in enumerate(zip(grads, got))]
    others = exchange_chips(parts, "rs_chips")

    out_g, out_d, out_m, out_v = {}, {}, {}, {}
    per_layer = {}
    for t, (n, _) in enumerate(big):
        if n.startswith('ffn_w'):
            base, l = n[:-1], int(n[-1])
            res = adamw(w_loc[base][l], m_loc[base][l], v_loc[base][l], f"adamw_{n}", p=parts[t], l2=others[t])
            per_layer.setdefault(base, []).append(res)
        else:
            res = adamw(w_loc[n][0], m_loc[n][0], v_loc[n][0], f"adamw_{n}", p=parts[t], l2=others[t])
            out_g[n], out_d[n], out_m[n], out_v[n] = (r[None] for r in res)
    for base, res in per_layer.items():
        out_g[base], out_d[base], out_m[base], out_v[base] = (jnp.stack([r[i] for r in res]) for i in range(4))

    t_small = len(big)
    res = adamw(small_packed, _pack([m_loc[n] for n in small_names], small_rows),
                _pack([v_loc[n] for n in small_names], small_rows), "adamw_small", p=parts[t_small], l2=others[t_small])
    for dst, r in zip((out_g, out_d, out_m, out_v), res):
        for n, a in zip(small_names, _unpack(r, small_shapes)):
            dst[n] = a

    t_repl = t_small + 1
    chunk = sum_parts(parts[t_repl], others[t_repl], "rs_sum_repl")
    repl_g = all_gather([chunk], "gather_repl")[0].reshape(repl_rows, LANES)
    res = adamw(_pack([w_loc[n] for n in REPLICATED], repl_rows), _pack([m_loc[n] for n in REPLICATED], repl_rows),
                _pack([v_loc[n] for n in REPLICATED], repl_rows), "adamw_repl", g=repl_g)
    for dst, r in zip((out_g, out_d, out_m, out_v), res):
        for n, a in zip(REPLICATED, _unpack(r, repl_shapes)):
            dst[n] = a

    return (loss, grad_x, *[out_g[n] for n in WEIGHTS], *[out_d[n] for n in WEIGHTS], *[out_m[n] for n in WEIGHTS],
            *[out_v[n] for n in WEIGHTS])
```

```python
import functools

import jax
import jax.numpy as jnp
from jax import lax
from jax.experimental import pallas as pl
from jax.experimental.pallas import tpu as pltpu

F32, BF16 = jnp.float32, jnp.bfloat16
MESH_ID = pl.DeviceIdType.MESH
N_DEV = 8
V7X_VMEM_LIMIT_BYTES = 56 << 20
LANES = 128
HALO = 8
POOL_HALO = 16

EPS = 1e-6
A_CHUNK, A_GROUPS = 128, 4
B_WINDOWS = (2, 4, 8, 16)
C_GATE_C = 8.0
ADAM_LR, ADAM_B1, ADAM_B2, ADAM_EPS, ADAM_WD, ADAM_STEP = 0.001, 0.9, 0.999, 1e-08, 0.01, 10

WEIGHTS = ['a_norm_g', 'a_w_in', 'a_b_in', 'a_v_norm_g', 'a_w_s', 'a_b_s', 'a_w_out', 'b_norm_g', 'b_w_in', 'b_w_grp',
           'b_b_grp', 'b_scale', 'b_w_out', 'c_norm_g', 'c_w_in', 'c_b_in', 'c_conv_w', 'c_conv_b', 'c_w_a', 'c_b_a',
           'c_w_i', 'c_b_i', 'c_lambda', 'c_w_out', 'd_norm_g', 'd_w_in', 'd_conv_w', 'd_w_out', 'ffn_norm_g',
           'ffn_w_up', 'ffn_conv_w', 'ffn_conv_b', 'ffn_w_down', 'final_norm_g']
SMALL_SHARDED = {'b_norm_g': 1, 'b_w_grp': 2, 'b_b_grp': 2, 'b_scale': 1, 'c_norm_g': 1, 'c_b_in': 1, 'c_conv_w': 2,
                 'c_conv_b': 1, 'c_w_a': 2, 'c_b_a': 2, 'c_w_i': 2, 'c_b_i': 2, 'c_lambda': 1, 'd_norm_g': 1,
                 'd_conv_w': 2, 'ffn_conv_w': 2}
REPLICATED = ['a_norm_g', 'a_b_in', 'a_v_norm_g', 'a_w_s', 'a_b_s', 'ffn_norm_g', 'ffn_conv_b', 'final_norm_g']


_GELU_C0, _GELU_C1 = 0.7978845608028654, 0.044715


def _gelu(x):
    return 0.5 * x * (1.0 + jnp.tanh(_GELU_C0 * (x + _GELU_C1 * (x * x * x))))


def _gelu_grad(x):
    t = jnp.tanh(_GELU_C0 * (x + _GELU_C1 * (x * x * x)))
    return 0.5 * (1.0 + t) + 0.5 * x * (1.0 - t * t) * (_GELU_C0 * (1.0 + 3.0 * _GELU_C1 * (x * x)))


def _sigmoid(x):
    return jax.nn.sigmoid(x)


def _log1p(x):
    u = 1.0 + x
    return jnp.where(u == 1.0, x, jnp.log(u) * (x / (u - 1.0)))


def _softplus(x):
    return jnp.maximum(x, 0.0) + _log1p(jnp.exp(-jnp.abs(x)))


def _expm1(x):
    poly = x * (1.0 + x * (1 / 2) * (1.0 + x * (1 / 3) * (1.0 + x * (1 / 4) * (1.0 + x * (1 / 5) * (
        1.0 + x * (1 / 6) * (1.0 + x * (1 / 7) * (1.0 + x * (1 / 8))))))))
    return jnp.where(jnp.abs(x) < 0.35, poly, jnp.exp(x) - 1.0)


def _down(xe, s):
    return xe if s == 0 else pltpu.roll(xe, s, 0)


def _up(xe, s):
    return xe if s == 0 else pltpu.roll(xe, xe.shape[0] - s, 0)


def _conv_ext(xe, w, taps):
    y = xe * w[taps - 1:taps]
    for s in range(1, taps):
        y = y + _down(xe, s) * w[taps - 1 - s:taps - s]
    return y


def _acc(ref, val, first):
    @pl.when(first)
    def _():
        ref[...] = val

    @pl.when(jnp.logical_not(first))
    def _():
        ref[...] += val


def _colsum(v):
    return jnp.sum(v, axis=0, keepdims=True)


def _dot(a, b, dims=((1,), (0,))):
    return lax.dot_general(a.astype(BF16), b.astype(BF16), (dims, ((), ())), preferred_element_type=F32)


_NN, _NT, _TN = ((1,), (0,)), ((1,), (1,)), ((0,), (0,))


def _call(body, name, grid, in_specs, out_specs, out_shape, scratch=(), after=()):
    n_in, n_after = len(in_specs), len(after)

    def ordered_body(*refs):
        return body(*refs[:n_in], *refs[n_in + n_after:])

    call = pl.pallas_call(
        ordered_body if n_after else body, name=name, grid=grid,
        in_specs=list(in_specs) + [pl.BlockSpec(memory_space=pl.ANY)] * n_after, out_specs=out_specs,
        out_shape=out_shape, scratch_shapes=list(scratch),
        compiler_params=pltpu.CompilerParams(dimension_semantics=("arbitrary",) * len(grid),
                                             vmem_limit_bytes=V7X_VMEM_LIMIT_BYTES))
    return lambda *args: call(*args, *after)


def _rows(m, t):
    t = min(m, t)
    assert m % t == 0, (m, t)
    return t


def _sds(shape, dtype=F32):
    return jax.ShapeDtypeStruct(tuple(shape), dtype)


def _prev_halo(tm, halo=HALO):
    return lambda i: jnp.maximum(i * (tm // halo) - 1, 0)


def _next_halo(tm, m, halo=HALO):
    return lambda i: jnp.minimum((i + 1) * (tm // halo), m // halo - 1)


def _matmul(name, a, b, out_shape, grid, a_spec, b_spec, o_spec, dims, reduce_last, extras=(), extra_specs=(), after=()):
    n_red = grid[-1] if reduce_last else 1
    n_grid = len(grid)
    scratch_acc = n_red > 1 and out_shape.dtype != F32
    scratch = [pltpu.VMEM(tuple(d for d in o_spec.block_shape if d is not None), F32)] if scratch_acc else []

    def body(a_ref, b_ref, *rest):
        if scratch_acc:
            acc_ref, rest = rest[-1], rest[:-1]
        o_ref, ex = rest[-1], rest[:-1]
        if not scratch_acc:
            acc_ref = o_ref
        p = _dot(a_ref[...], b_ref[...], dims)

        def first():
            q = p
            for e in ex:
                q = q + e[...]
            return q

        if n_red == 1:
            o_ref[...] = first().astype(o_ref.dtype)
        else:
            r = pl.program_id(n_grid - 1)

            @pl.when(r == 0)
            def _():
                acc_ref[...] = first()

            @pl.when(r > 0)
            def _():
                acc_ref[...] += p

            if scratch_acc:
                @pl.when(r == n_red - 1)
                def _():
                    o_ref[...] = acc_ref[...].astype(o_ref.dtype)

    return _call(body, name, grid, [a_spec, b_spec, *extra_specs], o_spec, out_shape, scratch, after)(a, b, *extras)


def mm_in(h, w_st, name, bias=None, stacked_out=False):
    m, k = h.shape
    nb, _, n = w_st.shape
    tm = _rows(m, 1024)
    a_spec = pl.BlockSpec((tm, k), lambda i, j: (i, 0))
    b_spec = pl.BlockSpec((None, k, n), lambda i, j: (j, 0, 0))
    if stacked_out:
        out, o_spec = _sds((nb, m, n)), pl.BlockSpec((None, tm, n), lambda i, j: (j, i, 0))
    else:
        out, o_spec = _sds((m, nb * n)), pl.BlockSpec((tm, n), lambda i, j: (i, j))
    extras, especs = (), ()
    if bias is not None:
        extras, especs = (bias,), (pl.BlockSpec((1, n), lambda i, j: (0, j)),)
    return _matmul(name, h, w_st, out, (m // tm, nb), a_spec, b_spec, o_spec, _NN, False, extras, especs)


def _split_rows(kf):
    g = max(1, kf // 1024)
    return g, kf // g


def mm_out(y, w, res, name):
    kf, n = w.shape
    if y.ndim == 3:
        g, m, k = y.shape
        a_spec_of = lambda tm: pl.BlockSpec((None, tm, k), lambda i, r: (r, i, 0))
    else:
        m = y.shape[0]
        g, k = _split_rows(kf)
        a_spec_of = lambda tm: pl.BlockSpec((tm, k), lambda i, r: (i, r))
    tm = _rows(m, 1024)
    b_spec = pl.BlockSpec((None, k, n), lambda i, r: (r, 0, 0))
    o_spec = pl.BlockSpec((tm, n), lambda i, r: (i, 0))
    return _matmul(name, y, w.reshape(g, k, n), _sds((m, n)), (m // tm, g), a_spec_of(tm), b_spec, o_spec, _NN, True,
                   (res,), (o_spec,))


def mm_dx_in(dz, w_st, name):
    nb, k, n = w_st.shape
    m = dz.shape[-2]
    tm = _rows(m, 1024)
    if dz.ndim == 3:
        a_spec = pl.BlockSpec((None, tm, n), lambda i, r: (r, i, 0))
    else:
        a_spec = pl.BlockSpec((tm, n), lambda i, r: (i, r))
    b_spec = pl.BlockSpec((None, k, n), lambda i, r: (r, 0, 0))
    o_spec = pl.BlockSpec((tm, k), lambda i, r: (i, 0))
    return _matmul(name, dz, w_st, _sds((m, k)), (m // tm, nb), a_spec, b_spec, o_spec, _NT, True)


def mm_dx_out(dout, w, name, groups=None, after=()):
    kf, n = w.shape
    m = dout.shape[0]
    tm = _rows(m, 1024)
    g, k = (groups, kf // groups) if groups else _split_rows(kf)
    a_spec = pl.BlockSpec((tm, n), lambda i, j: (i, 0))
    b_spec = pl.BlockSpec((None, k, n), lambda i, j: (j, 0, 0))
    if groups:
        out, o_spec = _sds((g, m, k)), pl.BlockSpec((None, tm, k), lambda i, j: (j, i, 0))
    else:
        out, o_spec = _sds((m, kf)), pl.BlockSpec((tm, k), lambda i, j: (i, j))
    return _matmul(name, dout, w.reshape(g, k, n), out, (m // tm, g), a_spec, b_spec, o_spec, _NT, False, after=after)


def mm_dw_in(h, dz, nb, name):
    m, k = h.shape
    tm = _rows(m, 1024)
    a_spec = pl.BlockSpec((tm, k), lambda j, r: (r, 0))
    if dz.ndim == 3:
        n = dz.shape[2]
        b_spec = pl.BlockSpec((None, tm, n), lambda j, r: (j, r, 0))
    else:
        n = dz.shape[1] // nb
        b_spec = pl.BlockSpec((tm, n), lambda j, r: (r, j))
    o_spec = pl.BlockSpec((None, k, n), lambda j, r: (j, 0, 0))
    return _matmul(name, h, dz, _sds((nb, k, n), BF16), (nb, m // tm), a_spec, b_spec, o_spec, _TN, True)


def mm_dw_out(y, dout, name):
    m, n = dout.shape
    tm = _rows(m, 1024)
    if y.ndim == 3:
        g, _, k = y.shape
        a_spec = pl.BlockSpec((None, tm, k), lambda j, r: (j, r, 0))
    else:
        g, k = _split_rows(y.shape[1])
        a_spec = pl.BlockSpec((tm, k), lambda j, r: (r, j))
    b_spec = pl.BlockSpec((tm, n), lambda j, r: (r, 0))
    o_spec = pl.BlockSpec((None, k, n), lambda j, r: (j, 0, 0))
    out = _matmul(name, y, dout, _sds((g, k, n), BF16), (g, m // tm), a_spec, b_spec, o_spec, _TN, True)
    return out.reshape(g * k, n)


def rms_fwd(x, g, name):
    m, d = x.shape
    tm = _rows(m, 512)

    def body(x_ref, g_ref, o_ref):
        xv = x_ref[...]
        rstd = lax.rsqrt(jnp.mean(xv * xv, axis=-1, keepdims=True) + EPS)
        o_ref[...] = (xv * rstd * g_ref[...]).astype(BF16)

    row = pl.BlockSpec((tm, d), lambda i: (i, 0))
    vec = pl.BlockSpec((1, d), lambda i: (0, 0))
    return _call(body, name, (m // tm,), [row, vec], row, _sds((m, d), BF16))(x, g)


def _rms_bwd_math(xv, g, dh):
    rstd = lax.rsqrt(jnp.mean(xv * xv, axis=-1, keepdims=True) + EPS)
    xhat = xv * rstd
    dxhat = dh * g
    dx = rstd * (dxhat - xhat * jnp.mean(dxhat * xhat, axis=-1, keepdims=True))
    return dx, _colsum(dh * xhat)


def rms_bwd(x, g, dh, dres, name):
    m, d = x.shape
    tm = _rows(m, 512)

    def body(x_ref, g_ref, dh_ref, dr_ref, dx_ref, dg_ref):
        dx, dg = _rms_bwd_math(x_ref[...], g_ref[...], dh_ref[...])
        dx_ref[...] = dr_ref[...] + dx
        _acc(dg_ref, dg, pl.program_id(0) == 0)

    row = pl.BlockSpec((tm, d), lambda i: (i, 0))
    vec = pl.BlockSpec((1, d), lambda i: (0, 0))
    return _call(body, name, (m // tm,), [row, vec, row, row], [row, vec], [_sds((m, d)), _sds((1, d))])(x, g, dh, dres)


def final_loss(x, g, target, name):
    m, d = x.shape
    tm = _rows(m, 512)

    def body(x_ref, g_ref, t_ref, l_ref, dx_ref, dg_ref):
        xv, gv = x_ref[...], g_ref[...]
        rstd = lax.rsqrt(jnp.mean(xv * xv, axis=-1, keepdims=True) + EPS)
        err = xv * rstd * gv - t_ref[...]
        part = 0.5 * jnp.sum(jnp.mean(err * err, axis=-1, keepdims=True), axis=0, keepdims=True)
        dx, dg = _rms_bwd_math(xv, gv, err * (1.0 / d))
        dx_ref[...] = dx
        first = pl.program_id(0) == 0
        _acc(l_ref, jnp.broadcast_to(part, l_ref.shape), first)
        _acc(dg_ref, dg, first)

    row = pl.BlockSpec((tm, d), lambda i: (i, 0))
    vec = pl.BlockSpec((1, d), lambda i: (0, 0))
    lsp = pl.BlockSpec((1, LANES), lambda i: (0, 0))
    return _call(body, name, (m // tm,), [row, vec, row], [lsp, row, vec],
                 [_sds((1, LANES)), _sds((m, d)), _sds((1, d))])(x, g, target)


def _a_common(z_ref, vg_ref, ws_ref, bst_ref, tm, width):
    gw = width // A_GROUPS
    zp = z_ref[...]
    z = _gelu(zp)
    u, v = z[:, :width], z[:, width:]
    rstd = lax.rsqrt(jnp.mean(v * v, axis=-1, keepdims=True) + EPS)
    vhat = v * rstd
    vn = vhat * vg_ref[...]
    t_i = lax.broadcasted_iota(jnp.int32, (A_CHUNK, A_CHUNK), 0)
    s_i = lax.broadcasted_iota(jnp.int32, (A_CHUNK, A_CHUNK), 1)
    wsm = [jnp.where(s_i <= t_i, ws_ref[g], 0.0).astype(BF16) for g in range(A_GROUPS)]
    bst = bst_ref[...]
    return zp, u, rstd, vhat, vn.astype(BF16), wsm, bst, gw


def a_mid_fwd(z, vg, ws, bst, name):
    m, w2 = z.shape
    width = w2 // 2
    tm = _rows(m, 256)

    def body(z_ref, vg_ref, ws_ref, bst_ref, y_ref):
        _, u, _, _, vnb, wsm, bst, gw = _a_common(z_ref, vg_ref, ws_ref, bst_ref, tm, width)
        for c in range(tm // A_CHUNK):
            r0 = c * A_CHUNK
            for g in range(A_GROUPS):
                c0 = g * gw
                vs = _dot(wsm[g], vnb[r0:r0 + A_CHUNK, c0:c0 + gw]) + bst[:, g:g + 1]
                y_ref[r0:r0 + A_CHUNK, c0:c0 + gw] = (u[r0:r0 + A_CHUNK, c0:c0 + gw] * vs).astype(BF16)

    in_specs = [pl.BlockSpec((tm, w2), lambda i: (i, 0)), pl.BlockSpec((1, width), lambda i: (0, 0)),
                pl.BlockSpec((A_GROUPS, A_CHUNK, A_CHUNK), lambda i: (0, 0, 0)),
                pl.BlockSpec((A_CHUNK, A_GROUPS), lambda i: (0, 0))]
    return _call(body, name, (m // tm,), in_specs, pl.BlockSpec((tm, width), lambda i: (i, 0)),
                 _sds((m, width), BF16))(z, vg, ws, bst)


def a_mid_bwd(z, dy, vg, ws, bst, name):
    m, w2 = z.shape
    width = w2 // 2
    tm = _rows(m, 256)

    def body(z_ref, dy_ref, vg_ref, ws_ref, bst_ref, dz_ref, dbin_ref, dvg_ref, dws_ref, dbs_ref, dvn_scr, du_scr):
        first = pl.program_id(0) == 0
        zp, u, rstd, vhat, vnb, wsm, bst, gw = _a_common(z_ref, vg_ref, ws_ref, bst_ref, tm, width)
        dy = dy_ref[...]
        dws = [jnp.zeros((A_CHUNK, A_CHUNK), F32) for _ in range(A_GROUPS)]
        dbs = [jnp.zeros((A_CHUNK, 1), F32) for _ in range(A_GROUPS)]
        for c in range(tm // A_CHUNK):
            r0 = c * A_CHUNK
            for g in range(A_GROUPS):
                c0 = g * gw
                vn_cg = vnb[r0:r0 + A_CHUNK, c0:c0 + gw]
                vs = _dot(wsm[g], vn_cg) + bst[:, g:g + 1]
                dy_cg = dy[r0:r0 + A_CHUNK, c0:c0 + gw]
                dvs = dy_cg * u[r0:r0 + A_CHUNK, c0:c0 + gw]
                du_scr[r0:r0 + A_CHUNK, c0:c0 + gw] = dy_cg * vs
                dws[g] = dws[g] + _dot(dvs, vn_cg, _NT)
                dbs[g] = dbs[g] + jnp.sum(dvs, axis=1, keepdims=True)
                dvn_scr[r0:r0 + A_CHUNK, c0:c0 + gw] = _dot(wsm[g], dvs, _TN)
        for g in range(A_GROUPS):
            _acc(dws_ref.at[g], dws[g], first)
            _acc(dbs_ref.at[:, g * LANES:(g + 1) * LANES], jnp.broadcast_to(dbs[g], (A_CHUNK, LANES)), first)
        dvn = dvn_scr[...]
        _acc(dvg_ref, _colsum(dvn * vhat), first)
        dvhat = dvn * vg_ref[...]
        dv = rstd * (dvhat - vhat * jnp.mean(dvhat * vhat, axis=-1, keepdims=True))
        gg = _gelu_grad(zp)
        dzu = du_scr[...] * gg[:, :width]
        dzv = dv * gg[:, width:]
        dz_ref[:, :width] = dzu.astype(BF16)
        dz_ref[:, width:] = dzv.astype(BF16)
        _acc(dbin_ref.at[:, :width], _colsum(dzu), first)
        _acc(dbin_ref.at[:, width:], _colsum(dzv), first)

    const2 = lambda i: (0, 0)
    in_specs = [pl.BlockSpec((tm, w2), lambda i: (i, 0)), pl.BlockSpec((tm, width), lambda i: (i, 0)),
                pl.BlockSpec((1, width), const2), pl.BlockSpec((A_GROUPS, A_CHUNK, A_CHUNK), lambda i: (0, 0, 0)),
                pl.BlockSpec((A_CHUNK, A_GROUPS), const2)]
    out_specs = [pl.BlockSpec((tm, w2), lambda i: (i, 0)), pl.BlockSpec((1, w2), const2), pl.BlockSpec((1, width), const2),
                 pl.BlockSpec((A_GROUPS, A_CHUNK, A_CHUNK), lambda i: (0, 0, 0)),
                 pl.BlockSpec((A_CHUNK, A_GROUPS * LANES), const2)]
    out_shape = [_sds((m, w2), BF16), _sds((1, w2)), _sds((1, width)), _sds((A_GROUPS, A_CHUNK, A_CHUNK)),
                 _sds((A_CHUNK, A_GROUPS * LANES))]
    scratch = [pltpu.VMEM((tm, width), F32), pltpu.VMEM((tm, width), F32)]
    return _call(body, name, (m // tm,), in_specs, out_specs, out_shape, scratch)(z, dy, vg, ws, bst)


def _pool_minus_id(ze, i, tm, gw):
    pos = i * tm + lax.broadcasted_iota(jnp.int32, (tm, 1), 0)
    out = []
    for gi, win in enumerate(B_WINDOWS):
        s = ze[:, gi * gw:(gi + 1) * gw]
        step = 1
        while step < win:
            s = s + _down(s, step)
            step *= 2
        inv = 1.0 / jnp.minimum(pos + 1, win).astype(F32)
        out.append(s[POOL_HALO:] * inv - ze[POOL_HALO:, gi * gw:(gi + 1) * gw])
    return out


def _b_specs(tm, width):
    return [pl.BlockSpec((POOL_HALO, width), lambda i: (_prev_halo(tm, POOL_HALO)(i), 0)),
            pl.BlockSpec((tm, width), lambda i: (i, 0))]


def b_mid_fwd(z, wgrp, bgrp, scale, name):
    m, width = z.shape
    ng = len(B_WINDOWS)
    gw = width // ng
    tm = _rows(m, 512)

    def body(zp_ref, zm_ref, w_ref, b_ref, s_ref, y_ref):
        i = pl.program_id(0)
        ze = jnp.concatenate([zp_ref[...] * (i > 0).astype(F32), zm_ref[...]], axis=0)
        p = _pool_minus_id(ze, i, tm, gw)
        for g in range(ng):
            cs = slice(g * gw, (g + 1) * gw)
            y = (_dot(p[g], w_ref[g]) + b_ref[:, cs]) * s_ref[:, cs]
            y_ref[:, cs] = y.astype(BF16)

    vec = pl.BlockSpec((1, width), lambda i: (0, 0))
    in_specs = _b_specs(tm, width) + [pl.BlockSpec((ng, gw, gw), lambda i: (0, 0, 0)), vec, vec]
    return _call(body, name, (m // tm,), in_specs, pl.BlockSpec((tm, width), lambda i: (i, 0)),
                 _sds((m, width), BF16))(z, z, wgrp, bgrp, scale)


def b_mid_bwd(z, dy, wgrp, bgrp, scale, name):
    m, width = z.shape
    ng = len(B_WINDOWS)
    gw = width // ng
    tm = _rows(m, 512)

    def body(zp_ref, zm_ref, dy_ref, w_ref, b_ref, s_ref, dp_ref, dw_ref, db_ref, ds_ref):
        i = pl.program_id(0)
        first = i == 0
        ze = jnp.concatenate([zp_ref[...] * (i > 0).astype(F32), zm_ref[...]], axis=0)
        p = _pool_minus_id(ze, i, tm, gw)
        for g in range(ng):
            cs = slice(g * gw, (g + 1) * gw)
            dyg = dy_ref[:, cs]
            ypre = _dot(p[g], w_ref[g]) + b_ref[:, cs]
            dyp = dyg * s_ref[:, cs]
            _acc(ds_ref.at[:, cs], _colsum(dyg * ypre), first)
            _acc(db_ref.at[:, cs], _colsum(dyp), first)
            _acc(dw_ref.at[g], _dot(p[g], dyp, _TN), first)
            dp_ref[:, cs] = _dot(dyp, w_ref[g], _NT)

    vec = pl.BlockSpec((1, width), lambda i: (0, 0))
    row = pl.BlockSpec((tm, width), lambda i: (i, 0))
    wsp = pl.BlockSpec((ng, gw, gw), lambda i: (0, 0, 0))
    return _call(body, name, (m // tm,), _b_specs(tm, width) + [row, wsp, vec, vec], [row, wsp, vec, vec],
                 [_sds((m, width)), _sds((ng, gw, gw)), _sds((1, width)), _sds((1, width))])(z, z, dy, wgrp, bgrp, scale)


def b_pool_bwd(dp, name):
    m, width = dp.shape
    gw = width // len(B_WINDOWS)
    tm = _rows(m, 512)
    n_i = m // tm

    def body(dm_ref, dn_ref, dz_ref):
        i = pl.program_id(0)
        de = jnp.concatenate([dm_ref[...], dn_ref[...] * (i < n_i - 1).astype(F32)], axis=0)
        pos = i * tm + lax.broadcasted_iota(jnp.int32, (tm + POOL_HALO, 1), 0)
        for gi, win in enumerate(B_WINDOWS):
            cs = slice(gi * gw, (gi + 1) * gw)
            d = de[:, cs]
            s = d * (1.0 / jnp.minimum(pos + 1, win).astype(F32))
            step = 1
            while step < win:
                s = s + _up(s, step)
                step *= 2
            dz_ref[:, cs] = (s[:tm] - d[:tm]).astype(BF16)

    in_specs = [pl.BlockSpec((tm, width), lambda i: (i, 0)),
                pl.BlockSpec((POOL_HALO, width), lambda i: (_next_halo(tm, m, POOL_HALO)(i), 0))]
    return _call(body, name, (n_i,), in_specs, pl.BlockSpec((tm, width), lambda i: (i, 0)), _sds((m, width), BF16))(dp, dp)


def _c_gates(xr, wa_ref, ba_ref, wi_ref, bi_ref, lam_ref, heads, hw):
    xb = xr.astype(BF16)
    ra = jnp.concatenate([_dot(xb[:, h * hw:(h + 1) * hw], wa_ref[h]) for h in range(heads)], axis=1) + ba_ref[...]
    ia = jnp.concatenate([_dot(xb[:, h * hw:(h + 1) * hw], wi_ref[h]) for h in range(heads)], axis=1) + bi_ref[...]
    r, ig = _sigmoid(ra), _sigmoid(ia)
    sp = _softplus(-lam_ref[...])
    log_a = (-C_GATE_C * r) * sp
    a = jnp.exp(log_a)
    mult = jnp.sqrt(-_expm1(2.0 * log_a))
    return xb, r, ig, sp, a, mult


def c_mid_fwd(z, cw, cb, wa, ba, wi, bi, lam, name):
    m, w2 = z.shape
    width = w2 // 2
    heads, hw = wa.shape[0], wa.shape[1]
    taps = cw.shape[0]
    tm = _rows(m, 512)

    def body(zp_ref, zm_ref, cw_ref, cb_ref, wa_ref, ba_ref, wi_ref, bi_ref, lam_ref, a_ref, b_ref, xr_ref):
        i = pl.program_id(0)
        xe = jnp.concatenate([zp_ref[...] * (i > 0).astype(F32), zm_ref[...]], axis=0)
        xr = _conv_ext(xe, cw_ref[...], taps)[HALO:] + cb_ref[...]
        _, _, ig, _, a, mult = _c_gates(xr, wa_ref, ba_ref, wi_ref, bi_ref, lam_ref, heads, hw)
        a_ref[...] = a
        b_ref[...] = mult * (ig * xr)
        xr_ref[...] = xr

    vec = pl.BlockSpec((1, width), lambda i: (0, 0))
    row = pl.BlockSpec((tm, width), lambda i: (i, 0))
    wsp = pl.BlockSpec((heads, hw, hw), lambda i: (0, 0, 0))
    in_specs = [pl.BlockSpec((HALO, width), lambda i: (_prev_halo(tm)(i), 1)), pl.BlockSpec((tm, width), lambda i: (i, 1)),
                pl.BlockSpec((taps, width), lambda i: (0, 0)), vec, wsp, vec, wsp, vec, vec]
    return _call(body, name, (m // tm,), in_specs, [row, row, row], [_sds((m, width))] * 3)(
        z, z, cw, cb, wa, ba, wi, bi, lam)


_SCAN_ROWS = 512


def c_scan_fwd(a, b, z, name):
    m, width = a.shape
    tm = _rows(m, _SCAN_ROWS)

    def body(a_ref, b_ref, g_ref, hs_ref, y_ref, h_carry):
        @pl.when(pl.program_id(0) == 0)
        def _():
            h_carry[...] = jnp.zeros_like(h_carry)

        def step(t, h):
            h = a_ref[pl.ds(t, 1), :] * h + b_ref[pl.ds(t, 1), :]
            hs_ref[pl.ds(t, 1), :] = h
            return h

        h_carry[...] = lax.fori_loop(0, tm, step, h_carry[...], unroll=8)
        y_ref[...] = (hs_ref[...] * _gelu(g_ref[...])).astype(BF16)

    row = pl.BlockSpec((tm, width), lambda i: (i, 0))
    return _call(body, name, (m // tm,), [row, row, row], [row, row], [_sds((m, width)), _sds((m, width), BF16)],
                 [pltpu.VMEM((1, width), F32)])(a, b, z)


def c_scan_bwd(dy, z, hs, a, name):
    m, width = a.shape
    tm = _rows(m, _SCAN_ROWS)
    n_i = m // tm

    def body(dy_ref, g_ref, hs_ref, hp_ref, a_ref, lam_ref, da_ref, dg_ref, dgs_ref, lam_carry, a_carry):
        i = pl.program_id(0)
        first = i == 0

        @pl.when(first)
        def _():
            lam_carry[...] = jnp.zeros_like(lam_carry)
            a_carry[...] = jnp.zeros_like(a_carry)

        gp, dyv, hsv = g_ref[...], dy_ref[...], hs_ref[...]
        dgate = dyv * hsv * _gelu_grad(gp)
        dg_ref[...] = dgate.astype(BF16)
        _acc(dgs_ref, _colsum(dgate), first)
        lam_ref[...] = dyv * _gelu(gp)

        def step(k, carry):
            lam_next, a_next = carry
            t = tm - 1 - k
            lam_t = lam_ref[pl.ds(t, 1), :] + a_next * lam_next
            lam_ref[pl.ds(t, 1), :] = lam_t
            return lam_t, a_ref[pl.ds(t, 1), :]

        lam_c, a_c = lax.fori_loop(0, tm, step, (lam_carry[...], a_carry[...]), unroll=8)
        lam_carry[...] = lam_c
        a_carry[...] = a_c
        h_before = hp_ref[HALO - 1:HALO, :] * (i < n_i - 1).astype(F32)
        t_i = lax.broadcasted_iota(jnp.int32, (tm, 1), 0)
        da_ref[...] = lam_ref[...] * jnp.where(t_i == 0, h_before, _down(hsv, 1))

    row = pl.BlockSpec((tm, width), lambda i: (n_i - 1 - i, 0))
    halo = pl.BlockSpec((HALO, width), lambda i: (_prev_halo(tm)(n_i - 1 - i), 0))
    vec = pl.BlockSpec((1, width), lambda i: (0, 0))
    return _call(body, name, (n_i,), [row, row, row, halo, row], [row, row, row, vec],
                 [_sds((m, width)), _sds((m, width)), _sds((m, width), BF16), _sds((1, width))],
                 [pltpu.VMEM((1, width), F32), pltpu.VMEM((1, width), F32)])(dy, z, hs, hs, a)


def c_mid_bwd(lam_seq, da, xr, wa, ba, wi, bi, lam, name):
    m, width = xr.shape
    heads, hw = wa.shape[0], wa.shape[1]
    tm = _rows(m, 512)

    def body(l_ref, da_ref, xr_ref, wa_ref, ba_ref, wi_ref, bi_ref, lam_ref,
             dxr_ref, dwa_ref, dwi_ref, dba_ref, dbi_ref, dlam_ref):
        first = pl.program_id(0) == 0
        xr_v, lmb = xr_ref[...], l_ref[...]
        xb, r, ig, sp, a, mult = _c_gates(xr_v, wa_ref, ba_ref, wi_ref, bi_ref, lam_ref, heads, hw)
        dmult = lmb * (ig * xr_v)
        dig = lmb * mult * xr_v
        dxr = lmb * mult * ig
        dla = da_ref[...] * a - dmult * (a * a) / mult
        dr = dla * (-C_GATE_C * sp)
        dsp = _colsum(dla * (-C_GATE_C * r))
        _acc(dlam_ref, dsp * (-_sigmoid(-lam_ref[...])), first)
        dra = dr * r * (1.0 - r)
        dia = dig * ig * (1.0 - ig)
        _acc(dba_ref, _colsum(dra), first)
        _acc(dbi_ref, _colsum(dia), first)
        for h in range(heads):
            cs = slice(h * hw, (h + 1) * hw)
            _acc(dwa_ref.at[h], _dot(xb[:, cs], dra[:, cs], _TN), first)
            _acc(dwi_ref.at[h], _dot(xb[:, cs], dia[:, cs], _TN), first)
            dxr_ref[:, cs] = dxr[:, cs] + _dot(dra[:, cs], wa_ref[h], _NT) + _dot(dia[:, cs], wi_ref[h], _NT)

    vec = pl.BlockSpec((1, width), lambda i: (0, 0))
    row = pl.BlockSpec((tm, width), lambda i: (i, 0))
    wsp = pl.BlockSpec((heads, hw, hw), lambda i: (0, 0, 0))
    return _call(body, name, (m // tm,), [row, row, row, wsp, vec, wsp, vec, vec], [row, wsp, wsp, vec, vec, vec],
                 [_sds((m, width)), _sds((heads, hw, hw)), _sds((heads, hw, hw)), _sds((1, width)), _sds((1, width)),
                  _sds((1, width))])(lam_seq, da, xr, wa, ba, wi, bi, lam)


def conv_bwd(dy, x_src, col_block, cw, name):
    m, width = dy.shape
    taps = cw.shape[0]
    tm = _rows(m, 512)
    n_i = m // tm

    def body(dm_ref, dn_ref, xp_ref, xm_ref, cw_ref, dx_ref, dw_ref, db_ref, dxs_ref):
        i = pl.program_id(0)
        first = i == 0
        de = jnp.concatenate([jnp.zeros((HALO, width), F32), dm_ref[...], dn_ref[...] * (i < n_i - 1).astype(F32)], axis=0)
        xe = jnp.concatenate([xp_ref[...] * (i > 0).astype(F32), xm_ref[...], jnp.zeros((HALO, width), F32)], axis=0)
        w = cw_ref[...]
        dx = de * w[taps - 1:taps]
        for s in range(1, taps):
            dx = dx + _up(de, s) * w[taps - 1 - s:taps - s]
        dx_ref[...] = dx[HALO:HALO + tm].astype(BF16)
        _acc(dxs_ref, _colsum(dx[HALO:HALO + tm]), first)
        dm = dm_ref[...]
        for s in range(taps):
            _acc(dw_ref.at[taps - 1 - s:taps - s, :], _colsum(dm * _down(xe, s)[HALO:HALO + tm]), first)
        _acc(db_ref, _colsum(dm), first)

    row = pl.BlockSpec((tm, width), lambda i: (i, 0))
    vec = pl.BlockSpec((1, width), lambda i: (0, 0))
    tsp = pl.BlockSpec((taps, width), lambda i: (0, 0))
    in_specs = [row, pl.BlockSpec((HALO, width), lambda i: (_next_halo(tm, m)(i), 0)),
                pl.BlockSpec((HALO, width), lambda i: (_prev_halo(tm)(i), col_block)),
                pl.BlockSpec((tm, width), lambda i: (i, col_block)), tsp]
    return _call(body, name, (n_i,), in_specs, [row, tsp, vec, vec],
                 [_sds((m, width), BF16), _sds((taps, width)), _sds((1, width)), _sds((1, width))])(dy, dy, x_src, x_src, cw)


def d_mid_fwd(z, cw, name):
    m, w3 = z.shape
    width = w3 // 3
    taps = cw.shape[0]
    tm = _rows(m, 512)

    def body(bm_ref, cp_ref, cm_ref, xp_ref, xm_ref, cw_ref, y_ref):
        keep = (pl.program_id(0) > 0).astype(F32)
        qe = (jnp.concatenate([cp_ref[...] * keep, cm_ref[...]], axis=0)
              * jnp.concatenate([xp_ref[...], xm_ref[...]], axis=0))
        y_ref[...] = (bm_ref[...] * _conv_ext(qe, cw_ref[...], taps)[HALO:]).astype(BF16)

    main = lambda c: pl.BlockSpec((tm, width), lambda i: (i, c))
    prev = lambda c: pl.BlockSpec((HALO, width), lambda i: (_prev_halo(tm)(i), c))
    in_specs = [main(0), prev(1), main(1), prev(2), main(2), pl.BlockSpec((taps, width), lambda i: (0, 0))]
    return _call(body, name, (m // tm,), in_specs, pl.BlockSpec((tm, width), lambda i: (i, 0)),
                 _sds((m, width), BF16))(z, z, z, z, z, cw)


def d_mid_bwd(z, dy, cw, name):
    m, w3 = z.shape
    width = w3 // 3
    taps = cw.shape[0]
    tm = _rows(m, 512)
    n_i = m // tm

    def body(bm_ref, bn_ref, cp_ref, cm_ref, cn_ref, xp_ref, xm_ref, xn_ref, dm_ref, dn_ref, cw_ref, dz_ref, dw_ref):
        i = pl.program_id(0)
        first = i == 0
        kp, kn = (i > 0).astype(F32), (i < n_i - 1).astype(F32)
        zeros = jnp.zeros((HALO, width), F32)
        ce = jnp.concatenate([cp_ref[...] * kp, cm_ref[...], cn_ref[...] * kn], axis=0)
        xe = jnp.concatenate([xp_ref[...], xm_ref[...], xn_ref[...]], axis=0)
        qe = ce * xe
        be = jnp.concatenate([zeros, bm_ref[...], bn_ref[...]], axis=0)
        dye = jnp.concatenate([zeros, dm_ref[...], dn_ref[...] * kn], axis=0)
        w = cw_ref[...]
        cq = _conv_ext(qe, w, taps)
        dcq = dye * be
        dq = dcq * w[taps - 1:taps]
        for s in range(1, taps):
            dq = dq + _up(dcq, s) * w[taps - 1 - s:taps - s]
        ms = slice(HALO, HALO + tm)
        dz_ref[:, :width] = (dye * cq)[ms].astype(BF16)
        dz_ref[:, width:2 * width] = (dq * xe)[ms].astype(BF16)
        dz_ref[:, 2 * width:] = (dq * ce)[ms].astype(BF16)
        for s in range(taps):
            _acc(dw_ref.at[taps - 1 - s:taps - s, :], _colsum(dcq[ms] * _down(qe, s)[ms]), first)

    main = lambda c: pl.BlockSpec((tm, width), lambda i: (i, c))
    prev = lambda c: pl.BlockSpec((HALO, width), lambda i: (_prev_halo(tm)(i), c))
    nxt = lambda c: pl.BlockSpec((HALO, width), lambda i: (_next_halo(tm, m)(i), c))
    tsp = pl.BlockSpec((taps, width), lambda i: (0, 0))
    in_specs = [main(0), nxt(0), prev(1), main(1), nxt(1), prev(2), main(2), nxt(2), main(0), nxt(0), tsp]
    return _call(body, name, (n_i,), in_specs, [pl.BlockSpec((tm, w3), lambda i: (i, 0)), tsp],
                 [_sds((m, w3), BF16), _sds((taps, width))])(z, z, z, z, z, z, z, z, dy, dy, cw)


def ffn_mid_fwd(z, cw, cb, name):
    _, nj, m, c = z.shape
    taps = cw.shape[2]
    tm = _rows(m, 512)

    def body(zp_ref, zm_ref, cw_ref, cb_ref, o_ref):
        keep = (pl.program_id(1) > 0).astype(F32)
        zc = []
        for s in range(2):
            xe = jnp.concatenate([zp_ref[s] * keep, zm_ref[s]], axis=0)
            zc.append(_conv_ext(xe, cw_ref[s], taps)[HALO:] + cb_ref[s])
        o_ref[...] = (zc[0] * _sigmoid(zc[0]) * zc[1]).astype(BF16)

    in_specs = [pl.BlockSpec((2, None, HALO, c), lambda j, i: (0, j, _prev_halo(tm)(i), 0)),
                pl.BlockSpec((2, None, tm, c), lambda j, i: (0, j, i, 0)),
                pl.BlockSpec((2, None, taps, c), lambda j, i: (0, j, 0, 0)),
                pl.BlockSpec((2, None, 1, c), lambda j, i: (0, j, 0, 0))]
    return _call(body, name, (nj, m // tm), in_specs, pl.BlockSpec((None, tm, c), lambda j, i: (j, i, 0)),
                 _sds((nj, m, c), BF16))(z, z, cw, cb)


def ffn_mid_bwd(z, dact, cw, cb, name):
    _, nj, m, c = z.shape
    taps = cw.shape[2]
    tm = _rows(m, 512)
    n_i = m // tm

    def body(zp_ref, zm_ref, zn_ref, dm_ref, dn_ref, cw_ref, cb_ref, dz_ref, dw_ref, db_ref):
        i = pl.program_id(1)
        first = i == 0
        kp, kn = (i > 0).astype(F32), (i < n_i - 1).astype(F32)
        xe = [jnp.concatenate([zp_ref[s] * kp, zm_ref[s], zn_ref[s] * kn], axis=0) for s in range(2)]
        zc = [_conv_ext(xe[s], cw_ref[s], taps) + cb_ref[s] for s in range(2)]
        dae = jnp.concatenate([jnp.zeros((HALO, c), F32), dm_ref[...], dn_ref[...] * kn], axis=0)
        sg = _sigmoid(zc[0])
        dzc = [dae * zc[1] * (sg * (1.0 + zc[0] * (1.0 - sg))), dae * (zc[0] * sg)]
        ms = slice(HALO, HALO + tm)
        for s in range(2):
            w = cw_ref[s]
            dx = dzc[s] * w[taps - 1:taps]
            for u in range(1, taps):
                dx = dx + _up(dzc[s], u) * w[taps - 1 - u:taps - u]
            dz_ref[s] = dx[ms].astype(BF16)
            dm = dzc[s][ms]
            for u in range(taps):
                _acc(dw_ref.at[s, taps - 1 - u:taps - u, :], _colsum(dm * _down(xe[s], u)[ms]), first)
            _acc(db_ref.at[s], _colsum(dm), first)

    in_specs = [pl.BlockSpec((2, None, HALO, c), lambda j, i: (0, j, _prev_halo(tm)(i), 0)),
                pl.BlockSpec((2, None, tm, c), lambda j, i: (0, j, i, 0)),
                pl.BlockSpec((2, None, HALO, c), lambda j, i: (0, j, _next_halo(tm, m)(i), 0)),
                pl.BlockSpec((None, tm, c), lambda j, i: (j, i, 0)),
                pl.BlockSpec((None, HALO, c), lambda j, i: (j, _next_halo(tm, m)(i), 0)),
                pl.BlockSpec((2, None, taps, c), lambda j, i: (0, j, 0, 0)),
                pl.BlockSpec((2, None, 1, c), lambda j, i: (0, j, 0, 0))]
    out_specs = [pl.BlockSpec((2, None, tm, c), lambda j, i: (0, j, i, 0)),
                 pl.BlockSpec((2, None, taps, c), lambda j, i: (0, j, 0, 0)),
                 pl.BlockSpec((2, None, 1, c), lambda j, i: (0, j, 0, 0))]
    return _call(body, name, (nj, n_i), in_specs, out_specs,
                 [_sds((2, nj, m, c), BF16), _sds((2, nj, taps, c)), _sds((2, nj, 1, c))])(z, z, z, dact, dact, cw, cb)


def _as2d(a, lead):
    shape = a.shape
    return a.reshape((lead, -1, shape[-1]) if lead else (-1, shape[-1]))


def add_pairs(g, l1, own, name):
    shape = l1.shape
    g3, l3 = _as2d(g, N_DEV), _as2d(l1, 4)
    _, r, c = l3.shape
    tr = _rows(r, 512) if r % 512 == 0 else r

    def body(own_ref, a_ref, b_ref, o_ref):
        o_ref[...] = (a_ref[...].astype(F32) + b_ref[...].astype(F32)).astype(o_ref.dtype)

    spec = pl.BlockSpec((None, tr, c), lambda k, i, own_ref: (k, i, 0))
    grid_spec = pltpu.PrefetchScalarGridSpec(
        num_scalar_prefetch=1, grid=(4, r // tr),
        in_specs=[pl.BlockSpec((None, tr, c), lambda k, i, own_ref: (own_ref[k], i, 0)), spec], out_specs=spec)
    out = pl.pallas_call(
        body, name=name, grid_spec=grid_spec, out_shape=_sds(l3.shape, l1.dtype),
        compiler_params=pltpu.CompilerParams(dimension_semantics=("arbitrary", "arbitrary"),
                                             vmem_limit_bytes=V7X_VMEM_LIMIT_BYTES))(own, g3, l3)
    return out.reshape(shape)


def _grad_sum(p_ref, l_ref):
    return ((p_ref[...].astype(F32) + l_ref[0].astype(F32)) + l_ref[1].astype(F32)) + l_ref[2].astype(F32)


def sum_parts(p, l2, name):
    _, r, c = p.shape

    def body(p_ref, l_ref, o_ref):
        o_ref[...] = _grad_sum(p_ref, l_ref)

    return _call(body, name, (1,), [pl.BlockSpec((None, r, c), lambda i: (0, 0, 0)), pl.BlockSpec((3, r, c), lambda i: (0, 0, 0))],
                 pl.BlockSpec((r, c), lambda i: (0, 0)), _sds((r, c)))(p, l2)


def _adamw_math(w, g, m, v):
    m = ADAM_B1 * m + (1.0 - ADAM_B1) * g
    v = ADAM_B2 * v + (1.0 - ADAM_B2) * (g * g)
    m_hat = m / (1.0 - ADAM_B1 ** ADAM_STEP)
    v_hat = v / (1.0 - ADAM_B2 ** ADAM_STEP)
    delta = -ADAM_LR * (m_hat / (jnp.sqrt(v_hat) + ADAM_EPS) + ADAM_WD * w)
    return delta, m, v


def adamw(w, m, v, name, g=None, p=None, l2=None):
    shape = w.shape
    w2, m2, v2 = (_as2d(t, 0) for t in (w, m, v))
    r, c = w2.shape
    tr = _rows(r, 512) if r % 512 == 0 else r
    row = pl.BlockSpec((tr, c), lambda i: (i, 0))
    if g is None:
        p3, l3 = _as2d(p, 4), _as2d(l2, 3)
        gin = (p3, l3)
        gspecs = [pl.BlockSpec((None, tr, c), lambda i: (0, i, 0)), pl.BlockSpec((3, tr, c), lambda i: (0, i, 0))]
    else:
        gin, gspecs = (_as2d(g, 0),), [row]

    def body(*refs):
        n_g = len(gin)
        w_ref, m_ref, v_ref, g_ref, d_ref, nm_ref, nv_ref = refs[n_g:]
        grad = refs[0][...] if n_g == 1 else _grad_sum(refs[0], refs[1])
        delta, nm, nv = _adamw_math(w_ref[...], grad, m_ref[...], v_ref[...])
        g_ref[...] = grad
        d_ref[...] = delta
        nm_ref[...] = nm
        nv_ref[...] = nv

    outs = _call(body, name, (r // tr,), gspecs + [row, row, row], [row] * 4, [_sds((r, c))] * 4)(*gin, w2, m2, v2)
    return tuple(o.reshape(shape) for o in outs)


def _comm_call(body, name, ins, out_shape, n_sems):
    any_spec = pl.BlockSpec(memory_space=pl.ANY)
    return pl.pallas_call(
        body, name=name, in_specs=[any_spec] * len(ins), out_specs=[any_spec] * len(out_shape), out_shape=out_shape,
        scratch_shapes=[pltpu.SemaphoreType.DMA((n,)) for n in n_sems],
        compiler_params=pltpu.CompilerParams(has_side_effects=True))(*ins)


def _place():
    return lax.axis_index("x"), lax.axis_index("y"), lax.axis_index("c")


def _dev_index(px, py, pc):
    return 4 * px + 2 * py + pc


def all_gather(blocks, name):
    n_t = len(blocks)

    def body(*refs):
        ins, outs = refs[:n_t], refs[n_t:2 * n_t]
        send_sems, recv_sems, local_sems = refs[2 * n_t:]
        x, y, c = _place()
        me, sibling = (x, y, c), (x, y, 1 - c)
        chips = [(1 - x, y), (x, 1 - y), (1 - x, 1 - y)]

        def copy(t, k, block, to, src=None):
            dst = outs[t].at[_dev_index(*block)]
            return pltpu.make_async_remote_copy(
                src_ref=dst if src is None else src, dst_ref=dst, send_sem=send_sems.at[t * 7 + k],
                recv_sem=recv_sems.at[t * 7 + k], device_id=to, device_id_type=MESH_ID)

        mine = [pltpu.make_async_copy(ins[t], outs[t].at[_dev_index(*me)], local_sems.at[t]) for t in range(n_t)]
        for cp in mine:
            cp.start()
        first = []
        for t in range(n_t):
            first.append(copy(t, 0, me, sibling, src=ins[t]))
            first += [copy(t, 1 + j, me, (*chip, c), src=ins[t]) for j, chip in enumerate(chips)]
        for cp in first:
            cp.start()
        passed = []
        for t in range(n_t):
            for j, chip in enumerate(chips):
                copy(t, 1 + j, (*chip, c), me).wait_recv()
                cp = copy(t, 4 + j, (*chip, c), sibling)
                cp.start()
                passed.append(cp)
        for t in range(n_t):
            copy(t, 0, sibling, me).wait_recv()
            for j, chip in enumerate(chips):
                copy(t, 4 + j, (*chip, 1 - c), me).wait_recv()
        for cp in first + passed:
            cp.wait_send()
        for cp in mine:
            cp.wait()

    out_shape = [_sds((N_DEV,) + b.shape, b.dtype) for b in blocks]
    return _comm_call(body, name, blocks, out_shape, (7 * n_t, 7 * n_t, n_t))


def _chip_of(x, y, k):
    return (x if k % 2 == 0 else 1 - x), (y if k // 2 == 0 else 1 - y)


_HBM_SPEC = pl.BlockSpec(memory_space=pltpu.HBM)
_SEM_SPEC = pl.BlockSpec(memory_space=pltpu.SEMAPHORE)
_DATAFLOW = pltpu.SideEffectType.DATAFLOW_SIDE_EFFECTING


def _in_hbm(a):
    return pltpu.with_memory_space_constraint(a, pltpu.HBM)


def _split_start(name, issue, srcs, land_shapes, sem_counts):
    n_buf, n_sem = len(srcs) + len(land_shapes), len(sem_counts)

    def body(*refs):
        issue(refs[:len(srcs)], refs[len(srcs):n_buf], refs[n_buf:n_buf + n_sem])
        refs[-1][...] = jnp.zeros_like(refs[-1])

    bufs = [pltpu.HBM(s.shape, s.dtype) for s in list(srcs) + list(land_shapes)]
    outs = pl.pallas_call(
        body, name=name, in_specs=(_HBM_SPEC,) * n_buf,
        out_shape=(*[pltpu.SemaphoreType.DMA((n,)) for n in sem_counts], *bufs, _sds((8, LANES))),
        out_specs=(*[_SEM_SPEC] * n_sem, *[_HBM_SPEC] * n_buf, pl.BlockSpec(memory_space=pltpu.VMEM)),
        input_output_aliases={i: n_sem + i for i in range(n_buf)},
        compiler_params=pltpu.CompilerParams(has_side_effects=_DATAFLOW),
    )(*[_in_hbm(s) for s in srcs], *[_in_hbm(lax.empty(s.shape, s.dtype)) for s in land_shapes])
    return outs[:n_sem], outs[n_sem:n_sem + len(srcs)], outs[n_sem + len(srcs):n_sem + n_buf], outs[-1]


def _split_wait(name, finish, sems, srcs, lands, after):
    n_buf, n_sem = len(srcs) + len(lands), len(sems)

    def body(*refs):
        finish(refs[:len(srcs)], refs[len(srcs):n_buf], refs[n_buf:n_buf + n_sem])

    bufs = [pltpu.HBM(s.shape, s.dtype) for s in list(srcs) + list(lands)]
    outs = pl.pallas_call(
        body, name=name, in_specs=(*[_HBM_SPEC] * n_buf, *[_SEM_SPEC] * n_sem, *[pl.BlockSpec(memory_space=pl.ANY)] * len(after)),
        out_shape=tuple(bufs), out_specs=(_HBM_SPEC,) * n_buf, input_output_aliases={i: i for i in range(n_buf)},
        compiler_params=pltpu.CompilerParams(has_side_effects=_DATAFLOW),
    )(*srcs, *lands, *sems, *after)
    return outs[:len(srcs)], outs[len(srcs):]


def _peer(x, y, c, r):
    return (1 - x if r & 4 else x), (1 - y if r & 2 else y), (1 - c if r & 1 else c)


def _gather_copies(src_refs, land_refs, sem_refs, arrivals):
    send_sems, recv_sems, local_sems = sem_refs
    x, y, c = _place()
    me = _dev_index(x, y, c)
    local, sends, recvs = [], [], []
    for j, (src, land) in enumerate(zip(src_refs, land_refs)):
        local.append(pltpu.make_async_copy(src, land.at[me], local_sems.at[j]))
        for r in range(1, N_DEV):
            peer = _peer(x, y, c, r)
            q = (N_DEV - 1) * j + r - 1
            sends.append(pltpu.make_async_remote_copy(src_ref=src, dst_ref=land.at[me], send_sem=send_sems.at[q],
                                                      recv_sem=recv_sems.at[q], device_id=peer, device_id_type=MESH_ID))
            if arrivals:
                recvs.append(pltpu.make_async_remote_copy(
                    src_ref=src, dst_ref=land.at[_dev_index(*peer)], send_sem=send_sems.at[q], recv_sem=recv_sems.at[q],
                    device_id=peer, device_id_type=MESH_ID))
    return local, sends, recvs


def gather_start(groups, name):
    flat = [b for g in groups for b in g]
    bounds = [sum(len(g) for g in groups[:i]) for i in range(len(groups) + 1)]

    def issue(src_refs, land_refs, sem_refs):
        for i in range(len(groups)):
            lo, hi = bounds[i], bounds[i + 1]
            local, sends, _ = _gather_copies(src_refs[lo:hi], land_refs[lo:hi], sem_refs[3 * i:3 * i + 3], False)
            for cp in local + sends:
                cp.start()

    sem_counts = [n for g in groups for n in ((N_DEV - 1) * len(g), (N_DEV - 1) * len(g), len(g))]
    sems, srcs, lands, token = _split_start(name, issue, flat, [_sds((N_DEV,) + b.shape, b.dtype) for b in flat], sem_counts)
    return [(sems[3 * i:3 * i + 3], srcs[bounds[i]:bounds[i + 1]], lands[bounds[i]:bounds[i + 1]])
            for i in range(len(groups))], token


def gather_wait(group, after, name):
    sems, srcs, lands = group

    def finish(src_refs, land_refs, sem_refs):
        local, sends, recvs = _gather_copies(src_refs, land_refs, sem_refs, True)
        for cp in local:
            cp.wait()
        for cp in recvs:
            cp.wait_recv()
        for cp in sends:
            cp.wait_send()

    return _split_wait(name, finish, sems, srcs, lands, after)[1]


def _sibling_copies(src_refs, land_refs, sem_refs):
    send_sems, recv_sems = sem_refs
    x, y, c = _place()
    copies = []
    for t, (src, land) in enumerate(zip(src_refs, land_refs)):
        for k in range(4):
            cx, cy = _chip_of(x, y, k)
            copies.append(pltpu.make_async_remote_copy(
                src_ref=src.at[_dev_index(cx, cy, 1 - c)], dst_ref=land.at[k], send_sem=send_sems.at[4 * t + k],
                recv_sem=recv_sems.at[4 * t + k], device_id=(x, y, 1 - c), device_id_type=MESH_ID))
    return copies


def _chip_copies(src_refs, land_refs, sem_refs):
    send_sems, recv_sems = sem_refs
    x, y, c = _place()
    copies = []
    for t, (src, land) in enumerate(zip(src_refs, land_refs)):
        for k in range(1, 4):
            cx, cy = _chip_of(x, y, k)
            copies.append(pltpu.make_async_remote_copy(
                src_ref=src.at[k], dst_ref=land.at[k - 1], send_sem=send_sems.at[3 * t + k - 1],
                recv_sem=recv_sems.at[3 * t + k - 1], device_id=(cx, cy, c), device_id_type=MESH_ID))
    return copies


def _exchange_start(copies_of, n_land, per_array, arrays, name):
    def issue(src_refs, land_refs, sem_refs):
        for cp in copies_of(src_refs, land_refs, sem_refs):
            cp.start()

    n = per_array * len(arrays)
    lands = [_sds((n_land,) + a.shape[1:], a.dtype) for a in arrays]
    return _split_start(name, issue, arrays, lands, (n, n))


def _exchange_wait(copies_of, started, after, name):
    sems, srcs, lands, _ = started

    def finish(src_refs, land_refs, sem_refs):
        copies = copies_of(src_refs, land_refs, sem_refs)
        for cp in copies:
            cp.wait_recv()
        for cp in copies:
            cp.wait_send()

    return _split_wait(name, finish, sems, srcs, lands, after)


def sibling_start(grads, name):
    return _exchange_start(_sibling_copies, 4, 4, grads, name)


def sibling_wait(started, after, name):
    return _exchange_wait(_sibling_copies, started, after, name)


def chips_start(parts, name):
    return _exchange_start(_chip_copies, 3, 3, parts, name)


def chips_wait(started, after, name):
    return _exchange_wait(_chip_copies, started, after, name)


def _pack(arrays, rows):
    flat = jnp.concatenate([a.reshape(-1) for a in arrays])
    return jnp.pad(flat, (0, rows * LANES - flat.shape[0])).reshape(rows, LANES)


def _pack_stacked(arrays, rows):
    flat = jnp.concatenate([a.reshape(N_DEV, -1) for a in arrays], axis=1)
    return jnp.pad(flat, ((0, 0), (0, rows * LANES - flat.shape[1]))).reshape(N_DEV, rows, LANES)


def _unpack(buf, shapes, lead=()):
    flat = buf.reshape(lead + (-1,))
    out, off = [], 0
    for s in shapes:
        n = 1
        for d in s:
            n *= d
        out.append(flat[..., off:off + n].reshape(lead + tuple(s)))
        off += n
    return out


def _padded_rows(shapes, multiple):
    n = sum(functools.reduce(lambda a, b: a * b, s, 1) for s in shapes)
    rows = -(-n // LANES)
    return -(-rows // multiple) * multiple


def _to_full(stacked, axis):
    t = jnp.moveaxis(stacked, 0, axis)
    return t.reshape(t.shape[:axis] + (t.shape[axis] * t.shape[axis + 1],) + t.shape[axis + 2:])


def _to_stacked(full, axis):
    s = full.shape
    t = full.reshape(s[:axis] + (N_DEV, s[axis] // N_DEV) + s[axis + 1:])
    return jnp.moveaxis(t, axis, 0)


def _ffn_forward(x, norm_g, w_up, cw, cb, w_down, tag):
    h = rms_fwd(x, norm_g, f"ffn{tag}_norm")
    z = mm_in(h, w_up, f"ffn{tag}_up", stacked_out=True)
    nb, m, c = z.shape
    z4 = z.reshape(2, nb // 2, m, c)
    act = ffn_mid_fwd(z4, cw, cb, f"ffn{tag}_mid")
    out = mm_out(act, w_down, x, f"ffn{tag}_down")
    return out, (x, h, z4, act)


def _ffn_backward(dx, saved, norm_g, w_up, cw, cb, w_down, tag, after=()):
    x, h, z4, act = saved
    nj = act.shape[0]
    dact = mm_dx_out(dx, w_down, f"ffn{tag}_down_dx", groups=nj, after=after)
    dw_down = mm_dw_out(act, dx, f"ffn{tag}_down_dw")
    dz4, dcw, dcb = ffn_mid_bwd(z4, dact, cw, cb, f"ffn{tag}_mid_bwd")
    dz = dz4.reshape((2 * nj,) + dz4.shape[2:])
    dh = mm_dx_in(dz, w_up, f"ffn{tag}_up_dx")
    dw_up = mm_dw_in(h, dz, 2 * nj, f"ffn{tag}_up_dw")
    dx, dg = rms_bwd(x, norm_g, dh, dx, f"ffn{tag}_norm_bwd")
    return dx, dict(norm_g=dg, w_up=dw_up, conv_w=dcw, conv_b=dcb, w_down=dw_down)


def kernel(x, a_norm_g, a_w_in, a_b_in, a_v_norm_g, a_w_s, a_b_s, a_w_out, b_norm_g, b_w_in, b_w_grp, b_b_grp, b_scale, b_w_out, c_norm_g, c_w_in, c_b_in, c_conv_w, c_conv_b, c_w_a, c_b_a, c_w_i, c_b_i, c_lambda, c_w_out, d_norm_g, d_w_in, d_conv_w, d_w_out, ffn_norm_g, ffn_w_up, ffn_conv_w, ffn_conv_b, ffn_w_down, final_norm_g, loss_target, m_a_norm_g, m_a_w_in, m_a_b_in, m_a_v_norm_g, m_a_w_s, m_a_b_s, m_a_w_out, m_b_norm_g, m_b_w_in, m_b_w_grp, m_b_b_grp, m_b_scale, m_b_w_out, m_c_norm_g, m_c_w_in, m_c_b_in, m_c_conv_w, m_c_conv_b, m_c_w_a, m_c_b_a, m_c_w_i, m_c_b_i, m_c_lambda, m_c_w_out, m_d_norm_g, m_d_w_in, m_d_conv_w, m_d_w_out, m_ffn_norm_g, m_ffn_w_up, m_ffn_conv_w, m_ffn_conv_b, m_ffn_w_down, m_final_norm_g, v_a_norm_g, v_a_w_in, v_a_b_in, v_a_v_norm_g, v_a_w_s, v_a_b_s, v_a_w_out, v_b_norm_g, v_b_w_in, v_b_w_grp, v_b_b_grp, v_b_scale, v_b_w_out, v_c_norm_g, v_c_w_in, v_c_b_in, v_c_conv_w, v_c_conv_b, v_c_w_a, v_c_b_a, v_c_w_i, v_c_b_i, v_c_lambda, v_c_w_out, v_d_norm_g, v_d_w_in, v_d_conv_w, v_d_w_out, v_ffn_norm_g, v_ffn_w_up, v_ffn_conv_w, v_ffn_conv_b, v_ffn_w_down, v_final_norm_g):
    args = locals()
    w_loc = {n: args[n] for n in WEIGHTS}
    m_loc = {n: args["m_" + n] for n in WEIGHTS}
    v_loc = {n: args["v_" + n] for n in WEIGHTS}
    depth = ffn_w_up.shape[0]
    xs = x[0]
    target = loss_target[0]

    small_names = list(SMALL_SHARDED)
    small_shapes = [w_loc[n].shape for n in small_names]
    small_rows = _padded_rows(small_shapes, 8)
    small_packed = _pack([w_loc[n] for n in small_names], small_rows)
    mixers = [(a_w_in, a_w_out), (b_w_in, b_w_out), (c_w_in, c_w_out), (d_w_in, d_w_out)]
    groups = []
    for l in range(depth):
        groups.append([mixers[l][0][0].astype(BF16), mixers[l][1][0].astype(BF16)] + ([small_packed] if l == 0 else []))
        groups.append([ffn_w_up[l].astype(BF16), ffn_w_down[l].astype(BF16)])
    in_flight, _ = gather_start(groups, "gather_start")

    def rows_full(st):
        return st.reshape((st.shape[0] * st.shape[1],) + st.shape[2:])

    nb = N_DEV
    ffn_cb = [ffn_conv_b[l].reshape(2, nb // 2, 1, -1) for l in range(depth)]
    ffn_g = [ffn_norm_g[l:l + 1] for l in range(depth)]
    a_bst = a_b_s[0].T
    w_up, w_down, saved = [None] * depth, [None] * depth, {}

    def ffn_forward(xl, l):
        up, down = gather_wait(in_flight[2 * l + 1], (xl,), f"gather_wait_ffn{l}")
        w_up[l], w_down[l] = up, rows_full(down)
        return _ffn_forward(xl, ffn_g[l], w_up[l], ffn_cw[l], ffn_cb[l], w_down[l], l)

    wa_in, wa_out, small_st = gather_wait(in_flight[0], (xs,), "gather_wait_a")
    wa_out = rows_full(wa_out)
    sm = {n: _to_full(s, SMALL_SHARDED[n]) for n, s in zip(small_names, _unpack(small_st, small_shapes, (N_DEV,)))}
    ffn_cw = [sm['ffn_conv_w'][l].reshape(ffn_conv_w.shape[1], 2, nb // 2, -1).transpose(1, 2, 0, 3) for l in range(depth)]
    b_wgrp, c_wa, c_wi = (sm[n][0].astype(BF16) for n in ('b_w_grp', 'c_w_a', 'c_w_i'))
    b_bgrp, c_ba, c_bi = (sm[n][0].reshape(1, -1) for n in ('b_b_grp', 'c_b_a', 'c_b_i'))
    h = rms_fwd(xs, a_norm_g, "a_norm")
    z = mm_in(h, wa_in, "a_in", bias=a_b_in)
    y = a_mid_fwd(z, a_v_norm_g, a_w_s[0], a_bst, "a_mid")
    x1 = mm_out(y, wa_out, xs, "a_out")
    saved['a'] = (xs, h, z, y)
    x1, saved['f0'] = ffn_forward(x1, 0)

    wb_in, wb_out = gather_wait(in_flight[2], (x1,), "gather_wait_b")
    wb_in, wb_out = rows_full(wb_in)[None], rows_full(wb_out)
    h = rms_fwd(x1, sm['b_norm_g'], "b_norm")
    z = mm_in(h, wb_in, "b_in")
    y = b_mid_fwd(z, b_wgrp, b_bgrp, sm['b_scale'], "b_mid")
    x2 = mm_out(y, wb_out, x1, "b_out")
    saved['b'] = (x1, h, z, y)
    x2, saved['f1'] = ffn_forward(x2, 1)

    wc_in, wc_out = gather_wait(in_flight[4], (x2,), "gather_wait_c")
    wc_out = rows_full(wc_out)
    h = rms_fwd(x2, sm['c_norm_g'], "c_norm")
    z = mm_in(h, wc_in, "c_in", bias=sm['c_b_in'])
    c_cw = sm['c_conv_w'][0]
    a_seq, b_seq, xr = c_mid_fwd(z, c_cw, sm['c_conv_b'], c_wa, c_ba, c_wi, c_bi, sm['c_lambda'], "c_mid")
    hs, y = c_scan_fwd(a_seq, b_seq, z, "c_scan")
    x3 = mm_out(y, wc_out, x2, "c_out")
    saved['c'] = (x2, h, z, y, a_seq, xr, hs)
    x3, saved['f2'] = ffn_forward(x3, 2)

    wd_in, wd_out = gather_wait(in_flight[6], (x3,), "gather_wait_d")
    wd_out = rows_full(wd_out)
    h = rms_fwd(x3, sm['d_norm_g'], "d_norm")
    z = mm_in(h, wd_in, "d_in")
    d_cw = sm['d_conv_w'][0]
    y = d_mid_fwd(z, d_cw, "d_mid")
    x4 = mm_out(y, wd_out, x3, "d_out")
    saved['d'] = (x3, h, z, y)
    x4, saved['f3'] = ffn_forward(x4, 3)

    loss_part, dx, d_final_g = final_loss(x4, final_norm_g.reshape(1, -1), target, "final_loss")
    loss = lax.psum(loss_part[0, 0], ("x", "y", "c"))

    def rows_stacked(full):
        return full.reshape((N_DEV, full.shape[0] // N_DEV) + full.shape[1:])

    mx, my, mc = _place()
    own = jnp.stack([_dev_index(*_chip_of(mx, my, k), mc) for k in range(4)]).astype(jnp.int32)
    repl_shapes = [w_loc[n].shape for n in REPLICATED]
    repl_rows = _padded_rows(repl_shapes, 8 * N_DEV)
    shard_of = {n: (w_loc[n][0], m_loc[n][0], v_loc[n][0])
                for n in ('a_w_in', 'a_w_out', 'b_w_in', 'b_w_out', 'c_w_in', 'c_w_out', 'd_w_in', 'd_w_out')}
    for l in range(depth):
        for n in ('ffn_w_up', 'ffn_w_down'):
            shard_of[f'{n}{l}'] = (w_loc[n][l], m_loc[n][l], v_loc[n][l])
    shard_of['small'] = (small_packed, _pack([m_loc[n] for n in small_names], small_rows),
                         _pack([v_loc[n] for n in small_names], small_rows))
    updated = {}

    def finish(n, part, others):
        if n == 'repl':
            chunk = sum_parts(part, others, "rs_sum_repl")
            repl_g = all_gather([chunk], "gather_repl")[0].reshape(repl_rows, LANES)
            updated[n] = adamw(*(_pack([src[k] for k in REPLICATED], repl_rows) for src in (w_loc, m_loc, v_loc)),
                               "adamw_repl", g=repl_g)
        else:
            updated[n] = adamw(*shard_of[n], f"adamw_{n}", p=part, l2=others)
        return updated[n][1]

    stages = [None, None]

    def advance(tag, new, after):
        behind = []
        first = None
        if new:
            first = ([n for n, _ in new], sibling_start([g for _, g in new], f"rs_sibling_start{tag}"))
            behind.append(first[1][3])
        second = None
        if stages[0] is not None:
            names, started = stages[0]
            grads, got = sibling_wait(started, after, f"rs_sibling_wait{tag}")
            parts = [add_pairs(g, l, own, f"rs_add_{n}") for n, g, l in zip(names, grads, got)]
            second = (names, chips_start(parts, f"rs_chips_start{tag}"))
            behind.append(second[1][3])
        if stages[1] is not None:
            names, started = stages[1]
            parts, others = chips_wait(started, after, f"rs_chips_wait{tag}")
            behind += [finish(n, p, o) for n, p, o in zip(names, parts, others)]
        stages[:] = [first, second]
        return tuple(behind)

    gf = [None] * depth
    dx, gf[3] = _ffn_backward(dx, saved['f3'], ffn_g[3], w_up[3], ffn_cw[3], ffn_cb[3], w_down[3], 3)
    xin, h, z, y = saved['d']
    dy = mm_dx_out(dx, wd_out, "d_out_dx")
    g_d_w_out = mm_dw_out(y, dx, "d_out_dw")
    dz, g_d_conv_w = d_mid_bwd(z, dy, d_cw, "d_mid_bwd")
    dh = mm_dx_in(dz, wd_in, "d_in_dx")
    g_d_w_in = mm_dw_in(h, dz, nb, "d_in_dw")
    dx, g_d_norm_g = rms_bwd(xin, sm['d_norm_g'], dh, dx, "d_norm_bwd")
    behind = advance(0, [('ffn_w_up3', gf[3]['w_up']), ('ffn_w_down3', rows_stacked(gf[3]['w_down'])), ('d_w_in', g_d_w_in),
                         ('d_w_out', rows_stacked(g_d_w_out))], (dx,))

    dx, gf[2] = _ffn_backward(dx, saved['f2'], ffn_g[2], w_up[2], ffn_cw[2], ffn_cb[2], w_down[2], 2, behind)
    xin, h, z, y, a_seq, xr, hs = saved['c']
    dy = mm_dx_out(dx, wc_out, "c_out_dx")
    g_c_w_out = mm_dw_out(y, dx, "c_out_dw")
    lam_seq, da_seq, dgate, dgate_sum = c_scan_bwd(dy, z, hs, a_seq, "c_scan_bwd")
    dxr, g_c_w_a, g_c_w_i, g_c_b_a, g_c_b_i, g_c_lambda = c_mid_bwd(
        lam_seq, da_seq, xr, c_wa, c_ba, c_wi, c_bi, sm['c_lambda'], "c_mid_bwd")
    dxr_pre, g_c_conv_w, g_c_conv_b, dxr_pre_sum = conv_bwd(dxr, z, 1, c_cw, "c_conv_bwd")
    dz = jnp.concatenate([dgate, dxr_pre], axis=1)
    g_c_b_in = jnp.concatenate([dgate_sum, dxr_pre_sum], axis=1)
    dh = mm_dx_in(dz, wc_in, "c_in_dx")
    g_c_w_in = mm_dw_in(h, dz, nb, "c_in_dw")
    dx, g_c_norm_g = rms_bwd(xin, sm['c_norm_g'], dh, dx, "c_norm_bwd")
    behind = advance(1, [('ffn_w_up2', gf[2]['w_up']), ('ffn_w_down2', rows_stacked(gf[2]['w_down'])), ('c_w_in', g_c_w_in),
                         ('c_w_out', rows_stacked(g_c_w_out))], (dx,))

    dx, gf[1] = _ffn_backward(dx, saved['f1'], ffn_g[1], w_up[1], ffn_cw[1], ffn_cb[1], w_down[1], 1, behind)
    xin, h, z, y = saved['b']
    dy = mm_dx_out(dx, wb_out, "b_out_dx")
    g_b_w_out = mm_dw_out(y, dx, "b_out_dw")
    dp, g_b_w_grp, g_b_b_grp, g_b_scale = b_mid_bwd(z, dy, b_wgrp, b_bgrp, sm['b_scale'], "b_mid_bwd")
    dz = b_pool_bwd(dp, "b_pool_bwd")
    dh = mm_dx_in(dz, wb_in, "b_in_dx")
    g_b_w_in = mm_dw_in(h, dz, 1, "b_in_dw")
    dx, g_b_norm_g = rms_bwd(xin, sm['b_norm_g'], dh, dx, "b_norm_bwd")
    behind = advance(2, [('ffn_w_up1', gf[1]['w_up']), ('ffn_w_down1', rows_stacked(gf[1]['w_down'])),
                         ('b_w_in', rows_stacked(g_b_w_in[0])), ('b_w_out', rows_stacked(g_b_w_out))], (dx,))

    dx, gf[0] = _ffn_backward(dx, saved['f0'], ffn_g[0], w_up[0], ffn_cw[0], ffn_cb[0], w_down[0], 0, behind)
    behind = advance(3, [('ffn_w_up0', gf[0]['w_up']), ('ffn_w_down0', rows_stacked(gf[0]['w_down']))], (dx,))
    xin, h, z, y = saved['a']
    dy = mm_dx_out(dx, wa_out, "a_out_dx", after=behind)
    g_a_w_out = mm_dw_out(y, dx, "a_out_dw")
    dz, g_a_b_in, g_a_v_norm_g, g_a_w_s, g_a_b_s = a_mid_bwd(z, dy, a_v_norm_g, a_w_s[0], a_bst, "a_mid_bwd")
    dh = mm_dx_in(dz, wa_in, "a_in_dx")
    g_a_w_in = mm_dw_in(h, dz, nb, "a_in_dw")
    dx, g_a_norm_g = rms_bwd(xin, a_norm_g, dh, dx, "a_norm_bwd")
    grad_x = dx[None]

    tril = jnp.tril(jnp.ones((A_CHUNK, A_CHUNK), bool))
    full_small = {
        'b_norm_g': g_b_norm_g, 'b_w_grp': g_b_w_grp[None], 'b_b_grp': g_b_b_grp.reshape(b_b_grp.shape[:2] + (-1,)),
        'b_scale': g_b_scale, 'c_norm_g': g_c_norm_g, 'c_b_in': g_c_b_in, 'c_conv_w': g_c_conv_w[None],
        'c_conv_b': g_c_conv_b, 'c_w_a': g_c_w_a[None], 'c_b_a': g_c_b_a.reshape(c_b_a.shape[:2] + (-1,)),
        'c_w_i': g_c_w_i[None], 'c_b_i': g_c_b_i.reshape(c_b_i.shape[:2] + (-1,)), 'c_lambda': g_c_lambda,
        'd_norm_g': g_d_norm_g, 'd_conv_w': g_d_conv_w[None],
        'ffn_conv_w': jnp.stack([gf[l]['conv_w'].transpose(2, 0, 1, 3).reshape(ffn_conv_w.shape[1], -1) for l in range(depth)])}
    small_grads = _pack_stacked([_to_stacked(full_small[n], SMALL_SHARDED[n]) for n in small_names], small_rows)
    repl_full = {
        'a_norm_g': g_a_norm_g, 'a_b_in': g_a_b_in, 'a_v_norm_g': g_a_v_norm_g,
        'a_w_s': jnp.where(tril, g_a_w_s, 0.0)[None], 'a_b_s': g_a_b_s[:, ::LANES].T[None],
        'ffn_norm_g': jnp.concatenate([gf[l]['norm_g'] for l in range(depth)], axis=0),
        'ffn_conv_b': jnp.stack([gf[l]['conv_b'].reshape(-1) for l in range(depth)]), 'final_norm_g': d_final_g.reshape(-1)}
    repl_grads = _pack([repl_full[n] for n in REPLICATED], repl_rows).reshape(N_DEV, repl_rows // N_DEV, LANES)
    behind = advance(4, [('a_w_in', g_a_w_in), ('a_w_out', rows_stacked(g_a_w_out)), ('small', small_grads),
                         ('repl', repl_grads)], (dx,))
    behind = advance(5, [], (dx, *behind))
    advance(6, [], (dx, *behind))

    outs = [{}, {}, {}, {}]
    for i, dst in enumerate(outs):
        for n in ('a_w_in', 'a_w_out', 'b_w_in', 'b_w_out', 'c_w_in', 'c_w_out', 'd_w_in', 'd_w_out'):
            dst[n] = updated[n][i][None]
        for n in ('ffn_w_up', 'ffn_w_down'):
            dst[n] = jnp.stack([updated[f'{n}{l}'][i] for l in range(depth)])
        dst.update(zip(small_names, _unpack(updated['small'][i], small_shapes)))
        dst.update(zip(REPLICATED, _unpack(updated['repl'][i], repl_shapes)))
    out_g, out_d, out_m, out_v = outs

    return (loss, grad_x, *[out_g[n] for n in WEIGHTS], *[out_d[n] for n in WEIGHTS], *[out_m[n] for n in WEIGHTS],
            *[out_v[n] for n in WEIGHTS])
```

```python
import functools

import jax
import jax.numpy as jnp
from jax import lax
from jax.experimental import pallas as pl
from jax.experimental.pallas import tpu as pltpu

F32, BF16 = jnp.float32, jnp.bfloat16
MESH_ID = pl.DeviceIdType.MESH
N_DEV = 8
V7X_VMEM_LIMIT_BYTES = 56 << 20
LANES = 128
HALO = 8
POOL_HALO = 16

EPS = 1e-6
A_CHUNK, A_GROUPS = 128, 4
B_WINDOWS = (2, 4, 8, 16)
C_GATE_C = 8.0
ADAM_LR, ADAM_B1, ADAM_B2, ADAM_EPS, ADAM_WD, ADAM_STEP = 0.001, 0.9, 0.999, 1e-08, 0.01, 10

WEIGHTS = ['a_norm_g', 'a_w_in', 'a_b_in', 'a_v_norm_g', 'a_w_s', 'a_b_s', 'a_w_out', 'b_norm_g', 'b_w_in', 'b_w_grp',
           'b_b_grp', 'b_scale', 'b_w_out', 'c_norm_g', 'c_w_in', 'c_b_in', 'c_conv_w', 'c_conv_b', 'c_w_a', 'c_b_a',
           'c_w_i', 'c_b_i', 'c_lambda', 'c_w_out', 'd_norm_g', 'd_w_in', 'd_conv_w', 'd_w_out', 'ffn_norm_g',
           'ffn_w_up', 'ffn_conv_w', 'ffn_conv_b', 'ffn_w_down', 'final_norm_g']
SMALL_SHARDED = {'b_norm_g': 1, 'b_w_grp': 2, 'b_b_grp': 2, 'b_scale': 1, 'c_norm_g': 1, 'c_b_in': 1, 'c_conv_w': 2,
                 'c_conv_b': 1, 'c_w_a': 2, 'c_b_a': 2, 'c_w_i': 2, 'c_b_i': 2, 'c_lambda': 1, 'd_norm_g': 1,
                 'd_conv_w': 2, 'ffn_conv_w': 2}
REPLICATED = ['a_norm_g', 'a_b_in', 'a_v_norm_g', 'a_w_s', 'a_b_s', 'ffn_norm_g', 'ffn_conv_b', 'final_norm_g']


_GELU_C0, _GELU_C1 = 0.7978845608028654, 0.044715


def _gelu(x):
    return 0.5 * x * (1.0 + jnp.tanh(_GELU_C0 * (x + _GELU_C1 * (x * x * x))))


def _gelu_grad(x):
    t = jnp.tanh(_GELU_C0 * (x + _GELU_C1 * (x * x * x)))
    return 0.5 * (1.0 + t) + 0.5 * x * (1.0 - t * t) * (_GELU_C0 * (1.0 + 3.0 * _GELU_C1 * (x * x)))


def _sigmoid(x):
    return jax.nn.sigmoid(x)


def _log1p(x):
    u = 1.0 + x
    return jnp.where(u == 1.0, x, jnp.log(u) * (x / (u - 1.0)))


def _softplus(x):
    return jnp.maximum(x, 0.0) + _log1p(jnp.exp(-jnp.abs(x)))


def _expm1(x):
    poly = x * (1.0 + x * (1 / 2) * (1.0 + x * (1 / 3) * (1.0 + x * (1 / 4) * (1.0 + x * (1 / 5) * (
        1.0 + x * (1 / 6) * (1.0 + x * (1 / 7) * (1.0 + x * (1 / 8))))))))
    return jnp.where(jnp.abs(x) < 0.35, poly, jnp.exp(x) - 1.0)


def _down(xe, s):
    return xe if s == 0 else pltpu.roll(xe, s, 0)


def _up(xe, s):
    return xe if s == 0 else pltpu.roll(xe, xe.shape[0] - s, 0)


def _conv_ext(xe, w, taps):
    y = xe * w[taps - 1:taps]
    for s in range(1, taps):
        y = y + _down(xe, s) * w[taps - 1 - s:taps - s]
    return y


def _acc(ref, val, first):
    @pl.when(first)
    def _():
        ref[...] = val

    @pl.when(jnp.logical_not(first))
    def _():
        ref[...] += val


def _colsum(v):
    return jnp.sum(v, axis=0, keepdims=True)


def _dot(a, b, dims=((1,), (0,))):
    return lax.dot_general(a.astype(BF16), b.astype(BF16), (dims, ((), ())), preferred_element_type=F32)


_NN, _NT, _TN = ((1,), (0,)), ((1,), (1,)), ((0,), (0,))


def _call(body, name, grid, in_specs, out_specs, out_shape, scratch=(), after=()):
    n_in, n_after = len(in_specs), len(after)

    def ordered_body(*refs):
        return body(*refs[:n_in], *refs[n_in + n_after:])

    call = pl.pallas_call(
        ordered_body if n_after else body, name=name, grid=grid,
        in_specs=list(in_specs) + [pl.BlockSpec(memory_space=pl.ANY)] * n_after, out_specs=out_specs,
        out_shape=out_shape, scratch_shapes=list(scratch),
        compiler_params=pltpu.CompilerParams(dimension_semantics=("arbitrary",) * len(grid),
                                             vmem_limit_bytes=V7X_VMEM_LIMIT_BYTES))
    return lambda *args: call(*args, *after)


def _rows(m, t):
    t = min(m, t)
    assert m % t == 0, (m, t)
    return t


def _sds(shape, dtype=F32):
    return jax.ShapeDtypeStruct(tuple(shape), dtype)


def _prev_halo(tm, halo=HALO):
    return lambda i: jnp.maximum(i * (tm // halo) - 1, 0)


def _next_halo(tm, m, halo=HALO):
    return lambda i: jnp.minimum((i + 1) * (tm // halo), m // halo - 1)


def _matmul(name, ins, in_specs, out_shape, o_spec, grid, compute, after=()):
    def body(*refs):
        refs[-1][...] = compute(*refs[:-1]).astype(refs[-1].dtype)

    return _call(body, name, grid, in_specs, o_spec, out_shape, (), after)(*ins)


def mm_in(h, w_st, name, bias=None, stacked_out=False):
    m, k = h.shape
    nb, _, n = w_st.shape
    tm = _rows(m, 1024)
    in_specs = [pl.BlockSpec((tm, k), lambda i, j: (i, 0)), pl.BlockSpec((None, k, n), lambda i, j: (j, 0, 0))]
    if stacked_out:
        out, o_spec = _sds((nb, m, n)), pl.BlockSpec((None, tm, n), lambda i, j: (j, i, 0))
    else:
        out, o_spec = _sds((m, nb * n)), pl.BlockSpec((tm, n), lambda i, j: (i, j))
    if bias is None:
        return _matmul(name, (h, w_st), in_specs, out, o_spec, (m // tm, nb), lambda a, b: _dot(a[...], b[...]))
    in_specs.append(pl.BlockSpec((1, n), lambda i, j: (0, j)))
    return _matmul(name, (h, w_st, bias), in_specs, out, o_spec, (m // tm, nb),
                   lambda a, b, c: _dot(a[...], b[...]) + c[...])


def _split_rows(kf):
    g = max(1, kf // 1024)
    return g, kf // g


def mm_out(y, w, res, name):
    kf, n = w.shape
    m = y.shape[-2]
    tm = _rows(m, 512)
    row = pl.BlockSpec((tm, n), lambda i: (i, 0))
    if y.ndim == 3:
        g, _, k = y.shape
        ys = [y] * g
        y_specs = [pl.BlockSpec((None, tm, k), lambda i, r=r: (r, i, 0)) for r in range(g)]
    else:
        g, k = 1, kf
        ys, y_specs = [y], [pl.BlockSpec((tm, kf), lambda i: (i, 0))]

    def compute(*refs):
        w_ref, res_ref = refs[g], refs[g + 1]
        acc = res_ref[...]
        for r in range(g):
            acc = acc + _dot(refs[r][...], w_ref[r])
        return acc

    in_specs = y_specs + [pl.BlockSpec((g, k, n), lambda i: (0, 0, 0)), row]
    return _matmul(name, (*ys, w.reshape(g, k, n), res), in_specs, _sds((m, n)), row, (m // tm,), compute)


def mm_dx_in(dz, w_st, name, after=()):
    nb, k, n = w_st.shape
    m = dz.shape[-2]
    tm = _rows(m, 512)
    w_spec = pl.BlockSpec((nb, k, n), lambda i: (0, 0, 0))
    if dz.ndim == 3:
        in_specs = [pl.BlockSpec((None, tm, n), lambda i, r=r: (r, i, 0)) for r in range(nb)] + [w_spec]

        def compute(*refs):
            acc = _dot(refs[0][...], refs[nb][0], _NT)
            for r in range(1, nb):
                acc = acc + _dot(refs[r][...], refs[nb][r], _NT)
            return acc

        ins = (*[dz] * nb, w_st)
    else:
        in_specs = [pl.BlockSpec((tm, nb * n), lambda i: (i, 0)), w_spec]

        def compute(dz_ref, w_ref):
            acc = _dot(dz_ref[:, :n], w_ref[0], _NT)
            for r in range(1, nb):
                acc = acc + _dot(dz_ref[:, r * n:(r + 1) * n], w_ref[r], _NT)
            return acc

        ins = (dz, w_st)
    return _matmul(name, ins, in_specs, _sds((m, k)), pl.BlockSpec((tm, k), lambda i: (i, 0)), (m // tm,), compute, after)


def mm_dx_out(dout, w, name, groups=None, after=()):
    kf, n = w.shape
    m = dout.shape[0]
    tm = _rows(m, 1024)
    g, k = (groups, kf // groups) if groups else _split_rows(kf)
    in_specs = [pl.BlockSpec((tm, n), lambda i, j: (i, 0)), pl.BlockSpec((None, k, n), lambda i, j: (j, 0, 0))]
    if groups:
        out, o_spec = _sds((g, m, k)), pl.BlockSpec((None, tm, k), lambda i, j: (j, i, 0))
    else:
        out, o_spec = _sds((m, kf)), pl.BlockSpec((tm, k), lambda i, j: (i, j))
    return _matmul(name, (dout, w.reshape(g, k, n)), in_specs, out, o_spec, (m // tm, g),
                   lambda a, b: _dot(a[...], b[...], _NT), after)


def mm_dw_in(h, dz, nb, name):
    m, k = h.shape
    if dz.ndim == 3:
        n = dz.shape[2]
        dz_spec = pl.BlockSpec((None, m, n), lambda j: (j, 0, 0))
    else:
        n = dz.shape[1] // nb
        dz_spec = pl.BlockSpec((m, n), lambda j: (0, j))
    in_specs = [pl.BlockSpec((m, k), lambda j: (0, 0)), dz_spec]
    return _matmul(name, (h, dz), in_specs, _sds((nb, k, n), BF16), pl.BlockSpec((None, k, n), lambda j: (j, 0, 0)), (nb,),
                   lambda a, b: _dot(a[...], b[...], _TN))


def mm_dw_out(y, dout, name):
    m, n = dout.shape
    if y.ndim == 3:
        g, _, k = y.shape
        y_spec = pl.BlockSpec((None, m, k), lambda j: (j, 0, 0))
    else:
        g, k = _split_rows(y.shape[1])
        y_spec = pl.BlockSpec((m, k), lambda j: (0, j))
    in_specs = [y_spec, pl.BlockSpec((m, n), lambda j: (0, 0))]
    out = _matmul(name, (y, dout), in_specs, _sds((g, k, n), BF16), pl.BlockSpec((None, k, n), lambda j: (j, 0, 0)), (g,),
                  lambda a, b: _dot(a[...], b[...], _TN))
    return out.reshape(g * k, n)


def rms_fwd(x, g, name):
    m, d = x.shape
    tm = _rows(m, 512)

    def body(x_ref, g_ref, o_ref):
        xv = x_ref[...]
        rstd = lax.rsqrt(jnp.mean(xv * xv, axis=-1, keepdims=True) + EPS)
        o_ref[...] = (xv * rstd * g_ref[...]).astype(BF16)

    row = pl.BlockSpec((tm, d), lambda i: (i, 0))
    vec = pl.BlockSpec((1, d), lambda i: (0, 0))
    return _call(body, name, (m // tm,), [row, vec], row, _sds((m, d), BF16))(x, g)


def _rms_bwd_math(xv, g, dh):
    rstd = lax.rsqrt(jnp.mean(xv * xv, axis=-1, keepdims=True) + EPS)
    xhat = xv * rstd
    dxhat = dh * g
    dx = rstd * (dxhat - xhat * jnp.mean(dxhat * xhat, axis=-1, keepdims=True))
    return dx, _colsum(dh * xhat)


def rms_bwd(x, g, dh, dres, name):
    m, d = x.shape
    tm = _rows(m, 512)

    def body(x_ref, g_ref, dh_ref, dr_ref, dx_ref, dxb_ref, dg_ref):
        dx, dg = _rms_bwd_math(x_ref[...], g_ref[...], dh_ref[...])
        dx = dr_ref[...] + dx
        dx_ref[...] = dx
        dxb_ref[...] = dx.astype(BF16)
        _acc(dg_ref, dg, pl.program_id(0) == 0)

    row = pl.BlockSpec((tm, d), lambda i: (i, 0))
    vec = pl.BlockSpec((1, d), lambda i: (0, 0))
    return _call(body, name, (m // tm,), [row, vec, row, row], [row, row, vec],
                 [_sds((m, d)), _sds((m, d), BF16), _sds((1, d))])(x, g, dh, dres)


def final_loss(x, g, target, name):
    m, d = x.shape
    tm = _rows(m, 512)

    def body(x_ref, g_ref, t_ref, l_ref, dx_ref, dxb_ref, dg_ref):
        xv, gv = x_ref[...], g_ref[...]
        rstd = lax.rsqrt(jnp.mean(xv * xv, axis=-1, keepdims=True) + EPS)
        err = xv * rstd * gv - t_ref[...]
        part = 0.5 * jnp.sum(jnp.mean(err * err, axis=-1, keepdims=True), axis=0, keepdims=True)
        dx, dg = _rms_bwd_math(xv, gv, err * (1.0 / d))
        dx_ref[...] = dx
        dxb_ref[...] = dx.astype(BF16)
        first = pl.program_id(0) == 0
        _acc(l_ref, jnp.broadcast_to(part, l_ref.shape), first)
        _acc(dg_ref, dg, first)

    row = pl.BlockSpec((tm, d), lambda i: (i, 0))
    vec = pl.BlockSpec((1, d), lambda i: (0, 0))
    lsp = pl.BlockSpec((1, LANES), lambda i: (0, 0))
    return _call(body, name, (m // tm,), [row, vec, row], [lsp, row, row, vec],
                 [_sds((1, LANES)), _sds((m, d)), _sds((m, d), BF16), _sds((1, d))])(x, g, target)


def _a_common(z_ref, vg_ref, ws_ref, bst_ref, tm, width):
    gw = width // A_GROUPS
    zp = z_ref[...]
    z = _gelu(zp)
    u, v = z[:, :width], z[:, width:]
    rstd = lax.rsqrt(jnp.mean(v * v, axis=-1, keepdims=True) + EPS)
    vhat = v * rstd
    vn = vhat * vg_ref[...]
    t_i = lax.broadcasted_iota(jnp.int32, (A_CHUNK, A_CHUNK), 0)
    s_i = lax.broadcasted_iota(jnp.int32, (A_CHUNK, A_CHUNK), 1)
    wsm = [jnp.where(s_i <= t_i, ws_ref[g], 0.0).astype(BF16) for g in range(A_GROUPS)]
    bst = bst_ref[...]
    return zp, u, rstd, vhat, vn.astype(BF16), wsm, bst, gw


def a_mid_fwd(z, vg, ws, bst, name):
    m, w2 = z.shape
    width = w2 // 2
    tm = _rows(m, 256)

    def body(z_ref, vg_ref, ws_ref, bst_ref, y_ref):
        _, u, _, _, vnb, wsm, bst, gw = _a_common(z_ref, vg_ref, ws_ref, bst_ref, tm, width)
        for c in range(tm // A_CHUNK):
            r0 = c * A_CHUNK
            for g in range(A_GROUPS):
                c0 = g * gw
                vs = _dot(wsm[g], vnb[r0:r0 + A_CHUNK, c0:c0 + gw]) + bst[:, g:g + 1]
                y_ref[r0:r0 + A_CHUNK, c0:c0 + gw] = (u[r0:r0 + A_CHUNK, c0:c0 + gw] * vs).astype(BF16)

    in_specs = [pl.BlockSpec((tm, w2), lambda i: (i, 0)), pl.BlockSpec((1, width), lambda i: (0, 0)),
                pl.BlockSpec((A_GROUPS, A_CHUNK, A_CHUNK), lambda i: (0, 0, 0)),
                pl.BlockSpec((A_CHUNK, A_GROUPS), lambda i: (0, 0))]
    return _call(body, name, (m // tm,), in_specs, pl.BlockSpec((tm, width), lambda i: (i, 0)),
                 _sds((m, width), BF16))(z, vg, ws, bst)


def a_mid_bwd(z, dy, vg, ws, bst, name, after=()):
    m, w2 = z.shape
    width = w2 // 2
    tm = _rows(m, 256)

    def body(z_ref, dy_ref, vg_ref, ws_ref, bst_ref, dz_ref, dbin_ref, dvg_ref, dws_ref, dbs_ref, dvn_scr, du_scr):
        first = pl.program_id(0) == 0
        zp, u, rstd, vhat, vnb, wsm, bst, gw = _a_common(z_ref, vg_ref, ws_ref, bst_ref, tm, width)
        dy = dy_ref[...]
        dws = [jnp.zeros((A_CHUNK, A_CHUNK), F32) for _ in range(A_GROUPS)]
        dbs = [jnp.zeros((A_CHUNK, 1), F32) for _ in range(A_GROUPS)]
        for c in range(tm // A_CHUNK):
            r0 = c * A_CHUNK
            for g in range(A_GROUPS):
                c0 = g * gw
                vn_cg = vnb[r0:r0 + A_CHUNK, c0:c0 + gw]
                vs = _dot(wsm[g], vn_cg) + bst[:, g:g + 1]
                dy_cg = dy[r0:r0 + A_CHUNK, c0:c0 + gw]
                dvs = dy_cg * u[r0:r0 + A_CHUNK, c0:c0 + gw]
                du_scr[r0:r0 + A_CHUNK, c0:c0 + gw] = dy_cg * vs
                dws[g] = dws[g] + _dot(dvs, vn_cg, _NT)
                dbs[g] = dbs[g] + jnp.sum(dvs, axis=1, keepdims=True)
                dvn_scr[r0:r0 + A_CHUNK, c0:c0 + gw] = _dot(wsm[g], dvs, _TN)
        for g in range(A_GROUPS):
            _acc(dws_ref.at[g], dws[g], first)
            _acc(dbs_ref.at[:, g * LANES:(g + 1) * LANES], jnp.broadcast_to(dbs[g], (A_CHUNK, LANES)), first)
        dvn = dvn_scr[...]
        _acc(dvg_ref, _colsum(dvn * vhat), first)
        dvhat = dvn * vg_ref[...]
        dv = rstd * (dvhat - vhat * jnp.mean(dvhat * vhat, axis=-1, keepdims=True))
        gg = _gelu_grad(zp)
        dzu = du_scr[...] * gg[:, :width]
        dzv = dv * gg[:, width:]
        dz_ref[:, :width] = dzu.astype(BF16)
        dz_ref[:, width:] = dzv.astype(BF16)
        _acc(dbin_ref.at[:, :width], _colsum(dzu), first)
        _acc(dbin_ref.at[:, width:], _colsum(dzv), first)

    const2 = lambda i: (0, 0)
    in_specs = [pl.BlockSpec((tm, w2), lambda i: (i, 0)), pl.BlockSpec((tm, width), lambda i: (i, 0)),
                pl.BlockSpec((1, width), const2), pl.BlockSpec((A_GROUPS, A_CHUNK, A_CHUNK), lambda i: (0, 0, 0)),
                pl.BlockSpec((A_CHUNK, A_GROUPS), const2)]
    out_specs = [pl.BlockSpec((tm, w2), lambda i: (i, 0)), pl.BlockSpec((1, w2), const2), pl.BlockSpec((1, width), const2),
                 pl.BlockSpec((A_GROUPS, A_CHUNK, A_CHUNK), lambda i: (0, 0, 0)),
                 pl.BlockSpec((A_CHUNK, A_GROUPS * LANES), const2)]
    out_shape = [_sds((m, w2), BF16), _sds((1, w2)), _sds((1, width)), _sds((A_GROUPS, A_CHUNK, A_CHUNK)),
                 _sds((A_CHUNK, A_GROUPS * LANES))]
    scratch = [pltpu.VMEM((tm, width), F32), pltpu.VMEM((tm, width), F32)]
    return _call(body, name, (m // tm,), in_specs, out_specs, out_shape, scratch, after)(z, dy, vg, ws, bst)


def _pool_minus_id(ze, i, tm, gw):
    pos = i * tm + lax.broadcasted_iota(jnp.int32, (tm, 1), 0)
    out = []
    for gi, win in enumerate(B_WINDOWS):
        s = ze[:, gi * gw:(gi + 1) * gw]
        step = 1
        while step < win:
            s = s + _down(s, step)
            step *= 2
        inv = 1.0 / jnp.minimum(pos + 1, win).astype(F32)
        out.append(s[POOL_HALO:] * inv - ze[POOL_HALO:, gi * gw:(gi + 1) * gw])
    return out


def _b_specs(tm, width):
    return [pl.BlockSpec((POOL_HALO, width), lambda i: (_prev_halo(tm, POOL_HALO)(i), 0)),
            pl.BlockSpec((tm, width), lambda i: (i, 0))]


def b_mid_fwd(z, wgrp, bgrp, scale, name):
    m, width = z.shape
    ng = len(B_WINDOWS)
    gw = width // ng
    tm = _rows(m, 512)

    def body(zp_ref, zm_ref, w_ref, b_ref, s_ref, y_ref):
        i = pl.program_id(0)
        ze = jnp.concatenate([zp_ref[...] * (i > 0).astype(F32), zm_ref[...]], axis=0)
        p = _pool_minus_id(ze, i, tm, gw)
        for g in range(ng):
            cs = slice(g * gw, (g + 1) * gw)
            y = (_dot(p[g], w_ref[g]) + b_ref[:, cs]) * s_ref[:, cs]
            y_ref[:, cs] = y.astype(BF16)

    vec = pl.BlockSpec((1, width), lambda i: (0, 0))
    in_specs = _b_specs(tm, width) + [pl.BlockSpec((ng, gw, gw), lambda i: (0, 0, 0)), vec, vec]
    return _call(body, name, (m // tm,), in_specs, pl.BlockSpec((tm, width), lambda i: (i, 0)),
                 _sds((m, width), BF16))(z, z, wgrp, bgrp, scale)


def b_mid_bwd(z, dy, wgrp, bgrp, scale, name):
    m, width = z.shape
    ng = len(B_WINDOWS)
    gw = width // ng
    tm = _rows(m, 512)

    def body(zp_ref, zm_ref, dy_ref, w_ref, b_ref, s_ref, dp_ref, dw_ref, db_ref, ds_ref):
        i = pl.program_id(0)
        first = i == 0
        ze = jnp.concatenate([zp_ref[...] * (i > 0).astype(F32), zm_ref[...]], axis=0)
        p = _pool_minus_id(ze, i, tm, gw)
        for g in range(ng):
            cs = slice(g * gw, (g + 1) * gw)
            dyg = dy_ref[:, cs]
            ypre = _dot(p[g], w_ref[g]) + b_ref[:, cs]
            dyp = dyg * s_ref[:, cs]
            _acc(ds_ref.at[:, cs], _colsum(dyg * ypre), first)
            _acc(db_ref.at[:, cs], _colsum(dyp), first)
            _acc(dw_ref.at[g], _dot(p[g], dyp, _TN), first)
            dp_ref[:, cs] = _dot(dyp, w_ref[g], _NT)

    vec = pl.BlockSpec((1, width), lambda i: (0, 0))
    row = pl.BlockSpec((tm, width), lambda i: (i, 0))
    wsp = pl.BlockSpec((ng, gw, gw), lambda i: (0, 0, 0))
    return _call(body, name, (m // tm,), _b_specs(tm, width) + [row, wsp, vec, vec], [row, wsp, vec, vec],
                 [_sds((m, width)), _sds((ng, gw, gw)), _sds((1, width)), _sds((1, width))])(z, z, dy, wgrp, bgrp, scale)


def b_pool_bwd(dp, name):
    m, width = dp.shape
    gw = width // len(B_WINDOWS)
    tm = _rows(m, 512)
    n_i = m // tm

    def body(dm_ref, dn_ref, dz_ref):
        i = pl.program_id(0)
        de = jnp.concatenate([dm_ref[...], dn_ref[...] * (i < n_i - 1).astype(F32)], axis=0)
        pos = i * tm + lax.broadcasted_iota(jnp.int32, (tm + POOL_HALO, 1), 0)
        for gi, win in enumerate(B_WINDOWS):
            cs = slice(gi * gw, (gi + 1) * gw)
            d = de[:, cs]
            s = d * (1.0 / jnp.minimum(pos + 1, win).astype(F32))
            step = 1
            while step < win:
                s = s + _up(s, step)
                step *= 2
            dz_ref[:, cs] = (s[:tm] - d[:tm]).astype(BF16)

    in_specs = [pl.BlockSpec((tm, width), lambda i: (i, 0)),
                pl.BlockSpec((POOL_HALO, width), lambda i: (_next_halo(tm, m, POOL_HALO)(i), 0))]
    return _call(body, name, (n_i,), in_specs, pl.BlockSpec((tm, width), lambda i: (i, 0)), _sds((m, width), BF16))(dp, dp)


def _c_gates(xr, wa_ref, ba_ref, wi_ref, bi_ref, lam_ref, heads, hw):
    xb = xr.astype(BF16)
    ra = jnp.concatenate([_dot(xb[:, h * hw:(h + 1) * hw], wa_ref[h]) for h in range(heads)], axis=1) + ba_ref[...]
    ia = jnp.concatenate([_dot(xb[:, h * hw:(h + 1) * hw], wi_ref[h]) for h in range(heads)], axis=1) + bi_ref[...]
    r, ig = _sigmoid(ra), _sigmoid(ia)
    sp = _softplus(-lam_ref[...])
    log_a = (-C_GATE_C * r) * sp
    a = jnp.exp(log_a)
    mult = jnp.sqrt(-_expm1(2.0 * log_a))
    return xb, r, ig, sp, a, mult


def c_mid_fwd(z, cw, cb, wa, ba, wi, bi, lam, name):
    m, w2 = z.shape
    width = w2 // 2
    heads, hw = wa.shape[0], wa.shape[1]
    taps = cw.shape[0]
    tm = _rows(m, 512)

    def body(zp_ref, zm_ref, cw_ref, cb_ref, wa_ref, ba_ref, wi_ref, bi_ref, lam_ref, a_ref, b_ref, xr_ref):
        i = pl.program_id(0)
        xe = jnp.concatenate([zp_ref[...] * (i > 0).astype(F32), zm_ref[...]], axis=0)
        xr = _conv_ext(xe, cw_ref[...], taps)[HALO:] + cb_ref[...]
        _, _, ig, _, a, mult = _c_gates(xr, wa_ref, ba_ref, wi_ref, bi_ref, lam_ref, heads, hw)
        a_ref[...] = a
        b_ref[...] = mult * (ig * xr)
        xr_ref[...] = xr

    vec = pl.BlockSpec((1, width), lambda i: (0, 0))
    row = pl.BlockSpec((tm, width), lambda i: (i, 0))
    wsp = pl.BlockSpec((heads, hw, hw), lambda i: (0, 0, 0))
    in_specs = [pl.BlockSpec((HALO, width), lambda i: (_prev_halo(tm)(i), 1)), pl.BlockSpec((tm, width), lambda i: (i, 1)),
                pl.BlockSpec((taps, width), lambda i: (0, 0)), vec, wsp, vec, wsp, vec, vec]
    return _call(body, name, (m // tm,), in_specs, [row, row, row], [_sds((m, width))] * 3)(
        z, z, cw, cb, wa, ba, wi, bi, lam)


_SCAN_ROWS = 512


def c_scan_fwd(a, b, z, name):
    m, width = a.shape
    tm = _rows(m, _SCAN_ROWS)

    def body(a_ref, b_ref, g_ref, hs_ref, y_ref, h_carry):
        @pl.when(pl.program_id(0) == 0)
        def _():
            h_carry[...] = jnp.zeros_like(h_carry)

        def step(t, h):
            h = a_ref[pl.ds(t, 1), :] * h + b_ref[pl.ds(t, 1), :]
            hs_ref[pl.ds(t, 1), :] = h
            return h

        h_carry[...] = lax.fori_loop(0, tm, step, h_carry[...], unroll=8)
        y_ref[...] = (hs_ref[...] * _gelu(g_ref[...])).astype(BF16)

    row = pl.BlockSpec((tm, width), lambda i: (i, 0))
    return _call(body, name, (m // tm,), [row, row, row], [row, row], [_sds((m, width)), _sds((m, width), BF16)],
                 [pltpu.VMEM((1, width), F32)])(a, b, z)


def c_scan_bwd(dy, z, hs, a, name):
    m, width = a.shape
    tm = _rows(m, _SCAN_ROWS)
    n_i = m // tm

    def body(dy_ref, g_ref, hs_ref, hp_ref, a_ref, lam_ref, da_ref, dg_ref, dgs_ref, lam_carry, a_carry):
        i = pl.program_id(0)
        first = i == 0

        @pl.when(first)
        def _():
            lam_carry[...] = jnp.zeros_like(lam_carry)
            a_carry[...] = jnp.zeros_like(a_carry)

        gp, dyv, hsv = g_ref[...], dy_ref[...], hs_ref[...]
        dgate = dyv * hsv * _gelu_grad(gp)
        dg_ref[...] = dgate.astype(BF16)
        _acc(dgs_ref, _colsum(dgate), first)
        lam_ref[...] = dyv * _gelu(gp)

        def step(k, carry):
            lam_next, a_next = carry
            t = tm - 1 - k
            lam_t = lam_ref[pl.ds(t, 1), :] + a_next * lam_next
            lam_ref[pl.ds(t, 1), :] = lam_t
            return lam_t, a_ref[pl.ds(t, 1), :]

        lam_c, a_c = lax.fori_loop(0, tm, step, (lam_carry[...], a_carry[...]), unroll=8)
        lam_carry[...] = lam_c
        a_carry[...] = a_c
        h_before = hp_ref[HALO - 1:HALO, :] * (i < n_i - 1).astype(F32)
        t_i = lax.broadcasted_iota(jnp.int32, (tm, 1), 0)
        da_ref[...] = lam_ref[...] * jnp.where(t_i == 0, h_before, _down(hsv, 1))

    row = pl.BlockSpec((tm, width), lambda i: (n_i - 1 - i, 0))
    halo = pl.BlockSpec((HALO, width), lambda i: (_prev_halo(tm)(n_i - 1 - i), 0))
    vec = pl.BlockSpec((1, width), lambda i: (0, 0))
    return _call(body, name, (n_i,), [row, row, row, halo, row], [row, row, row, vec],
                 [_sds((m, width)), _sds((m, width)), _sds((m, width), BF16), _sds((1, width))],
                 [pltpu.VMEM((1, width), F32), pltpu.VMEM((1, width), F32)])(dy, z, hs, hs, a)


def c_mid_bwd(lam_seq, da, xr, wa, ba, wi, bi, lam, name):
    m, width = xr.shape
    heads, hw = wa.shape[0], wa.shape[1]
    tm = _rows(m, 512)

    def body(l_ref, da_ref, xr_ref, wa_ref, ba_ref, wi_ref, bi_ref, lam_ref,
             dxr_ref, dwa_ref, dwi_ref, dba_ref, dbi_ref, dlam_ref):
        first = pl.program_id(0) == 0
        xr_v, lmb = xr_ref[...], l_ref[...]
        xb, r, ig, sp, a, mult = _c_gates(xr_v, wa_ref, ba_ref, wi_ref, bi_ref, lam_ref, heads, hw)
        dmult = lmb * (ig * xr_v)
        dig = lmb * mult * xr_v
        dxr = lmb * mult * ig
        dla = da_ref[...] * a - dmult * (a * a) / mult
        dr = dla * (-C_GATE_C * sp)
        dsp = _colsum(dla * (-C_GATE_C * r))
        _acc(dlam_ref, dsp * (-_sigmoid(-lam_ref[...])), first)
        dra = dr * r * (1.0 - r)
        dia = dig * ig * (1.0 - ig)
        _acc(dba_ref, _colsum(dra), first)
        _acc(dbi_ref, _colsum(dia), first)
        for h in range(heads):
            cs = slice(h * hw, (h + 1) * hw)
            _acc(dwa_ref.at[h], _dot(xb[:, cs], dra[:, cs], _TN), first)
            _acc(dwi_ref.at[h], _dot(xb[:, cs], dia[:, cs], _TN), first)
            dxr_ref[:, cs] = dxr[:, cs] + _dot(dra[:, cs], wa_ref[h], _NT) + _dot(dia[:, cs], wi_ref[h], _NT)

    vec = pl.BlockSpec((1, width), lambda i: (0, 0))
    row = pl.BlockSpec((tm, width), lambda i: (i, 0))
    wsp = pl.BlockSpec((heads, hw, hw), lambda i: (0, 0, 0))
    return _call(body, name, (m // tm,), [row, row, row, wsp, vec, wsp, vec, vec], [row, wsp, wsp, vec, vec, vec],
                 [_sds((m, width)), _sds((heads, hw, hw)), _sds((heads, hw, hw)), _sds((1, width)), _sds((1, width)),
                  _sds((1, width))])(lam_seq, da, xr, wa, ba, wi, bi, lam)


def conv_bwd(dy, x_src, col_block, cw, name):
    m, width = dy.shape
    taps = cw.shape[0]
    tm = _rows(m, 512)
    n_i = m // tm

    def body(dm_ref, dn_ref, xp_ref, xm_ref, cw_ref, dx_ref, dw_ref, db_ref, dxs_ref):
        i = pl.program_id(0)
        first = i == 0
        de = jnp.concatenate([jnp.zeros((HALO, width), F32), dm_ref[...], dn_ref[...] * (i < n_i - 1).astype(F32)], axis=0)
        xe = jnp.concatenate([xp_ref[...] * (i > 0).astype(F32), xm_ref[...], jnp.zeros((HALO, width), F32)], axis=0)
        w = cw_ref[...]
        dx = de * w[taps - 1:taps]
        for s in range(1, taps):
            dx = dx + _up(de, s) * w[taps - 1 - s:taps - s]
        dx_ref[...] = dx[HALO:HALO + tm].astype(BF16)
        _acc(dxs_ref, _colsum(dx[HALO:HALO + tm]), first)
        dm = dm_ref[...]
        for s in range(taps):
            _acc(dw_ref.at[taps - 1 - s:taps - s, :], _colsum(dm * _down(xe, s)[HALO:HALO + tm]), first)
        _acc(db_ref, _colsum(dm), first)

    row = pl.BlockSpec((tm, width), lambda i: (i, 0))
    vec = pl.BlockSpec((1, width), lambda i: (0, 0))
    tsp = pl.BlockSpec((taps, width), lambda i: (0, 0))
    in_specs = [row, pl.BlockSpec((HALO, width), lambda i: (_next_halo(tm, m)(i), 0)),
                pl.BlockSpec((HALO, width), lambda i: (_prev_halo(tm)(i), col_block)),
                pl.BlockSpec((tm, width), lambda i: (i, col_block)), tsp]
    return _call(body, name, (n_i,), in_specs, [row, tsp, vec, vec],
                 [_sds((m, width), BF16), _sds((taps, width)), _sds((1, width)), _sds((1, width))])(dy, dy, x_src, x_src, cw)


def d_mid_fwd(z, cw, name):
    m, w3 = z.shape
    width = w3 // 3
    taps = cw.shape[0]
    tm = _rows(m, 512)

    def body(bm_ref, cp_ref, cm_ref, xp_ref, xm_ref, cw_ref, y_ref):
        keep = (pl.program_id(0) > 0).astype(F32)
        qe = (jnp.concatenate([cp_ref[...] * keep, cm_ref[...]], axis=0)
              * jnp.concatenate([xp_ref[...], xm_ref[...]], axis=0))
        y_ref[...] = (bm_ref[...] * _conv_ext(qe, cw_ref[...], taps)[HALO:]).astype(BF16)

    main = lambda c: pl.BlockSpec((tm, width), lambda i: (i, c))
    prev = lambda c: pl.BlockSpec((HALO, width), lambda i: (_prev_halo(tm)(i), c))
    in_specs = [main(0), prev(1), main(1), prev(2), main(2), pl.BlockSpec((taps, width), lambda i: (0, 0))]
    return _call(body, name, (m // tm,), in_specs, pl.BlockSpec((tm, width), lambda i: (i, 0)),
                 _sds((m, width), BF16))(z, z, z, z, z, cw)


def d_mid_bwd(z, dy, cw, name):
    m, w3 = z.shape
    width = w3 // 3
    taps = cw.shape[0]
    tm = _rows(m, 512)
    n_i = m // tm

    def body(bm_ref, bn_ref, cp_ref, cm_ref, cn_ref, xp_ref, xm_ref, xn_ref, dm_ref, dn_ref, cw_ref, dz_ref, dw_ref):
        i = pl.program_id(0)
        first = i == 0
        kp, kn = (i > 0).astype(F32), (i < n_i - 1).astype(F32)
        zeros = jnp.zeros((HALO, width), F32)
        ce = jnp.concatenate([cp_ref[...] * kp, cm_ref[...], cn_ref[...] * kn], axis=0)
        xe = jnp.concatenate([xp_ref[...], xm_ref[...], xn_ref[...]], axis=0)
        qe = ce * xe
        be = jnp.concatenate([zeros, bm_ref[...], bn_ref[...]], axis=0)
        dye = jnp.concatenate([zeros, dm_ref[...], dn_ref[...] * kn], axis=0)
        w = cw_ref[...]
        cq = _conv_ext(qe, w, taps)
        dcq = dye * be
        dq = dcq * w[taps - 1:taps]
        for s in range(1, taps):
            dq = dq + _up(dcq, s) * w[taps - 1 - s:taps - s]
        ms = slice(HALO, HALO + tm)
        dz_ref[:, :width] = (dye * cq)[ms].astype(BF16)
        dz_ref[:, width:2 * width] = (dq * xe)[ms].astype(BF16)
        dz_ref[:, 2 * width:] = (dq * ce)[ms].astype(BF16)
        for s in range(taps):
            _acc(dw_ref.at[taps - 1 - s:taps - s, :], _colsum(dcq[ms] * _down(qe, s)[ms]), first)

    main = lambda c: pl.BlockSpec((tm, width), lambda i: (i, c))
    prev = lambda c: pl.BlockSpec((HALO, width), lambda i: (_prev_halo(tm)(i), c))
    nxt = lambda c: pl.BlockSpec((HALO, width), lambda i: (_next_halo(tm, m)(i), c))
    tsp = pl.BlockSpec((taps, width), lambda i: (0, 0))
    in_specs = [main(0), nxt(0), prev(1), main(1), nxt(1), prev(2), main(2), nxt(2), main(0), nxt(0), tsp]
    return _call(body, name, (n_i,), in_specs, [pl.BlockSpec((tm, w3), lambda i: (i, 0)), tsp],
                 [_sds((m, w3), BF16), _sds((taps, width))])(z, z, z, z, z, z, z, z, dy, dy, cw)


def ffn_mid_fwd(z, cw, cb, name):
    _, nj, m, c = z.shape
    taps = cw.shape[2]
    tm = _rows(m, 512)

    def body(zp_ref, zm_ref, cw_ref, cb_ref, o_ref):
        keep = (pl.program_id(1) > 0).astype(F32)
        zc = []
        for s in range(2):
            xe = jnp.concatenate([zp_ref[s] * keep, zm_ref[s]], axis=0)
            zc.append(_conv_ext(xe, cw_ref[s], taps)[HALO:] + cb_ref[s])
        o_ref[...] = (zc[0] * _sigmoid(zc[0]) * zc[1]).astype(BF16)

    in_specs = [pl.BlockSpec((2, None, HALO, c), lambda j, i: (0, j, _prev_halo(tm)(i), 0)),
                pl.BlockSpec((2, None, tm, c), lambda j, i: (0, j, i, 0)),
                pl.BlockSpec((2, None, taps, c), lambda j, i: (0, j, 0, 0)),
                pl.BlockSpec((2, None, 1, c), lambda j, i: (0, j, 0, 0))]
    return _call(body, name, (nj, m // tm), in_specs, pl.BlockSpec((None, tm, c), lambda j, i: (j, i, 0)),
                 _sds((nj, m, c), BF16))(z, z, cw, cb)


def ffn_mid_bwd(z, dact, cw, cb, name):
    _, nj, m, c = z.shape
    taps = cw.shape[2]
    tm = _rows(m, 512)
    n_i = m // tm

    def body(zp_ref, zm_ref, zn_ref, dm_ref, dn_ref, cw_ref, cb_ref, dz_ref, dw_ref, db_ref):
        i = pl.program_id(1)
        first = i == 0
        kp, kn = (i > 0).astype(F32), (i < n_i - 1).astype(F32)
        xe = [jnp.concatenate([zp_ref[s] * kp, zm_ref[s], zn_ref[s] * kn], axis=0) for s in range(2)]
        zc = [_conv_ext(xe[s], cw_ref[s], taps) + cb_ref[s] for s in range(2)]
        dae = jnp.concatenate([jnp.zeros((HALO, c), F32), dm_ref[...], dn_ref[...] * kn], axis=0)
        sg = _sigmoid(zc[0])
        dzc = [dae * zc[1] * (sg * (1.0 + zc[0] * (1.0 - sg))), dae * (zc[0] * sg)]
        ms = slice(HALO, HALO + tm)
        for s in range(2):
            w = cw_ref[s]
            dx = dzc[s] * w[taps - 1:taps]
            for u in range(1, taps):
                dx = dx + _up(dzc[s], u) * w[taps - 1 - u:taps - u]
            dz_ref[s] = dx[ms].astype(BF16)
            dm = dzc[s][ms]
            for u in range(taps):
                _acc(dw_ref.at[s, taps - 1 - u:taps - u, :], _colsum(dm * _down(xe[s], u)[ms]), first)
            _acc(db_ref.at[s], _colsum(dm), first)

    in_specs = [pl.BlockSpec((2, None, HALO, c), lambda j, i: (0, j, _prev_halo(tm)(i), 0)),
                pl.BlockSpec((2, None, tm, c), lambda j, i: (0, j, i, 0)),
                pl.BlockSpec((2, None, HALO, c), lambda j, i: (0, j, _next_halo(tm, m)(i), 0)),
                pl.BlockSpec((None, tm, c), lambda j, i: (j, i, 0)),
                pl.BlockSpec((None, HALO, c), lambda j, i: (j, _next_halo(tm, m)(i), 0)),
                pl.BlockSpec((2, None, taps, c), lambda j, i: (0, j, 0, 0)),
                pl.BlockSpec((2, None, 1, c), lambda j, i: (0, j, 0, 0))]
    out_specs = [pl.BlockSpec((2, None, tm, c), lambda j, i: (0, j, i, 0)),
                 pl.BlockSpec((2, None, taps, c), lambda j, i: (0, j, 0, 0)),
                 pl.BlockSpec((2, None, 1, c), lambda j, i: (0, j, 0, 0))]
    return _call(body, name, (nj, n_i), in_specs, out_specs,
                 [_sds((2, nj, m, c), BF16), _sds((2, nj, taps, c)), _sds((2, nj, 1, c))])(z, z, z, dact, dact, cw, cb)


def _as2d(a, lead):
    shape = a.shape
    return a.reshape((lead, -1, shape[-1]) if lead else (-1, shape[-1]))


def add_pairs(g, l1, own, name):
    shape = l1.shape
    g3, l3 = _as2d(g, N_DEV), _as2d(l1, 4)
    _, r, c = l3.shape
    tr = _rows(r, 512) if r % 512 == 0 else r

    def body(own_ref, a_ref, b_ref, o_ref):
        o_ref[...] = (a_ref[...].astype(F32) + b_ref[...].astype(F32)).astype(o_ref.dtype)

    spec = pl.BlockSpec((None, tr, c), lambda k, i, own_ref: (k, i, 0))
    grid_spec = pltpu.PrefetchScalarGridSpec(
        num_scalar_prefetch=1, grid=(4, r // tr),
        in_specs=[pl.BlockSpec((None, tr, c), lambda k, i, own_ref: (own_ref[k], i, 0)), spec], out_specs=spec)
    out = pl.pallas_call(
        body, name=name, grid_spec=grid_spec, out_shape=_sds(l3.shape, l1.dtype),
        compiler_params=pltpu.CompilerParams(dimension_semantics=("arbitrary", "arbitrary"),
                                             vmem_limit_bytes=V7X_VMEM_LIMIT_BYTES))(own, g3, l3)
    return out.reshape(shape)


def _grad_sum(p_ref, l_ref):
    return ((p_ref[...].astype(F32) + l_ref[0].astype(F32)) + l_ref[1].astype(F32)) + l_ref[2].astype(F32)


def sum_parts(p, l2, name):
    _, r, c = p.shape

    def body(p_ref, l_ref, o_ref):
        o_ref[...] = _grad_sum(p_ref, l_ref)

    return _call(body, name, (1,), [pl.BlockSpec((None, r, c), lambda i: (0, 0, 0)), pl.BlockSpec((3, r, c), lambda i: (0, 0, 0))],
                 pl.BlockSpec((r, c), lambda i: (0, 0)), _sds((r, c)))(p, l2)


def _adamw_math(w, g, m, v):
    m = ADAM_B1 * m + (1.0 - ADAM_B1) * g
    v = ADAM_B2 * v + (1.0 - ADAM_B2) * (g * g)
    m_hat = m / (1.0 - ADAM_B1 ** ADAM_STEP)
    v_hat = v / (1.0 - ADAM_B2 ** ADAM_STEP)
    delta = -ADAM_LR * (m_hat / (jnp.sqrt(v_hat) + ADAM_EPS) + ADAM_WD * w)
    return delta, m, v


def adamw(w, m, v, name, g=None, p=None, l2=None):
    shape = w.shape
    w2, m2, v2 = (_as2d(t, 0) for t in (w, m, v))
    r, c = w2.shape
    tr = _rows(r, 512) if r % 512 == 0 else r
    row = pl.BlockSpec((tr, c), lambda i: (i, 0))
    if g is None:
        p3, l3 = _as2d(p, 4), _as2d(l2, 3)
        gin = (p3, l3)
        gspecs = [pl.BlockSpec((None, tr, c), lambda i: (0, i, 0)), pl.BlockSpec((3, tr, c), lambda i: (0, i, 0))]
    else:
        gin, gspecs = (_as2d(g, 0),), [row]

    def body(*refs):
        n_g = len(gin)
        w_ref, m_ref, v_ref, g_ref, d_ref, nm_ref, nv_ref = refs[n_g:]
        grad = refs[0][...] if n_g == 1 else _grad_sum(refs[0], refs[1])
        delta, nm, nv = _adamw_math(w_ref[...], grad, m_ref[...], v_ref[...])
        g_ref[...] = grad
        d_ref[...] = delta
        nm_ref[...] = nm
        nv_ref[...] = nv

    outs = _call(body, name, (r // tr,), gspecs + [row, row, row], [row] * 4, [_sds((r, c))] * 4)(*gin, w2, m2, v2)
    return tuple(o.reshape(shape) for o in outs)


def adamw_layer(w, m, v, layer, prev, name, p, l2):
    n_l, r, c = w.shape
    tr = _rows(r, 512) if r % 512 == 0 else r
    slab = pl.BlockSpec((None, tr, c), lambda i: (layer, i, 0))
    in_specs = [pl.BlockSpec((None, tr, c), lambda i: (0, i, 0)), pl.BlockSpec((3, tr, c), lambda i: (0, i, 0)), slab, slab, slab]
    n_in = len(in_specs)
    prev = () if prev is None else tuple(prev)

    def body(p_ref, l_ref, w_ref, m_ref, v_ref, *rest):
        g_ref, d_ref, nm_ref, nv_ref = rest[len(prev):]
        grad = _grad_sum(p_ref, l_ref)
        delta, nm, nv = _adamw_math(w_ref[...], grad, m_ref[...], v_ref[...])
        g_ref[...] = grad
        d_ref[...] = delta
        nm_ref[...] = nm
        nv_ref[...] = nv

    return pl.pallas_call(
        body, name=name, grid=(r // tr,), in_specs=in_specs + [pl.BlockSpec(memory_space=pl.ANY)] * len(prev),
        out_specs=[slab] * 4, out_shape=[_sds((n_l, r, c))] * 4,
        input_output_aliases={n_in + q: q for q in range(len(prev))},
        compiler_params=pltpu.CompilerParams(dimension_semantics=("arbitrary",), vmem_limit_bytes=V7X_VMEM_LIMIT_BYTES),
    )(_as2d(p, 4), _as2d(l2, 3), w, m, v, *prev)


def _comm_call(body, name, ins, out_shape, n_sems):
    any_spec = pl.BlockSpec(memory_space=pl.ANY)
    return pl.pallas_call(
        body, name=name, in_specs=[any_spec] * len(ins), out_specs=[any_spec] * len(out_shape), out_shape=out_shape,
        scratch_shapes=[pltpu.SemaphoreType.DMA((n,)) for n in n_sems],
        compiler_params=pltpu.CompilerParams(has_side_effects=True))(*ins)


def _place():
    return lax.axis_index("x"), lax.axis_index("y"), lax.axis_index("c")


def _dev_index(px, py, pc):
    return 4 * px + 2 * py + pc


def all_gather(blocks, name):
    n_t = len(blocks)

    def body(*refs):
        ins, outs = refs[:n_t], refs[n_t:2 * n_t]
        send_sems, recv_sems, local_sems = refs[2 * n_t:]
        x, y, c = _place()
        me, sibling = (x, y, c), (x, y, 1 - c)
        chips = [(1 - x, y), (x, 1 - y), (1 - x, 1 - y)]

        def copy(t, k, block, to, src=None):
            dst = outs[t].at[_dev_index(*block)]
            return pltpu.make_async_remote_copy(
                src_ref=dst if src is None else src, dst_ref=dst, send_sem=send_sems.at[t * 7 + k],
                recv_sem=recv_sems.at[t * 7 + k], device_id=to, device_id_type=MESH_ID)

        mine = [pltpu.make_async_copy(ins[t], outs[t].at[_dev_index(*me)], local_sems.at[t]) for t in range(n_t)]
        for cp in mine:
            cp.start()
        first = []
        for t in range(n_t):
            first.append(copy(t, 0, me, sibling, src=ins[t]))
            first += [copy(t, 1 + j, me, (*chip, c), src=ins[t]) for j, chip in enumerate(chips)]
        for cp in first:
            cp.start()
        passed = []
        for t in range(n_t):
            for j, chip in enumerate(chips):
                copy(t, 1 + j, (*chip, c), me).wait_recv()
                cp = copy(t, 4 + j, (*chip, c), sibling)
                cp.start()
                passed.append(cp)
        for t in range(n_t):
            copy(t, 0, sibling, me).wait_recv()
            for j, chip in enumerate(chips):
                copy(t, 4 + j, (*chip, 1 - c), me).wait_recv()
        for cp in first + passed:
            cp.wait_send()
        for cp in mine:
            cp.wait()

    out_shape = [_sds((N_DEV,) + b.shape, b.dtype) for b in blocks]
    return _comm_call(body, name, blocks, out_shape, (7 * n_t, 7 * n_t, n_t))


def _chip_of(x, y, k):
    return (x if k % 2 == 0 else 1 - x), (y if k // 2 == 0 else 1 - y)


_HBM_SPEC = pl.BlockSpec(memory_space=pltpu.HBM)
_SEM_SPEC = pl.BlockSpec(memory_space=pltpu.SEMAPHORE)
_DATAFLOW = pltpu.SideEffectType.DATAFLOW_SIDE_EFFECTING


def _in_hbm(a):
    return pltpu.with_memory_space_constraint(a, pltpu.HBM)


def _split_start(name, issue, srcs, land_shapes, sem_counts):
    n_buf, n_sem = len(srcs) + len(land_shapes), len(sem_counts)

    def body(*refs):
        issue(refs[:len(srcs)], refs[len(srcs):n_buf], refs[n_buf:n_buf + n_sem])
        refs[-1][...] = jnp.zeros_like(refs[-1])

    bufs = [pltpu.HBM(s.shape, s.dtype) for s in list(srcs) + list(land_shapes)]
    outs = pl.pallas_call(
        body, name=name, in_specs=(_HBM_SPEC,) * n_buf,
        out_shape=(*[pltpu.SemaphoreType.DMA((n,)) for n in sem_counts], *bufs, _sds((8, LANES))),
        out_specs=(*[_SEM_SPEC] * n_sem, *[_HBM_SPEC] * n_buf, pl.BlockSpec(memory_space=pltpu.VMEM)),
        input_output_aliases={i: n_sem + i for i in range(n_buf)},
        compiler_params=pltpu.CompilerParams(has_side_effects=_DATAFLOW),
    )(*[_in_hbm(s) for s in srcs], *[_in_hbm(lax.empty(s.shape, s.dtype)) for s in land_shapes])
    return outs[:n_sem], outs[n_sem:n_sem + len(srcs)], outs[n_sem + len(srcs):n_sem + n_buf], outs[-1]


def _split_wait(name, finish, sems, srcs, lands, after):
    n_buf, n_sem = len(srcs) + len(lands), len(sems)

    def body(*refs):
        finish(refs[:len(srcs)], refs[len(srcs):n_buf], refs[n_buf:n_buf + n_sem])

    bufs = [pltpu.HBM(s.shape, s.dtype) for s in list(srcs) + list(lands)]
    outs = pl.pallas_call(
        body, name=name, in_specs=(*[_HBM_SPEC] * n_buf, *[_SEM_SPEC] * n_sem, *[pl.BlockSpec(memory_space=pl.ANY)] * len(after)),
        out_shape=tuple(bufs), out_specs=(_HBM_SPEC,) * n_buf, input_output_aliases={i: i for i in range(n_buf)},
        compiler_params=pltpu.CompilerParams(has_side_effects=_DATAFLOW),
    )(*srcs, *lands, *sems, *after)
    return outs[:len(srcs)], outs[len(srcs):]


def _peer(x, y, c, r):
    return (1 - x if r & 4 else x), (1 - y if r & 2 else y), (1 - c if r & 1 else c)


def _gather_copies(src_refs, land_refs, sem_refs, arrivals):
    send_sems, recv_sems, local_sems = sem_refs
    x, y, c = _place()
    me = _dev_index(x, y, c)
    local, sends, recvs = [], [], []
    for j, (src, land) in enumerate(zip(src_refs, land_refs)):
        local.append(pltpu.make_async_copy(src, land.at[me], local_sems.at[j]))
        for r in range(1, N_DEV):
            peer = _peer(x, y, c, r)
            q = (N_DEV - 1) * j + r - 1
            sends.append(pltpu.make_async_remote_copy(src_ref=src, dst_ref=land.at[me], send_sem=send_sems.at[q],
                                                      recv_sem=recv_sems.at[q], device_id=peer, device_id_type=MESH_ID))
            if arrivals:
                recvs.append(pltpu.make_async_remote_copy(
                    src_ref=src, dst_ref=land.at[_dev_index(*peer)], send_sem=send_sems.at[q], recv_sem=recv_sems.at[q],
                    device_id=peer, device_id_type=MESH_ID))
    return local, sends, recvs


def gather_start(groups, name):
    flat = [b for g in groups for b in g]
    bounds = [sum(len(g) for g in groups[:i]) for i in range(len(groups) + 1)]

    def issue(src_refs, land_refs, sem_refs):
        for i in range(len(groups)):
            lo, hi = bounds[i], bounds[i + 1]
            local, sends, _ = _gather_copies(src_refs[lo:hi], land_refs[lo:hi], sem_refs[3 * i:3 * i + 3], False)
            for cp in local + sends:
                cp.start()

    sem_counts = [n for g in groups for n in ((N_DEV - 1) * len(g), (N_DEV - 1) * len(g), len(g))]
    sems, srcs, lands, token = _split_start(name, issue, flat, [_sds((N_DEV,) + b.shape, b.dtype) for b in flat], sem_counts)
    return [(sems[3 * i:3 * i + 3], srcs[bounds[i]:bounds[i + 1]], lands[bounds[i]:bounds[i + 1]])
            for i in range(len(groups))], token


def gather_wait(group, after, name):
    sems, srcs, lands = group

    def finish(src_refs, land_refs, sem_refs):
        local, sends, recvs = _gather_copies(src_refs, land_refs, sem_refs, True)
        for cp in local:
            cp.wait()
        for cp in recvs:
            cp.wait_recv()
        for cp in sends:
            cp.wait_send()

    return _split_wait(name, finish, sems, srcs, lands, after)[1]


def _sibling_copies(src_refs, land_refs, sem_refs):
    send_sems, recv_sems = sem_refs
    x, y, c = _place()
    copies = []
    for t, (src, land) in enumerate(zip(src_refs, land_refs)):
        for k in range(4):
            cx, cy = _chip_of(x, y, k)
            copies.append(pltpu.make_async_remote_copy(
                src_ref=src.at[_dev_index(cx, cy, 1 - c)], dst_ref=land.at[k], send_sem=send_sems.at[4 * t + k],
                recv_sem=recv_sems.at[4 * t + k], device_id=(x, y, 1 - c), device_id_type=MESH_ID))
    return copies


def _chip_copies(src_refs, land_refs, sem_refs):
    send_sems, recv_sems = sem_refs
    x, y, c = _place()
    copies = []
    for t, (src, land) in enumerate(zip(src_refs, land_refs)):
        for k in range(1, 4):
            cx, cy = _chip_of(x, y, k)
            copies.append(pltpu.make_async_remote_copy(
                src_ref=src.at[k], dst_ref=land.at[k - 1], send_sem=send_sems.at[3 * t + k - 1],
                recv_sem=recv_sems.at[3 * t + k - 1], device_id=(cx, cy, c), device_id_type=MESH_ID))
    return copies


def _exchange_start(copies_of, n_land, per_array, arrays, name):
    def issue(src_refs, land_refs, sem_refs):
        for cp in copies_of(src_refs, land_refs, sem_refs):
            cp.start()

    n = per_array * len(arrays)
    lands = [_sds((n_land,) + a.shape[1:], a.dtype) for a in arrays]
    return _split_start(name, issue, arrays, lands, (n, n))


def _exchange_wait(copies_of, started, after, name):
    sems, srcs, lands, _ = started

    def finish(src_refs, land_refs, sem_refs):
        copies = copies_of(src_refs, land_refs, sem_refs)
        for cp in copies:
            cp.wait_recv()
        for cp in copies:
            cp.wait_send()

    return _split_wait(name, finish, sems, srcs, lands, after)


def sibling_start(grads, name):
    return _exchange_start(_sibling_copies, 4, 4, grads, name)


def sibling_wait(started, after, name):
    return _exchange_wait(_sibling_copies, started, after, name)


def chips_start(parts, name):
    return _exchange_start(_chip_copies, 3, 3, parts, name)


def chips_wait(started, after, name):
    return _exchange_wait(_chip_copies, started, after, name)


def _pack(arrays, rows):
    flat = jnp.concatenate([a.reshape(-1) for a in arrays])
    return jnp.pad(flat, (0, rows * LANES - flat.shape[0])).reshape(rows, LANES)


def _pack_stacked(arrays, rows):
    flat = jnp.concatenate([a.reshape(N_DEV, -1) for a in arrays], axis=1)
    return jnp.pad(flat, ((0, 0), (0, rows * LANES - flat.shape[1]))).reshape(N_DEV, rows, LANES)


def _unpack(buf, shapes, lead=()):
    flat = buf.reshape(lead + (-1,))
    out, off = [], 0
    for s in shapes:
        n = 1
        for d in s:
            n *= d
        out.append(flat[..., off:off + n].reshape(lead + tuple(s)))
        off += n
    return out


def _padded_rows(shapes, multiple):
    n = sum(functools.reduce(lambda a, b: a * b, s, 1) for s in shapes)
    rows = -(-n // LANES)
    return -(-rows // multiple) * multiple


def _to_full(stacked, axis):
    t = jnp.moveaxis(stacked, 0, axis)
    return t.reshape(t.shape[:axis] + (t.shape[axis] * t.shape[axis + 1],) + t.shape[axis + 2:])


def _to_stacked(full, axis):
    s = full.shape
    t = full.reshape(s[:axis] + (N_DEV, s[axis] // N_DEV) + s[axis + 1:])
    return jnp.moveaxis(t, axis, 0)


def _ffn_forward(x, norm_g, w_up, cw, cb, w_down, tag):
    h = rms_fwd(x, norm_g, f"ffn{tag}_norm")
    z = mm_in(h, w_up, f"ffn{tag}_up", stacked_out=True)
    nb, m, c = z.shape
    z4 = z.reshape(2, nb // 2, m, c)
    act = ffn_mid_fwd(z4, cw, cb, f"ffn{tag}_mid")
    out = mm_out(act, w_down, x, f"ffn{tag}_down")
    return out, (x, h, z4, act)


def _ffn_backward(dx, dxb, saved, norm_g, w_up, cw, cb, w_down, tag, after=()):
    x, h, z4, act = saved
    nj = act.shape[0]
    dact = mm_dx_out(dxb, w_down, f"ffn{tag}_down_dx", groups=nj, after=after)
    dw_down = mm_dw_out(act, dxb, f"ffn{tag}_down_dw")
    dz4, dcw, dcb = ffn_mid_bwd(z4, dact, cw, cb, f"ffn{tag}_mid_bwd")
    dz = dz4.reshape((2 * nj,) + dz4.shape[2:])
    dh = mm_dx_in(dz, w_up, f"ffn{tag}_up_dx")
    dw_up = mm_dw_in(h, dz, 2 * nj, f"ffn{tag}_up_dw")
    dx, dxb, dg = rms_bwd(x, norm_g, dh, dx, f"ffn{tag}_norm_bwd")
    return dx, dxb, dict(norm_g=dg, w_up=dw_up, conv_w=dcw, conv_b=dcb, w_down=dw_down)


def kernel(x, a_norm_g, a_w_in, a_b_in, a_v_norm_g, a_w_s, a_b_s, a_w_out, b_norm_g, b_w_in, b_w_grp, b_b_grp, b_scale, b_w_out, c_norm_g, c_w_in, c_b_in, c_conv_w, c_conv_b, c_w_a, c_b_a, c_w_i, c_b_i, c_lambda, c_w_out, d_norm_g, d_w_in, d_conv_w, d_w_out, ffn_norm_g, ffn_w_up, ffn_conv_w, ffn_conv_b, ffn_w_down, final_norm_g, loss_target, m_a_norm_g, m_a_w_in, m_a_b_in, m_a_v_norm_g, m_a_w_s, m_a_b_s, m_a_w_out, m_b_norm_g, m_b_w_in, m_b_w_grp, m_b_b_grp, m_b_scale, m_b_w_out, m_c_norm_g, m_c_w_in, m_c_b_in, m_c_conv_w, m_c_conv_b, m_c_w_a, m_c_b_a, m_c_w_i, m_c_b_i, m_c_lambda, m_c_w_out, m_d_norm_g, m_d_w_in, m_d_conv_w, m_d_w_out, m_ffn_norm_g, m_ffn_w_up, m_ffn_conv_w, m_ffn_conv_b, m_ffn_w_down, m_final_norm_g, v_a_norm_g, v_a_w_in, v_a_b_in, v_a_v_norm_g, v_a_w_s, v_a_b_s, v_a_w_out, v_b_norm_g, v_b_w_in, v_b_w_grp, v_b_b_grp, v_b_scale, v_b_w_out, v_c_norm_g, v_c_w_in, v_c_b_in, v_c_conv_w, v_c_conv_b, v_c_w_a, v_c_b_a, v_c_w_i, v_c_b_i, v_c_lambda, v_c_w_out, v_d_norm_g, v_d_w_in, v_d_conv_w, v_d_w_out, v_ffn_norm_g, v_ffn_w_up, v_ffn_conv_w, v_ffn_conv_b, v_ffn_w_down, v_final_norm_g):
    args = locals()
    w_loc = {n: args[n] for n in WEIGHTS}
    m_loc = {n: args["m_" + n] for n in WEIGHTS}
    v_loc = {n: args["v_" + n] for n in WEIGHTS}
    depth = ffn_w_up.shape[0]
    xs = x[0]
    target = loss_target[0]

    small_names = list(SMALL_SHARDED)
    small_shapes = [w_loc[n].shape for n in small_names]
    small_rows = _padded_rows(small_shapes, 8)
    small_packed = _pack([w_loc[n] for n in small_names], small_rows)
    mixers = [(a_w_in, a_w_out), (b_w_in, b_w_out), (c_w_in, c_w_out), (d_w_in, d_w_out)]
    groups = []
    for l in range(depth):
        groups.append([mixers[l][0][0].astype(BF16), mixers[l][1][0].astype(BF16)] + ([small_packed] if l == 0 else []))
        groups.append([ffn_w_up[l].astype(BF16), ffn_w_down[l].astype(BF16)])
    in_flight, _ = gather_start(groups, "gather_start")

    def rows_full(st):
        return st.reshape((st.shape[0] * st.shape[1],) + st.shape[2:])

    nb = N_DEV
    ffn_cb = [ffn_conv_b[l].reshape(2, nb // 2, 1, -1) for l in range(depth)]
    ffn_g = [ffn_norm_g[l:l + 1] for l in range(depth)]
    a_bst = a_b_s[0].T
    w_up, w_down, saved = [None] * depth, [None] * depth, {}

    def ffn_forward(xl, l):
        up, down = gather_wait(in_flight[2 * l + 1], (xl,), f"gather_wait_ffn{l}")
        w_up[l], w_down[l] = up, rows_full(down)
        return _ffn_forward(xl, ffn_g[l], w_up[l], ffn_cw[l], ffn_cb[l], w_down[l], l)

    wa_in, wa_out, small_st = gather_wait(in_flight[0], (xs,), "gather_wait_a")
    wa_out = rows_full(wa_out)
    sm = {n: _to_full(s, SMALL_SHARDED[n]) for n, s in zip(small_names, _unpack(small_st, small_shapes, (N_DEV,)))}
    ffn_cw = [sm['ffn_conv_w'][l].reshape(ffn_conv_w.shape[1], 2, nb // 2, -1).transpose(1, 2, 0, 3) for l in range(depth)]
    b_wgrp, c_wa, c_wi = (sm[n][0].astype(BF16) for n in ('b_w_grp', 'c_w_a', 'c_w_i'))
    b_bgrp, c_ba, c_bi = (sm[n][0].reshape(1, -1) for n in ('b_b_grp', 'c_b_a', 'c_b_i'))
    h = rms_fwd(xs, a_norm_g, "a_norm")
    z = mm_in(h, wa_in, "a_in", bias=a_b_in)
    y = a_mid_fwd(z, a_v_norm_g, a_w_s[0], a_bst, "a_mid")
    x1 = mm_out(y, wa_out, xs, "a_out")
    saved['a'] = (xs, h, z, y)
    x1, saved['f0'] = ffn_forward(x1, 0)

    wb_in, wb_out = gather_wait(in_flight[2], (x1,), "gather_wait_b")
    wb_in, wb_out = rows_full(wb_in)[None], rows_full(wb_out)
    h = rms_fwd(x1, sm['b_norm_g'], "b_norm")
    z = mm_in(h, wb_in, "b_in")
    y = b_mid_fwd(z, b_wgrp, b_bgrp, sm['b_scale'], "b_mid")
    x2 = mm_out(y, wb_out, x1, "b_out")
    saved['b'] = (x1, h, z, y)
    x2, saved['f1'] = ffn_forward(x2, 1)

    wc_in, wc_out = gather_wait(in_flight[4], (x2,), "gather_wait_c")
    wc_out = rows_full(wc_out)
    h = rms_fwd(x2, sm['c_norm_g'], "c_norm")
    z = mm_in(h, wc_in, "c_in", bias=sm['c_b_in'])
    c_cw = sm['c_conv_w'][0]
    a_seq, b_seq, xr = c_mid_fwd(z, c_cw, sm['c_conv_b'], c_wa, c_ba, c_wi, c_bi, sm['c_lambda'], "c_mid")
    hs, y = c_scan_fwd(a_seq, b_seq, z, "c_scan")
    x3 = mm_out(y, wc_out, x2, "c_out")
    saved['c'] = (x2, h, z, y, a_seq, xr, hs)
    x3, saved['f2'] = ffn_forward(x3, 2)

    wd_in, wd_out = gather_wait(in_flight[6], (x3,), "gather_wait_d")
    wd_out = rows_full(wd_out)
    h = rms_fwd(x3, sm['d_norm_g'], "d_norm")
    z = mm_in(h, wd_in, "d_in")
    d_cw = sm['d_conv_w'][0]
    y = d_mid_fwd(z, d_cw, "d_mid")
    x4 = mm_out(y, wd_out, x3, "d_out")
    saved['d'] = (x3, h, z, y)
    x4, saved['f3'] = ffn_forward(x4, 3)

    loss_part, dx, dxb, d_final_g = final_loss(x4, final_norm_g.reshape(1, -1), target, "final_loss")
    loss = lax.psum(loss_part[0, 0], ("x", "y", "c"))

    def rows_stacked(full):
        return full.reshape((N_DEV, full.shape[0] // N_DEV) + full.shape[1:])

    mx, my, mc = _place()
    own = jnp.stack([_dev_index(*_chip_of(mx, my, k), mc) for k in range(4)]).astype(jnp.int32)
    repl_shapes = [w_loc[n].shape for n in REPLICATED]
    repl_rows = _padded_rows(repl_shapes, 8 * N_DEV)
    shard_of = {n: (w_loc[n][0], m_loc[n][0], v_loc[n][0])
                for n in ('a_w_in', 'a_w_out', 'b_w_in', 'b_w_out', 'c_w_in', 'c_w_out', 'd_w_in', 'd_w_out')}
    shard_of['small'] = (small_packed, _pack([m_loc[n] for n in small_names], small_rows),
                         _pack([v_loc[n] for n in small_names], small_rows))
    updated = {}

    def finish(n, part, others):
        if n == 'repl':
            chunk = sum_parts(part, others, "rs_sum_repl")
            repl_g = all_gather([chunk], "gather_repl")[0].reshape(repl_rows, LANES)
            updated[n] = adamw(*(_pack([src[k] for k in REPLICATED], repl_rows) for src in (w_loc, m_loc, v_loc)),
                               "adamw_repl", g=repl_g)
        elif n.startswith('ffn_w'):
            base, l = n[:-1], int(n[-1])
            updated[base] = adamw_layer(w_loc[base], m_loc[base], v_loc[base], l, updated.get(base), f"adamw_{n}", part, others)
            return updated[base][1]
        else:
            updated[n] = adamw(*shard_of[n], f"adamw_{n}", p=part, l2=others)
        return updated[n][1]

    stages = [None, None]

    def advance(tag, new, after):
        behind = []
        first = None
        if new:
            first = ([n for n, _ in new], sibling_start([g for _, g in new], f"rs_sibling_start{tag}"))
            behind.append(first[1][3])
        second = None
        if stages[0] is not None:
            names, started = stages[0]
            grads, got = sibling_wait(started, after, f"rs_sibling_wait{tag}")
            parts = [add_pairs(g, l, own, f"rs_add_{n}") for n, g, l in zip(names, grads, got)]
            second = (names, chips_start(parts, f"rs_chips_start{tag}"))
            behind.append(second[1][3])
        if stages[1] is not None:
            names, started = stages[1]
            parts, others = chips_wait(started, after, f"rs_chips_wait{tag}")
            behind += [finish(n, p, o) for n, p, o in zip(names, parts, others)]
        stages[:] = [first, second]
        return tuple(behind)

    gf = [None] * depth
    dx, dxb, gf[3] = _ffn_backward(dx, dxb, saved['f3'], ffn_g[3], w_up[3], ffn_cw[3], ffn_cb[3], w_down[3], 3)
    xin, h, z, y = saved['d']
    dy = mm_dx_out(dxb, wd_out, "d_out_dx")
    g_d_w_out = mm_dw_out(y, dxb, "d_out_dw")
    dz, g_d_conv_w = d_mid_bwd(z, dy, d_cw, "d_mid_bwd")
    dh = mm_dx_in(dz, wd_in, "d_in_dx")
    g_d_w_in = mm_dw_in(h, dz, nb, "d_in_dw")
    dx, dxb, g_d_norm_g = rms_bwd(xin, sm['d_norm_g'], dh, dx, "d_norm_bwd")
    behind = advance(0, [('ffn_w_up3', gf[3]['w_up']), ('ffn_w_down3', rows_stacked(gf[3]['w_down'])), ('d_w_in', g_d_w_in),
                         ('d_w_out', rows_stacked(g_d_w_out))], (dx,))

    dx, dxb, gf[2] = _ffn_backward(dx, dxb, saved['f2'], ffn_g[2], w_up[2], ffn_cw[2], ffn_cb[2], w_down[2], 2, behind)
    xin, h, z, y, a_seq, xr, hs = saved['c']
    dy = mm_dx_out(dxb, wc_out, "c_out_dx")
    g_c_w_out = mm_dw_out(y, dxb, "c_out_dw")
    lam_seq, da_seq, dgate, dgate_sum = c_scan_bwd(dy, z, hs, a_seq, "c_scan_bwd")
    dxr, g_c_w_a, g_c_w_i, g_c_b_a, g_c_b_i, g_c_lambda = c_mid_bwd(
        lam_seq, da_seq, xr, c_wa, c_ba, c_wi, c_bi, sm['c_lambda'], "c_mid_bwd")
    dxr_pre, g_c_conv_w, g_c_conv_b, dxr_pre_sum = conv_bwd(dxr, z, 1, c_cw, "c_conv_bwd")
    dz = jnp.concatenate([dgate, dxr_pre], axis=1)
    g_c_b_in = jnp.concatenate([dgate_sum, dxr_pre_sum], axis=1)
    dh = mm_dx_in(dz, wc_in, "c_in_dx")
    g_c_w_in = mm_dw_in(h, dz, nb, "c_in_dw")
    dx, dxb, g_c_norm_g = rms_bwd(xin, sm['c_norm_g'], dh, dx, "c_norm_bwd")
    behind = advance(1, [('ffn_w_up2', gf[2]['w_up']), ('ffn_w_down2', rows_stacked(gf[2]['w_down'])), ('c_w_in', g_c_w_in),
                         ('c_w_out', rows_stacked(g_c_w_out))], (dx,))

    dx, dxb, gf[1] = _ffn_backward(dx, dxb, saved['f1'], ffn_g[1], w_up[1], ffn_cw[1], ffn_cb[1], w_down[1], 1, behind)
    xin, h, z, y = saved['b']
    dy = mm_dx_out(dxb, wb_out, "b_out_dx")
    g_b_w_out = mm_dw_out(y, dxb, "b_out_dw")
    dp, g_b_w_grp, g_b_b_grp, g_b_scale = b_mid_bwd(z, dy, b_wgrp, b_bgrp, sm['b_scale'], "b_mid_bwd")
    dz = b_pool_bwd(dp, "b_pool_bwd")
    dh = mm_dx_in(dz, wb_in, "b_in_dx")
    g_b_w_in = mm_dw_in(h, dz, 1, "b_in_dw")
    dx, dxb, g_b_norm_g = rms_bwd(xin, sm['b_norm_g'], dh, dx, "b_norm_bwd")
    behind = advance(2, [('ffn_w_up1', gf[1]['w_up']), ('ffn_w_down1', rows_stacked(gf[1]['w_down'])),
                         ('b_w_in', rows_stacked(g_b_w_in[0])), ('b_w_out', rows_stacked(g_b_w_out))], (dx,))

    dx, dxb, gf[0] = _ffn_backward(dx, dxb, saved['f0'], ffn_g[0], w_up[0], ffn_cw[0], ffn_cb[0], w_down[0], 0, behind)
    full_small = {
        'b_norm_g': g_b_norm_g, 'b_w_grp': g_b_w_grp[None], 'b_b_grp': g_b_b_grp.reshape(b_b_grp.shape[:2] + (-1,)),
        'b_scale': g_b_scale, 'c_norm_g': g_c_norm_g, 'c_b_in': g_c_b_in, 'c_conv_w': g_c_conv_w[None],
        'c_conv_b': g_c_conv_b, 'c_w_a': g_c_w_a[None], 'c_b_a': g_c_b_a.reshape(c_b_a.shape[:2] + (-1,)),
        'c_w_i': g_c_w_i[None], 'c_b_i': g_c_b_i.reshape(c_b_i.shape[:2] + (-1,)), 'c_lambda': g_c_lambda,
        'd_norm_g': g_d_norm_g, 'd_conv_w': g_d_conv_w[None],
        'ffn_conv_w': jnp.stack([gf[l]['conv_w'].transpose(2, 0, 1, 3).reshape(ffn_conv_w.shape[1], -1) for l in range(depth)])}
    small_grads = _pack_stacked([_to_stacked(full_small[n], SMALL_SHARDED[n]) for n in small_names], small_rows)
    behind = advance(3, [('ffn_w_up0', gf[0]['w_up']), ('ffn_w_down0', rows_stacked(gf[0]['w_down'])), ('small', small_grads)],
                     (dx,))

    xin, h, z, y = saved['a']
    dy = mm_dx_out(dxb, wa_out, "a_out_dx", after=behind)
    g_a_w_out = mm_dw_out(y, dxb, "a_out_dw")
    behind = advance(4, [('a_w_out', rows_stacked(g_a_w_out))], (dy,))
    dz, g_a_b_in, g_a_v_norm_g, g_a_w_s, g_a_b_s = a_mid_bwd(z, dy, a_v_norm_g, a_w_s[0], a_bst, "a_mid_bwd", behind)
    g_a_w_in = mm_dw_in(h, dz, nb, "a_in_dw")
    behind = advance(5, [('a_w_in', g_a_w_in)], (dz,))
    dh = mm_dx_in(dz, wa_in, "a_in_dx", behind)
    dx, _, g_a_norm_g = rms_bwd(xin, a_norm_g, dh, dx, "a_norm_bwd")
    grad_x = dx[None]

    tril = jnp.tril(jnp.ones((A_CHUNK, A_CHUNK), bool))
    repl_full = {
        'a_norm_g': g_a_norm_g, 'a_b_in': g_a_b_in, 'a_v_norm_g': g_a_v_norm_g,
        'a_w_s': jnp.where(tril, g_a_w_s, 0.0)[None], 'a_b_s': g_a_b_s[:, ::LANES].T[None],
        'ffn_norm_g': jnp.concatenate([gf[l]['norm_g'] for l in range(depth)], axis=0),
        'ffn_conv_b': jnp.stack([gf[l]['conv_b'].reshape(-1) for l in range(depth)]), 'final_norm_g': d_final_g.reshape(-1)}
    repl_grads = _pack([repl_full[n] for n in REPLICATED], repl_rows).reshape(N_DEV, repl_rows // N_DEV, LANES)
    behind = advance(6, [('repl', repl_grads)], (dx,))
    behind = advance(7, [], (dx, *behind))
    advance(8, [], (dx, *behind))

    outs = [{}, {}, {}, {}]
    for i, dst in enumerate(outs):
        for n in ('a_w_in', 'a_w_out', 'b_w_in', 'b_w_out', 'c_w_in', 'c_w_out', 'd_w_in', 'd_w_out'):
            dst[n] = updated[n][i][None]
        for n in ('ffn_w_up', 'ffn_w_down'):
            dst[n] = updated[n][i]
        dst.update(zip(small_names, _unpack(updated['small'][i], small_shapes)))
        dst.update(zip(REPLICATED, _unpack(updated['repl'][i], repl_shapes)))
    out_g, out_d, out_m, out_v = outs

    return (loss, grad_x, *[out_g[n] for n in WEIGHTS], *[out_d[n] for n in WEIGHTS], *[out_m[n] for n in WEIGHTS],
            *[out_v[n] for n in WEIGHTS])
```

```python
import functools

import jax
import jax.numpy as jnp
from jax import lax
from jax.experimental import pallas as pl
from jax.experimental.pallas import tpu as pltpu

F32, BF16 = jnp.float32, jnp.bfloat16
MESH_ID = pl.DeviceIdType.MESH
N_DEV = 8
V7X_VMEM_LIMIT_BYTES = 56 << 20
LANES = 128
HALO = 8
POOL_HALO = 16

EPS = 1e-6
A_CHUNK, A_GROUPS = 128, 4
B_WINDOWS = (2, 4, 8, 16)
C_GATE_C = 8.0
ADAM_LR, ADAM_B1, ADAM_B2, ADAM_EPS, ADAM_WD, ADAM_STEP = 0.001, 0.9, 0.999, 1e-08, 0.01, 10

WEIGHTS = ['a_norm_g', 'a_w_in', 'a_b_in', 'a_v_norm_g', 'a_w_s', 'a_b_s', 'a_w_out', 'b_norm_g', 'b_w_in', 'b_w_grp',
           'b_b_grp', 'b_scale', 'b_w_out', 'c_norm_g', 'c_w_in', 'c_b_in', 'c_conv_w', 'c_conv_b', 'c_w_a', 'c_b_a',
           'c_w_i', 'c_b_i', 'c_lambda', 'c_w_out', 'd_norm_g', 'd_w_in', 'd_conv_w', 'd_w_out', 'ffn_norm_g',
           'ffn_w_up', 'ffn_conv_w', 'ffn_conv_b', 'ffn_w_down', 'final_norm_g']
SMALL_SHARDED = {'b_norm_g': 1, 'b_w_grp': 2, 'b_b_grp': 2, 'b_scale': 1, 'c_norm_g': 1, 'c_b_in': 1, 'c_conv_w': 2,
                 'c_conv_b': 1, 'c_w_a': 2, 'c_b_a': 2, 'c_w_i': 2, 'c_b_i': 2, 'c_lambda': 1, 'd_norm_g': 1,
                 'd_conv_w': 2, 'ffn_conv_w': 2}
REPLICATED = ['a_norm_g', 'a_b_in', 'a_v_norm_g', 'a_w_s', 'a_b_s', 'ffn_norm_g', 'ffn_conv_b', 'final_norm_g']


_GELU_C0, _GELU_C1 = 0.7978845608028654, 0.044715


def _gelu(x):
    return 0.5 * x * (1.0 + jnp.tanh(_GELU_C0 * (x + _GELU_C1 * (x * x * x))))


def _gelu_grad(x):
    t = jnp.tanh(_GELU_C0 * (x + _GELU_C1 * (x * x * x)))
    return 0.5 * (1.0 + t) + 0.5 * x * (1.0 - t * t) * (_GELU_C0 * (1.0 + 3.0 * _GELU_C1 * (x * x)))


def _sigmoid(x):
    return jax.nn.sigmoid(x)


def _log1p(x):
    u = 1.0 + x
    return jnp.where(u == 1.0, x, jnp.log(u) * (x / (u - 1.0)))


def _softplus(x):
    return jnp.maximum(x, 0.0) + _log1p(jnp.exp(-jnp.abs(x)))


def _expm1(x):
    poly = x * (1.0 + x * (1 / 2) * (1.0 + x * (1 / 3) * (1.0 + x * (1 / 4) * (1.0 + x * (1 / 5) * (
        1.0 + x * (1 / 6) * (1.0 + x * (1 / 7) * (1.0 + x * (1 / 8))))))))
    return jnp.where(jnp.abs(x) < 0.35, poly, jnp.exp(x) - 1.0)


def _down(xe, s):
    return xe if s == 0 else pltpu.roll(xe, s, 0)


def _up(xe, s):
    return xe if s == 0 else pltpu.roll(xe, xe.shape[0] - s, 0)


def _conv_ext(xe, w, taps):
    y = xe * w[taps - 1:taps]
    for s in range(1, taps):
        y = y + _down(xe, s) * w[taps - 1 - s:taps - s]
    return y


def _acc(ref, val, first):
    @pl.when(first)
    def _():
        ref[...] = val

    @pl.when(jnp.logical_not(first))
    def _():
        ref[...] += val


def _colsum(v):
    return jnp.sum(v, axis=0, keepdims=True)


def _dot(a, b, dims=((1,), (0,))):
    return lax.dot_general(a.astype(BF16), b.astype(BF16), (dims, ((), ())), preferred_element_type=F32)


_NN, _NT, _TN = ((1,), (0,)), ((1,), (1,)), ((0,), (0,))


def _call(body, name, grid, in_specs, out_specs, out_shape, scratch=(), after=()):
    n_in, n_after = len(in_specs), len(after)

    def ordered_body(*refs):
        return body(*refs[:n_in], *refs[n_in + n_after:])

    call = pl.pallas_call(
        ordered_body if n_after else body, name=name, grid=grid,
        in_specs=list(in_specs) + [pl.BlockSpec(memory_space=pl.ANY)] * n_after, out_specs=out_specs,
        out_shape=out_shape, scratch_shapes=list(scratch),
        compiler_params=pltpu.CompilerParams(dimension_semantics=("arbitrary",) * len(grid),
                                             vmem_limit_bytes=V7X_VMEM_LIMIT_BYTES))
    return lambda *args: call(*args, *after)


def _rows(m, t):
    t = min(m, t)
    assert m % t == 0, (m, t)
    return t


def _sds(shape, dtype=F32):
    return jax.ShapeDtypeStruct(tuple(shape), dtype)


def _prev_halo(tm, halo=HALO):
    return lambda i: jnp.maximum(i * (tm // halo) - 1, 0)


def _next_halo(tm, m, halo=HALO):
    return lambda i: jnp.minimum((i + 1) * (tm // halo), m // halo - 1)


def _matmul(name, ins, in_specs, out_shape, o_spec, grid, compute, after=()):
    def body(*refs):
        refs[-1][...] = compute(*refs[:-1]).astype(refs[-1].dtype)

    return _call(body, name, grid, in_specs, o_spec, out_shape, (), after)(*ins)


def mm_in(h, w_st, name, bias=None, stacked_out=False, out_dtype=F32, rows=1024):
    m, k = h.shape
    nb, _, n = w_st.shape
    tm = _rows(m, rows)
    in_specs = [pl.BlockSpec((tm, k), lambda i, j: (i, 0)), pl.BlockSpec((None, k, n), lambda i, j: (j, 0, 0))]
    if stacked_out:
        out, o_spec = _sds((nb, m, n), out_dtype), pl.BlockSpec((None, tm, n), lambda i, j: (j, i, 0))
    else:
        out, o_spec = _sds((m, nb * n), out_dtype), pl.BlockSpec((tm, n), lambda i, j: (i, j))
    if bias is None:
        return _matmul(name, (h, w_st), in_specs, out, o_spec, (m // tm, nb), lambda a, b: _dot(a[...], b[...]))
    in_specs.append(pl.BlockSpec((1, n), lambda i, j: (0, j)))
    return _matmul(name, (h, w_st, bias), in_specs, out, o_spec, (m // tm, nb),
                   lambda a, b, c: _dot(a[...], b[...]) + c[...])


def _split_rows(kf):
    g = max(1, kf // 1024)
    return g, kf // g


def mm_out(y, w, res, name):
    kf, n = w.shape
    m = y.shape[-2]
    tm = _rows(m, 512)
    row = pl.BlockSpec((tm, n), lambda i: (i, 0))
    if y.ndim == 3:
        g, _, k = y.shape
        ys = [y] * g
        y_specs = [pl.BlockSpec((None, tm, k), lambda i, r=r: (r, i, 0)) for r in range(g)]
    else:
        g, k = 1, kf
        ys, y_specs = [y], [pl.BlockSpec((tm, kf), lambda i: (i, 0))]

    def compute(*refs):
        w_ref, res_ref = refs[g], refs[g + 1]
        acc = res_ref[...]
        for r in range(g):
            acc = acc + _dot(refs[r][...], w_ref[r])
        return acc

    in_specs = y_specs + [pl.BlockSpec((g, k, n), lambda i: (0, 0, 0)), row]
    return _matmul(name, (*ys, w.reshape(g, k, n), res), in_specs, _sds((m, n)), row, (m // tm,), compute)


def mm_dx_in(dz, w_st, name, after=()):
    nb, k, n = w_st.shape
    m = dz.shape[-2]
    tm = _rows(m, 512)
    w_spec = pl.BlockSpec((nb, k, n), lambda i: (0, 0, 0))
    if dz.ndim == 3:
        in_specs = [pl.BlockSpec((None, tm, n), lambda i, r=r: (r, i, 0)) for r in range(nb)] + [w_spec]

        def compute(*refs):
            acc = _dot(refs[0][...], refs[nb][0], _NT)
            for r in range(1, nb):
                acc = acc + _dot(refs[r][...], refs[nb][r], _NT)
            return acc

        ins = (*[dz] * nb, w_st)
    else:
        in_specs = [pl.BlockSpec((tm, nb * n), lambda i: (i, 0)), w_spec]

        def compute(dz_ref, w_ref):
            acc = _dot(dz_ref[:, :n], w_ref[0], _NT)
            for r in range(1, nb):
                acc = acc + _dot(dz_ref[:, r * n:(r + 1) * n], w_ref[r], _NT)
            return acc

        ins = (dz, w_st)
    return _matmul(name, ins, in_specs, _sds((m, k)), pl.BlockSpec((tm, k), lambda i: (i, 0)), (m // tm,), compute, after)


def mm_dx_out(dout, w, name, groups=None, after=()):
    kf, n = w.shape
    m = dout.shape[0]
    tm = _rows(m, 1024)
    g, k = (groups, kf // groups) if groups else _split_rows(kf)
    in_specs = [pl.BlockSpec((tm, n), lambda i, j: (i, 0)), pl.BlockSpec((None, k, n), lambda i, j: (j, 0, 0))]
    if groups:
        out, o_spec = _sds((g, m, k)), pl.BlockSpec((None, tm, k), lambda i, j: (j, i, 0))
    else:
        out, o_spec = _sds((m, kf)), pl.BlockSpec((tm, k), lambda i, j: (i, j))
    return _matmul(name, (dout, w.reshape(g, k, n)), in_specs, out, o_spec, (m // tm, g),
                   lambda a, b: _dot(a[...], b[...], _NT), after)


def mm_dw_in(h, dz, nb, name, transposed=False):
    m, k = h.shape
    if dz.ndim == 3:
        n = dz.shape[2]
        dz_spec = pl.BlockSpec((None, m, n), lambda j: (j, 0, 0))
    else:
        n = dz.shape[1] // nb
        dz_spec = pl.BlockSpec((m, n), lambda j: (0, j))
    in_specs = [pl.BlockSpec((m, k), lambda j: (0, 0)), dz_spec]
    if transposed:
        return _matmul(name, (h, dz), in_specs, _sds((nb, n, k), BF16), pl.BlockSpec((None, n, k), lambda j: (j, 0, 0)), (nb,),
                       lambda a, b: _dot(b[...], a[...], _TN))
    return _matmul(name, (h, dz), in_specs, _sds((nb, k, n), BF16), pl.BlockSpec((None, k, n), lambda j: (j, 0, 0)), (nb,),
                   lambda a, b: _dot(a[...], b[...], _TN))


def mm_dw_out(y, dout, name):
    m, n = dout.shape
    if y.ndim == 3:
        g, _, k = y.shape
        y_spec = pl.BlockSpec((None, m, k), lambda j: (j, 0, 0))
    else:
        g, k = _split_rows(y.shape[1])
        y_spec = pl.BlockSpec((m, k), lambda j: (0, j))
    in_specs = [y_spec, pl.BlockSpec((m, n), lambda j: (0, 0))]
    out = _matmul(name, (y, dout), in_specs, _sds((g, k, n), BF16), pl.BlockSpec((None, k, n), lambda j: (j, 0, 0)), (g,),
                  lambda a, b: _dot(a[...], b[...], _TN))
    return out.reshape(g * k, n)


def rms_fwd(x, g, name):
    m, d = x.shape
    tm = _rows(m, 512)

    def body(x_ref, g_ref, o_ref):
        xv = x_ref[...]
        rstd = lax.rsqrt(jnp.mean(xv * xv, axis=-1, keepdims=True) + EPS)
        o_ref[...] = (xv * rstd * g_ref[...]).astype(BF16)

    row = pl.BlockSpec((tm, d), lambda i: (i, 0))
    vec = pl.BlockSpec((1, d), lambda i: (0, 0))
    return _call(body, name, (m // tm,), [row, vec], row, _sds((m, d), BF16))(x, g)


def _rms_bwd_math(xv, g, dh):
    rstd = lax.rsqrt(jnp.mean(xv * xv, axis=-1, keepdims=True) + EPS)
    xhat = xv * rstd
    dxhat = dh * g
    dx = rstd * (dxhat - xhat * jnp.mean(dxhat * xhat, axis=-1, keepdims=True))
    return dx, _colsum(dh * xhat)


def rms_bwd(x, g, dh, dres, name):
    m, d = x.shape
    tm = _rows(m, 512)

    def body(x_ref, g_ref, dh_ref, dr_ref, dx_ref, dxb_ref, dg_ref):
        dx, dg = _rms_bwd_math(x_ref[...], g_ref[...], dh_ref[...])
        dx = dr_ref[...] + dx
        dx_ref[...] = dx
        dxb_ref[...] = dx.astype(BF16)
        _acc(dg_ref, dg, pl.program_id(0) == 0)

    row = pl.BlockSpec((tm, d), lambda i: (i, 0))
    vec = pl.BlockSpec((1, d), lambda i: (0, 0))
    return _call(body, name, (m // tm,), [row, vec, row, row], [row, row, vec],
                 [_sds((m, d)), _sds((m, d), BF16), _sds((1, d))])(x, g, dh, dres)


def final_loss(x, g, target, name):
    m, d = x.shape
    tm = _rows(m, 512)

    def body(x_ref, g_ref, t_ref, l_ref, dx_ref, dxb_ref, dg_ref):
        xv, gv = x_ref[...], g_ref[...]
        rstd = lax.rsqrt(jnp.mean(xv * xv, axis=-1, keepdims=True) + EPS)
        err = xv * rstd * gv - t_ref[...]
        part = 0.5 * jnp.sum(jnp.mean(err * err, axis=-1, keepdims=True), axis=0, keepdims=True)
        dx, dg = _rms_bwd_math(xv, gv, err * (1.0 / d))
        dx_ref[...] = dx
        dxb_ref[...] = dx.astype(BF16)
        first = pl.program_id(0) == 0
        _acc(l_ref, jnp.broadcast_to(part, l_ref.shape), first)
        _acc(dg_ref, dg, first)

    row = pl.BlockSpec((tm, d), lambda i: (i, 0))
    vec = pl.BlockSpec((1, d), lambda i: (0, 0))
    lsp = pl.BlockSpec((1, LANES), lambda i: (0, 0))
    return _call(body, name, (m // tm,), [row, vec, row], [lsp, row, row, vec],
                 [_sds((1, LANES)), _sds((m, d)), _sds((m, d), BF16), _sds((1, d))])(x, g, target)


def _a_common(z_ref, vg_ref, ws_ref, bst_ref, tm, width):
    gw = width // A_GROUPS
    zp = z_ref[...]
    z = _gelu(zp)
    u, v = z[:, :width], z[:, width:]
    rstd = lax.rsqrt(jnp.mean(v * v, axis=-1, keepdims=True) + EPS)
    vhat = v * rstd
    vn = vhat * vg_ref[...]
    t_i = lax.broadcasted_iota(jnp.int32, (A_CHUNK, A_CHUNK), 0)
    s_i = lax.broadcasted_iota(jnp.int32, (A_CHUNK, A_CHUNK), 1)
    wsm = [jnp.where(s_i <= t_i, ws_ref[g], 0.0).astype(BF16) for g in range(A_GROUPS)]
    bst = bst_ref[...]
    return zp, u, rstd, vhat, vn.astype(BF16), wsm, bst, gw


def a_mid_fwd(z, vg, ws, bst, name):
    m, w2 = z.shape
    width = w2 // 2
    tm = _rows(m, 256)

    def body(z_ref, vg_ref, ws_ref, bst_ref, y_ref):
        _, u, _, _, vnb, wsm, bst, gw = _a_common(z_ref, vg_ref, ws_ref, bst_ref, tm, width)
        for c in range(tm // A_CHUNK):
            r0 = c * A_CHUNK
            for g in range(A_GROUPS):
                c0 = g * gw
                vs = _dot(wsm[g], vnb[r0:r0 + A_CHUNK, c0:c0 + gw]) + bst[:, g:g + 1]
                y_ref[r0:r0 + A_CHUNK, c0:c0 + gw] = (u[r0:r0 + A_CHUNK, c0:c0 + gw] * vs).astype(BF16)

    in_specs = [pl.BlockSpec((tm, w2), lambda i: (i, 0)), pl.BlockSpec((1, width), lambda i: (0, 0)),
                pl.BlockSpec((A_GROUPS, A_CHUNK, A_CHUNK), lambda i: (0, 0, 0)),
                pl.BlockSpec((A_CHUNK, A_GROUPS), lambda i: (0, 0))]
    return _call(body, name, (m // tm,), in_specs, pl.BlockSpec((tm, width), lambda i: (i, 0)),
                 _sds((m, width), BF16))(z, vg, ws, bst)


def a_mid_bwd(z, dy, vg, ws, bst, name, after=()):
    m, w2 = z.shape
    width = w2 // 2
    tm = _rows(m, 256)

    def body(z_ref, dy_ref, vg_ref, ws_ref, bst_ref, dz_ref, dbin_ref, dvg_ref, dws_ref, dbs_ref, dvn_scr, du_scr):
        first = pl.program_id(0) == 0
        zp, u, rstd, vhat, vnb, wsm, bst, gw = _a_common(z_ref, vg_ref, ws_ref, bst_ref, tm, width)
        dy = dy_ref[...]
        dws = [jnp.zeros((A_CHUNK, A_CHUNK), F32) for _ in range(A_GROUPS)]
        dbs = [jnp.zeros((A_CHUNK, 1), F32) for _ in range(A_GROUPS)]
        for c in range(tm // A_CHUNK):
            r0 = c * A_CHUNK
            for g in range(A_GROUPS):
                c0 = g * gw
                vn_cg = vnb[r0:r0 + A_CHUNK, c0:c0 + gw]
                vs = _dot(wsm[g], vn_cg) + bst[:, g:g + 1]
                dy_cg = dy[r0:r0 + A_CHUNK, c0:c0 + gw]
                dvs = dy_cg * u[r0:r0 + A_CHUNK, c0:c0 + gw]
                du_scr[r0:r0 + A_CHUNK, c0:c0 + gw] = dy_cg * vs
                dws[g] = dws[g] + _dot(dvs, vn_cg, _NT)
                dbs[g] = dbs[g] + jnp.sum(dvs, axis=1, keepdims=True)
                dvn_scr[r0:r0 + A_CHUNK, c0:c0 + gw] = _dot(wsm[g], dvs, _TN)
        for g in range(A_GROUPS):
            _acc(dws_ref.at[g], dws[g], first)
            _acc(dbs_ref.at[:, g * LANES:(g + 1) * LANES], jnp.broadcast_to(dbs[g], (A_CHUNK, LANES)), first)
        dvn = dvn_scr[...]
        _acc(dvg_ref, _colsum(dvn * vhat), first)
        dvhat = dvn * vg_ref[...]
        dv = rstd * (dvhat - vhat * jnp.mean(dvhat * vhat, axis=-1, keepdims=True))
        gg = _gelu_grad(zp)
        dzu = du_scr[...] * gg[:, :width]
        dzv = dv * gg[:, width:]
        dz_ref[:, :width] = dzu.astype(BF16)
        dz_ref[:, width:] = dzv.astype(BF16)
        _acc(dbin_ref.at[:, :width], _colsum(dzu), first)
        _acc(dbin_ref.at[:, width:], _colsum(dzv), first)

    const2 = lambda i: (0, 0)
    in_specs = [pl.BlockSpec((tm, w2), lambda i: (i, 0)), pl.BlockSpec((tm, width), lambda i: (i, 0)),
                pl.BlockSpec((1, width), const2), pl.BlockSpec((A_GROUPS, A_CHUNK, A_CHUNK), lambda i: (0, 0, 0)),
                pl.BlockSpec((A_CHUNK, A_GROUPS), const2)]
    out_specs = [pl.BlockSpec((tm, w2), lambda i: (i, 0)), pl.BlockSpec((1, w2), const2), pl.BlockSpec((1, width), const2),
                 pl.BlockSpec((A_GROUPS, A_CHUNK, A_CHUNK), lambda i: (0, 0, 0)),
                 pl.BlockSpec((A_CHUNK, A_GROUPS * LANES), const2)]
    out_shape = [_sds((m, w2), BF16), _sds((1, w2)), _sds((1, width)), _sds((A_GROUPS, A_CHUNK, A_CHUNK)),
                 _sds((A_CHUNK, A_GROUPS * LANES))]
    scratch = [pltpu.VMEM((tm, width), F32), pltpu.VMEM((tm, width), F32)]
    return _call(body, name, (m // tm,), in_specs, out_specs, out_shape, scratch, after)(z, dy, vg, ws, bst)


def _pool_minus_id(ze, i, tm, gw):
    pos = i * tm + lax.broadcasted_iota(jnp.int32, (tm, 1), 0)
    out = []
    for gi, win in enumerate(B_WINDOWS):
        s = ze[:, gi * gw:(gi + 1) * gw]
        step = 1
        while step < win:
            s = s + _down(s, step)
            step *= 2
        inv = 1.0 / jnp.minimum(pos + 1, win).astype(F32)
        out.append(s[POOL_HALO:] * inv - ze[POOL_HALO:, gi * gw:(gi + 1) * gw])
    return out


def _b_specs(tm, width):
    return [pl.BlockSpec((POOL_HALO, width), lambda i: (_prev_halo(tm, POOL_HALO)(i), 0)),
            pl.BlockSpec((tm, width), lambda i: (i, 0))]


def b_mid_fwd(z, wgrp, bgrp, scale, name):
    m, width = z.shape
    ng = len(B_WINDOWS)
    gw = width // ng
    tm = _rows(m, 512)

    def body(zp_ref, zm_ref, w_ref, b_ref, s_ref, y_ref):
        i = pl.program_id(0)
        ze = jnp.concatenate([zp_ref[...] * (i > 0).astype(F32), zm_ref[...]], axis=0)
        p = _pool_minus_id(ze, i, tm, gw)
        for g in range(ng):
            cs = slice(g * gw, (g + 1) * gw)
            y = (_dot(p[g], w_ref[g]) + b_ref[:, cs]) * s_ref[:, cs]
            y_ref[:, cs] = y.astype(BF16)

    vec = pl.BlockSpec((1, width), lambda i: (0, 0))
    in_specs = _b_specs(tm, width) + [pl.BlockSpec((ng, gw, gw), lambda i: (0, 0, 0)), vec, vec]
    return _call(body, name, (m // tm,), in_specs, pl.BlockSpec((tm, width), lambda i: (i, 0)),
                 _sds((m, width), BF16))(z, z, wgrp, bgrp, scale)


def b_mid_bwd(z, dy, wgrp, bgrp, scale, name):
    m, width = z.shape
    ng = len(B_WINDOWS)
    gw = width // ng
    tm = _rows(m, 512)

    def body(zp_ref, zm_ref, dy_ref, w_ref, b_ref, s_ref, dp_ref, dw_ref, db_ref, ds_ref):
        i = pl.program_id(0)
        first = i == 0
        ze = jnp.concatenate([zp_ref[...] * (i > 0).astype(F32), zm_ref[...]], axis=0)
        p = _pool_minus_id(ze, i, tm, gw)
        for g in range(ng):
            cs = slice(g * gw, (g + 1) * gw)
            dyg = dy_ref[:, cs]
            ypre = _dot(p[g], w_ref[g]) + b_ref[:, cs]
            dyp = dyg * s_ref[:, cs]
            _acc(ds_ref.at[:, cs], _colsum(dyg * ypre), first)
            _acc(db_ref.at[:, cs], _colsum(dyp), first)
            _acc(dw_ref.at[g], _dot(p[g], dyp, _TN), first)
            dp_ref[:, cs] = _dot(dyp, w_ref[g], _NT)

    vec = pl.BlockSpec((1, width), lambda i: (0, 0))
    row = pl.BlockSpec((tm, width), lambda i: (i, 0))
    wsp = pl.BlockSpec((ng, gw, gw), lambda i: (0, 0, 0))
    return _call(body, name, (m // tm,), _b_specs(tm, width) + [row, wsp, vec, vec], [row, wsp, vec, vec],
                 [_sds((m, width)), _sds((ng, gw, gw)), _sds((1, width)), _sds((1, width))])(z, z, dy, wgrp, bgrp, scale)


def b_pool_bwd(dp, name):
    m, width = dp.shape
    gw = width // len(B_WINDOWS)
    tm = _rows(m, 512)
    n_i = m // tm

    def body(dm_ref, dn_ref, dz_ref):
        i = pl.program_id(0)
        de = jnp.concatenate([dm_ref[...], dn_ref[...] * (i < n_i - 1).astype(F32)], axis=0)
        pos = i * tm + lax.broadcasted_iota(jnp.int32, (tm + POOL_HALO, 1), 0)
        for gi, win in enumerate(B_WINDOWS):
            cs = slice(gi * gw, (gi + 1) * gw)
            d = de[:, cs]
            s = d * (1.0 / jnp.minimum(pos + 1, win).astype(F32))
            step = 1
            while step < win:
                s = s + _up(s, step)
                step *= 2
            dz_ref[:, cs] = (s[:tm] - d[:tm]).astype(BF16)

    in_specs = [pl.BlockSpec((tm, width), lambda i: (i, 0)),
                pl.BlockSpec((POOL_HALO, width), lambda i: (_next_halo(tm, m, POOL_HALO)(i), 0))]
    return _call(body, name, (n_i,), in_specs, pl.BlockSpec((tm, width), lambda i: (i, 0)), _sds((m, width), BF16))(dp, dp)


def _c_gates(xr, wa_ref, ba_ref, wi_ref, bi_ref, lam_ref, heads, hw):
    xb = xr.astype(BF16)
    ra = jnp.concatenate([_dot(xb[:, h * hw:(h + 1) * hw], wa_ref[h]) for h in range(heads)], axis=1) + ba_ref[...]
    ia = jnp.concatenate([_dot(xb[:, h * hw:(h + 1) * hw], wi_ref[h]) for h in range(heads)], axis=1) + bi_ref[...]
    r, ig = _sigmoid(ra), _sigmoid(ia)
    sp = _softplus(-lam_ref[...])
    log_a = (-C_GATE_C * r) * sp
    a = jnp.exp(log_a)
    mult = jnp.sqrt(-_expm1(2.0 * log_a))
    return xb, r, ig, sp, a, mult


def c_mid_fwd(z, cw, cb, wa, ba, wi, bi, lam, name):
    m, w2 = z.shape
    width = w2 // 2
    heads, hw = wa.shape[0], wa.shape[1]
    taps = cw.shape[0]
    tm = _rows(m, 512)

    def body(zp_ref, zm_ref, cw_ref, cb_ref, wa_ref, ba_ref, wi_ref, bi_ref, lam_ref, a_ref, b_ref, xr_ref):
        i = pl.program_id(0)
        xe = jnp.concatenate([zp_ref[...] * (i > 0).astype(F32), zm_ref[...]], axis=0)
        xr = _conv_ext(xe, cw_ref[...], taps)[HALO:] + cb_ref[...]
        _, _, ig, _, a, mult = _c_gates(xr, wa_ref, ba_ref, wi_ref, bi_ref, lam_ref, heads, hw)
        a_ref[...] = a
        b_ref[...] = mult * (ig * xr)
        xr_ref[...] = xr

    vec = pl.BlockSpec((1, width), lambda i: (0, 0))
    row = pl.BlockSpec((tm, width), lambda i: (i, 0))
    wsp = pl.BlockSpec((heads, hw, hw), lambda i: (0, 0, 0))
    in_specs = [pl.BlockSpec((HALO, width), lambda i: (_prev_halo(tm)(i), 1)), pl.BlockSpec((tm, width), lambda i: (i, 1)),
                pl.BlockSpec((taps, width), lambda i: (0, 0)), vec, wsp, vec, wsp, vec, vec]
    return _call(body, name, (m // tm,), in_specs, [row, row, row], [_sds((m, width))] * 3)(
        z, z, cw, cb, wa, ba, wi, bi, lam)


_SCAN_ROWS = 512


def c_scan_fwd(a, b, z, name):
    m, width = a.shape
    tm = _rows(m, _SCAN_ROWS)

    def body(a_ref, b_ref, g_ref, hs_ref, y_ref, h_carry):
        @pl.when(pl.program_id(0) == 0)
        def _():
            h_carry[...] = jnp.zeros_like(h_carry)

        def step(t, h):
            h = a_ref[pl.ds(t, 1), :] * h + b_ref[pl.ds(t, 1), :]
            hs_ref[pl.ds(t, 1), :] = h
            return h

        h_carry[...] = lax.fori_loop(0, tm, step, h_carry[...], unroll=8)
        y_ref[...] = (hs_ref[...] * _gelu(g_ref[...])).astype(BF16)

    row = pl.BlockSpec((tm, width), lambda i: (i, 0))
    return _call(body, name, (m // tm,), [row, row, row], [row, row], [_sds((m, width)), _sds((m, width), BF16)],
                 [pltpu.VMEM((1, width), F32)])(a, b, z)


def c_scan_bwd(dy, z, hs, a, name):
    m, width = a.shape
    tm = _rows(m, _SCAN_ROWS)
    n_i = m // tm

    def body(dy_ref, g_ref, hs_ref, hp_ref, a_ref, lam_ref, da_ref, dg_ref, dgs_ref, lam_carry, a_carry):
        i = pl.program_id(0)
        first = i == 0

        @pl.when(first)
        def _():
            lam_carry[...] = jnp.zeros_like(lam_carry)
            a_carry[...] = jnp.zeros_like(a_carry)

        gp, dyv, hsv = g_ref[...], dy_ref[...], hs_ref[...]
        dgate = dyv * hsv * _gelu_grad(gp)
        dg_ref[...] = dgate.astype(BF16)
        _acc(dgs_ref, _colsum(dgate), first)
        lam_ref[...] = dyv * _gelu(gp)

        def step(k, carry):
            lam_next, a_next = carry
            t = tm - 1 - k
            lam_t = lam_ref[pl.ds(t, 1), :] + a_next * lam_next
            lam_ref[pl.ds(t, 1), :] = lam_t
            return lam_t, a_ref[pl.ds(t, 1), :]

        lam_c, a_c = lax.fori_loop(0, tm, step, (lam_carry[...], a_carry[...]), unroll=8)
        lam_carry[...] = lam_c
        a_carry[...] = a_c
        h_before = hp_ref[HALO - 1:HALO, :] * (i < n_i - 1).astype(F32)
        t_i = lax.broadcasted_iota(jnp.int32, (tm, 1), 0)
        da_ref[...] = lam_ref[...] * jnp.where(t_i == 0, h_before, _down(hsv, 1))

    row = pl.BlockSpec((tm, width), lambda i: (n_i - 1 - i, 0))
    halo = pl.BlockSpec((HALO, width), lambda i: (_prev_halo(tm)(n_i - 1 - i), 0))
    vec = pl.BlockSpec((1, width), lambda i: (0, 0))
    return _call(body, name, (n_i,), [row, row, row, halo, row], [row, row, row, vec],
                 [_sds((m, width)), _sds((m, width)), _sds((m, width), BF16), _sds((1, width))],
                 [pltpu.VMEM((1, width), F32), pltpu.VMEM((1, width), F32)])(dy, z, hs, hs, a)


def c_mid_bwd(lam_seq, da, xr, wa, ba, wi, bi, lam, name):
    m, width = xr.shape
    heads, hw = wa.shape[0], wa.shape[1]
    tm = _rows(m, 512)

    def body(l_ref, da_ref, xr_ref, wa_ref, ba_ref, wi_ref, bi_ref, lam_ref,
             dxr_ref, dwa_ref, dwi_ref, dba_ref, dbi_ref, dlam_ref):
        first = pl.program_id(0) == 0
        xr_v, lmb = xr_ref[...], l_ref[...]
        xb, r, ig, sp, a, mult = _c_gates(xr_v, wa_ref, ba_ref, wi_ref, bi_ref, lam_ref, heads, hw)
        dmult = lmb * (ig * xr_v)
        dig = lmb * mult * xr_v
        dxr = lmb * mult * ig
        dla = da_ref[...] * a - dmult * (a * a) / mult
        dr = dla * (-C_GATE_C * sp)
        dsp = _colsum(dla * (-C_GATE_C * r))
        _acc(dlam_ref, dsp * (-_sigmoid(-lam_ref[...])), first)
        dra = dr * r * (1.0 - r)
        dia = dig * ig * (1.0 - ig)
        _acc(dba_ref, _colsum(dra), first)
        _acc(dbi_ref, _colsum(dia), first)
        for h in range(heads):
            cs = slice(h * hw, (h + 1) * hw)
            _acc(dwa_ref.at[h], _dot(xb[:, cs], dra[:, cs], _TN), first)
            _acc(dwi_ref.at[h], _dot(xb[:, cs], dia[:, cs], _TN), first)
            dxr_ref[:, cs] = dxr[:, cs] + _dot(dra[:, cs], wa_ref[h], _NT) + _dot(dia[:, cs], wi_ref[h], _NT)

    vec = pl.BlockSpec((1, width), lambda i: (0, 0))
    row = pl.BlockSpec((tm, width), lambda i: (i, 0))
    wsp = pl.BlockSpec((heads, hw, hw), lambda i: (0, 0, 0))
    return _call(body, name, (m // tm,), [row, row, row, wsp, vec, wsp, vec, vec], [row, wsp, wsp, vec, vec, vec],
                 [_sds((m, width)), _sds((heads, hw, hw)), _sds((heads, hw, hw)), _sds((1, width)), _sds((1, width)),
                  _sds((1, width))])(lam_seq, da, xr, wa, ba, wi, bi, lam)


def conv_bwd(dy, x_src, col_block, cw, name):
    m, width = dy.shape
    taps = cw.shape[0]
    tm = _rows(m, 512)
    n_i = m // tm

    def body(dm_ref, dn_ref, xp_ref, xm_ref, cw_ref, dx_ref, dw_ref, db_ref, dxs_ref):
        i = pl.program_id(0)
        first = i == 0
        de = jnp.concatenate([jnp.zeros((HALO, width), F32), dm_ref[...], dn_ref[...] * (i < n_i - 1).astype(F32)], axis=0)
        xe = jnp.concatenate([xp_ref[...] * (i > 0).astype(F32), xm_ref[...], jnp.zeros((HALO, width), F32)], axis=0)
        w = cw_ref[...]
        dx = de * w[taps - 1:taps]
        for s in range(1, taps):
            dx = dx + _up(de, s) * w[taps - 1 - s:taps - s]
        dx_ref[...] = dx[HALO:HALO + tm].astype(BF16)
        _acc(dxs_ref, _colsum(dx[HALO:HALO + tm]), first)
        dm = dm_ref[...]
        for s in range(taps):
            _acc(dw_ref.at[taps - 1 - s:taps - s, :], _colsum(dm * _down(xe, s)[HALO:HALO + tm]), first)
        _acc(db_ref, _colsum(dm), first)

    row = pl.BlockSpec((tm, width), lambda i: (i, 0))
    vec = pl.BlockSpec((1, width), lambda i: (0, 0))
    tsp = pl.BlockSpec((taps, width), lambda i: (0, 0))
    in_specs = [row, pl.BlockSpec((HALO, width), lambda i: (_next_halo(tm, m)(i), 0)),
                pl.BlockSpec((HALO, width), lambda i: (_prev_halo(tm)(i), col_block)),
                pl.BlockSpec((tm, width), lambda i: (i, col_block)), tsp]
    return _call(body, name, (n_i,), in_specs, [row, tsp, vec, vec],
                 [_sds((m, width), BF16), _sds((taps, width)), _sds((1, width)), _sds((1, width))])(dy, dy, x_src, x_src, cw)


def d_mid_fwd(z, cw, name):
    m, w3 = z.shape
    width = w3 // 3
    taps = cw.shape[0]
    tm = _rows(m, 512)

    def body(bm_ref, cp_ref, cm_ref, xp_ref, xm_ref, cw_ref, y_ref):
        keep = (pl.program_id(0) > 0).astype(F32)
        qe = (jnp.concatenate([cp_ref[...] * keep, cm_ref[...]], axis=0)
              * jnp.concatenate([xp_ref[...], xm_ref[...]], axis=0))
        y_ref[...] = (bm_ref[...] * _conv_ext(qe, cw_ref[...], taps)[HALO:]).astype(BF16)

    main = lambda c: pl.BlockSpec((tm, width), lambda i: (i, c))
    prev = lambda c: pl.BlockSpec((HALO, width), lambda i: (_prev_halo(tm)(i), c))
    in_specs = [main(0), prev(1), main(1), prev(2), main(2), pl.BlockSpec((taps, width), lambda i: (0, 0))]
    return _call(body, name, (m // tm,), in_specs, pl.BlockSpec((tm, width), lambda i: (i, 0)),
                 _sds((m, width), BF16))(z, z, z, z, z, cw)


def d_mid_bwd(z, dy, cw, name):
    m, w3 = z.shape
    width = w3 // 3
    taps = cw.shape[0]
    tm = _rows(m, 512)
    n_i = m // tm

    def body(bm_ref, bn_ref, cp_ref, cm_ref, cn_ref, xp_ref, xm_ref, xn_ref, dm_ref, dn_ref, cw_ref, dz_ref, dw_ref):
        i = pl.program_id(0)
        first = i == 0
        kp, kn = (i > 0).astype(F32), (i < n_i - 1).astype(F32)
        zeros = jnp.zeros((HALO, width), F32)
        ce = jnp.concatenate([cp_ref[...] * kp, cm_ref[...], cn_ref[...] * kn], axis=0)
        xe = jnp.concatenate([xp_ref[...], xm_ref[...], xn_ref[...]], axis=0)
        qe = ce * xe
        be = jnp.concatenate([zeros, bm_ref[...], bn_ref[...]], axis=0)
        dye = jnp.concatenate([zeros, dm_ref[...], dn_ref[...] * kn], axis=0)
        w = cw_ref[...]
        cq = _conv_ext(qe, w, taps)
        dcq = dye * be
        dq = dcq * w[taps - 1:taps]
        for s in range(1, taps):
            dq = dq + _up(dcq, s) * w[taps - 1 - s:taps - s]
        ms = slice(HALO, HALO + tm)
        dz_ref[:, :width] = (dye * cq)[ms].astype(BF16)
        dz_ref[:, width:2 * width] = (dq * xe)[ms].astype(BF16)
        dz_ref[:, 2 * width:] = (dq * ce)[ms].astype(BF16)
        for s in range(taps):
            _acc(dw_ref.at[taps - 1 - s:taps - s, :], _colsum(dcq[ms] * _down(qe, s)[ms]), first)

    main = lambda c: pl.BlockSpec((tm, width), lambda i: (i, c))
    prev = lambda c: pl.BlockSpec((HALO, width), lambda i: (_prev_halo(tm)(i), c))
    nxt = lambda c: pl.BlockSpec((HALO, width), lambda i: (_next_halo(tm, m)(i), c))
    tsp = pl.BlockSpec((taps, width), lambda i: (0, 0))
    in_specs = [main(0), nxt(0), prev(1), main(1), nxt(1), prev(2), main(2), nxt(2), main(0), nxt(0), tsp]
    return _call(body, name, (n_i,), in_specs, [pl.BlockSpec((tm, w3), lambda i: (i, 0)), tsp],
                 [_sds((m, w3), BF16), _sds((taps, width))])(z, z, z, z, z, z, z, z, dy, dy, cw)


def _halo_rows(dtype):
    return HALO * (4 // jnp.dtype(dtype).itemsize)


def ffn_mid_fwd(z, cw, cb, name):
    _, nj, m, c = z.shape
    taps = cw.shape[2]
    tm = _rows(m, 512)
    hz = _halo_rows(z.dtype)

    def body(zp_ref, zm_ref, cw_ref, cb_ref, o_ref):
        keep = (pl.program_id(1) > 0).astype(F32)
        zc = []
        for s in range(2):
            xe = jnp.concatenate([zp_ref[s].astype(F32) * keep, zm_ref[s].astype(F32)], axis=0)
            zc.append(_conv_ext(xe, cw_ref[s], taps)[hz:] + cb_ref[s])
        o_ref[...] = (zc[0] * _sigmoid(zc[0]) * zc[1]).astype(BF16)

    in_specs = [pl.BlockSpec((2, None, hz, c), lambda j, i: (0, j, _prev_halo(tm, hz)(i), 0)),
                pl.BlockSpec((2, None, tm, c), lambda j, i: (0, j, i, 0)),
                pl.BlockSpec((2, None, taps, c), lambda j, i: (0, j, 0, 0)),
                pl.BlockSpec((2, None, 1, c), lambda j, i: (0, j, 0, 0))]
    return _call(body, name, (nj, m // tm), in_specs, pl.BlockSpec((None, tm, c), lambda j, i: (j, i, 0)),
                 _sds((nj, m, c), BF16))(z, z, cw, cb)


def ffn_mid_bwd(z, dact, cw, cb, name):
    _, nj, m, c = z.shape
    taps = cw.shape[2]
    tm = _rows(m, 512)
    n_i = m // tm
    hz = _halo_rows(z.dtype)

    def body(zp_ref, zm_ref, zn_ref, dm_ref, dn_ref, cw_ref, cb_ref, dz_ref, dw_ref, db_ref):
        i = pl.program_id(1)
        first = i == 0
        kp, kn = (i > 0).astype(F32), (i < n_i - 1).astype(F32)
        xe = [jnp.concatenate([zp_ref[s].astype(F32) * kp, zm_ref[s].astype(F32), zn_ref[s].astype(F32) * kn], axis=0)
              for s in range(2)]
        zc = [_conv_ext(xe[s], cw_ref[s], taps) + cb_ref[s] for s in range(2)]
        dae = jnp.concatenate([jnp.zeros((hz, c), F32), dm_ref[...], dn_ref[...] * kn]
                              + ([jnp.zeros((hz - HALO, c), F32)] if hz > HALO else []), axis=0)
        sg = _sigmoid(zc[0])
        dzc = [dae * zc[1] * (sg * (1.0 + zc[0] * (1.0 - sg))), dae * (zc[0] * sg)]
        ms = slice(hz, hz + tm)
        for s in range(2):
            w = cw_ref[s]
            ups = [dzc[s]] + [_up(dzc[s], u) for u in range(1, taps)]
            dx = ups[0] * w[taps - 1:taps]
            for u in range(1, taps):
                dx = dx + ups[u] * w[taps - 1 - u:taps - u]
            dz_ref[s] = dx[ms].astype(BF16)
            tail = dzc[s][hz + tm:hz + tm + HALO]
            x_end = xe[s][hz + tm - HALO:hz + tm + HALO]
            for u in range(taps):
                total = _colsum(ups[u] * xe[s]) - _colsum(tail * _down(x_end, u)[HALO:])
                _acc(dw_ref.at[s, taps - 1 - u:taps - u, :], total, first)
            _acc(db_ref.at[s], _colsum(dzc[s][ms]), first)

    in_specs = [pl.BlockSpec((2, None, hz, c), lambda j, i: (0, j, _prev_halo(tm, hz)(i), 0)),
                pl.BlockSpec((2, None, tm, c), lambda j, i: (0, j, i, 0)),
                pl.BlockSpec((2, None, hz, c), lambda j, i: (0, j, _next_halo(tm, m, hz)(i), 0)),
                pl.BlockSpec((None, tm, c), lambda j, i: (j, i, 0)),
                pl.BlockSpec((None, HALO, c), lambda j, i: (j, _next_halo(tm, m)(i), 0)),
                pl.BlockSpec((2, None, taps, c), lambda j, i: (0, j, 0, 0)),
                pl.BlockSpec((2, None, 1, c), lambda j, i: (0, j, 0, 0))]
    out_specs = [pl.BlockSpec((2, None, tm, c), lambda j, i: (0, j, i, 0)),
                 pl.BlockSpec((2, None, taps, c), lambda j, i: (0, j, 0, 0)),
                 pl.BlockSpec((2, None, 1, c), lambda j, i: (0, j, 0, 0))]
    return _call(body, name, (nj, n_i), in_specs, out_specs,
                 [_sds((2, nj, m, c), BF16), _sds((2, nj, taps, c)), _sds((2, nj, 1, c))])(z, z, z, dact, dact, cw, cb)


_STREAM_TILE_BYTES = 2 << 20


def _tile_rows(r, c):
    if r * c * 4 <= _STREAM_TILE_BYTES:
        return r
    fits = [d for d in range(16, r, 16) if r % d == 0 and d * c * 4 <= _STREAM_TILE_BYTES]
    return max(fits) if fits else r


def _as2d(a, lead):
    shape = a.shape
    return a.reshape((lead, -1, shape[-1]) if lead else (-1, shape[-1]))


def add_pairs(g, l1, own, name):
    shape = l1.shape
    g3, l3 = _as2d(g, N_DEV), _as2d(l1, 4)
    _, r, c = l3.shape
    tr = _tile_rows(r, c)

    def body(own_ref, a_ref, b_ref, o_ref):
        o_ref[...] = (a_ref[...].astype(F32) + b_ref[...].astype(F32)).astype(o_ref.dtype)

    spec = pl.BlockSpec((None, tr, c), lambda k, i, own_ref: (k, i, 0))
    grid_spec = pltpu.PrefetchScalarGridSpec(
        num_scalar_prefetch=1, grid=(4, r // tr),
        in_specs=[pl.BlockSpec((None, tr, c), lambda k, i, own_ref: (own_ref[k], i, 0)), spec], out_specs=spec)
    out = pl.pallas_call(
        body, name=name, grid_spec=grid_spec, out_shape=_sds(l3.shape, l1.dtype),
        compiler_params=pltpu.CompilerParams(dimension_semantics=("arbitrary", "arbitrary"),
                                             vmem_limit_bytes=V7X_VMEM_LIMIT_BYTES))(own, g3, l3)
    return out.reshape(shape)


def _grad_sum(p_ref, l_ref):
    return ((p_ref[...].astype(F32) + l_ref[0].astype(F32)) + l_ref[1].astype(F32)) + l_ref[2].astype(F32)


def sum_parts(p, l2, name):
    _, r, c = p.shape

    def body(p_ref, l_ref, o_ref):
        o_ref[...] = _grad_sum(p_ref, l_ref)

    return _call(body, name, (1,), [pl.BlockSpec((None, r, c), lambda i: (0, 0, 0)), pl.BlockSpec((3, r, c), lambda i: (0, 0, 0))],
                 pl.BlockSpec((r, c), lambda i: (0, 0)), _sds((r, c)))(p, l2)


def _adamw_math(w, g, m, v):
    m = ADAM_B1 * m + (1.0 - ADAM_B1) * g
    v = ADAM_B2 * v + (1.0 - ADAM_B2) * (g * g)
    m_hat = m / (1.0 - ADAM_B1 ** ADAM_STEP)
    v_hat = v / (1.0 - ADAM_B2 ** ADAM_STEP)
    delta = -ADAM_LR * (m_hat / (jnp.sqrt(v_hat) + ADAM_EPS) + ADAM_WD * w)
    return delta, m, v


def adamw(w, m, v, name, g=None, p=None, l2=None):
    shape = w.shape
    w2, m2, v2 = (_as2d(t, 0) for t in (w, m, v))
    r, c = w2.shape
    tr = _tile_rows(r, c)
    row = pl.BlockSpec((tr, c), lambda i: (i, 0))
    if g is None:
        p3, l3 = _as2d(p, 4), _as2d(l2, 3)
        gin = (p3, l3)
        gspecs = [pl.BlockSpec((None, tr, c), lambda i: (0, i, 0)), pl.BlockSpec((3, tr, c), lambda i: (0, i, 0))]
    else:
        gin, gspecs = (_as2d(g, 0),), [row]

    def body(*refs):
        n_g = len(gin)
        w_ref, m_ref, v_ref, g_ref, d_ref, nm_ref, nv_ref = refs[n_g:]
        grad = refs[0][...] if n_g == 1 else _grad_sum(refs[0], refs[1])
        delta, nm, nv = _adamw_math(w_ref[...], grad, m_ref[...], v_ref[...])
        g_ref[...] = grad
        d_ref[...] = delta
        nm_ref[...] = nm
        nv_ref[...] = nv

    outs = _call(body, name, (r // tr,), gspecs + [row, row, row], [row] * 4, [_sds((r, c))] * 4)(*gin, w2, m2, v2)
    return tuple(o.reshape(shape) for o in outs)


def adamw_layer(w, m, v, layer, prev, name, p, l2):
    n_l, r, c = w.shape
    tr = _tile_rows(r, c)
    slab = pl.BlockSpec((None, tr, c), lambda i: (layer, i, 0))
    in_specs = [pl.BlockSpec((None, tr, c), lambda i: (0, i, 0)), pl.BlockSpec((3, tr, c), lambda i: (0, i, 0)), slab, slab, slab]
    n_in = len(in_specs)
    prev = () if prev is None else tuple(prev)

    def body(p_ref, l_ref, w_ref, m_ref, v_ref, *rest):
        g_ref, d_ref, nm_ref, nv_ref = rest[len(prev):]
        grad = _grad_sum(p_ref, l_ref)
        delta, nm, nv = _adamw_math(w_ref[...], grad, m_ref[...], v_ref[...])
        g_ref[...] = grad
        d_ref[...] = delta
        nm_ref[...] = nm
        nv_ref[...] = nv

    return pl.pallas_call(
        body, name=name, grid=(r // tr,), in_specs=in_specs + [pl.BlockSpec(memory_space=pl.ANY)] * len(prev),
        out_specs=[slab] * 4, out_shape=[_sds((n_l, r, c))] * 4,
        input_output_aliases={n_in + q: q for q in range(len(prev))},
        compiler_params=pltpu.CompilerParams(dimension_semantics=("arbitrary",), vmem_limit_bytes=V7X_VMEM_LIMIT_BYTES),
    )(_as2d(p, 4), _as2d(l2, 3), w, m, v, *prev)


def _comm_call(body, name, ins, out_shape, n_sems):
    any_spec = pl.BlockSpec(memory_space=pl.ANY)
    return pl.pallas_call(
        body, name=name, in_specs=[any_spec] * len(ins), out_specs=[any_spec] * len(out_shape), out_shape=out_shape,
        scratch_shapes=[pltpu.SemaphoreType.DMA((n,)) for n in n_sems],
        compiler_params=pltpu.CompilerParams(has_side_effects=True))(*ins)


def _place():
    return lax.axis_index("x"), lax.axis_index("y"), lax.axis_index("c")


def _dev_index(px, py, pc):
    return 4 * px + 2 * py + pc


def all_gather(blocks, name):
    n_t = len(blocks)

    def body(*refs):
        ins, outs = refs[:n_t], refs[n_t:2 * n_t]
        send_sems, recv_sems, local_sems = refs[2 * n_t:]
        x, y, c = _place()
        me, sibling = (x, y, c), (x, y, 1 - c)
        chips = [(1 - x, y), (x, 1 - y), (1 - x, 1 - y)]

        def copy(t, k, block, to, src=None):
            dst = outs[t].at[_dev_index(*block)]
            return pltpu.make_async_remote_copy(
                src_ref=dst if src is None else src, dst_ref=dst, send_sem=send_sems.at[t * 7 + k],
                recv_sem=recv_sems.at[t * 7 + k], device_id=to, device_id_type=MESH_ID)

        mine = [pltpu.make_async_copy(ins[t], outs[t].at[_dev_index(*me)], local_sems.at[t]) for t in range(n_t)]
        for cp in mine:
            cp.start()
        first = []
        for t in range(n_t):
            first.append(copy(t, 0, me, sibling, src=ins[t]))
            first += [copy(t, 1 + j, me, (*chip, c), src=ins[t]) for j, chip in enumerate(chips)]
        for cp in first:
            cp.start()
        passed = []
        for t in range(n_t):
            for j, chip in enumerate(chips):
                copy(t, 1 + j, (*chip, c), me).wait_recv()
                cp = copy(t, 4 + j, (*chip, c), sibling)
                cp.start()
                passed.append(cp)
        for t in range(n_t):
            copy(t, 0, sibling, me).wait_recv()
            for j, chip in enumerate(chips):
                copy(t, 4 + j, (*chip, 1 - c), me).wait_recv()
        for cp in first + passed:
            cp.wait_send()
        for cp in mine:
            cp.wait()

    out_shape = [_sds((N_DEV,) + b.shape, b.dtype) for b in blocks]
    return _comm_call(body, name, blocks, out_shape, (7 * n_t, 7 * n_t, n_t))


def _chip_of(x, y, k):
    return (x if k % 2 == 0 else 1 - x), (y if k // 2 == 0 else 1 - y)


_HBM_SPEC = pl.BlockSpec(memory_space=pltpu.HBM)
_SEM_SPEC = pl.BlockSpec(memory_space=pltpu.SEMAPHORE)
_DATAFLOW = pltpu.SideEffectType.DATAFLOW_SIDE_EFFECTING


def _in_hbm(a):
    return pltpu.with_memory_space_constraint(a, pltpu.HBM)


def _split_start(name, issue, srcs, land_shapes, sem_counts):
    n_buf, n_sem = len(srcs) + len(land_shapes), len(sem_counts)

    def body(*refs):
        issue(refs[:len(srcs)], refs[len(srcs):n_buf], refs[n_buf:n_buf + n_sem])
        refs[-1][...] = jnp.zeros_like(refs[-1])

    bufs = [pltpu.HBM(s.shape, s.dtype) for s in list(srcs) + list(land_shapes)]
    outs = pl.pallas_call(
        body, name=name, in_specs=(_HBM_SPEC,) * n_buf,
        out_shape=(*[pltpu.SemaphoreType.DMA((n,)) for n in sem_counts], *bufs, _sds((8, LANES))),
        out_specs=(*[_SEM_SPEC] * n_sem, *[_HBM_SPEC] * n_buf, pl.BlockSpec(memory_space=pltpu.VMEM)),
        input_output_aliases={i: n_sem + i for i in range(n_buf)},
        compiler_params=pltpu.CompilerParams(has_side_effects=_DATAFLOW),
    )(*[_in_hbm(s) for s in srcs], *[_in_hbm(lax.empty(s.shape, s.dtype)) for s in land_shapes])
    return outs[:n_sem], outs[n_sem:n_sem + len(srcs)], outs[n_sem + len(srcs):n_sem + n_buf], outs[-1]


def _split_wait(name, finish, sems, srcs, lands, after):
    n_buf, n_sem = len(srcs) + len(lands), len(sems)

    def body(*refs):
        finish(refs[:len(srcs)], refs[len(srcs):n_buf], refs[n_buf:n_buf + n_sem])

    bufs = [pltpu.HBM(s.shape, s.dtype) for s in list(srcs) + list(lands)]
    outs = pl.pallas_call(
        body, name=name, in_specs=(*[_HBM_SPEC] * n_buf, *[_SEM_SPEC] * n_sem, *[pl.BlockSpec(memory_space=pl.ANY)] * len(after)),
        out_shape=tuple(bufs), out_specs=(_HBM_SPEC,) * n_buf, input_output_aliases={i: i for i in range(n_buf)},
        compiler_params=pltpu.CompilerParams(has_side_effects=_DATAFLOW),
    )(*srcs, *lands, *sems, *after)
    return outs[:len(srcs)], outs[len(srcs):]


def _peer(x, y, c, r):
    return (1 - x if r & 4 else x), (1 - y if r & 2 else y), (1 - c if r & 1 else c)


def _gather_copies(src_refs, land_refs, sem_refs, arrivals):
    send_sems, recv_sems, local_sems = sem_refs
    x, y, c = _place()
    me = _dev_index(x, y, c)
    local, sends, recvs = [], [], []
    for j, (src, land) in enumerate(zip(src_refs, land_refs)):
        local.append(pltpu.make_async_copy(src, land.at[me], local_sems.at[j]))
        for r in range(1, N_DEV):
            peer = _peer(x, y, c, r)
            q = (N_DEV - 1) * j + r - 1
            sends.append(pltpu.make_async_remote_copy(src_ref=src, dst_ref=land.at[me], send_sem=send_sems.at[q],
                                                      recv_sem=recv_sems.at[q], device_id=peer, device_id_type=MESH_ID))
            if arrivals:
                recvs.append(pltpu.make_async_remote_copy(
                    src_ref=src, dst_ref=land.at[_dev_index(*peer)], send_sem=send_sems.at[q], recv_sem=recv_sems.at[q],
                    device_id=peer, device_id_type=MESH_ID))
    return local, sends, recvs


def gather_start(groups, name):
    flat = [b for g in groups for b in g]
    bounds = [sum(len(g) for g in groups[:i]) for i in range(len(groups) + 1)]

    def issue(src_refs, land_refs, sem_refs):
        for i in range(len(groups)):
            lo, hi = bounds[i], bounds[i + 1]
            local, sends, _ = _gather_copies(src_refs[lo:hi], land_refs[lo:hi], sem_refs[3 * i:3 * i + 3], False)
            for cp in local + sends:
                cp.start()

    sem_counts = [n for g in groups for n in ((N_DEV - 1) * len(g), (N_DEV - 1) * len(g), len(g))]
    sems, srcs, lands, token = _split_start(name, issue, flat, [_sds((N_DEV,) + b.shape, b.dtype) for b in flat], sem_counts)
    return [(sems[3 * i:3 * i + 3], srcs[bounds[i]:bounds[i + 1]], lands[bounds[i]:bounds[i + 1]])
            for i in range(len(groups))], token


def gather_wait(group, after, name):
    sems, srcs, lands = group

    def finish(src_refs, land_refs, sem_refs):
        local, sends, recvs = _gather_copies(src_refs, land_refs, sem_refs, True)
        for cp in local:
            cp.wait()
        for cp in recvs:
            cp.wait_recv()
        for cp in sends:
            cp.wait_send()

    return _split_wait(name, finish, sems, srcs, lands, after)[1]


def _sibling_copies(src_refs, land_refs, sem_refs):
    send_sems, recv_sems = sem_refs
    x, y, c = _place()
    copies = []
    for t, (src, land) in enumerate(zip(src_refs, land_refs)):
        for k in range(4):
            cx, cy = _chip_of(x, y, k)
            copies.append(pltpu.make_async_remote_copy(
                src_ref=src.at[_dev_index(cx, cy, 1 - c)], dst_ref=land.at[k], send_sem=send_sems.at[4 * t + k],
                recv_sem=recv_sems.at[4 * t + k], device_id=(x, y, 1 - c), device_id_type=MESH_ID))
    return copies


def _chip_copies(src_refs, land_refs, sem_refs):
    send_sems, recv_sems = sem_refs
    x, y, c = _place()
    copies = []
    for t, (src, land) in enumerate(zip(src_refs, land_refs)):
        for k in range(1, 4):
            cx, cy = _chip_of(x, y, k)
            copies.append(pltpu.make_async_remote_copy(
                src_ref=src.at[k], dst_ref=land.at[k - 1], send_sem=send_sems.at[3 * t + k - 1],
                recv_sem=recv_sems.at[3 * t + k - 1], device_id=(cx, cy, c), device_id_type=MESH_ID))
    return copies


def _exchange_start(copies_of, n_land, per_array, arrays, name):
    def issue(src_refs, land_refs, sem_refs):
        for cp in copies_of(src_refs, land_refs, sem_refs):
            cp.start()

    n = per_array * len(arrays)
    lands = [_sds((n_land,) + a.shape[1:], a.dtype) for a in arrays]
    return _split_start(name, issue, arrays, lands, (n, n))


def _exchange_wait(copies_of, started, after, name):
    sems, srcs, lands, _ = started

    def finish(src_refs, land_refs, sem_refs):
        copies = copies_of(src_refs, land_refs, sem_refs)
        for cp in copies:
            cp.wait_recv()
        for cp in copies:
            cp.wait_send()

    return _split_wait(name, finish, sems, srcs, lands, after)


def sibling_start(grads, name):
    return _exchange_start(_sibling_copies, 4, 4, grads, name)


def sibling_wait(started, after, name):
    return _exchange_wait(_sibling_copies, started, after, name)


def chips_start(parts, name):
    return _exchange_start(_chip_copies, 3, 3, parts, name)


def chips_wait(started, after, name):
    return _exchange_wait(_chip_copies, started, after, name)


def _pack(arrays, rows):
    flat = jnp.concatenate([a.reshape(-1) for a in arrays])
    return jnp.pad(flat, (0, rows * LANES - flat.shape[0])).reshape(rows, LANES)


def _pack_stacked(arrays, rows):
    flat = jnp.concatenate([a.reshape(N_DEV, -1) for a in arrays], axis=1)
    return jnp.pad(flat, ((0, 0), (0, rows * LANES - flat.shape[1]))).reshape(N_DEV, rows, LANES)


def _unpack(buf, shapes, lead=()):
    flat = buf.reshape(lead + (-1,))
    out, off = [], 0
    for s in shapes:
        n = 1
        for d in s:
            n *= d
        out.append(flat[..., off:off + n].reshape(lead + tuple(s)))
        off += n
    return out


def _padded_rows(shapes, multiple):
    n = sum(functools.reduce(lambda a, b: a * b, s, 1) for s in shapes)
    rows = -(-n // LANES)
    return -(-rows // multiple) * multiple


def _to_full(stacked, axis):
    t = jnp.moveaxis(stacked, 0, axis)
    return t.reshape(t.shape[:axis] + (t.shape[axis] * t.shape[axis + 1],) + t.shape[axis + 2:])


def _to_stacked(full, axis):
    s = full.shape
    t = full.reshape(s[:axis] + (N_DEV, s[axis] // N_DEV) + s[axis + 1:])
    return jnp.moveaxis(t, axis, 0)


def _ffn_forward(x, norm_g, w_up, cw, cb, w_down, tag):
    h = rms_fwd(x, norm_g, f"ffn{tag}_norm")
    z = mm_in(h, w_up, f"ffn{tag}_up", stacked_out=True, out_dtype=BF16, rows=2048)
    nb, m, c = z.shape
    z4 = z.reshape(2, nb // 2, m, c)
    act = ffn_mid_fwd(z4, cw, cb, f"ffn{tag}_mid")
    out = mm_out(act, w_down, x, f"ffn{tag}_down")
    return out, (x, h, z4, act)


def _ffn_backward(dx, dxb, saved, norm_g, w_up, cw, cb, w_down, tag, after=()):
    x, h, z4, act = saved
    nj = act.shape[0]
    dact = mm_dx_out(dxb, w_down, f"ffn{tag}_down_dx", groups=nj, after=after)
    dw_down = mm_dw_out(act, dxb, f"ffn{tag}_down_dw")
    dz4, dcw, dcb = ffn_mid_bwd(z4, dact, cw, cb, f"ffn{tag}_mid_bwd")
    dz = dz4.reshape((2 * nj,) + dz4.shape[2:])
    dh = mm_dx_in(dz, w_up, f"ffn{tag}_up_dx")
    dw_up = mm_dw_in(h, dz, 2 * nj, f"ffn{tag}_up_dw", transposed=True)
    dx, dxb, dg = rms_bwd(x, norm_g, dh, dx, f"ffn{tag}_norm_bwd")
    return dx, dxb, dict(norm_g=dg, w_up=dw_up, conv_w=dcw, conv_b=dcb, w_down=dw_down)


def kernel(x, a_norm_g, a_w_in, a_b_in, a_v_norm_g, a_w_s, a_b_s, a_w_out, b_norm_g, b_w_in, b_w_grp, b_b_grp, b_scale, b_w_out, c_norm_g, c_w_in, c_b_in, c_conv_w, c_conv_b, c_w_a, c_b_a, c_w_i, c_b_i, c_lambda, c_w_out, d_norm_g, d_w_in, d_conv_w, d_w_out, ffn_norm_g, ffn_w_up, ffn_conv_w, ffn_conv_b, ffn_w_down, final_norm_g, loss_target, m_a_norm_g, m_a_w_in, m_a_b_in, m_a_v_norm_g, m_a_w_s, m_a_b_s, m_a_w_out, m_b_norm_g, m_b_w_in, m_b_w_grp, m_b_b_grp, m_b_scale, m_b_w_out, m_c_norm_g, m_c_w_in, m_c_b_in, m_c_conv_w, m_c_conv_b, m_c_w_a, m_c_b_a, m_c_w_i, m_c_b_i, m_c_lambda, m_c_w_out, m_d_norm_g, m_d_w_in, m_d_conv_w, m_d_w_out, m_ffn_norm_g, m_ffn_w_up, m_ffn_conv_w, m_ffn_conv_b, m_ffn_w_down, m_final_norm_g, v_a_norm_g, v_a_w_in, v_a_b_in, v_a_v_norm_g, v_a_w_s, v_a_b_s, v_a_w_out, v_b_norm_g, v_b_w_in, v_b_w_grp, v_b_b_grp, v_b_scale, v_b_w_out, v_c_norm_g, v_c_w_in, v_c_b_in, v_c_conv_w, v_c_conv_b, v_c_w_a, v_c_b_a, v_c_w_i, v_c_b_i, v_c_lambda, v_c_w_out, v_d_norm_g, v_d_w_in, v_d_conv_w, v_d_w_out, v_ffn_norm_g, v_ffn_w_up, v_ffn_conv_w, v_ffn_conv_b, v_ffn_w_down, v_final_norm_g):
    args = locals()
    w_loc = {n: args[n] for n in WEIGHTS}
    m_loc = {n: args["m_" + n] for n in WEIGHTS}
    v_loc = {n: args["v_" + n] for n in WEIGHTS}
    depth = ffn_w_up.shape[0]
    xs = x[0]
    target = loss_target[0]

    small_names = list(SMALL_SHARDED)
    small_shapes = [w_loc[n].shape for n in small_names]
    small_rows = _padded_rows(small_shapes, 8)
    small_packed = _pack([w_loc[n] for n in small_names], small_rows)
    mixers = [(a_w_in, a_w_out), (b_w_in, b_w_out), (c_w_in, c_w_out), (d_w_in, d_w_out)]
    groups = []
    for l in range(depth):
        groups.append([mixers[l][0][0].astype(BF16), mixers[l][1][0].astype(BF16)] + ([small_packed] if l == 0 else []))
        groups.append([ffn_w_up[l].astype(BF16), ffn_w_down[l].astype(BF16)])
    in_flight, _ = gather_start(groups, "gather_start")

    def rows_full(st):
        return st.reshape((st.shape[0] * st.shape[1],) + st.shape[2:])

    nb = N_DEV
    ffn_cb = [ffn_conv_b[l].reshape(2, nb // 2, 1, -1) for l in range(depth)]
    ffn_g = [ffn_norm_g[l:l + 1] for l in range(depth)]
    a_bst = a_b_s[0].T
    w_up, w_down, saved = [None] * depth, [None] * depth, {}

    def ffn_forward(xl, l):
        up, down = gather_wait(in_flight[2 * l + 1], (xl,), f"gather_wait_ffn{l}")
        w_up[l], w_down[l] = up, rows_full(down)
        return _ffn_forward(xl, ffn_g[l], w_up[l], ffn_cw[l], ffn_cb[l], w_down[l], l)

    wa_in, wa_out, small_st = gather_wait(in_flight[0], (xs,), "gather_wait_a")
    wa_out = rows_full(wa_out)
    sm = {n: _to_full(s, SMALL_SHARDED[n]) for n, s in zip(small_names, _unpack(small_st, small_shapes, (N_DEV,)))}
    ffn_cw = [sm['ffn_conv_w'][l].reshape(ffn_conv_w.shape[1], 2, nb // 2, -1).transpose(1, 2, 0, 3) for l in range(depth)]
    b_wgrp, c_wa, c_wi = (sm[n][0].astype(BF16) for n in ('b_w_grp', 'c_w_a', 'c_w_i'))
    b_bgrp, c_ba, c_bi = (sm[n][0].reshape(1, -1) for n in ('b_b_grp', 'c_b_a', 'c_b_i'))
    h = rms_fwd(xs, a_norm_g, "a_norm")
    z = mm_in(h, wa_in, "a_in", bias=a_b_in)
    y = a_mid_fwd(z, a_v_norm_g, a_w_s[0], a_bst, "a_mid")
    x1 = mm_out(y, wa_out, xs, "a_out")
    saved['a'] = (xs, h, z, y)
    x1, saved['f0'] = ffn_forward(x1, 0)

    wb_in, wb_out = gather_wait(in_flight[2], (x1,), "gather_wait_b")
    wb_in, wb_out = rows_full(wb_in)[None], rows_full(wb_out)
    h = rms_fwd(x1, sm['b_norm_g'], "b_norm")
    z = mm_in(h, wb_in, "b_in")
    y = b_mid_fwd(z, b_wgrp, b_bgrp, sm['b_scale'], "b_mid")
    x2 = mm_out(y, wb_out, x1, "b_out")
    saved['b'] = (x1, h, z, y)
    x2, saved['f1'] = ffn_forward(x2, 1)

    wc_in, wc_out = gather_wait(in_flight[4], (x2,), "gather_wait_c")
    wc_out = rows_full(wc_out)
    h = rms_fwd(x2, sm['c_norm_g'], "c_norm")
    z = mm_in(h, wc_in, "c_in", bias=sm['c_b_in'])
    c_cw = sm['c_conv_w'][0]
    a_seq, b_seq, xr = c_mid_fwd(z, c_cw, sm['c_conv_b'], c_wa, c_ba, c_wi, c_bi, sm['c_lambda'], "c_mid")
    hs, y = c_scan_fwd(a_seq, b_seq, z, "c_scan")
    x3 = mm_out(y, wc_out, x2, "c_out")
    saved['c'] = (x2, h, z, y, a_seq, xr, hs)
    x3, saved['f2'] = ffn_forward(x3, 2)

    wd_in, wd_out = gather_wait(in_flight[6], (x3,), "gather_wait_d")
    wd_out = rows_full(wd_out)
    h = rms_fwd(x3, sm['d_norm_g'], "d_norm")
    z = mm_in(h, wd_in, "d_in")
    d_cw = sm['d_conv_w'][0]
    y = d_mid_fwd(z, d_cw, "d_mid")
    x4 = mm_out(y, wd_out, x3, "d_out")
    saved['d'] = (x3, h, z, y)
    x4, saved['f3'] = ffn_forward(x4, 3)

    loss_part, dx, dxb, d_final_g = final_loss(x4, final_norm_g.reshape(1, -1), target, "final_loss")
    loss = lax.psum(loss_part[0, 0], ("x", "y", "c"))

    def rows_stacked(full):
        return full.reshape((N_DEV, full.shape[0] // N_DEV) + full.shape[1:])

    mx, my, mc = _place()
    own = jnp.stack([_dev_index(*_chip_of(mx, my, k), mc) for k in range(4)]).astype(jnp.int32)
    repl_shapes = [w_loc[n].shape for n in REPLICATED]
    repl_rows = _padded_rows(repl_shapes, 8 * N_DEV)
    shard_of = {n: (w_loc[n][0], m_loc[n][0], v_loc[n][0])
                for n in ('a_w_in', 'a_w_out', 'b_w_in', 'b_w_out', 'c_w_in', 'c_w_out', 'd_w_in', 'd_w_out')}
    shard_of['small'] = (small_packed, _pack([m_loc[n] for n in small_names], small_rows),
                         _pack([v_loc[n] for n in small_names], small_rows))
    updated = {}

    def finish(n, part, others):
        if n == 'repl':
            chunk = sum_parts(part, others, "rs_sum_repl")
            repl_g = all_gather([chunk], "gather_repl")[0].reshape(repl_rows, LANES)
            updated[n] = adamw(*(_pack([src[k] for k in REPLICATED], repl_rows) for src in (w_loc, m_loc, v_loc)),
                               "adamw_repl", g=repl_g)
        elif n.startswith('ffn_w'):
            base, l = n[:-1], int(n[-1])
            turn = (lambda t: jnp.swapaxes(t, 1, 2)) if base == 'ffn_w_up' else (lambda t: t)
            updated[base] = adamw_layer(turn(w_loc[base]), turn(m_loc[base]), turn(v_loc[base]), l, updated.get(base),
                                        f"adamw_{n}", part, others)
            return updated[base][1]
        else:
            updated[n] = adamw(*shard_of[n], f"adamw_{n}", p=part, l2=others)
        return updated[n][1]

    stages = [None, None]

    def advance(tag, new, after):
        behind = []
        first = None
        if new:
            first = ([n for n, _ in new], sibling_start([g for _, g in new], f"rs_sibling_start{tag}"))
            behind.append(first[1][3])
        second = None
        if stages[0] is not None:
            names, started = stages[0]
            grads, got = sibling_wait(started, after, f"rs_sibling_wait{tag}")
            parts = [add_pairs(g, l, own, f"rs_add_{n}") for n, g, l in zip(names, grads, got)]
            second = (names, chips_start(parts, f"rs_chips_start{tag}"))
            behind.append(second[1][3])
        if stages[1] is not None:
            names, started = stages[1]
            parts, others = chips_wait(started, after, f"rs_chips_wait{tag}")
            behind += [finish(n, p, o) for n, p, o in zip(names, parts, others)]
        stages[:] = [first, second]
        return tuple(behind)

    gf = [None] * depth
    dx, dxb, gf[3] = _ffn_backward(dx, dxb, saved['f3'], ffn_g[3], w_up[3], ffn_cw[3], ffn_cb[3], w_down[3], 3)
    xin, h, z, y = saved['d']
    dy = mm_dx_out(dxb, wd_out, "d_out_dx")
    g_d_w_out = mm_dw_out(y, dxb, "d_out_dw")
    dz, g_d_conv_w = d_mid_bwd(z, dy, d_cw, "d_mid_bwd")
    dh = mm_dx_in(dz, wd_in, "d_in_dx")
    g_d_w_in = mm_dw_in(h, dz, nb, "d_in_dw")
    dx, dxb, g_d_norm_g = rms_bwd(xin, sm['d_norm_g'], dh, dx, "d_norm_bwd")
    behind = advance(0, [('ffn_w_up3', gf[3]['w_up']), ('ffn_w_down3', rows_stacked(gf[3]['w_down'])), ('d_w_in', g_d_w_in),
                         ('d_w_out', rows_stacked(g_d_w_out))], (dx,))

    dx, dxb, gf[2] = _ffn_backward(dx, dxb, saved['f2'], ffn_g[2], w_up[2], ffn_cw[2], ffn_cb[2], w_down[2], 2, behind)
    xin, h, z, y, a_seq, xr, hs = saved['c']
    dy = mm_dx_out(dxb, wc_out, "c_out_dx")
    g_c_w_out = mm_dw_out(y, dxb, "c_out_dw")
    lam_seq, da_seq, dgate, dgate_sum = c_scan_bwd(dy, z, hs, a_seq, "c_scan_bwd")
    dxr, g_c_w_a, g_c_w_i, g_c_b_a, g_c_b_i, g_c_lambda = c_mid_bwd(
        lam_seq, da_seq, xr, c_wa, c_ba, c_wi, c_bi, sm['c_lambda'], "c_mid_bwd")
    dxr_pre, g_c_conv_w, g_c_conv_b, dxr_pre_sum = conv_bwd(dxr, z, 1, c_cw, "c_conv_bwd")
    dz = jnp.concatenate([dgate, dxr_pre], axis=1)
    g_c_b_in = jnp.concatenate([dgate_sum, dxr_pre_sum], axis=1)
    dh = mm_dx_in(dz, wc_in, "c_in_dx")
    g_c_w_in = mm_dw_in(h, dz, nb, "c_in_dw")
    dx, dxb, g_c_norm_g = rms_bwd(xin, sm['c_norm_g'], dh, dx, "c_norm_bwd")
    behind = advance(1, [('ffn_w_up2', gf[2]['w_up']), ('ffn_w_down2', rows_stacked(gf[2]['w_down'])), ('c_w_in', g_c_w_in),
                         ('c_w_out', rows_stacked(g_c_w_out))], (dx,))

    dx, dxb, gf[1] = _ffn_backward(dx, dxb, saved['f1'], ffn_g[1], w_up[1], ffn_cw[1], ffn_cb[1], w_down[1], 1, behind)
    xin, h, z, y = saved['b']
    dy = mm_dx_out(dxb, wb_out, "b_out_dx")
    g_b_w_out = mm_dw_out(y, dxb, "b_out_dw")
    dp, g_b_w_grp, g_b_b_grp, g_b_scale = b_mid_bwd(z, dy, b_wgrp, b_bgrp, sm['b_scale'], "b_mid_bwd")
    dz = b_pool_bwd(dp, "b_pool_bwd")
    dh = mm_dx_in(dz, wb_in, "b_in_dx")
    g_b_w_in = mm_dw_in(h, dz, 1, "b_in_dw")
    dx, dxb, g_b_norm_g = rms_bwd(xin, sm['b_norm_g'], dh, dx, "b_norm_bwd")
    behind = advance(2, [('ffn_w_up1', gf[1]['w_up']), ('ffn_w_down1', rows_stacked(gf[1]['w_down'])),
                         ('b_w_in', rows_stacked(g_b_w_in[0])), ('b_w_out', rows_stacked(g_b_w_out))], (dx,))

    dx, dxb, gf[0] = _ffn_backward(dx, dxb, saved['f0'], ffn_g[0], w_up[0], ffn_cw[0], ffn_cb[0], w_down[0], 0, behind)
    full_small = {
        'b_norm_g': g_b_norm_g, 'b_w_grp': g_b_w_grp[None], 'b_b_grp': g_b_b_grp.reshape(b_b_grp.shape[:2] + (-1,)),
        'b_scale': g_b_scale, 'c_norm_g': g_c_norm_g, 'c_b_in': g_c_b_in, 'c_conv_w': g_c_conv_w[None],
        'c_conv_b': g_c_conv_b, 'c_w_a': g_c_w_a[None], 'c_b_a': g_c_b_a.reshape(c_b_a.shape[:2] + (-1,)),
        'c_w_i': g_c_w_i[None], 'c_b_i': g_c_b_i.reshape(c_b_i.shape[:2] + (-1,)), 'c_lambda': g_c_lambda,
        'd_norm_g': g_d_norm_g, 'd_conv_w': g_d_conv_w[None],
        'ffn_conv_w': jnp.stack([gf[l]['conv_w'].transpose(2, 0, 1, 3).reshape(ffn_conv_w.shape[1], -1) for l in range(depth)])}
    small_grads = _pack_stacked([_to_stacked(full_small[n], SMALL_SHARDED[n]) for n in small_names], small_rows)
    behind = advance(3, [('ffn_w_up0', gf[0]['w_up']), ('ffn_w_down0', rows_stacked(gf[0]['w_down'])), ('small', small_grads)],
                     (dx,))

    xin, h, z, y = saved['a']
    dy = mm_dx_out(dxb, wa_out, "a_out_dx", after=behind)
    g_a_w_out = mm_dw_out(y, dxb, "a_out_dw")
    behind = advance(4, [('a_w_out', rows_stacked(g_a_w_out))], (dy,))
    dz, g_a_b_in, g_a_v_norm_g, g_a_w_s, g_a_b_s = a_mid_bwd(z, dy, a_v_norm_g, a_w_s[0], a_bst, "a_mid_bwd", behind)
    g_a_w_in = mm_dw_in(h, dz, nb, "a_in_dw")
    behind = advance(5, [('a_w_in', g_a_w_in)], (dz,))
    dh = mm_dx_in(dz, wa_in, "a_in_dx", behind)
    dx, _, g_a_norm_g = rms_bwd(xin, a_norm_g, dh, dx, "a_norm_bwd")
    grad_x = dx[None]

    tril = jnp.tril(jnp.ones((A_CHUNK, A_CHUNK), bool))
    repl_full = {
        'a_norm_g': g_a_norm_g, 'a_b_in': g_a_b_in, 'a_v_norm_g': g_a_v_norm_g,
        'a_w_s': jnp.where(tril, g_a_w_s, 0.0)[None], 'a_b_s': g_a_b_s[:, ::LANES].T[None],
        'ffn_norm_g': jnp.concatenate([gf[l]['norm_g'] for l in range(depth)], axis=0),
        'ffn_conv_b': jnp.stack([gf[l]['conv_b'].reshape(-1) for l in range(depth)]), 'final_norm_g': d_final_g.reshape(-1)}
    repl_grads = _pack([repl_full[n] for n in REPLICATED], repl_rows).reshape(N_DEV, repl_rows // N_DEV, LANES)
    behind = advance(6, [('repl', repl_grads)], (dx,))
    behind = advance(7, [], (dx, *behind))
    advance(8, [], (dx, *behind))

    outs = [{}, {}, {}, {}]
    for i, dst in enumerate(outs):
        for n in ('a_w_in', 'a_w_out', 'b_w_in', 'b_w_out', 'c_w_in', 'c_w_out', 'd_w_in', 'd_w_out'):
            dst[n] = updated[n][i][None]
        dst['ffn_w_up'] = jnp.swapaxes(updated['ffn_w_up'][i], 1, 2)
        dst['ffn_w_down'] = updated['ffn_w_down'][i]
        dst.update(zip(small_names, _unpack(updated['small'][i], small_shapes)))
        dst.update(zip(REPLICATED, _unpack(updated['repl'][i], repl_shapes)))
    out_g, out_d, out_m, out_v = outs

    return (loss, grad_x, *[out_g[n] for n in WEIGHTS], *[out_d[n] for n in WEIGHTS], *[out_m[n] for n in WEIGHTS],
            *[out_v[n] for n in WEIGHTS])
```

```python
import functools

import jax
import jax.numpy as jnp
from jax import lax
from jax.experimental import pallas as pl
from jax.experimental.pallas import tpu as pltpu

F32, BF16 = jnp.float32, jnp.bfloat16
MESH_ID = pl.DeviceIdType.MESH
N_DEV = 8
V7X_VMEM_LIMIT_BYTES = 56 << 20
LANES = 128
HALO = 8
POOL_HALO = 16

EPS = 1e-6
A_CHUNK, A_GROUPS = 128, 4
B_WINDOWS = (2, 4, 8, 16)
C_GATE_C = 8.0
ADAM_LR, ADAM_B1, ADAM_B2, ADAM_EPS, ADAM_WD, ADAM_STEP = 0.001, 0.9, 0.999, 1e-08, 0.01, 10

WEIGHTS = ['a_norm_g', 'a_w_in', 'a_b_in', 'a_v_norm_g', 'a_w_s', 'a_b_s', 'a_w_out', 'b_norm_g', 'b_w_in', 'b_w_grp',
           'b_b_grp', 'b_scale', 'b_w_out', 'c_norm_g', 'c_w_in', 'c_b_in', 'c_conv_w', 'c_conv_b', 'c_w_a', 'c_b_a',
           'c_w_i', 'c_b_i', 'c_lambda', 'c_w_out', 'd_norm_g', 'd_w_in', 'd_conv_w', 'd_w_out', 'ffn_norm_g',
           'ffn_w_up', 'ffn_conv_w', 'ffn_conv_b', 'ffn_w_down', 'final_norm_g']
SMALL_SHARDED = {'b_norm_g': 1, 'b_w_grp': 2, 'b_b_grp': 2, 'b_scale': 1, 'c_norm_g': 1, 'c_b_in': 1, 'c_conv_w': 2,
                 'c_conv_b': 1, 'c_w_a': 2, 'c_b_a': 2, 'c_w_i': 2, 'c_b_i': 2, 'c_lambda': 1, 'd_norm_g': 1,
                 'd_conv_w': 2, 'ffn_conv_w': 2}
REPLICATED = ['a_norm_g', 'a_b_in', 'a_v_norm_g', 'a_w_s', 'a_b_s', 'ffn_norm_g', 'ffn_conv_b', 'final_norm_g']


_GELU_C0, _GELU_C1 = 0.7978845608028654, 0.044715


def _gelu(x):
    return 0.5 * x * (1.0 + jnp.tanh(_GELU_C0 * (x + _GELU_C1 * (x * x * x))))


def _gelu_grad(x):
    t = jnp.tanh(_GELU_C0 * (x + _GELU_C1 * (x * x * x)))
    return 0.5 * (1.0 + t) + 0.5 * x * (1.0 - t * t) * (_GELU_C0 * (1.0 + 3.0 * _GELU_C1 * (x * x)))


def _sigmoid(x):
    return jax.nn.sigmoid(x)


def _log1p(x):
    u = 1.0 + x
    return jnp.where(u == 1.0, x, jnp.log(u) * (x / (u - 1.0)))


def _softplus(x):
    return jnp.maximum(x, 0.0) + _log1p(jnp.exp(-jnp.abs(x)))


def _expm1(x):
    poly = x * (1.0 + x * (1 / 2) * (1.0 + x * (1 / 3) * (1.0 + x * (1 / 4) * (1.0 + x * (1 / 5) * (
        1.0 + x * (1 / 6) * (1.0 + x * (1 / 7) * (1.0 + x * (1 / 8))))))))
    return jnp.where(jnp.abs(x) < 0.35, poly, jnp.exp(x) - 1.0)


def _down(xe, s):
    return xe if s == 0 else pltpu.roll(xe, s, 0)


def _up(xe, s):
    return xe if s == 0 else pltpu.roll(xe, xe.shape[0] - s, 0)


def _conv_ext(xe, w, taps):
    y = xe * w[taps - 1:taps]
    for s in range(1, taps):
        y = y + _down(xe, s) * w[taps - 1 - s:taps - s]
    return y


def _acc(ref, val, first):
    @pl.when(first)
    def _():
        ref[...] = val

    @pl.when(jnp.logical_not(first))
    def _():
        ref[...] += val


def _colsum(v):
    return jnp.sum(v, axis=0, keepdims=True)


def _dot(a, b, dims=((1,), (0,))):
    return lax.dot_general(a.astype(BF16), b.astype(BF16), (dims, ((), ())), preferred_element_type=F32)


_NN, _NT, _TN = ((1,), (0,)), ((1,), (1,)), ((0,), (0,))


def _call(body, name, grid, in_specs, out_specs, out_shape, scratch=(), after=()):
    n_in, n_after = len(in_specs), len(after)

    def ordered_body(*refs):
        return body(*refs[:n_in], *refs[n_in + n_after:])

    call = pl.pallas_call(
        ordered_body if n_after else body, name=name, grid=grid,
        in_specs=list(in_specs) + [pl.BlockSpec(memory_space=pl.ANY)] * n_after, out_specs=out_specs,
        out_shape=out_shape, scratch_shapes=list(scratch),
        compiler_params=pltpu.CompilerParams(dimension_semantics=("arbitrary",) * len(grid),
                                             vmem_limit_bytes=V7X_VMEM_LIMIT_BYTES))
    return lambda *args: call(*args, *after)


def _rows(m, t):
    t = min(m, t)
    assert m % t == 0, (m, t)
    return t


def _sds(shape, dtype=F32):
    return jax.ShapeDtypeStruct(tuple(shape), dtype)


def _prev_halo(tm, halo=HALO):
    return lambda i: jnp.maximum(i * (tm // halo) - 1, 0)


def _next_halo(tm, m, halo=HALO):
    return lambda i: jnp.minimum((i + 1) * (tm // halo), m // halo - 1)


def _matmul(name, ins, in_specs, out_shape, o_spec, grid, compute, after=()):
    def body(*refs):
        refs[-1][...] = compute(*refs[:-1]).astype(refs[-1].dtype)

    return _call(body, name, grid, in_specs, o_spec, out_shape, (), after)(*ins)


def mm_in(h, w_st, name, bias=None, stacked_out=False, out_dtype=F32, rows=1024):
    m, k = h.shape
    nb, _, n = w_st.shape
    tm = _rows(m, rows)
    in_specs = [pl.BlockSpec((tm, k), lambda i, j: (i, 0)), pl.BlockSpec((None, k, n), lambda i, j: (j, 0, 0))]
    if stacked_out:
        out, o_spec = _sds((nb, m, n), out_dtype), pl.BlockSpec((None, tm, n), lambda i, j: (j, i, 0))
    else:
        out, o_spec = _sds((m, nb * n), out_dtype), pl.BlockSpec((tm, n), lambda i, j: (i, j))
    if bias is None:
        return _matmul(name, (h, w_st), in_specs, out, o_spec, (m // tm, nb), lambda a, b: _dot(a[...], b[...]))
    in_specs.append(pl.BlockSpec((1, n), lambda i, j: (0, j)))
    return _matmul(name, (h, w_st, bias), in_specs, out, o_spec, (m // tm, nb),
                   lambda a, b, c: _dot(a[...], b[...]) + c[...])


def _split_rows(kf):
    g = max(1, kf // 1024)
    return g, kf // g


def _resident(shape):
    return pl.BlockSpec(shape, lambda *_: (0,) * len(shape), pipeline_mode=pl.Buffered(1))


def _norm_rows(xv, g):
    return (xv * lax.rsqrt(jnp.mean(xv * xv, axis=-1, keepdims=True) + EPS) * g).astype(BF16)


def mm_out(y, w, res, next_g, name):
    kf, n = w.shape
    m = y.shape[0]
    tm = _rows(m, 512)
    row = pl.BlockSpec((tm, n), lambda i: (i, 0))

    def body(y_ref, w_ref, res_ref, g_ref, x_ref, h_ref):
        xv = res_ref[...] + _dot(y_ref[...], w_ref[...])
        x_ref[...] = xv
        h_ref[...] = _norm_rows(xv, g_ref[...])

    in_specs = [pl.BlockSpec((tm, kf), lambda i: (i, 0)), _resident((kf, n)), row, pl.BlockSpec((1, n), lambda i: (0, 0))]
    return _call(body, name, (m // tm,), in_specs, [row, row], [_sds((m, n)), _sds((m, n), BF16)])(y, w, res, next_g)


def mm_dx_in(dz, w_st, name, after=(), norm=None):
    nb, k, n = w_st.shape
    m = dz.shape[-2]
    tm = _rows(m, 512)
    w_spec = _resident((nb, k, n))
    if dz.ndim == 3:
        in_specs = [pl.BlockSpec((None, tm, n), lambda i, r=r: (r, i, 0)) for r in range(nb)] + [w_spec]

        def compute(*refs):
            acc = _dot(refs[0][...], refs[nb][0], _NT)
            for r in range(1, nb):
                acc = acc + _dot(refs[r][...], refs[nb][r], _NT)
            return acc

        ins = (*[dz] * nb, w_st)
    else:
        in_specs = [pl.BlockSpec((tm, nb * n), lambda i: (i, 0)), w_spec]

        def compute(dz_ref, w_ref):
            acc = _dot(dz_ref[:, :n], w_ref[0], _NT)
            for r in range(1, nb):
                acc = acc + _dot(dz_ref[:, r * n:(r + 1) * n], w_ref[r], _NT)
            return acc

        ins = (dz, w_st)
    row = pl.BlockSpec((tm, k), lambda i: (i, 0))
    if norm is None:
        return _matmul(name, ins, in_specs, _sds((m, k)), row, (m // tm,), compute, after)
    n_in = len(in_specs)
    vec = pl.BlockSpec((1, k), lambda i: (0, 0))

    def body(*refs):
        x_ref, g_ref, dr_ref, dx_ref, dxb_ref, dg_ref = refs[n_in:]
        dx, dg = _rms_bwd_math(x_ref[...], g_ref[...], compute(*refs[:n_in]))
        dx = dr_ref[...] + dx
        dx_ref[...] = dx
        dxb_ref[...] = dx.astype(BF16)
        _acc(dg_ref, dg, pl.program_id(0) == 0)

    return _call(body, name, (m // tm,), in_specs + [row, vec, row], [row, row, vec],
                 [_sds((m, k)), _sds((m, k), BF16), _sds((1, k))], (), after)(*ins, *norm)


def mm_dx_out(dout, w, name, groups=None, after=()):
    kf, n = w.shape
    m = dout.shape[0]
    tm = _rows(m, 1024)
    g, k = (groups, kf // groups) if groups else _split_rows(kf)
    in_specs = [pl.BlockSpec((tm, n), lambda i, j: (i, 0)), pl.BlockSpec((None, k, n), lambda i, j: (j, 0, 0))]
    if groups:
        out, o_spec = _sds((g, m, k)), pl.BlockSpec((None, tm, k), lambda i, j: (j, i, 0))
    else:
        out, o_spec = _sds((m, kf)), pl.BlockSpec((tm, k), lambda i, j: (i, j))
    return _matmul(name, (dout, w.reshape(g, k, n)), in_specs, out, o_spec, (m // tm, g),
                   lambda a, b: _dot(a[...], b[...], _NT), after)


def mm_dw_in(h, dz, nb, name, transposed=False):
    m, k = h.shape
    if dz.ndim == 3:
        n = dz.shape[2]
        dz_spec = pl.BlockSpec((None, m, n), lambda j: (j, 0, 0))
    else:
        n = dz.shape[1] // nb
        dz_spec = pl.BlockSpec((m, n), lambda j: (0, j))
    in_specs = [_resident((m, k)), dz_spec]
    if transposed:
        return _matmul(name, (h, dz), in_specs, _sds((nb, n, k), BF16), pl.BlockSpec((None, n, k), lambda j: (j, 0, 0)), (nb,),
                       lambda a, b: _dot(b[...], a[...], _TN))
    return _matmul(name, (h, dz), in_specs, _sds((nb, k, n), BF16), pl.BlockSpec((None, k, n), lambda j: (j, 0, 0)), (nb,),
                   lambda a, b: _dot(a[...], b[...], _TN))


def mm_dw_out(y, dout, name):
    m, n = dout.shape
    if y.ndim == 3:
        g, _, k = y.shape
        y_spec = pl.BlockSpec((None, m, k), lambda j: (j, 0, 0))
    else:
        g, k = _split_rows(y.shape[1])
        y_spec = pl.BlockSpec((m, k), lambda j: (0, j))
    in_specs = [y_spec, _resident((m, n))]
    out = _matmul(name, (y, dout), in_specs, _sds((g, k, n), BF16), pl.BlockSpec((None, k, n), lambda j: (j, 0, 0)), (g,),
                  lambda a, b: _dot(a[...], b[...], _TN))
    return out.reshape(g * k, n)


def rms_fwd(x, g, name):
    m, d = x.shape
    tm = _rows(m, 512)

    def body(x_ref, g_ref, o_ref):
        xv = x_ref[...]
        rstd = lax.rsqrt(jnp.mean(xv * xv, axis=-1, keepdims=True) + EPS)
        o_ref[...] = (xv * rstd * g_ref[...]).astype(BF16)

    row = pl.BlockSpec((tm, d), lambda i: (i, 0))
    vec = pl.BlockSpec((1, d), lambda i: (0, 0))
    return _call(body, name, (m // tm,), [row, vec], row, _sds((m, d), BF16))(x, g)


def _rms_bwd_math(xv, g, dh):
    rstd = lax.rsqrt(jnp.mean(xv * xv, axis=-1, keepdims=True) + EPS)
    xhat = xv * rstd
    dxhat = dh * g
    dx = rstd * (dxhat - xhat * jnp.mean(dxhat * xhat, axis=-1, keepdims=True))
    return dx, _colsum(dh * xhat)


def final_loss(x, g, target, name):
    m, d = x.shape
    tm = _rows(m, 512)

    def body(x_ref, g_ref, t_ref, l_ref, dx_ref, dxb_ref, dg_ref):
        xv, gv = x_ref[...], g_ref[...]
        rstd = lax.rsqrt(jnp.mean(xv * xv, axis=-1, keepdims=True) + EPS)
        err = xv * rstd * gv - t_ref[...]
        part = 0.5 * jnp.sum(jnp.mean(err * err, axis=-1, keepdims=True), axis=0, keepdims=True)
        dx, dg = _rms_bwd_math(xv, gv, err * (1.0 / d))
        dx_ref[...] = dx
        dxb_ref[...] = dx.astype(BF16)
        first = pl.program_id(0) == 0
        _acc(l_ref, jnp.broadcast_to(part, l_ref.shape), first)
        _acc(dg_ref, dg, first)

    row = pl.BlockSpec((tm, d), lambda i: (i, 0))
    vec = pl.BlockSpec((1, d), lambda i: (0, 0))
    lsp = pl.BlockSpec((1, LANES), lambda i: (0, 0))
    return _call(body, name, (m // tm,), [row, vec, row], [lsp, row, row, vec],
                 [_sds((1, LANES)), _sds((m, d)), _sds((m, d), BF16), _sds((1, d))])(x, g, target)


def _a_common(z_ref, vg_ref, ws_ref, bst_ref, tm, width):
    gw = width // A_GROUPS
    zp = z_ref[...]
    z = _gelu(zp)
    u, v = z[:, :width], z[:, width:]
    rstd = lax.rsqrt(jnp.mean(v * v, axis=-1, keepdims=True) + EPS)
    vhat = v * rstd
    vn = vhat * vg_ref[...]
    t_i = lax.broadcasted_iota(jnp.int32, (A_CHUNK, A_CHUNK), 0)
    s_i = lax.broadcasted_iota(jnp.int32, (A_CHUNK, A_CHUNK), 1)
    wsm = [jnp.where(s_i <= t_i, ws_ref[g], 0.0).astype(BF16) for g in range(A_GROUPS)]
    bst = bst_ref[...]
    return zp, u, rstd, vhat, vn.astype(BF16), wsm, bst, gw


def a_mid_fwd(z, vg, ws, bst, name):
    m, w2 = z.shape
    width = w2 // 2
    tm = _rows(m, 256)

    def body(z_ref, vg_ref, ws_ref, bst_ref, y_ref):
        _, u, _, _, vnb, wsm, bst, gw = _a_common(z_ref, vg_ref, ws_ref, bst_ref, tm, width)
        for c in range(tm // A_CHUNK):
            r0 = c * A_CHUNK
            for g in range(A_GROUPS):
                c0 = g * gw
                vs = _dot(wsm[g], vnb[r0:r0 + A_CHUNK, c0:c0 + gw]) + bst[:, g:g + 1]
                y_ref[r0:r0 + A_CHUNK, c0:c0 + gw] = (u[r0:r0 + A_CHUNK, c0:c0 + gw] * vs).astype(BF16)

    in_specs = [pl.BlockSpec((tm, w2), lambda i: (i, 0)), pl.BlockSpec((1, width), lambda i: (0, 0)),
                pl.BlockSpec((A_GROUPS, A_CHUNK, A_CHUNK), lambda i: (0, 0, 0)),
                pl.BlockSpec((A_CHUNK, A_GROUPS), lambda i: (0, 0))]
    return _call(body, name, (m // tm,), in_specs, pl.BlockSpec((tm, width), lambda i: (i, 0)),
                 _sds((m, width), BF16))(z, vg, ws, bst)


def a_mid_bwd(z, dy, vg, ws, bst, name, after=()):
    m, w2 = z.shape
    width = w2 // 2
    tm = _rows(m, 256)

    def body(z_ref, dy_ref, vg_ref, ws_ref, bst_ref, dz_ref, dbin_ref, dvg_ref, dws_ref, dbs_ref, dvn_scr, du_scr):
        first = pl.program_id(0) == 0
        zp, u, rstd, vhat, vnb, wsm, bst, gw = _a_common(z_ref, vg_ref, ws_ref, bst_ref, tm, width)
        dy = dy_ref[...]
        dws = [jnp.zeros((A_CHUNK, A_CHUNK), F32) for _ in range(A_GROUPS)]
        dbs = [jnp.zeros((A_CHUNK, 1), F32) for _ in range(A_GROUPS)]
        for c in range(tm // A_CHUNK):
            r0 = c * A_CHUNK
            for g in range(A_GROUPS):
                c0 = g * gw
                vn_cg = vnb[r0:r0 + A_CHUNK, c0:c0 + gw]
                vs = _dot(wsm[g], vn_cg) + bst[:, g:g + 1]
                dy_cg = dy[r0:r0 + A_CHUNK, c0:c0 + gw]
                dvs = dy_cg * u[r0:r0 + A_CHUNK, c0:c0 + gw]
                du_scr[r0:r0 + A_CHUNK, c0:c0 + gw] = dy_cg * vs
                dws[g] = dws[g] + _dot(dvs, vn_cg, _NT)
                dbs[g] = dbs[g] + jnp.sum(dvs, axis=1, keepdims=True)
                dvn_scr[r0:r0 + A_CHUNK, c0:c0 + gw] = _dot(wsm[g], dvs, _TN)
        for g in range(A_GROUPS):
            _acc(dws_ref.at[g], dws[g], first)
            _acc(dbs_ref.at[:, g * LANES:(g + 1) * LANES], jnp.broadcast_to(dbs[g], (A_CHUNK, LANES)), first)
        dvn = dvn_scr[...]
        _acc(dvg_ref, _colsum(dvn * vhat), first)
        dvhat = dvn * vg_ref[...]
        dv = rstd * (dvhat - vhat * jnp.mean(dvhat * vhat, axis=-1, keepdims=True))
        gg = _gelu_grad(zp)
        dzu = du_scr[...] * gg[:, :width]
        dzv = dv * gg[:, width:]
        dz_ref[:, :width] = dzu.astype(BF16)
        dz_ref[:, width:] = dzv.astype(BF16)
        _acc(dbin_ref.at[:, :width], _colsum(dzu), first)
        _acc(dbin_ref.at[:, width:], _colsum(dzv), first)

    const2 = lambda i: (0, 0)
    in_specs = [pl.BlockSpec((tm, w2), lambda i: (i, 0)), pl.BlockSpec((tm, width), lambda i: (i, 0)),
                pl.BlockSpec((1, width), const2), pl.BlockSpec((A_GROUPS, A_CHUNK, A_CHUNK), lambda i: (0, 0, 0)),
                pl.BlockSpec((A_CHUNK, A_GROUPS), const2)]
    out_specs = [pl.BlockSpec((tm, w2), lambda i: (i, 0)), pl.BlockSpec((1, w2), const2), pl.BlockSpec((1, width), const2),
                 pl.BlockSpec((A_GROUPS, A_CHUNK, A_CHUNK), lambda i: (0, 0, 0)),
                 pl.BlockSpec((A_CHUNK, A_GROUPS * LANES), const2)]
    out_shape = [_sds((m, w2), BF16), _sds((1, w2)), _sds((1, width)), _sds((A_GROUPS, A_CHUNK, A_CHUNK)),
                 _sds((A_CHUNK, A_GROUPS * LANES))]
    scratch = [pltpu.VMEM((tm, width), F32), pltpu.VMEM((tm, width), F32)]
    return _call(body, name, (m // tm,), in_specs, out_specs, out_shape, scratch, after)(z, dy, vg, ws, bst)


def _pool_minus_id(ze, i, tm, gw):
    pos = i * tm + lax.broadcasted_iota(jnp.int32, (tm, 1), 0)
    out = []
    for gi, win in enumerate(B_WINDOWS):
        s = ze[:, gi * gw:(gi + 1) * gw]
        step = 1
        while step < win:
            s = s + _down(s, step)
            step *= 2
        inv = 1.0 / jnp.minimum(pos + 1, win).astype(F32)
        out.append(s[POOL_HALO:] * inv - ze[POOL_HALO:, gi * gw:(gi + 1) * gw])
    return out


def _b_specs(tm, width):
    return [pl.BlockSpec((POOL_HALO, width), lambda i: (_prev_halo(tm, POOL_HALO)(i), 0)),
            pl.BlockSpec((tm, width), lambda i: (i, 0))]


def b_mid_fwd(z, wgrp, bgrp, scale, name):
    m, width = z.shape
    ng = len(B_WINDOWS)
    gw = width // ng
    tm = _rows(m, 512)

    def body(zp_ref, zm_ref, w_ref, b_ref, s_ref, y_ref):
        i = pl.program_id(0)
        ze = jnp.concatenate([zp_ref[...] * (i > 0).astype(F32), zm_ref[...]], axis=0)
        p = _pool_minus_id(ze, i, tm, gw)
        for g in range(ng):
            cs = slice(g * gw, (g + 1) * gw)
            y = (_dot(p[g], w_ref[g]) + b_ref[:, cs]) * s_ref[:, cs]
            y_ref[:, cs] = y.astype(BF16)

    vec = pl.BlockSpec((1, width), lambda i: (0, 0))
    in_specs = _b_specs(tm, width) + [pl.BlockSpec((ng, gw, gw), lambda i: (0, 0, 0)), vec, vec]
    return _call(body, name, (m // tm,), in_specs, pl.BlockSpec((tm, width), lambda i: (i, 0)),
                 _sds((m, width), BF16))(z, z, wgrp, bgrp, scale)


def b_mid_bwd(z, dy, wgrp, bgrp, scale, name):
    m, width = z.shape
    ng = len(B_WINDOWS)
    gw = width // ng
    tm = _rows(m, 512)

    def body(zp_ref, zm_ref, dy_ref, w_ref, b_ref, s_ref, dp_ref, dw_ref, db_ref, ds_ref):
        i = pl.program_id(0)
        first = i == 0
        ze = jnp.concatenate([zp_ref[...] * (i > 0).astype(F32), zm_ref[...]], axis=0)
        p = _pool_minus_id(ze, i, tm, gw)
        for g in range(ng):
            cs = slice(g * gw, (g + 1) * gw)
            dyg = dy_ref[:, cs]
            ypre = _dot(p[g], w_ref[g]) + b_ref[:, cs]
            dyp = dyg * s_ref[:, cs]
            _acc(ds_ref.at[:, cs], _colsum(dyg * ypre), first)
            _acc(db_ref.at[:, cs], _colsum(dyp), first)
            _acc(dw_ref.at[g], _dot(p[g], dyp, _TN), first)
            dp_ref[:, cs] = _dot(dyp, w_ref[g], _NT)

    vec = pl.BlockSpec((1, width), lambda i: (0, 0))
    row = pl.BlockSpec((tm, width), lambda i: (i, 0))
    wsp = pl.BlockSpec((ng, gw, gw), lambda i: (0, 0, 0))
    return _call(body, name, (m // tm,), _b_specs(tm, width) + [row, wsp, vec, vec], [row, wsp, vec, vec],
                 [_sds((m, width)), _sds((ng, gw, gw)), _sds((1, width)), _sds((1, width))])(z, z, dy, wgrp, bgrp, scale)


def b_pool_bwd(dp, name):
    m, width = dp.shape
    gw = width // len(B_WINDOWS)
    tm = _rows(m, 512)
    n_i = m // tm

    def body(dm_ref, dn_ref, dz_ref):
        i = pl.program_id(0)
        de = jnp.concatenate([dm_ref[...], dn_ref[...] * (i < n_i - 1).astype(F32)], axis=0)
        pos = i * tm + lax.broadcasted_iota(jnp.int32, (tm + POOL_HALO, 1), 0)
        for gi, win in enumerate(B_WINDOWS):
            cs = slice(gi * gw, (gi + 1) * gw)
            d = de[:, cs]
            s = d * (1.0 / jnp.minimum(pos + 1, win).astype(F32))
            step = 1
            while step < win:
                s = s + _up(s, step)
                step *= 2
            dz_ref[:, cs] = (s[:tm] - d[:tm]).astype(BF16)

    in_specs = [pl.BlockSpec((tm, width), lambda i: (i, 0)),
                pl.BlockSpec((POOL_HALO, width), lambda i: (_next_halo(tm, m, POOL_HALO)(i), 0))]
    return _call(body, name, (n_i,), in_specs, pl.BlockSpec((tm, width), lambda i: (i, 0)), _sds((m, width), BF16))(dp, dp)


def _c_gates(xr, wa_ref, ba_ref, wi_ref, bi_ref, lam_ref, heads, hw):
    xb = xr.astype(BF16)
    ra = jnp.concatenate([_dot(xb[:, h * hw:(h + 1) * hw], wa_ref[h]) for h in range(heads)], axis=1) + ba_ref[...]
    ia = jnp.concatenate([_dot(xb[:, h * hw:(h + 1) * hw], wi_ref[h]) for h in range(heads)], axis=1) + bi_ref[...]
    r, ig = _sigmoid(ra), _sigmoid(ia)
    sp = _softplus(-lam_ref[...])
    log_a = (-C_GATE_C * r) * sp
    a = jnp.exp(log_a)
    mult = jnp.sqrt(-_expm1(2.0 * log_a))
    return xb, r, ig, sp, a, mult


def c_mid_fwd(z, cw, cb, wa, ba, wi, bi, lam, name):
    m, w2 = z.shape
    width = w2 // 2
    heads, hw = wa.shape[0], wa.shape[1]
    taps = cw.shape[0]
    tm = _rows(m, 512)

    def body(zp_ref, zm_ref, cw_ref, cb_ref, wa_ref, ba_ref, wi_ref, bi_ref, lam_ref, a_ref, b_ref, xr_ref):
        i = pl.program_id(0)
        xe = jnp.concatenate([zp_ref[...] * (i > 0).astype(F32), zm_ref[...]], axis=0)
        xr = _conv_ext(xe, cw_ref[...], taps)[HALO:] + cb_ref[...]
        _, _, ig, _, a, mult = _c_gates(xr, wa_ref, ba_ref, wi_ref, bi_ref, lam_ref, heads, hw)
        a_ref[...] = a
        b_ref[...] = mult * (ig * xr)
        xr_ref[...] = xr

    vec = pl.BlockSpec((1, width), lambda i: (0, 0))
    row = pl.BlockSpec((tm, width), lambda i: (i, 0))
    wsp = pl.BlockSpec((heads, hw, hw), lambda i: (0, 0, 0))
    in_specs = [pl.BlockSpec((HALO, width), lambda i: (_prev_halo(tm)(i), 1)), pl.BlockSpec((tm, width), lambda i: (i, 1)),
                pl.BlockSpec((taps, width), lambda i: (0, 0)), vec, wsp, vec, wsp, vec, vec]
    return _call(body, name, (m // tm,), in_specs, [row, row, row], [_sds((m, width))] * 3)(
        z, z, cw, cb, wa, ba, wi, bi, lam)


_SCAN_ROWS = 512


def c_scan_fwd(a, b, z, name):
    m, width = a.shape
    tm = _rows(m, _SCAN_ROWS)

    def body(a_ref, b_ref, g_ref, hs_ref, y_ref, h_carry):
        @pl.when(pl.program_id(0) == 0)
        def _():
            h_carry[...] = jnp.zeros_like(h_carry)

        def step(t, h):
            h = a_ref[pl.ds(t, 1), :] * h + b_ref[pl.ds(t, 1), :]
            hs_ref[pl.ds(t, 1), :] = h
            return h

        h_carry[...] = lax.fori_loop(0, tm, step, h_carry[...], unroll=8)
        y_ref[...] = (hs_ref[...] * _gelu(g_ref[...])).astype(BF16)

    row = pl.BlockSpec((tm, width), lambda i: (i, 0))
    return _call(body, name, (m // tm,), [row, row, row], [row, row], [_sds((m, width)), _sds((m, width), BF16)],
                 [pltpu.VMEM((1, width), F32)])(a, b, z)


def c_scan_bwd(dy, z, hs, a, name):
    m, width = a.shape
    tm = _rows(m, _SCAN_ROWS)
    n_i = m // tm

    def body(dy_ref, g_ref, hs_ref, hp_ref, a_ref, lam_ref, da_ref, dg_ref, dgs_ref, lam_carry, a_carry):
        i = pl.program_id(0)
        first = i == 0

        @pl.when(first)
        def _():
            lam_carry[...] = jnp.zeros_like(lam_carry)
            a_carry[...] = jnp.zeros_like(a_carry)

        gp, dyv, hsv = g_ref[...], dy_ref[...], hs_ref[...]
        dgate = dyv * hsv * _gelu_grad(gp)
        dg_ref[...] = dgate.astype(BF16)
        _acc(dgs_ref, _colsum(dgate), first)
        lam_ref[...] = dyv * _gelu(gp)

        def step(k, carry):
            lam_next, a_next = carry
            t = tm - 1 - k
            lam_t = lam_ref[pl.ds(t, 1), :] + a_next * lam_next
            lam_ref[pl.ds(t, 1), :] = lam_t
            return lam_t, a_ref[pl.ds(t, 1), :]

        lam_c, a_c = lax.fori_loop(0, tm, step, (lam_carry[...], a_carry[...]), unroll=8)
        lam_carry[...] = lam_c
        a_carry[...] = a_c
        h_before = hp_ref[HALO - 1:HALO, :] * (i < n_i - 1).astype(F32)
        t_i = lax.broadcasted_iota(jnp.int32, (tm, 1), 0)
        da_ref[...] = lam_ref[...] * jnp.where(t_i == 0, h_before, _down(hsv, 1))

    row = pl.BlockSpec((tm, width), lambda i: (n_i - 1 - i, 0))
    halo = pl.BlockSpec((HALO, width), lambda i: (_prev_halo(tm)(n_i - 1 - i), 0))
    vec = pl.BlockSpec((1, width), lambda i: (0, 0))
    return _call(body, name, (n_i,), [row, row, row, halo, row], [row, row, row, vec],
                 [_sds((m, width)), _sds((m, width)), _sds((m, width), BF16), _sds((1, width))],
                 [pltpu.VMEM((1, width), F32), pltpu.VMEM((1, width), F32)])(dy, z, hs, hs, a)


def c_mid_bwd(lam_seq, da, xr, wa, ba, wi, bi, lam, name):
    m, width = xr.shape
    heads, hw = wa.shape[0], wa.shape[1]
    tm = _rows(m, 512)

    def body(l_ref, da_ref, xr_ref, wa_ref, ba_ref, wi_ref, bi_ref, lam_ref,
             dxr_ref, dwa_ref, dwi_ref, dba_ref, dbi_ref, dlam_ref):
        first = pl.program_id(0) == 0
        xr_v, lmb = xr_ref[...], l_ref[...]
        xb, r, ig, sp, a, mult = _c_gates(xr_v, wa_ref, ba_ref, wi_ref, bi_ref, lam_ref, heads, hw)
        dmult = lmb * (ig * xr_v)
        dig = lmb * mult * xr_v
        dxr = lmb * mult * ig
        dla = da_ref[...] * a - dmult * (a * a) / mult
        dr = dla * (-C_GATE_C * sp)
        dsp = _colsum(dla * (-C_GATE_C * r))
        _acc(dlam_ref, dsp * (-_sigmoid(-lam_ref[...])), first)
        dra = dr * r * (1.0 - r)
        dia = dig * ig * (1.0 - ig)
        _acc(dba_ref, _colsum(dra), first)
        _acc(dbi_ref, _colsum(dia), first)
        for h in range(heads):
            cs = slice(h * hw, (h + 1) * hw)
            _acc(dwa_ref.at[h], _dot(xb[:, cs], dra[:, cs], _TN), first)
            _acc(dwi_ref.at[h], _dot(xb[:, cs], dia[:, cs], _TN), first)
            dxr_ref[:, cs] = dxr[:, cs] + _dot(dra[:, cs], wa_ref[h], _NT) + _dot(dia[:, cs], wi_ref[h], _NT)

    vec = pl.BlockSpec((1, width), lambda i: (0, 0))
    row = pl.BlockSpec((tm, width), lambda i: (i, 0))
    wsp = pl.BlockSpec((heads, hw, hw), lambda i: (0, 0, 0))
    return _call(body, name, (m // tm,), [row, row, row, wsp, vec, wsp, vec, vec], [row, wsp, wsp, vec, vec, vec],
                 [_sds((m, width)), _sds((heads, hw, hw)), _sds((heads, hw, hw)), _sds((1, width)), _sds((1, width)),
                  _sds((1, width))])(lam_seq, da, xr, wa, ba, wi, bi, lam)


def conv_bwd(dy, x_src, col_block, cw, name):
    m, width = dy.shape
    taps = cw.shape[0]
    tm = _rows(m, 512)
    n_i = m // tm

    def body(dm_ref, dn_ref, xp_ref, xm_ref, cw_ref, dx_ref, dw_ref, db_ref, dxs_ref):
        i = pl.program_id(0)
        first = i == 0
        de = jnp.concatenate([jnp.zeros((HALO, width), F32), dm_ref[...], dn_ref[...] * (i < n_i - 1).astype(F32)], axis=0)
        xe = jnp.concatenate([xp_ref[...] * (i > 0).astype(F32), xm_ref[...], jnp.zeros((HALO, width), F32)], axis=0)
        w = cw_ref[...]
        dx = de * w[taps - 1:taps]
        for s in range(1, taps):
            dx = dx + _up(de, s) * w[taps - 1 - s:taps - s]
        dx_ref[...] = dx[HALO:HALO + tm].astype(BF16)
        _acc(dxs_ref, _colsum(dx[HALO:HALO + tm]), first)
        dm = dm_ref[...]
        for s in range(taps):
            _acc(dw_ref.at[taps - 1 - s:taps - s, :], _colsum(dm * _down(xe, s)[HALO:HALO + tm]), first)
        _acc(db_ref, _colsum(dm), first)

    row = pl.BlockSpec((tm, width), lambda i: (i, 0))
    vec = pl.BlockSpec((1, width), lambda i: (0, 0))
    tsp = pl.BlockSpec((taps, width), lambda i: (0, 0))
    in_specs = [row, pl.BlockSpec((HALO, width), lambda i: (_next_halo(tm, m)(i), 0)),
                pl.BlockSpec((HALO, width), lambda i: (_prev_halo(tm)(i), col_block)),
                pl.BlockSpec((tm, width), lambda i: (i, col_block)), tsp]
    return _call(body, name, (n_i,), in_specs, [row, tsp, vec, vec],
                 [_sds((m, width), BF16), _sds((taps, width)), _sds((1, width)), _sds((1, width))])(dy, dy, x_src, x_src, cw)


def d_mid_fwd(z, cw, name):
    m, w3 = z.shape
    width = w3 // 3
    taps = cw.shape[0]
    tm = _rows(m, 512)

    def body(bm_ref, cp_ref, cm_ref, xp_ref, xm_ref, cw_ref, y_ref):
        keep = (pl.program_id(0) > 0).astype(F32)
        qe = (jnp.concatenate([cp_ref[...] * keep, cm_ref[...]], axis=0)
              * jnp.concatenate([xp_ref[...], xm_ref[...]], axis=0))
        y_ref[...] = (bm_ref[...] * _conv_ext(qe, cw_ref[...], taps)[HALO:]).astype(BF16)

    main = lambda c: pl.BlockSpec((tm, width), lambda i: (i, c))
    prev = lambda c: pl.BlockSpec((HALO, width), lambda i: (_prev_halo(tm)(i), c))
    in_specs = [main(0), prev(1), main(1), prev(2), main(2), pl.BlockSpec((taps, width), lambda i: (0, 0))]
    return _call(body, name, (m // tm,), in_specs, pl.BlockSpec((tm, width), lambda i: (i, 0)),
                 _sds((m, width), BF16))(z, z, z, z, z, cw)


def d_mid_bwd(z, dy, cw, name):
    m, w3 = z.shape
    width = w3 // 3
    taps = cw.shape[0]
    tm = _rows(m, 512)
    n_i = m // tm

    def body(bm_ref, bn_ref, cp_ref, cm_ref, cn_ref, xp_ref, xm_ref, xn_ref, dm_ref, dn_ref, cw_ref, dz_ref, dw_ref):
        i = pl.program_id(0)
        first = i == 0
        kp, kn = (i > 0).astype(F32), (i < n_i - 1).astype(F32)
        zeros = jnp.zeros((HALO, width), F32)
        ce = jnp.concatenate([cp_ref[...] * kp, cm_ref[...], cn_ref[...] * kn], axis=0)
        xe = jnp.concatenate([xp_ref[...], xm_ref[...], xn_ref[...]], axis=0)
        qe = ce * xe
        be = jnp.concatenate([zeros, bm_ref[...], bn_ref[...]], axis=0)
        dye = jnp.concatenate([zeros, dm_ref[...], dn_ref[...] * kn], axis=0)
        w = cw_ref[...]
        cq = _conv_ext(qe, w, taps)
        dcq = dye * be
        dq = dcq * w[taps - 1:taps]
        for s in range(1, taps):
            dq = dq + _up(dcq, s) * w[taps - 1 - s:taps - s]
        ms = slice(HALO, HALO + tm)
        dz_ref[:, :width] = (dye * cq)[ms].astype(BF16)
        dz_ref[:, width:2 * width] = (dq * xe)[ms].astype(BF16)
        dz_ref[:, 2 * width:] = (dq * ce)[ms].astype(BF16)
        for s in range(taps):
            _acc(dw_ref.at[taps - 1 - s:taps - s, :], _colsum(dcq[ms] * _down(qe, s)[ms]), first)

    main = lambda c: pl.BlockSpec((tm, width), lambda i: (i, c))
    prev = lambda c: pl.BlockSpec((HALO, width), lambda i: (_prev_halo(tm)(i), c))
    nxt = lambda c: pl.BlockSpec((HALO, width), lambda i: (_next_halo(tm, m)(i), c))
    tsp = pl.BlockSpec((taps, width), lambda i: (0, 0))
    in_specs = [main(0), nxt(0), prev(1), main(1), nxt(1), prev(2), main(2), nxt(2), main(0), nxt(0), tsp]
    return _call(body, name, (n_i,), in_specs, [pl.BlockSpec((tm, w3), lambda i: (i, 0)), tsp],
                 [_sds((m, w3), BF16), _sds((taps, width))])(z, z, z, z, z, z, z, z, dy, dy, cw)


def _halo_rows(dtype):
    return HALO * (4 // jnp.dtype(dtype).itemsize)


def ffn_gate_down(z, cw, cb, w_down, res, next_g, name):
    _, nj, m, c = z.shape
    n = w_down.shape[1]
    taps = cw.shape[2]
    tm = _rows(m, 512)
    hz = _halo_rows(z.dtype)

    def body(zp_ref, zm_ref, cw_ref, cb_ref, w_ref, res_ref, *rest):
        keep = (pl.program_id(0) > 0).astype(F32)
        act_ref, x_ref = rest[-3:-1] if next_g is not None else rest[-2:]
        acc = res_ref[...]
        for j in range(nj):
            zc = []
            for s in range(2):
                xe = jnp.concatenate([zp_ref[s, j].astype(F32) * keep, zm_ref[s, j].astype(F32)], axis=0)
                zc.append(_conv_ext(xe, cw_ref[s, j], taps)[hz:] + cb_ref[s, j])
            act = (zc[0] * _sigmoid(zc[0]) * zc[1]).astype(BF16)
            act_ref[j] = act
            acc = acc + _dot(act, w_ref[j])
        x_ref[...] = acc
        if next_g is not None:
            rest[-1][...] = _norm_rows(acc, rest[0][...])

    row = pl.BlockSpec((tm, n), lambda i: (i, 0))
    in_specs = [pl.BlockSpec((2, nj, hz, c), lambda i: (0, 0, _prev_halo(tm, hz)(i), 0)),
                pl.BlockSpec((2, nj, tm, c), lambda i: (0, 0, i, 0)),
                _resident(cw.shape), _resident(cb.shape), _resident((nj, c, n)), row]
    out_specs = [pl.BlockSpec((nj, tm, c), lambda i: (0, i, 0)), row]
    out_shape = [_sds((nj, m, c), BF16), _sds((m, n))]
    ins = [z, z, cw, cb, w_down.reshape(nj, c, n), res]
    if next_g is not None:
        in_specs.append(pl.BlockSpec((1, n), lambda i: (0, 0)))
        out_specs.append(row)
        out_shape.append(_sds((m, n), BF16))
        ins.append(next_g)
    return _call(body, name, (m // tm,), in_specs, out_specs, out_shape)(*ins)


def ffn_gate_bwd(z, dx, w_down, cw, cb, name, after=()):
    _, nj, m, c = z.shape
    n = w_down.shape[1]
    taps = cw.shape[2]
    tm = _rows(m, 512)
    n_i = m // tm
    hz = _halo_rows(z.dtype)
    assert _halo_rows(dx.dtype) == hz, (z.dtype, dx.dtype)

    def body(zp_ref, zm_ref, zn_ref, dm_ref, dn_ref, wd_ref, cw_ref, cb_ref, dz_ref, dw_ref, db_ref):
        i = pl.program_id(1)
        first = i == 0
        kp, kn = (i > 0).astype(F32), (i < n_i - 1).astype(F32)
        xe = [jnp.concatenate([zp_ref[s].astype(F32) * kp, zm_ref[s].astype(F32), zn_ref[s].astype(F32) * kn], axis=0)
              for s in range(2)]
        zc = [_conv_ext(xe[s], cw_ref[s], taps) + cb_ref[s] for s in range(2)]
        dact = _dot(jnp.concatenate([dm_ref[...], dn_ref[...]], axis=0), wd_ref[...], _NT)
        dae = jnp.concatenate([jnp.zeros((hz, c), F32), dact[:tm], dact[tm:] * kn], axis=0)
        sg = _sigmoid(zc[0])
        dzc = [dae * zc[1] * (sg * (1.0 + zc[0] * (1.0 - sg))), dae * (zc[0] * sg)]
        ms = slice(hz, hz + tm)
        for s in range(2):
            w = cw_ref[s]
            ups = [dzc[s]] + [_up(dzc[s], u) for u in range(1, taps)]
            dxs = ups[0] * w[taps - 1:taps]
            for u in range(1, taps):
                dxs = dxs + ups[u] * w[taps - 1 - u:taps - u]
            dz_ref[s] = dxs[ms].astype(BF16)
            tail = dzc[s][hz + tm:]
            x_end = xe[s][tm:]
            for u in range(taps):
                total = _colsum(ups[u] * xe[s]) - _colsum(tail * _down(x_end, u)[hz:])
                _acc(dw_ref.at[s, taps - 1 - u:taps - u, :], total, first)
            _acc(db_ref.at[s], _colsum(dzc[s][ms]), first)

    in_specs = [pl.BlockSpec((2, None, hz, c), lambda j, i: (0, j, _prev_halo(tm, hz)(i), 0)),
                pl.BlockSpec((2, None, tm, c), lambda j, i: (0, j, i, 0)),
                pl.BlockSpec((2, None, hz, c), lambda j, i: (0, j, _next_halo(tm, m, hz)(i), 0)),
                pl.BlockSpec((tm, n), lambda j, i: (i, 0)),
                pl.BlockSpec((hz, n), lambda j, i: (_next_halo(tm, m, hz)(i), 0)),
                pl.BlockSpec((None, c, n), lambda j, i: (j, 0, 0)),
                pl.BlockSpec((2, None, taps, c), lambda j, i: (0, j, 0, 0)),
                pl.BlockSpec((2, None, 1, c), lambda j, i: (0, j, 0, 0))]
    out_specs = [pl.BlockSpec((2, None, tm, c), lambda j, i: (0, j, i, 0)),
                 pl.BlockSpec((2, None, taps, c), lambda j, i: (0, j, 0, 0)),
                 pl.BlockSpec((2, None, 1, c), lambda j, i: (0, j, 0, 0))]
    return _call(body, name, (nj, n_i), in_specs, out_specs,
                 [_sds((2, nj, m, c), BF16), _sds((2, nj, taps, c)), _sds((2, nj, 1, c))], (), after)(
        z, z, z, dx, dx, w_down.reshape(nj, c, n), cw, cb)


_STREAM_TILE_BYTES = 2 << 20


def _tile_rows(r, c):
    if r * c * 4 <= _STREAM_TILE_BYTES:
        return r
    fits = [d for d in range(16, r, 16) if r % d == 0 and d * c * 4 <= _STREAM_TILE_BYTES]
    return max(fits) if fits else r


def _as2d(a, lead):
    shape = a.shape
    return a.reshape((lead, -1, shape[-1]) if lead else (-1, shape[-1]))


def add_pairs(g, l1, own, name):
    shape = l1.shape
    g3, l3 = _as2d(g, N_DEV), _as2d(l1, 4)
    _, r, c = l3.shape
    tr = _tile_rows(r, c)

    def body(own_ref, a_ref, b_ref, o_ref):
        o_ref[...] = (a_ref[...].astype(F32) + b_ref[...].astype(F32)).astype(o_ref.dtype)

    spec = pl.BlockSpec((None, tr, c), lambda k, i, own_ref: (k, i, 0))
    grid_spec = pltpu.PrefetchScalarGridSpec(
        num_scalar_prefetch=1, grid=(4, r // tr),
        in_specs=[pl.BlockSpec((None, tr, c), lambda k, i, own_ref: (own_ref[k], i, 0)), spec], out_specs=spec)
    out = pl.pallas_call(
        body, name=name, grid_spec=grid_spec, out_shape=_sds(l3.shape, l1.dtype),
        compiler_params=pltpu.CompilerParams(dimension_semantics=("arbitrary", "arbitrary"),
                                             vmem_limit_bytes=V7X_VMEM_LIMIT_BYTES))(own, g3, l3)
    return out.reshape(shape)


def _grad_sum(p_ref, l_ref):
    return ((p_ref[...].astype(F32) + l_ref[0].astype(F32)) + l_ref[1].astype(F32)) + l_ref[2].astype(F32)


def sum_parts(p, l2, name):
    _, r, c = p.shape

    def body(p_ref, l_ref, o_ref):
        o_ref[...] = _grad_sum(p_ref, l_ref)

    return _call(body, name, (1,), [pl.BlockSpec((None, r, c), lambda i: (0, 0, 0)), pl.BlockSpec((3, r, c), lambda i: (0, 0, 0))],
                 pl.BlockSpec((r, c), lambda i: (0, 0)), _sds((r, c)))(p, l2)


def _adamw_math(w, g, m, v):
    m = ADAM_B1 * m + (1.0 - ADAM_B1) * g
    v = ADAM_B2 * v + (1.0 - ADAM_B2) * (g * g)
    m_hat = m / (1.0 - ADAM_B1 ** ADAM_STEP)
    v_hat = v / (1.0 - ADAM_B2 ** ADAM_STEP)
    delta = -ADAM_LR * (m_hat / (jnp.sqrt(v_hat) + ADAM_EPS) + ADAM_WD * w)
    return delta, m, v


def adamw(w, m, v, name, g=None, p=None, l2=None):
    shape = w.shape
    w2, m2, v2 = (_as2d(t, 0) for t in (w, m, v))
    r, c = w2.shape
    tr = _tile_rows(r, c)
    row = pl.BlockSpec((tr, c), lambda i: (i, 0))
    if g is None:
        p3, l3 = _as2d(p, 4), _as2d(l2, 3)
        gin = (p3, l3)
        gspecs = [pl.BlockSpec((None, tr, c), lambda i: (0, i, 0)), pl.BlockSpec((3, tr, c), lambda i: (0, i, 0))]
    else:
        gin, gspecs = (_as2d(g, 0),), [row]

    def body(*refs):
        n_g = len(gin)
        w_ref, m_ref, v_ref, g_ref, d_ref, nm_ref, nv_ref = refs[n_g:]
        grad = refs[0][...] if n_g == 1 else _grad_sum(refs[0], refs[1])
        delta, nm, nv = _adamw_math(w_ref[...], grad, m_ref[...], v_ref[...])
        g_ref[...] = grad
        d_ref[...] = delta
        nm_ref[...] = nm
        nv_ref[...] = nv

    outs = _call(body, name, (r // tr,), gspecs + [row, row, row], [row] * 4, [_sds((r, c))] * 4)(*gin, w2, m2, v2)
    return tuple(o.reshape(shape) for o in outs)


def adamw_layer(w, m, v, layer, prev, name, p, l2):
    n_l, r, c = w.shape
    tr = _tile_rows(r, c)
    slab = pl.BlockSpec((None, tr, c), lambda i: (layer, i, 0))
    in_specs = [pl.BlockSpec((None, tr, c), lambda i: (0, i, 0)), pl.BlockSpec((3, tr, c), lambda i: (0, i, 0)), slab, slab, slab]
    n_in = len(in_specs)
    prev = () if prev is None else tuple(prev)

    def body(p_ref, l_ref, w_ref, m_ref, v_ref, *rest):
        g_ref, d_ref, nm_ref, nv_ref = rest[len(prev):]
        grad = _grad_sum(p_ref, l_ref)
        delta, nm, nv = _adamw_math(w_ref[...], grad, m_ref[...], v_ref[...])
        g_ref[...] = grad
        d_ref[...] = delta
        nm_ref[...] = nm
        nv_ref[...] = nv

    return pl.pallas_call(
        body, name=name, grid=(r // tr,), in_specs=in_specs + [pl.BlockSpec(memory_space=pl.ANY)] * len(prev),
        out_specs=[slab] * 4, out_shape=[_sds((n_l, r, c))] * 4,
        input_output_aliases={n_in + q: q for q in range(len(prev))},
        compiler_params=pltpu.CompilerParams(dimension_semantics=("arbitrary",), vmem_limit_bytes=V7X_VMEM_LIMIT_BYTES),
    )(_as2d(p, 4), _as2d(l2, 3), w, m, v, *prev)


def _comm_call(body, name, ins, out_shape, n_sems):
    any_spec = pl.BlockSpec(memory_space=pl.ANY)
    return pl.pallas_call(
        body, name=name, in_specs=[any_spec] * len(ins), out_specs=[any_spec] * len(out_shape), out_shape=out_shape,
        scratch_shapes=[pltpu.SemaphoreType.DMA((n,)) for n in n_sems],
        compiler_params=pltpu.CompilerParams(has_side_effects=True))(*ins)


def _place():
    return lax.axis_index("x"), lax.axis_index("y"), lax.axis_index("c")


def _dev_index(px, py, pc):
    return 4 * px + 2 * py + pc


def all_gather(blocks, name):
    n_t = len(blocks)

    def body(*refs):
        ins, outs = refs[:n_t], refs[n_t:2 * n_t]
        send_sems, recv_sems, local_sems = refs[2 * n_t:]
        x, y, c = _place()
        me, sibling = (x, y, c), (x, y, 1 - c)
        chips = [(1 - x, y), (x, 1 - y), (1 - x, 1 - y)]

        def copy(t, k, block, to, src=None):
            dst = outs[t].at[_dev_index(*block)]
            return pltpu.make_async_remote_copy(
                src_ref=dst if src is None else src, dst_ref=dst, send_sem=send_sems.at[t * 7 + k],
                recv_sem=recv_sems.at[t * 7 + k], device_id=to, device_id_type=MESH_ID)

        mine = [pltpu.make_async_copy(ins[t], outs[t].at[_dev_index(*me)], local_sems.at[t]) for t in range(n_t)]
        for cp in mine:
            cp.start()
        first = []
        for t in range(n_t):
            first.append(copy(t, 0, me, sibling, src=ins[t]))
            first += [copy(t, 1 + j, me, (*chip, c), src=ins[t]) for j, chip in enumerate(chips)]
        for cp in first:
            cp.start()
        passed = []
        for t in range(n_t):
            for j, chip in enumerate(chips):
                copy(t, 1 + j, (*chip, c), me).wait_recv()
                cp = copy(t, 4 + j, (*chip, c), sibling)
                cp.start()
                passed.append(cp)
        for t in range(n_t):
            copy(t, 0, sibling, me).wait_recv()
            for j, chip in enumerate(chips):
                copy(t, 4 + j, (*chip, 1 - c), me).wait_recv()
        for cp in first + passed:
            cp.wait_send()
        for cp in mine:
            cp.wait()

    out_shape = [_sds((N_DEV,) + b.shape, b.dtype) for b in blocks]
    return _comm_call(body, name, blocks, out_shape, (7 * n_t, 7 * n_t, n_t))


def _chip_of(x, y, k):
    return (x if k % 2 == 0 else 1 - x), (y if k // 2 == 0 else 1 - y)


_HBM_SPEC = pl.BlockSpec(memory_space=pltpu.HBM)
_SEM_SPEC = pl.BlockSpec(memory_space=pltpu.SEMAPHORE)
_DATAFLOW = pltpu.SideEffectType.DATAFLOW_SIDE_EFFECTING


def _in_hbm(a):
    return pltpu.with_memory_space_constraint(a, pltpu.HBM)


def _split_start(name, issue, srcs, land_shapes, sem_counts):
    n_buf, n_sem = len(srcs) + len(land_shapes), len(sem_counts)

    def body(*refs):
        issue(refs[:len(srcs)], refs[len(srcs):n_buf], refs[n_buf:n_buf + n_sem])
        refs[-1][...] = jnp.zeros_like(refs[-1])

    bufs = [pltpu.HBM(s.shape, s.dtype) for s in list(srcs) + list(land_shapes)]
    outs = pl.pallas_call(
        body, name=name, in_specs=(_HBM_SPEC,) * n_buf,
        out_shape=(*[pltpu.SemaphoreType.DMA((n,)) for n in sem_counts], *bufs, _sds((8, LANES))),
        out_specs=(*[_SEM_SPEC] * n_sem, *[_HBM_SPEC] * n_buf, pl.BlockSpec(memory_space=pltpu.VMEM)),
        input_output_aliases={i: n_sem + i for i in range(n_buf)},
        compiler_params=pltpu.CompilerParams(has_side_effects=_DATAFLOW),
    )(*[_in_hbm(s) for s in srcs], *[_in_hbm(lax.empty(s.shape, s.dtype)) for s in land_shapes])
    return outs[:n_sem], outs[n_sem:n_sem + len(srcs)], outs[n_sem + len(srcs):n_sem + n_buf], outs[-1]


def _split_wait(name, finish, sems, srcs, lands, after):
    n_buf, n_sem = len(srcs) + len(lands), len(sems)

    def body(*refs):
        finish(refs[:len(srcs)], refs[len(srcs):n_buf], refs[n_buf:n_buf + n_sem])

    bufs = [pltpu.HBM(s.shape, s.dtype) for s in list(srcs) + list(lands)]
    outs = pl.pallas_call(
        body, name=name, in_specs=(*[_HBM_SPEC] * n_buf, *[_SEM_SPEC] * n_sem, *[pl.BlockSpec(memory_space=pl.ANY)] * len(after)),
        out_shape=tuple(bufs), out_specs=(_HBM_SPEC,) * n_buf, input_output_aliases={i: i for i in range(n_buf)},
        compiler_params=pltpu.CompilerParams(has_side_effects=_DATAFLOW),
    )(*srcs, *lands, *sems, *after)
    return outs[:len(srcs)], outs[len(srcs):]


def _peer(x, y, c, r):
    return (1 - x if r & 4 else x), (1 - y if r & 2 else y), (1 - c if r & 1 else c)


def _gather_copies(src_refs, land_refs, sem_refs, arrivals):
    send_sems, recv_sems, local_sems = sem_refs
    x, y, c = _place()
    me = _dev_index(x, y, c)
    local, sends, recvs = [], [], []
    for j, (src, land) in enumerate(zip(src_refs, land_refs)):
        local.append(pltpu.make_async_copy(src, land.at[me], local_sems.at[j]))
        for r in range(1, N_DEV):
            peer = _peer(x, y, c, r)
            q = (N_DEV - 1) * j + r - 1
            sends.append(pltpu.make_async_remote_copy(src_ref=src, dst_ref=land.at[me], send_sem=send_sems.at[q],
                                                      recv_sem=recv_sems.at[q], device_id=peer, device_id_type=MESH_ID))
            if arrivals:
                recvs.append(pltpu.make_async_remote_copy(
                    src_ref=src, dst_ref=land.at[_dev_index(*peer)], send_sem=send_sems.at[q], recv_sem=recv_sems.at[q],
                    device_id=peer, device_id_type=MESH_ID))
    return local, sends, recvs


def gather_start(groups, name):
    flat = [b for g in groups for b in g]
    bounds = [sum(len(g) for g in groups[:i]) for i in range(len(groups) + 1)]

    def issue(src_refs, land_refs, sem_refs):
        for i in range(len(groups)):
            lo, hi = bounds[i], bounds[i + 1]
            local, sends, _ = _gather_copies(src_refs[lo:hi], land_refs[lo:hi], sem_refs[3 * i:3 * i + 3], False)
            for cp in local + sends:
                cp.start()

    sem_counts = [n for g in groups for n in ((N_DEV - 1) * len(g), (N_DEV - 1) * len(g), len(g))]
    sems, srcs, lands, token = _split_start(name, issue, flat, [_sds((N_DEV,) + b.shape, b.dtype) for b in flat], sem_counts)
    return [(sems[3 * i:3 * i + 3], srcs[bounds[i]:bounds[i + 1]], lands[bounds[i]:bounds[i + 1]])
            for i in range(len(groups))], token


def gather_wait(group, after, name):
    sems, srcs, lands = group

    def finish(src_refs, land_refs, sem_refs):
        local, sends, recvs = _gather_copies(src_refs, land_refs, sem_refs, True)
        for cp in local:
            cp.wait()
        for cp in recvs:
            cp.wait_recv()
        for cp in sends:
            cp.wait_send()

    return _split_wait(name, finish, sems, srcs, lands, after)[1]


def _sibling_copies(src_refs, land_refs, sem_refs):
    send_sems, recv_sems = sem_refs
    x, y, c = _place()
    copies = []
    for t, (src, land) in enumerate(zip(src_refs, land_refs)):
        for k in range(4):
            cx, cy = _chip_of(x, y, k)
            copies.append(pltpu.make_async_remote_copy(
                src_ref=src.at[_dev_index(cx, cy, 1 - c)], dst_ref=land.at[k], send_sem=send_sems.at[4 * t + k],
                recv_sem=recv_sems.at[4 * t + k], device_id=(x, y, 1 - c), device_id_type=MESH_ID))
    return copies


def _chip_copies(src_refs, land_refs, sem_refs):
    send_sems, recv_sems = sem_refs
    x, y, c = _place()
    copies = []
    for t, (src, land) in enumerate(zip(src_refs, land_refs)):
        for k in range(1, 4):
            cx, cy = _chip_of(x, y, k)
            copies.append(pltpu.make_async_remote_copy(
                src_ref=src.at[k], dst_ref=land.at[k - 1], send_sem=send_sems.at[3 * t + k - 1],
                recv_sem=recv_sems.at[3 * t + k - 1], device_id=(cx, cy, c), device_id_type=MESH_ID))
    return copies


def _exchange_start(copies_of, n_land, per_array, arrays, name):
    def issue(src_refs, land_refs, sem_refs):
        for cp in copies_of(src_refs, land_refs, sem_refs):
            cp.start()

    n = per_array * len(arrays)
    lands = [_sds((n_land,) + a.shape[1:], a.dtype) for a in arrays]
    return _split_start(name, issue, arrays, lands, (n, n))


def _exchange_wait(copies_of, started, after, name):
    sems, srcs, lands, _ = started

    def finish(src_refs, land_refs, sem_refs):
        copies = copies_of(src_refs, land_refs, sem_refs)
        for cp in copies:
            cp.wait_recv()
        for cp in copies:
            cp.wait_send()

    return _split_wait(name, finish, sems, srcs, lands, after)


def sibling_start(grads, name):
    return _exchange_start(_sibling_copies, 4, 4, grads, name)


def sibling_wait(started, after, name):
    return _exchange_wait(_sibling_copies, started, after, name)


def chips_start(parts, name):
    return _exchange_start(_chip_copies, 3, 3, parts, name)


def chips_wait(started, after, name):
    return _exchange_wait(_chip_copies, started, after, name)


def _pack(arrays, rows):
    flat = jnp.concatenate([a.reshape(-1) for a in arrays])
    return jnp.pad(flat, (0, rows * LANES - flat.shape[0])).reshape(rows, LANES)


def _pack_stacked(arrays, rows):
    flat = jnp.concatenate([a.reshape(N_DEV, -1) for a in arrays], axis=1)
    return jnp.pad(flat, ((0, 0), (0, rows * LANES - flat.shape[1]))).reshape(N_DEV, rows, LANES)


def _unpack(buf, shapes, lead=()):
    flat = buf.reshape(lead + (-1,))
    out, off = [], 0
    for s in shapes:
        n = 1
        for d in s:
            n *= d
        out.append(flat[..., off:off + n].reshape(lead + tuple(s)))
        off += n
    return out


def _padded_rows(shapes, multiple):
    n = sum(functools.reduce(lambda a, b: a * b, s, 1) for s in shapes)
    rows = -(-n // LANES)
    return -(-rows // multiple) * multiple


def _to_full(stacked, axis):
    t = jnp.moveaxis(stacked, 0, axis)
    return t.reshape(t.shape[:axis] + (t.shape[axis] * t.shape[axis + 1],) + t.shape[axis + 2:])


def _to_stacked(full, axis):
    s = full.shape
    t = full.reshape(s[:axis] + (N_DEV, s[axis] // N_DEV) + s[axis + 1:])
    return jnp.moveaxis(t, axis, 0)


def _ffn_forward(x, h, w_up, cw, cb, w_down, next_g, tag):
    z = mm_in(h, w_up, f"ffn{tag}_up", stacked_out=True, out_dtype=BF16, rows=2048)
    nb, m, c = z.shape
    z4 = z.reshape(2, nb // 2, m, c)
    act, *out = ffn_gate_down(z4, cw, cb, w_down, x, next_g, f"ffn{tag}_gate_down")
    return (out[0], out[1] if next_g is not None else None), (x, h, z4, act)


def _ffn_backward(dx, dxb, saved, norm_g, w_up, cw, cb, w_down, tag, after=()):
    x, h, z4, act = saved
    nj = act.shape[0]
    dz4, dcw, dcb = ffn_gate_bwd(z4, dxb, w_down, cw, cb, f"ffn{tag}_gate_bwd", after)
    dw_down = mm_dw_out(act, dxb, f"ffn{tag}_down_dw")
    dz = dz4.reshape((2 * nj,) + dz4.shape[2:])
    dx, dxb, dg = mm_dx_in(dz, w_up, f"ffn{tag}_up_dx", norm=(x, norm_g, dx))
    dw_up = mm_dw_in(h, dz, 2 * nj, f"ffn{tag}_up_dw", transposed=True)
    return dx, dxb, dict(norm_g=dg, w_up=dw_up, conv_w=dcw, conv_b=dcb, w_down=dw_down)


def kernel(x, a_norm_g, a_w_in, a_b_in, a_v_norm_g, a_w_s, a_b_s, a_w_out, b_norm_g, b_w_in, b_w_grp, b_b_grp, b_scale, b_w_out, c_norm_g, c_w_in, c_b_in, c_conv_w, c_conv_b, c_w_a, c_b_a, c_w_i, c_b_i, c_lambda, c_w_out, d_norm_g, d_w_in, d_conv_w, d_w_out, ffn_norm_g, ffn_w_up, ffn_conv_w, ffn_conv_b, ffn_w_down, final_norm_g, loss_target, m_a_norm_g, m_a_w_in, m_a_b_in, m_a_v_norm_g, m_a_w_s, m_a_b_s, m_a_w_out, m_b_norm_g, m_b_w_in, m_b_w_grp, m_b_b_grp, m_b_scale, m_b_w_out, m_c_norm_g, m_c_w_in, m_c_b_in, m_c_conv_w, m_c_conv_b, m_c_w_a, m_c_b_a, m_c_w_i, m_c_b_i, m_c_lambda, m_c_w_out, m_d_norm_g, m_d_w_in, m_d_conv_w, m_d_w_out, m_ffn_norm_g, m_ffn_w_up, m_ffn_conv_w, m_ffn_conv_b, m_ffn_w_down, m_final_norm_g, v_a_norm_g, v_a_w_in, v_a_b_in, v_a_v_norm_g, v_a_w_s, v_a_b_s, v_a_w_out, v_b_norm_g, v_b_w_in, v_b_w_grp, v_b_b_grp, v_b_scale, v_b_w_out, v_c_norm_g, v_c_w_in, v_c_b_in, v_c_conv_w, v_c_conv_b, v_c_w_a, v_c_b_a, v_c_w_i, v_c_b_i, v_c_lambda, v_c_w_out, v_d_norm_g, v_d_w_in, v_d_conv_w, v_d_w_out, v_ffn_norm_g, v_ffn_w_up, v_ffn_conv_w, v_ffn_conv_b, v_ffn_w_down, v_final_norm_g):
    args = locals()
    w_loc = {n: args[n] for n in WEIGHTS}
    m_loc = {n: args["m_" + n] for n in WEIGHTS}
    v_loc = {n: args["v_" + n] for n in WEIGHTS}
    depth = ffn_w_up.shape[0]
    xs = x[0]
    target = loss_target[0]

    small_names = list(SMALL_SHARDED)
    small_shapes = [w_loc[n].shape for n in small_names]
    small_rows = _padded_rows(small_shapes, 8)
    small_packed = _pack([w_loc[n] for n in small_names], small_rows)
    early_names = ['ffn_conv_w', 'b_norm_g', 'c_norm_g', 'd_norm_g']
    late_names = [n for n in small_names if n not in early_names]
    packs = []
    for names in (early_names, late_names):
        shapes = [w_loc[n].shape for n in names]
        packs.append((names, shapes, _pack([w_loc[n] for n in names], _padded_rows(shapes, 8))))
    mixers = [(a_w_in, a_w_out), (b_w_in, b_w_out), (c_w_in, c_w_out), (d_w_in, d_w_out)]
    groups = []
    for l in range(depth):
        groups.append([mixers[l][0][0].astype(BF16), mixers[l][1][0].astype(BF16)] + ([packs[1][2]] if l == 1 else []))
        groups.append([ffn_w_up[l].astype(BF16), ffn_w_down[l].astype(BF16)] + ([packs[0][2]] if l == 0 else []))
    in_flight, _ = gather_start(groups, "gather_start")
    sm = {}

    def unpack_small(pack, gathered):
        names, shapes, _ = pack
        sm.update((n, _to_full(s, SMALL_SHARDED[n])) for n, s in zip(names, _unpack(gathered, shapes, (N_DEV,))))

    def rows_full(st):
        return st.reshape((st.shape[0] * st.shape[1],) + st.shape[2:])

    nb = N_DEV
    ffn_cb = [ffn_conv_b[l].reshape(2, nb // 2, 1, -1) for l in range(depth)]
    ffn_g = [ffn_norm_g[l:l + 1] for l in range(depth)]
    a_bst = a_b_s[0].T
    w_up, w_down, saved = [None] * depth, [None] * depth, {}

    def ffn_forward(xl, hl, l, next_g):
        up, down, *early = gather_wait(in_flight[2 * l + 1], (xl,), f"gather_wait_ffn{l}")
        if early:
            unpack_small(packs[0], early[0])
            ffn_cw.extend(sm['ffn_conv_w'][k].reshape(ffn_conv_w.shape[1], 2, nb // 2, -1).transpose(1, 2, 0, 3)
                          for k in range(depth))
        w_up[l], w_down[l] = up, rows_full(down)
        return _ffn_forward(xl, hl, w_up[l], ffn_cw[l], ffn_cb[l], w_down[l], next_g(), l)

    ffn_cw = []
    wa_in, wa_out = gather_wait(in_flight[0], (xs,), "gather_wait_a")
    wa_out = rows_full(wa_out)
    h = rms_fwd(xs, a_norm_g, "a_norm")
    z = mm_in(h, wa_in, "a_in", bias=a_b_in)
    y = a_mid_fwd(z, a_v_norm_g, a_w_s[0], a_bst, "a_mid")
    x1, hf = mm_out(y, wa_out, xs, ffn_g[0], "a_out")
    saved['a'] = (xs, h, z, y)
    (x1, h), saved['f0'] = ffn_forward(x1, hf, 0, lambda: sm['b_norm_g'])

    wb_in, wb_out, late = gather_wait(in_flight[2], (x1,), "gather_wait_b")
    unpack_small(packs[1], late)
    b_wgrp, c_wa, c_wi = (sm[n][0].astype(BF16) for n in ('b_w_grp', 'c_w_a', 'c_w_i'))
    b_bgrp, c_ba, c_bi = (sm[n][0].reshape(1, -1) for n in ('b_b_grp', 'c_b_a', 'c_b_i'))
    wb_in, wb_out = rows_full(wb_in)[None], rows_full(wb_out)
    z = mm_in(h, wb_in, "b_in")
    y = b_mid_fwd(z, b_wgrp, b_bgrp, sm['b_scale'], "b_mid")
    x2, hf = mm_out(y, wb_out, x1, ffn_g[1], "b_out")
    saved['b'] = (x1, h, z, y)
    (x2, h), saved['f1'] = ffn_forward(x2, hf, 1, lambda: sm['c_norm_g'])

    wc_in, wc_out = gather_wait(in_flight[4], (x2,), "gather_wait_c")
    wc_out = rows_full(wc_out)
    z = mm_in(h, wc_in, "c_in", bias=sm['c_b_in'])
    c_cw = sm['c_conv_w'][0]
    a_seq, b_seq, xr = c_mid_fwd(z, c_cw, sm['c_conv_b'], c_wa, c_ba, c_wi, c_bi, sm['c_lambda'], "c_mid")
    hs, y = c_scan_fwd(a_seq, b_seq, z, "c_scan")
    x3, hf = mm_out(y, wc_out, x2, ffn_g[2], "c_out")
    saved['c'] = (x2, h, z, y, a_seq, xr, hs)
    (x3, h), saved['f2'] = ffn_forward(x3, hf, 2, lambda: sm['d_norm_g'])

    wd_in, wd_out = gather_wait(in_flight[6], (x3,), "gather_wait_d")
    wd_out = rows_full(wd_out)
    z = mm_in(h, wd_in, "d_in")
    d_cw = sm['d_conv_w'][0]
    y = d_mid_fwd(z, d_cw, "d_mid")
    x4, hf = mm_out(y, wd_out, x3, ffn_g[3], "d_out")
    saved['d'] = (x3, h, z, y)
    (x4, _), saved['f3'] = ffn_forward(x4, hf, 3, lambda: None)

    loss_part, dx, dxb, d_final_g = final_loss(x4, final_norm_g.reshape(1, -1), target, "final_loss")
    loss = lax.psum(loss_part[0, 0], ("x", "y", "c"))

    def rows_stacked(full):
        return full.reshape((N_DEV, full.shape[0] // N_DEV) + full.shape[1:])

    mx, my, mc = _place()
    own = jnp.stack([_dev_index(*_chip_of(mx, my, k), mc) for k in range(4)]).astype(jnp.int32)
    repl_shapes = [w_loc[n].shape for n in REPLICATED]
    repl_rows = _padded_rows(repl_shapes, 8 * N_DEV)
    shard_of = {n: (w_loc[n][0], m_loc[n][0], v_loc[n][0])
                for n in ('a_w_in', 'a_w_out', 'b_w_in', 'b_w_out', 'c_w_in', 'c_w_out', 'd_w_in', 'd_w_out')}
    shard_of['small'] = (small_packed, _pack([m_loc[n] for n in small_names], small_rows),
                         _pack([v_loc[n] for n in small_names], small_rows))
    updated = {}

    def finish(n, part, others):
        if n == 'repl':
            chunk = sum_parts(part, others, "rs_sum_repl")
            repl_g = all_gather([chunk], "gather_repl")[0].reshape(repl_rows, LANES)
            updated[n] = adamw(*(_pack([src[k] for k in REPLICATED], repl_rows) for src in (w_loc, m_loc, v_loc)),
                               "adamw_repl", g=repl_g)
        elif n.startswith('ffn_w'):
            base, l = n[:-1], int(n[-1])
            turn = (lambda t: jnp.swapaxes(t, 1, 2)) if base == 'ffn_w_up' else (lambda t: t)
            updated[base] = adamw_layer(turn(w_loc[base]), turn(m_loc[base]), turn(v_loc[base]), l, updated.get(base),
                                        f"adamw_{n}", part, others)
            return updated[base][1]
        else:
            updated[n] = adamw(*shard_of[n], f"adamw_{n}", p=part, l2=others)
        return updated[n][1]

    stages = [None, None]

    def advance(tag, new, after):
        behind = []
        first = None
        if new:
            first = ([n for n, _ in new], sibling_start([g for _, g in new], f"rs_sibling_start{tag}"))
            behind.append(first[1][3])
        second = None
        if stages[0] is not None:
            names, started = stages[0]
            grads, got = sibling_wait(started, after, f"rs_sibling_wait{tag}")
            parts = [add_pairs(g, l, own, f"rs_add_{n}") for n, g, l in zip(names, grads, got)]
            second = (names, chips_start(parts, f"rs_chips_start{tag}"))
            behind.append(second[1][3])
        if stages[1] is not None:
            names, started = stages[1]
            parts, others = chips_wait(started, after, f"rs_chips_wait{tag}")
            behind += [finish(n, p, o) for n, p, o in zip(names, parts, others)]
        stages[:] = [first, second]
        return tuple(behind)

    gf = [None] * depth
    dx, dxb, gf[3] = _ffn_backward(dx, dxb, saved['f3'], ffn_g[3], w_up[3], ffn_cw[3], ffn_cb[3], w_down[3], 3)
    xin, h, z, y = saved['d']
    dy = mm_dx_out(dxb, wd_out, "d_out_dx")
    g_d_w_out = mm_dw_out(y, dxb, "d_out_dw")
    dz, g_d_conv_w = d_mid_bwd(z, dy, d_cw, "d_mid_bwd")
    dx, dxb, g_d_norm_g = mm_dx_in(dz, wd_in, "d_in_dx", norm=(xin, sm['d_norm_g'], dx))
    g_d_w_in = mm_dw_in(h, dz, nb, "d_in_dw")
    behind = advance(0, [('ffn_w_up3', gf[3]['w_up']), ('ffn_w_down3', rows_stacked(gf[3]['w_down'])), ('d_w_in', g_d_w_in),
                         ('d_w_out', rows_stacked(g_d_w_out))], (dx,))

    dx, dxb, gf[2] = _ffn_backward(dx, dxb, saved['f2'], ffn_g[2], w_up[2], ffn_cw[2], ffn_cb[2], w_down[2], 2, behind)
    xin, h, z, y, a_seq, xr, hs = saved['c']
    dy = mm_dx_out(dxb, wc_out, "c_out_dx")
    g_c_w_out = mm_dw_out(y, dxb, "c_out_dw")
    lam_seq, da_seq, dgate, dgate_sum = c_scan_bwd(dy, z, hs, a_seq, "c_scan_bwd")
    dxr, g_c_w_a, g_c_w_i, g_c_b_a, g_c_b_i, g_c_lambda = c_mid_bwd(
        lam_seq, da_seq, xr, c_wa, c_ba, c_wi, c_bi, sm['c_lambda'], "c_mid_bwd")
    dxr_pre, g_c_conv_w, g_c_conv_b, dxr_pre_sum = conv_bwd(dxr, z, 1, c_cw, "c_conv_bwd")
    dz = jnp.concatenate([dgate, dxr_pre], axis=1)
    g_c_b_in = jnp.concatenate([dgate_sum, dxr_pre_sum], axis=1)
    dx, dxb, g_c_norm_g = mm_dx_in(dz, wc_in, "c_in_dx", norm=(xin, sm['c_norm_g'], dx))
    g_c_w_in = mm_dw_in(h, dz, nb, "c_in_dw")
    behind = advance(1, [('ffn_w_up2', gf[2]['w_up']), ('ffn_w_down2', rows_stacked(gf[2]['w_down'])), ('c_w_in', g_c_w_in),
                         ('c_w_out', rows_stacked(g_c_w_out))], (dx,))

    dx, dxb, gf[1] = _ffn_backward(dx, dxb, saved['f1'], ffn_g[1], w_up[1], ffn_cw[1], ffn_cb[1], w_down[1], 1, behind)
    xin, h, z, y = saved['b']
    dy = mm_dx_out(dxb, wb_out, "b_out_dx")
    g_b_w_out = mm_dw_out(y, dxb, "b_out_dw")
    dp, g_b_w_grp, g_b_b_grp, g_b_scale = b_mid_bwd(z, dy, b_wgrp, b_bgrp, sm['b_scale'], "b_mid_bwd")
    dz = b_pool_bwd(dp, "b_pool_bwd")
    dx, dxb, g_b_norm_g = mm_dx_in(dz, wb_in, "b_in_dx", norm=(xin, sm['b_norm_g'], dx))
    g_b_w_in = mm_dw_in(h, dz, 1, "b_in_dw")
    behind = advance(2, [('ffn_w_up1', gf[1]['w_up']), ('ffn_w_down1', rows_stacked(gf[1]['w_down'])),
                         ('b_w_in', rows_stacked(g_b_w_in[0])), ('b_w_out', rows_stacked(g_b_w_out))], (dx,))

    dx, dxb, gf[0] = _ffn_backward(dx, dxb, saved['f0'], ffn_g[0], w_up[0], ffn_cw[0], ffn_cb[0], w_down[0], 0, behind)
    full_small = {
        'b_norm_g': g_b_norm_g, 'b_w_grp': g_b_w_grp[None], 'b_b_grp': g_b_b_grp.reshape(b_b_grp.shape[:2] + (-1,)),
        'b_scale': g_b_scale, 'c_norm_g': g_c_norm_g, 'c_b_in': g_c_b_in, 'c_conv_w': g_c_conv_w[None],
        'c_conv_b': g_c_conv_b, 'c_w_a': g_c_w_a[None], 'c_b_a': g_c_b_a.reshape(c_b_a.shape[:2] + (-1,)),
        'c_w_i': g_c_w_i[None], 'c_b_i': g_c_b_i.reshape(c_b_i.shape[:2] + (-1,)), 'c_lambda': g_c_lambda,
        'd_norm_g': g_d_norm_g, 'd_conv_w': g_d_conv_w[None],
        'ffn_conv_w': jnp.stack([gf[l]['conv_w'].transpose(2, 0, 1, 3).reshape(ffn_conv_w.shape[1], -1) for l in range(depth)])}
    small_grads = _pack_stacked([_to_stacked(full_small[n], SMALL_SHARDED[n]) for n in small_names], small_rows)
    behind = advance(3, [('ffn_w_up0', gf[0]['w_up']), ('ffn_w_down0', rows_stacked(gf[0]['w_down'])), ('small', small_grads)],
                     (dx,))

    xin, h, z, y = saved['a']
    dy = mm_dx_out(dxb, wa_out, "a_out_dx", after=behind)
    g_a_w_out = mm_dw_out(y, dxb, "a_out_dw")
    behind = advance(4, [('a_w_out', rows_stacked(g_a_w_out))], (dy,))
    dz, g_a_b_in, g_a_v_norm_g, g_a_w_s, g_a_b_s = a_mid_bwd(z, dy, a_v_norm_g, a_w_s[0], a_bst, "a_mid_bwd", behind)
    g_a_w_in = mm_dw_in(h, dz, nb, "a_in_dw")
    behind = advance(5, [('a_w_in', g_a_w_in)], (dz,))
    dx, _, g_a_norm_g = mm_dx_in(dz, wa_in, "a_in_dx", behind, norm=(xin, a_norm_g, dx))
    grad_x = dx[None]

    tril = jnp.tril(jnp.ones((A_CHUNK, A_CHUNK), bool))
    repl_full = {
        'a_norm_g': g_a_norm_g, 'a_b_in': g_a_b_in, 'a_v_norm_g': g_a_v_norm_g,
        'a_w_s': jnp.where(tril, g_a_w_s, 0.0)[None], 'a_b_s': g_a_b_s[:, ::LANES].T[None],
        'ffn_norm_g': jnp.concatenate([gf[l]['norm_g'] for l in range(depth)], axis=0),
        'ffn_conv_b': jnp.stack([gf[l]['conv_b'].reshape(-1) for l in range(depth)]), 'final_norm_g': d_final_g.reshape(-1)}
    repl_grads = _pack([repl_full[n] for n in REPLICATED], repl_rows).reshape(N_DEV, repl_rows // N_DEV, LANES)
    behind = advance(6, [('repl', repl_grads)], (dx,))
    behind = advance(7, [], (dx, *behind))
    advance(8, [], (dx, *behind))

    outs = [{}, {}, {}, {}]
    for i, dst in enumerate(outs):
        for n in ('a_w_in', 'a_w_out', 'b_w_in', 'b_w_out', 'c_w_in', 'c_w_out', 'd_w_in', 'd_w_out'):
            dst[n] = updated[n][i][None]
        dst['ffn_w_up'] = jnp.swapaxes(updated['ffn_w_up'][i], 1, 2)
        dst['ffn_w_down'] = updated['ffn_w_down'][i]
        dst.update(zip(small_names, _unpack(updated['small'][i], small_shapes)))
        dst.update(zip(REPLICATED, _unpack(updated['repl'][i], repl_shapes)))
    out_g, out_d, out_m, out_v = outs

    return (loss, grad_x, *[out_g[n] for n in WEIGHTS], *[out_d[n] for n in WEIGHTS], *[out_m[n] for n in WEIGHTS],
            *[out_v[n] for n in WEIGHTS])
```

```python
import functools

import jax
import jax.numpy as jnp
from jax import lax
from jax.experimental import pallas as pl
from jax.experimental.pallas import tpu as pltpu

F32, BF16 = jnp.float32, jnp.bfloat16
MESH_ID = pl.DeviceIdType.MESH
N_DEV = 8
V7X_VMEM_LIMIT_BYTES = 56 << 20
LANES = 128
HALO = 8
POOL_HALO = 16

EPS = 1e-6
A_CHUNK, A_GROUPS = 128, 4
B_WINDOWS = (2, 4, 8, 16)
C_GATE_C = 8.0
ADAM_LR, ADAM_B1, ADAM_B2, ADAM_EPS, ADAM_WD, ADAM_STEP = 0.001, 0.9, 0.999, 1e-08, 0.01, 10

WEIGHTS = ['a_norm_g', 'a_w_in', 'a_b_in', 'a_v_norm_g', 'a_w_s', 'a_b_s', 'a_w_out', 'b_norm_g', 'b_w_in', 'b_w_grp',
           'b_b_grp', 'b_scale', 'b_w_out', 'c_norm_g', 'c_w_in', 'c_b_in', 'c_conv_w', 'c_conv_b', 'c_w_a', 'c_b_a',
           'c_w_i', 'c_b_i', 'c_lambda', 'c_w_out', 'd_norm_g', 'd_w_in', 'd_conv_w', 'd_w_out', 'ffn_norm_g',
           'ffn_w_up', 'ffn_conv_w', 'ffn_conv_b', 'ffn_w_down', 'final_norm_g']
SMALL_SHARDED = {'b_norm_g': 1, 'b_w_grp': 2, 'b_b_grp': 2, 'b_scale': 1, 'c_norm_g': 1, 'c_b_in': 1, 'c_conv_w': 2,
                 'c_conv_b': 1, 'c_w_a': 2, 'c_b_a': 2, 'c_w_i': 2, 'c_b_i': 2, 'c_lambda': 1, 'd_norm_g': 1,
                 'd_conv_w': 2, 'ffn_conv_w': 2}
REPLICATED = ['a_norm_g', 'a_b_in', 'a_v_norm_g', 'a_w_s', 'a_b_s', 'ffn_norm_g', 'ffn_conv_b', 'final_norm_g']


_GELU_C0, _GELU_C1 = 0.7978845608028654, 0.044715


def _gelu(x):
    return 0.5 * x * (1.0 + jnp.tanh(_GELU_C0 * (x + _GELU_C1 * (x * x * x))))


def _gelu_grad(x):
    t = jnp.tanh(_GELU_C0 * (x + _GELU_C1 * (x * x * x)))
    return 0.5 * (1.0 + t) + 0.5 * x * (1.0 - t * t) * (_GELU_C0 * (1.0 + 3.0 * _GELU_C1 * (x * x)))


def _sigmoid(x):
    return jax.nn.sigmoid(x)


def _log1p(x):
    u = 1.0 + x
    return jnp.where(u == 1.0, x, jnp.log(u) * (x / (u - 1.0)))


def _softplus(x):
    return jnp.maximum(x, 0.0) + _log1p(jnp.exp(-jnp.abs(x)))


def _expm1(x):
    poly = x * (1.0 + x * (1 / 2) * (1.0 + x * (1 / 3) * (1.0 + x * (1 / 4) * (1.0 + x * (1 / 5) * (
        1.0 + x * (1 / 6) * (1.0 + x * (1 / 7) * (1.0 + x * (1 / 8))))))))
    return jnp.where(jnp.abs(x) < 0.35, poly, jnp.exp(x) - 1.0)


def _down(xe, s):
    return xe if s == 0 else pltpu.roll(xe, s, 0)


def _up(xe, s):
    return xe if s == 0 else pltpu.roll(xe, xe.shape[0] - s, 0)


def _conv_ext(xe, w, taps):
    y = xe * w[taps - 1:taps]
    for s in range(1, taps):
        y = y + _down(xe, s) * w[taps - 1 - s:taps - s]
    return y


def _acc(ref, val, first):
    @pl.when(first)
    def _():
        ref[...] = val

    @pl.when(jnp.logical_not(first))
    def _():
        ref[...] += val


def _colsum(v):
    return jnp.sum(v, axis=0, keepdims=True)


def _dot(a, b, dims=((1,), (0,))):
    return lax.dot_general(a.astype(BF16), b.astype(BF16), (dims, ((), ())), preferred_element_type=F32)


_NN, _NT, _TN = ((1,), (0,)), ((1,), (1,)), ((0,), (0,))


def _call(body, name, grid, in_specs, out_specs, out_shape, scratch=(), after=()):
    n_in, n_after = len(in_specs), len(after)

    def ordered_body(*refs):
        return body(*refs[:n_in], *refs[n_in + n_after:])

    call = pl.pallas_call(
        ordered_body if n_after else body, name=name, grid=grid,
        in_specs=list(in_specs) + [pl.BlockSpec(memory_space=pl.ANY)] * n_after, out_specs=out_specs,
        out_shape=out_shape, scratch_shapes=list(scratch),
        compiler_params=pltpu.CompilerParams(dimension_semantics=("arbitrary",) * len(grid),
                                             vmem_limit_bytes=V7X_VMEM_LIMIT_BYTES))
    return lambda *args: call(*args, *after)


def _rows(m, t):
    t = min(m, t)
    assert m % t == 0, (m, t)
    return t


def _sds(shape, dtype=F32):
    return jax.ShapeDtypeStruct(tuple(shape), dtype)


def _prev_halo(tm, halo=HALO):
    return lambda i: jnp.maximum(i * (tm // halo) - 1, 0)


def _next_halo(tm, m, halo=HALO):
    return lambda i: jnp.minimum((i + 1) * (tm // halo), m // halo - 1)


def _matmul(name, ins, in_specs, out_shape, o_spec, grid, compute, after=()):
    def body(*refs):
        refs[-1][...] = compute(*refs[:-1]).astype(refs[-1].dtype)

    return _call(body, name, grid, in_specs, o_spec, out_shape, (), after)(*ins)


def mm_in(h, w_st, name, bias=None, stacked_out=False, out_dtype=F32, rows=1024):
    m, k = h.shape
    nb, _, n = w_st.shape
    tm = _rows(m, rows)
    in_specs = [pl.BlockSpec((tm, k), lambda i, j: (i, 0)), pl.BlockSpec((None, k, n), lambda i, j: (j, 0, 0))]
    if stacked_out:
        out, o_spec = _sds((nb, m, n), out_dtype), pl.BlockSpec((None, tm, n), lambda i, j: (j, i, 0))
    else:
        out, o_spec = _sds((m, nb * n), out_dtype), pl.BlockSpec((tm, n), lambda i, j: (i, j))
    if bias is None:
        return _matmul(name, (h, w_st), in_specs, out, o_spec, (m // tm, nb), lambda a, b: _dot(a[...], b[...]))
    in_specs.append(pl.BlockSpec((1, n), lambda i, j: (0, j)))
    return _matmul(name, (h, w_st, bias), in_specs, out, o_spec, (m // tm, nb),
                   lambda a, b, c: _dot(a[...], b[...]) + c[...])


def _split_rows(kf):
    g = max(1, kf // 1024)
    return g, kf // g


def _resident(shape):
    return pl.BlockSpec(shape, lambda *_: (0,) * len(shape), pipeline_mode=pl.Buffered(1))


def _norm_rows(xv, g):
    return (xv * lax.rsqrt(jnp.mean(xv * xv, axis=-1, keepdims=True) + EPS) * g).astype(BF16)


def mm_out(y, w, res, next_g, name):
    kf, n = w.shape
    m = y.shape[0]
    tm = _rows(m, 512)
    row = pl.BlockSpec((tm, n), lambda i: (i, 0))

    def body(y_ref, w_ref, res_ref, g_ref, x_ref, h_ref):
        xv = res_ref[...] + _dot(y_ref[...], w_ref[...])
        x_ref[...] = xv
        h_ref[...] = _norm_rows(xv, g_ref[...])

    in_specs = [pl.BlockSpec((tm, kf), lambda i: (i, 0)), _resident((kf, n)), row, pl.BlockSpec((1, n), lambda i: (0, 0))]
    return _call(body, name, (m // tm,), in_specs, [row, row], [_sds((m, n)), _sds((m, n), BF16)])(y, w, res, next_g)


def mm_dx_in(dz, w_st, name, after=(), norm=None):
    nb, k, n = w_st.shape
    m = dz.shape[-2]
    tm = _rows(m, 512)
    w_spec = _resident((nb, k, n))
    if dz.ndim == 3:
        in_specs = [pl.BlockSpec((None, tm, n), lambda i, r=r: (r, i, 0)) for r in range(nb)] + [w_spec]

        def compute(*refs):
            acc = _dot(refs[0][...], refs[nb][0], _NT)
            for r in range(1, nb):
                acc = acc + _dot(refs[r][...], refs[nb][r], _NT)
            return acc

        ins = (*[dz] * nb, w_st)
    else:
        in_specs = [pl.BlockSpec((tm, nb * n), lambda i: (i, 0)), w_spec]

        def compute(dz_ref, w_ref):
            acc = _dot(dz_ref[:, :n], w_ref[0], _NT)
            for r in range(1, nb):
                acc = acc + _dot(dz_ref[:, r * n:(r + 1) * n], w_ref[r], _NT)
            return acc

        ins = (dz, w_st)
    row = pl.BlockSpec((tm, k), lambda i: (i, 0))
    if norm is None:
        return _matmul(name, ins, in_specs, _sds((m, k)), row, (m // tm,), compute, after)
    n_in = len(in_specs)
    vec = pl.BlockSpec((1, k), lambda i: (0, 0))

    def body(*refs):
        x_ref, g_ref, dr_ref, dx_ref, dxb_ref, dg_ref = refs[n_in:]
        dx, dg = _rms_bwd_math(x_ref[...], g_ref[...], compute(*refs[:n_in]))
        dx = dr_ref[...] + dx
        dx_ref[...] = dx
        dxb_ref[...] = dx.astype(BF16)
        _acc(dg_ref, dg, pl.program_id(0) == 0)

    return _call(body, name, (m // tm,), in_specs + [row, vec, row], [row, row, vec],
                 [_sds((m, k)), _sds((m, k), BF16), _sds((1, k))], (), after)(*ins, *norm)


def mm_dx_out(dout, w, name, groups=None, after=()):
    kf, n = w.shape
    m = dout.shape[0]
    tm = _rows(m, 1024)
    g, k = (groups, kf // groups) if groups else _split_rows(kf)
    in_specs = [pl.BlockSpec((tm, n), lambda i, j: (i, 0)), pl.BlockSpec((None, k, n), lambda i, j: (j, 0, 0))]
    if groups:
        out, o_spec = _sds((g, m, k)), pl.BlockSpec((None, tm, k), lambda i, j: (j, i, 0))
    else:
        out, o_spec = _sds((m, kf)), pl.BlockSpec((tm, k), lambda i, j: (i, j))
    return _matmul(name, (dout, w.reshape(g, k, n)), in_specs, out, o_spec, (m // tm, g),
                   lambda a, b: _dot(a[...], b[...], _NT), after)


def mm_dw_in(h, dz, nb, name, transposed=False):
    m, k = h.shape
    if dz.ndim == 3:
        n = dz.shape[2]
        dz_spec = pl.BlockSpec((None, m, n), lambda j: (j, 0, 0))
    else:
        n = dz.shape[1] // nb
        dz_spec = pl.BlockSpec((m, n), lambda j: (0, j))
    in_specs = [_resident((m, k)), dz_spec]
    if transposed:
        return _matmul(name, (h, dz), in_specs, _sds((nb, n, k), BF16), pl.BlockSpec((None, n, k), lambda j: (j, 0, 0)), (nb,),
                       lambda a, b: _dot(b[...], a[...], _TN))
    return _matmul(name, (h, dz), in_specs, _sds((nb, k, n), BF16), pl.BlockSpec((None, k, n), lambda j: (j, 0, 0)), (nb,),
                   lambda a, b: _dot(a[...], b[...], _TN))


def mm_dw_out(y, dout, name):
    m, n = dout.shape
    if y.ndim == 3:
        g, _, k = y.shape
        y_spec = pl.BlockSpec((None, m, k), lambda j: (j, 0, 0))
    else:
        g, k = _split_rows(y.shape[1])
        y_spec = pl.BlockSpec((m, k), lambda j: (0, j))
    in_specs = [y_spec, _resident((m, n))]
    out = _matmul(name, (y, dout), in_specs, _sds((g, k, n), BF16), pl.BlockSpec((None, k, n), lambda j: (j, 0, 0)), (g,),
                  lambda a, b: _dot(a[...], b[...], _TN))
    return out.reshape(g * k, n)


def rms_fwd(x, g, name):
    m, d = x.shape
    tm = _rows(m, 512)

    def body(x_ref, g_ref, o_ref):
        xv = x_ref[...]
        rstd = lax.rsqrt(jnp.mean(xv * xv, axis=-1, keepdims=True) + EPS)
        o_ref[...] = (xv * rstd * g_ref[...]).astype(BF16)

    row = pl.BlockSpec((tm, d), lambda i: (i, 0))
    vec = pl.BlockSpec((1, d), lambda i: (0, 0))
    return _call(body, name, (m // tm,), [row, vec], row, _sds((m, d), BF16))(x, g)


def _rms_bwd_math(xv, g, dh):
    rstd = lax.rsqrt(jnp.mean(xv * xv, axis=-1, keepdims=True) + EPS)
    xhat = xv * rstd
    dxhat = dh * g
    dx = rstd * (dxhat - xhat * jnp.mean(dxhat * xhat, axis=-1, keepdims=True))
    return dx, _colsum(dh * xhat)


def final_loss(x, g, target, name):
    m, d = x.shape
    tm = _rows(m, 512)

    def body(x_ref, g_ref, t_ref, l_ref, dx_ref, dxb_ref, dg_ref):
        xv, gv = x_ref[...], g_ref[...]
        rstd = lax.rsqrt(jnp.mean(xv * xv, axis=-1, keepdims=True) + EPS)
        err = xv * rstd * gv - t_ref[...]
        part = 0.5 * jnp.sum(jnp.mean(err * err, axis=-1, keepdims=True), axis=0, keepdims=True)
        dx, dg = _rms_bwd_math(xv, gv, err * (1.0 / d))
        dx_ref[...] = dx
        dxb_ref[...] = dx.astype(BF16)
        first = pl.program_id(0) == 0
        _acc(l_ref, jnp.broadcast_to(part, l_ref.shape), first)
        _acc(dg_ref, dg, first)

    row = pl.BlockSpec((tm, d), lambda i: (i, 0))
    vec = pl.BlockSpec((1, d), lambda i: (0, 0))
    lsp = pl.BlockSpec((1, LANES), lambda i: (0, 0))
    return _call(body, name, (m // tm,), [row, vec, row], [lsp, row, row, vec],
                 [_sds((1, LANES)), _sds((m, d)), _sds((m, d), BF16), _sds((1, d))])(x, g, target)


def _a_common(z_ref, vg_ref, ws_ref, bst_ref, tm, width):
    gw = width // A_GROUPS
    zp = z_ref[...]
    z = _gelu(zp)
    u, v = z[:, :width], z[:, width:]
    rstd = lax.rsqrt(jnp.mean(v * v, axis=-1, keepdims=True) + EPS)
    vhat = v * rstd
    vn = vhat * vg_ref[...]
    t_i = lax.broadcasted_iota(jnp.int32, (A_CHUNK, A_CHUNK), 0)
    s_i = lax.broadcasted_iota(jnp.int32, (A_CHUNK, A_CHUNK), 1)
    wsm = [jnp.where(s_i <= t_i, ws_ref[g], 0.0).astype(BF16) for g in range(A_GROUPS)]
    bst = bst_ref[...]
    return zp, u, rstd, vhat, vn.astype(BF16), wsm, bst, gw


def a_mid_fwd(z, vg, ws, bst, name):
    m, w2 = z.shape
    width = w2 // 2
    tm = _rows(m, 256)

    def body(z_ref, vg_ref, ws_ref, bst_ref, y_ref):
        _, u, _, _, vnb, wsm, bst, gw = _a_common(z_ref, vg_ref, ws_ref, bst_ref, tm, width)
        for c in range(tm // A_CHUNK):
            r0 = c * A_CHUNK
            for g in range(A_GROUPS):
                c0 = g * gw
                vs = _dot(wsm[g], vnb[r0:r0 + A_CHUNK, c0:c0 + gw]) + bst[:, g:g + 1]
                y_ref[r0:r0 + A_CHUNK, c0:c0 + gw] = (u[r0:r0 + A_CHUNK, c0:c0 + gw] * vs).astype(BF16)

    in_specs = [pl.BlockSpec((tm, w2), lambda i: (i, 0)), pl.BlockSpec((1, width), lambda i: (0, 0)),
                pl.BlockSpec((A_GROUPS, A_CHUNK, A_CHUNK), lambda i: (0, 0, 0)),
                pl.BlockSpec((A_CHUNK, A_GROUPS), lambda i: (0, 0))]
    return _call(body, name, (m // tm,), in_specs, pl.BlockSpec((tm, width), lambda i: (i, 0)),
                 _sds((m, width), BF16))(z, vg, ws, bst)


def a_mid_bwd(z, dy, vg, ws, bst, name, after=()):
    m, w2 = z.shape
    width = w2 // 2
    tm = _rows(m, 256)

    def body(z_ref, dy_ref, vg_ref, ws_ref, bst_ref, dz_ref, dbin_ref, dvg_ref, dws_ref, dbs_ref, dvn_scr, du_scr):
        first = pl.program_id(0) == 0
        zp, u, rstd, vhat, vnb, wsm, bst, gw = _a_common(z_ref, vg_ref, ws_ref, bst_ref, tm, width)
        dy = dy_ref[...]
        dws = [jnp.zeros((A_CHUNK, A_CHUNK), F32) for _ in range(A_GROUPS)]
        dbs = [jnp.zeros((A_CHUNK, 1), F32) for _ in range(A_GROUPS)]
        for c in range(tm // A_CHUNK):
            r0 = c * A_CHUNK
            for g in range(A_GROUPS):
                c0 = g * gw
                vn_cg = vnb[r0:r0 + A_CHUNK, c0:c0 + gw]
                vs = _dot(wsm[g], vn_cg) + bst[:, g:g + 1]
                dy_cg = dy[r0:r0 + A_CHUNK, c0:c0 + gw]
                dvs = dy_cg * u[r0:r0 + A_CHUNK, c0:c0 + gw]
                du_scr[r0:r0 + A_CHUNK, c0:c0 + gw] = dy_cg * vs
                dws[g] = dws[g] + _dot(dvs, vn_cg, _NT)
                dbs[g] = dbs[g] + jnp.sum(dvs, axis=1, keepdims=True)
                dvn_scr[r0:r0 + A_CHUNK, c0:c0 + gw] = _dot(wsm[g], dvs, _TN)
        for g in range(A_GROUPS):
            _acc(dws_ref.at[g], dws[g], first)
            _acc(dbs_ref.at[:, g * LANES:(g + 1) * LANES], jnp.broadcast_to(dbs[g], (A_CHUNK, LANES)), first)
        dvn = dvn_scr[...]
        _acc(dvg_ref, _colsum(dvn * vhat), first)
        dvhat = dvn * vg_ref[...]
        dv = rstd * (dvhat - vhat * jnp.mean(dvhat * vhat, axis=-1, keepdims=True))
        gg = _gelu_grad(zp)
        dzu = du_scr[...] * gg[:, :width]
        dzv = dv * gg[:, width:]
        dz_ref[:, :width] = dzu.astype(BF16)
        dz_ref[:, width:] = dzv.astype(BF16)
        _acc(dbin_ref.at[:, :width], _colsum(dzu), first)
        _acc(dbin_ref.at[:, width:], _colsum(dzv), first)

    const2 = lambda i: (0, 0)
    in_specs = [pl.BlockSpec((tm, w2), lambda i: (i, 0)), pl.BlockSpec((tm, width), lambda i: (i, 0)),
                pl.BlockSpec((1, width), const2), pl.BlockSpec((A_GROUPS, A_CHUNK, A_CHUNK), lambda i: (0, 0, 0)),
                pl.BlockSpec((A_CHUNK, A_GROUPS), const2)]
    out_specs = [pl.BlockSpec((tm, w2), lambda i: (i, 0)), pl.BlockSpec((1, w2), const2), pl.BlockSpec((1, width), const2),
                 pl.BlockSpec((A_GROUPS, A_CHUNK, A_CHUNK), lambda i: (0, 0, 0)),
                 pl.BlockSpec((A_CHUNK, A_GROUPS * LANES), const2)]
    out_shape = [_sds((m, w2), BF16), _sds((1, w2)), _sds((1, width)), _sds((A_GROUPS, A_CHUNK, A_CHUNK)),
                 _sds((A_CHUNK, A_GROUPS * LANES))]
    scratch = [pltpu.VMEM((tm, width), F32), pltpu.VMEM((tm, width), F32)]
    return _call(body, name, (m // tm,), in_specs, out_specs, out_shape, scratch, after)(z, dy, vg, ws, bst)


def _pool_minus_id(ze, i, tm, gw):
    pos = i * tm + lax.broadcasted_iota(jnp.int32, (tm, 1), 0)
    out = []
    for gi, win in enumerate(B_WINDOWS):
        s = ze[:, gi * gw:(gi + 1) * gw]
        step = 1
        while step < win:
            s = s + _down(s, step)
            step *= 2
        inv = 1.0 / jnp.minimum(pos + 1, win).astype(F32)
        out.append(s[POOL_HALO:] * inv - ze[POOL_HALO:, gi * gw:(gi + 1) * gw])
    return out


def _b_specs(tm, width):
    return [pl.BlockSpec((POOL_HALO, width), lambda i: (_prev_halo(tm, POOL_HALO)(i), 0)),
            pl.BlockSpec((tm, width), lambda i: (i, 0))]


def b_mid_fwd(z, wgrp, bgrp, scale, name):
    m, width = z.shape
    ng = len(B_WINDOWS)
    gw = width // ng
    tm = _rows(m, 512)

    def body(zp_ref, zm_ref, w_ref, b_ref, s_ref, y_ref):
        i = pl.program_id(0)
        ze = jnp.concatenate([zp_ref[...] * (i > 0).astype(F32), zm_ref[...]], axis=0)
        p = _pool_minus_id(ze, i, tm, gw)
        for g in range(ng):
            cs = slice(g * gw, (g + 1) * gw)
            y = (_dot(p[g], w_ref[g]) + b_ref[:, cs]) * s_ref[:, cs]
            y_ref[:, cs] = y.astype(BF16)

    vec = pl.BlockSpec((1, width), lambda i: (0, 0))
    in_specs = _b_specs(tm, width) + [pl.BlockSpec((ng, gw, gw), lambda i: (0, 0, 0)), vec, vec]
    return _call(body, name, (m // tm,), in_specs, pl.BlockSpec((tm, width), lambda i: (i, 0)),
                 _sds((m, width), BF16))(z, z, wgrp, bgrp, scale)


def b_mid_bwd(z, dy, wgrp, bgrp, scale, name):
    m, width = z.shape
    ng = len(B_WINDOWS)
    gw = width // ng
    tm = _rows(m, 512)

    def body(zp_ref, zm_ref, dy_ref, w_ref, b_ref, s_ref, dp_ref, dw_ref, db_ref, ds_ref):
        i = pl.program_id(0)
        first = i == 0
        ze = jnp.concatenate([zp_ref[...] * (i > 0).astype(F32), zm_ref[...]], axis=0)
        p = _pool_minus_id(ze, i, tm, gw)
        for g in range(ng):
            cs = slice(g * gw, (g + 1) * gw)
            dyg = dy_ref[:, cs]
            ypre = _dot(p[g], w_ref[g]) + b_ref[:, cs]
            dyp = dyg * s_ref[:, cs]
            _acc(ds_ref.at[:, cs], _colsum(dyg * ypre), first)
            _acc(db_ref.at[:, cs], _colsum(dyp), first)
            _acc(dw_ref.at[g], _dot(p[g], dyp, _TN), first)
            dp_ref[:, cs] = _dot(dyp, w_ref[g], _NT)

    vec = pl.BlockSpec((1, width), lambda i: (0, 0))
    row = pl.BlockSpec((tm, width), lambda i: (i, 0))
    wsp = pl.BlockSpec((ng, gw, gw), lambda i: (0, 0, 0))
    return _call(body, name, (m // tm,), _b_specs(tm, width) + [row, wsp, vec, vec], [row, wsp, vec, vec],
                 [_sds((m, width)), _sds((ng, gw, gw)), _sds((1, width)), _sds((1, width))])(z, z, dy, wgrp, bgrp, scale)


def b_pool_bwd(dp, name):
    m, width = dp.shape
    gw = width // len(B_WINDOWS)
    tm = _rows(m, 512)
    n_i = m // tm

    def body(dm_ref, dn_ref, dz_ref):
        i = pl.program_id(0)
        de = jnp.concatenate([dm_ref[...], dn_ref[...] * (i < n_i - 1).astype(F32)], axis=0)
        pos = i * tm + lax.broadcasted_iota(jnp.int32, (tm + POOL_HALO, 1), 0)
        for gi, win in enumerate(B_WINDOWS):
            cs = slice(gi * gw, (gi + 1) * gw)
            d = de[:, cs]
            s = d * (1.0 / jnp.minimum(pos + 1, win).astype(F32))
            step = 1
            while step < win:
                s = s + _up(s, step)
                step *= 2
            dz_ref[:, cs] = (s[:tm] - d[:tm]).astype(BF16)

    in_specs = [pl.BlockSpec((tm, width), lambda i: (i, 0)),
                pl.BlockSpec((POOL_HALO, width), lambda i: (_next_halo(tm, m, POOL_HALO)(i), 0))]
    return _call(body, name, (n_i,), in_specs, pl.BlockSpec((tm, width), lambda i: (i, 0)), _sds((m, width), BF16))(dp, dp)


def _c_gates(xr, wa_ref, ba_ref, wi_ref, bi_ref, lam_ref, heads, hw):
    xb = xr.astype(BF16)
    ra = jnp.concatenate([_dot(xb[:, h * hw:(h + 1) * hw], wa_ref[h]) for h in range(heads)], axis=1) + ba_ref[...]
    ia = jnp.concatenate([_dot(xb[:, h * hw:(h + 1) * hw], wi_ref[h]) for h in range(heads)], axis=1) + bi_ref[...]
    r, ig = _sigmoid(ra), _sigmoid(ia)
    sp = _softplus(-lam_ref[...])
    log_a = (-C_GATE_C * r) * sp
    a = jnp.exp(log_a)
    mult = jnp.sqrt(-_expm1(2.0 * log_a))
    return xb, r, ig, sp, a, mult


def c_mid_fwd(z, cw, cb, wa, ba, wi, bi, lam, name):
    m, w2 = z.shape
    width = w2 // 2
    heads, hw = wa.shape[0], wa.shape[1]
    taps = cw.shape[0]
    tm = _rows(m, 512)

    def body(zp_ref, zm_ref, cw_ref, cb_ref, wa_ref, ba_ref, wi_ref, bi_ref, lam_ref, a_ref, b_ref, xr_ref):
        i = pl.program_id(0)
        xe = jnp.concatenate([zp_ref[...] * (i > 0).astype(F32), zm_ref[...]], axis=0)
        xr = _conv_ext(xe, cw_ref[...], taps)[HALO:] + cb_ref[...]
        _, _, ig, _, a, mult = _c_gates(xr, wa_ref, ba_ref, wi_ref, bi_ref, lam_ref, heads, hw)
        a_ref[...] = a
        b_ref[...] = mult * (ig * xr)
        xr_ref[...] = xr

    vec = pl.BlockSpec((1, width), lambda i: (0, 0))
    row = pl.BlockSpec((tm, width), lambda i: (i, 0))
    wsp = pl.BlockSpec((heads, hw, hw), lambda i: (0, 0, 0))
    in_specs = [pl.BlockSpec((HALO, width), lambda i: (_prev_halo(tm)(i), 1)), pl.BlockSpec((tm, width), lambda i: (i, 1)),
                pl.BlockSpec((taps, width), lambda i: (0, 0)), vec, wsp, vec, wsp, vec, vec]
    return _call(body, name, (m // tm,), in_specs, [row, row, row], [_sds((m, width))] * 3)(
        z, z, cw, cb, wa, ba, wi, bi, lam)


_SCAN_ROWS = 512


def c_scan_fwd(a, b, z, name):
    m, width = a.shape
    tm = _rows(m, _SCAN_ROWS)

    def body(a_ref, b_ref, g_ref, hs_ref, y_ref, h_carry):
        @pl.when(pl.program_id(0) == 0)
        def _():
            h_carry[...] = jnp.zeros_like(h_carry)

        def step(t, h):
            h = a_ref[pl.ds(t, 1), :] * h + b_ref[pl.ds(t, 1), :]
            hs_ref[pl.ds(t, 1), :] = h
            return h

        h_carry[...] = lax.fori_loop(0, tm, step, h_carry[...], unroll=8)
        y_ref[...] = (hs_ref[...] * _gelu(g_ref[...])).astype(BF16)

    row = pl.BlockSpec((tm, width), lambda i: (i, 0))
    return _call(body, name, (m // tm,), [row, row, row], [row, row], [_sds((m, width)), _sds((m, width), BF16)],
                 [pltpu.VMEM((1, width), F32)])(a, b, z)


def c_scan_bwd(dy, z, hs, a, name):
    m, width = a.shape
    tm = _rows(m, _SCAN_ROWS)
    n_i = m // tm

    def body(dy_ref, g_ref, hs_ref, hp_ref, a_ref, lam_ref, da_ref, dg_ref, dgs_ref, lam_carry, a_carry):
        i = pl.program_id(0)
        first = i == 0

        @pl.when(first)
        def _():
            lam_carry[...] = jnp.zeros_like(lam_carry)
            a_carry[...] = jnp.zeros_like(a_carry)

        gp, dyv, hsv = g_ref[...], dy_ref[...], hs_ref[...]
        dgate = dyv * hsv * _gelu_grad(gp)
        dg_ref[...] = dgate.astype(BF16)
        _acc(dgs_ref, _colsum(dgate), first)
        lam_ref[...] = dyv * _gelu(gp)

        def step(k, carry):
            lam_next, a_next = carry
            t = tm - 1 - k
            lam_t = lam_ref[pl.ds(t, 1), :] + a_next * lam_next
            lam_ref[pl.ds(t, 1), :] = lam_t
            return lam_t, a_ref[pl.ds(t, 1), :]

        lam_c, a_c = lax.fori_loop(0, tm, step, (lam_carry[...], a_carry[...]), unroll=8)
        lam_carry[...] = lam_c
        a_carry[...] = a_c
        h_before = hp_ref[HALO - 1:HALO, :] * (i < n_i - 1).astype(F32)
        t_i = lax.broadcasted_iota(jnp.int32, (tm, 1), 0)
        da_ref[...] = lam_ref[...] * jnp.where(t_i == 0, h_before, _down(hsv, 1))

    row = pl.BlockSpec((tm, width), lambda i: (n_i - 1 - i, 0))
    halo = pl.BlockSpec((HALO, width), lambda i: (_prev_halo(tm)(n_i - 1 - i), 0))
    vec = pl.BlockSpec((1, width), lambda i: (0, 0))
    return _call(body, name, (n_i,), [row, row, row, halo, row], [row, row, row, vec],
                 [_sds((m, width)), _sds((m, width)), _sds((m, width), BF16), _sds((1, width))],
                 [pltpu.VMEM((1, width), F32), pltpu.VMEM((1, width), F32)])(dy, z, hs, hs, a)


def c_mid_bwd(lam_seq, da, xr, wa, ba, wi, bi, lam, name):
    m, width = xr.shape
    heads, hw = wa.shape[0], wa.shape[1]
    tm = _rows(m, 512)

    def body(l_ref, da_ref, xr_ref, wa_ref, ba_ref, wi_ref, bi_ref, lam_ref,
             dxr_ref, dwa_ref, dwi_ref, dba_ref, dbi_ref, dlam_ref):
        first = pl.program_id(0) == 0
        xr_v, lmb = xr_ref[...], l_ref[...]
        xb, r, ig, sp, a, mult = _c_gates(xr_v, wa_ref, ba_ref, wi_ref, bi_ref, lam_ref, heads, hw)
        dmult = lmb * (ig * xr_v)
        dig = lmb * mult * xr_v
        dxr = lmb * mult * ig
        dla = da_ref[...] * a - dmult * (a * a) / mult
        dr = dla * (-C_GATE_C * sp)
        dsp = _colsum(dla * (-C_GATE_C * r))
        _acc(dlam_ref, dsp * (-_sigmoid(-lam_ref[...])), first)
        dra = dr * r * (1.0 - r)
        dia = dig * ig * (1.0 - ig)
        _acc(dba_ref, _colsum(dra), first)
        _acc(dbi_ref, _colsum(dia), first)
        for h in range(heads):
            cs = slice(h * hw, (h + 1) * hw)
            _acc(dwa_ref.at[h], _dot(xb[:, cs], dra[:, cs], _TN), first)
            _acc(dwi_ref.at[h], _dot(xb[:, cs], dia[:, cs], _TN), first)
            dxr_ref[:, cs] = dxr[:, cs] + _dot(dra[:, cs], wa_ref[h], _NT) + _dot(dia[:, cs], wi_ref[h], _NT)

    vec = pl.BlockSpec((1, width), lambda i: (0, 0))
    row = pl.BlockSpec((tm, width), lambda i: (i, 0))
    wsp = pl.BlockSpec((heads, hw, hw), lambda i: (0, 0, 0))
    return _call(body, name, (m // tm,), [row, row, row, wsp, vec, wsp, vec, vec], [row, wsp, wsp, vec, vec, vec],
                 [_sds((m, width)), _sds((heads, hw, hw)), _sds((heads, hw, hw)), _sds((1, width)), _sds((1, width)),
                  _sds((1, width))])(lam_seq, da, xr, wa, ba, wi, bi, lam)


def conv_bwd(dy, x_src, col_block, cw, name):
    m, width = dy.shape
    taps = cw.shape[0]
    tm = _rows(m, 512)
    n_i = m // tm

    def body(dm_ref, dn_ref, xp_ref, xm_ref, cw_ref, dx_ref, dw_ref, db_ref, dxs_ref):
        i = pl.program_id(0)
        first = i == 0
        de = jnp.concatenate([jnp.zeros((HALO, width), F32), dm_ref[...], dn_ref[...] * (i < n_i - 1).astype(F32)], axis=0)
        xe = jnp.concatenate([xp_ref[...] * (i > 0).astype(F32), xm_ref[...], jnp.zeros((HALO, width), F32)], axis=0)
        w = cw_ref[...]
        dx = de * w[taps - 1:taps]
        for s in range(1, taps):
            dx = dx + _up(de, s) * w[taps - 1 - s:taps - s]
        dx_ref[...] = dx[HALO:HALO + tm].astype(BF16)
        _acc(dxs_ref, _colsum(dx[HALO:HALO + tm]), first)
        dm = dm_ref[...]
        for s in range(taps):
            _acc(dw_ref.at[taps - 1 - s:taps - s, :], _colsum(dm * _down(xe, s)[HALO:HALO + tm]), first)
        _acc(db_ref, _colsum(dm), first)

    row = pl.BlockSpec((tm, width), lambda i: (i, 0))
    vec = pl.BlockSpec((1, width), lambda i: (0, 0))
    tsp = pl.BlockSpec((taps, width), lambda i: (0, 0))
    in_specs = [row, pl.BlockSpec((HALO, width), lambda i: (_next_halo(tm, m)(i), 0)),
                pl.BlockSpec((HALO, width), lambda i: (_prev_halo(tm)(i), col_block)),
                pl.BlockSpec((tm, width), lambda i: (i, col_block)), tsp]
    return _call(body, name, (n_i,), in_specs, [row, tsp, vec, vec],
                 [_sds((m, width), BF16), _sds((taps, width)), _sds((1, width)), _sds((1, width))])(dy, dy, x_src, x_src, cw)


def d_mid_fwd(z, cw, name):
    m, w3 = z.shape
    width = w3 // 3
    taps = cw.shape[0]
    tm = _rows(m, 512)

    def body(bm_ref, cp_ref, cm_ref, xp_ref, xm_ref, cw_ref, y_ref):
        keep = (pl.program_id(0) > 0).astype(F32)
        qe = (jnp.concatenate([cp_ref[...] * keep, cm_ref[...]], axis=0)
              * jnp.concatenate([xp_ref[...], xm_ref[...]], axis=0))
        y_ref[...] = (bm_ref[...] * _conv_ext(qe, cw_ref[...], taps)[HALO:]).astype(BF16)

    main = lambda c: pl.BlockSpec((tm, width), lambda i: (i, c))
    prev = lambda c: pl.BlockSpec((HALO, width), lambda i: (_prev_halo(tm)(i), c))
    in_specs = [main(0), prev(1), main(1), prev(2), main(2), pl.BlockSpec((taps, width), lambda i: (0, 0))]
    return _call(body, name, (m // tm,), in_specs, pl.BlockSpec((tm, width), lambda i: (i, 0)),
                 _sds((m, width), BF16))(z, z, z, z, z, cw)


def d_mid_bwd(z, dy, cw, name):
    m, w3 = z.shape
    width = w3 // 3
    taps = cw.shape[0]
    tm = _rows(m, 512)
    n_i = m // tm

    def body(bm_ref, bn_ref, cp_ref, cm_ref, cn_ref, xp_ref, xm_ref, xn_ref, dm_ref, dn_ref, cw_ref, dz_ref, dw_ref):
        i = pl.program_id(0)
        first = i == 0
        kp, kn = (i > 0).astype(F32), (i < n_i - 1).astype(F32)
        zeros = jnp.zeros((HALO, width), F32)
        ce = jnp.concatenate([cp_ref[...] * kp, cm_ref[...], cn_ref[...] * kn], axis=0)
        xe = jnp.concatenate([xp_ref[...], xm_ref[...], xn_ref[...]], axis=0)
        qe = ce * xe
        be = jnp.concatenate([zeros, bm_ref[...], bn_ref[...]], axis=0)
        dye = jnp.concatenate([zeros, dm_ref[...], dn_ref[...] * kn], axis=0)
        w = cw_ref[...]
        cq = _conv_ext(qe, w, taps)
        dcq = dye * be
        dq = dcq * w[taps - 1:taps]
        for s in range(1, taps):
            dq = dq + _up(dcq, s) * w[taps - 1 - s:taps - s]
        ms = slice(HALO, HALO + tm)
        dz_ref[:, :width] = (dye * cq)[ms].astype(BF16)
        dz_ref[:, width:2 * width] = (dq * xe)[ms].astype(BF16)
        dz_ref[:, 2 * width:] = (dq * ce)[ms].astype(BF16)
        for s in range(taps):
            _acc(dw_ref.at[taps - 1 - s:taps - s, :], _colsum(dcq[ms] * _down(qe, s)[ms]), first)

    main = lambda c: pl.BlockSpec((tm, width), lambda i: (i, c))
    prev = lambda c: pl.BlockSpec((HALO, width), lambda i: (_prev_halo(tm)(i), c))
    nxt = lambda c: pl.BlockSpec((HALO, width), lambda i: (_next_halo(tm, m)(i), c))
    tsp = pl.BlockSpec((taps, width), lambda i: (0, 0))
    in_specs = [main(0), nxt(0), prev(1), main(1), nxt(1), prev(2), main(2), nxt(2), main(0), nxt(0), tsp]
    return _call(body, name, (n_i,), in_specs, [pl.BlockSpec((tm, w3), lambda i: (i, 0)), tsp],
                 [_sds((m, w3), BF16), _sds((taps, width))])(z, z, z, z, z, z, z, z, dy, dy, cw)


def _halo_rows(dtype):
    return HALO * (4 // jnp.dtype(dtype).itemsize)


def ffn_gate_down(z, cw, cb, w_down, res, next_g, name):
    _, nj, m, c = z.shape
    n = w_down.shape[1]
    taps = cw.shape[2]
    tm = _rows(m, 256)
    hz = _halo_rows(z.dtype)

    def body(zp_ref, zm_ref, cw_ref, cb_ref, w_ref, res_ref, *rest):
        keep = (pl.program_id(0) > 0).astype(F32)
        zc_ref, act_ref, x_ref = rest[-4:-1] if next_g is not None else rest[-3:]
        acc = res_ref[...]
        for j in range(nj):
            zc = []
            for s in range(2):
                xe = jnp.concatenate([zp_ref[s, j].astype(F32) * keep, zm_ref[s, j].astype(F32)], axis=0)
                zc.append(_conv_ext(xe, cw_ref[s, j], taps)[hz:] + cb_ref[s, j])
                zc_ref[s, j] = zc[s].astype(BF16)
            act = (zc[0] * _sigmoid(zc[0]) * zc[1]).astype(BF16)
            act_ref[j] = act
            acc = acc + _dot(act, w_ref[j])
        x_ref[...] = acc
        if next_g is not None:
            rest[-1][...] = _norm_rows(acc, rest[0][...])

    row = pl.BlockSpec((tm, n), lambda i: (i, 0))
    in_specs = [pl.BlockSpec((2, nj, hz, c), lambda i: (0, 0, _prev_halo(tm, hz)(i), 0)),
                pl.BlockSpec((2, nj, tm, c), lambda i: (0, 0, i, 0)),
                _resident(cw.shape), _resident(cb.shape), _resident((nj, c, n)), row]
    out_specs = [pl.BlockSpec((2, nj, tm, c), lambda i: (0, 0, i, 0)), pl.BlockSpec((nj, tm, c), lambda i: (0, i, 0)), row]
    out_shape = [_sds((2, nj, m, c), BF16), _sds((nj, m, c), BF16), _sds((m, n))]
    ins = [z, z, cw, cb, w_down.reshape(nj, c, n), res]
    if next_g is not None:
        in_specs.append(pl.BlockSpec((1, n), lambda i: (0, 0)))
        out_specs.append(row)
        out_shape.append(_sds((m, n), BF16))
        ins.append(next_g)
    return _call(body, name, (m // tm,), in_specs, out_specs, out_shape)(*ins)


def ffn_gate_bwd(z, zc, dx, w_down, cw, name, after=()):
    _, nj, m, c = z.shape
    n = w_down.shape[1]
    taps = cw.shape[2]
    tm = _rows(m, 512)
    n_i = m // tm
    hz = _halo_rows(z.dtype)
    assert _halo_rows(dx.dtype) == hz and zc.dtype == z.dtype, (z.dtype, zc.dtype, dx.dtype)

    def body(zp_ref, zm_ref, zn_ref, cm_ref, cn_ref, dm_ref, dn_ref, wd_ref, cw_ref, dz_ref, dw_ref, db_ref):
        i = pl.program_id(1)
        first = i == 0
        kp, kn = (i > 0).astype(F32), (i < n_i - 1).astype(F32)
        xe = [jnp.concatenate([zp_ref[s].astype(F32) * kp, zm_ref[s].astype(F32), zn_ref[s].astype(F32) * kn], axis=0)
              for s in range(2)]
        zc = [jnp.concatenate([jnp.zeros((hz, c), F32), cm_ref[s].astype(F32), cn_ref[s].astype(F32)], axis=0)
              for s in range(2)]
        dact = _dot(jnp.concatenate([dm_ref[...], dn_ref[...]], axis=0), wd_ref[...], _NT)
        dae = jnp.concatenate([jnp.zeros((hz, c), F32), dact[:tm], dact[tm:] * kn], axis=0)
        sg = _sigmoid(zc[0])
        dzc = [dae * zc[1] * (sg * (1.0 + zc[0] * (1.0 - sg))), dae * (zc[0] * sg)]
        ms = slice(hz, hz + tm)
        for s in range(2):
            w = cw_ref[s]
            ups = [dzc[s]] + [_up(dzc[s], u) for u in range(1, taps)]
            dxs = ups[0] * w[taps - 1:taps]
            for u in range(1, taps):
                dxs = dxs + ups[u] * w[taps - 1 - u:taps - u]
            dz_ref[s] = dxs[ms].astype(BF16)
            tail = dzc[s][hz + tm:]
            x_end = xe[s][tm:]
            for u in range(taps):
                total = _colsum(ups[u] * xe[s]) - _colsum(tail * _down(x_end, u)[hz:])
                _acc(dw_ref.at[s, taps - 1 - u:taps - u, :], total, first)
            _acc(db_ref.at[s], _colsum(dzc[s][ms]), first)

    in_specs = [pl.BlockSpec((2, None, hz, c), lambda j, i: (0, j, _prev_halo(tm, hz)(i), 0)),
                pl.BlockSpec((2, None, tm, c), lambda j, i: (0, j, i, 0)),
                pl.BlockSpec((2, None, hz, c), lambda j, i: (0, j, _next_halo(tm, m, hz)(i), 0)),
                pl.BlockSpec((2, None, tm, c), lambda j, i: (0, j, i, 0)),
                pl.BlockSpec((2, None, hz, c), lambda j, i: (0, j, _next_halo(tm, m, hz)(i), 0)),
                pl.BlockSpec((tm, n), lambda j, i: (i, 0)),
                pl.BlockSpec((hz, n), lambda j, i: (_next_halo(tm, m, hz)(i), 0)),
                pl.BlockSpec((None, c, n), lambda j, i: (j, 0, 0)),
                pl.BlockSpec((2, None, taps, c), lambda j, i: (0, j, 0, 0))]
    out_specs = [pl.BlockSpec((2, None, tm, c), lambda j, i: (0, j, i, 0)),
                 pl.BlockSpec((2, None, taps, c), lambda j, i: (0, j, 0, 0)),
                 pl.BlockSpec((2, None, 1, c), lambda j, i: (0, j, 0, 0))]
    return _call(body, name, (nj, n_i), in_specs, out_specs,
                 [_sds((2, nj, m, c), BF16), _sds((2, nj, taps, c)), _sds((2, nj, 1, c))], (), after)(
        z, z, z, zc, zc, dx, dx, w_down.reshape(nj, c, n), cw)


_STREAM_TILE_BYTES = 2 << 20


def _tile_rows(r, c):
    if r * c * 4 <= _STREAM_TILE_BYTES:
        return r
    fits = [d for d in range(16, r, 16) if r % d == 0 and d * c * 4 <= _STREAM_TILE_BYTES]
    return max(fits) if fits else r


def _as2d(a, lead):
    shape = a.shape
    return a.reshape((lead, -1, shape[-1]) if lead else (-1, shape[-1]))


def add_pairs(g, l1, own, name):
    shape = l1.shape
    g3, l3 = _as2d(g, N_DEV), _as2d(l1, 4)
    _, r, c = l3.shape
    tr = _tile_rows(r, c)

    def body(own_ref, a_ref, b_ref, o_ref):
        o_ref[...] = (a_ref[...].astype(F32) + b_ref[...].astype(F32)).astype(o_ref.dtype)

    spec = pl.BlockSpec((None, tr, c), lambda k, i, own_ref: (k, i, 0))
    grid_spec = pltpu.PrefetchScalarGridSpec(
        num_scalar_prefetch=1, grid=(4, r // tr),
        in_specs=[pl.BlockSpec((None, tr, c), lambda k, i, own_ref: (own_ref[k], i, 0)), spec], out_specs=spec)
    out = pl.pallas_call(
        body, name=name, grid_spec=grid_spec, out_shape=_sds(l3.shape, l1.dtype),
        compiler_params=pltpu.CompilerParams(dimension_semantics=("arbitrary", "arbitrary"),
                                             vmem_limit_bytes=V7X_VMEM_LIMIT_BYTES))(own, g3, l3)
    return out.reshape(shape)


def _grad_sum(p_ref, l_ref):
    return ((p_ref[...].astype(F32) + l_ref[0].astype(F32)) + l_ref[1].astype(F32)) + l_ref[2].astype(F32)


def sum_parts(p, l2, name):
    _, r, c = p.shape

    def body(p_ref, l_ref, o_ref):
        o_ref[...] = _grad_sum(p_ref, l_ref)

    return _call(body, name, (1,), [pl.BlockSpec((None, r, c), lambda i: (0, 0, 0)), pl.BlockSpec((3, r, c), lambda i: (0, 0, 0))],
                 pl.BlockSpec((r, c), lambda i: (0, 0)), _sds((r, c)))(p, l2)


def _adamw_math(w, g, m, v):
    m = ADAM_B1 * m + (1.0 - ADAM_B1) * g
    v = ADAM_B2 * v + (1.0 - ADAM_B2) * (g * g)
    m_hat = m / (1.0 - ADAM_B1 ** ADAM_STEP)
    v_hat = v / (1.0 - ADAM_B2 ** ADAM_STEP)
    delta = -ADAM_LR * (m_hat / (jnp.sqrt(v_hat) + ADAM_EPS) + ADAM_WD * w)
    return delta, m, v


def adamw(w, m, v, name, g=None, p=None, l2=None):
    shape = w.shape
    w2, m2, v2 = (_as2d(t, 0) for t in (w, m, v))
    r, c = w2.shape
    tr = _tile_rows(r, c)
    row = pl.BlockSpec((tr, c), lambda i: (i, 0))
    if g is None:
        p3, l3 = _as2d(p, 4), _as2d(l2, 3)
        gin = (p3, l3)
        gspecs = [pl.BlockSpec((None, tr, c), lambda i: (0, i, 0)), pl.BlockSpec((3, tr, c), lambda i: (0, i, 0))]
    else:
        gin, gspecs = (_as2d(g, 0),), [row]

    def body(*refs):
        n_g = len(gin)
        w_ref, m_ref, v_ref, g_ref, d_ref, nm_ref, nv_ref = refs[n_g:]
        grad = refs[0][...] if n_g == 1 else _grad_sum(refs[0], refs[1])
        delta, nm, nv = _adamw_math(w_ref[...], grad, m_ref[...], v_ref[...])
        g_ref[...] = grad
        d_ref[...] = delta
        nm_ref[...] = nm
        nv_ref[...] = nv

    outs = _call(body, name, (r // tr,), gspecs + [row, row, row], [row] * 4, [_sds((r, c))] * 4)(*gin, w2, m2, v2)
    return tuple(o.reshape(shape) for o in outs)


def adamw_layer(w, m, v, layer, prev, name, p, l2):
    n_l, r, c = w.shape
    tr = _tile_rows(r, c)
    slab = pl.BlockSpec((None, tr, c), lambda i: (layer, i, 0))
    in_specs = [pl.BlockSpec((None, tr, c), lambda i: (0, i, 0)), pl.BlockSpec((3, tr, c), lambda i: (0, i, 0)), slab, slab, slab]
    n_in = len(in_specs)
    prev = () if prev is None else tuple(prev)

    def body(p_ref, l_ref, w_ref, m_ref, v_ref, *rest):
        g_ref, d_ref, nm_ref, nv_ref = rest[len(prev):]
        grad = _grad_sum(p_ref, l_ref)
        delta, nm, nv = _adamw_math(w_ref[...], grad, m_ref[...], v_ref[...])
        g_ref[...] = grad
        d_ref[...] = delta
        nm_ref[...] = nm
        nv_ref[...] = nv

    return pl.pallas_call(
        body, name=name, grid=(r // tr,), in_specs=in_specs + [pl.BlockSpec(memory_space=pl.ANY)] * len(prev),
        out_specs=[slab] * 4, out_shape=[_sds((n_l, r, c))] * 4,
        input_output_aliases={n_in + q: q for q in range(len(prev))},
        compiler_params=pltpu.CompilerParams(dimension_semantics=("arbitrary",), vmem_limit_bytes=V7X_VMEM_LIMIT_BYTES),
    )(_as2d(p, 4), _as2d(l2, 3), w, m, v, *prev)


def _comm_call(body, name, ins, out_shape, n_sems):
    any_spec = pl.BlockSpec(memory_space=pl.ANY)
    return pl.pallas_call(
        body, name=name, in_specs=[any_spec] * len(ins), out_specs=[any_spec] * len(out_shape), out_shape=out_shape,
        scratch_shapes=[pltpu.SemaphoreType.DMA((n,)) for n in n_sems],
        compiler_params=pltpu.CompilerParams(has_side_effects=True))(*ins)


def _place():
    return lax.axis_index("x"), lax.axis_index("y"), lax.axis_index("c")


def _dev_index(px, py, pc):
    return 4 * px + 2 * py + pc


def all_gather(blocks, name):
    n_t = len(blocks)

    def body(*refs):
        ins, outs = refs[:n_t], refs[n_t:2 * n_t]
        send_sems, recv_sems, local_sems = refs[2 * n_t:]
        x, y, c = _place()
        me, sibling = (x, y, c), (x, y, 1 - c)
        chips = [(1 - x, y), (x, 1 - y), (1 - x, 1 - y)]

        def copy(t, k, block, to, src=None):
            dst = outs[t].at[_dev_index(*block)]
            return pltpu.make_async_remote_copy(
                src_ref=dst if src is None else src, dst_ref=dst, send_sem=send_sems.at[t * 7 + k],
                recv_sem=recv_sems.at[t * 7 + k], device_id=to, device_id_type=MESH_ID)

        mine = [pltpu.make_async_copy(ins[t], outs[t].at[_dev_index(*me)], local_sems.at[t]) for t in range(n_t)]
        for cp in mine:
            cp.start()
        first = []
        for t in range(n_t):
            first.append(copy(t, 0, me, sibling, src=ins[t]))
            first += [copy(t, 1 + j, me, (*chip, c), src=ins[t]) for j, chip in enumerate(chips)]
        for cp in first:
            cp.start()
        passed = []
        for t in range(n_t):
            for j, chip in enumerate(chips):
                copy(t, 1 + j, (*chip, c), me).wait_recv()
                cp = copy(t, 4 + j, (*chip, c), sibling)
                cp.start()
                passed.append(cp)
        for t in range(n_t):
            copy(t, 0, sibling, me).wait_recv()
            for j, chip in enumerate(chips):
                copy(t, 4 + j, (*chip, 1 - c), me).wait_recv()
        for cp in first + passed:
            cp.wait_send()
        for cp in mine:
            cp.wait()

    out_shape = [_sds((N_DEV,) + b.shape, b.dtype) for b in blocks]
    return _comm_call(body, name, blocks, out_shape, (7 * n_t, 7 * n_t, n_t))


def _chip_of(x, y, k):
    return (x if k % 2 == 0 else 1 - x), (y if k // 2 == 0 else 1 - y)


_HBM_SPEC = pl.BlockSpec(memory_space=pltpu.HBM)
_SEM_SPEC = pl.BlockSpec(memory_space=pltpu.SEMAPHORE)
_DATAFLOW = pltpu.SideEffectType.DATAFLOW_SIDE_EFFECTING


def _in_hbm(a):
    return pltpu.with_memory_space_constraint(a, pltpu.HBM)


def _split_start(name, issue, srcs, land_shapes, sem_counts, after=()):
    n_buf, n_sem, n_after = len(srcs) + len(land_shapes), len(sem_counts), len(after)

    def body(*refs):
        issue(refs[:len(srcs)], refs[len(srcs):n_buf], refs[n_buf + n_after:n_buf + n_after + n_sem])
        refs[-1][...] = jnp.zeros_like(refs[-1])

    bufs = [pltpu.HBM(s.shape, s.dtype) for s in list(srcs) + list(land_shapes)]
    outs = pl.pallas_call(
        body, name=name, in_specs=(*[_HBM_SPEC] * n_buf, *[pl.BlockSpec(memory_space=pl.ANY)] * n_after),
        out_shape=(*[pltpu.SemaphoreType.DMA((n,)) for n in sem_counts], *bufs, _sds((8, LANES))),
        out_specs=(*[_SEM_SPEC] * n_sem, *[_HBM_SPEC] * n_buf, pl.BlockSpec(memory_space=pltpu.VMEM)),
        input_output_aliases={i: n_sem + i for i in range(n_buf)},
        compiler_params=pltpu.CompilerParams(has_side_effects=_DATAFLOW),
    )(*[_in_hbm(s) for s in srcs], *[_in_hbm(lax.empty(s.shape, s.dtype)) for s in land_shapes], *after)
    return outs[:n_sem], outs[n_sem:n_sem + len(srcs)], outs[n_sem + len(srcs):n_sem + n_buf], outs[-1]


def _split_wait(name, finish, sems, srcs, lands, after):
    n_buf, n_sem = len(srcs) + len(lands), len(sems)

    def body(*refs):
        finish(refs[:len(srcs)], refs[len(srcs):n_buf], refs[n_buf:n_buf + n_sem])

    bufs = [pltpu.HBM(s.shape, s.dtype) for s in list(srcs) + list(lands)]
    outs = pl.pallas_call(
        body, name=name, in_specs=(*[_HBM_SPEC] * n_buf, *[_SEM_SPEC] * n_sem, *[pl.BlockSpec(memory_space=pl.ANY)] * len(after)),
        out_shape=tuple(bufs), out_specs=(_HBM_SPEC,) * n_buf, input_output_aliases={i: i for i in range(n_buf)},
        compiler_params=pltpu.CompilerParams(has_side_effects=_DATAFLOW),
    )(*srcs, *lands, *sems, *after)
    return outs[:len(srcs)], outs[len(srcs):]


def _peer(x, y, c, r):
    return (1 - x if r & 4 else x), (1 - y if r & 2 else y), (1 - c if r & 1 else c)


def _gather_copies(src_refs, land_refs, sem_refs, arrivals):
    send_sems, recv_sems, local_sems = sem_refs
    x, y, c = _place()
    me = _dev_index(x, y, c)
    local, sends, recvs = [], [], []
    for j, (src, land) in enumerate(zip(src_refs, land_refs)):
        local.append(pltpu.make_async_copy(src, land.at[me], local_sems.at[j]))
        for r in range(1, N_DEV):
            peer = _peer(x, y, c, r)
            q = (N_DEV - 1) * j + r - 1
            sends.append(pltpu.make_async_remote_copy(src_ref=src, dst_ref=land.at[me], send_sem=send_sems.at[q],
                                                      recv_sem=recv_sems.at[q], device_id=peer, device_id_type=MESH_ID))
            if arrivals:
                recvs.append(pltpu.make_async_remote_copy(
                    src_ref=src, dst_ref=land.at[_dev_index(*peer)], send_sem=send_sems.at[q], recv_sem=recv_sems.at[q],
                    device_id=peer, device_id_type=MESH_ID))
    return local, sends, recvs


def gather_start(groups, name, after=()):
    flat = [b for g in groups for b in g]
    bounds = [sum(len(g) for g in groups[:i]) for i in range(len(groups) + 1)]

    def issue(src_refs, land_refs, sem_refs):
        for i in range(len(groups)):
            lo, hi = bounds[i], bounds[i + 1]
            local, sends, _ = _gather_copies(src_refs[lo:hi], land_refs[lo:hi], sem_refs[3 * i:3 * i + 3], False)
            for cp in local + sends:
                cp.start()

    sem_counts = [n for g in groups for n in ((N_DEV - 1) * len(g), (N_DEV - 1) * len(g), len(g))]
    sems, srcs, lands, token = _split_start(name, issue, flat, [_sds((N_DEV,) + b.shape, b.dtype) for b in flat], sem_counts,
                                            after)
    return [(sems[3 * i:3 * i + 3], srcs[bounds[i]:bounds[i + 1]], lands[bounds[i]:bounds[i + 1]])
            for i in range(len(groups))], token


def gather_wait(group, after, name):
    sems, srcs, lands = group

    def finish(src_refs, land_refs, sem_refs):
        local, sends, recvs = _gather_copies(src_refs, land_refs, sem_refs, True)
        for cp in local:
            cp.wait()
        for cp in recvs:
            cp.wait_recv()
        for cp in sends:
            cp.wait_send()

    return _split_wait(name, finish, sems, srcs, lands, after)[1]


def _sibling_copies(src_refs, land_refs, sem_refs):
    send_sems, recv_sems = sem_refs
    x, y, c = _place()
    copies = []
    for t, (src, land) in enumerate(zip(src_refs, land_refs)):
        for k in range(4):
            cx, cy = _chip_of(x, y, k)
            copies.append(pltpu.make_async_remote_copy(
                src_ref=src.at[_dev_index(cx, cy, 1 - c)], dst_ref=land.at[k], send_sem=send_sems.at[4 * t + k],
                recv_sem=recv_sems.at[4 * t + k], device_id=(x, y, 1 - c), device_id_type=MESH_ID))
    return copies


def _chip_copies(src_refs, land_refs, sem_refs):
    send_sems, recv_sems = sem_refs
    x, y, c = _place()
    copies = []
    for t, (src, land) in enumerate(zip(src_refs, land_refs)):
        for k in range(1, 4):
            cx, cy = _chip_of(x, y, k)
            copies.append(pltpu.make_async_remote_copy(
                src_ref=src.at[k], dst_ref=land.at[k - 1], send_sem=send_sems.at[3 * t + k - 1],
                recv_sem=recv_sems.at[3 * t + k - 1], device_id=(cx, cy, c), device_id_type=MESH_ID))
    return copies


def _exchange_start(copies_of, n_land, per_array, arrays, name):
    def issue(src_refs, land_refs, sem_refs):
        for cp in copies_of(src_refs, land_refs, sem_refs):
            cp.start()

    n = per_array * len(arrays)
    lands = [_sds((n_land,) + a.shape[1:], a.dtype) for a in arrays]
    return _split_start(name, issue, arrays, lands, (n, n))


def _exchange_wait(copies_of, started, after, name):
    sems, srcs, lands, _ = started

    def finish(src_refs, land_refs, sem_refs):
        copies = copies_of(src_refs, land_refs, sem_refs)
        for cp in copies:
            cp.wait_recv()
        for cp in copies:
            cp.wait_send()

    return _split_wait(name, finish, sems, srcs, lands, after)


def sibling_start(grads, name):
    return _exchange_start(_sibling_copies, 4, 4, grads, name)


def sibling_wait(started, after, name):
    return _exchange_wait(_sibling_copies, started, after, name)


def chips_start(parts, name):
    return _exchange_start(_chip_copies, 3, 3, parts, name)


def chips_wait(started, after, name):
    return _exchange_wait(_chip_copies, started, after, name)


def _pack(arrays, rows):
    flat = jnp.concatenate([a.reshape(-1) for a in arrays])
    return jnp.pad(flat, (0, rows * LANES - flat.shape[0])).reshape(rows, LANES)


def _pack_stacked(arrays, rows):
    flat = jnp.concatenate([a.reshape(N_DEV, -1) for a in arrays], axis=1)
    return jnp.pad(flat, ((0, 0), (0, rows * LANES - flat.shape[1]))).reshape(N_DEV, rows, LANES)


def _unpack(buf, shapes, lead=()):
    flat = buf.reshape(lead + (-1,))
    out, off = [], 0
    for s in shapes:
        n = 1
        for d in s:
            n *= d
        out.append(flat[..., off:off + n].reshape(lead + tuple(s)))
        off += n
    return out


def _padded_rows(shapes, multiple):
    n = sum(functools.reduce(lambda a, b: a * b, s, 1) for s in shapes)
    rows = -(-n // LANES)
    return -(-rows // multiple) * multiple


def _to_full(stacked, axis):
    t = jnp.moveaxis(stacked, 0, axis)
    return t.reshape(t.shape[:axis] + (t.shape[axis] * t.shape[axis + 1],) + t.shape[axis + 2:])


def _to_stacked(full, axis):
    s = full.shape
    t = full.reshape(s[:axis] + (N_DEV, s[axis] // N_DEV) + s[axis + 1:])
    return jnp.moveaxis(t, axis, 0)


def _ffn_forward(x, h, w_up, cw, cb, w_down, next_g, tag):
    z = mm_in(h, w_up, f"ffn{tag}_up", stacked_out=True, out_dtype=BF16, rows=2048)
    nb, m, c = z.shape
    z4 = z.reshape(2, nb // 2, m, c)
    zc, act, *out = ffn_gate_down(z4, cw, cb, w_down, x, next_g, f"ffn{tag}_gate_down")
    return (out[0], out[1] if next_g is not None else None), (x, h, z4, zc, act)


def _ffn_backward(dx, dxb, saved, norm_g, w_up, cw, w_down, tag, after=()):
    x, h, z4, zc, act = saved
    nj = act.shape[0]
    dz4, dcw, dcb = ffn_gate_bwd(z4, zc, dxb, w_down, cw, f"ffn{tag}_gate_bwd", after)
    dw_down = mm_dw_out(act, dxb, f"ffn{tag}_down_dw")
    dz = dz4.reshape((2 * nj,) + dz4.shape[2:])
    dx, dxb, dg = mm_dx_in(dz, w_up, f"ffn{tag}_up_dx", norm=(x, norm_g, dx))
    dw_up = mm_dw_in(h, dz, 2 * nj, f"ffn{tag}_up_dw", transposed=True)
    return dx, dxb, dict(norm_g=dg, w_up=dw_up, conv_w=dcw, conv_b=dcb, w_down=dw_down)


def kernel(x, a_norm_g, a_w_in, a_b_in, a_v_norm_g, a_w_s, a_b_s, a_w_out, b_norm_g, b_w_in, b_w_grp, b_b_grp, b_scale, b_w_out, c_norm_g, c_w_in, c_b_in, c_conv_w, c_conv_b, c_w_a, c_b_a, c_w_i, c_b_i, c_lambda, c_w_out, d_norm_g, d_w_in, d_conv_w, d_w_out, ffn_norm_g, ffn_w_up, ffn_conv_w, ffn_conv_b, ffn_w_down, final_norm_g, loss_target, m_a_norm_g, m_a_w_in, m_a_b_in, m_a_v_norm_g, m_a_w_s, m_a_b_s, m_a_w_out, m_b_norm_g, m_b_w_in, m_b_w_grp, m_b_b_grp, m_b_scale, m_b_w_out, m_c_norm_g, m_c_w_in, m_c_b_in, m_c_conv_w, m_c_conv_b, m_c_w_a, m_c_b_a, m_c_w_i, m_c_b_i, m_c_lambda, m_c_w_out, m_d_norm_g, m_d_w_in, m_d_conv_w, m_d_w_out, m_ffn_norm_g, m_ffn_w_up, m_ffn_conv_w, m_ffn_conv_b, m_ffn_w_down, m_final_norm_g, v_a_norm_g, v_a_w_in, v_a_b_in, v_a_v_norm_g, v_a_w_s, v_a_b_s, v_a_w_out, v_b_norm_g, v_b_w_in, v_b_w_grp, v_b_b_grp, v_b_scale, v_b_w_out, v_c_norm_g, v_c_w_in, v_c_b_in, v_c_conv_w, v_c_conv_b, v_c_w_a, v_c_b_a, v_c_w_i, v_c_b_i, v_c_lambda, v_c_w_out, v_d_norm_g, v_d_w_in, v_d_conv_w, v_d_w_out, v_ffn_norm_g, v_ffn_w_up, v_ffn_conv_w, v_ffn_conv_b, v_ffn_w_down, v_final_norm_g):
    args = locals()
    w_loc = {n: args[n] for n in WEIGHTS}
    m_loc = {n: args["m_" + n] for n in WEIGHTS}
    v_loc = {n: args["v_" + n] for n in WEIGHTS}
    depth = ffn_w_up.shape[0]
    xs = x[0]
    target = loss_target[0]

    small_names = list(SMALL_SHARDED)
    small_shapes = [w_loc[n].shape for n in small_names]
    small_rows = _padded_rows(small_shapes, 8)
    small_packed = _pack([w_loc[n] for n in small_names], small_rows)
    early_names = ['ffn_conv_w', 'b_norm_g', 'c_norm_g', 'd_norm_g']
    late_names = [n for n in small_names if n not in early_names]
    packs = []
    for names in (early_names, late_names):
        shapes = [w_loc[n].shape for n in names]
        packs.append((names, shapes, _pack([w_loc[n] for n in names], _padded_rows(shapes, 8))))
    wa_in, wa_out = all_gather([a_w_in[0].astype(BF16), a_w_out[0].astype(BF16)], "gather_a")
    mixers = [None, (b_w_in, b_w_out), (c_w_in, c_w_out), (d_w_in, d_w_out)]
    groups = {}
    for l in range(depth):
        if l > 0:
            groups[f'mixer{l}'] = [mixers[l][0][0].astype(BF16), mixers[l][1][0].astype(BF16)] + ([packs[1][2]] if l == 1 else [])
        groups[f'ffn{l}'] = [ffn_w_up[l].astype(BF16), ffn_w_down[l].astype(BF16)] + ([packs[0][2]] if l == 0 else [])
    in_flight = dict(zip(groups, gather_start(list(groups.values()), "gather_start", (wa_in, wa_out))[0]))
    sm = {}

    def unpack_small(pack, gathered):
        names, shapes, _ = pack
        sm.update((n, _to_full(s, SMALL_SHARDED[n])) for n, s in zip(names, _unpack(gathered, shapes, (N_DEV,))))

    def rows_full(st):
        return st.reshape((st.shape[0] * st.shape[1],) + st.shape[2:])

    nb = N_DEV
    ffn_cb = [ffn_conv_b[l].reshape(2, nb // 2, 1, -1) for l in range(depth)]
    ffn_g = [ffn_norm_g[l:l + 1] for l in range(depth)]
    a_bst = a_b_s[0].T
    w_up, w_down, saved = [None] * depth, [None] * depth, {}

    def ffn_forward(xl, hl, l, next_g):
        up, down, *early = gather_wait(in_flight[f'ffn{l}'], (xl,), f"gather_wait_ffn{l}")
        if early:
            unpack_small(packs[0], early[0])
            ffn_cw.extend(sm['ffn_conv_w'][k].reshape(ffn_conv_w.shape[1], 2, nb // 2, -1).transpose(1, 2, 0, 3)
                          for k in range(depth))
        w_up[l], w_down[l] = up, rows_full(down)
        return _ffn_forward(xl, hl, w_up[l], ffn_cw[l], ffn_cb[l], w_down[l], next_g(), l)

    ffn_cw = []
    wa_out = rows_full(wa_out)
    h = rms_fwd(xs, a_norm_g, "a_norm")
    z = mm_in(h, wa_in, "a_in", bias=a_b_in)
    y = a_mid_fwd(z, a_v_norm_g, a_w_s[0], a_bst, "a_mid")
    x1, hf = mm_out(y, wa_out, xs, ffn_g[0], "a_out")
    saved['a'] = (xs, h, z, y)
    (x1, h), saved['f0'] = ffn_forward(x1, hf, 0, lambda: sm['b_norm_g'])

    wb_in, wb_out, late = gather_wait(in_flight['mixer1'], (x1,), "gather_wait_b")
    unpack_small(packs[1], late)
    b_wgrp, c_wa, c_wi = (sm[n][0].astype(BF16) for n in ('b_w_grp', 'c_w_a', 'c_w_i'))
    b_bgrp, c_ba, c_bi = (sm[n][0].reshape(1, -1) for n in ('b_b_grp', 'c_b_a', 'c_b_i'))
    wb_in, wb_out = rows_full(wb_in)[None], rows_full(wb_out)
    z = mm_in(h, wb_in, "b_in")
    y = b_mid_fwd(z, b_wgrp, b_bgrp, sm['b_scale'], "b_mid")
    x2, hf = mm_out(y, wb_out, x1, ffn_g[1], "b_out")
    saved['b'] = (x1, h, z, y)
    (x2, h), saved['f1'] = ffn_forward(x2, hf, 1, lambda: sm['c_norm_g'])

    wc_in, wc_out = gather_wait(in_flight['mixer2'], (x2,), "gather_wait_c")
    wc_out = rows_full(wc_out)
    z = mm_in(h, wc_in, "c_in", bias=sm['c_b_in'])
    c_cw = sm['c_conv_w'][0]
    a_seq, b_seq, xr = c_mid_fwd(z, c_cw, sm['c_conv_b'], c_wa, c_ba, c_wi, c_bi, sm['c_lambda'], "c_mid")
    hs, y = c_scan_fwd(a_seq, b_seq, z, "c_scan")
    x3, hf = mm_out(y, wc_out, x2, ffn_g[2], "c_out")
    saved['c'] = (x2, h, z, y, a_seq, xr, hs)
    (x3, h), saved['f2'] = ffn_forward(x3, hf, 2, lambda: sm['d_norm_g'])

    wd_in, wd_out = gather_wait(in_flight['mixer3'], (x3,), "gather_wait_d")
    wd_out = rows_full(wd_out)
    z = mm_in(h, wd_in, "d_in")
    d_cw = sm['d_conv_w'][0]
    y = d_mid_fwd(z, d_cw, "d_mid")
    x4, hf = mm_out(y, wd_out, x3, ffn_g[3], "d_out")
    saved['d'] = (x3, h, z, y)
    (x4, _), saved['f3'] = ffn_forward(x4, hf, 3, lambda: None)

    loss_part, dx, dxb, d_final_g = final_loss(x4, final_norm_g.reshape(1, -1), target, "final_loss")
    loss = lax.psum(loss_part[0, 0], ("x", "y", "c"))

    def rows_stacked(full):
        return full.reshape((N_DEV, full.shape[0] // N_DEV) + full.shape[1:])

    mx, my, mc = _place()
    own = jnp.stack([_dev_index(*_chip_of(mx, my, k), mc) for k in range(4)]).astype(jnp.int32)
    repl_shapes = [w_loc[n].shape for n in REPLICATED]
    repl_rows = _padded_rows(repl_shapes, 8 * N_DEV)
    shard_of = {n: (w_loc[n][0], m_loc[n][0], v_loc[n][0])
                for n in ('a_w_in', 'a_w_out', 'b_w_in', 'b_w_out', 'c_w_in', 'c_w_out', 'd_w_in', 'd_w_out')}
    shard_of['small'] = (small_packed, _pack([m_loc[n] for n in small_names], small_rows),
                         _pack([v_loc[n] for n in small_names], small_rows))
    updated = {}

    def finish(n, part, others):
        if n == 'repl':
            chunk = sum_parts(part, others, "rs_sum_repl")
            repl_g = all_gather([chunk], "gather_repl")[0].reshape(repl_rows, LANES)
            updated[n] = adamw(*(_pack([src[k] for k in REPLICATED], repl_rows) for src in (w_loc, m_loc, v_loc)),
                               "adamw_repl", g=repl_g)
        elif n.startswith('ffn_w'):
            base, l = n[:-1], int(n[-1])
            turn = (lambda t: jnp.swapaxes(t, 1, 2)) if base == 'ffn_w_up' else (lambda t: t)
            updated[base] = adamw_layer(turn(w_loc[base]), turn(m_loc[base]), turn(v_loc[base]), l, updated.get(base),
                                        f"adamw_{n}", part, others)
            return updated[base][1]
        else:
            updated[n] = adamw(*shard_of[n], f"adamw_{n}", p=part, l2=others)
        return updated[n][1]

    stages = [None, None]

    def advance(tag, new, after):
        behind = []
        first = None
        if new:
            first = ([n for n, _ in new], sibling_start([g for _, g in new], f"rs_sibling_start{tag}"))
            behind.append(first[1][3])
        second = None
        if stages[0] is not None:
            names, started = stages[0]
            grads, got = sibling_wait(started, after, f"rs_sibling_wait{tag}")
            parts = [add_pairs(g, l, own, f"rs_add_{n}") for n, g, l in zip(names, grads, got)]
            second = (names, chips_start(parts, f"rs_chips_start{tag}"))
            behind.append(second[1][3])
        if stages[1] is not None:
            names, started = stages[1]
            parts, others = chips_wait(started, after, f"rs_chips_wait{tag}")
            behind += [finish(n, p, o) for n, p, o in zip(names, parts, others)]
        stages[:] = [first, second]
        return tuple(behind)

    gf = [None] * depth
    dx, dxb, gf[3] = _ffn_backward(dx, dxb, saved['f3'], ffn_g[3], w_up[3], ffn_cw[3], w_down[3], 3)
    xin, h, z, y = saved['d']
    dy = mm_dx_out(dxb, wd_out, "d_out_dx")
    g_d_w_out = mm_dw_out(y, dxb, "d_out_dw")
    dz, g_d_conv_w = d_mid_bwd(z, dy, d_cw, "d_mid_bwd")
    dx, dxb, g_d_norm_g = mm_dx_in(dz, wd_in, "d_in_dx", norm=(xin, sm['d_norm_g'], dx))
    g_d_w_in = mm_dw_in(h, dz, nb, "d_in_dw")
    behind = advance(0, [('ffn_w_up3', gf[3]['w_up']), ('ffn_w_down3', rows_stacked(gf[3]['w_down'])), ('d_w_in', g_d_w_in),
                         ('d_w_out', rows_stacked(g_d_w_out))], (dx,))

    dx, dxb, gf[2] = _ffn_backward(dx, dxb, saved['f2'], ffn_g[2], w_up[2], ffn_cw[2], w_down[2], 2, behind)
    xin, h, z, y, a_seq, xr, hs = saved['c']
    dy = mm_dx_out(dxb, wc_out, "c_out_dx")
    g_c_w_out = mm_dw_out(y, dxb, "c_out_dw")
    lam_seq, da_seq, dgate, dgate_sum = c_scan_bwd(dy, z, hs, a_seq, "c_scan_bwd")
    dxr, g_c_w_a, g_c_w_i, g_c_b_a, g_c_b_i, g_c_lambda = c_mid_bwd(
        lam_seq, da_seq, xr, c_wa, c_ba, c_wi, c_bi, sm['c_lambda'], "c_mid_bwd")
    dxr_pre, g_c_conv_w, g_c_conv_b, dxr_pre_sum = conv_bwd(dxr, z, 1, c_cw, "c_conv_bwd")
    dz = jnp.concatenate([dgate, dxr_pre], axis=1)
    g_c_b_in = jnp.concatenate([dgate_sum, dxr_pre_sum], axis=1)
    dx, dxb, g_c_norm_g = mm_dx_in(dz, wc_in, "c_in_dx", norm=(xin, sm['c_norm_g'], dx))
    g_c_w_in = mm_dw_in(h, dz, nb, "c_in_dw")
    behind = advance(1, [('ffn_w_up2', gf[2]['w_up']), ('ffn_w_down2', rows_stacked(gf[2]['w_down'])), ('c_w_in', g_c_w_in),
                         ('c_w_out', rows_stacked(g_c_w_out))], (dx,))

    dx, dxb, gf[1] = _ffn_backward(dx, dxb, saved['f1'], ffn_g[1], w_up[1], ffn_cw[1], w_down[1], 1, behind)
    xin, h, z, y = saved['b']
    dy = mm_dx_out(dxb, wb_out, "b_out_dx")
    g_b_w_out = mm_dw_out(y, dxb, "b_out_dw")
    dp, g_b_w_grp, g_b_b_grp, g_b_scale = b_mid_bwd(z, dy, b_wgrp, b_bgrp, sm['b_scale'], "b_mid_bwd")
    dz = b_pool_bwd(dp, "b_pool_bwd")
    dx, dxb, g_b_norm_g = mm_dx_in(dz, wb_in, "b_in_dx", norm=(xin, sm['b_norm_g'], dx))
    g_b_w_in = mm_dw_in(h, dz, 1, "b_in_dw")
    behind = advance(2, [('ffn_w_up1', gf[1]['w_up']), ('ffn_w_down1', rows_stacked(gf[1]['w_down'])),
                         ('b_w_in', rows_stacked(g_b_w_in[0])), ('b_w_out', rows_stacked(g_b_w_out))], (dx,))

    dx, dxb, gf[0] = _ffn_backward(dx, dxb, saved['f0'], ffn_g[0], w_up[0], ffn_cw[0], w_down[0], 0, behind)
    full_small = {
        'b_norm_g': g_b_norm_g, 'b_w_grp': g_b_w_grp[None], 'b_b_grp': g_b_b_grp.reshape(b_b_grp.shape[:2] + (-1,)),
        'b_scale': g_b_scale, 'c_norm_g': g_c_norm_g, 'c_b_in': g_c_b_in, 'c_conv_w': g_c_conv_w[None],
        'c_conv_b': g_c_conv_b, 'c_w_a': g_c_w_a[None], 'c_b_a': g_c_b_a.reshape(c_b_a.shape[:2] + (-1,)),
        'c_w_i': g_c_w_i[None], 'c_b_i': g_c_b_i.reshape(c_b_i.shape[:2] + (-1,)), 'c_lambda': g_c_lambda,
        'd_norm_g': g_d_norm_g, 'd_conv_w': g_d_conv_w[None],
        'ffn_conv_w': jnp.stack([gf[l]['conv_w'].transpose(2, 0, 1, 3).reshape(ffn_conv_w.shape[1], -1) for l in range(depth)])}
    small_grads = _pack_stacked([_to_stacked(full_small[n], SMALL_SHARDED[n]) for n in small_names], small_rows)
    behind = advance(3, [('ffn_w_up0', gf[0]['w_up']), ('ffn_w_down0', rows_stacked(gf[0]['w_down'])), ('small', small_grads)],
                     (dx,))

    xin, h, z, y = saved['a']
    dy = mm_dx_out(dxb, wa_out, "a_out_dx", after=behind)
    g_a_w_out = mm_dw_out(y, dxb, "a_out_dw")
    behind = advance(4, [('a_w_out', rows_stacked(g_a_w_out))], (dy,))
    dz, g_a_b_in, g_a_v_norm_g, g_a_w_s, g_a_b_s = a_mid_bwd(z, dy, a_v_norm_g, a_w_s[0], a_bst, "a_mid_bwd", behind)
    g_a_w_in = mm_dw_in(h, dz, nb, "a_in_dw")
    behind = advance(5, [('a_w_in', g_a_w_in)], (dz,))
    dx, _, g_a_norm_g = mm_dx_in(dz, wa_in, "a_in_dx", behind, norm=(xin, a_norm_g, dx))
    grad_x = dx[None]

    tril = jnp.tril(jnp.ones((A_CHUNK, A_CHUNK), bool))
    repl_full = {
        'a_norm_g': g_a_norm_g, 'a_b_in': g_a_b_in, 'a_v_norm_g': g_a_v_norm_g,
        'a_w_s': jnp.where(tril, g_a_w_s, 0.0)[None], 'a_b_s': g_a_b_s[:, ::LANES].T[None],
        'ffn_norm_g': jnp.concatenate([gf[l]['norm_g'] for l in range(depth)], axis=0),
        'ffn_conv_b': jnp.stack([gf[l]['conv_b'].reshape(-1) for l in range(depth)]), 'final_norm_g': d_final_g.reshape(-1)}
    repl_grads = _pack([repl_full[n] for n in REPLICATED], repl_rows).reshape(N_DEV, repl_rows // N_DEV, LANES)
    behind = advance(6, [('repl', repl_grads)], (dx,))
    behind = advance(7, [], (dx, *behind))
    advance(8, [], (dx, *behind))

    outs = [{}, {}, {}, {}]
    for i, dst in enumerate(outs):
        for n in ('a_w_in', 'a_w_out', 'b_w_in', 'b_w_out', 'c_w_in', 'c_w_out', 'd_w_in', 'd_w_out'):
            dst[n] = updated[n][i][None]
        dst['ffn_w_up'] = jnp.swapaxes(updated['ffn_w_up'][i], 1, 2)
        dst['ffn_w_down'] = updated['ffn_w_down'][i]
        dst.update(zip(small_names, _unpack(updated['small'][i], small_shapes)))
        dst.update(zip(REPLICATED, _unpack(updated['repl'][i], repl_shapes)))
    out_g, out_d, out_m, out_v = outs

    return (loss, grad_x, *[out_g[n] for n in WEIGHTS], *[out_d[n] for n in WEIGHTS], *[out_m[n] for n in WEIGHTS],
            *[out_v[n] for n in WEIGHTS])
```

```python
import functools

import jax
import jax.numpy as jnp
from jax import lax
from jax.experimental import pallas as pl
from jax.experimental.pallas import tpu as pltpu

F32, BF16 = jnp.float32, jnp.bfloat16
MESH_ID = pl.DeviceIdType.MESH
N_DEV = 8
V7X_VMEM_LIMIT_BYTES = 56 << 20
LANES = 128
HALO = 8
POOL_HALO = 16

EPS = 1e-6
A_CHUNK, A_GROUPS = 128, 4
B_WINDOWS = (2, 4, 8, 16)
C_GATE_C = 8.0
ADAM_LR, ADAM_B1, ADAM_B2, ADAM_EPS, ADAM_WD, ADAM_STEP = 0.001, 0.9, 0.999, 1e-08, 0.01, 10

WEIGHTS = ['a_norm_g', 'a_w_in', 'a_b_in', 'a_v_norm_g', 'a_w_s', 'a_b_s', 'a_w_out', 'b_norm_g', 'b_w_in', 'b_w_grp',
           'b_b_grp', 'b_scale', 'b_w_out', 'c_norm_g', 'c_w_in', 'c_b_in', 'c_conv_w', 'c_conv_b', 'c_w_a', 'c_b_a',
           'c_w_i', 'c_b_i', 'c_lambda', 'c_w_out', 'd_norm_g', 'd_w_in', 'd_conv_w', 'd_w_out', 'ffn_norm_g',
           'ffn_w_up', 'ffn_conv_w', 'ffn_conv_b', 'ffn_w_down', 'final_norm_g']
SMALL_SHARDED = {'b_norm_g': 1, 'b_w_grp': 2, 'b_b_grp': 2, 'b_scale': 1, 'c_norm_g': 1, 'c_b_in': 1, 'c_conv_w': 2,
                 'c_conv_b': 1, 'c_w_a': 2, 'c_b_a': 2, 'c_w_i': 2, 'c_b_i': 2, 'c_lambda': 1, 'd_norm_g': 1,
                 'd_conv_w': 2, 'ffn_conv_w': 2}
REPLICATED = ['a_norm_g', 'a_b_in', 'a_v_norm_g', 'a_w_s', 'a_b_s', 'ffn_norm_g', 'ffn_conv_b', 'final_norm_g']


_GELU_C0, _GELU_C1 = 0.7978845608028654, 0.044715


def _gelu(x):
    return 0.5 * x * (1.0 + jnp.tanh(_GELU_C0 * (x + _GELU_C1 * (x * x * x))))


def _gelu_grad(x):
    t = jnp.tanh(_GELU_C0 * (x + _GELU_C1 * (x * x * x)))
    return 0.5 * (1.0 + t) + 0.5 * x * (1.0 - t * t) * (_GELU_C0 * (1.0 + 3.0 * _GELU_C1 * (x * x)))


def _sigmoid(x):
    return jax.nn.sigmoid(x)


def _log1p(x):
    u = 1.0 + x
    return jnp.where(u == 1.0, x, jnp.log(u) * (x / (u - 1.0)))


def _softplus(x):
    return jnp.maximum(x, 0.0) + _log1p(jnp.exp(-jnp.abs(x)))


def _expm1(x):
    poly = x * (1.0 + x * (1 / 2) * (1.0 + x * (1 / 3) * (1.0 + x * (1 / 4) * (1.0 + x * (1 / 5) * (
        1.0 + x * (1 / 6) * (1.0 + x * (1 / 7) * (1.0 + x * (1 / 8))))))))
    return jnp.where(jnp.abs(x) < 0.35, poly, jnp.exp(x) - 1.0)


def _down(xe, s):
    return xe if s == 0 else pltpu.roll(xe, s, 0)


def _up(xe, s):
    return xe if s == 0 else pltpu.roll(xe, xe.shape[0] - s, 0)


def _conv_ext(xe, w, taps):
    y = xe * w[taps - 1:taps]
    for s in range(1, taps):
        y = y + _down(xe, s) * w[taps - 1 - s:taps - s]
    return y


def _acc(ref, val, first):
    @pl.when(first)
    def _():
        ref[...] = val

    @pl.when(jnp.logical_not(first))
    def _():
        ref[...] += val


def _colsum(v):
    return jnp.sum(v, axis=0, keepdims=True)


def _dot(a, b, dims=((1,), (0,))):
    return lax.dot_general(a.astype(BF16), b.astype(BF16), (dims, ((), ())), preferred_element_type=F32)


_NN, _NT, _TN = ((1,), (0,)), ((1,), (1,)), ((0,), (0,))


def _call(body, name, grid, in_specs, out_specs, out_shape, scratch=(), after=()):
    n_in, n_after = len(in_specs), len(after)

    def ordered_body(*refs):
        return body(*refs[:n_in], *refs[n_in + n_after:])

    call = pl.pallas_call(
        ordered_body if n_after else body, name=name, grid=grid,
        in_specs=list(in_specs) + [pl.BlockSpec(memory_space=pl.ANY)] * n_after, out_specs=out_specs,
        out_shape=out_shape, scratch_shapes=list(scratch),
        compiler_params=pltpu.CompilerParams(dimension_semantics=("arbitrary",) * len(grid),
                                             vmem_limit_bytes=V7X_VMEM_LIMIT_BYTES))
    return lambda *args: call(*args, *after)


def _rows(m, t):
    t = min(m, t)
    assert m % t == 0, (m, t)
    return t


def _sds(shape, dtype=F32):
    return jax.ShapeDtypeStruct(tuple(shape), dtype)


def _prev_halo(tm, halo=HALO):
    return lambda i: jnp.maximum(i * (tm // halo) - 1, 0)


def _next_halo(tm, m, halo=HALO):
    return lambda i: jnp.minimum((i + 1) * (tm // halo), m // halo - 1)


def _matmul(name, ins, in_specs, out_shape, o_spec, grid, compute, after=()):
    def body(*refs):
        refs[-1][...] = compute(*refs[:-1]).astype(refs[-1].dtype)

    return _call(body, name, grid, in_specs, o_spec, out_shape, (), after)(*ins)


def mm_in(h, w_st, name, bias=None, stacked_out=False, out_dtype=F32, rows=1024):
    m, k = h.shape
    nb, _, n = w_st.shape
    tm = _rows(m, rows)
    in_specs = [pl.BlockSpec((tm, k), lambda i, j: (i, 0)), pl.BlockSpec((None, k, n), lambda i, j: (j, 0, 0))]
    if stacked_out:
        out, o_spec = _sds((nb, m, n), out_dtype), pl.BlockSpec((None, tm, n), lambda i, j: (j, i, 0))
    else:
        out, o_spec = _sds((m, nb * n), out_dtype), pl.BlockSpec((tm, n), lambda i, j: (i, j))
    if bias is None:
        return _matmul(name, (h, w_st), in_specs, out, o_spec, (m // tm, nb), lambda a, b: _dot(a[...], b[...]))
    in_specs.append(pl.BlockSpec((1, n), lambda i, j: (0, j)))
    return _matmul(name, (h, w_st, bias), in_specs, out, o_spec, (m // tm, nb),
                   lambda a, b, c: _dot(a[...], b[...]) + c[...])


def _split_rows(kf):
    g = max(1, kf // 1024)
    return g, kf // g


def _resident(shape):
    return pl.BlockSpec(shape, lambda *_: (0,) * len(shape), pipeline_mode=pl.Buffered(1))


def _norm_rows(xv, g):
    return (xv * lax.rsqrt(jnp.mean(xv * xv, axis=-1, keepdims=True) + EPS) * g).astype(BF16)


def mm_out(y, w, res, next_g, name, after=()):
    kf, n = w.shape
    m = y.shape[0]
    tm = _rows(m, 512)
    row = pl.BlockSpec((tm, n), lambda i: (i, 0))

    def body(y_ref, w_ref, res_ref, g_ref, x_ref, h_ref):
        xv = res_ref[...] + _dot(y_ref[...], w_ref[...])
        x_ref[...] = xv
        h_ref[...] = _norm_rows(xv, g_ref[...])

    in_specs = [pl.BlockSpec((tm, kf), lambda i: (i, 0)), _resident((kf, n)), row, pl.BlockSpec((1, n), lambda i: (0, 0))]
    return _call(body, name, (m // tm,), in_specs, [row, row], [_sds((m, n)), _sds((m, n), BF16)], (), after)(y, w, res, next_g)


def mm_dx_in(dz, w_st, name, after=(), norm=None):
    nb, k, n = w_st.shape
    m = dz.shape[-2]
    tm = _rows(m, 512)
    w_spec = _resident((nb, k, n))
    if dz.ndim == 3:
        in_specs = [pl.BlockSpec((None, tm, n), lambda i, r=r: (r, i, 0)) for r in range(nb)] + [w_spec]

        def compute(*refs):
            acc = _dot(refs[0][...], refs[nb][0], _NT)
            for r in range(1, nb):
                acc = acc + _dot(refs[r][...], refs[nb][r], _NT)
            return acc

        ins = (*[dz] * nb, w_st)
    else:
        in_specs = [pl.BlockSpec((tm, nb * n), lambda i: (i, 0)), w_spec]

        def compute(dz_ref, w_ref):
            acc = _dot(dz_ref[:, :n], w_ref[0], _NT)
            for r in range(1, nb):
                acc = acc + _dot(dz_ref[:, r * n:(r + 1) * n], w_ref[r], _NT)
            return acc

        ins = (dz, w_st)
    row = pl.BlockSpec((tm, k), lambda i: (i, 0))
    if norm is None:
        return _matmul(name, ins, in_specs, _sds((m, k)), row, (m // tm,), compute, after)
    n_in = len(in_specs)
    vec = pl.BlockSpec((1, k), lambda i: (0, 0))

    def body(*refs):
        x_ref, g_ref, dr_ref, dx_ref, dxb_ref, dg_ref = refs[n_in:]
        dx, dg = _rms_bwd_math(x_ref[...], g_ref[...], compute(*refs[:n_in]))
        dx = dr_ref[...] + dx
        dx_ref[...] = dx
        dxb_ref[...] = dx.astype(BF16)
        _acc(dg_ref, dg, pl.program_id(0) == 0)

    return _call(body, name, (m // tm,), in_specs + [row, vec, row], [row, row, vec],
                 [_sds((m, k)), _sds((m, k), BF16), _sds((1, k))], (), after)(*ins, *norm)


def mm_dx_out(dout, w, name, groups=None, after=()):
    kf, n = w.shape
    m = dout.shape[0]
    tm = _rows(m, 1024)
    g, k = (groups, kf // groups) if groups else _split_rows(kf)
    in_specs = [pl.BlockSpec((tm, n), lambda i, j: (i, 0)), pl.BlockSpec((None, k, n), lambda i, j: (j, 0, 0))]
    if groups:
        out, o_spec = _sds((g, m, k)), pl.BlockSpec((None, tm, k), lambda i, j: (j, i, 0))
    else:
        out, o_spec = _sds((m, kf)), pl.BlockSpec((tm, k), lambda i, j: (i, j))
    return _matmul(name, (dout, w.reshape(g, k, n)), in_specs, out, o_spec, (m // tm, g),
                   lambda a, b: _dot(a[...], b[...], _NT), after)


def mm_dw_in(h, dz, nb, name, transposed=False):
    m, k = h.shape
    if dz.ndim == 3:
        n = dz.shape[2]
        dz_spec = pl.BlockSpec((None, m, n), lambda j: (j, 0, 0))
    else:
        n = dz.shape[1] // nb
        dz_spec = pl.BlockSpec((m, n), lambda j: (0, j))
    in_specs = [_resident((m, k)), dz_spec]
    if transposed:
        return _matmul(name, (h, dz), in_specs, _sds((nb, n, k), BF16), pl.BlockSpec((None, n, k), lambda j: (j, 0, 0)), (nb,),
                       lambda a, b: _dot(b[...], a[...], _TN))
    return _matmul(name, (h, dz), in_specs, _sds((nb, k, n), BF16), pl.BlockSpec((None, k, n), lambda j: (j, 0, 0)), (nb,),
                   lambda a, b: _dot(a[...], b[...], _TN))


def mm_dw_out(y, dout, name):
    m, n = dout.shape
    if y.ndim == 3:
        g, _, k = y.shape
        y_spec = pl.BlockSpec((None, m, k), lambda j: (j, 0, 0))
    else:
        g, k = _split_rows(y.shape[1])
        y_spec = pl.BlockSpec((m, k), lambda j: (0, j))
    in_specs = [y_spec, _resident((m, n))]
    out = _matmul(name, (y, dout), in_specs, _sds((g, k, n), BF16), pl.BlockSpec((None, k, n), lambda j: (j, 0, 0)), (g,),
                  lambda a, b: _dot(a[...], b[...], _TN))
    return out.reshape(g * k, n)


def rms_fwd(x, g, name):
    m, d = x.shape
    tm = _rows(m, 512)

    def body(x_ref, g_ref, o_ref):
        xv = x_ref[...]
        rstd = lax.rsqrt(jnp.mean(xv * xv, axis=-1, keepdims=True) + EPS)
        o_ref[...] = (xv * rstd * g_ref[...]).astype(BF16)

    row = pl.BlockSpec((tm, d), lambda i: (i, 0))
    vec = pl.BlockSpec((1, d), lambda i: (0, 0))
    return _call(body, name, (m // tm,), [row, vec], row, _sds((m, d), BF16))(x, g)


def _rms_bwd_math(xv, g, dh):
    rstd = lax.rsqrt(jnp.mean(xv * xv, axis=-1, keepdims=True) + EPS)
    xhat = xv * rstd
    dxhat = dh * g
    dx = rstd * (dxhat - xhat * jnp.mean(dxhat * xhat, axis=-1, keepdims=True))
    return dx, _colsum(dh * xhat)


def final_loss(x, g, target, name):
    m, d = x.shape
    tm = _rows(m, 512)

    def body(x_ref, g_ref, t_ref, l_ref, dx_ref, dxb_ref, dg_ref):
        xv, gv = x_ref[...], g_ref[...]
        rstd = lax.rsqrt(jnp.mean(xv * xv, axis=-1, keepdims=True) + EPS)
        err = xv * rstd * gv - t_ref[...]
        part = 0.5 * jnp.sum(jnp.mean(err * err, axis=-1, keepdims=True), axis=0, keepdims=True)
        dx, dg = _rms_bwd_math(xv, gv, err * (1.0 / d))
        dx_ref[...] = dx
        dxb_ref[...] = dx.astype(BF16)
        first = pl.program_id(0) == 0
        _acc(l_ref, jnp.broadcast_to(part, l_ref.shape), first)
        _acc(dg_ref, dg, first)

    row = pl.BlockSpec((tm, d), lambda i: (i, 0))
    vec = pl.BlockSpec((1, d), lambda i: (0, 0))
    lsp = pl.BlockSpec((1, LANES), lambda i: (0, 0))
    return _call(body, name, (m // tm,), [row, vec, row], [lsp, row, row, vec],
                 [_sds((1, LANES)), _sds((m, d)), _sds((m, d), BF16), _sds((1, d))])(x, g, target)


def _a_common(z_ref, vg_ref, ws_ref, bst_ref, tm, width):
    gw = width // A_GROUPS
    zp = z_ref[...]
    z = _gelu(zp)
    u, v = z[:, :width], z[:, width:]
    rstd = lax.rsqrt(jnp.mean(v * v, axis=-1, keepdims=True) + EPS)
    vhat = v * rstd
    vn = vhat * vg_ref[...]
    t_i = lax.broadcasted_iota(jnp.int32, (A_CHUNK, A_CHUNK), 0)
    s_i = lax.broadcasted_iota(jnp.int32, (A_CHUNK, A_CHUNK), 1)
    wsm = [jnp.where(s_i <= t_i, ws_ref[g], 0.0).astype(BF16) for g in range(A_GROUPS)]
    bst = bst_ref[...]
    return zp, u, rstd, vhat, vn.astype(BF16), wsm, bst, gw


def a_mid_fwd(z, vg, ws, bst, name):
    m, w2 = z.shape
    width = w2 // 2
    tm = _rows(m, 256)

    def body(z_ref, vg_ref, ws_ref, bst_ref, y_ref):
        _, u, _, _, vnb, wsm, bst, gw = _a_common(z_ref, vg_ref, ws_ref, bst_ref, tm, width)
        for c in range(tm // A_CHUNK):
            r0 = c * A_CHUNK
            for g in range(A_GROUPS):
                c0 = g * gw
                vs = _dot(wsm[g], vnb[r0:r0 + A_CHUNK, c0:c0 + gw]) + bst[:, g:g + 1]
                y_ref[r0:r0 + A_CHUNK, c0:c0 + gw] = (u[r0:r0 + A_CHUNK, c0:c0 + gw] * vs).astype(BF16)

    in_specs = [pl.BlockSpec((tm, w2), lambda i: (i, 0)), pl.BlockSpec((1, width), lambda i: (0, 0)),
                pl.BlockSpec((A_GROUPS, A_CHUNK, A_CHUNK), lambda i: (0, 0, 0)),
                pl.BlockSpec((A_CHUNK, A_GROUPS), lambda i: (0, 0))]
    return _call(body, name, (m // tm,), in_specs, pl.BlockSpec((tm, width), lambda i: (i, 0)),
                 _sds((m, width), BF16))(z, vg, ws, bst)


def a_mid_bwd(z, dy, vg, ws, bst, name, after=()):
    m, w2 = z.shape
    width = w2 // 2
    tm = _rows(m, 256)

    def body(z_ref, dy_ref, vg_ref, ws_ref, bst_ref, dz_ref, dbin_ref, dvg_ref, dws_ref, dbs_ref, dvn_scr, du_scr):
        first = pl.program_id(0) == 0
        zp, u, rstd, vhat, vnb, wsm, bst, gw = _a_common(z_ref, vg_ref, ws_ref, bst_ref, tm, width)
        dy = dy_ref[...]
        dws = [jnp.zeros((A_CHUNK, A_CHUNK), F32) for _ in range(A_GROUPS)]
        dbs = [jnp.zeros((A_CHUNK, 1), F32) for _ in range(A_GROUPS)]
        for c in range(tm // A_CHUNK):
            r0 = c * A_CHUNK
            for g in range(A_GROUPS):
                c0 = g * gw
                vn_cg = vnb[r0:r0 + A_CHUNK, c0:c0 + gw]
                vs = _dot(wsm[g], vn_cg) + bst[:, g:g + 1]
                dy_cg = dy[r0:r0 + A_CHUNK, c0:c0 + gw]
                dvs = dy_cg * u[r0:r0 + A_CHUNK, c0:c0 + gw]
                du_scr[r0:r0 + A_CHUNK, c0:c0 + gw] = dy_cg * vs
                dws[g] = dws[g] + _dot(dvs, vn_cg, _NT)
                dbs[g] = dbs[g] + jnp.sum(dvs, axis=1, keepdims=True)
                dvn_scr[r0:r0 + A_CHUNK, c0:c0 + gw] = _dot(wsm[g], dvs, _TN)
        for g in range(A_GROUPS):
            _acc(dws_ref.at[g], dws[g], first)
            _acc(dbs_ref.at[:, g * LANES:(g + 1) * LANES], jnp.broadcast_to(dbs[g], (A_CHUNK, LANES)), first)
        dvn = dvn_scr[...]
        _acc(dvg_ref, _colsum(dvn * vhat), first)
        dvhat = dvn * vg_ref[...]
        dv = rstd * (dvhat - vhat * jnp.mean(dvhat * vhat, axis=-1, keepdims=True))
        gg = _gelu_grad(zp)
        dzu = du_scr[...] * gg[:, :width]
        dzv = dv * gg[:, width:]
        dz_ref[:, :width] = dzu.astype(BF16)
        dz_ref[:, width:] = dzv.astype(BF16)
        _acc(dbin_ref.at[:, :width], _colsum(dzu), first)
        _acc(dbin_ref.at[:, width:], _colsum(dzv), first)

    const2 = lambda i: (0, 0)
    in_specs = [pl.BlockSpec((tm, w2), lambda i: (i, 0)), pl.BlockSpec((tm, width), lambda i: (i, 0)),
                pl.BlockSpec((1, width), const2), pl.BlockSpec((A_GROUPS, A_CHUNK, A_CHUNK), lambda i: (0, 0, 0)),
                pl.BlockSpec((A_CHUNK, A_GROUPS), const2)]
    out_specs = [pl.BlockSpec((tm, w2), lambda i: (i, 0)), pl.BlockSpec((1, w2), const2), pl.BlockSpec((1, width), const2),
                 pl.BlockSpec((A_GROUPS, A_CHUNK, A_CHUNK), lambda i: (0, 0, 0)),
                 pl.BlockSpec((A_CHUNK, A_GROUPS * LANES), const2)]
    out_shape = [_sds((m, w2), BF16), _sds((1, w2)), _sds((1, width)), _sds((A_GROUPS, A_CHUNK, A_CHUNK)),
                 _sds((A_CHUNK, A_GROUPS * LANES))]
    scratch = [pltpu.VMEM((tm, width), F32), pltpu.VMEM((tm, width), F32)]
    return _call(body, name, (m // tm,), in_specs, out_specs, out_shape, scratch, after)(z, dy, vg, ws, bst)


def _pool_minus_id(ze, i, tm, gw):
    pos = i * tm + lax.broadcasted_iota(jnp.int32, (tm, 1), 0)
    out = []
    for gi, win in enumerate(B_WINDOWS):
        s = ze[:, gi * gw:(gi + 1) * gw]
        step = 1
        while step < win:
            s = s + _down(s, step)
            step *= 2
        inv = 1.0 / jnp.minimum(pos + 1, win).astype(F32)
        out.append(s[POOL_HALO:] * inv - ze[POOL_HALO:, gi * gw:(gi + 1) * gw])
    return out


def _b_specs(tm, width):
    return [pl.BlockSpec((POOL_HALO, width), lambda i: (_prev_halo(tm, POOL_HALO)(i), 0)),
            pl.BlockSpec((tm, width), lambda i: (i, 0))]


def b_mid_fwd(z, wgrp, bgrp, scale, name):
    m, width = z.shape
    ng = len(B_WINDOWS)
    gw = width // ng
    tm = _rows(m, 512)

    def body(zp_ref, zm_ref, w_ref, b_ref, s_ref, y_ref):
        i = pl.program_id(0)
        ze = jnp.concatenate([zp_ref[...] * (i > 0).astype(F32), zm_ref[...]], axis=0)
        p = _pool_minus_id(ze, i, tm, gw)
        for g in range(ng):
            cs = slice(g * gw, (g + 1) * gw)
            y = (_dot(p[g], w_ref[g]) + b_ref[:, cs]) * s_ref[:, cs]
            y_ref[:, cs] = y.astype(BF16)

    vec = pl.BlockSpec((1, width), lambda i: (0, 0))
    in_specs = _b_specs(tm, width) + [pl.BlockSpec((ng, gw, gw), lambda i: (0, 0, 0)), vec, vec]
    return _call(body, name, (m // tm,), in_specs, pl.BlockSpec((tm, width), lambda i: (i, 0)),
                 _sds((m, width), BF16))(z, z, wgrp, bgrp, scale)


def b_mid_bwd(z, dy, wgrp, bgrp, scale, name):
    m, width = z.shape
    ng = len(B_WINDOWS)
    gw = width // ng
    tm = _rows(m, 512)

    def body(zp_ref, zm_ref, dy_ref, w_ref, b_ref, s_ref, dp_ref, dw_ref, db_ref, ds_ref):
        i = pl.program_id(0)
        first = i == 0
        ze = jnp.concatenate([zp_ref[...] * (i > 0).astype(F32), zm_ref[...]], axis=0)
        p = _pool_minus_id(ze, i, tm, gw)
        for g in range(ng):
            cs = slice(g * gw, (g + 1) * gw)
            dyg = dy_ref[:, cs]
            ypre = _dot(p[g], w_ref[g]) + b_ref[:, cs]
            dyp = dyg * s_ref[:, cs]
            _acc(ds_ref.at[:, cs], _colsum(dyg * ypre), first)
            _acc(db_ref.at[:, cs], _colsum(dyp), first)
            _acc(dw_ref.at[g], _dot(p[g], dyp, _TN), first)
            dp_ref[:, cs] = _dot(dyp, w_ref[g], _NT)

    vec = pl.BlockSpec((1, width), lambda i: (0, 0))
    row = pl.BlockSpec((tm, width), lambda i: (i, 0))
    wsp = pl.BlockSpec((ng, gw, gw), lambda i: (0, 0, 0))
    return _call(body, name, (m // tm,), _b_specs(tm, width) + [row, wsp, vec, vec], [row, wsp, vec, vec],
                 [_sds((m, width)), _sds((ng, gw, gw)), _sds((1, width)), _sds((1, width))])(z, z, dy, wgrp, bgrp, scale)


def b_pool_bwd(dp, name):
    m, width = dp.shape
    gw = width // len(B_WINDOWS)
    tm = _rows(m, 512)
    n_i = m // tm

    def body(dm_ref, dn_ref, dz_ref):
        i = pl.program_id(0)
        de = jnp.concatenate([dm_ref[...], dn_ref[...] * (i < n_i - 1).astype(F32)], axis=0)
        pos = i * tm + lax.broadcasted_iota(jnp.int32, (tm + POOL_HALO, 1), 0)
        for gi, win in enumerate(B_WINDOWS):
            cs = slice(gi * gw, (gi + 1) * gw)
            d = de[:, cs]
            s = d * (1.0 / jnp.minimum(pos + 1, win).astype(F32))
            step = 1
            while step < win:
                s = s + _up(s, step)
                step *= 2
            dz_ref[:, cs] = (s[:tm] - d[:tm]).astype(BF16)

    in_specs = [pl.BlockSpec((tm, width), lambda i: (i, 0)),
                pl.BlockSpec((POOL_HALO, width), lambda i: (_next_halo(tm, m, POOL_HALO)(i), 0))]
    return _call(body, name, (n_i,), in_specs, pl.BlockSpec((tm, width), lambda i: (i, 0)), _sds((m, width), BF16))(dp, dp)


def _c_gates(xr, wa_ref, ba_ref, wi_ref, bi_ref, lam_ref, heads, hw):
    xb = xr.astype(BF16)
    ra = jnp.concatenate([_dot(xb[:, h * hw:(h + 1) * hw], wa_ref[h]) for h in range(heads)], axis=1) + ba_ref[...]
    ia = jnp.concatenate([_dot(xb[:, h * hw:(h + 1) * hw], wi_ref[h]) for h in range(heads)], axis=1) + bi_ref[...]
    r, ig = _sigmoid(ra), _sigmoid(ia)
    sp = _softplus(-lam_ref[...])
    log_a = (-C_GATE_C * r) * sp
    a = jnp.exp(log_a)
    mult = jnp.sqrt(-_expm1(2.0 * log_a))
    return xb, r, ig, sp, a, mult


def c_mid_fwd(z, cw, cb, wa, ba, wi, bi, lam, name):
    m, w2 = z.shape
    width = w2 // 2
    heads, hw = wa.shape[0], wa.shape[1]
    taps = cw.shape[0]
    tm = _rows(m, 512)

    def body(zp_ref, zm_ref, cw_ref, cb_ref, wa_ref, ba_ref, wi_ref, bi_ref, lam_ref, a_ref, b_ref, xr_ref):
        i = pl.program_id(0)
        xe = jnp.concatenate([zp_ref[...] * (i > 0).astype(F32), zm_ref[...]], axis=0)
        xr = _conv_ext(xe, cw_ref[...], taps)[HALO:] + cb_ref[...]
        _, _, ig, _, a, mult = _c_gates(xr, wa_ref, ba_ref, wi_ref, bi_ref, lam_ref, heads, hw)
        a_ref[...] = a
        b_ref[...] = mult * (ig * xr)
        xr_ref[...] = xr

    vec = pl.BlockSpec((1, width), lambda i: (0, 0))
    row = pl.BlockSpec((tm, width), lambda i: (i, 0))
    wsp = pl.BlockSpec((heads, hw, hw), lambda i: (0, 0, 0))
    in_specs = [pl.BlockSpec((HALO, width), lambda i: (_prev_halo(tm)(i), 1)), pl.BlockSpec((tm, width), lambda i: (i, 1)),
                pl.BlockSpec((taps, width), lambda i: (0, 0)), vec, wsp, vec, wsp, vec, vec]
    return _call(body, name, (m // tm,), in_specs, [row, row, row], [_sds((m, width))] * 3)(
        z, z, cw, cb, wa, ba, wi, bi, lam)


_SCAN_ROWS = 512


def c_scan_fwd(a, b, z, name):
    m, width = a.shape
    tm = _rows(m, _SCAN_ROWS)

    def body(a_ref, b_ref, g_ref, hs_ref, y_ref, h_carry):
        @pl.when(pl.program_id(0) == 0)
        def _():
            h_carry[...] = jnp.zeros_like(h_carry)

        def step(t, h):
            h = a_ref[pl.ds(t, 1), :] * h + b_ref[pl.ds(t, 1), :]
            hs_ref[pl.ds(t, 1), :] = h
            return h

        h_carry[...] = lax.fori_loop(0, tm, step, h_carry[...], unroll=8)
        y_ref[...] = (hs_ref[...] * _gelu(g_ref[...])).astype(BF16)

    row = pl.BlockSpec((tm, width), lambda i: (i, 0))
    return _call(body, name, (m // tm,), [row, row, row], [row, row], [_sds((m, width)), _sds((m, width), BF16)],
                 [pltpu.VMEM((1, width), F32)])(a, b, z)


def c_scan_bwd(dy, z, hs, a, name):
    m, width = a.shape
    tm = _rows(m, _SCAN_ROWS)
    n_i = m // tm

    def body(dy_ref, g_ref, hs_ref, hp_ref, a_ref, lam_ref, da_ref, dg_ref, dgs_ref, lam_carry, a_carry):
        i = pl.program_id(0)
        first = i == 0

        @pl.when(first)
        def _():
            lam_carry[...] = jnp.zeros_like(lam_carry)
            a_carry[...] = jnp.zeros_like(a_carry)

        gp, dyv, hsv = g_ref[...], dy_ref[...], hs_ref[...]
        dgate = dyv * hsv * _gelu_grad(gp)
        dg_ref[...] = dgate.astype(BF16)
        _acc(dgs_ref, _colsum(dgate), first)
        lam_ref[...] = dyv * _gelu(gp)

        def step(k, carry):
            lam_next, a_next = carry
            t = tm - 1 - k
            lam_t = lam_ref[pl.ds(t, 1), :] + a_next * lam_next
            lam_ref[pl.ds(t, 1), :] = lam_t
            return lam_t, a_ref[pl.ds(t, 1), :]

        lam_c, a_c = lax.fori_loop(0, tm, step, (lam_carry[...], a_carry[...]), unroll=8)
        lam_carry[...] = lam_c
        a_carry[...] = a_c
        h_before = hp_ref[HALO - 1:HALO, :] * (i < n_i - 1).astype(F32)
        t_i = lax.broadcasted_iota(jnp.int32, (tm, 1), 0)
        da_ref[...] = lam_ref[...] * jnp.where(t_i == 0, h_before, _down(hsv, 1))

    row = pl.BlockSpec((tm, width), lambda i: (n_i - 1 - i, 0))
    halo = pl.BlockSpec((HALO, width), lambda i: (_prev_halo(tm)(n_i - 1 - i), 0))
    vec = pl.BlockSpec((1, width), lambda i: (0, 0))
    return _call(body, name, (n_i,), [row, row, row, halo, row], [row, row, row, vec],
                 [_sds((m, width)), _sds((m, width)), _sds((m, width), BF16), _sds((1, width))],
                 [pltpu.VMEM((1, width), F32), pltpu.VMEM((1, width), F32)])(dy, z, hs, hs, a)


def c_mid_bwd(lam_seq, da, xr, wa, ba, wi, bi, lam, name):
    m, width = xr.shape
    heads, hw = wa.shape[0], wa.shape[1]
    tm = _rows(m, 512)

    def body(l_ref, da_ref, xr_ref, wa_ref, ba_ref, wi_ref, bi_ref, lam_ref,
             dxr_ref, dwa_ref, dwi_ref, dba_ref, dbi_ref, dlam_ref):
        first = pl.program_id(0) == 0
        xr_v, lmb = xr_ref[...], l_ref[...]
        xb, r, ig, sp, a, mult = _c_gates(xr_v, wa_ref, ba_ref, wi_ref, bi_ref, lam_ref, heads, hw)
        dmult = lmb * (ig * xr_v)
        dig = lmb * mult * xr_v
        dxr = lmb * mult * ig
        dla = da_ref[...] * a - dmult * (a * a) / mult
        dr = dla * (-C_GATE_C * sp)
        dsp = _colsum(dla * (-C_GATE_C * r))
        _acc(dlam_ref, dsp * (-_sigmoid(-lam_ref[...])), first)
        dra = dr * r * (1.0 - r)
        dia = dig * ig * (1.0 - ig)
        _acc(dba_ref, _colsum(dra), first)
        _acc(dbi_ref, _colsum(dia), first)
        for h in range(heads):
            cs = slice(h * hw, (h + 1) * hw)
            _acc(dwa_ref.at[h], _dot(xb[:, cs], dra[:, cs], _TN), first)
            _acc(dwi_ref.at[h], _dot(xb[:, cs], dia[:, cs], _TN), first)
            dxr_ref[:, cs] = dxr[:, cs] + _dot(dra[:, cs], wa_ref[h], _NT) + _dot(dia[:, cs], wi_ref[h], _NT)

    vec = pl.BlockSpec((1, width), lambda i: (0, 0))
    row = pl.BlockSpec((tm, width), lambda i: (i, 0))
    wsp = pl.BlockSpec((heads, hw, hw), lambda i: (0, 0, 0))
    return _call(body, name, (m // tm,), [row, row, row, wsp, vec, wsp, vec, vec], [row, wsp, wsp, vec, vec, vec],
                 [_sds((m, width)), _sds((heads, hw, hw)), _sds((heads, hw, hw)), _sds((1, width)), _sds((1, width)),
                  _sds((1, width))])(lam_seq, da, xr, wa, ba, wi, bi, lam)


def conv_bwd(dy, x_src, col_block, cw, name):
    m, width = dy.shape
    taps = cw.shape[0]
    tm = _rows(m, 512)
    n_i = m // tm

    def body(dm_ref, dn_ref, xp_ref, xm_ref, cw_ref, dx_ref, dw_ref, db_ref, dxs_ref):
        i = pl.program_id(0)
        first = i == 0
        de = jnp.concatenate([jnp.zeros((HALO, width), F32), dm_ref[...], dn_ref[...] * (i < n_i - 1).astype(F32)], axis=0)
        xe = jnp.concatenate([xp_ref[...] * (i > 0).astype(F32), xm_ref[...], jnp.zeros((HALO, width), F32)], axis=0)
        w = cw_ref[...]
        dx = de * w[taps - 1:taps]
        for s in range(1, taps):
            dx = dx + _up(de, s) * w[taps - 1 - s:taps - s]
        dx_ref[...] = dx[HALO:HALO + tm].astype(BF16)
        _acc(dxs_ref, _colsum(dx[HALO:HALO + tm]), first)
        dm = dm_ref[...]
        for s in range(taps):
            _acc(dw_ref.at[taps - 1 - s:taps - s, :], _colsum(dm * _down(xe, s)[HALO:HALO + tm]), first)
        _acc(db_ref, _colsum(dm), first)

    row = pl.BlockSpec((tm, width), lambda i: (i, 0))
    vec = pl.BlockSpec((1, width), lambda i: (0, 0))
    tsp = pl.BlockSpec((taps, width), lambda i: (0, 0))
    in_specs = [row, pl.BlockSpec((HALO, width), lambda i: (_next_halo(tm, m)(i), 0)),
                pl.BlockSpec((HALO, width), lambda i: (_prev_halo(tm)(i), col_block)),
                pl.BlockSpec((tm, width), lambda i: (i, col_block)), tsp]
    return _call(body, name, (n_i,), in_specs, [row, tsp, vec, vec],
                 [_sds((m, width), BF16), _sds((taps, width)), _sds((1, width)), _sds((1, width))])(dy, dy, x_src, x_src, cw)


def d_mid_fwd(z, cw, name):
    m, w3 = z.shape
    width = w3 // 3
    taps = cw.shape[0]
    tm = _rows(m, 512)

    def body(bm_ref, cp_ref, cm_ref, xp_ref, xm_ref, cw_ref, y_ref):
        keep = (pl.program_id(0) > 0).astype(F32)
        qe = (jnp.concatenate([cp_ref[...] * keep, cm_ref[...]], axis=0)
              * jnp.concatenate([xp_ref[...], xm_ref[...]], axis=0))
        y_ref[...] = (bm_ref[...] * _conv_ext(qe, cw_ref[...], taps)[HALO:]).astype(BF16)

    main = lambda c: pl.BlockSpec((tm, width), lambda i: (i, c))
    prev = lambda c: pl.BlockSpec((HALO, width), lambda i: (_prev_halo(tm)(i), c))
    in_specs = [main(0), prev(1), main(1), prev(2), main(2), pl.BlockSpec((taps, width), lambda i: (0, 0))]
    return _call(body, name, (m // tm,), in_specs, pl.BlockSpec((tm, width), lambda i: (i, 0)),
                 _sds((m, width), BF16))(z, z, z, z, z, cw)


def d_mid_bwd(z, dy, cw, name):
    m, w3 = z.shape
    width = w3 // 3
    taps = cw.shape[0]
    tm = _rows(m, 512)
    n_i = m // tm

    def body(bm_ref, bn_ref, cp_ref, cm_ref, cn_ref, xp_ref, xm_ref, xn_ref, dm_ref, dn_ref, cw_ref, dz_ref, dw_ref):
        i = pl.program_id(0)
        first = i == 0
        kp, kn = (i > 0).astype(F32), (i < n_i - 1).astype(F32)
        zeros = jnp.zeros((HALO, width), F32)
        ce = jnp.concatenate([cp_ref[...] * kp, cm_ref[...], cn_ref[...] * kn], axis=0)
        xe = jnp.concatenate([xp_ref[...], xm_ref[...], xn_ref[...]], axis=0)
        qe = ce * xe
        be = jnp.concatenate([zeros, bm_ref[...], bn_ref[...]], axis=0)
        dye = jnp.concatenate([zeros, dm_ref[...], dn_ref[...] * kn], axis=0)
        w = cw_ref[...]
        cq = _conv_ext(qe, w, taps)
        dcq = dye * be
        dq = dcq * w[taps - 1:taps]
        for s in range(1, taps):
            dq = dq + _up(dcq, s) * w[taps - 1 - s:taps - s]
        ms = slice(HALO, HALO + tm)
        dz_ref[:, :width] = (dye * cq)[ms].astype(BF16)
        dz_ref[:, width:2 * width] = (dq * xe)[ms].astype(BF16)
        dz_ref[:, 2 * width:] = (dq * ce)[ms].astype(BF16)
        for s in range(taps):
            _acc(dw_ref.at[taps - 1 - s:taps - s, :], _colsum(dcq[ms] * _down(qe, s)[ms]), first)

    main = lambda c: pl.BlockSpec((tm, width), lambda i: (i, c))
    prev = lambda c: pl.BlockSpec((HALO, width), lambda i: (_prev_halo(tm)(i), c))
    nxt = lambda c: pl.BlockSpec((HALO, width), lambda i: (_next_halo(tm, m)(i), c))
    tsp = pl.BlockSpec((taps, width), lambda i: (0, 0))
    in_specs = [main(0), nxt(0), prev(1), main(1), nxt(1), prev(2), main(2), nxt(2), main(0), nxt(0), tsp]
    return _call(body, name, (n_i,), in_specs, [pl.BlockSpec((tm, w3), lambda i: (i, 0)), tsp],
                 [_sds((m, w3), BF16), _sds((taps, width))])(z, z, z, z, z, z, z, z, dy, dy, cw)


def _halo_rows(dtype):
    return HALO * (4 // jnp.dtype(dtype).itemsize)


def ffn_gate_down(z, cw, cb, w_down, res, next_g, name):
    _, nj, m, c = z.shape
    n = w_down.shape[1]
    taps = cw.shape[2]
    tm = _rows(m, 256)
    hz = _halo_rows(z.dtype)

    def body(zp_ref, zm_ref, cw_ref, cb_ref, w_ref, res_ref, *rest):
        keep = (pl.program_id(0) > 0).astype(F32)
        zc_ref, act_ref, x_ref = rest[-4:-1] if next_g is not None else rest[-3:]
        acc = res_ref[...]
        for j in range(nj):
            zc = []
            for s in range(2):
                xe = jnp.concatenate([zp_ref[s, j].astype(F32) * keep, zm_ref[s, j].astype(F32)], axis=0)
                zc.append(_conv_ext(xe, cw_ref[s, j], taps)[hz:] + cb_ref[s, j])
                zc_ref[s, j] = zc[s].astype(BF16)
            act = (zc[0] * _sigmoid(zc[0]) * zc[1]).astype(BF16)
            act_ref[j] = act
            acc = acc + _dot(act, w_ref[j])
        x_ref[...] = acc
        if next_g is not None:
            rest[-1][...] = _norm_rows(acc, rest[0][...])

    row = pl.BlockSpec((tm, n), lambda i: (i, 0))
    in_specs = [pl.BlockSpec((2, nj, hz, c), lambda i: (0, 0, _prev_halo(tm, hz)(i), 0)),
                pl.BlockSpec((2, nj, tm, c), lambda i: (0, 0, i, 0)),
                _resident(cw.shape), _resident(cb.shape), _resident((nj, c, n)), row]
    out_specs = [pl.BlockSpec((2, nj, tm, c), lambda i: (0, 0, i, 0)), pl.BlockSpec((nj, tm, c), lambda i: (0, i, 0)), row]
    out_shape = [_sds((2, nj, m, c), BF16), _sds((nj, m, c), BF16), _sds((m, n))]
    ins = [z, z, cw, cb, w_down.reshape(nj, c, n), res]
    if next_g is not None:
        in_specs.append(pl.BlockSpec((1, n), lambda i: (0, 0)))
        out_specs.append(row)
        out_shape.append(_sds((m, n), BF16))
        ins.append(next_g)
    return _call(body, name, (m // tm,), in_specs, out_specs, out_shape)(*ins)


def ffn_gate_bwd(z, zc, dx, w_down, cw, name, after=()):
    _, nj, m, c = z.shape
    n = w_down.shape[1]
    taps = cw.shape[2]
    tm = _rows(m, 512)
    n_i = m // tm
    hz = _halo_rows(z.dtype)
    assert _halo_rows(dx.dtype) == hz and zc.dtype == z.dtype, (z.dtype, zc.dtype, dx.dtype)

    def body(zp_ref, zm_ref, zn_ref, cm_ref, cn_ref, dm_ref, dn_ref, wd_ref, cw_ref, dz_ref, dw_ref, db_ref):
        i = pl.program_id(1)
        first = i == 0
        kp, kn = (i > 0).astype(F32), (i < n_i - 1).astype(F32)
        xe = [jnp.concatenate([zp_ref[s].astype(F32) * kp, zm_ref[s].astype(F32), zn_ref[s].astype(F32) * kn], axis=0)
              for s in range(2)]
        zc = [jnp.concatenate([jnp.zeros((hz, c), F32), cm_ref[s].astype(F32), cn_ref[s].astype(F32)], axis=0)
              for s in range(2)]
        dact = _dot(jnp.concatenate([dm_ref[...], dn_ref[...]], axis=0), wd_ref[...], _NT)
        dae = jnp.concatenate([jnp.zeros((hz, c), F32), dact[:tm], dact[tm:] * kn], axis=0)
        sg = _sigmoid(zc[0])
        dzc = [dae * zc[1] * (sg * (1.0 + zc[0] * (1.0 - sg))), dae * (zc[0] * sg)]
        ms = slice(hz, hz + tm)
        for s in range(2):
            w = cw_ref[s]
            ups = [dzc[s]] + [_up(dzc[s], u) for u in range(1, taps)]
            dxs = ups[0] * w[taps - 1:taps]
            for u in range(1, taps):
                dxs = dxs + ups[u] * w[taps - 1 - u:taps - u]
            dz_ref[s] = dxs[ms].astype(BF16)
            tail = dzc[s][hz + tm:]
            x_end = xe[s][tm:]
            for u in range(taps):
                total = _colsum(ups[u] * xe[s]) - _colsum(tail * _down(x_end, u)[hz:])
                _acc(dw_ref.at[s, taps - 1 - u:taps - u, :], total, first)
            _acc(db_ref.at[s], _colsum(dzc[s][ms]), first)

    in_specs = [pl.BlockSpec((2, None, hz, c), lambda j, i: (0, j, _prev_halo(tm, hz)(i), 0)),
                pl.BlockSpec((2, None, tm, c), lambda j, i: (0, j, i, 0)),
                pl.BlockSpec((2, None, hz, c), lambda j, i: (0, j, _next_halo(tm, m, hz)(i), 0)),
                pl.BlockSpec((2, None, tm, c), lambda j, i: (0, j, i, 0)),
                pl.BlockSpec((2, None, hz, c), lambda j, i: (0, j, _next_halo(tm, m, hz)(i), 0)),
                pl.BlockSpec((tm, n), lambda j, i: (i, 0)),
                pl.BlockSpec((hz, n), lambda j, i: (_next_halo(tm, m, hz)(i), 0)),
                pl.BlockSpec((None, c, n), lambda j, i: (j, 0, 0)),
                pl.BlockSpec((2, None, taps, c), lambda j, i: (0, j, 0, 0))]
    out_specs = [pl.BlockSpec((2, None, tm, c), lambda j, i: (0, j, i, 0)),
                 pl.BlockSpec((2, None, taps, c), lambda j, i: (0, j, 0, 0)),
                 pl.BlockSpec((2, None, 1, c), lambda j, i: (0, j, 0, 0))]
    return _call(body, name, (nj, n_i), in_specs, out_specs,
                 [_sds((2, nj, m, c), BF16), _sds((2, nj, taps, c)), _sds((2, nj, 1, c))], (), after)(
        z, z, z, zc, zc, dx, dx, w_down.reshape(nj, c, n), cw)


_STREAM_TILE_BYTES = 2 << 20


def _tile_rows(r, c):
    if r * c * 4 <= _STREAM_TILE_BYTES:
        return r
    fits = [d for d in range(16, r, 16) if r % d == 0 and d * c * 4 <= _STREAM_TILE_BYTES]
    return max(fits) if fits else r


def _as2d(a, lead):
    shape = a.shape
    return a.reshape((lead, -1, shape[-1]) if lead else (-1, shape[-1]))


def add_pairs(g, l1, own, name):
    shape = l1.shape
    g3, l3 = _as2d(g, N_DEV), _as2d(l1, 4)
    _, r, c = l3.shape
    tr = _tile_rows(r, c)

    def body(own_ref, a_ref, b_ref, o_ref):
        o_ref[...] = (a_ref[...].astype(F32) + b_ref[...].astype(F32)).astype(o_ref.dtype)

    spec = pl.BlockSpec((None, tr, c), lambda k, i, own_ref: (k, i, 0))
    grid_spec = pltpu.PrefetchScalarGridSpec(
        num_scalar_prefetch=1, grid=(4, r // tr),
        in_specs=[pl.BlockSpec((None, tr, c), lambda k, i, own_ref: (own_ref[k], i, 0)), spec], out_specs=spec)
    out = pl.pallas_call(
        body, name=name, grid_spec=grid_spec, out_shape=_sds(l3.shape, l1.dtype),
        compiler_params=pltpu.CompilerParams(dimension_semantics=("arbitrary", "arbitrary"),
                                             vmem_limit_bytes=V7X_VMEM_LIMIT_BYTES))(own, g3, l3)
    return out.reshape(shape)


def _grad_sum(p_ref, l_ref):
    return ((p_ref[...].astype(F32) + l_ref[0].astype(F32)) + l_ref[1].astype(F32)) + l_ref[2].astype(F32)


def sum_parts(p, l2, name):
    _, r, c = p.shape

    def body(p_ref, l_ref, o_ref):
        o_ref[...] = _grad_sum(p_ref, l_ref)

    return _call(body, name, (1,), [pl.BlockSpec((None, r, c), lambda i: (0, 0, 0)), pl.BlockSpec((3, r, c), lambda i: (0, 0, 0))],
                 pl.BlockSpec((r, c), lambda i: (0, 0)), _sds((r, c)))(p, l2)


def _adamw_math(w, g, m, v):
    m = ADAM_B1 * m + (1.0 - ADAM_B1) * g
    v = ADAM_B2 * v + (1.0 - ADAM_B2) * (g * g)
    m_hat = m / (1.0 - ADAM_B1 ** ADAM_STEP)
    v_hat = v / (1.0 - ADAM_B2 ** ADAM_STEP)
    delta = -ADAM_LR * (m_hat / (jnp.sqrt(v_hat) + ADAM_EPS) + ADAM_WD * w)
    return delta, m, v


def adamw(w, m, v, name, g=None, p=None, l2=None):
    shape = w.shape
    w2, m2, v2 = (_as2d(t, 0) for t in (w, m, v))
    r, c = w2.shape
    tr = _tile_rows(r, c)
    row = pl.BlockSpec((tr, c), lambda i: (i, 0))
    if g is None:
        p3, l3 = _as2d(p, 4), _as2d(l2, 3)
        gin = (p3, l3)
        gspecs = [pl.BlockSpec((None, tr, c), lambda i: (0, i, 0)), pl.BlockSpec((3, tr, c), lambda i: (0, i, 0))]
    else:
        gin, gspecs = (_as2d(g, 0),), [row]

    def body(*refs):
        n_g = len(gin)
        w_ref, m_ref, v_ref, g_ref, d_ref, nm_ref, nv_ref = refs[n_g:]
        grad = refs[0][...] if n_g == 1 else _grad_sum(refs[0], refs[1])
        delta, nm, nv = _adamw_math(w_ref[...], grad, m_ref[...], v_ref[...])
        g_ref[...] = grad
        d_ref[...] = delta
        nm_ref[...] = nm
        nv_ref[...] = nv

    outs = _call(body, name, (r // tr,), gspecs + [row, row, row], [row] * 4, [_sds((r, c))] * 4)(*gin, w2, m2, v2)
    return tuple(o.reshape(shape) for o in outs)


def adamw_layer(w, m, v, layer, prev, name, p, l2):
    n_l, r, c = w.shape
    tr = _tile_rows(r, c)
    slab = pl.BlockSpec((None, tr, c), lambda i: (layer, i, 0))
    in_specs = [pl.BlockSpec((None, tr, c), lambda i: (0, i, 0)), pl.BlockSpec((3, tr, c), lambda i: (0, i, 0)), slab, slab, slab]
    n_in = len(in_specs)
    prev = () if prev is None else tuple(prev)

    def body(p_ref, l_ref, w_ref, m_ref, v_ref, *rest):
        g_ref, d_ref, nm_ref, nv_ref = rest[len(prev):]
        grad = _grad_sum(p_ref, l_ref)
        delta, nm, nv = _adamw_math(w_ref[...], grad, m_ref[...], v_ref[...])
        g_ref[...] = grad
        d_ref[...] = delta
        nm_ref[...] = nm
        nv_ref[...] = nv

    return pl.pallas_call(
        body, name=name, grid=(r // tr,), in_specs=in_specs + [pl.BlockSpec(memory_space=pl.ANY)] * len(prev),
        out_specs=[slab] * 4, out_shape=[_sds((n_l, r, c))] * 4,
        input_output_aliases={n_in + q: q for q in range(len(prev))},
        compiler_params=pltpu.CompilerParams(dimension_semantics=("arbitrary",), vmem_limit_bytes=V7X_VMEM_LIMIT_BYTES),
    )(_as2d(p, 4), _as2d(l2, 3), w, m, v, *prev)


def _comm_call(body, name, ins, out_shape, n_sems):
    any_spec = pl.BlockSpec(memory_space=pl.ANY)
    return pl.pallas_call(
        body, name=name, in_specs=[any_spec] * len(ins), out_specs=[any_spec] * len(out_shape), out_shape=out_shape,
        scratch_shapes=[pltpu.SemaphoreType.DMA((n,)) for n in n_sems],
        compiler_params=pltpu.CompilerParams(has_side_effects=True))(*ins)


def _place():
    return lax.axis_index("x"), lax.axis_index("y"), lax.axis_index("c")


def _dev_index(px, py, pc):
    return 4 * px + 2 * py + pc


def all_gather(blocks, name):
    n_t = len(blocks)

    def body(*refs):
        ins, outs = refs[:n_t], refs[n_t:2 * n_t]
        send_sems, recv_sems, local_sems = refs[2 * n_t:]
        x, y, c = _place()
        me, sibling = (x, y, c), (x, y, 1 - c)
        chips = [(1 - x, y), (x, 1 - y), (1 - x, 1 - y)]

        def copy(t, k, block, to, src=None):
            dst = outs[t].at[_dev_index(*block)]
            return pltpu.make_async_remote_copy(
                src_ref=dst if src is None else src, dst_ref=dst, send_sem=send_sems.at[t * 7 + k],
                recv_sem=recv_sems.at[t * 7 + k], device_id=to, device_id_type=MESH_ID)

        mine = [pltpu.make_async_copy(ins[t], outs[t].at[_dev_index(*me)], local_sems.at[t]) for t in range(n_t)]
        for cp in mine:
            cp.start()
        first = []
        for t in range(n_t):
            first.append(copy(t, 0, me, sibling, src=ins[t]))
            first += [copy(t, 1 + j, me, (*chip, c), src=ins[t]) for j, chip in enumerate(chips)]
        for cp in first:
            cp.start()
        passed = []
        for t in range(n_t):
            for j, chip in enumerate(chips):
                copy(t, 1 + j, (*chip, c), me).wait_recv()
                cp = copy(t, 4 + j, (*chip, c), sibling)
                cp.start()
                passed.append(cp)
        for t in range(n_t):
            copy(t, 0, sibling, me).wait_recv()
            for j, chip in enumerate(chips):
                copy(t, 4 + j, (*chip, 1 - c), me).wait_recv()
        for cp in first + passed:
            cp.wait_send()
        for cp in mine:
            cp.wait()

    out_shape = [_sds((N_DEV,) + b.shape, b.dtype) for b in blocks]
    return _comm_call(body, name, blocks, out_shape, (7 * n_t, 7 * n_t, n_t))


def _chip_of(x, y, k):
    return (x if k % 2 == 0 else 1 - x), (y if k // 2 == 0 else 1 - y)


_HBM_SPEC = pl.BlockSpec(memory_space=pltpu.HBM)
_SEM_SPEC = pl.BlockSpec(memory_space=pltpu.SEMAPHORE)
_DATAFLOW = pltpu.SideEffectType.DATAFLOW_SIDE_EFFECTING


def _in_hbm(a):
    return pltpu.with_memory_space_constraint(a, pltpu.HBM)


def _split_start(name, issue, srcs, land_shapes, sem_counts, after=()):
    n_buf, n_sem, n_after = len(srcs) + len(land_shapes), len(sem_counts), len(after)

    def body(*refs):
        issue(refs[:len(srcs)], refs[len(srcs):n_buf], refs[n_buf + n_after:n_buf + n_after + n_sem])
        refs[-1][...] = jnp.zeros_like(refs[-1])

    bufs = [pltpu.HBM(s.shape, s.dtype) for s in list(srcs) + list(land_shapes)]
    outs = pl.pallas_call(
        body, name=name, in_specs=(*[_HBM_SPEC] * n_buf, *[pl.BlockSpec(memory_space=pl.ANY)] * n_after),
        out_shape=(*[pltpu.SemaphoreType.DMA((n,)) for n in sem_counts], *bufs, _sds((8, LANES))),
        out_specs=(*[_SEM_SPEC] * n_sem, *[_HBM_SPEC] * n_buf, pl.BlockSpec(memory_space=pltpu.VMEM)),
        input_output_aliases={i: n_sem + i for i in range(n_buf)},
        compiler_params=pltpu.CompilerParams(has_side_effects=_DATAFLOW),
    )(*[_in_hbm(s) for s in srcs], *[_in_hbm(lax.empty(s.shape, s.dtype)) for s in land_shapes], *after)
    return outs[:n_sem], outs[n_sem:n_sem + len(srcs)], outs[n_sem + len(srcs):n_sem + n_buf], outs[-1]


def _split_wait(name, finish, sems, srcs, lands, after):
    n_buf, n_sem = len(srcs) + len(lands), len(sems)

    def body(*refs):
        finish(refs[:len(srcs)], refs[len(srcs):n_buf], refs[n_buf:n_buf + n_sem])

    bufs = [pltpu.HBM(s.shape, s.dtype) for s in list(srcs) + list(lands)]
    outs = pl.pallas_call(
        body, name=name, in_specs=(*[_HBM_SPEC] * n_buf, *[_SEM_SPEC] * n_sem, *[pl.BlockSpec(memory_space=pl.ANY)] * len(after)),
        out_shape=tuple(bufs), out_specs=(_HBM_SPEC,) * n_buf, input_output_aliases={i: i for i in range(n_buf)},
        compiler_params=pltpu.CompilerParams(has_side_effects=_DATAFLOW),
    )(*srcs, *lands, *sems, *after)
    return outs[:len(srcs)], outs[len(srcs):]


def _peer(x, y, c, r):
    return (1 - x if r & 4 else x), (1 - y if r & 2 else y), (1 - c if r & 1 else c)


ALL_PEERS = tuple(range(1, N_DEV))
SAME_CORE_PEERS = (2, 4, 6)


def _gather_copies(src_refs, land_refs, sem_refs, arrivals, peers):
    send_sems, recv_sems, local_sems = sem_refs
    x, y, c = _place()
    me = _dev_index(x, y, c)
    local, sends, recvs = [], [], []
    for j, (src, land) in enumerate(zip(src_refs, land_refs)):
        local.append(pltpu.make_async_copy(src, land.at[me], local_sems.at[j]))
        for p, r in enumerate(peers):
            peer = _peer(x, y, c, r)
            q = len(peers) * j + p
            sends.append(pltpu.make_async_remote_copy(src_ref=src, dst_ref=land.at[me], send_sem=send_sems.at[q],
                                                      recv_sem=recv_sems.at[q], device_id=peer, device_id_type=MESH_ID))
            if arrivals:
                recvs.append(pltpu.make_async_remote_copy(
                    src_ref=src, dst_ref=land.at[_dev_index(*peer)], send_sem=send_sems.at[q], recv_sem=recv_sems.at[q],
                    device_id=peer, device_id_type=MESH_ID))
    return local, sends, recvs


def gather_start(groups, peers, name, after=()):
    flat = [b for g in groups for b in g]
    bounds = [sum(len(g) for g in groups[:i]) for i in range(len(groups) + 1)]

    def issue(src_refs, land_refs, sem_refs):
        for i in range(len(groups)):
            lo, hi = bounds[i], bounds[i + 1]
            local, sends, _ = _gather_copies(src_refs[lo:hi], land_refs[lo:hi], sem_refs[3 * i:3 * i + 3], False, peers[i])
            for cp in local + sends:
                cp.start()

    sem_counts = [n for g, p in zip(groups, peers) for n in (len(p) * len(g), len(p) * len(g), len(g))]
    sems, srcs, lands, _ = _split_start(name, issue, flat, [_sds((N_DEV,) + b.shape, b.dtype) for b in flat], sem_counts, after)
    return [(sems[3 * i:3 * i + 3], srcs[bounds[i]:bounds[i + 1]], lands[bounds[i]:bounds[i + 1]], peers[i])
            for i in range(len(groups))]


def gather_wait(group, after, name):
    sems, srcs, lands, peers = group

    def finish(src_refs, land_refs, sem_refs):
        local, sends, recvs = _gather_copies(src_refs, land_refs, sem_refs, True, peers)
        for cp in local:
            cp.wait()
        for cp in recvs:
            cp.wait_recv()
        for cp in sends:
            cp.wait_send()

    return _split_wait(name, finish, sems, srcs, lands, after)[1]


def _forward_copies(land_refs, sem_refs, arrivals):
    send_sems, recv_sems = sem_refs
    x, y, c = _place()
    sends, recvs = [], []
    for t, land in enumerate(land_refs):
        for k in range(4):
            cx, cy = _chip_of(x, y, k)
            mine, theirs = land.at[_dev_index(cx, cy, c)], land.at[_dev_index(cx, cy, 1 - c)]
            sems = dict(send_sem=send_sems.at[4 * t + k], recv_sem=recv_sems.at[4 * t + k], device_id=(x, y, 1 - c),
                        device_id_type=MESH_ID)
            sends.append(pltpu.make_async_remote_copy(src_ref=mine, dst_ref=mine, **sems))
            if arrivals:
                recvs.append(pltpu.make_async_remote_copy(src_ref=theirs, dst_ref=theirs, **sems))
    return sends, recvs


def forward_start(lands, name):
    def issue(land_refs, _, sem_refs):
        for cp in _forward_copies(land_refs, sem_refs, False)[0]:
            cp.start()

    n = 4 * len(lands)
    sems, lands, _, token = _split_start(name, issue, lands, [], (n, n))
    return sems, lands, token


def forward_wait(started, after, name):
    sems, lands, _ = started

    def finish(land_refs, _, sem_refs):
        sends, recvs = _forward_copies(land_refs, sem_refs, True)
        for cp in recvs:
            cp.wait_recv()
        for cp in sends:
            cp.wait_send()

    return _split_wait(name, finish, sems, lands, [], after)[0]


def _sibling_copies(src_refs, land_refs, sem_refs):
    send_sems, recv_sems = sem_refs
    x, y, c = _place()
    copies = []
    for t, (src, land) in enumerate(zip(src_refs, land_refs)):
        for k in range(4):
            cx, cy = _chip_of(x, y, k)
            copies.append(pltpu.make_async_remote_copy(
                src_ref=src.at[_dev_index(cx, cy, 1 - c)], dst_ref=land.at[k], send_sem=send_sems.at[4 * t + k],
                recv_sem=recv_sems.at[4 * t + k], device_id=(x, y, 1 - c), device_id_type=MESH_ID))
    return copies


def _chip_copies(src_refs, land_refs, sem_refs):
    send_sems, recv_sems = sem_refs
    x, y, c = _place()
    copies = []
    for t, (src, land) in enumerate(zip(src_refs, land_refs)):
        for k in range(1, 4):
            cx, cy = _chip_of(x, y, k)
            copies.append(pltpu.make_async_remote_copy(
                src_ref=src.at[k], dst_ref=land.at[k - 1], send_sem=send_sems.at[3 * t + k - 1],
                recv_sem=recv_sems.at[3 * t + k - 1], device_id=(cx, cy, c), device_id_type=MESH_ID))
    return copies


def _exchange_start(copies_of, n_land, per_array, arrays, name):
    def issue(src_refs, land_refs, sem_refs):
        for cp in copies_of(src_refs, land_refs, sem_refs):
            cp.start()

    n = per_array * len(arrays)
    lands = [_sds((n_land,) + a.shape[1:], a.dtype) for a in arrays]
    return _split_start(name, issue, arrays, lands, (n, n))


def _exchange_wait(copies_of, started, after, name):
    sems, srcs, lands, _ = started

    def finish(src_refs, land_refs, sem_refs):
        copies = copies_of(src_refs, land_refs, sem_refs)
        for cp in copies:
            cp.wait_recv()
        for cp in copies:
            cp.wait_send()

    return _split_wait(name, finish, sems, srcs, lands, after)


def sibling_start(grads, name):
    return _exchange_start(_sibling_copies, 4, 4, grads, name)


def sibling_wait(started, after, name):
    return _exchange_wait(_sibling_copies, started, after, name)


def chips_start(parts, name):
    return _exchange_start(_chip_copies, 3, 3, parts, name)


def chips_wait(started, after, name):
    return _exchange_wait(_chip_copies, started, after, name)


def _pack(arrays, rows):
    flat = jnp.concatenate([a.reshape(-1) for a in arrays])
    return jnp.pad(flat, (0, rows * LANES - flat.shape[0])).reshape(rows, LANES)


def _pack_stacked(arrays, rows):
    flat = jnp.concatenate([a.reshape(N_DEV, -1) for a in arrays], axis=1)
    return jnp.pad(flat, ((0, 0), (0, rows * LANES - flat.shape[1]))).reshape(N_DEV, rows, LANES)


def _unpack(buf, shapes, lead=()):
    flat = buf.reshape(lead + (-1,))
    out, off = [], 0
    for s in shapes:
        n = 1
        for d in s:
            n *= d
        out.append(flat[..., off:off + n].reshape(lead + tuple(s)))
        off += n
    return out


def _padded_rows(shapes, multiple):
    n = sum(functools.reduce(lambda a, b: a * b, s, 1) for s in shapes)
    rows = -(-n // LANES)
    return -(-rows // multiple) * multiple


def _to_full(stacked, axis):
    t = jnp.moveaxis(stacked, 0, axis)
    return t.reshape(t.shape[:axis] + (t.shape[axis] * t.shape[axis + 1],) + t.shape[axis + 2:])


def _to_stacked(full, axis):
    s = full.shape
    t = full.reshape(s[:axis] + (N_DEV, s[axis] // N_DEV) + s[axis + 1:])
    return jnp.moveaxis(t, axis, 0)


def _ffn_forward(x, h, w_up, cw, cb, w_down, next_g, tag):
    z = mm_in(h, w_up, f"ffn{tag}_up", stacked_out=True, out_dtype=BF16, rows=2048)
    nb, m, c = z.shape
    z4 = z.reshape(2, nb // 2, m, c)
    zc, act, *out = ffn_gate_down(z4, cw, cb, w_down, x, next_g, f"ffn{tag}_gate_down")
    return (out[0], out[1] if next_g is not None else None), (x, h, z4, zc, act)


def _ffn_backward(dx, dxb, saved, norm_g, w_up, cw, w_down, tag, after=()):
    x, h, z4, zc, act = saved
    nj = act.shape[0]
    dz4, dcw, dcb = ffn_gate_bwd(z4, zc, dxb, w_down, cw, f"ffn{tag}_gate_bwd", after)
    dw_down = mm_dw_out(act, dxb, f"ffn{tag}_down_dw")
    dz = dz4.reshape((2 * nj,) + dz4.shape[2:])
    dx, dxb, dg = mm_dx_in(dz, w_up, f"ffn{tag}_up_dx", norm=(x, norm_g, dx))
    dw_up = mm_dw_in(h, dz, 2 * nj, f"ffn{tag}_up_dw", transposed=True)
    return dx, dxb, dict(norm_g=dg, w_up=dw_up, conv_w=dcw, conv_b=dcb, w_down=dw_down)


def kernel(x, a_norm_g, a_w_in, a_b_in, a_v_norm_g, a_w_s, a_b_s, a_w_out, b_norm_g, b_w_in, b_w_grp, b_b_grp, b_scale, b_w_out, c_norm_g, c_w_in, c_b_in, c_conv_w, c_conv_b, c_w_a, c_b_a, c_w_i, c_b_i, c_lambda, c_w_out, d_norm_g, d_w_in, d_conv_w, d_w_out, ffn_norm_g, ffn_w_up, ffn_conv_w, ffn_conv_b, ffn_w_down, final_norm_g, loss_target, m_a_norm_g, m_a_w_in, m_a_b_in, m_a_v_norm_g, m_a_w_s, m_a_b_s, m_a_w_out, m_b_norm_g, m_b_w_in, m_b_w_grp, m_b_b_grp, m_b_scale, m_b_w_out, m_c_norm_g, m_c_w_in, m_c_b_in, m_c_conv_w, m_c_conv_b, m_c_w_a, m_c_b_a, m_c_w_i, m_c_b_i, m_c_lambda, m_c_w_out, m_d_norm_g, m_d_w_in, m_d_conv_w, m_d_w_out, m_ffn_norm_g, m_ffn_w_up, m_ffn_conv_w, m_ffn_conv_b, m_ffn_w_down, m_final_norm_g, v_a_norm_g, v_a_w_in, v_a_b_in, v_a_v_norm_g, v_a_w_s, v_a_b_s, v_a_w_out, v_b_norm_g, v_b_w_in, v_b_w_grp, v_b_b_grp, v_b_scale, v_b_w_out, v_c_norm_g, v_c_w_in, v_c_b_in, v_c_conv_w, v_c_conv_b, v_c_w_a, v_c_b_a, v_c_w_i, v_c_b_i, v_c_lambda, v_c_w_out, v_d_norm_g, v_d_w_in, v_d_conv_w, v_d_w_out, v_ffn_norm_g, v_ffn_w_up, v_ffn_conv_w, v_ffn_conv_b, v_ffn_w_down, v_final_norm_g):
    args = locals()
    w_loc = {n: args[n] for n in WEIGHTS}
    m_loc = {n: args["m_" + n] for n in WEIGHTS}
    v_loc = {n: args["v_" + n] for n in WEIGHTS}
    depth = ffn_w_up.shape[0]
    xs = x[0]
    target = loss_target[0]

    small_names = list(SMALL_SHARDED)
    small_shapes = [w_loc[n].shape for n in small_names]
    small_rows = _padded_rows(small_shapes, 8)
    small_packed = _pack([w_loc[n] for n in small_names], small_rows)
    early_names = ['ffn_conv_w', 'b_norm_g', 'c_norm_g', 'd_norm_g']
    late_names = [n for n in small_names if n not in early_names]
    packs = []
    for names in (early_names, late_names):
        shapes = [w_loc[n].shape for n in names]
        packs.append((names, shapes, _pack([w_loc[n] for n in names], _padded_rows(shapes, 8))))
    wa_in, wa_out = all_gather([a_w_in[0].astype(BF16), a_w_out[0].astype(BF16)], "gather_a")
    mixers = [None, (b_w_in, b_w_out), (c_w_in, c_w_out), (d_w_in, d_w_out)]
    groups = {}
    for l in range(depth):
        if l > 0:
            groups[f'mixer{l}'] = [mixers[l][0][0].astype(BF16), mixers[l][1][0].astype(BF16)] + ([packs[1][2]] if l == 1 else [])
        groups[f'ffn{l}'] = [ffn_w_up[l].astype(BF16), ffn_w_down[l].astype(BF16)] + ([packs[0][2]] if l == 0 else [])
    two_level = ('ffn0', 'mixer1', 'ffn1')
    in_flight = dict(zip(groups, gather_start(list(groups.values()),
                                              [SAME_CORE_PEERS if k in two_level else ALL_PEERS for k in groups],
                                              "gather_start", (wa_in, wa_out))))
    forwarding = {}

    def forward_on(key, after):
        forwarding[key] = forward_start(gather_wait(in_flight[key], after, f"gather_wait_{key}"), f"forward_start_{key}")
        return (forwarding[key][2],)

    def gathered(key, after):
        if key in two_level:
            return forward_wait(forwarding[key], after, f"forward_wait_{key}")
        return gather_wait(in_flight[key], after, f"gather_wait_{key}")

    sm = {}

    def unpack_small(pack, gathered):
        names, shapes, _ = pack
        sm.update((n, _to_full(s, SMALL_SHARDED[n])) for n, s in zip(names, _unpack(gathered, shapes, (N_DEV,))))

    def rows_full(st):
        return st.reshape((st.shape[0] * st.shape[1],) + st.shape[2:])

    nb = N_DEV
    ffn_cb = [ffn_conv_b[l].reshape(2, nb // 2, 1, -1) for l in range(depth)]
    ffn_g = [ffn_norm_g[l:l + 1] for l in range(depth)]
    a_bst = a_b_s[0].T
    w_up, w_down, saved = [None] * depth, [None] * depth, {}

    def ffn_forward(xl, hl, l, next_g):
        up, down, *early = gathered(f'ffn{l}', (xl,))
        if early:
            unpack_small(packs[0], early[0])
            ffn_cw.extend(sm['ffn_conv_w'][k].reshape(ffn_conv_w.shape[1], 2, nb // 2, -1).transpose(1, 2, 0, 3)
                          for k in range(depth))
        w_up[l], w_down[l] = up, rows_full(down)
        return _ffn_forward(xl, hl, w_up[l], ffn_cw[l], ffn_cb[l], w_down[l], next_g(), l)

    ffn_cw = []
    wa_out = rows_full(wa_out)
    h = rms_fwd(xs, a_norm_g, "a_norm")
    z = mm_in(h, wa_in, "a_in", bias=a_b_in)
    y = a_mid_fwd(z, a_v_norm_g, a_w_s[0], a_bst, "a_mid")
    x1, hf = mm_out(y, wa_out, xs, ffn_g[0], "a_out", forward_on('ffn0', (y,)))
    saved['a'] = (xs, h, z, y)
    forward_on('mixer1', (x1,))
    (x1, h), saved['f0'] = ffn_forward(x1, hf, 0, lambda: sm['b_norm_g'])

    forward_on('ffn1', (x1,))
    wb_in, wb_out, late = gathered('mixer1', (x1,))
    unpack_small(packs[1], late)
    b_wgrp, c_wa, c_wi = (sm[n][0].astype(BF16) for n in ('b_w_grp', 'c_w_a', 'c_w_i'))
    b_bgrp, c_ba, c_bi = (sm[n][0].reshape(1, -1) for n in ('b_b_grp', 'c_b_a', 'c_b_i'))
    wb_in, wb_out = rows_full(wb_in)[None], rows_full(wb_out)
    z = mm_in(h, wb_in, "b_in")
    y = b_mid_fwd(z, b_wgrp, b_bgrp, sm['b_scale'], "b_mid")
    x2, hf = mm_out(y, wb_out, x1, ffn_g[1], "b_out")
    saved['b'] = (x1, h, z, y)
    (x2, h), saved['f1'] = ffn_forward(x2, hf, 1, lambda: sm['c_norm_g'])

    wc_in, wc_out = gathered('mixer2', (x2,))
    wc_out = rows_full(wc_out)
    z = mm_in(h, wc_in, "c_in", bias=sm['c_b_in'])
    c_cw = sm['c_conv_w'][0]
    a_seq, b_seq, xr = c_mid_fwd(z, c_cw, sm['c_conv_b'], c_wa, c_ba, c_wi, c_bi, sm['c_lambda'], "c_mid")
    hs, y = c_scan_fwd(a_seq, b_seq, z, "c_scan")
    x3, hf = mm_out(y, wc_out, x2, ffn_g[2], "c_out")
    saved['c'] = (x2, h, z, y, a_seq, xr, hs)
    (x3, h), saved['f2'] = ffn_forward(x3, hf, 2, lambda: sm['d_norm_g'])

    wd_in, wd_out = gathered('mixer3', (x3,))
    wd_out = rows_full(wd_out)
    z = mm_in(h, wd_in, "d_in")
    d_cw = sm['d_conv_w'][0]
    y = d_mid_fwd(z, d_cw, "d_mid")
    x4, hf = mm_out(y, wd_out, x3, ffn_g[3], "d_out")
    saved['d'] = (x3, h, z, y)
    (x4, _), saved['f3'] = ffn_forward(x4, hf, 3, lambda: None)

    loss_part, dx, dxb, d_final_g = final_loss(x4, final_norm_g.reshape(1, -1), target, "final_loss")
    loss = lax.psum(loss_part[0, 0], ("x", "y", "c"))

    def rows_stacked(full):
        return full.reshape((N_DEV, full.shape[0] // N_DEV) + full.shape[1:])

    mx, my, mc = _place()
    own = jnp.stack([_dev_index(*_chip_of(mx, my, k), mc) for k in range(4)]).astype(jnp.int32)
    repl_shapes = [w_loc[n].shape for n in REPLICATED]
    repl_rows = _padded_rows(repl_shapes, 8 * N_DEV)
    shard_of = {n: (w_loc[n][0], m_loc[n][0], v_loc[n][0])
                for n in ('a_w_in', 'a_w_out', 'b_w_in', 'b_w_out', 'c_w_in', 'c_w_out', 'd_w_in', 'd_w_out')}
    shard_of['small'] = (small_packed, _pack([m_loc[n] for n in small_names], small_rows),
                         _pack([v_loc[n] for n in small_names], small_rows))
    updated = {}

    def finish(n, part, others):
        if n == 'repl':
            chunk = sum_parts(part, others, "rs_sum_repl")
            repl_g = all_gather([chunk], "gather_repl")[0].reshape(repl_rows, LANES)
            updated[n] = adamw(*(_pack([src[k] for k in REPLICATED], repl_rows) for src in (w_loc, m_loc, v_loc)),
                               "adamw_repl", g=repl_g)
        elif n.startswith('ffn_w'):
            base, l = n[:-1], int(n[-1])
            turn = (lambda t: jnp.swapaxes(t, 1, 2)) if base == 'ffn_w_up' else (lambda t: t)
            updated[base] = adamw_layer(turn(w_loc[base]), turn(m_loc[base]), turn(v_loc[base]), l, updated.get(base),
                                        f"adamw_{n}", part, others)
            return updated[base][1]
        else:
            updated[n] = adamw(*shard_of[n], f"adamw_{n}", p=part, l2=others)
        return updated[n][1]

    stages = [None, []]

    def advance(tag, new, after, finish_older=True):
        behind = []
        first = None
        if new:
            first = ([n for n, _ in new], sibling_start([g for _, g in new], f"rs_sibling_start{tag}"))
            behind.append(first[1][3])
        older = stages[1]
        if finish_older:
            for k, (names, started) in enumerate(older):
                parts, others = chips_wait(started, after, f"rs_chips_wait{tag}_{k}")
                behind += [finish(n, p, o) for n, p, o in zip(names, parts, others)]
            older = []
        if stages[0] is not None:
            names, started = stages[0]
            grads, got = sibling_wait(started, after, f"rs_sibling_wait{tag}")
            parts = [add_pairs(g, l, own, f"rs_add_{n}") for n, g, l in zip(names, grads, got)]
            older = older + [(names, chips_start(parts, f"rs_chips_start{tag}"))]
            behind.append(older[-1][1][3])
        stages[:] = [first, older]
        return tuple(behind)

    gf = [None] * depth
    dx, dxb, gf[3] = _ffn_backward(dx, dxb, saved['f3'], ffn_g[3], w_up[3], ffn_cw[3], w_down[3], 3)
    xin, h, z, y = saved['d']
    dy = mm_dx_out(dxb, wd_out, "d_out_dx")
    g_d_w_out = mm_dw_out(y, dxb, "d_out_dw")
    dz, g_d_conv_w = d_mid_bwd(z, dy, d_cw, "d_mid_bwd")
    dx, dxb, g_d_norm_g = mm_dx_in(dz, wd_in, "d_in_dx", norm=(xin, sm['d_norm_g'], dx))
    g_d_w_in = mm_dw_in(h, dz, nb, "d_in_dw")
    behind = advance(0, [('ffn_w_up3', gf[3]['w_up']), ('ffn_w_down3', rows_stacked(gf[3]['w_down'])), ('d_w_in', g_d_w_in),
                         ('d_w_out', rows_stacked(g_d_w_out))], (dx,))

    dx, dxb, gf[2] = _ffn_backward(dx, dxb, saved['f2'], ffn_g[2], w_up[2], ffn_cw[2], w_down[2], 2, behind)
    xin, h, z, y, a_seq, xr, hs = saved['c']
    dy = mm_dx_out(dxb, wc_out, "c_out_dx")
    g_c_w_out = mm_dw_out(y, dxb, "c_out_dw")
    lam_seq, da_seq, dgate, dgate_sum = c_scan_bwd(dy, z, hs, a_seq, "c_scan_bwd")
    dxr, g_c_w_a, g_c_w_i, g_c_b_a, g_c_b_i, g_c_lambda = c_mid_bwd(
        lam_seq, da_seq, xr, c_wa, c_ba, c_wi, c_bi, sm['c_lambda'], "c_mid_bwd")
    dxr_pre, g_c_conv_w, g_c_conv_b, dxr_pre_sum = conv_bwd(dxr, z, 1, c_cw, "c_conv_bwd")
    dz = jnp.concatenate([dgate, dxr_pre], axis=1)
    g_c_b_in = jnp.concatenate([dgate_sum, dxr_pre_sum], axis=1)
    dx, dxb, g_c_norm_g = mm_dx_in(dz, wc_in, "c_in_dx", norm=(xin, sm['c_norm_g'], dx))
    g_c_w_in = mm_dw_in(h, dz, nb, "c_in_dw")
    behind = advance(1, [('ffn_w_up2', gf[2]['w_up']), ('ffn_w_down2', rows_stacked(gf[2]['w_down'])), ('c_w_in', g_c_w_in),
                         ('c_w_out', rows_stacked(g_c_w_out))], (dx,))

    dx, dxb, gf[1] = _ffn_backward(dx, dxb, saved['f1'], ffn_g[1], w_up[1], ffn_cw[1], w_down[1], 1, behind)
    xin, h, z, y = saved['b']
    dy = mm_dx_out(dxb, wb_out, "b_out_dx")
    g_b_w_out = mm_dw_out(y, dxb, "b_out_dw")
    dp, g_b_w_grp, g_b_b_grp, g_b_scale = b_mid_bwd(z, dy, b_wgrp, b_bgrp, sm['b_scale'], "b_mid_bwd")
    dz = b_pool_bwd(dp, "b_pool_bwd")
    dx, dxb, g_b_norm_g = mm_dx_in(dz, wb_in, "b_in_dx", norm=(xin, sm['b_norm_g'], dx))
    g_b_w_in = mm_dw_in(h, dz, 1, "b_in_dw")
    behind = advance(2, [('ffn_w_up1', gf[1]['w_up']), ('ffn_w_down1', rows_stacked(gf[1]['w_down'])),
                         ('b_w_in', rows_stacked(g_b_w_in[0])), ('b_w_out', rows_stacked(g_b_w_out))], (dx,))

    dx, dxb, gf[0] = _ffn_backward(dx, dxb, saved['f0'], ffn_g[0], w_up[0], ffn_cw[0], w_down[0], 0, behind)
    full_small = {
        'b_norm_g': g_b_norm_g, 'b_w_grp': g_b_w_grp[None], 'b_b_grp': g_b_b_grp.reshape(b_b_grp.shape[:2] + (-1,)),
        'b_scale': g_b_scale, 'c_norm_g': g_c_norm_g, 'c_b_in': g_c_b_in, 'c_conv_w': g_c_conv_w[None],
        'c_conv_b': g_c_conv_b, 'c_w_a': g_c_w_a[None], 'c_b_a': g_c_b_a.reshape(c_b_a.shape[:2] + (-1,)),
        'c_w_i': g_c_w_i[None], 'c_b_i': g_c_b_i.reshape(c_b_i.shape[:2] + (-1,)), 'c_lambda': g_c_lambda,
        'd_norm_g': g_d_norm_g, 'd_conv_w': g_d_conv_w[None],
        'ffn_conv_w': jnp.stack([gf[l]['conv_w'].transpose(2, 0, 1, 3).reshape(ffn_conv_w.shape[1], -1) for l in range(depth)])}
    small_grads = _pack_stacked([_to_stacked(full_small[n], SMALL_SHARDED[n]) for n in small_names], small_rows)
    behind = advance(3, [('ffn_w_up0', gf[0]['w_up']), ('ffn_w_down0', rows_stacked(gf[0]['w_down'])), ('small', small_grads)],
                     (dx,))

    xin, h, z, y = saved['a']
    dy = mm_dx_out(dxb, wa_out, "a_out_dx", after=behind)
    g_a_w_out = mm_dw_out(y, dxb, "a_out_dw")
    behind = advance(4, [('a_w_out', rows_stacked(g_a_w_out))], (dy,))
    dz, g_a_b_in, g_a_v_norm_g, g_a_w_s, g_a_b_s = a_mid_bwd(z, dy, a_v_norm_g, a_w_s[0], a_bst, "a_mid_bwd", behind)
    g_a_w_in = mm_dw_in(h, dz, nb, "a_in_dw")
    behind = advance(5, [('a_w_in', g_a_w_in)], (dz,))
    behind = advance('5b', [], (dz, *behind), finish_older=False)
    dx, _, g_a_norm_g = mm_dx_in(dz, wa_in, "a_in_dx", behind, norm=(xin, a_norm_g, dx))
    grad_x = dx[None]

    tril = jnp.tril(jnp.ones((A_CHUNK, A_CHUNK), bool))
    repl_full = {
        'a_norm_g': g_a_norm_g, 'a_b_in': g_a_b_in, 'a_v_norm_g': g_a_v_norm_g,
        'a_w_s': jnp.where(tril, g_a_w_s, 0.0)[None], 'a_b_s': g_a_b_s[:, ::LANES].T[None],
        'ffn_norm_g': jnp.concatenate([gf[l]['norm_g'] for l in range(depth)], axis=0),
        'ffn_conv_b': jnp.stack([gf[l]['conv_b'].reshape(-1) for l in range(depth)]), 'final_norm_g': d_final_g.reshape(-1)}
    repl_grads = _pack([repl_full[n] for n in REPLICATED], repl_rows).reshape(N_DEV, repl_rows // N_DEV, LANES)
    behind = advance(6, [('repl', repl_grads)], (dx,))
    behind = advance(7, [], (dx, *behind))
    advance(8, [], (dx, *behind))

    outs = [{}, {}, {}, {}]
    for i, dst in enumerate(outs):
        for n in ('a_w_in', 'a_w_out', 'b_w_in', 'b_w_out', 'c_w_in', 'c_w_out', 'd_w_in', 'd_w_out'):
            dst[n] = updated[n][i][None]
        dst['ffn_w_up'] = jnp.swapaxes(updated['ffn_w_up'][i], 1, 2)
        dst['ffn_w_down'] = updated['ffn_w_down'][i]
        dst.update(zip(small_names, _unpack(updated['small'][i], small_shapes)))
        dst.update(zip(REPLICATED, _unpack(updated['repl'][i], repl_shapes)))
    out_g, out_d, out_m, out_v = outs

    return (loss, grad_x, *[out_g[n] for n in WEIGHTS], *[out_d[n] for n in WEIGHTS], *[out_m[n] for n in WEIGHTS],
            *[out_v[n] for n in WEIGHTS])
```

```python
import functools

import jax
import jax.numpy as jnp
from jax import lax
from jax.experimental import pallas as pl
from jax.experimental.pallas import tpu as pltpu

F32, BF16 = jnp.float32, jnp.bfloat16
MESH_ID = pl.DeviceIdType.MESH
N_DEV = 8
V7X_VMEM_LIMIT_BYTES = 56 << 20
LANES = 128
HALO = 8
POOL_HALO = 16

EPS = 1e-6
A_CHUNK, A_GROUPS = 128, 4
B_WINDOWS = (2, 4, 8, 16)
C_GATE_C = 8.0
ADAM_LR, ADAM_B1, ADAM_B2, ADAM_EPS, ADAM_WD, ADAM_STEP = 0.001, 0.9, 0.999, 1e-08, 0.01, 10

WEIGHTS = ['a_norm_g', 'a_w_in', 'a_b_in', 'a_v_norm_g', 'a_w_s', 'a_b_s', 'a_w_out', 'b_norm_g', 'b_w_in', 'b_w_grp',
           'b_b_grp', 'b_scale', 'b_w_out', 'c_norm_g', 'c_w_in', 'c_b_in', 'c_conv_w', 'c_conv_b', 'c_w_a', 'c_b_a',
           'c_w_i', 'c_b_i', 'c_lambda', 'c_w_out', 'd_norm_g', 'd_w_in', 'd_conv_w', 'd_w_out', 'ffn_norm_g',
           'ffn_w_up', 'ffn_conv_w', 'ffn_conv_b', 'ffn_w_down', 'final_norm_g']
SMALL_SHARDED = {'b_norm_g': 1, 'b_w_grp': 2, 'b_b_grp': 2, 'b_scale': 1, 'c_norm_g': 1, 'c_b_in': 1, 'c_conv_w': 2,
                 'c_conv_b': 1, 'c_w_a': 2, 'c_b_a': 2, 'c_w_i': 2, 'c_b_i': 2, 'c_lambda': 1, 'd_norm_g': 1,
                 'd_conv_w': 2, 'ffn_conv_w': 2}
REPLICATED = ['a_norm_g', 'a_b_in', 'a_v_norm_g', 'a_w_s', 'a_b_s', 'ffn_norm_g', 'ffn_conv_b', 'final_norm_g']


_GELU_C0, _GELU_C1 = 0.7978845608028654, 0.044715


def _gelu(x):
    return 0.5 * x * (1.0 + jnp.tanh(_GELU_C0 * (x + _GELU_C1 * (x * x * x))))


def _gelu_grad(x):
    t = jnp.tanh(_GELU_C0 * (x + _GELU_C1 * (x * x * x)))
    return 0.5 * (1.0 + t) + 0.5 * x * (1.0 - t * t) * (_GELU_C0 * (1.0 + 3.0 * _GELU_C1 * (x * x)))


def _sigmoid(x):
    return jax.nn.sigmoid(x)


def _log1p(x):
    u = 1.0 + x
    return jnp.where(u == 1.0, x, jnp.log(u) * (x / (u - 1.0)))


def _softplus(x):
    return jnp.maximum(x, 0.0) + _log1p(jnp.exp(-jnp.abs(x)))


def _expm1(x):
    poly = x * (1.0 + x * (1 / 2) * (1.0 + x * (1 / 3) * (1.0 + x * (1 / 4) * (1.0 + x * (1 / 5) * (
        1.0 + x * (1 / 6) * (1.0 + x * (1 / 7) * (1.0 + x * (1 / 8))))))))
    return jnp.where(jnp.abs(x) < 0.35, poly, jnp.exp(x) - 1.0)


def _down(xe, s):
    return xe if s == 0 else pltpu.roll(xe, s, 0)


def _up(xe, s):
    return xe if s == 0 else pltpu.roll(xe, xe.shape[0] - s, 0)


def _conv_ext(xe, w, taps):
    y = xe * w[taps - 1:taps]
    for s in range(1, taps):
        y = y + _down(xe, s) * w[taps - 1 - s:taps - s]
    return y


def _acc(ref, val, first):
    @pl.when(first)
    def _():
        ref[...] = val

    @pl.when(jnp.logical_not(first))
    def _():
        ref[...] += val


def _colsum(v):
    return jnp.sum(v, axis=0, keepdims=True)


def _dot(a, b, dims=((1,), (0,))):
    return lax.dot_general(a.astype(BF16), b.astype(BF16), (dims, ((), ())), preferred_element_type=F32)


_NN, _NT, _TN = ((1,), (0,)), ((1,), (1,)), ((0,), (0,))


def _call(body, name, grid, in_specs, out_specs, out_shape, scratch=(), after=()):
    n_in, n_after = len(in_specs), len(after)

    def ordered_body(*refs):
        return body(*refs[:n_in], *refs[n_in + n_after:])

    call = pl.pallas_call(
        ordered_body if n_after else body, name=name, grid=grid,
        in_specs=list(in_specs) + [pl.BlockSpec(memory_space=pl.ANY)] * n_after, out_specs=out_specs,
        out_shape=out_shape, scratch_shapes=list(scratch),
        compiler_params=pltpu.CompilerParams(dimension_semantics=("arbitrary",) * len(grid),
                                             vmem_limit_bytes=V7X_VMEM_LIMIT_BYTES))
    return lambda *args: call(*args, *after)


def _rows(m, t):
    t = min(m, t)
    assert m % t == 0, (m, t)
    return t


def _sds(shape, dtype=F32):
    return jax.ShapeDtypeStruct(tuple(shape), dtype)


def _prev_halo(tm, halo=HALO):
    return lambda i: jnp.maximum(i * (tm // halo) - 1, 0)


def _next_halo(tm, m, halo=HALO):
    return lambda i: jnp.minimum((i + 1) * (tm // halo), m // halo - 1)


def _matmul(name, ins, in_specs, out_shape, o_spec, grid, compute, after=()):
    def body(*refs):
        refs[-1][...] = compute(*refs[:-1]).astype(refs[-1].dtype)

    return _call(body, name, grid, in_specs, o_spec, out_shape, (), after)(*ins)


def mm_in(h, w_st, name, bias=None, stacked_out=False, out_dtype=F32, rows=1024):
    m, k = h.shape
    nb, _, n = w_st.shape
    tm = _rows(m, rows)
    in_specs = [pl.BlockSpec((tm, k), lambda i, j: (i, 0)), pl.BlockSpec((None, k, n), lambda i, j: (j, 0, 0))]
    if stacked_out:
        out, o_spec = _sds((nb, m, n), out_dtype), pl.BlockSpec((None, tm, n), lambda i, j: (j, i, 0))
    else:
        out, o_spec = _sds((m, nb * n), out_dtype), pl.BlockSpec((tm, n), lambda i, j: (i, j))
    if bias is None:
        return _matmul(name, (h, w_st), in_specs, out, o_spec, (m // tm, nb), lambda a, b: _dot(a[...], b[...]))
    in_specs.append(pl.BlockSpec((1, n), lambda i, j: (0, j)))
    return _matmul(name, (h, w_st, bias), in_specs, out, o_spec, (m // tm, nb),
                   lambda a, b, c: _dot(a[...], b[...]) + c[...])


def _split_rows(kf):
    g = max(1, kf // 1024)
    return g, kf // g


def _resident(shape):
    return pl.BlockSpec(shape, lambda *_: (0,) * len(shape), pipeline_mode=pl.Buffered(1))


def _norm_rows(xv, g):
    return (xv * lax.rsqrt(jnp.mean(xv * xv, axis=-1, keepdims=True) + EPS) * g).astype(BF16)


def _project_out(y_ref, w_ref, res_ref, g_ref, x_ref, h_ref):
    xv = res_ref[...] + _dot(y_ref[...], w_ref[...])
    x_ref[...] = xv
    h_ref[...] = _norm_rows(xv, g_ref[...])


def _call_projected(body, name, m, tm, in_specs, ins, own_outs, w_out, res, next_g, scratch=()):
    n = w_out.shape[1]
    row = lambda width: pl.BlockSpec((tm, width), lambda i: (i, 0))
    specs = list(in_specs) + [_resident(w_out.shape), row(n), pl.BlockSpec((1, n), lambda i: (0, 0))]
    out_specs = [row(width) for width, _ in own_outs] + [row(n), row(n)]
    out_shape = [_sds((m, width), dtype) for width, dtype in own_outs] + [_sds((m, n)), _sds((m, n), BF16)]
    return _call(body, name, (m // tm,), specs, out_specs, out_shape, scratch)(*ins, w_out, res, next_g)


def mm_out(y, w, res, next_g, name, after=()):
    kf, n = w.shape
    m = y.shape[0]
    tm = _rows(m, 512)
    row = pl.BlockSpec((tm, n), lambda i: (i, 0))

    def body(y_ref, w_ref, res_ref, g_ref, x_ref, h_ref):
        xv = res_ref[...] + _dot(y_ref[...], w_ref[...])
        x_ref[...] = xv
        h_ref[...] = _norm_rows(xv, g_ref[...])

    in_specs = [pl.BlockSpec((tm, kf), lambda i: (i, 0)), _resident((kf, n)), row, pl.BlockSpec((1, n), lambda i: (0, 0))]
    return _call(body, name, (m // tm,), in_specs, [row, row], [_sds((m, n)), _sds((m, n), BF16)], (), after)(y, w, res, next_g)


def mm_dx_in(dz, w_st, name, after=(), norm=None):
    nb, k, n = w_st.shape
    m = dz.shape[-2]
    tm = _rows(m, 512)
    w_spec = _resident((nb, k, n))
    if dz.ndim == 3:
        in_specs = [pl.BlockSpec((None, tm, n), lambda i, r=r: (r, i, 0)) for r in range(nb)] + [w_spec]

        def compute(*refs):
            acc = _dot(refs[0][...], refs[nb][0], _NT)
            for r in range(1, nb):
                acc = acc + _dot(refs[r][...], refs[nb][r], _NT)
            return acc

        ins = (*[dz] * nb, w_st)
    else:
        in_specs = [pl.BlockSpec((tm, nb * n), lambda i: (i, 0)), w_spec]

        def compute(dz_ref, w_ref):
            acc = _dot(dz_ref[:, :n], w_ref[0], _NT)
            for r in range(1, nb):
                acc = acc + _dot(dz_ref[:, r * n:(r + 1) * n], w_ref[r], _NT)
            return acc

        ins = (dz, w_st)
    row = pl.BlockSpec((tm, k), lambda i: (i, 0))
    if norm is None:
        return _matmul(name, ins, in_specs, _sds((m, k)), row, (m // tm,), compute, after)
    n_in = len(in_specs)
    vec = pl.BlockSpec((1, k), lambda i: (0, 0))

    def body(*refs):
        x_ref, g_ref, dr_ref, dx_ref, dxb_ref, dg_ref = refs[n_in:]
        dx, dg = _rms_bwd_math(x_ref[...], g_ref[...], compute(*refs[:n_in]))
        dx = dr_ref[...] + dx
        dx_ref[...] = dx
        dxb_ref[...] = dx.astype(BF16)
        _acc(dg_ref, dg, pl.program_id(0) == 0)

    return _call(body, name, (m // tm,), in_specs + [row, vec, row], [row, row, vec],
                 [_sds((m, k)), _sds((m, k), BF16), _sds((1, k))], (), after)(*ins, *norm)


def mm_dx_out(dout, w, name, groups=None, after=()):
    kf, n = w.shape
    m = dout.shape[0]
    tm = _rows(m, 1024)
    g, k = (groups, kf // groups) if groups else _split_rows(kf)
    in_specs = [pl.BlockSpec((tm, n), lambda i, j: (i, 0)), pl.BlockSpec((None, k, n), lambda i, j: (j, 0, 0))]
    if groups:
        out, o_spec = _sds((g, m, k)), pl.BlockSpec((None, tm, k), lambda i, j: (j, i, 0))
    else:
        out, o_spec = _sds((m, kf)), pl.BlockSpec((tm, k), lambda i, j: (i, j))
    return _matmul(name, (dout, w.reshape(g, k, n)), in_specs, out, o_spec, (m // tm, g),
                   lambda a, b: _dot(a[...], b[...], _NT), after)


def mm_dw_in(h, dz, nb, name, transposed=False):
    m, k = h.shape
    if dz.ndim == 3:
        n = dz.shape[2]
        dz_spec = pl.BlockSpec((None, m, n), lambda j: (j, 0, 0))
    else:
        n = dz.shape[1] // nb
        dz_spec = pl.BlockSpec((m, n), lambda j: (0, j))
    in_specs = [_resident((m, k)), dz_spec]
    if transposed:
        return _matmul(name, (h, dz), in_specs, _sds((nb, n, k), BF16), pl.BlockSpec((None, n, k), lambda j: (j, 0, 0)), (nb,),
                       lambda a, b: _dot(b[...], a[...], _TN))
    return _matmul(name, (h, dz), in_specs, _sds((nb, k, n), BF16), pl.BlockSpec((None, k, n), lambda j: (j, 0, 0)), (nb,),
                   lambda a, b: _dot(a[...], b[...], _TN))


def mm_dw_out(y, dout, name):
    m, n = dout.shape
    if y.ndim == 3:
        g, _, k = y.shape
        y_spec = pl.BlockSpec((None, m, k), lambda j: (j, 0, 0))
    else:
        g, k = _split_rows(y.shape[1])
        y_spec = pl.BlockSpec((m, k), lambda j: (0, j))
    in_specs = [y_spec, _resident((m, n))]
    out = _matmul(name, (y, dout), in_specs, _sds((g, k, n), BF16), pl.BlockSpec((None, k, n), lambda j: (j, 0, 0)), (g,),
                  lambda a, b: _dot(a[...], b[...], _TN))
    return out.reshape(g * k, n)


def rms_fwd(x, g, name):
    m, d = x.shape
    tm = _rows(m, 512)

    def body(x_ref, g_ref, o_ref):
        xv = x_ref[...]
        rstd = lax.rsqrt(jnp.mean(xv * xv, axis=-1, keepdims=True) + EPS)
        o_ref[...] = (xv * rstd * g_ref[...]).astype(BF16)

    row = pl.BlockSpec((tm, d), lambda i: (i, 0))
    vec = pl.BlockSpec((1, d), lambda i: (0, 0))
    return _call(body, name, (m // tm,), [row, vec], row, _sds((m, d), BF16))(x, g)


def _rms_bwd_math(xv, g, dh):
    rstd = lax.rsqrt(jnp.mean(xv * xv, axis=-1, keepdims=True) + EPS)
    xhat = xv * rstd
    dxhat = dh * g
    dx = rstd * (dxhat - xhat * jnp.mean(dxhat * xhat, axis=-1, keepdims=True))
    return dx, _colsum(dh * xhat)


def final_loss(x, g, target, name):
    m, d = x.shape
    tm = _rows(m, 512)

    def body(x_ref, g_ref, t_ref, l_ref, dx_ref, dxb_ref, dg_ref):
        xv, gv = x_ref[...], g_ref[...]
        rstd = lax.rsqrt(jnp.mean(xv * xv, axis=-1, keepdims=True) + EPS)
        err = xv * rstd * gv - t_ref[...]
        part = 0.5 * jnp.sum(jnp.mean(err * err, axis=-1, keepdims=True), axis=0, keepdims=True)
        dx, dg = _rms_bwd_math(xv, gv, err * (1.0 / d))
        dx_ref[...] = dx
        dxb_ref[...] = dx.astype(BF16)
        first = pl.program_id(0) == 0
        _acc(l_ref, jnp.broadcast_to(part, l_ref.shape), first)
        _acc(dg_ref, dg, first)

    row = pl.BlockSpec((tm, d), lambda i: (i, 0))
    vec = pl.BlockSpec((1, d), lambda i: (0, 0))
    lsp = pl.BlockSpec((1, LANES), lambda i: (0, 0))
    return _call(body, name, (m // tm,), [row, vec, row], [lsp, row, row, vec],
                 [_sds((1, LANES)), _sds((m, d)), _sds((m, d), BF16), _sds((1, d))])(x, g, target)


def _a_common(z_ref, vg_ref, ws_ref, bst_ref, tm, width):
    gw = width // A_GROUPS
    zp = z_ref[...]
    z = _gelu(zp)
    u, v = z[:, :width], z[:, width:]
    rstd = lax.rsqrt(jnp.mean(v * v, axis=-1, keepdims=True) + EPS)
    vhat = v * rstd
    vn = vhat * vg_ref[...]
    t_i = lax.broadcasted_iota(jnp.int32, (A_CHUNK, A_CHUNK), 0)
    s_i = lax.broadcasted_iota(jnp.int32, (A_CHUNK, A_CHUNK), 1)
    wsm = [jnp.where(s_i <= t_i, ws_ref[g], 0.0).astype(BF16) for g in range(A_GROUPS)]
    bst = bst_ref[...]
    return zp, u, rstd, vhat, vn.astype(BF16), wsm, bst, gw


def a_mid_fwd(z, vg, ws, bst, name):
    m, w2 = z.shape
    width = w2 // 2
    tm = _rows(m, 256)

    def body(z_ref, vg_ref, ws_ref, bst_ref, y_ref):
        _, u, _, _, vnb, wsm, bst, gw = _a_common(z_ref, vg_ref, ws_ref, bst_ref, tm, width)
        for c in range(tm // A_CHUNK):
            r0 = c * A_CHUNK
            for g in range(A_GROUPS):
                c0 = g * gw
                vs = _dot(wsm[g], vnb[r0:r0 + A_CHUNK, c0:c0 + gw]) + bst[:, g:g + 1]
                y_ref[r0:r0 + A_CHUNK, c0:c0 + gw] = (u[r0:r0 + A_CHUNK, c0:c0 + gw] * vs).astype(BF16)

    in_specs = [pl.BlockSpec((tm, w2), lambda i: (i, 0)), pl.BlockSpec((1, width), lambda i: (0, 0)),
                pl.BlockSpec((A_GROUPS, A_CHUNK, A_CHUNK), lambda i: (0, 0, 0)),
                pl.BlockSpec((A_CHUNK, A_GROUPS), lambda i: (0, 0))]
    return _call(body, name, (m // tm,), in_specs, pl.BlockSpec((tm, width), lambda i: (i, 0)),
                 _sds((m, width), BF16))(z, vg, ws, bst)


def a_mid_bwd(z, dx, w_out, vg, ws, bst, name, after=()):
    m, w2 = z.shape
    width = w2 // 2
    tm = _rows(m, 256)

    def body(z_ref, dx_ref, wo_ref, vg_ref, ws_ref, bst_ref, dz_ref, dbin_ref, dvg_ref, dws_ref, dbs_ref, dvn_scr, du_scr):
        first = pl.program_id(0) == 0
        zp, u, rstd, vhat, vnb, wsm, bst, gw = _a_common(z_ref, vg_ref, ws_ref, bst_ref, tm, width)
        dy = _dot(dx_ref[...], wo_ref[...], _NT)
        dws = [jnp.zeros((A_CHUNK, A_CHUNK), F32) for _ in range(A_GROUPS)]
        dbs = [jnp.zeros((A_CHUNK, 1), F32) for _ in range(A_GROUPS)]
        for c in range(tm // A_CHUNK):
            r0 = c * A_CHUNK
            for g in range(A_GROUPS):
                c0 = g * gw
                vn_cg = vnb[r0:r0 + A_CHUNK, c0:c0 + gw]
                vs = _dot(wsm[g], vn_cg) + bst[:, g:g + 1]
                dy_cg = dy[r0:r0 + A_CHUNK, c0:c0 + gw]
                dvs = dy_cg * u[r0:r0 + A_CHUNK, c0:c0 + gw]
                du_scr[r0:r0 + A_CHUNK, c0:c0 + gw] = dy_cg * vs
                dws[g] = dws[g] + _dot(dvs, vn_cg, _NT)
                dbs[g] = dbs[g] + jnp.sum(dvs, axis=1, keepdims=True)
                dvn_scr[r0:r0 + A_CHUNK, c0:c0 + gw] = _dot(wsm[g], dvs, _TN)
        for g in range(A_GROUPS):
            _acc(dws_ref.at[g], dws[g], first)
            _acc(dbs_ref.at[:, g * LANES:(g + 1) * LANES], jnp.broadcast_to(dbs[g], (A_CHUNK, LANES)), first)
        dvn = dvn_scr[...]
        _acc(dvg_ref, _colsum(dvn * vhat), first)
        dvhat = dvn * vg_ref[...]
        dv = rstd * (dvhat - vhat * jnp.mean(dvhat * vhat, axis=-1, keepdims=True))
        gg = _gelu_grad(zp)
        dzu = du_scr[...] * gg[:, :width]
        dzv = dv * gg[:, width:]
        dz_ref[:, :width] = dzu.astype(BF16)
        dz_ref[:, width:] = dzv.astype(BF16)
        _acc(dbin_ref.at[:, :width], _colsum(dzu), first)
        _acc(dbin_ref.at[:, width:], _colsum(dzv), first)

    const2 = lambda i: (0, 0)
    in_specs = [pl.BlockSpec((tm, w2), lambda i: (i, 0)), pl.BlockSpec((tm, dx.shape[1]), lambda i: (i, 0)),
                _resident(w_out.shape), pl.BlockSpec((1, width), const2),
                pl.BlockSpec((A_GROUPS, A_CHUNK, A_CHUNK), lambda i: (0, 0, 0)), pl.BlockSpec((A_CHUNK, A_GROUPS), const2)]
    out_specs = [pl.BlockSpec((tm, w2), lambda i: (i, 0)), pl.BlockSpec((1, w2), const2), pl.BlockSpec((1, width), const2),
                 pl.BlockSpec((A_GROUPS, A_CHUNK, A_CHUNK), lambda i: (0, 0, 0)),
                 pl.BlockSpec((A_CHUNK, A_GROUPS * LANES), const2)]
    out_shape = [_sds((m, w2), BF16), _sds((1, w2)), _sds((1, width)), _sds((A_GROUPS, A_CHUNK, A_CHUNK)),
                 _sds((A_CHUNK, A_GROUPS * LANES))]
    scratch = [pltpu.VMEM((tm, width), F32), pltpu.VMEM((tm, width), F32)]
    return _call(body, name, (m // tm,), in_specs, out_specs, out_shape, scratch, after)(z, dx, w_out, vg, ws, bst)


def _pool_minus_id(ze, i, tm, gw):
    pos = i * tm + lax.broadcasted_iota(jnp.int32, (tm, 1), 0)
    out = []
    for gi, win in enumerate(B_WINDOWS):
        s = ze[:, gi * gw:(gi + 1) * gw]
        step = 1
        while step < win:
            s = s + _down(s, step)
            step *= 2
        inv = 1.0 / jnp.minimum(pos + 1, win).astype(F32)
        out.append(s[POOL_HALO:] * inv - ze[POOL_HALO:, gi * gw:(gi + 1) * gw])
    return out


def _b_specs(tm, width):
    return [pl.BlockSpec((POOL_HALO, width), lambda i: (_prev_halo(tm, POOL_HALO)(i), 0)),
            pl.BlockSpec((tm, width), lambda i: (i, 0))]


def b_mid_fwd(z, wgrp, bgrp, scale, w_out, res, next_g, name):
    m, width = z.shape
    ng = len(B_WINDOWS)
    gw = width // ng
    tm = _rows(m, 512)

    def body(zp_ref, zm_ref, w_ref, b_ref, s_ref, *proj):
        i = pl.program_id(0)
        y_ref = proj[3]
        ze = jnp.concatenate([zp_ref[...] * (i > 0).astype(F32), zm_ref[...]], axis=0)
        p = _pool_minus_id(ze, i, tm, gw)
        for g in range(ng):
            cs = slice(g * gw, (g + 1) * gw)
            y = (_dot(p[g], w_ref[g]) + b_ref[:, cs]) * s_ref[:, cs]
            y_ref[:, cs] = y.astype(BF16)
        _project_out(y_ref, *proj[:3], *proj[4:])

    vec = pl.BlockSpec((1, width), lambda i: (0, 0))
    in_specs = _b_specs(tm, width) + [pl.BlockSpec((ng, gw, gw), lambda i: (0, 0, 0)), vec, vec]
    return _call_projected(body, name, m, tm, in_specs, (z, z, wgrp, bgrp, scale), [(width, BF16)], w_out, res, next_g)


def b_mid_bwd(z, dx, w_out, wgrp, bgrp, scale, name):
    m, width = z.shape
    ng = len(B_WINDOWS)
    gw = width // ng
    tm = _rows(m, 512)

    def body(zp_ref, zm_ref, dx_ref, wo_ref, w_ref, b_ref, s_ref, dp_ref, dw_ref, db_ref, ds_ref):
        i = pl.program_id(0)
        first = i == 0
        ze = jnp.concatenate([zp_ref[...] * (i > 0).astype(F32), zm_ref[...]], axis=0)
        p = _pool_minus_id(ze, i, tm, gw)
        dy = _dot(dx_ref[...], wo_ref[...], _NT)
        for g in range(ng):
            cs = slice(g * gw, (g + 1) * gw)
            dyg = dy[:, cs]
            ypre = _dot(p[g], w_ref[g]) + b_ref[:, cs]
            dyp = dyg * s_ref[:, cs]
            _acc(ds_ref.at[:, cs], _colsum(dyg * ypre), first)
            _acc(db_ref.at[:, cs], _colsum(dyp), first)
            _acc(dw_ref.at[g], _dot(p[g], dyp, _TN), first)
            dp_ref[:, cs] = _dot(dyp, w_ref[g], _NT)

    vec = pl.BlockSpec((1, width), lambda i: (0, 0))
    row = pl.BlockSpec((tm, width), lambda i: (i, 0))
    wsp = pl.BlockSpec((ng, gw, gw), lambda i: (0, 0, 0))
    in_specs = _b_specs(tm, width) + [pl.BlockSpec((tm, dx.shape[1]), lambda i: (i, 0)), _resident(w_out.shape), wsp, vec, vec]
    return _call(body, name, (m // tm,), in_specs, [row, wsp, vec, vec],
                 [_sds((m, width)), _sds((ng, gw, gw)), _sds((1, width)), _sds((1, width))])(z, z, dx, w_out, wgrp, bgrp, scale)


def b_pool_bwd(dp, name):
    m, width = dp.shape
    gw = width // len(B_WINDOWS)
    tm = _rows(m, 512)
    n_i = m // tm

    def body(dm_ref, dn_ref, dz_ref):
        i = pl.program_id(0)
        de = jnp.concatenate([dm_ref[...], dn_ref[...] * (i < n_i - 1).astype(F32)], axis=0)
        pos = i * tm + lax.broadcasted_iota(jnp.int32, (tm + POOL_HALO, 1), 0)
        for gi, win in enumerate(B_WINDOWS):
            cs = slice(gi * gw, (gi + 1) * gw)
            d = de[:, cs]
            s = d * (1.0 / jnp.minimum(pos + 1, win).astype(F32))
            step = 1
            while step < win:
                s = s + _up(s, step)
                step *= 2
            dz_ref[:, cs] = (s[:tm] - d[:tm]).astype(BF16)

    in_specs = [pl.BlockSpec((tm, width), lambda i: (i, 0)),
                pl.BlockSpec((POOL_HALO, width), lambda i: (_next_halo(tm, m, POOL_HALO)(i), 0))]
    return _call(body, name, (n_i,), in_specs, pl.BlockSpec((tm, width), lambda i: (i, 0)), _sds((m, width), BF16))(dp, dp)


def _c_gates(xr, wa_ref, ba_ref, wi_ref, bi_ref, lam_ref, heads, hw):
    xb = xr.astype(BF16)
    ra = jnp.concatenate([_dot(xb[:, h * hw:(h + 1) * hw], wa_ref[h]) for h in range(heads)], axis=1) + ba_ref[...]
    ia = jnp.concatenate([_dot(xb[:, h * hw:(h + 1) * hw], wi_ref[h]) for h in range(heads)], axis=1) + bi_ref[...]
    r, ig = _sigmoid(ra), _sigmoid(ia)
    sp = _softplus(-lam_ref[...])
    log_a = (-C_GATE_C * r) * sp
    a = jnp.exp(log_a)
    mult = jnp.sqrt(-_expm1(2.0 * log_a))
    return xb, r, ig, sp, a, mult


def c_mid_fwd(z, cw, cb, wa, ba, wi, bi, lam, name):
    m, w2 = z.shape
    width = w2 // 2
    heads, hw = wa.shape[0], wa.shape[1]
    taps = cw.shape[0]
    tm = _rows(m, 512)

    def body(zp_ref, zm_ref, cw_ref, cb_ref, wa_ref, ba_ref, wi_ref, bi_ref, lam_ref, a_ref, b_ref, xr_ref):
        i = pl.program_id(0)
        xe = jnp.concatenate([zp_ref[...] * (i > 0).astype(F32), zm_ref[...]], axis=0)
        xr = _conv_ext(xe, cw_ref[...], taps)[HALO:] + cb_ref[...]
        _, _, ig, _, a, mult = _c_gates(xr, wa_ref, ba_ref, wi_ref, bi_ref, lam_ref, heads, hw)
        a_ref[...] = a
        b_ref[...] = mult * (ig * xr)
        xr_ref[...] = xr

    vec = pl.BlockSpec((1, width), lambda i: (0, 0))
    row = pl.BlockSpec((tm, width), lambda i: (i, 0))
    wsp = pl.BlockSpec((heads, hw, hw), lambda i: (0, 0, 0))
    in_specs = [pl.BlockSpec((HALO, width), lambda i: (_prev_halo(tm)(i), 1)), pl.BlockSpec((tm, width), lambda i: (i, 1)),
                pl.BlockSpec((taps, width), lambda i: (0, 0)), vec, wsp, vec, wsp, vec, vec]
    return _call(body, name, (m // tm,), in_specs, [row, row, row], [_sds((m, width))] * 3)(
        z, z, cw, cb, wa, ba, wi, bi, lam)


_SCAN_ROWS = 512


def c_scan_fwd(a, b, z, w_out, res, next_g, name):
    m, width = a.shape
    tm = _rows(m, _SCAN_ROWS)

    def body(a_ref, b_ref, g_ref, wo_ref, res_ref, ng_ref, hs_ref, y_ref, x_ref, h_ref, h_carry):
        @pl.when(pl.program_id(0) == 0)
        def _():
            h_carry[...] = jnp.zeros_like(h_carry)

        def step(t, h):
            h = a_ref[pl.ds(t, 1), :] * h + b_ref[pl.ds(t, 1), :]
            hs_ref[pl.ds(t, 1), :] = h
            return h

        h_carry[...] = lax.fori_loop(0, tm, step, h_carry[...], unroll=8)
        y_ref[...] = (hs_ref[...] * _gelu(g_ref[...])).astype(BF16)
        _project_out(y_ref, wo_ref, res_ref, ng_ref, x_ref, h_ref)

    row = pl.BlockSpec((tm, width), lambda i: (i, 0))
    return _call_projected(body, name, m, tm, [row, row, row], (a, b, z), [(width, F32), (width, BF16)], w_out, res, next_g,
                           [pltpu.VMEM((1, width), F32)])


def c_scan_bwd(dx, w_out, z, hs, a, name):
    m, width = a.shape
    tm = _rows(m, _SCAN_ROWS)
    n_i = m // tm

    def body(dx_ref, wo_ref, g_ref, hs_ref, hp_ref, a_ref, lam_ref, da_ref, dg_ref, dgs_ref, lam_carry, a_carry):
        i = pl.program_id(0)
        first = i == 0

        @pl.when(first)
        def _():
            lam_carry[...] = jnp.zeros_like(lam_carry)
            a_carry[...] = jnp.zeros_like(a_carry)

        gp, dyv, hsv = g_ref[...], _dot(dx_ref[...], wo_ref[...], _NT), hs_ref[...]
        dgate = dyv * hsv * _gelu_grad(gp)
        dg_ref[...] = dgate.astype(BF16)
        _acc(dgs_ref, _colsum(dgate), first)
        lam_ref[...] = dyv * _gelu(gp)

        def step(k, carry):
            lam_next, a_next = carry
            t = tm - 1 - k
            lam_t = lam_ref[pl.ds(t, 1), :] + a_next * lam_next
            lam_ref[pl.ds(t, 1), :] = lam_t
            return lam_t, a_ref[pl.ds(t, 1), :]

        lam_c, a_c = lax.fori_loop(0, tm, step, (lam_carry[...], a_carry[...]), unroll=8)
        lam_carry[...] = lam_c
        a_carry[...] = a_c
        h_before = hp_ref[HALO - 1:HALO, :] * (i < n_i - 1).astype(F32)
        t_i = lax.broadcasted_iota(jnp.int32, (tm, 1), 0)
        da_ref[...] = lam_ref[...] * jnp.where(t_i == 0, h_before, _down(hsv, 1))

    row = pl.BlockSpec((tm, width), lambda i: (n_i - 1 - i, 0))
    halo = pl.BlockSpec((HALO, width), lambda i: (_prev_halo(tm)(n_i - 1 - i), 0))
    vec = pl.BlockSpec((1, width), lambda i: (0, 0))
    in_specs = [pl.BlockSpec((tm, dx.shape[1]), lambda i: (n_i - 1 - i, 0)), _resident(w_out.shape), row, row, halo, row]
    return _call(body, name, (n_i,), in_specs, [row, row, row, vec],
                 [_sds((m, width)), _sds((m, width)), _sds((m, width), BF16), _sds((1, width))],
                 [pltpu.VMEM((1, width), F32), pltpu.VMEM((1, width), F32)])(dx, w_out, z, hs, hs, a)


def c_mid_bwd(lam_seq, da, xr, wa, ba, wi, bi, lam, name):
    m, width = xr.shape
    heads, hw = wa.shape[0], wa.shape[1]
    tm = _rows(m, 512)

    def body(l_ref, da_ref, xr_ref, wa_ref, ba_ref, wi_ref, bi_ref, lam_ref,
             dxr_ref, dwa_ref, dwi_ref, dba_ref, dbi_ref, dlam_ref):
        first = pl.program_id(0) == 0
        xr_v, lmb = xr_ref[...], l_ref[...]
        xb, r, ig, sp, a, mult = _c_gates(xr_v, wa_ref, ba_ref, wi_ref, bi_ref, lam_ref, heads, hw)
        dmult = lmb * (ig * xr_v)
        dig = lmb * mult * xr_v
        dxr = lmb * mult * ig
        dla = da_ref[...] * a - dmult * (a * a) / mult
        dr = dla * (-C_GATE_C * sp)
        dsp = _colsum(dla * (-C_GATE_C * r))
        _acc(dlam_ref, dsp * (-_sigmoid(-lam_ref[...])), first)
        dra = dr * r * (1.0 - r)
        dia = dig * ig * (1.0 - ig)
        _acc(dba_ref, _colsum(dra), first)
        _acc(dbi_ref, _colsum(dia), first)
        for h in range(heads):
            cs = slice(h * hw, (h + 1) * hw)
            _acc(dwa_ref.at[h], _dot(xb[:, cs], dra[:, cs], _TN), first)
            _acc(dwi_ref.at[h], _dot(xb[:, cs], dia[:, cs], _TN), first)
            dxr_ref[:, cs] = dxr[:, cs] + _dot(dra[:, cs], wa_ref[h], _NT) + _dot(dia[:, cs], wi_ref[h], _NT)

    vec = pl.BlockSpec((1, width), lambda i: (0, 0))
    row = pl.BlockSpec((tm, width), lambda i: (i, 0))
    wsp = pl.BlockSpec((heads, hw, hw), lambda i: (0, 0, 0))
    return _call(body, name, (m // tm,), [row, row, row, wsp, vec, wsp, vec, vec], [row, wsp, wsp, vec, vec, vec],
                 [_sds((m, width)), _sds((heads, hw, hw)), _sds((heads, hw, hw)), _sds((1, width)), _sds((1, width)),
                  _sds((1, width))])(lam_seq, da, xr, wa, ba, wi, bi, lam)


def conv_bwd(dy, x_src, col_block, cw, name):
    m, width = dy.shape
    taps = cw.shape[0]
    tm = _rows(m, 512)
    n_i = m // tm

    def body(dm_ref, dn_ref, xp_ref, xm_ref, cw_ref, dx_ref, dw_ref, db_ref, dxs_ref):
        i = pl.program_id(0)
        first = i == 0
        de = jnp.concatenate([jnp.zeros((HALO, width), F32), dm_ref[...], dn_ref[...] * (i < n_i - 1).astype(F32)], axis=0)
        xe = jnp.concatenate([xp_ref[...] * (i > 0).astype(F32), xm_ref[...], jnp.zeros((HALO, width), F32)], axis=0)
        w = cw_ref[...]
        dx = de * w[taps - 1:taps]
        for s in range(1, taps):
            dx = dx + _up(de, s) * w[taps - 1 - s:taps - s]
        dx_ref[...] = dx[HALO:HALO + tm].astype(BF16)
        _acc(dxs_ref, _colsum(dx[HALO:HALO + tm]), first)
        dm = dm_ref[...]
        for s in range(taps):
            _acc(dw_ref.at[taps - 1 - s:taps - s, :], _colsum(dm * _down(xe, s)[HALO:HALO + tm]), first)
        _acc(db_ref, _colsum(dm), first)

    row = pl.BlockSpec((tm, width), lambda i: (i, 0))
    vec = pl.BlockSpec((1, width), lambda i: (0, 0))
    tsp = pl.BlockSpec((taps, width), lambda i: (0, 0))
    in_specs = [row, pl.BlockSpec((HALO, width), lambda i: (_next_halo(tm, m)(i), 0)),
                pl.BlockSpec((HALO, width), lambda i: (_prev_halo(tm)(i), col_block)),
                pl.BlockSpec((tm, width), lambda i: (i, col_block)), tsp]
    return _call(body, name, (n_i,), in_specs, [row, tsp, vec, vec],
                 [_sds((m, width), BF16), _sds((taps, width)), _sds((1, width)), _sds((1, width))])(dy, dy, x_src, x_src, cw)


def d_mid_fwd(z, cw, w_out, res, next_g, name):
    m, w3 = z.shape
    width = w3 // 3
    taps = cw.shape[0]
    tm = _rows(m, 512)

    def body(bm_ref, cp_ref, cm_ref, xp_ref, xm_ref, cw_ref, wo_ref, res_ref, ng_ref, y_ref, x_ref, h_ref):
        keep = (pl.program_id(0) > 0).astype(F32)
        qe = (jnp.concatenate([cp_ref[...] * keep, cm_ref[...]], axis=0)
              * jnp.concatenate([xp_ref[...], xm_ref[...]], axis=0))
        y_ref[...] = (bm_ref[...] * _conv_ext(qe, cw_ref[...], taps)[HALO:]).astype(BF16)
        _project_out(y_ref, wo_ref, res_ref, ng_ref, x_ref, h_ref)

    main = lambda c: pl.BlockSpec((tm, width), lambda i: (i, c))
    prev = lambda c: pl.BlockSpec((HALO, width), lambda i: (_prev_halo(tm)(i), c))
    in_specs = [main(0), prev(1), main(1), prev(2), main(2), pl.BlockSpec((taps, width), lambda i: (0, 0))]
    return _call_projected(body, name, m, tm, in_specs, (z, z, z, z, z, cw), [(width, BF16)], w_out, res, next_g)


def d_mid_bwd(z, dy, cw, name):
    m, w3 = z.shape
    width = w3 // 3
    taps = cw.shape[0]
    tm = _rows(m, 512)
    n_i = m // tm

    def body(bm_ref, bn_ref, cp_ref, cm_ref, cn_ref, xp_ref, xm_ref, xn_ref, dm_ref, dn_ref, cw_ref, dz_ref, dw_ref):
        i = pl.program_id(0)
        first = i == 0
        kp, kn = (i > 0).astype(F32), (i < n_i - 1).astype(F32)
        zeros = jnp.zeros((HALO, width), F32)
        ce = jnp.concatenate([cp_ref[...] * kp, cm_ref[...], cn_ref[...] * kn], axis=0)
        xe = jnp.concatenate([xp_ref[...], xm_ref[...], xn_ref[...]], axis=0)
        qe = ce * xe
        be = jnp.concatenate([zeros, bm_ref[...], bn_ref[...]], axis=0)
        dye = jnp.concatenate([zeros, dm_ref[...], dn_ref[...] * kn], axis=0)
        w = cw_ref[...]
        cq = _conv_ext(qe, w, taps)
        dcq = dye * be
        dq = dcq * w[taps - 1:taps]
        for s in range(1, taps):
            dq = dq + _up(dcq, s) * w[taps - 1 - s:taps - s]
        ms = slice(HALO, HALO + tm)
        dz_ref[:, :width] = (dye * cq)[ms].astype(BF16)
        dz_ref[:, width:2 * width] = (dq * xe)[ms].astype(BF16)
        dz_ref[:, 2 * width:] = (dq * ce)[ms].astype(BF16)
        for s in range(taps):
            _acc(dw_ref.at[taps - 1 - s:taps - s, :], _colsum(dcq[ms] * _down(qe, s)[ms]), first)

    main = lambda c: pl.BlockSpec((tm, width), lambda i: (i, c))
    prev = lambda c: pl.BlockSpec((HALO, width), lambda i: (_prev_halo(tm)(i), c))
    nxt = lambda c: pl.BlockSpec((HALO, width), lambda i: (_next_halo(tm, m)(i), c))
    tsp = pl.BlockSpec((taps, width), lambda i: (0, 0))
    in_specs = [main(0), nxt(0), prev(1), main(1), nxt(1), prev(2), main(2), nxt(2), main(0), nxt(0), tsp]
    return _call(body, name, (n_i,), in_specs, [pl.BlockSpec((tm, w3), lambda i: (i, 0)), tsp],
                 [_sds((m, w3), BF16), _sds((taps, width))])(z, z, z, z, z, z, z, z, dy, dy, cw)


def _halo_rows(dtype):
    return HALO * (4 // jnp.dtype(dtype).itemsize)


def ffn_gate_down(z, cw, cb, w_down, res, next_g, name):
    _, nj, m, c = z.shape
    n = w_down.shape[1]
    taps = cw.shape[2]
    tm = _rows(m, 256)
    hz = _halo_rows(z.dtype)

    def body(zp_ref, zm_ref, cw_ref, cb_ref, w_ref, res_ref, *rest):
        keep = (pl.program_id(0) > 0).astype(F32)
        zc_ref, act_ref, x_ref = rest[-4:-1] if next_g is not None else rest[-3:]
        acc = res_ref[...]
        for j in range(nj):
            zc = []
            for s in range(2):
                xe = jnp.concatenate([zp_ref[s, j].astype(F32) * keep, zm_ref[s, j].astype(F32)], axis=0)
                zc.append(_conv_ext(xe, cw_ref[s, j], taps)[hz:] + cb_ref[s, j])
                zc_ref[s, j] = zc[s].astype(BF16)
            act = (zc[0] * _sigmoid(zc[0]) * zc[1]).astype(BF16)
            act_ref[j] = act
            acc = acc + _dot(act, w_ref[j])
        x_ref[...] = acc
        if next_g is not None:
            rest[-1][...] = _norm_rows(acc, rest[0][...])

    row = pl.BlockSpec((tm, n), lambda i: (i, 0))
    in_specs = [pl.BlockSpec((2, nj, hz, c), lambda i: (0, 0, _prev_halo(tm, hz)(i), 0)),
                pl.BlockSpec((2, nj, tm, c), lambda i: (0, 0, i, 0)),
                _resident(cw.shape), _resident(cb.shape), _resident((nj, c, n)), row]
    out_specs = [pl.BlockSpec((2, nj, tm, c), lambda i: (0, 0, i, 0)), pl.BlockSpec((nj, tm, c), lambda i: (0, i, 0)), row]
    out_shape = [_sds((2, nj, m, c), BF16), _sds((nj, m, c), BF16), _sds((m, n))]
    ins = [z, z, cw, cb, w_down.reshape(nj, c, n), res]
    if next_g is not None:
        in_specs.append(pl.BlockSpec((1, n), lambda i: (0, 0)))
        out_specs.append(row)
        out_shape.append(_sds((m, n), BF16))
        ins.append(next_g)
    return _call(body, name, (m // tm,), in_specs, out_specs, out_shape)(*ins)


def ffn_gate_bwd(z, zc, dx, w_down, cw, name, after=()):
    _, nj, m, c = z.shape
    n = w_down.shape[1]
    taps = cw.shape[2]
    tm = _rows(m, 512)
    n_i = m // tm
    hz = _halo_rows(z.dtype)
    assert _halo_rows(dx.dtype) == hz and zc.dtype == z.dtype, (z.dtype, zc.dtype, dx.dtype)

    def body(zp_ref, zm_ref, zn_ref, cm_ref, cn_ref, dm_ref, dn_ref, wd_ref, cw_ref, dz_ref, dw_ref, db_ref):
        i = pl.program_id(1)
        first = i == 0
        kp, kn = (i > 0).astype(F32), (i < n_i - 1).astype(F32)
        xe = [jnp.concatenate([zp_ref[s].astype(F32) * kp, zm_ref[s].astype(F32), zn_ref[s].astype(F32) * kn], axis=0)
              for s in range(2)]
        zc = [jnp.concatenate([jnp.zeros((hz, c), F32), cm_ref[s].astype(F32), cn_ref[s].astype(F32)], axis=0)
              for s in range(2)]
        dact = _dot(jnp.concatenate([dm_ref[...], dn_ref[...]], axis=0), wd_ref[...], _NT)
        dae = jnp.concatenate([jnp.zeros((hz, c), F32), dact[:tm], dact[tm:] * kn], axis=0)
        sg = _sigmoid(zc[0])
        dzc = [dae * zc[1] * (sg * (1.0 + zc[0] * (1.0 - sg))), dae * (zc[0] * sg)]
        ms = slice(hz, hz + tm)
        for s in range(2):
            w = cw_ref[s]
            ups = [dzc[s]] + [_up(dzc[s], u) for u in range(1, taps)]
            dxs = ups[0] * w[taps - 1:taps]
            for u in range(1, taps):
                dxs = dxs + ups[u] * w[taps - 1 - u:taps - u]
            dz_ref[s] = dxs[ms].astype(BF16)
            tail = dzc[s][hz + tm:]
            x_end = xe[s][tm:]
            for u in range(taps):
                total = _colsum(ups[u] * xe[s]) - _colsum(tail * _down(x_end, u)[hz:])
                _acc(dw_ref.at[s, taps - 1 - u:taps - u, :], total, first)
            _acc(db_ref.at[s], _colsum(dzc[s][ms]), first)

    in_specs = [pl.BlockSpec((2, None, hz, c), lambda j, i: (0, j, _prev_halo(tm, hz)(i), 0)),
                pl.BlockSpec((2, None, tm, c), lambda j, i: (0, j, i, 0)),
                pl.BlockSpec((2, None, hz, c), lambda j, i: (0, j, _next_halo(tm, m, hz)(i), 0)),
                pl.BlockSpec((2, None, tm, c), lambda j, i: (0, j, i, 0)),
                pl.BlockSpec((2, None, hz, c), lambda j, i: (0, j, _next_halo(tm, m, hz)(i), 0)),
                pl.BlockSpec((tm, n), lambda j, i: (i, 0)),
                pl.BlockSpec((hz, n), lambda j, i: (_next_halo(tm, m, hz)(i), 0)),
                pl.BlockSpec((None, c, n), lambda j, i: (j, 0, 0)),
                pl.BlockSpec((2, None, taps, c), lambda j, i: (0, j, 0, 0))]
    out_specs = [pl.BlockSpec((2, None, tm, c), lambda j, i: (0, j, i, 0)),
                 pl.BlockSpec((2, None, taps, c), lambda j, i: (0, j, 0, 0)),
                 pl.BlockSpec((2, None, 1, c), lambda j, i: (0, j, 0, 0))]
    return _call(body, name, (nj, n_i), in_specs, out_specs,
                 [_sds((2, nj, m, c), BF16), _sds((2, nj, taps, c)), _sds((2, nj, 1, c))], (), after)(
        z, z, z, zc, zc, dx, dx, w_down.reshape(nj, c, n), cw)


_STREAM_TILE_BYTES = 2 << 20


def _tile_rows(r, c):
    if r * c * 4 <= _STREAM_TILE_BYTES:
        return r
    fits = [d for d in range(16, r, 16) if r % d == 0 and d * c * 4 <= _STREAM_TILE_BYTES]
    return max(fits) if fits else r


def _as2d(a, lead):
    shape = a.shape
    return a.reshape((lead, -1, shape[-1]) if lead else (-1, shape[-1]))


def add_pairs(g, l1, own, name):
    shape = l1.shape
    g3, l3 = _as2d(g, N_DEV), _as2d(l1, 4)
    _, r, c = l3.shape
    tr = _tile_rows(r, c)

    def body(own_ref, a_ref, b_ref, o_ref):
        o_ref[...] = (a_ref[...].astype(F32) + b_ref[...].astype(F32)).astype(o_ref.dtype)

    spec = pl.BlockSpec((None, tr, c), lambda k, i, own_ref: (k, i, 0))
    grid_spec = pltpu.PrefetchScalarGridSpec(
        num_scalar_prefetch=1, grid=(4, r // tr),
        in_specs=[pl.BlockSpec((None, tr, c), lambda k, i, own_ref: (own_ref[k], i, 0)), spec], out_specs=spec)
    out = pl.pallas_call(
        body, name=name, grid_spec=grid_spec, out_shape=_sds(l3.shape, l1.dtype),
        compiler_params=pltpu.CompilerParams(dimension_semantics=("arbitrary", "arbitrary"),
                                             vmem_limit_bytes=V7X_VMEM_LIMIT_BYTES))(own, g3, l3)
    return out.reshape(shape)


def _grad_sum(p_ref, l_ref):
    return ((p_ref[...].astype(F32) + l_ref[0].astype(F32)) + l_ref[1].astype(F32)) + l_ref[2].astype(F32)


def sum_parts(p, l2, name):
    _, r, c = p.shape

    def body(p_ref, l_ref, o_ref):
        o_ref[...] = _grad_sum(p_ref, l_ref)

    return _call(body, name, (1,), [pl.BlockSpec((None, r, c), lambda i: (0, 0, 0)), pl.BlockSpec((3, r, c), lambda i: (0, 0, 0))],
                 pl.BlockSpec((r, c), lambda i: (0, 0)), _sds((r, c)))(p, l2)


def _adamw_math(w, g, m, v):
    m = ADAM_B1 * m + (1.0 - ADAM_B1) * g
    v = ADAM_B2 * v + (1.0 - ADAM_B2) * (g * g)
    m_hat = m / (1.0 - ADAM_B1 ** ADAM_STEP)
    v_hat = v / (1.0 - ADAM_B2 ** ADAM_STEP)
    delta = -ADAM_LR * (m_hat / (jnp.sqrt(v_hat) + ADAM_EPS) + ADAM_WD * w)
    return delta, m, v


def adamw(w, m, v, name, g=None, p=None, l2=None):
    shape = w.shape
    w2, m2, v2 = (_as2d(t, 0) for t in (w, m, v))
    r, c = w2.shape
    tr = _tile_rows(r, c)
    row = pl.BlockSpec((tr, c), lambda i: (i, 0))
    if g is None:
        p3, l3 = _as2d(p, 4), _as2d(l2, 3)
        gin = (p3, l3)
        gspecs = [pl.BlockSpec((None, tr, c), lambda i: (0, i, 0)), pl.BlockSpec((3, tr, c), lambda i: (0, i, 0))]
    else:
        gin, gspecs = (_as2d(g, 0),), [row]

    def body(*refs):
        n_g = len(gin)
        w_ref, m_ref, v_ref, g_ref, d_ref, nm_ref, nv_ref = refs[n_g:]
        grad = refs[0][...] if n_g == 1 else _grad_sum(refs[0], refs[1])
        delta, nm, nv = _adamw_math(w_ref[...], grad, m_ref[...], v_ref[...])
        g_ref[...] = grad
        d_ref[...] = delta
        nm_ref[...] = nm
        nv_ref[...] = nv

    outs = _call(body, name, (r // tr,), gspecs + [row, row, row], [row] * 4, [_sds((r, c))] * 4)(*gin, w2, m2, v2)
    return tuple(o.reshape(shape) for o in outs)


def adamw_layer(w, m, v, layer, prev, name, p, l2):
    n_l, r, c = w.shape
    tr = _tile_rows(r, c)
    slab = pl.BlockSpec((None, tr, c), lambda i: (layer, i, 0))
    in_specs = [pl.BlockSpec((None, tr, c), lambda i: (0, i, 0)), pl.BlockSpec((3, tr, c), lambda i: (0, i, 0)), slab, slab, slab]
    n_in = len(in_specs)
    prev = () if prev is None else tuple(prev)

    def body(p_ref, l_ref, w_ref, m_ref, v_ref, *rest):
        g_ref, d_ref, nm_ref, nv_ref = rest[len(prev):]
        grad = _grad_sum(p_ref, l_ref)
        delta, nm, nv = _adamw_math(w_ref[...], grad, m_ref[...], v_ref[...])
        g_ref[...] = grad
        d_ref[...] = delta
        nm_ref[...] = nm
        nv_ref[...] = nv

    return pl.pallas_call(
        body, name=name, grid=(r // tr,), in_specs=in_specs + [pl.BlockSpec(memory_space=pl.ANY)] * len(prev),
        out_specs=[slab] * 4, out_shape=[_sds((n_l, r, c))] * 4,
        input_output_aliases={n_in + q: q for q in range(len(prev))},
        compiler_params=pltpu.CompilerParams(dimension_semantics=("arbitrary",), vmem_limit_bytes=V7X_VMEM_LIMIT_BYTES),
    )(_as2d(p, 4), _as2d(l2, 3), w, m, v, *prev)


def _comm_call(body, name, ins, out_shape, n_sems):
    any_spec = pl.BlockSpec(memory_space=pl.ANY)
    return pl.pallas_call(
        body, name=name, in_specs=[any_spec] * len(ins), out_specs=[any_spec] * len(out_shape), out_shape=out_shape,
        scratch_shapes=[pltpu.SemaphoreType.DMA((n,)) for n in n_sems],
        compiler_params=pltpu.CompilerParams(has_side_effects=True))(*ins)


def _place():
    return lax.axis_index("x"), lax.axis_index("y"), lax.axis_index("c")


def _dev_index(px, py, pc):
    return 4 * px + 2 * py + pc


def all_gather(blocks, name):
    n_t = len(blocks)

    def body(*refs):
        ins, outs = refs[:n_t], refs[n_t:2 * n_t]
        send_sems, recv_sems, local_sems = refs[2 * n_t:]
        x, y, c = _place()
        me, sibling = (x, y, c), (x, y, 1 - c)
        chips = [(1 - x, y), (x, 1 - y), (1 - x, 1 - y)]

        def copy(t, k, block, to, src=None):
            dst = outs[t].at[_dev_index(*block)]
            return pltpu.make_async_remote_copy(
                src_ref=dst if src is None else src, dst_ref=dst, send_sem=send_sems.at[t * 7 + k],
                recv_sem=recv_sems.at[t * 7 + k], device_id=to, device_id_type=MESH_ID)

        mine = [pltpu.make_async_copy(ins[t], outs[t].at[_dev_index(*me)], local_sems.at[t]) for t in range(n_t)]
        for cp in mine:
            cp.start()
        first = []
        for t in range(n_t):
            first.append(copy(t, 0, me, sibling, src=ins[t]))
            first += [copy(t, 1 + j, me, (*chip, c), src=ins[t]) for j, chip in enumerate(chips)]
        for cp in first:
            cp.start()
        passed = []
        for t in range(n_t):
            for j, chip in enumerate(chips):
                copy(t, 1 + j, (*chip, c), me).wait_recv()
                cp = copy(t, 4 + j, (*chip, c), sibling)
                cp.start()
                passed.append(cp)
        for t in range(n_t):
            copy(t, 0, sibling, me).wait_recv()
            for j, chip in enumerate(chips):
                copy(t, 4 + j, (*chip, 1 - c), me).wait_recv()
        for cp in first + passed:
            cp.wait_send()
        for cp in mine:
            cp.wait()

    out_shape = [_sds((N_DEV,) + b.shape, b.dtype) for b in blocks]
    return _comm_call(body, name, blocks, out_shape, (7 * n_t, 7 * n_t, n_t))


def _chip_of(x, y, k):
    return (x if k % 2 == 0 else 1 - x), (y if k // 2 == 0 else 1 - y)


_HBM_SPEC = pl.BlockSpec(memory_space=pltpu.HBM)
_SEM_SPEC = pl.BlockSpec(memory_space=pltpu.SEMAPHORE)
_DATAFLOW = pltpu.SideEffectType.DATAFLOW_SIDE_EFFECTING


def _in_hbm(a):
    return pltpu.with_memory_space_constraint(a, pltpu.HBM)


def _split_start(name, issue, srcs, land_shapes, sem_counts, after=()):
    n_buf, n_sem, n_after = len(srcs) + len(land_shapes), len(sem_counts), len(after)

    def body(*refs):
        issue(refs[:len(srcs)], refs[len(srcs):n_buf], refs[n_buf + n_after:n_buf + n_after + n_sem])
        refs[-1][...] = jnp.zeros_like(refs[-1])

    bufs = [pltpu.HBM(s.shape, s.dtype) for s in list(srcs) + list(land_shapes)]
    outs = pl.pallas_call(
        body, name=name, in_specs=(*[_HBM_SPEC] * n_buf, *[pl.BlockSpec(memory_space=pl.ANY)] * n_after),
        out_shape=(*[pltpu.SemaphoreType.DMA((n,)) for n in sem_counts], *bufs, _sds((8, LANES))),
        out_specs=(*[_SEM_SPEC] * n_sem, *[_HBM_SPEC] * n_buf, pl.BlockSpec(memory_space=pltpu.VMEM)),
        input_output_aliases={i: n_sem + i for i in range(n_buf)},
        compiler_params=pltpu.CompilerParams(has_side_effects=_DATAFLOW),
    )(*[_in_hbm(s) for s in srcs], *[_in_hbm(lax.empty(s.shape, s.dtype)) for s in land_shapes], *after)
    return outs[:n_sem], outs[n_sem:n_sem + len(srcs)], outs[n_sem + len(srcs):n_sem + n_buf], outs[-1]


def _split_wait(name, finish, sems, srcs, lands, after):
    n_buf, n_sem = len(srcs) + len(lands), len(sems)

    def body(*refs):
        finish(refs[:len(srcs)], refs[len(srcs):n_buf], refs[n_buf:n_buf + n_sem])

    bufs = [pltpu.HBM(s.shape, s.dtype) for s in list(srcs) + list(lands)]
    outs = pl.pallas_call(
        body, name=name, in_specs=(*[_HBM_SPEC] * n_buf, *[_SEM_SPEC] * n_sem, *[pl.BlockSpec(memory_space=pl.ANY)] * len(after)),
        out_shape=tuple(bufs), out_specs=(_HBM_SPEC,) * n_buf, input_output_aliases={i: i for i in range(n_buf)},
        compiler_params=pltpu.CompilerParams(has_side_effects=_DATAFLOW),
    )(*srcs, *lands, *sems, *after)
    return outs[:len(srcs)], outs[len(srcs):]


def _peer(x, y, c, r):
    return (1 - x if r & 4 else x), (1 - y if r & 2 else y), (1 - c if r & 1 else c)


ALL_PEERS = tuple(range(1, N_DEV))
SAME_CORE_PEERS = (2, 4, 6)


def _gather_copies(src_refs, land_refs, sem_refs, arrivals, peers):
    send_sems, recv_sems, local_sems = sem_refs
    x, y, c = _place()
    me = _dev_index(x, y, c)
    local, sends, recvs = [], [], []
    for j, (src, land) in enumerate(zip(src_refs, land_refs)):
        local.append(pltpu.make_async_copy(src, land.at[me], local_sems.at[j]))
        for p, r in enumerate(peers):
            peer = _peer(x, y, c, r)
            q = len(peers) * j + p
            sends.append(pltpu.make_async_remote_copy(src_ref=src, dst_ref=land.at[me], send_sem=send_sems.at[q],
                                                      recv_sem=recv_sems.at[q], device_id=peer, device_id_type=MESH_ID))
            if arrivals:
                recvs.append(pltpu.make_async_remote_copy(
                    src_ref=src, dst_ref=land.at[_dev_index(*peer)], send_sem=send_sems.at[q], recv_sem=recv_sems.at[q],
                    device_id=peer, device_id_type=MESH_ID))
    return local, sends, recvs


def gather_start(groups, peers, name, after=()):
    flat = [b for g in groups for b in g]
    bounds = [sum(len(g) for g in groups[:i]) for i in range(len(groups) + 1)]

    def issue(src_refs, land_refs, sem_refs):
        for i in range(len(groups)):
            lo, hi = bounds[i], bounds[i + 1]
            local, sends, _ = _gather_copies(src_refs[lo:hi], land_refs[lo:hi], sem_refs[3 * i:3 * i + 3], False, peers[i])
            for cp in local + sends:
                cp.start()

    sem_counts = [n for g, p in zip(groups, peers) for n in (len(p) * len(g), len(p) * len(g), len(g))]
    sems, srcs, lands, _ = _split_start(name, issue, flat, [_sds((N_DEV,) + b.shape, b.dtype) for b in flat], sem_counts, after)
    return [(sems[3 * i:3 * i + 3], srcs[bounds[i]:bounds[i + 1]], lands[bounds[i]:bounds[i + 1]], peers[i])
            for i in range(len(groups))]


def gather_wait(group, after, name):
    sems, srcs, lands, peers = group

    def finish(src_refs, land_refs, sem_refs):
        local, sends, recvs = _gather_copies(src_refs, land_refs, sem_refs, True, peers)
        for cp in local:
            cp.wait()
        for cp in recvs:
            cp.wait_recv()
        for cp in sends:
            cp.wait_send()

    return _split_wait(name, finish, sems, srcs, lands, after)[1]


def _forward_copies(land_refs, sem_refs, arrivals):
    send_sems, recv_sems = sem_refs
    x, y, c = _place()
    sends, recvs = [], []
    for t, land in enumerate(land_refs):
        for k in range(4):
            cx, cy = _chip_of(x, y, k)
            mine, theirs = land.at[_dev_index(cx, cy, c)], land.at[_dev_index(cx, cy, 1 - c)]
            sems = dict(send_sem=send_sems.at[4 * t + k], recv_sem=recv_sems.at[4 * t + k], device_id=(x, y, 1 - c),
                        device_id_type=MESH_ID)
            sends.append(pltpu.make_async_remote_copy(src_ref=mine, dst_ref=mine, **sems))
            if arrivals:
                recvs.append(pltpu.make_async_remote_copy(src_ref=theirs, dst_ref=theirs, **sems))
    return sends, recvs


def forward_start(lands, name):
    def issue(land_refs, _, sem_refs):
        for cp in _forward_copies(land_refs, sem_refs, False)[0]:
            cp.start()

    n = 4 * len(lands)
    sems, lands, _, token = _split_start(name, issue, lands, [], (n, n))
    return sems, lands, token


def forward_wait(started, after, name):
    sems, lands, _ = started

    def finish(land_refs, _, sem_refs):
        sends, recvs = _forward_copies(land_refs, sem_refs, True)
        for cp in recvs:
            cp.wait_recv()
        for cp in sends:
            cp.wait_send()

    return _split_wait(name, finish, sems, lands, [], after)[0]


def _sibling_copies(src_refs, land_refs, sem_refs):
    send_sems, recv_sems = sem_refs
    x, y, c = _place()
    copies = []
    for t, (src, land) in enumerate(zip(src_refs, land_refs)):
        for k in range(4):
            cx, cy = _chip_of(x, y, k)
            copies.append(pltpu.make_async_remote_copy(
                src_ref=src.at[_dev_index(cx, cy, 1 - c)], dst_ref=land.at[k], send_sem=send_sems.at[4 * t + k],
                recv_sem=recv_sems.at[4 * t + k], device_id=(x, y, 1 - c), device_id_type=MESH_ID))
    return copies


def _chip_copies(src_refs, land_refs, sem_refs):
    send_sems, recv_sems = sem_refs
    x, y, c = _place()
    copies = []
    for t, (src, land) in enumerate(zip(src_refs, land_refs)):
        for k in range(1, 4):
            cx, cy = _chip_of(x, y, k)
            copies.append(pltpu.make_async_remote_copy(
                src_ref=src.at[k], dst_ref=land.at[k - 1], send_sem=send_sems.at[3 * t + k - 1],
                recv_sem=recv_sems.at[3 * t + k - 1], device_id=(cx, cy, c), device_id_type=MESH_ID))
    return copies


def _exchange_start(copies_of, n_land, per_array, arrays, name):
    def issue(src_refs, land_refs, sem_refs):
        for cp in copies_of(src_refs, land_refs, sem_refs):
            cp.start()

    n = per_array * len(arrays)
    lands = [_sds((n_land,) + a.shape[1:], a.dtype) for a in arrays]
    return _split_start(name, issue, arrays, lands, (n, n))


def _exchange_wait(copies_of, started, after, name):
    sems, srcs, lands, _ = started

    def finish(src_refs, land_refs, sem_refs):
        copies = copies_of(src_refs, land_refs, sem_refs)
        for cp in copies:
            cp.wait_recv()
        for cp in copies:
            cp.wait_send()

    return _split_wait(name, finish, sems, srcs, lands, after)


def sibling_start(grads, name):
    return _exchange_start(_sibling_copies, 4, 4, grads, name)


def sibling_wait(started, after, name):
    return _exchange_wait(_sibling_copies, started, after, name)


def chips_start(parts, name):
    return _exchange_start(_chip_copies, 3, 3, parts, name)


def chips_wait(started, after, name):
    return _exchange_wait(_chip_copies, started, after, name)


def _pack(arrays, rows):
    flat = jnp.concatenate([a.reshape(-1) for a in arrays])
    return jnp.pad(flat, (0, rows * LANES - flat.shape[0])).reshape(rows, LANES)


def _pack_stacked(arrays, rows):
    flat = jnp.concatenate([a.reshape(N_DEV, -1) for a in arrays], axis=1)
    return jnp.pad(flat, ((0, 0), (0, rows * LANES - flat.shape[1]))).reshape(N_DEV, rows, LANES)


def _unpack(buf, shapes, lead=()):
    flat = buf.reshape(lead + (-1,))
    out, off = [], 0
    for s in shapes:
        n = 1
        for d in s:
            n *= d
        out.append(flat[..., off:off + n].reshape(lead + tuple(s)))
        off += n
    return out


def _padded_rows(shapes, multiple):
    n = sum(functools.reduce(lambda a, b: a * b, s, 1) for s in shapes)
    rows = -(-n // LANES)
    return -(-rows // multiple) * multiple


def _to_full(stacked, axis):
    t = jnp.moveaxis(stacked, 0, axis)
    return t.reshape(t.shape[:axis] + (t.shape[axis] * t.shape[axis + 1],) + t.shape[axis + 2:])


def _to_stacked(full, axis):
    s = full.shape
    t = full.reshape(s[:axis] + (N_DEV, s[axis] // N_DEV) + s[axis + 1:])
    return jnp.moveaxis(t, axis, 0)


def _ffn_forward(x, h, w_up, cw, cb, w_down, next_g, tag):
    z = mm_in(h, w_up, f"ffn{tag}_up", stacked_out=True, out_dtype=BF16, rows=2048)
    nb, m, c = z.shape
    z4 = z.reshape(2, nb // 2, m, c)
    zc, act, *out = ffn_gate_down(z4, cw, cb, w_down, x, next_g, f"ffn{tag}_gate_down")
    return (out[0], out[1] if next_g is not None else None), (x, h, z4, zc, act)


def _ffn_backward(dx, dxb, saved, norm_g, w_up, cw, w_down, tag, after=()):
    x, h, z4, zc, act = saved
    nj = act.shape[0]
    dz4, dcw, dcb = ffn_gate_bwd(z4, zc, dxb, w_down, cw, f"ffn{tag}_gate_bwd", after)
    dw_down = mm_dw_out(act, dxb, f"ffn{tag}_down_dw")
    dz = dz4.reshape((2 * nj,) + dz4.shape[2:])
    dx, dxb, dg = mm_dx_in(dz, w_up, f"ffn{tag}_up_dx", norm=(x, norm_g, dx))
    dw_up = mm_dw_in(h, dz, 2 * nj, f"ffn{tag}_up_dw", transposed=True)
    return dx, dxb, dict(norm_g=dg, w_up=dw_up, conv_w=dcw, conv_b=dcb, w_down=dw_down)


def kernel(x, a_norm_g, a_w_in, a_b_in, a_v_norm_g, a_w_s, a_b_s, a_w_out, b_norm_g, b_w_in, b_w_grp, b_b_grp, b_scale, b_w_out, c_norm_g, c_w_in, c_b_in, c_conv_w, c_conv_b, c_w_a, c_b_a, c_w_i, c_b_i, c_lambda, c_w_out, d_norm_g, d_w_in, d_conv_w, d_w_out, ffn_norm_g, ffn_w_up, ffn_conv_w, ffn_conv_b, ffn_w_down, final_norm_g, loss_target, m_a_norm_g, m_a_w_in, m_a_b_in, m_a_v_norm_g, m_a_w_s, m_a_b_s, m_a_w_out, m_b_norm_g, m_b_w_in, m_b_w_grp, m_b_b_grp, m_b_scale, m_b_w_out, m_c_norm_g, m_c_w_in, m_c_b_in, m_c_conv_w, m_c_conv_b, m_c_w_a, m_c_b_a, m_c_w_i, m_c_b_i, m_c_lambda, m_c_w_out, m_d_norm_g, m_d_w_in, m_d_conv_w, m_d_w_out, m_ffn_norm_g, m_ffn_w_up, m_ffn_conv_w, m_ffn_conv_b, m_ffn_w_down, m_final_norm_g, v_a_norm_g, v_a_w_in, v_a_b_in, v_a_v_norm_g, v_a_w_s, v_a_b_s, v_a_w_out, v_b_norm_g, v_b_w_in, v_b_w_grp, v_b_b_grp, v_b_scale, v_b_w_out, v_c_norm_g, v_c_w_in, v_c_b_in, v_c_conv_w, v_c_conv_b, v_c_w_a, v_c_b_a, v_c_w_i, v_c_b_i, v_c_lambda, v_c_w_out, v_d_norm_g, v_d_w_in, v_d_conv_w, v_d_w_out, v_ffn_norm_g, v_ffn_w_up, v_ffn_conv_w, v_ffn_conv_b, v_ffn_w_down, v_final_norm_g):
    args = locals()
    w_loc = {n: args[n] for n in WEIGHTS}
    m_loc = {n: args["m_" + n] for n in WEIGHTS}
    v_loc = {n: args["v_" + n] for n in WEIGHTS}
    depth = ffn_w_up.shape[0]
    xs = x[0]
    target = loss_target[0]

    small_names = list(SMALL_SHARDED)
    small_shapes = [w_loc[n].shape for n in small_names]
    small_rows = _padded_rows(small_shapes, 8)
    small_packed = _pack([w_loc[n] for n in small_names], small_rows)
    early_names = ['ffn_conv_w', 'b_norm_g', 'c_norm_g', 'd_norm_g']
    late_names = [n for n in small_names if n not in early_names]
    packs = []
    for names in (early_names, late_names):
        shapes = [w_loc[n].shape for n in names]
        packs.append((names, shapes, _pack([w_loc[n] for n in names], _padded_rows(shapes, 8))))
    wa_in, wa_out = all_gather([a_w_in[0].astype(BF16), a_w_out[0].astype(BF16)], "gather_a")
    mixers = [None, (b_w_in, b_w_out), (c_w_in, c_w_out), (d_w_in, d_w_out)]
    groups = {}
    for l in range(depth):
        if l > 0:
            groups[f'mixer{l}'] = [mixers[l][0][0].astype(BF16), mixers[l][1][0].astype(BF16)] + ([packs[1][2]] if l == 1 else [])
        groups[f'ffn{l}'] = [ffn_w_up[l].astype(BF16), ffn_w_down[l].astype(BF16)] + ([packs[0][2]] if l == 0 else [])
    two_level = ('ffn0', 'mixer1', 'ffn1')
    in_flight = dict(zip(groups, gather_start(list(groups.values()),
                                              [SAME_CORE_PEERS if k in two_level else ALL_PEERS for k in groups],
                                              "gather_start", (wa_in, wa_out))))
    forwarding = {}

    def forward_on(key, after):
        forwarding[key] = forward_start(gather_wait(in_flight[key], after, f"gather_wait_{key}"), f"forward_start_{key}")
        return (forwarding[key][2],)

    def gathered(key, after):
        if key in two_level:
            return forward_wait(forwarding[key], after, f"forward_wait_{key}")
        return gather_wait(in_flight[key], after, f"gather_wait_{key}")

    sm = {}

    def unpack_small(pack, gathered):
        names, shapes, _ = pack
        sm.update((n, _to_full(s, SMALL_SHARDED[n])) for n, s in zip(names, _unpack(gathered, shapes, (N_DEV,))))

    def rows_full(st):
        return st.reshape((st.shape[0] * st.shape[1],) + st.shape[2:])

    nb = N_DEV
    ffn_cb = [ffn_conv_b[l].reshape(2, nb // 2, 1, -1) for l in range(depth)]
    ffn_g = [ffn_norm_g[l:l + 1] for l in range(depth)]
    a_bst = a_b_s[0].T
    w_up, w_down, saved = [None] * depth, [None] * depth, {}

    def ffn_forward(xl, hl, l, next_g):
        up, down, *early = gathered(f'ffn{l}', (xl,))
        if early:
            unpack_small(packs[0], early[0])
            ffn_cw.extend(sm['ffn_conv_w'][k].reshape(ffn_conv_w.shape[1], 2, nb // 2, -1).transpose(1, 2, 0, 3)
                          for k in range(depth))
        w_up[l], w_down[l] = up, rows_full(down)
        return _ffn_forward(xl, hl, w_up[l], ffn_cw[l], ffn_cb[l], w_down[l], next_g(), l)

    ffn_cw = []
    wa_out = rows_full(wa_out)
    h = rms_fwd(xs, a_norm_g, "a_norm")
    z = mm_in(h, wa_in, "a_in", bias=a_b_in)
    y = a_mid_fwd(z, a_v_norm_g, a_w_s[0], a_bst, "a_mid")
    x1, hf = mm_out(y, wa_out, xs, ffn_g[0], "a_out", forward_on('ffn0', (y,)))
    saved['a'] = (xs, h, z, y)
    forward_on('mixer1', (x1,))
    (x1, h), saved['f0'] = ffn_forward(x1, hf, 0, lambda: sm['b_norm_g'])

    forward_on('ffn1', (x1,))
    wb_in, wb_out, late = gathered('mixer1', (x1,))
    unpack_small(packs[1], late)
    b_wgrp, c_wa, c_wi = (sm[n][0].astype(BF16) for n in ('b_w_grp', 'c_w_a', 'c_w_i'))
    b_bgrp, c_ba, c_bi = (sm[n][0].reshape(1, -1) for n in ('b_b_grp', 'c_b_a', 'c_b_i'))
    wb_in, wb_out = rows_full(wb_in)[None], rows_full(wb_out)
    z = mm_in(h, wb_in, "b_in")
    y, x2, hf = b_mid_fwd(z, b_wgrp, b_bgrp, sm['b_scale'], wb_out, x1, ffn_g[1], "b_mid_out")
    saved['b'] = (x1, h, z, y)
    (x2, h), saved['f1'] = ffn_forward(x2, hf, 1, lambda: sm['c_norm_g'])

    wc_in, wc_out = gathered('mixer2', (x2,))
    wc_out = rows_full(wc_out)
    z = mm_in(h, wc_in, "c_in", bias=sm['c_b_in'])
    c_cw = sm['c_conv_w'][0]
    a_seq, b_seq, xr = c_mid_fwd(z, c_cw, sm['c_conv_b'], c_wa, c_ba, c_wi, c_bi, sm['c_lambda'], "c_mid")
    hs, y, x3, hf = c_scan_fwd(a_seq, b_seq, z, wc_out, x2, ffn_g[2], "c_scan_out")
    saved['c'] = (x2, h, z, y, a_seq, xr, hs)
    (x3, h), saved['f2'] = ffn_forward(x3, hf, 2, lambda: sm['d_norm_g'])

    wd_in, wd_out = gathered('mixer3', (x3,))
    wd_out = rows_full(wd_out)
    z = mm_in(h, wd_in, "d_in")
    d_cw = sm['d_conv_w'][0]
    y, x4, hf = d_mid_fwd(z, d_cw, wd_out, x3, ffn_g[3], "d_mid_out")
    saved['d'] = (x3, h, z, y)
    (x4, _), saved['f3'] = ffn_forward(x4, hf, 3, lambda: None)

    loss_part, dx, dxb, d_final_g = final_loss(x4, final_norm_g.reshape(1, -1), target, "final_loss")
    loss = lax.psum(loss_part[0, 0], ("x", "y", "c"))

    def rows_stacked(full):
        return full.reshape((N_DEV, full.shape[0] // N_DEV) + full.shape[1:])

    mx, my, mc = _place()
    own = jnp.stack([_dev_index(*_chip_of(mx, my, k), mc) for k in range(4)]).astype(jnp.int32)
    repl_shapes = [w_loc[n].shape for n in REPLICATED]
    repl_rows = _padded_rows(repl_shapes, 8 * N_DEV)
    shard_of = {n: (w_loc[n][0], m_loc[n][0], v_loc[n][0])
                for n in ('a_w_in', 'a_w_out', 'b_w_in', 'b_w_out', 'c_w_in', 'c_w_out', 'd_w_in', 'd_w_out')}
    shard_of['small'] = (small_packed, _pack([m_loc[n] for n in small_names], small_rows),
                         _pack([v_loc[n] for n in small_names], small_rows))
    updated = {}

    def finish(n, part, others):
        if n == 'repl':
            chunk = sum_parts(part, others, "rs_sum_repl")
            repl_g = all_gather([chunk], "gather_repl")[0].reshape(repl_rows, LANES)
            updated[n] = adamw(*(_pack([src[k] for k in REPLICATED], repl_rows) for src in (w_loc, m_loc, v_loc)),
                               "adamw_repl", g=repl_g)
        elif n.startswith('ffn_w'):
            base, l = n[:-1], int(n[-1])
            turn = (lambda t: jnp.swapaxes(t, 1, 2)) if base == 'ffn_w_up' else (lambda t: t)
            updated[base] = adamw_layer(turn(w_loc[base]), turn(m_loc[base]), turn(v_loc[base]), l, updated.get(base),
                                        f"adamw_{n}", part, others)
            return updated[base][1]
        else:
            updated[n] = adamw(*shard_of[n], f"adamw_{n}", p=part, l2=others)
        return updated[n][1]

    stages = [None, []]

    def advance(tag, new, after, finish_older=True):
        behind = []
        first = None
        if new:
            first = ([n for n, _ in new], sibling_start([g for _, g in new], f"rs_sibling_start{tag}"))
            behind.append(first[1][3])
        older = stages[1]
        if finish_older:
            for k, (names, started) in enumerate(older):
                parts, others = chips_wait(started, after, f"rs_chips_wait{tag}_{k}")
                behind += [finish(n, p, o) for n, p, o in zip(names, parts, others)]
            older = []
        if stages[0] is not None:
            names, started = stages[0]
            grads, got = sibling_wait(started, after, f"rs_sibling_wait{tag}")
            parts = [add_pairs(g, l, own, f"rs_add_{n}") for n, g, l in zip(names, grads, got)]
            older = older + [(names, chips_start(parts, f"rs_chips_start{tag}"))]
            behind.append(older[-1][1][3])
        stages[:] = [first, older]
        return tuple(behind)

    gf = [None] * depth
    dx, dxb, gf[3] = _ffn_backward(dx, dxb, saved['f3'], ffn_g[3], w_up[3], ffn_cw[3], w_down[3], 3)
    xin, h, z, y = saved['d']
    dy = mm_dx_out(dxb, wd_out, "d_out_dx")
    g_d_w_out = mm_dw_out(y, dxb, "d_out_dw")
    dz, g_d_conv_w = d_mid_bwd(z, dy, d_cw, "d_mid_bwd")
    dx, dxb, g_d_norm_g = mm_dx_in(dz, wd_in, "d_in_dx", norm=(xin, sm['d_norm_g'], dx))
    g_d_w_in = mm_dw_in(h, dz, nb, "d_in_dw")
    behind = advance(0, [('ffn_w_up3', gf[3]['w_up']), ('ffn_w_down3', rows_stacked(gf[3]['w_down'])), ('d_w_in', g_d_w_in),
                         ('d_w_out', rows_stacked(g_d_w_out))], (dx,))

    dx, dxb, gf[2] = _ffn_backward(dx, dxb, saved['f2'], ffn_g[2], w_up[2], ffn_cw[2], w_down[2], 2, behind)
    xin, h, z, y, a_seq, xr, hs = saved['c']
    g_c_w_out = mm_dw_out(y, dxb, "c_out_dw")
    lam_seq, da_seq, dgate, dgate_sum = c_scan_bwd(dxb, wc_out, z, hs, a_seq, "c_scan_bwd")
    dxr, g_c_w_a, g_c_w_i, g_c_b_a, g_c_b_i, g_c_lambda = c_mid_bwd(
        lam_seq, da_seq, xr, c_wa, c_ba, c_wi, c_bi, sm['c_lambda'], "c_mid_bwd")
    dxr_pre, g_c_conv_w, g_c_conv_b, dxr_pre_sum = conv_bwd(dxr, z, 1, c_cw, "c_conv_bwd")
    dz = jnp.concatenate([dgate, dxr_pre], axis=1)
    g_c_b_in = jnp.concatenate([dgate_sum, dxr_pre_sum], axis=1)
    dx, dxb, g_c_norm_g = mm_dx_in(dz, wc_in, "c_in_dx", norm=(xin, sm['c_norm_g'], dx))
    g_c_w_in = mm_dw_in(h, dz, nb, "c_in_dw")
    behind = advance(1, [('ffn_w_up2', gf[2]['w_up']), ('ffn_w_down2', rows_stacked(gf[2]['w_down'])), ('c_w_in', g_c_w_in),
                         ('c_w_out', rows_stacked(g_c_w_out))], (dx,))

    dx, dxb, gf[1] = _ffn_backward(dx, dxb, saved['f1'], ffn_g[1], w_up[1], ffn_cw[1], w_down[1], 1, behind)
    xin, h, z, y = saved['b']
    g_b_w_out = mm_dw_out(y, dxb, "b_out_dw")
    dp, g_b_w_grp, g_b_b_grp, g_b_scale = b_mid_bwd(z, dxb, wb_out, b_wgrp, b_bgrp, sm['b_scale'], "b_mid_bwd")
    dz = b_pool_bwd(dp, "b_pool_bwd")
    dx, dxb, g_b_norm_g = mm_dx_in(dz, wb_in, "b_in_dx", norm=(xin, sm['b_norm_g'], dx))
    g_b_w_in = mm_dw_in(h, dz, 1, "b_in_dw")
    behind = advance(2, [('ffn_w_up1', gf[1]['w_up']), ('ffn_w_down1', rows_stacked(gf[1]['w_down'])),
                         ('b_w_in', rows_stacked(g_b_w_in[0])), ('b_w_out', rows_stacked(g_b_w_out))], (dx,))

    dx, dxb, gf[0] = _ffn_backward(dx, dxb, saved['f0'], ffn_g[0], w_up[0], ffn_cw[0], w_down[0], 0, behind)
    full_small = {
        'b_norm_g': g_b_norm_g, 'b_w_grp': g_b_w_grp[None], 'b_b_grp': g_b_b_grp.reshape(b_b_grp.shape[:2] + (-1,)),
        'b_scale': g_b_scale, 'c_norm_g': g_c_norm_g, 'c_b_in': g_c_b_in, 'c_conv_w': g_c_conv_w[None],
        'c_conv_b': g_c_conv_b, 'c_w_a': g_c_w_a[None], 'c_b_a': g_c_b_a.reshape(c_b_a.shape[:2] + (-1,)),
        'c_w_i': g_c_w_i[None], 'c_b_i': g_c_b_i.reshape(c_b_i.shape[:2] + (-1,)), 'c_lambda': g_c_lambda,
        'd_norm_g': g_d_norm_g, 'd_conv_w': g_d_conv_w[None],
        'ffn_conv_w': jnp.stack([gf[l]['conv_w'].transpose(2, 0, 1, 3).reshape(ffn_conv_w.shape[1], -1) for l in range(depth)])}
    small_grads = _pack_stacked([_to_stacked(full_small[n], SMALL_SHARDED[n]) for n in small_names], small_rows)
    behind = advance(3, [('ffn_w_up0', gf[0]['w_up']), ('ffn_w_down0', rows_stacked(gf[0]['w_down'])), ('small', small_grads)],
                     (dx,))

    xin, h, z, y = saved['a']
    g_a_w_out = mm_dw_out(y, dxb, "a_out_dw")
    behind += advance(4, [('a_w_out', rows_stacked(g_a_w_out))], (dxb, *behind))
    dz, g_a_b_in, g_a_v_norm_g, g_a_w_s, g_a_b_s = a_mid_bwd(z, dxb, wa_out, a_v_norm_g, a_w_s[0], a_bst, "a_mid_bwd", behind)
    g_a_w_in = mm_dw_in(h, dz, nb, "a_in_dw")
    behind = advance(5, [('a_w_in', g_a_w_in)], (dz,))
    behind = advance('5b', [], (dz, *behind), finish_older=False)
    dx, _, g_a_norm_g = mm_dx_in(dz, wa_in, "a_in_dx", behind, norm=(xin, a_norm_g, dx))
    grad_x = dx[None]

    tril = jnp.tril(jnp.ones((A_CHUNK, A_CHUNK), bool))
    repl_full = {
        'a_norm_g': g_a_norm_g, 'a_b_in': g_a_b_in, 'a_v_norm_g': g_a_v_norm_g,
        'a_w_s': jnp.where(tril, g_a_w_s, 0.0)[None], 'a_b_s': g_a_b_s[:, ::LANES].T[None],
        'ffn_norm_g': jnp.concatenate([gf[l]['norm_g'] for l in range(depth)], axis=0),
        'ffn_conv_b': jnp.stack([gf[l]['conv_b'].reshape(-1) for l in range(depth)]), 'final_norm_g': d_final_g.reshape(-1)}
    repl_grads = _pack([repl_full[n] for n in REPLICATED], repl_rows).reshape(N_DEV, repl_rows // N_DEV, LANES)
    behind = advance(6, [('repl', repl_grads)], (dx,))
    behind = advance(7, [], (dx, *behind))
    advance(8, [], (dx, *behind))

    outs = [{}, {}, {}, {}]
    for i, dst in enumerate(outs):
        for n in ('a_w_in', 'a_w_out', 'b_w_in', 'b_w_out', 'c_w_in', 'c_w_out', 'd_w_in', 'd_w_out'):
            dst[n] = updated[n][i][None]
        dst['ffn_w_up'] = jnp.swapaxes(updated['ffn_w_up'][i], 1, 2)
        dst['ffn_w_down'] = updated['ffn_w_down'][i]
        dst.update(zip(small_names, _unpack(updated['small'][i], small_shapes)))
        dst.update(zip(REPLICATED, _unpack(updated['repl'][i], repl_shapes)))
    out_g, out_d, out_m, out_v = outs

    return (loss, grad_x, *[out_g[n] for n in WEIGHTS], *[out_d[n] for n in WEIGHTS], *[out_m[n] for n in WEIGHTS],
            *[out_v[n] for n in WEIGHTS])
```

```python
import functools

import jax
import jax.numpy as jnp
from jax import lax
from jax.experimental import pallas as pl
from jax.experimental.pallas import tpu as pltpu

F32, BF16 = jnp.float32, jnp.bfloat16
MESH_ID = pl.DeviceIdType.MESH
N_DEV = 8
V7X_VMEM_LIMIT_BYTES = 56 << 20
LANES = 128
HALO = 8
POOL_HALO = 16

EPS = 1e-6
A_CHUNK, A_GROUPS = 128, 4
B_WINDOWS = (2, 4, 8, 16)
C_GATE_C = 8.0
ADAM_LR, ADAM_B1, ADAM_B2, ADAM_EPS, ADAM_WD, ADAM_STEP = 0.001, 0.9, 0.999, 1e-08, 0.01, 10

WEIGHTS = ['a_norm_g', 'a_w_in', 'a_b_in', 'a_v_norm_g', 'a_w_s', 'a_b_s', 'a_w_out', 'b_norm_g', 'b_w_in', 'b_w_grp',
           'b_b_grp', 'b_scale', 'b_w_out', 'c_norm_g', 'c_w_in', 'c_b_in', 'c_conv_w', 'c_conv_b', 'c_w_a', 'c_b_a',
           'c_w_i', 'c_b_i', 'c_lambda', 'c_w_out', 'd_norm_g', 'd_w_in', 'd_conv_w', 'd_w_out', 'ffn_norm_g',
           'ffn_w_up', 'ffn_conv_w', 'ffn_conv_b', 'ffn_w_down', 'final_norm_g']
SMALL_SHARDED = {'b_norm_g': 1, 'b_w_grp': 2, 'b_b_grp': 2, 'b_scale': 1, 'c_norm_g': 1, 'c_b_in': 1, 'c_conv_w': 2,
                 'c_conv_b': 1, 'c_w_a': 2, 'c_b_a': 2, 'c_w_i': 2, 'c_b_i': 2, 'c_lambda': 1, 'd_norm_g': 1,
                 'd_conv_w': 2, 'ffn_conv_w': 2}
REPLICATED = ['a_norm_g', 'a_b_in', 'a_v_norm_g', 'a_w_s', 'a_b_s', 'ffn_norm_g', 'ffn_conv_b', 'final_norm_g']


_GELU_C0, _GELU_C1 = 0.7978845608028654, 0.044715


def _gelu(x):
    return 0.5 * x * (1.0 + jnp.tanh(_GELU_C0 * (x + _GELU_C1 * (x * x * x))))


def _gelu_grad(x):
    t = jnp.tanh(_GELU_C0 * (x + _GELU_C1 * (x * x * x)))
    return 0.5 * (1.0 + t) + 0.5 * x * (1.0 - t * t) * (_GELU_C0 * (1.0 + 3.0 * _GELU_C1 * (x * x)))


def _sigmoid(x):
    return jax.nn.sigmoid(x)


def _log1p(x):
    u = 1.0 + x
    return jnp.where(u == 1.0, x, jnp.log(u) * (x / (u - 1.0)))


def _softplus(x):
    return jnp.maximum(x, 0.0) + _log1p(jnp.exp(-jnp.abs(x)))


def _expm1(x):
    poly = x * (1.0 + x * (1 / 2) * (1.0 + x * (1 / 3) * (1.0 + x * (1 / 4) * (1.0 + x * (1 / 5) * (
        1.0 + x * (1 / 6) * (1.0 + x * (1 / 7) * (1.0 + x * (1 / 8))))))))
    return jnp.where(jnp.abs(x) < 0.35, poly, jnp.exp(x) - 1.0)


def _down(xe, s):
    return xe if s == 0 else pltpu.roll(xe, s, 0)


def _up(xe, s):
    return xe if s == 0 else pltpu.roll(xe, xe.shape[0] - s, 0)


def _conv_ext(xe, w, taps):
    y = xe * w[taps - 1:taps]
    for s in range(1, taps):
        y = y + _down(xe, s) * w[taps - 1 - s:taps - s]
    return y


def _acc(ref, val, first):
    @pl.when(first)
    def _():
        ref[...] = val

    @pl.when(jnp.logical_not(first))
    def _():
        ref[...] += val


def _colsum(v):
    return jnp.sum(v, axis=0, keepdims=True)


def _dot(a, b, dims=((1,), (0,))):
    return lax.dot_general(a.astype(BF16), b.astype(BF16), (dims, ((), ())), preferred_element_type=F32)


_NN, _NT, _TN = ((1,), (0,)), ((1,), (1,)), ((0,), (0,))


def _call(body, name, grid, in_specs, out_specs, out_shape, scratch=(), after=()):
    n_in, n_after = len(in_specs), len(after)

    def ordered_body(*refs):
        return body(*refs[:n_in], *refs[n_in + n_after:])

    call = pl.pallas_call(
        ordered_body if n_after else body, name=name, grid=grid,
        in_specs=list(in_specs) + [pl.BlockSpec(memory_space=pl.ANY)] * n_after, out_specs=out_specs,
        out_shape=out_shape, scratch_shapes=list(scratch),
        compiler_params=pltpu.CompilerParams(dimension_semantics=("arbitrary",) * len(grid),
                                             vmem_limit_bytes=V7X_VMEM_LIMIT_BYTES))
    return lambda *args: call(*args, *after)


def _rows(m, t):
    t = min(m, t)
    assert m % t == 0, (m, t)
    return t


def _sds(shape, dtype=F32):
    return jax.ShapeDtypeStruct(tuple(shape), dtype)


def _prev_halo(tm, halo=HALO):
    return lambda i: jnp.maximum(i * (tm // halo) - 1, 0)


def _next_halo(tm, m, halo=HALO):
    return lambda i: jnp.minimum((i + 1) * (tm // halo), m // halo - 1)


def _matmul(name, ins, in_specs, out_shape, o_spec, grid, compute, after=()):
    def body(*refs):
        refs[-1][...] = compute(*refs[:-1]).astype(refs[-1].dtype)

    return _call(body, name, grid, in_specs, o_spec, out_shape, (), after)(*ins)


def mm_in(h, w_st, name, bias=None, stacked_out=False, out_dtype=F32, rows=1024):
    m, k = h.shape
    nb, _, n = w_st.shape
    tm = _rows(m, rows)
    in_specs = [pl.BlockSpec((tm, k), lambda i, j: (i, 0)), pl.BlockSpec((None, k, n), lambda i, j: (j, 0, 0))]
    if stacked_out:
        out, o_spec = _sds((nb, m, n), out_dtype), pl.BlockSpec((None, tm, n), lambda i, j: (j, i, 0))
    else:
        out, o_spec = _sds((m, nb * n), out_dtype), pl.BlockSpec((tm, n), lambda i, j: (i, j))
    if bias is None:
        return _matmul(name, (h, w_st), in_specs, out, o_spec, (m // tm, nb), lambda a, b: _dot(a[...], b[...]))
    in_specs.append(pl.BlockSpec((1, n), lambda i, j: (0, j)))
    return _matmul(name, (h, w_st, bias), in_specs, out, o_spec, (m // tm, nb),
                   lambda a, b, c: _dot(a[...], b[...]) + c[...])


def _split_rows(kf):
    g = max(1, kf // 1024)
    return g, kf // g


def _resident(shape):
    return pl.BlockSpec(shape, lambda *_: (0,) * len(shape), pipeline_mode=pl.Buffered(1))


def _norm_rows(xv, g):
    return (xv * lax.rsqrt(jnp.mean(xv * xv, axis=-1, keepdims=True) + EPS) * g).astype(BF16)


def _project_out(y_ref, w_ref, res_ref, g_ref, x_ref, h_ref):
    xv = res_ref[...] + _dot(y_ref[...], w_ref[...])
    x_ref[...] = xv
    h_ref[...] = _norm_rows(xv, g_ref[...])


def _call_projected(body, name, m, tm, in_specs, ins, own_outs, w_out, res, next_g, scratch=()):
    n = w_out.shape[1]
    row = lambda width: pl.BlockSpec((tm, width), lambda i: (i, 0))
    specs = list(in_specs) + [_resident(w_out.shape), row(n), pl.BlockSpec((1, n), lambda i: (0, 0))]
    out_specs = [row(width) for width, _ in own_outs] + [row(n), row(n)]
    out_shape = [_sds((m, width), dtype) for width, dtype in own_outs] + [_sds((m, n)), _sds((m, n), BF16)]
    return _call(body, name, (m // tm,), specs, out_specs, out_shape, scratch)(*ins, w_out, res, next_g)


def mm_out(y, w, res, next_g, name, after=()):
    kf, n = w.shape
    m = y.shape[0]
    tm = _rows(m, 512)
    row = pl.BlockSpec((tm, n), lambda i: (i, 0))

    def body(y_ref, w_ref, res_ref, g_ref, x_ref, h_ref):
        xv = res_ref[...] + _dot(y_ref[...], w_ref[...])
        x_ref[...] = xv
        h_ref[...] = _norm_rows(xv, g_ref[...])

    in_specs = [pl.BlockSpec((tm, kf), lambda i: (i, 0)), _resident((kf, n)), row, pl.BlockSpec((1, n), lambda i: (0, 0))]
    return _call(body, name, (m // tm,), in_specs, [row, row], [_sds((m, n)), _sds((m, n), BF16)], (), after)(y, w, res, next_g)


def mm_dx_in(dz, w_st, name, after=(), norm=None):
    nb, k, n = w_st.shape
    m = dz.shape[-2]
    tm = _rows(m, 512)
    w_spec = _resident((nb, k, n))
    if dz.ndim == 3:
        in_specs = [pl.BlockSpec((None, tm, n), lambda i, r=r: (r, i, 0)) for r in range(nb)] + [w_spec]

        def compute(*refs):
            acc = _dot(refs[0][...], refs[nb][0], _NT)
            for r in range(1, nb):
                acc = acc + _dot(refs[r][...], refs[nb][r], _NT)
            return acc

        ins = (*[dz] * nb, w_st)
    else:
        in_specs = [pl.BlockSpec((tm, nb * n), lambda i: (i, 0)), w_spec]

        def compute(dz_ref, w_ref):
            acc = _dot(dz_ref[:, :n], w_ref[0], _NT)
            for r in range(1, nb):
                acc = acc + _dot(dz_ref[:, r * n:(r + 1) * n], w_ref[r], _NT)
            return acc

        ins = (dz, w_st)
    row = pl.BlockSpec((tm, k), lambda i: (i, 0))
    if norm is None:
        return _matmul(name, ins, in_specs, _sds((m, k)), row, (m // tm,), compute, after)
    n_in = len(in_specs)
    vec = pl.BlockSpec((1, k), lambda i: (0, 0))

    def body(*refs):
        x_ref, g_ref, dr_ref, dx_ref, dxb_ref, dg_ref = refs[n_in:]
        dx, dg = _rms_bwd_math(x_ref[...], g_ref[...], compute(*refs[:n_in]))
        dx = dr_ref[...] + dx
        dx_ref[...] = dx
        dxb_ref[...] = dx.astype(BF16)
        _acc(dg_ref, dg, pl.program_id(0) == 0)

    return _call(body, name, (m // tm,), in_specs + [row, vec, row], [row, row, vec],
                 [_sds((m, k)), _sds((m, k), BF16), _sds((1, k))], (), after)(*ins, *norm)


def mm_dx_out(dout, w, name, groups=None, after=()):
    kf, n = w.shape
    m = dout.shape[0]
    tm = _rows(m, 1024)
    g, k = (groups, kf // groups) if groups else _split_rows(kf)
    in_specs = [pl.BlockSpec((tm, n), lambda i, j: (i, 0)), pl.BlockSpec((None, k, n), lambda i, j: (j, 0, 0))]
    if groups:
        out, o_spec = _sds((g, m, k)), pl.BlockSpec((None, tm, k), lambda i, j: (j, i, 0))
    else:
        out, o_spec = _sds((m, kf)), pl.BlockSpec((tm, k), lambda i, j: (i, j))
    return _matmul(name, (dout, w.reshape(g, k, n)), in_specs, out, o_spec, (m // tm, g),
                   lambda a, b: _dot(a[...], b[...], _NT), after)


def mm_dw_in(h, dz, nb, name, transposed=False):
    m, k = h.shape
    if dz.ndim == 3:
        n = dz.shape[2]
        dz_spec = pl.BlockSpec((None, m, n), lambda j: (j, 0, 0))
    else:
        n = dz.shape[1] // nb
        dz_spec = pl.BlockSpec((m, n), lambda j: (0, j))
    in_specs = [_resident((m, k)), dz_spec]
    if transposed:
        return _matmul(name, (h, dz), in_specs, _sds((nb, n, k), BF16), pl.BlockSpec((None, n, k), lambda j: (j, 0, 0)), (nb,),
                       lambda a, b: _dot(b[...], a[...], _TN))
    return _matmul(name, (h, dz), in_specs, _sds((nb, k, n), BF16), pl.BlockSpec((None, k, n), lambda j: (j, 0, 0)), (nb,),
                   lambda a, b: _dot(a[...], b[...], _TN))


def mm_dw_out(y, dout, name):
    m, n = dout.shape
    if y.ndim == 3:
        g, _, k = y.shape
        y_spec = pl.BlockSpec((None, m, k), lambda j: (j, 0, 0))
    else:
        g, k = _split_rows(y.shape[1])
        y_spec = pl.BlockSpec((m, k), lambda j: (0, j))
    in_specs = [y_spec, _resident((m, n))]
    out = _matmul(name, (y, dout), in_specs, _sds((g, k, n), BF16), pl.BlockSpec((None, k, n), lambda j: (j, 0, 0)), (g,),
                  lambda a, b: _dot(a[...], b[...], _TN))
    return out.reshape(g * k, n)


def rms_fwd(x, g, name):
    m, d = x.shape
    tm = _rows(m, 512)

    def body(x_ref, g_ref, o_ref):
        xv = x_ref[...]
        rstd = lax.rsqrt(jnp.mean(xv * xv, axis=-1, keepdims=True) + EPS)
        o_ref[...] = (xv * rstd * g_ref[...]).astype(BF16)

    row = pl.BlockSpec((tm, d), lambda i: (i, 0))
    vec = pl.BlockSpec((1, d), lambda i: (0, 0))
    return _call(body, name, (m // tm,), [row, vec], row, _sds((m, d), BF16))(x, g)


def _rms_bwd_math(xv, g, dh):
    rstd = lax.rsqrt(jnp.mean(xv * xv, axis=-1, keepdims=True) + EPS)
    xhat = xv * rstd
    dxhat = dh * g
    dx = rstd * (dxhat - xhat * jnp.mean(dxhat * xhat, axis=-1, keepdims=True))
    return dx, _colsum(dh * xhat)


def final_loss(x, g, target, name):
    m, d = x.shape
    tm = _rows(m, 512)

    def body(x_ref, g_ref, t_ref, l_ref, dx_ref, dxb_ref, dg_ref):
        xv, gv = x_ref[...], g_ref[...]
        rstd = lax.rsqrt(jnp.mean(xv * xv, axis=-1, keepdims=True) + EPS)
        err = xv * rstd * gv - t_ref[...]
        part = 0.5 * jnp.sum(jnp.mean(err * err, axis=-1, keepdims=True), axis=0, keepdims=True)
        dx, dg = _rms_bwd_math(xv, gv, err * (1.0 / d))
        dx_ref[...] = dx
        dxb_ref[...] = dx.astype(BF16)
        first = pl.program_id(0) == 0
        _acc(l_ref, jnp.broadcast_to(part, l_ref.shape), first)
        _acc(dg_ref, dg, first)

    row = pl.BlockSpec((tm, d), lambda i: (i, 0))
    vec = pl.BlockSpec((1, d), lambda i: (0, 0))
    lsp = pl.BlockSpec((1, LANES), lambda i: (0, 0))
    return _call(body, name, (m // tm,), [row, vec, row], [lsp, row, row, vec],
                 [_sds((1, LANES)), _sds((m, d)), _sds((m, d), BF16), _sds((1, d))])(x, g, target)


def _a_common(z_ref, vg_ref, ws_ref, bst_ref, tm, width):
    gw = width // A_GROUPS
    zp = z_ref[...]
    z = _gelu(zp)
    u, v = z[:, :width], z[:, width:]
    rstd = lax.rsqrt(jnp.mean(v * v, axis=-1, keepdims=True) + EPS)
    vhat = v * rstd
    vn = vhat * vg_ref[...]
    t_i = lax.broadcasted_iota(jnp.int32, (A_CHUNK, A_CHUNK), 0)
    s_i = lax.broadcasted_iota(jnp.int32, (A_CHUNK, A_CHUNK), 1)
    wsm = [jnp.where(s_i <= t_i, ws_ref[g], 0.0).astype(BF16) for g in range(A_GROUPS)]
    bst = bst_ref[...]
    return zp, u, rstd, vhat, vn.astype(BF16), wsm, bst, gw


def a_mid_fwd(z, vg, ws, bst, name):
    m, w2 = z.shape
    width = w2 // 2
    tm = _rows(m, 256)

    def body(z_ref, vg_ref, ws_ref, bst_ref, y_ref):
        _, u, _, _, vnb, wsm, bst, gw = _a_common(z_ref, vg_ref, ws_ref, bst_ref, tm, width)
        for c in range(tm // A_CHUNK):
            r0 = c * A_CHUNK
            for g in range(A_GROUPS):
                c0 = g * gw
                vs = _dot(wsm[g], vnb[r0:r0 + A_CHUNK, c0:c0 + gw]) + bst[:, g:g + 1]
                y_ref[r0:r0 + A_CHUNK, c0:c0 + gw] = (u[r0:r0 + A_CHUNK, c0:c0 + gw] * vs).astype(BF16)

    in_specs = [pl.BlockSpec((tm, w2), lambda i: (i, 0)), pl.BlockSpec((1, width), lambda i: (0, 0)),
                pl.BlockSpec((A_GROUPS, A_CHUNK, A_CHUNK), lambda i: (0, 0, 0)),
                pl.BlockSpec((A_CHUNK, A_GROUPS), lambda i: (0, 0))]
    return _call(body, name, (m // tm,), in_specs, pl.BlockSpec((tm, width), lambda i: (i, 0)),
                 _sds((m, width), BF16))(z, vg, ws, bst)


def a_mid_bwd(z, dx, w_out, vg, ws, bst, name, after=()):
    m, w2 = z.shape
    width = w2 // 2
    tm = _rows(m, 256)

    def body(z_ref, dx_ref, wo_ref, vg_ref, ws_ref, bst_ref, dz_ref, dbin_ref, dvg_ref, dws_ref, dbs_ref, dvn_scr, du_scr):
        first = pl.program_id(0) == 0
        zp, u, rstd, vhat, vnb, wsm, bst, gw = _a_common(z_ref, vg_ref, ws_ref, bst_ref, tm, width)
        dy = _dot(dx_ref[...], wo_ref[...], _NT)
        dws = [jnp.zeros((A_CHUNK, A_CHUNK), F32) for _ in range(A_GROUPS)]
        dbs = [jnp.zeros((A_CHUNK, 1), F32) for _ in range(A_GROUPS)]
        for c in range(tm // A_CHUNK):
            r0 = c * A_CHUNK
            for g in range(A_GROUPS):
                c0 = g * gw
                vn_cg = vnb[r0:r0 + A_CHUNK, c0:c0 + gw]
                vs = _dot(wsm[g], vn_cg) + bst[:, g:g + 1]
                dy_cg = dy[r0:r0 + A_CHUNK, c0:c0 + gw]
                dvs = dy_cg * u[r0:r0 + A_CHUNK, c0:c0 + gw]
                du_scr[r0:r0 + A_CHUNK, c0:c0 + gw] = dy_cg * vs
                dws[g] = dws[g] + _dot(dvs, vn_cg, _NT)
                dbs[g] = dbs[g] + jnp.sum(dvs, axis=1, keepdims=True)
                dvn_scr[r0:r0 + A_CHUNK, c0:c0 + gw] = _dot(wsm[g], dvs, _TN)
        for g in range(A_GROUPS):
            _acc(dws_ref.at[g], dws[g], first)
            _acc(dbs_ref.at[:, g * LANES:(g + 1) * LANES], jnp.broadcast_to(dbs[g], (A_CHUNK, LANES)), first)
        dvn = dvn_scr[...]
        _acc(dvg_ref, _colsum(dvn * vhat), first)
        dvhat = dvn * vg_ref[...]
        dv = rstd * (dvhat - vhat * jnp.mean(dvhat * vhat, axis=-1, keepdims=True))
        gg = _gelu_grad(zp)
        dzu = du_scr[...] * gg[:, :width]
        dzv = dv * gg[:, width:]
        dz_ref[:, :width] = dzu.astype(BF16)
        dz_ref[:, width:] = dzv.astype(BF16)
        _acc(dbin_ref.at[:, :width], _colsum(dzu), first)
        _acc(dbin_ref.at[:, width:], _colsum(dzv), first)

    const2 = lambda i: (0, 0)
    in_specs = [pl.BlockSpec((tm, w2), lambda i: (i, 0)), pl.BlockSpec((tm, dx.shape[1]), lambda i: (i, 0)),
                _resident(w_out.shape), pl.BlockSpec((1, width), const2),
                pl.BlockSpec((A_GROUPS, A_CHUNK, A_CHUNK), lambda i: (0, 0, 0)), pl.BlockSpec((A_CHUNK, A_GROUPS), const2)]
    out_specs = [pl.BlockSpec((tm, w2), lambda i: (i, 0)), pl.BlockSpec((1, w2), const2), pl.BlockSpec((1, width), const2),
                 pl.BlockSpec((A_GROUPS, A_CHUNK, A_CHUNK), lambda i: (0, 0, 0)),
                 pl.BlockSpec((A_CHUNK, A_GROUPS * LANES), const2)]
    out_shape = [_sds((m, w2), BF16), _sds((1, w2)), _sds((1, width)), _sds((A_GROUPS, A_CHUNK, A_CHUNK)),
                 _sds((A_CHUNK, A_GROUPS * LANES))]
    scratch = [pltpu.VMEM((tm, width), F32), pltpu.VMEM((tm, width), F32)]
    return _call(body, name, (m // tm,), in_specs, out_specs, out_shape, scratch, after)(z, dx, w_out, vg, ws, bst)


def _pool_minus_id(ze, i, tm, gw):
    pos = i * tm + lax.broadcasted_iota(jnp.int32, (tm, 1), 0)
    out = []
    for gi, win in enumerate(B_WINDOWS):
        s = ze[:, gi * gw:(gi + 1) * gw]
        step = 1
        while step < win:
            s = s + _down(s, step)
            step *= 2
        inv = 1.0 / jnp.minimum(pos + 1, win).astype(F32)
        out.append(s[POOL_HALO:] * inv - ze[POOL_HALO:, gi * gw:(gi + 1) * gw])
    return out


def _b_specs(tm, width):
    return [pl.BlockSpec((POOL_HALO, width), lambda i: (_prev_halo(tm, POOL_HALO)(i), 0)),
            pl.BlockSpec((tm, width), lambda i: (i, 0))]


def b_mid_fwd(z, wgrp, bgrp, scale, w_out, res, next_g, name):
    m, width = z.shape
    ng = len(B_WINDOWS)
    gw = width // ng
    tm = _rows(m, 512)

    def body(zp_ref, zm_ref, w_ref, b_ref, s_ref, *proj):
        i = pl.program_id(0)
        y_ref = proj[3]
        ze = jnp.concatenate([zp_ref[...] * (i > 0).astype(F32), zm_ref[...]], axis=0)
        p = _pool_minus_id(ze, i, tm, gw)
        for g in range(ng):
            cs = slice(g * gw, (g + 1) * gw)
            y = (_dot(p[g], w_ref[g]) + b_ref[:, cs]) * s_ref[:, cs]
            y_ref[:, cs] = y.astype(BF16)
        _project_out(y_ref, *proj[:3], *proj[4:])

    vec = pl.BlockSpec((1, width), lambda i: (0, 0))
    in_specs = _b_specs(tm, width) + [pl.BlockSpec((ng, gw, gw), lambda i: (0, 0, 0)), vec, vec]
    return _call_projected(body, name, m, tm, in_specs, (z, z, wgrp, bgrp, scale), [(width, BF16)], w_out, res, next_g)


def b_mid_bwd(z, dx, w_out, wgrp, bgrp, scale, name):
    m, width = z.shape
    ng = len(B_WINDOWS)
    gw = width // ng
    tm = _rows(m, 512)

    def body(zp_ref, zm_ref, dx_ref, wo_ref, w_ref, b_ref, s_ref, dp_ref, dw_ref, db_ref, ds_ref):
        i = pl.program_id(0)
        first = i == 0
        ze = jnp.concatenate([zp_ref[...] * (i > 0).astype(F32), zm_ref[...]], axis=0)
        p = _pool_minus_id(ze, i, tm, gw)
        dy = _dot(dx_ref[...], wo_ref[...], _NT)
        for g in range(ng):
            cs = slice(g * gw, (g + 1) * gw)
            dyg = dy[:, cs]
            ypre = _dot(p[g], w_ref[g]) + b_ref[:, cs]
            dyp = dyg * s_ref[:, cs]
            _acc(ds_ref.at[:, cs], _colsum(dyg * ypre), first)
            _acc(db_ref.at[:, cs], _colsum(dyp), first)
            _acc(dw_ref.at[g], _dot(p[g], dyp, _TN), first)
            dp_ref[:, cs] = _dot(dyp, w_ref[g], _NT)

    vec = pl.BlockSpec((1, width), lambda i: (0, 0))
    row = pl.BlockSpec((tm, width), lambda i: (i, 0))
    wsp = pl.BlockSpec((ng, gw, gw), lambda i: (0, 0, 0))
    in_specs = _b_specs(tm, width) + [pl.BlockSpec((tm, dx.shape[1]), lambda i: (i, 0)), _resident(w_out.shape), wsp, vec, vec]
    return _call(body, name, (m // tm,), in_specs, [row, wsp, vec, vec],
                 [_sds((m, width)), _sds((ng, gw, gw)), _sds((1, width)), _sds((1, width))])(z, z, dx, w_out, wgrp, bgrp, scale)


def b_pool_bwd(dp, name):
    m, width = dp.shape
    gw = width // len(B_WINDOWS)
    tm = _rows(m, 512)
    n_i = m // tm

    def body(dm_ref, dn_ref, dz_ref):
        i = pl.program_id(0)
        de = jnp.concatenate([dm_ref[...], dn_ref[...] * (i < n_i - 1).astype(F32)], axis=0)
        pos = i * tm + lax.broadcasted_iota(jnp.int32, (tm + POOL_HALO, 1), 0)
        for gi, win in enumerate(B_WINDOWS):
            cs = slice(gi * gw, (gi + 1) * gw)
            d = de[:, cs]
            s = d * (1.0 / jnp.minimum(pos + 1, win).astype(F32))
            step = 1
            while step < win:
                s = s + _up(s, step)
                step *= 2
            dz_ref[:, cs] = (s[:tm] - d[:tm]).astype(BF16)

    in_specs = [pl.BlockSpec((tm, width), lambda i: (i, 0)),
                pl.BlockSpec((POOL_HALO, width), lambda i: (_next_halo(tm, m, POOL_HALO)(i), 0))]
    return _call(body, name, (n_i,), in_specs, pl.BlockSpec((tm, width), lambda i: (i, 0)), _sds((m, width), BF16))(dp, dp)


def _c_gates(xr, wa_ref, ba_ref, wi_ref, bi_ref, lam_ref, heads, hw):
    xb = xr.astype(BF16)
    ra = jnp.concatenate([_dot(xb[:, h * hw:(h + 1) * hw], wa_ref[h]) for h in range(heads)], axis=1) + ba_ref[...]
    ia = jnp.concatenate([_dot(xb[:, h * hw:(h + 1) * hw], wi_ref[h]) for h in range(heads)], axis=1) + bi_ref[...]
    r, ig = _sigmoid(ra), _sigmoid(ia)
    sp = _softplus(-lam_ref[...])
    log_a = (-C_GATE_C * r) * sp
    a = jnp.exp(log_a)
    mult = jnp.sqrt(-_expm1(2.0 * log_a))
    return xb, r, ig, sp, a, mult


def c_mid_fwd(z, cw, cb, wa, ba, wi, bi, lam, name):
    m, w2 = z.shape
    width = w2 // 2
    heads, hw = wa.shape[0], wa.shape[1]
    taps = cw.shape[0]
    tm = _rows(m, 512)

    def body(zp_ref, zm_ref, cw_ref, cb_ref, wa_ref, ba_ref, wi_ref, bi_ref, lam_ref, a_ref, b_ref, xr_ref):
        i = pl.program_id(0)
        xe = jnp.concatenate([zp_ref[...] * (i > 0).astype(F32), zm_ref[...]], axis=0)
        xr = _conv_ext(xe, cw_ref[...], taps)[HALO:] + cb_ref[...]
        _, _, ig, _, a, mult = _c_gates(xr, wa_ref, ba_ref, wi_ref, bi_ref, lam_ref, heads, hw)
        a_ref[...] = a
        b_ref[...] = mult * (ig * xr)
        xr_ref[...] = xr

    vec = pl.BlockSpec((1, width), lambda i: (0, 0))
    row = pl.BlockSpec((tm, width), lambda i: (i, 0))
    wsp = pl.BlockSpec((heads, hw, hw), lambda i: (0, 0, 0))
    in_specs = [pl.BlockSpec((HALO, width), lambda i: (_prev_halo(tm)(i), 1)), pl.BlockSpec((tm, width), lambda i: (i, 1)),
                pl.BlockSpec((taps, width), lambda i: (0, 0)), vec, wsp, vec, wsp, vec, vec]
    return _call(body, name, (m // tm,), in_specs, [row, row, row], [_sds((m, width))] * 3)(
        z, z, cw, cb, wa, ba, wi, bi, lam)


_SCAN_ROWS = 512


def c_scan_fwd(a, b, z, w_out, res, next_g, name):
    m, width = a.shape
    tm = _rows(m, _SCAN_ROWS)

    def body(a_ref, b_ref, g_ref, wo_ref, res_ref, ng_ref, hs_ref, y_ref, x_ref, h_ref, h_carry):
        @pl.when(pl.program_id(0) == 0)
        def _():
            h_carry[...] = jnp.zeros_like(h_carry)

        def step(t, h):
            h = a_ref[pl.ds(t, 1), :] * h + b_ref[pl.ds(t, 1), :]
            hs_ref[pl.ds(t, 1), :] = h
            return h

        h_carry[...] = lax.fori_loop(0, tm, step, h_carry[...], unroll=8)
        y_ref[...] = (hs_ref[...] * _gelu(g_ref[...])).astype(BF16)
        _project_out(y_ref, wo_ref, res_ref, ng_ref, x_ref, h_ref)

    row = pl.BlockSpec((tm, width), lambda i: (i, 0))
    return _call_projected(body, name, m, tm, [row, row, row], (a, b, z), [(width, F32), (width, BF16)], w_out, res, next_g,
                           [pltpu.VMEM((1, width), F32)])


def c_scan_bwd(dx, w_out, z, hs, a, name):
    m, width = a.shape
    tm = _rows(m, _SCAN_ROWS)
    n_i = m // tm

    def body(dx_ref, wo_ref, g_ref, hs_ref, hp_ref, a_ref, lam_ref, da_ref, dg_ref, dgs_ref, lam_carry, a_carry):
        i = pl.program_id(0)
        first = i == 0

        @pl.when(first)
        def _():
            lam_carry[...] = jnp.zeros_like(lam_carry)
            a_carry[...] = jnp.zeros_like(a_carry)

        gp, dyv, hsv = g_ref[...], _dot(dx_ref[...], wo_ref[...], _NT), hs_ref[...]
        dgate = dyv * hsv * _gelu_grad(gp)
        dg_ref[...] = dgate.astype(BF16)
        _acc(dgs_ref, _colsum(dgate), first)
        lam_ref[...] = dyv * _gelu(gp)

        def step(k, carry):
            lam_next, a_next = carry
            t = tm - 1 - k
            lam_t = lam_ref[pl.ds(t, 1), :] + a_next * lam_next
            lam_ref[pl.ds(t, 1), :] = lam_t
            return lam_t, a_ref[pl.ds(t, 1), :]

        lam_c, a_c = lax.fori_loop(0, tm, step, (lam_carry[...], a_carry[...]), unroll=8)
        lam_carry[...] = lam_c
        a_carry[...] = a_c
        h_before = hp_ref[HALO - 1:HALO, :] * (i < n_i - 1).astype(F32)
        t_i = lax.broadcasted_iota(jnp.int32, (tm, 1), 0)
        da_ref[...] = lam_ref[...] * jnp.where(t_i == 0, h_before, _down(hsv, 1))

    row = pl.BlockSpec((tm, width), lambda i: (n_i - 1 - i, 0))
    halo = pl.BlockSpec((HALO, width), lambda i: (_prev_halo(tm)(n_i - 1 - i), 0))
    vec = pl.BlockSpec((1, width), lambda i: (0, 0))
    in_specs = [pl.BlockSpec((tm, dx.shape[1]), lambda i: (n_i - 1 - i, 0)), _resident(w_out.shape), row, row, halo, row]
    return _call(body, name, (n_i,), in_specs, [row, row, row, vec],
                 [_sds((m, width)), _sds((m, width)), _sds((m, width), BF16), _sds((1, width))],
                 [pltpu.VMEM((1, width), F32), pltpu.VMEM((1, width), F32)])(dx, w_out, z, hs, hs, a)


def c_mid_bwd(lam_seq, da, xr, wa, ba, wi, bi, lam, name):
    m, width = xr.shape
    heads, hw = wa.shape[0], wa.shape[1]
    tm = _rows(m, 512)

    def body(l_ref, da_ref, xr_ref, wa_ref, ba_ref, wi_ref, bi_ref, lam_ref,
             dxr_ref, dwa_ref, dwi_ref, dba_ref, dbi_ref, dlam_ref):
        first = pl.program_id(0) == 0
        xr_v, lmb = xr_ref[...], l_ref[...]
        xb, r, ig, sp, a, mult = _c_gates(xr_v, wa_ref, ba_ref, wi_ref, bi_ref, lam_ref, heads, hw)
        dmult = lmb * (ig * xr_v)
        dig = lmb * mult * xr_v
        dxr = lmb * mult * ig
        dla = da_ref[...] * a - dmult * (a * a) / mult
        dr = dla * (-C_GATE_C * sp)
        dsp = _colsum(dla * (-C_GATE_C * r))
        _acc(dlam_ref, dsp * (-_sigmoid(-lam_ref[...])), first)
        dra = dr * r * (1.0 - r)
        dia = dig * ig * (1.0 - ig)
        _acc(dba_ref, _colsum(dra), first)
        _acc(dbi_ref, _colsum(dia), first)
        for h in range(heads):
            cs = slice(h * hw, (h + 1) * hw)
            _acc(dwa_ref.at[h], _dot(xb[:, cs], dra[:, cs], _TN), first)
            _acc(dwi_ref.at[h], _dot(xb[:, cs], dia[:, cs], _TN), first)
            dxr_ref[:, cs] = dxr[:, cs] + _dot(dra[:, cs], wa_ref[h], _NT) + _dot(dia[:, cs], wi_ref[h], _NT)

    vec = pl.BlockSpec((1, width), lambda i: (0, 0))
    row = pl.BlockSpec((tm, width), lambda i: (i, 0))
    wsp = pl.BlockSpec((heads, hw, hw), lambda i: (0, 0, 0))
    return _call(body, name, (m // tm,), [row, row, row, wsp, vec, wsp, vec, vec], [row, wsp, wsp, vec, vec, vec],
                 [_sds((m, width)), _sds((heads, hw, hw)), _sds((heads, hw, hw)), _sds((1, width)), _sds((1, width)),
                  _sds((1, width))])(lam_seq, da, xr, wa, ba, wi, bi, lam)


def conv_bwd(dy, x_src, col_block, cw, name):
    m, width = dy.shape
    taps = cw.shape[0]
    tm = _rows(m, 512)
    n_i = m // tm

    def body(dm_ref, dn_ref, xp_ref, xm_ref, cw_ref, dx_ref, dw_ref, db_ref, dxs_ref):
        i = pl.program_id(0)
        first = i == 0
        de = jnp.concatenate([jnp.zeros((HALO, width), F32), dm_ref[...], dn_ref[...] * (i < n_i - 1).astype(F32)], axis=0)
        xe = jnp.concatenate([xp_ref[...] * (i > 0).astype(F32), xm_ref[...], jnp.zeros((HALO, width), F32)], axis=0)
        w = cw_ref[...]
        dx = de * w[taps - 1:taps]
        for s in range(1, taps):
            dx = dx + _up(de, s) * w[taps - 1 - s:taps - s]
        dx_ref[...] = dx[HALO:HALO + tm].astype(BF16)
        _acc(dxs_ref, _colsum(dx[HALO:HALO + tm]), first)
        dm = dm_ref[...]
        for s in range(taps):
            _acc(dw_ref.at[taps - 1 - s:taps - s, :], _colsum(dm * _down(xe, s)[HALO:HALO + tm]), first)
        _acc(db_ref, _colsum(dm), first)

    row = pl.BlockSpec((tm, width), lambda i: (i, 0))
    vec = pl.BlockSpec((1, width), lambda i: (0, 0))
    tsp = pl.BlockSpec((taps, width), lambda i: (0, 0))
    in_specs = [row, pl.BlockSpec((HALO, width), lambda i: (_next_halo(tm, m)(i), 0)),
                pl.BlockSpec((HALO, width), lambda i: (_prev_halo(tm)(i), col_block)),
                pl.BlockSpec((tm, width), lambda i: (i, col_block)), tsp]
    return _call(body, name, (n_i,), in_specs, [row, tsp, vec, vec],
                 [_sds((m, width), BF16), _sds((taps, width)), _sds((1, width)), _sds((1, width))])(dy, dy, x_src, x_src, cw)


def d_mid_fwd(z, cw, w_out, res, next_g, name):
    m, w3 = z.shape
    width = w3 // 3
    taps = cw.shape[0]
    tm = _rows(m, 512)

    def body(bm_ref, cp_ref, cm_ref, xp_ref, xm_ref, cw_ref, wo_ref, res_ref, ng_ref, y_ref, x_ref, h_ref):
        keep = (pl.program_id(0) > 0).astype(F32)
        qe = (jnp.concatenate([cp_ref[...] * keep, cm_ref[...]], axis=0)
              * jnp.concatenate([xp_ref[...], xm_ref[...]], axis=0))
        y_ref[...] = (bm_ref[...] * _conv_ext(qe, cw_ref[...], taps)[HALO:]).astype(BF16)
        _project_out(y_ref, wo_ref, res_ref, ng_ref, x_ref, h_ref)

    main = lambda c: pl.BlockSpec((tm, width), lambda i: (i, c))
    prev = lambda c: pl.BlockSpec((HALO, width), lambda i: (_prev_halo(tm)(i), c))
    in_specs = [main(0), prev(1), main(1), prev(2), main(2), pl.BlockSpec((taps, width), lambda i: (0, 0))]
    return _call_projected(body, name, m, tm, in_specs, (z, z, z, z, z, cw), [(width, BF16)], w_out, res, next_g)


def d_mid_bwd(z, dy, cw, name):
    m, w3 = z.shape
    width = w3 // 3
    taps = cw.shape[0]
    tm = _rows(m, 512)
    n_i = m // tm

    def body(bm_ref, bn_ref, cp_ref, cm_ref, cn_ref, xp_ref, xm_ref, xn_ref, dm_ref, dn_ref, cw_ref, dz_ref, dw_ref):
        i = pl.program_id(0)
        first = i == 0
        kp, kn = (i > 0).astype(F32), (i < n_i - 1).astype(F32)
        zeros = jnp.zeros((HALO, width), F32)
        ce = jnp.concatenate([cp_ref[...] * kp, cm_ref[...], cn_ref[...] * kn], axis=0)
        xe = jnp.concatenate([xp_ref[...], xm_ref[...], xn_ref[...]], axis=0)
        qe = ce * xe
        be = jnp.concatenate([zeros, bm_ref[...], bn_ref[...]], axis=0)
        dye = jnp.concatenate([zeros, dm_ref[...], dn_ref[...] * kn], axis=0)
        w = cw_ref[...]
        cq = _conv_ext(qe, w, taps)
        dcq = dye * be
        dq = dcq * w[taps - 1:taps]
        for s in range(1, taps):
            dq = dq + _up(dcq, s) * w[taps - 1 - s:taps - s]
        ms = slice(HALO, HALO + tm)
        dz_ref[:, :width] = (dye * cq)[ms].astype(BF16)
        dz_ref[:, width:2 * width] = (dq * xe)[ms].astype(BF16)
        dz_ref[:, 2 * width:] = (dq * ce)[ms].astype(BF16)
        for s in range(taps):
            _acc(dw_ref.at[taps - 1 - s:taps - s, :], _colsum(dcq[ms] * _down(qe, s)[ms]), first)

    main = lambda c: pl.BlockSpec((tm, width), lambda i: (i, c))
    prev = lambda c: pl.BlockSpec((HALO, width), lambda i: (_prev_halo(tm)(i), c))
    nxt = lambda c: pl.BlockSpec((HALO, width), lambda i: (_next_halo(tm, m)(i), c))
    tsp = pl.BlockSpec((taps, width), lambda i: (0, 0))
    in_specs = [main(0), nxt(0), prev(1), main(1), nxt(1), prev(2), main(2), nxt(2), main(0), nxt(0), tsp]
    return _call(body, name, (n_i,), in_specs, [pl.BlockSpec((tm, w3), lambda i: (i, 0)), tsp],
                 [_sds((m, w3), BF16), _sds((taps, width))])(z, z, z, z, z, z, z, z, dy, dy, cw)


def _halo_rows(dtype):
    return HALO * (4 // jnp.dtype(dtype).itemsize)


def ffn_gate_down(z, cw, cb, w_down, res, next_g, name):
    _, nj, m, c = z.shape
    n = w_down.shape[1]
    taps = cw.shape[2]
    tm = _rows(m, 256)
    hz = _halo_rows(z.dtype)

    def body(zp_ref, zm_ref, cw_ref, cb_ref, w_ref, res_ref, *rest):
        keep = (pl.program_id(0) > 0).astype(F32)
        zc_ref, act_ref, x_ref = rest[-4:-1] if next_g is not None else rest[-3:]
        acc = res_ref[...]
        for j in range(nj):
            zc = []
            for s in range(2):
                xe = jnp.concatenate([zp_ref[s, j].astype(F32) * keep, zm_ref[s, j].astype(F32)], axis=0)
                zc.append(_conv_ext(xe, cw_ref[s, j], taps)[hz:] + cb_ref[s, j])
                zc_ref[s, j] = zc[s].astype(BF16)
            act = (zc[0] * _sigmoid(zc[0]) * zc[1]).astype(BF16)
            act_ref[j] = act
            acc = acc + _dot(act, w_ref[j])
        x_ref[...] = acc
        if next_g is not None:
            rest[-1][...] = _norm_rows(acc, rest[0][...])

    row = pl.BlockSpec((tm, n), lambda i: (i, 0))
    in_specs = [pl.BlockSpec((2, nj, hz, c), lambda i: (0, 0, _prev_halo(tm, hz)(i), 0)),
                pl.BlockSpec((2, nj, tm, c), lambda i: (0, 0, i, 0)),
                _resident(cw.shape), _resident(cb.shape), _resident((nj, c, n)), row]
    out_specs = [pl.BlockSpec((2, nj, tm, c), lambda i: (0, 0, i, 0)), pl.BlockSpec((nj, tm, c), lambda i: (0, i, 0)), row]
    out_shape = [_sds((2, nj, m, c), BF16), _sds((nj, m, c), BF16), _sds((m, n))]
    ins = [z, z, cw, cb, w_down.reshape(nj, c, n), res]
    if next_g is not None:
        in_specs.append(pl.BlockSpec((1, n), lambda i: (0, 0)))
        out_specs.append(row)
        out_shape.append(_sds((m, n), BF16))
        ins.append(next_g)
    return _call(body, name, (m // tm,), in_specs, out_specs, out_shape)(*ins)


def ffn_gate_bwd(z, zc, dx, w_down, cw, name, after=()):
    _, nj, m, c = z.shape
    n = w_down.shape[1]
    taps = cw.shape[2]
    tm = _rows(m, 512)
    n_i = m // tm
    hz = _halo_rows(z.dtype)
    assert _halo_rows(dx.dtype) == hz and zc.dtype == z.dtype, (z.dtype, zc.dtype, dx.dtype)

    def body(zp_ref, zm_ref, zn_ref, cm_ref, cn_ref, dm_ref, dn_ref, wd_ref, cw_ref, dz_ref, dw_ref, db_ref):
        i = pl.program_id(1)
        first = i == 0
        kp, kn = (i > 0).astype(F32), (i < n_i - 1).astype(F32)
        xe = [jnp.concatenate([zp_ref[s].astype(F32) * kp, zm_ref[s].astype(F32), zn_ref[s].astype(F32) * kn], axis=0)
              for s in range(2)]
        zc = [jnp.concatenate([jnp.zeros((hz, c), F32), cm_ref[s].astype(F32), cn_ref[s].astype(F32)], axis=0)
              for s in range(2)]
        dact = _dot(jnp.concatenate([dm_ref[...], dn_ref[...]], axis=0), wd_ref[...], _NT)
        dae = jnp.concatenate([jnp.zeros((hz, c), F32), dact[:tm], dact[tm:] * kn], axis=0)
        sg = _sigmoid(zc[0])
        dzc = [dae * zc[1] * (sg * (1.0 + zc[0] * (1.0 - sg))), dae * (zc[0] * sg)]
        ms = slice(hz, hz + tm)
        for s in range(2):
            w = cw_ref[s]
            ups = [dzc[s]] + [_up(dzc[s], u) for u in range(1, taps)]
            dxs = ups[0] * w[taps - 1:taps]
            for u in range(1, taps):
                dxs = dxs + ups[u] * w[taps - 1 - u:taps - u]
            dz_ref[s] = dxs[ms].astype(BF16)
            tail = dzc[s][hz + tm:]
            x_end = xe[s][tm:]
            for u in range(taps):
                total = _colsum(ups[u] * xe[s]) - _colsum(tail * _down(x_end, u)[hz:])
                _acc(dw_ref.at[s, taps - 1 - u:taps - u, :], total, first)
            _acc(db_ref.at[s], _colsum(dzc[s][ms]), first)

    in_specs = [pl.BlockSpec((2, None, hz, c), lambda j, i: (0, j, _prev_halo(tm, hz)(i), 0)),
                pl.BlockSpec((2, None, tm, c), lambda j, i: (0, j, i, 0)),
                pl.BlockSpec((2, None, hz, c), lambda j, i: (0, j, _next_halo(tm, m, hz)(i), 0)),
                pl.BlockSpec((2, None, tm, c), lambda j, i: (0, j, i, 0)),
                pl.BlockSpec((2, None, hz, c), lambda j, i: (0, j, _next_halo(tm, m, hz)(i), 0)),
                pl.BlockSpec((tm, n), lambda j, i: (i, 0)),
                pl.BlockSpec((hz, n), lambda j, i: (_next_halo(tm, m, hz)(i), 0)),
                pl.BlockSpec((None, c, n), lambda j, i: (j, 0, 0)),
                pl.BlockSpec((2, None, taps, c), lambda j, i: (0, j, 0, 0))]
    out_specs = [pl.BlockSpec((2, None, tm, c), lambda j, i: (0, j, i, 0)),
                 pl.BlockSpec((2, None, taps, c), lambda j, i: (0, j, 0, 0)),
                 pl.BlockSpec((2, None, 1, c), lambda j, i: (0, j, 0, 0))]
    return _call(body, name, (nj, n_i), in_specs, out_specs,
                 [_sds((2, nj, m, c), BF16), _sds((2, nj, taps, c)), _sds((2, nj, 1, c))], (), after)(
        z, z, z, zc, zc, dx, dx, w_down.reshape(nj, c, n), cw)


_STREAM_TILE_BYTES = 2 << 20


def _tile_rows(r, c):
    if r * c * 4 <= _STREAM_TILE_BYTES:
        return r
    fits = [d for d in range(16, r, 16) if r % d == 0 and d * c * 4 <= _STREAM_TILE_BYTES]
    return max(fits) if fits else r


def _as2d(a, lead):
    shape = a.shape
    return a.reshape((lead, -1, shape[-1]) if lead else (-1, shape[-1]))


def add_pairs(g, l1, own, name):
    shape = l1.shape
    g3, l3 = _as2d(g, N_DEV), _as2d(l1, 4)
    _, r, c = l3.shape
    tr = _tile_rows(r, c)

    def body(own_ref, a_ref, b_ref, o_ref):
        o_ref[...] = (a_ref[...].astype(F32) + b_ref[...].astype(F32)).astype(o_ref.dtype)

    spec = pl.BlockSpec((None, tr, c), lambda k, i, own_ref: (k, i, 0))
    grid_spec = pltpu.PrefetchScalarGridSpec(
        num_scalar_prefetch=1, grid=(4, r // tr),
        in_specs=[pl.BlockSpec((None, tr, c), lambda k, i, own_ref: (own_ref[k], i, 0)), spec], out_specs=spec)
    out = pl.pallas_call(
        body, name=name, grid_spec=grid_spec, out_shape=_sds(l3.shape, l1.dtype),
        compiler_params=pltpu.CompilerParams(dimension_semantics=("arbitrary", "arbitrary"),
                                             vmem_limit_bytes=V7X_VMEM_LIMIT_BYTES))(own, g3, l3)
    return out.reshape(shape)


def _grad_sum(p_ref, l_ref):
    return ((p_ref[...].astype(F32) + l_ref[0].astype(F32)) + l_ref[1].astype(F32)) + l_ref[2].astype(F32)


def sum_parts(p, l2, name):
    _, r, c = p.shape

    def body(p_ref, l_ref, o_ref):
        o_ref[...] = _grad_sum(p_ref, l_ref)

    return _call(body, name, (1,), [pl.BlockSpec((None, r, c), lambda i: (0, 0, 0)), pl.BlockSpec((3, r, c), lambda i: (0, 0, 0))],
                 pl.BlockSpec((r, c), lambda i: (0, 0)), _sds((r, c)))(p, l2)


def _adamw_math(w, g, m, v):
    m = ADAM_B1 * m + (1.0 - ADAM_B1) * g
    v = ADAM_B2 * v + (1.0 - ADAM_B2) * (g * g)
    m_hat = m / (1.0 - ADAM_B1 ** ADAM_STEP)
    v_hat = v / (1.0 - ADAM_B2 ** ADAM_STEP)
    delta = -ADAM_LR * (m_hat / (jnp.sqrt(v_hat) + ADAM_EPS) + ADAM_WD * w)
    return delta, m, v


def adamw(w, m, v, name, g=None, p=None, l2=None):
    shape = w.shape
    w2, m2, v2 = (_as2d(t, 0) for t in (w, m, v))
    r, c = w2.shape
    tr = _tile_rows(r, c)
    row = pl.BlockSpec((tr, c), lambda i: (i, 0))
    if g is None:
        p3, l3 = _as2d(p, 4), _as2d(l2, 3)
        gin = (p3, l3)
        gspecs = [pl.BlockSpec((None, tr, c), lambda i: (0, i, 0)), pl.BlockSpec((3, tr, c), lambda i: (0, i, 0))]
    else:
        gin, gspecs = (_as2d(g, 0),), [row]

    def body(*refs):
        n_g = len(gin)
        w_ref, m_ref, v_ref, g_ref, d_ref, nm_ref, nv_ref = refs[n_g:]
        grad = refs[0][...] if n_g == 1 else _grad_sum(refs[0], refs[1])
        delta, nm, nv = _adamw_math(w_ref[...], grad, m_ref[...], v_ref[...])
        g_ref[...] = grad
        d_ref[...] = delta
        nm_ref[...] = nm
        nv_ref[...] = nv

    outs = _call(body, name, (r // tr,), gspecs + [row, row, row], [row] * 4, [_sds((r, c))] * 4)(*gin, w2, m2, v2)
    return tuple(o.reshape(shape) for o in outs)


def adamw_layer(w, m, v, layer, prev, name, p, l2):
    n_l, r, c = w.shape
    tr = _tile_rows(r, c)
    slab = pl.BlockSpec((None, tr, c), lambda i: (layer, i, 0))
    in_specs = [pl.BlockSpec((None, tr, c), lambda i: (0, i, 0)), pl.BlockSpec((3, tr, c), lambda i: (0, i, 0)), slab, slab, slab]
    n_in = len(in_specs)
    prev = () if prev is None else tuple(prev)

    def body(p_ref, l_ref, w_ref, m_ref, v_ref, *rest):
        g_ref, d_ref, nm_ref, nv_ref = rest[len(prev):]
        grad = _grad_sum(p_ref, l_ref)
        delta, nm, nv = _adamw_math(w_ref[...], grad, m_ref[...], v_ref[...])
        g_ref[...] = grad
        d_ref[...] = delta
        nm_ref[...] = nm
        nv_ref[...] = nv

    return pl.pallas_call(
        body, name=name, grid=(r // tr,), in_specs=in_specs + [pl.BlockSpec(memory_space=pl.ANY)] * len(prev),
        out_specs=[slab] * 4, out_shape=[_sds((n_l, r, c))] * 4,
        input_output_aliases={n_in + q: q for q in range(len(prev))},
        compiler_params=pltpu.CompilerParams(dimension_semantics=("arbitrary",), vmem_limit_bytes=V7X_VMEM_LIMIT_BYTES),
    )(_as2d(p, 4), _as2d(l2, 3), w, m, v, *prev)


def _comm_call(body, name, ins, out_shape, n_sems):
    any_spec = pl.BlockSpec(memory_space=pl.ANY)
    return pl.pallas_call(
        body, name=name, in_specs=[any_spec] * len(ins), out_specs=[any_spec] * len(out_shape), out_shape=out_shape,
        scratch_shapes=[pltpu.SemaphoreType.DMA((n,)) for n in n_sems],
        compiler_params=pltpu.CompilerParams(has_side_effects=True))(*ins)


def _place():
    return lax.axis_index("x"), lax.axis_index("y"), lax.axis_index("c")


def _dev_index(px, py, pc):
    return 4 * px + 2 * py + pc


def all_gather(blocks, name):
    n_t = len(blocks)

    def body(*refs):
        ins, outs = refs[:n_t], refs[n_t:2 * n_t]
        send_sems, recv_sems, local_sems = refs[2 * n_t:]
        x, y, c = _place()
        me, sibling = (x, y, c), (x, y, 1 - c)
        chips = [(1 - x, y), (x, 1 - y), (1 - x, 1 - y)]

        def copy(t, k, block, to, src=None):
            dst = outs[t].at[_dev_index(*block)]
            return pltpu.make_async_remote_copy(
                src_ref=dst if src is None else src, dst_ref=dst, send_sem=send_sems.at[t * 7 + k],
                recv_sem=recv_sems.at[t * 7 + k], device_id=to, device_id_type=MESH_ID)

        mine = [pltpu.make_async_copy(ins[t], outs[t].at[_dev_index(*me)], local_sems.at[t]) for t in range(n_t)]
        for cp in mine:
            cp.start()
        first = []
        for t in range(n_t):
            first.append(copy(t, 0, me, sibling, src=ins[t]))
            first += [copy(t, 1 + j, me, (*chip, c), src=ins[t]) for j, chip in enumerate(chips)]
        for cp in first:
            cp.start()
        passed = []
        for t in range(n_t):
            for j, chip in enumerate(chips):
                copy(t, 1 + j, (*chip, c), me).wait_recv()
                cp = copy(t, 4 + j, (*chip, c), sibling)
                cp.start()
                passed.append(cp)
        for t in range(n_t):
            copy(t, 0, sibling, me).wait_recv()
            for j, chip in enumerate(chips):
                copy(t, 4 + j, (*chip, 1 - c), me).wait_recv()
        for cp in first + passed:
            cp.wait_send()
        for cp in mine:
            cp.wait()

    out_shape = [_sds((N_DEV,) + b.shape, b.dtype) for b in blocks]
    return _comm_call(body, name, blocks, out_shape, (7 * n_t, 7 * n_t, n_t))


def _chip_of(x, y, k):
    return (x if k % 2 == 0 else 1 - x), (y if k // 2 == 0 else 1 - y)


_HBM_SPEC = pl.BlockSpec(memory_space=pltpu.HBM)
_SEM_SPEC = pl.BlockSpec(memory_space=pltpu.SEMAPHORE)
_DATAFLOW = pltpu.SideEffectType.DATAFLOW_SIDE_EFFECTING


def _in_hbm(a):
    return pltpu.with_memory_space_constraint(a, pltpu.HBM)


def _split_start(name, issue, srcs, land_shapes, sem_counts, after=()):
    n_buf, n_sem, n_after = len(srcs) + len(land_shapes), len(sem_counts), len(after)

    def body(*refs):
        issue(refs[:len(srcs)], refs[len(srcs):n_buf], refs[n_buf + n_after:n_buf + n_after + n_sem])
        refs[-1][...] = jnp.zeros_like(refs[-1])

    bufs = [pltpu.HBM(s.shape, s.dtype) for s in list(srcs) + list(land_shapes)]
    outs = pl.pallas_call(
        body, name=name, in_specs=(*[_HBM_SPEC] * n_buf, *[pl.BlockSpec(memory_space=pl.ANY)] * n_after),
        out_shape=(*[pltpu.SemaphoreType.DMA((n,)) for n in sem_counts], *bufs, _sds((8, LANES))),
        out_specs=(*[_SEM_SPEC] * n_sem, *[_HBM_SPEC] * n_buf, pl.BlockSpec(memory_space=pltpu.VMEM)),
        input_output_aliases={i: n_sem + i for i in range(n_buf)},
        compiler_params=pltpu.CompilerParams(has_side_effects=_DATAFLOW),
    )(*[_in_hbm(s) for s in srcs], *[_in_hbm(lax.empty(s.shape, s.dtype)) for s in land_shapes], *after)
    return outs[:n_sem], outs[n_sem:n_sem + len(srcs)], outs[n_sem + len(srcs):n_sem + n_buf], outs[-1]


def _split_wait(name, finish, sems, srcs, lands, after):
    n_buf, n_sem = len(srcs) + len(lands), len(sems)

    def body(*refs):
        finish(refs[:len(srcs)], refs[len(srcs):n_buf], refs[n_buf:n_buf + n_sem])

    bufs = [pltpu.HBM(s.shape, s.dtype) for s in list(srcs) + list(lands)]
    outs = pl.pallas_call(
        body, name=name, in_specs=(*[_HBM_SPEC] * n_buf, *[_SEM_SPEC] * n_sem, *[pl.BlockSpec(memory_space=pl.ANY)] * len(after)),
        out_shape=tuple(bufs), out_specs=(_HBM_SPEC,) * n_buf, input_output_aliases={i: i for i in range(n_buf)},
        compiler_params=pltpu.CompilerParams(has_side_effects=_DATAFLOW),
    )(*srcs, *lands, *sems, *after)
    return outs[:len(srcs)], outs[len(srcs):]


def _peer(x, y, c, r):
    return (1 - x if r & 4 else x), (1 - y if r & 2 else y), (1 - c if r & 1 else c)


ALL_PEERS = tuple(range(1, N_DEV))
SAME_CORE_PEERS = (2, 4, 6)


def _gather_copies(src_refs, land_refs, sem_refs, arrivals, peers):
    send_sems, recv_sems, local_sems = sem_refs
    x, y, c = _place()
    me = _dev_index(x, y, c)
    local, sends, recvs = [], [], []
    for j, (src, land) in enumerate(zip(src_refs, land_refs)):
        local.append(pltpu.make_async_copy(src, land.at[me], local_sems.at[j]))
        for p, r in enumerate(peers):
            peer = _peer(x, y, c, r)
            q = len(peers) * j + p
            sends.append(pltpu.make_async_remote_copy(src_ref=src, dst_ref=land.at[me], send_sem=send_sems.at[q],
                                                      recv_sem=recv_sems.at[q], device_id=peer, device_id_type=MESH_ID))
            if arrivals:
                recvs.append(pltpu.make_async_remote_copy(
                    src_ref=src, dst_ref=land.at[_dev_index(*peer)], send_sem=send_sems.at[q], recv_sem=recv_sems.at[q],
                    device_id=peer, device_id_type=MESH_ID))
    return local, sends, recvs


def gather_start(groups, peers, name, after=()):
    flat = [b for g in groups for b in g]
    bounds = [sum(len(g) for g in groups[:i]) for i in range(len(groups) + 1)]

    def issue(src_refs, land_refs, sem_refs):
        for i in range(len(groups)):
            lo, hi = bounds[i], bounds[i + 1]
            local, sends, _ = _gather_copies(src_refs[lo:hi], land_refs[lo:hi], sem_refs[3 * i:3 * i + 3], False, peers[i])
            for cp in local + sends:
                cp.start()

    sem_counts = [n for g, p in zip(groups, peers) for n in (len(p) * len(g), len(p) * len(g), len(g))]
    sems, srcs, lands, _ = _split_start(name, issue, flat, [_sds((N_DEV,) + b.shape, b.dtype) for b in flat], sem_counts, after)
    return [(sems[3 * i:3 * i + 3], srcs[bounds[i]:bounds[i + 1]], lands[bounds[i]:bounds[i + 1]], peers[i])
            for i in range(len(groups))]


def gather_wait(group, after, name):
    sems, srcs, lands, peers = group

    def finish(src_refs, land_refs, sem_refs):
        local, sends, recvs = _gather_copies(src_refs, land_refs, sem_refs, True, peers)
        for cp in local:
            cp.wait()
        for cp in recvs:
            cp.wait_recv()
        for cp in sends:
            cp.wait_send()

    return _split_wait(name, finish, sems, srcs, lands, after)[1]


def _forward_copies(land_refs, sem_refs, arrivals):
    send_sems, recv_sems = sem_refs
    x, y, c = _place()
    sends, recvs = [], []
    for t, land in enumerate(land_refs):
        for k in range(4):
            cx, cy = _chip_of(x, y, k)
            mine, theirs = land.at[_dev_index(cx, cy, c)], land.at[_dev_index(cx, cy, 1 - c)]
            sems = dict(send_sem=send_sems.at[4 * t + k], recv_sem=recv_sems.at[4 * t + k], device_id=(x, y, 1 - c),
                        device_id_type=MESH_ID)
            sends.append(pltpu.make_async_remote_copy(src_ref=mine, dst_ref=mine, **sems))
            if arrivals:
                recvs.append(pltpu.make_async_remote_copy(src_ref=theirs, dst_ref=theirs, **sems))
    return sends, recvs


def forward_start(lands, name):
    def issue(land_refs, _, sem_refs):
        for cp in _forward_copies(land_refs, sem_refs, False)[0]:
            cp.start()

    n = 4 * len(lands)
    sems, lands, _, token = _split_start(name, issue, lands, [], (n, n))
    return sems, lands, token


def forward_wait(started, after, name):
    sems, lands, _ = started

    def finish(land_refs, _, sem_refs):
        sends, recvs = _forward_copies(land_refs, sem_refs, True)
        for cp in recvs:
            cp.wait_recv()
        for cp in sends:
            cp.wait_send()

    return _split_wait(name, finish, sems, lands, [], after)[0]


def _sibling_copies(src_refs, land_refs, sem_refs):
    send_sems, recv_sems = sem_refs
    x, y, c = _place()
    copies = []
    for t, (src, land) in enumerate(zip(src_refs, land_refs)):
        for k in range(4):
            cx, cy = _chip_of(x, y, k)
            copies.append(pltpu.make_async_remote_copy(
                src_ref=src.at[_dev_index(cx, cy, 1 - c)], dst_ref=land.at[k], send_sem=send_sems.at[4 * t + k],
                recv_sem=recv_sems.at[4 * t + k], device_id=(x, y, 1 - c), device_id_type=MESH_ID))
    return copies


def _chip_copies(src_refs, land_refs, sem_refs):
    send_sems, recv_sems = sem_refs
    x, y, c = _place()
    copies = []
    for t, (src, land) in enumerate(zip(src_refs, land_refs)):
        for k in range(1, 4):
            cx, cy = _chip_of(x, y, k)
            copies.append(pltpu.make_async_remote_copy(
                src_ref=src.at[k], dst_ref=land.at[k - 1], send_sem=send_sems.at[3 * t + k - 1],
                recv_sem=recv_sems.at[3 * t + k - 1], device_id=(cx, cy, c), device_id_type=MESH_ID))
    return copies


def _exchange_start(copies_of, n_land, per_array, arrays, name):
    def issue(src_refs, land_refs, sem_refs):
        for cp in copies_of(src_refs, land_refs, sem_refs):
            cp.start()

    n = per_array * len(arrays)
    lands = [_sds((n_land,) + a.shape[1:], a.dtype) for a in arrays]
    return _split_start(name, issue, arrays, lands, (n, n))


def _exchange_wait(copies_of, started, after, name):
    sems, srcs, lands, _ = started

    def finish(src_refs, land_refs, sem_refs):
        copies = copies_of(src_refs, land_refs, sem_refs)
        for cp in copies:
            cp.wait_recv()
        for cp in copies:
            cp.wait_send()

    return _split_wait(name, finish, sems, srcs, lands, after)


def sibling_start(grads, name):
    return _exchange_start(_sibling_copies, 4, 4, grads, name)


def sibling_wait(started, after, name):
    return _exchange_wait(_sibling_copies, started, after, name)


def chips_start(parts, name):
    return _exchange_start(_chip_copies, 3, 3, parts, name)


def chips_wait(started, after, name):
    return _exchange_wait(_chip_copies, started, after, name)


def _pack(arrays, rows):
    flat = jnp.concatenate([a.reshape(-1) for a in arrays])
    return jnp.pad(flat, (0, rows * LANES - flat.shape[0])).reshape(rows, LANES)


def _pack_stacked(arrays, rows):
    flat = jnp.concatenate([a.reshape(N_DEV, -1) for a in arrays], axis=1)
    return jnp.pad(flat, ((0, 0), (0, rows * LANES - flat.shape[1]))).reshape(N_DEV, rows, LANES)


def _unpack(buf, shapes, lead=()):
    flat = buf.reshape(lead + (-1,))
    out, off = [], 0
    for s in shapes:
        n = 1
        for d in s:
            n *= d
        out.append(flat[..., off:off + n].reshape(lead + tuple(s)))
        off += n
    return out


def _padded_rows(shapes, multiple):
    n = sum(functools.reduce(lambda a, b: a * b, s, 1) for s in shapes)
    rows = -(-n // LANES)
    return -(-rows // multiple) * multiple


def _to_full(stacked, axis):
    t = jnp.moveaxis(stacked, 0, axis)
    return t.reshape(t.shape[:axis] + (t.shape[axis] * t.shape[axis + 1],) + t.shape[axis + 2:])


def _to_stacked(full, axis):
    s = full.shape
    t = full.reshape(s[:axis] + (N_DEV, s[axis] // N_DEV) + s[axis + 1:])
    return jnp.moveaxis(t, axis, 0)


def _ffn_forward(x, h, w_up, cw, cb, w_down, next_g, tag):
    z = mm_in(h, w_up, f"ffn{tag}_up", stacked_out=True, out_dtype=BF16, rows=2048)
    nb, m, c = z.shape
    z4 = z.reshape(2, nb // 2, m, c)
    zc, act, *out = ffn_gate_down(z4, cw, cb, w_down, x, next_g, f"ffn{tag}_gate_down")
    return (out[0], out[1] if next_g is not None else None), (x, h, z4, zc, act)


def _ffn_backward(dx, dxb, saved, norm_g, w_up, cw, w_down, tag, after=()):
    x, h, z4, zc, act = saved
    nj = act.shape[0]
    dz4, dcw, dcb = ffn_gate_bwd(z4, zc, dxb, w_down, cw, f"ffn{tag}_gate_bwd", after)
    dw_down = mm_dw_out(act, dxb, f"ffn{tag}_down_dw")
    dz = dz4.reshape((2 * nj,) + dz4.shape[2:])
    dx, dxb, dg = mm_dx_in(dz, w_up, f"ffn{tag}_up_dx", norm=(x, norm_g, dx))
    dw_up = mm_dw_in(h, dz, 2 * nj, f"ffn{tag}_up_dw", transposed=True)
    return dx, dxb, dict(norm_g=dg, w_up=dw_up, conv_w=dcw, conv_b=dcb, w_down=dw_down)


def kernel(x, a_norm_g, a_w_in, a_b_in, a_v_norm_g, a_w_s, a_b_s, a_w_out, b_norm_g, b_w_in, b_w_grp, b_b_grp, b_scale, b_w_out, c_norm_g, c_w_in, c_b_in, c_conv_w, c_conv_b, c_w_a, c_b_a, c_w_i, c_b_i, c_lambda, c_w_out, d_norm_g, d_w_in, d_conv_w, d_w_out, ffn_norm_g, ffn_w_up, ffn_conv_w, ffn_conv_b, ffn_w_down, final_norm_g, loss_target, m_a_norm_g, m_a_w_in, m_a_b_in, m_a_v_norm_g, m_a_w_s, m_a_b_s, m_a_w_out, m_b_norm_g, m_b_w_in, m_b_w_grp, m_b_b_grp, m_b_scale, m_b_w_out, m_c_norm_g, m_c_w_in, m_c_b_in, m_c_conv_w, m_c_conv_b, m_c_w_a, m_c_b_a, m_c_w_i, m_c_b_i, m_c_lambda, m_c_w_out, m_d_norm_g, m_d_w_in, m_d_conv_w, m_d_w_out, m_ffn_norm_g, m_ffn_w_up, m_ffn_conv_w, m_ffn_conv_b, m_ffn_w_down, m_final_norm_g, v_a_norm_g, v_a_w_in, v_a_b_in, v_a_v_norm_g, v_a_w_s, v_a_b_s, v_a_w_out, v_b_norm_g, v_b_w_in, v_b_w_grp, v_b_b_grp, v_b_scale, v_b_w_out, v_c_norm_g, v_c_w_in, v_c_b_in, v_c_conv_w, v_c_conv_b, v_c_w_a, v_c_b_a, v_c_w_i, v_c_b_i, v_c_lambda, v_c_w_out, v_d_norm_g, v_d_w_in, v_d_conv_w, v_d_w_out, v_ffn_norm_g, v_ffn_w_up, v_ffn_conv_w, v_ffn_conv_b, v_ffn_w_down, v_final_norm_g):
    args = locals()
    w_loc = {n: args[n] for n in WEIGHTS}
    m_loc = {n: args["m_" + n] for n in WEIGHTS}
    v_loc = {n: args["v_" + n] for n in WEIGHTS}
    depth = ffn_w_up.shape[0]
    xs = x[0]
    target = loss_target[0]

    small_names = list(SMALL_SHARDED)
    small_shapes = [w_loc[n].shape for n in small_names]
    small_rows = _padded_rows(small_shapes, 8)
    small_packed = _pack([w_loc[n] for n in small_names], small_rows)
    early_names = ['ffn_conv_w', 'b_norm_g', 'c_norm_g', 'd_norm_g']
    late_names = [n for n in small_names if n not in early_names]
    packs = []
    for names in (early_names, late_names):
        shapes = [w_loc[n].shape for n in names]
        packs.append((names, shapes, _pack([w_loc[n] for n in names], _padded_rows(shapes, 8))))
    wa_in, wa_out = all_gather([a_w_in[0].astype(BF16), a_w_out[0].astype(BF16)], "gather_a")
    mixers = [None, (b_w_in, b_w_out), (c_w_in, c_w_out), (d_w_in, d_w_out)]
    groups = {}
    for l in range(depth):
        if l > 0:
            groups[f'mixer{l}'] = [mixers[l][0][0].astype(BF16), mixers[l][1][0].astype(BF16)] + ([packs[1][2]] if l == 1 else [])
        groups[f'ffn{l}'] = [ffn_w_up[l].astype(BF16), ffn_w_down[l].astype(BF16)] + ([packs[0][2]] if l == 0 else [])
    two_level = ('ffn0', 'mixer1', 'ffn1')
    in_flight = dict(zip(groups, gather_start(list(groups.values()),
                                              [SAME_CORE_PEERS if k in two_level else ALL_PEERS for k in groups],
                                              "gather_start", (wa_in, wa_out))))
    forwarding = {}

    def forward_on(key, after):
        forwarding[key] = forward_start(gather_wait(in_flight[key], after, f"gather_wait_{key}"), f"forward_start_{key}")
        return (forwarding[key][2],)

    def gathered(key, after):
        if key in two_level:
            return forward_wait(forwarding[key], after, f"forward_wait_{key}")
        return gather_wait(in_flight[key], after, f"gather_wait_{key}")

    sm = {}

    def unpack_small(pack, gathered):
        names, shapes, _ = pack
        sm.update((n, _to_full(s, SMALL_SHARDED[n])) for n, s in zip(names, _unpack(gathered, shapes, (N_DEV,))))

    def rows_full(st):
        return st.reshape((st.shape[0] * st.shape[1],) + st.shape[2:])

    nb = N_DEV
    ffn_cb = [ffn_conv_b[l].reshape(2, nb // 2, 1, -1) for l in range(depth)]
    ffn_g = [ffn_norm_g[l:l + 1] for l in range(depth)]
    a_bst = a_b_s[0].T
    w_up, w_down, saved = [None] * depth, [None] * depth, {}

    def ffn_forward(xl, hl, l, next_g):
        up, down, *early = gathered(f'ffn{l}', (xl,))
        if early:
            unpack_small(packs[0], early[0])
            ffn_cw.extend(sm['ffn_conv_w'][k].reshape(ffn_conv_w.shape[1], 2, nb // 2, -1).transpose(1, 2, 0, 3)
                          for k in range(depth))
        w_up[l], w_down[l] = up, rows_full(down)
        return _ffn_forward(xl, hl, w_up[l], ffn_cw[l], ffn_cb[l], w_down[l], next_g(), l)

    ffn_cw = []
    wa_out = rows_full(wa_out)
    h = rms_fwd(xs, a_norm_g, "a_norm")
    z = mm_in(h, wa_in, "a_in", bias=a_b_in)
    y = a_mid_fwd(z, a_v_norm_g, a_w_s[0], a_bst, "a_mid")
    x1, hf = mm_out(y, wa_out, xs, ffn_g[0], "a_out", forward_on('ffn0', (y,)))
    saved['a'] = (xs, h, z, y)
    forward_on('mixer1', (x1,))
    (x1, h), saved['f0'] = ffn_forward(x1, hf, 0, lambda: sm['b_norm_g'])

    forward_on('ffn1', (x1,))
    wb_in, wb_out, late = gathered('mixer1', (x1,))
    unpack_small(packs[1], late)
    b_wgrp, c_wa, c_wi = (sm[n][0].astype(BF16) for n in ('b_w_grp', 'c_w_a', 'c_w_i'))
    b_bgrp, c_ba, c_bi = (sm[n][0].reshape(1, -1) for n in ('b_b_grp', 'c_b_a', 'c_b_i'))
    wb_in, wb_out = rows_full(wb_in)[None], rows_full(wb_out)
    z = mm_in(h, wb_in, "b_in")
    y, x2, hf = b_mid_fwd(z, b_wgrp, b_bgrp, sm['b_scale'], wb_out, x1, ffn_g[1], "b_mid_out")
    saved['b'] = (x1, h, z, y)
    (x2, h), saved['f1'] = ffn_forward(x2, hf, 1, lambda: sm['c_norm_g'])

    wc_in, wc_out = gathered('mixer2', (x2,))
    wc_out = rows_full(wc_out)
    z = mm_in(h, wc_in, "c_in", bias=sm['c_b_in'])
    c_cw = sm['c_conv_w'][0]
    a_seq, b_seq, xr = c_mid_fwd(z, c_cw, sm['c_conv_b'], c_wa, c_ba, c_wi, c_bi, sm['c_lambda'], "c_mid")
    hs, y, x3, hf = c_scan_fwd(a_seq, b_seq, z, wc_out, x2, ffn_g[2], "c_scan_out")
    saved['c'] = (x2, h, z, y, a_seq, xr, hs)
    (x3, h), saved['f2'] = ffn_forward(x3, hf, 2, lambda: sm['d_norm_g'])

    wd_in, wd_out = gathered('mixer3', (x3,))
    wd_out = rows_full(wd_out)
    z = mm_in(h, wd_in, "d_in")
    d_cw = sm['d_conv_w'][0]
    y, x4, hf = d_mid_fwd(z, d_cw, wd_out, x3, ffn_g[3], "d_mid_out")
    saved['d'] = (x3, h, z, y)
    (x4, _), saved['f3'] = ffn_forward(x4, hf, 3, lambda: None)

    loss_part, dx, dxb, d_final_g = final_loss(x4, final_norm_g.reshape(1, -1), target, "final_loss")
    loss = lax.psum(loss_part[0, 0], ("x", "y", "c"))

    def rows_stacked(full):
        return full.reshape((N_DEV, full.shape[0] // N_DEV) + full.shape[1:])

    mx, my, mc = _place()
    own = jnp.stack([_dev_index(*_chip_of(mx, my, k), mc) for k in range(4)]).astype(jnp.int32)
    repl_shapes = [w_loc[n].shape for n in REPLICATED]
    repl_rows = _padded_rows(repl_shapes, 8 * N_DEV)
    shard_of = {n: (w_loc[n][0], m_loc[n][0], v_loc[n][0])
                for n in ('a_w_in', 'a_w_out', 'b_w_in', 'b_w_out', 'c_w_in', 'c_w_out', 'd_w_in', 'd_w_out')}
    shard_of['small'] = (small_packed, _pack([m_loc[n] for n in small_names], small_rows),
                         _pack([v_loc[n] for n in small_names], small_rows))
    updated = {}

    def finish(n, part, others):
        if n == 'repl':
            chunk = sum_parts(part, others, "rs_sum_repl")
            repl_g = all_gather([chunk], "gather_repl")[0].reshape(repl_rows, LANES)
            updated[n] = adamw(*(_pack([src[k] for k in REPLICATED], repl_rows) for src in (w_loc, m_loc, v_loc)),
                               "adamw_repl", g=repl_g)
        elif n.startswith('ffn_w'):
            base, l = n[:-1], int(n[-1])
            turn = (lambda t: jnp.swapaxes(t, 1, 2)) if base == 'ffn_w_up' else (lambda t: t)
            updated[base] = adamw_layer(turn(w_loc[base]), turn(m_loc[base]), turn(v_loc[base]), l, updated.get(base),
                                        f"adamw_{n}", part, others)
            return updated[base][1]
        else:
            updated[n] = adamw(*shard_of[n], f"adamw_{n}", p=part, l2=others)
        return updated[n][1]

    stages = [None, []]

    def advance(tag, new, after, finish_older=True):
        behind = []
        first = None
        if new:
            first = ([n for n, _ in new], sibling_start([g for _, g in new], f"rs_sibling_start{tag}"))
            behind.append(first[1][3])
        older = stages[1]
        if finish_older:
            for k, (names, started) in enumerate(older):
                parts, others = chips_wait(started, after, f"rs_chips_wait{tag}_{k}")
                behind += [finish(n, p, o) for n, p, o in zip(names, parts, others)]
            older = []
        if stages[0] is not None:
            names, started = stages[0]
            grads, got = sibling_wait(started, after, f"rs_sibling_wait{tag}")
            parts = [add_pairs(g, l, own, f"rs_add_{n}") for n, g, l in zip(names, grads, got)]
            older = older + [(names, chips_start(parts, f"rs_chips_start{tag}"))]
            behind.append(older[-1][1][3])
        stages[:] = [first, older]
        return tuple(behind)

    gf = [None] * depth
    dx, dxb, gf[3] = _ffn_backward(dx, dxb, saved['f3'], ffn_g[3], w_up[3], ffn_cw[3], w_down[3], 3)
    xin, h, z, y = saved['d']
    dy = mm_dx_out(dxb, wd_out, "d_out_dx")
    g_d_w_out = mm_dw_out(y, dxb, "d_out_dw")
    dz, g_d_conv_w = d_mid_bwd(z, dy, d_cw, "d_mid_bwd")
    dx, dxb, g_d_norm_g = mm_dx_in(dz, wd_in, "d_in_dx", norm=(xin, sm['d_norm_g'], dx))
    g_d_w_in = mm_dw_in(h, dz, nb, "d_in_dw")
    behind = advance(0, [('ffn_w_up3', gf[3]['w_up']), ('ffn_w_down3', rows_stacked(gf[3]['w_down'])), ('d_w_in', g_d_w_in),
                         ('d_w_out', rows_stacked(g_d_w_out))], (dx,))

    dx, dxb, gf[2] = _ffn_backward(dx, dxb, saved['f2'], ffn_g[2], w_up[2], ffn_cw[2], w_down[2], 2, behind)
    xin, h, z, y, a_seq, xr, hs = saved['c']
    g_c_w_out = mm_dw_out(y, dxb, "c_out_dw")
    lam_seq, da_seq, dgate, dgate_sum = c_scan_bwd(dxb, wc_out, z, hs, a_seq, "c_scan_bwd")
    dxr, g_c_w_a, g_c_w_i, g_c_b_a, g_c_b_i, g_c_lambda = c_mid_bwd(
        lam_seq, da_seq, xr, c_wa, c_ba, c_wi, c_bi, sm['c_lambda'], "c_mid_bwd")
    dxr_pre, g_c_conv_w, g_c_conv_b, dxr_pre_sum = conv_bwd(dxr, z, 1, c_cw, "c_conv_bwd")
    dz = jnp.concatenate([dgate, dxr_pre], axis=1)
    g_c_b_in = jnp.concatenate([dgate_sum, dxr_pre_sum], axis=1)
    dx, dxb, g_c_norm_g = mm_dx_in(dz, wc_in, "c_in_dx", norm=(xin, sm['c_norm_g'], dx))
    g_c_w_in = mm_dw_in(h, dz, nb, "c_in_dw")
    behind = advance(1, [('ffn_w_up2', gf[2]['w_up']), ('ffn_w_down2', rows_stacked(gf[2]['w_down'])), ('c_w_in', g_c_w_in),
                         ('c_w_out', rows_stacked(g_c_w_out))], (dx,))

    dx, dxb, gf[1] = _ffn_backward(dx, dxb, saved['f1'], ffn_g[1], w_up[1], ffn_cw[1], w_down[1], 1, behind)
    xin, h, z, y = saved['b']
    g_b_w_out = mm_dw_out(y, dxb, "b_out_dw")
    dp, g_b_w_grp, g_b_b_grp, g_b_scale = b_mid_bwd(z, dxb, wb_out, b_wgrp, b_bgrp, sm['b_scale'], "b_mid_bwd")
    dz = b_pool_bwd(dp, "b_pool_bwd")
    dx, dxb, g_b_norm_g = mm_dx_in(dz, wb_in, "b_in_dx", norm=(xin, sm['b_norm_g'], dx))
    g_b_w_in = mm_dw_in(h, dz, 1, "b_in_dw")
    behind = advance(2, [('ffn_w_up1', gf[1]['w_up']), ('ffn_w_down1', rows_stacked(gf[1]['w_down'])),
                         ('b_w_in', rows_stacked(g_b_w_in[0])), ('b_w_out', rows_stacked(g_b_w_out))], (dx,))

    dx, dxb, gf[0] = _ffn_backward(dx, dxb, saved['f0'], ffn_g[0], w_up[0], ffn_cw[0], w_down[0], 0, behind)
    full_small = {
        'b_norm_g': g_b_norm_g, 'b_w_grp': g_b_w_grp[None], 'b_b_grp': g_b_b_grp.reshape(b_b_grp.shape[:2] + (-1,)),
        'b_scale': g_b_scale, 'c_norm_g': g_c_norm_g, 'c_b_in': g_c_b_in, 'c_conv_w': g_c_conv_w[None],
        'c_conv_b': g_c_conv_b, 'c_w_a': g_c_w_a[None], 'c_b_a': g_c_b_a.reshape(c_b_a.shape[:2] + (-1,)),
        'c_w_i': g_c_w_i[None], 'c_b_i': g_c_b_i.reshape(c_b_i.shape[:2] + (-1,)), 'c_lambda': g_c_lambda,
        'd_norm_g': g_d_norm_g, 'd_conv_w': g_d_conv_w[None],
        'ffn_conv_w': jnp.stack([gf[l]['conv_w'].transpose(2, 0, 1, 3).reshape(ffn_conv_w.shape[1], -1) for l in range(depth)])}
    small_grads = _pack_stacked([_to_stacked(full_small[n], SMALL_SHARDED[n]) for n in small_names], small_rows)
    behind = advance(3, [('ffn_w_up0', gf[0]['w_up']), ('ffn_w_down0', rows_stacked(gf[0]['w_down'])), ('small', small_grads)],
                     (dx,))

    xin, h, z, y = saved['a']
    dz, g_a_b_in, g_a_v_norm_g, g_a_w_s, g_a_b_s = a_mid_bwd(z, dxb, wa_out, a_v_norm_g, a_w_s[0], a_bst, "a_mid_bwd", behind)
    g_a_w_out = mm_dw_out(y, dxb, "a_out_dw")
    behind = advance(4, [('a_w_out', rows_stacked(g_a_w_out))], (dz,))
    g_a_w_in = mm_dw_in(h, dz, nb, "a_in_dw")
    behind = advance(5, [('a_w_in', g_a_w_in)], (dz, *behind))
    behind = advance('5b', [], (dz, *behind), finish_older=False)
    dx, _, g_a_norm_g = mm_dx_in(dz, wa_in, "a_in_dx", behind, norm=(xin, a_norm_g, dx))
    grad_x = dx[None]

    tril = jnp.tril(jnp.ones((A_CHUNK, A_CHUNK), bool))
    repl_full = {
        'a_norm_g': g_a_norm_g, 'a_b_in': g_a_b_in, 'a_v_norm_g': g_a_v_norm_g,
        'a_w_s': jnp.where(tril, g_a_w_s, 0.0)[None], 'a_b_s': g_a_b_s[:, ::LANES].T[None],
        'ffn_norm_g': jnp.concatenate([gf[l]['norm_g'] for l in range(depth)], axis=0),
        'ffn_conv_b': jnp.stack([gf[l]['conv_b'].reshape(-1) for l in range(depth)]), 'final_norm_g': d_final_g.reshape(-1)}
    repl_grads = _pack([repl_full[n] for n in REPLICATED], repl_rows).reshape(N_DEV, repl_rows // N_DEV, LANES)
    behind = advance(6, [('repl', repl_grads)], (dx,))
    behind = advance(7, [], (dx, *behind))
    advance(8, [], (dx, *behind))

    outs = [{}, {}, {}, {}]
    for i, dst in enumerate(outs):
        for n in ('a_w_in', 'a_w_out', 'b_w_in', 'b_w_out', 'c_w_in', 'c_w_out', 'd_w_in', 'd_w_out'):
            dst[n] = updated[n][i][None]
        dst['ffn_w_up'] = jnp.swapaxes(updated['ffn_w_up'][i], 1, 2)
        dst['ffn_w_down'] = updated['ffn_w_down'][i]
        dst.update(zip(small_names, _unpack(updated['small'][i], small_shapes)))
        dst.update(zip(REPLICATED, _unpack(updated['repl'][i], repl_shapes)))
    out_g, out_d, out_m, out_v = outs

    return (loss, grad_x, *[out_g[n] for n in WEIGHTS], *[out_d[n] for n in WEIGHTS], *[out_m[n] for n in WEIGHTS],
            *[out_v[n] for n in WEIGHTS])
```

```python
import functools

import jax
import jax.numpy as jnp
from jax import lax
from jax.experimental import pallas as pl
from jax.experimental.pallas import tpu as pltpu

F32, BF16 = jnp.float32, jnp.bfloat16
MESH_ID = pl.DeviceIdType.MESH
N_DEV = 8
V7X_VMEM_LIMIT_BYTES = 56 << 20
LANES = 128
HALO = 8
POOL_HALO = 16

EPS = 1e-6
A_CHUNK, A_GROUPS = 128, 4
B_WINDOWS = (2, 4, 8, 16)
C_GATE_C = 8.0
ADAM_LR, ADAM_B1, ADAM_B2, ADAM_EPS, ADAM_WD, ADAM_STEP = 0.001, 0.9, 0.999, 1e-08, 0.01, 10

WEIGHTS = ['a_norm_g', 'a_w_in', 'a_b_in', 'a_v_norm_g', 'a_w_s', 'a_b_s', 'a_w_out', 'b_norm_g', 'b_w_in', 'b_w_grp',
           'b_b_grp', 'b_scale', 'b_w_out', 'c_norm_g', 'c_w_in', 'c_b_in', 'c_conv_w', 'c_conv_b', 'c_w_a', 'c_b_a',
           'c_w_i', 'c_b_i', 'c_lambda', 'c_w_out', 'd_norm_g', 'd_w_in', 'd_conv_w', 'd_w_out', 'ffn_norm_g',
           'ffn_w_up', 'ffn_conv_w', 'ffn_conv_b', 'ffn_w_down', 'final_norm_g']
SMALL_SHARDED = {'b_norm_g': 1, 'b_w_grp': 2, 'b_b_grp': 2, 'b_scale': 1, 'c_norm_g': 1, 'c_b_in': 1, 'c_conv_w': 2,
                 'c_conv_b': 1, 'c_w_a': 2, 'c_b_a': 2, 'c_w_i': 2, 'c_b_i': 2, 'c_lambda': 1, 'd_norm_g': 1,
                 'd_conv_w': 2, 'ffn_conv_w': 2}
REPLICATED = ['a_norm_g', 'a_b_in', 'a_v_norm_g', 'a_w_s', 'a_b_s', 'ffn_norm_g', 'ffn_conv_b', 'final_norm_g']


_GELU_C0, _GELU_C1 = 0.7978845608028654, 0.044715


def _gelu(x):
    return 0.5 * x * (1.0 + jnp.tanh(_GELU_C0 * (x + _GELU_C1 * (x * x * x))))


def _gelu_grad(x):
    t = jnp.tanh(_GELU_C0 * (x + _GELU_C1 * (x * x * x)))
    return 0.5 * (1.0 + t) + 0.5 * x * (1.0 - t * t) * (_GELU_C0 * (1.0 + 3.0 * _GELU_C1 * (x * x)))


def _sigmoid(x):
    return jax.nn.sigmoid(x)


def _log1p(x):
    u = 1.0 + x
    return jnp.where(u == 1.0, x, jnp.log(u) * (x / (u - 1.0)))


def _softplus(x):
    return jnp.maximum(x, 0.0) + _log1p(jnp.exp(-jnp.abs(x)))


def _expm1(x):
    poly = x * (1.0 + x * (1 / 2) * (1.0 + x * (1 / 3) * (1.0 + x * (1 / 4) * (1.0 + x * (1 / 5) * (
        1.0 + x * (1 / 6) * (1.0 + x * (1 / 7) * (1.0 + x * (1 / 8))))))))
    return jnp.where(jnp.abs(x) < 0.35, poly, jnp.exp(x) - 1.0)


def _down(xe, s):
    return xe if s == 0 else pltpu.roll(xe, s, 0)


def _up(xe, s):
    return xe if s == 0 else pltpu.roll(xe, xe.shape[0] - s, 0)


def _conv_ext(xe, w, taps):
    y = xe * w[taps - 1:taps]
    for s in range(1, taps):
        y = y + _down(xe, s) * w[taps - 1 - s:taps - s]
    return y


def _acc(ref, val, first):
    @pl.when(first)
    def _():
        ref[...] = val

    @pl.when(jnp.logical_not(first))
    def _():
        ref[...] += val


def _colsum(v):
    return jnp.sum(v, axis=0, keepdims=True)


def _dot(a, b, dims=((1,), (0,))):
    return lax.dot_general(a.astype(BF16), b.astype(BF16), (dims, ((), ())), preferred_element_type=F32)


_NN, _NT, _TN = ((1,), (0,)), ((1,), (1,)), ((0,), (0,))


def _call(body, name, grid, in_specs, out_specs, out_shape, scratch=(), after=()):
    n_in, n_after = len(in_specs), len(after)

    def ordered_body(*refs):
        return body(*refs[:n_in], *refs[n_in + n_after:])

    call = pl.pallas_call(
        ordered_body if n_after else body, name=name, grid=grid,
        in_specs=list(in_specs) + [pl.BlockSpec(memory_space=pl.ANY)] * n_after, out_specs=out_specs,
        out_shape=out_shape, scratch_shapes=list(scratch),
        compiler_params=pltpu.CompilerParams(dimension_semantics=("arbitrary",) * len(grid),
                                             vmem_limit_bytes=V7X_VMEM_LIMIT_BYTES))
    return lambda *args: call(*args, *after)


def _rows(m, t):
    t = min(m, t)
    assert m % t == 0, (m, t)
    return t


def _sds(shape, dtype=F32):
    return jax.ShapeDtypeStruct(tuple(shape), dtype)


def _prev_halo(tm, halo=HALO):
    return lambda i: jnp.maximum(i * (tm // halo) - 1, 0)


def _next_halo(tm, m, halo=HALO):
    return lambda i: jnp.minimum((i + 1) * (tm // halo), m // halo - 1)


def _matmul(name, ins, in_specs, out_shape, o_spec, grid, compute, after=()):
    def body(*refs):
        refs[-1][...] = compute(*refs[:-1]).astype(refs[-1].dtype)

    return _call(body, name, grid, in_specs, o_spec, out_shape, (), after)(*ins)


def mm_in(h, w_st, name, bias=None, stacked_out=False, out_dtype=F32, rows=1024):
    m, k = h.shape
    nb, _, n = w_st.shape
    tm = _rows(m, rows)
    in_specs = [pl.BlockSpec((tm, k), lambda i, j: (i, 0)), pl.BlockSpec((None, k, n), lambda i, j: (j, 0, 0))]
    if stacked_out:
        out, o_spec = _sds((nb, m, n), out_dtype), pl.BlockSpec((None, tm, n), lambda i, j: (j, i, 0))
    else:
        out, o_spec = _sds((m, nb * n), out_dtype), pl.BlockSpec((tm, n), lambda i, j: (i, j))
    if bias is None:
        return _matmul(name, (h, w_st), in_specs, out, o_spec, (m // tm, nb), lambda a, b: _dot(a[...], b[...]))
    in_specs.append(pl.BlockSpec((1, n), lambda i, j: (0, j)))
    return _matmul(name, (h, w_st, bias), in_specs, out, o_spec, (m // tm, nb),
                   lambda a, b, c: _dot(a[...], b[...]) + c[...])


def _split_rows(kf):
    g = max(1, kf // 1024)
    return g, kf // g


def _resident(shape):
    return pl.BlockSpec(shape, lambda *_: (0,) * len(shape), pipeline_mode=pl.Buffered(1))


def _norm_rows(xv, g):
    return (xv * lax.rsqrt(jnp.mean(xv * xv, axis=-1, keepdims=True) + EPS) * g).astype(BF16)


def _project_out(y_ref, w_ref, res_ref, g_ref, x_ref, h_ref):
    xv = res_ref[...] + _dot(y_ref[...], w_ref[...])
    x_ref[...] = xv
    h_ref[...] = _norm_rows(xv, g_ref[...])


def _call_projected(body, name, m, tm, in_specs, ins, own_outs, w_out, res, next_g, scratch=()):
    n = w_out.shape[1]
    row = lambda width: pl.BlockSpec((tm, width), lambda i: (i, 0))
    specs = list(in_specs) + [_resident(w_out.shape), row(n), pl.BlockSpec((1, n), lambda i: (0, 0))]
    out_specs = [row(width) for width, _ in own_outs] + [row(n), row(n)]
    out_shape = [_sds((m, width), dtype) for width, dtype in own_outs] + [_sds((m, n)), _sds((m, n), BF16)]
    return _call(body, name, (m // tm,), specs, out_specs, out_shape, scratch)(*ins, w_out, res, next_g)


def mm_out(y, w, res, next_g, name, after=()):
    kf, n = w.shape
    m = y.shape[0]
    tm = _rows(m, 512)
    row = pl.BlockSpec((tm, n), lambda i: (i, 0))

    def body(y_ref, w_ref, res_ref, g_ref, x_ref, h_ref):
        xv = res_ref[...] + _dot(y_ref[...], w_ref[...])
        x_ref[...] = xv
        h_ref[...] = _norm_rows(xv, g_ref[...])

    in_specs = [pl.BlockSpec((tm, kf), lambda i: (i, 0)), _resident((kf, n)), row, pl.BlockSpec((1, n), lambda i: (0, 0))]
    return _call(body, name, (m // tm,), in_specs, [row, row], [_sds((m, n)), _sds((m, n), BF16)], (), after)(y, w, res, next_g)


def mm_dx_in(dz, w_st, name, after=(), norm=None):
    nb, k, n = w_st.shape
    m = dz.shape[-2]
    tm = _rows(m, 512)
    w_spec = _resident((nb, k, n))
    if dz.ndim == 3:
        in_specs = [pl.BlockSpec((None, tm, n), lambda i, r=r: (r, i, 0)) for r in range(nb)] + [w_spec]

        def compute(*refs):
            acc = _dot(refs[0][...], refs[nb][0], _NT)
            for r in range(1, nb):
                acc = acc + _dot(refs[r][...], refs[nb][r], _NT)
            return acc

        ins = (*[dz] * nb, w_st)
    else:
        in_specs = [pl.BlockSpec((tm, nb * n), lambda i: (i, 0)), w_spec]

        def compute(dz_ref, w_ref):
            acc = _dot(dz_ref[:, :n], w_ref[0], _NT)
            for r in range(1, nb):
                acc = acc + _dot(dz_ref[:, r * n:(r + 1) * n], w_ref[r], _NT)
            return acc

        ins = (dz, w_st)
    row = pl.BlockSpec((tm, k), lambda i: (i, 0))
    if norm is None:
        return _matmul(name, ins, in_specs, _sds((m, k)), row, (m // tm,), compute, after)
    n_in = len(in_specs)
    vec = pl.BlockSpec((1, k), lambda i: (0, 0))

    def body(*refs):
        x_ref, g_ref, dr_ref, dx_ref, dxb_ref, dg_ref = refs[n_in:]
        dx, dg = _rms_bwd_math(x_ref[...], g_ref[...], compute(*refs[:n_in]))
        dx = dr_ref[...] + dx
        dx_ref[...] = dx
        dxb_ref[...] = dx.astype(BF16)
        _acc(dg_ref, dg, pl.program_id(0) == 0)

    return _call(body, name, (m // tm,), in_specs + [row, vec, row], [row, row, vec],
                 [_sds((m, k)), _sds((m, k), BF16), _sds((1, k))], (), after)(*ins, *norm)


def mm_dx_out(dout, w, name, groups=None, after=()):
    kf, n = w.shape
    m = dout.shape[0]
    tm = _rows(m, 1024)
    g, k = (groups, kf // groups) if groups else _split_rows(kf)
    in_specs = [pl.BlockSpec((tm, n), lambda i, j: (i, 0)), pl.BlockSpec((None, k, n), lambda i, j: (j, 0, 0))]
    if groups:
        out, o_spec = _sds((g, m, k)), pl.BlockSpec((None, tm, k), lambda i, j: (j, i, 0))
    else:
        out, o_spec = _sds((m, kf)), pl.BlockSpec((tm, k), lambda i, j: (i, j))
    return _matmul(name, (dout, w.reshape(g, k, n)), in_specs, out, o_spec, (m // tm, g),
                   lambda a, b: _dot(a[...], b[...], _NT), after)


def mm_dw_in(h, dz, nb, name, transposed=False):
    m, k = h.shape
    if dz.ndim == 3:
        n = dz.shape[2]
        dz_spec = pl.BlockSpec((None, m, n), lambda j: (j, 0, 0))
    else:
        n = dz.shape[1] // nb
        dz_spec = pl.BlockSpec((m, n), lambda j: (0, j))
    in_specs = [_resident((m, k)), dz_spec]
    if transposed:
        return _matmul(name, (h, dz), in_specs, _sds((nb, n, k), BF16), pl.BlockSpec((None, n, k), lambda j: (j, 0, 0)), (nb,),
                       lambda a, b: _dot(b[...], a[...], _TN))
    return _matmul(name, (h, dz), in_specs, _sds((nb, k, n), BF16), pl.BlockSpec((None, k, n), lambda j: (j, 0, 0)), (nb,),
                   lambda a, b: _dot(a[...], b[...], _TN))


def mm_dw_out(y, dout, name):
    m, n = dout.shape
    if y.ndim == 3:
        g, _, k = y.shape
        y_spec = pl.BlockSpec((None, m, k), lambda j: (j, 0, 0))
    else:
        g, k = _split_rows(y.shape[1])
        y_spec = pl.BlockSpec((m, k), lambda j: (0, j))
    in_specs = [y_spec, _resident((m, n))]
    out = _matmul(name, (y, dout), in_specs, _sds((g, k, n), BF16), pl.BlockSpec((None, k, n), lambda j: (j, 0, 0)), (g,),
                  lambda a, b: _dot(a[...], b[...], _TN))
    return out.reshape(g * k, n)


def rms_fwd(x, g, name):
    m, d = x.shape
    tm = _rows(m, 512)

    def body(x_ref, g_ref, o_ref):
        xv = x_ref[...]
        rstd = lax.rsqrt(jnp.mean(xv * xv, axis=-1, keepdims=True) + EPS)
        o_ref[...] = (xv * rstd * g_ref[...]).astype(BF16)

    row = pl.BlockSpec((tm, d), lambda i: (i, 0))
    vec = pl.BlockSpec((1, d), lambda i: (0, 0))
    return _call(body, name, (m // tm,), [row, vec], row, _sds((m, d), BF16))(x, g)


def _rms_bwd_math(xv, g, dh):
    rstd = lax.rsqrt(jnp.mean(xv * xv, axis=-1, keepdims=True) + EPS)
    xhat = xv * rstd
    dxhat = dh * g
    dx = rstd * (dxhat - xhat * jnp.mean(dxhat * xhat, axis=-1, keepdims=True))
    return dx, _colsum(dh * xhat)


def final_loss(x, g, target, name):
    m, d = x.shape
    tm = _rows(m, 512)

    def body(x_ref, g_ref, t_ref, l_ref, dx_ref, dxb_ref, dg_ref):
        xv, gv = x_ref[...], g_ref[...]
        rstd = lax.rsqrt(jnp.mean(xv * xv, axis=-1, keepdims=True) + EPS)
        err = xv * rstd * gv - t_ref[...]
        part = 0.5 * jnp.sum(jnp.mean(err * err, axis=-1, keepdims=True), axis=0, keepdims=True)
        dx, dg = _rms_bwd_math(xv, gv, err * (1.0 / d))
        dx_ref[...] = dx
        dxb_ref[...] = dx.astype(BF16)
        first = pl.program_id(0) == 0
        _acc(l_ref, jnp.broadcast_to(part, l_ref.shape), first)
        _acc(dg_ref, dg, first)

    row = pl.BlockSpec((tm, d), lambda i: (i, 0))
    vec = pl.BlockSpec((1, d), lambda i: (0, 0))
    lsp = pl.BlockSpec((1, LANES), lambda i: (0, 0))
    return _call(body, name, (m // tm,), [row, vec, row], [lsp, row, row, vec],
                 [_sds((1, LANES)), _sds((m, d)), _sds((m, d), BF16), _sds((1, d))])(x, g, target)


def _a_common(z_ref, vg_ref, ws_ref, bst_ref, tm, width):
    gw = width // A_GROUPS
    zp = z_ref[...]
    z = _gelu(zp)
    u, v = z[:, :width], z[:, width:]
    rstd = lax.rsqrt(jnp.mean(v * v, axis=-1, keepdims=True) + EPS)
    vhat = v * rstd
    vn = vhat * vg_ref[...]
    t_i = lax.broadcasted_iota(jnp.int32, (A_CHUNK, A_CHUNK), 0)
    s_i = lax.broadcasted_iota(jnp.int32, (A_CHUNK, A_CHUNK), 1)
    wsm = [jnp.where(s_i <= t_i, ws_ref[g], 0.0).astype(BF16) for g in range(A_GROUPS)]
    bst = bst_ref[...]
    return zp, u, rstd, vhat, vn.astype(BF16), wsm, bst, gw


def a_mid_fwd(z, vg, ws, bst, name):
    m, w2 = z.shape
    width = w2 // 2
    tm = _rows(m, 256)

    def body(z_ref, vg_ref, ws_ref, bst_ref, y_ref):
        _, u, _, _, vnb, wsm, bst, gw = _a_common(z_ref, vg_ref, ws_ref, bst_ref, tm, width)
        for c in range(tm // A_CHUNK):
            r0 = c * A_CHUNK
            for g in range(A_GROUPS):
                c0 = g * gw
                vs = _dot(wsm[g], vnb[r0:r0 + A_CHUNK, c0:c0 + gw]) + bst[:, g:g + 1]
                y_ref[r0:r0 + A_CHUNK, c0:c0 + gw] = (u[r0:r0 + A_CHUNK, c0:c0 + gw] * vs).astype(BF16)

    in_specs = [pl.BlockSpec((tm, w2), lambda i: (i, 0)), pl.BlockSpec((1, width), lambda i: (0, 0)),
                pl.BlockSpec((A_GROUPS, A_CHUNK, A_CHUNK), lambda i: (0, 0, 0)),
                pl.BlockSpec((A_CHUNK, A_GROUPS), lambda i: (0, 0))]
    return _call(body, name, (m // tm,), in_specs, pl.BlockSpec((tm, width), lambda i: (i, 0)),
                 _sds((m, width), BF16))(z, vg, ws, bst)


def a_mid_bwd(z, dx, w_out, vg, ws, bst, name, after=()):
    m, w2 = z.shape
    width = w2 // 2
    tm = _rows(m, 256)

    def body(z_ref, dx_ref, wo_ref, vg_ref, ws_ref, bst_ref, dz_ref, dbin_ref, dvg_ref, dws_ref, dbs_ref, dvn_scr, du_scr):
        first = pl.program_id(0) == 0
        zp, u, rstd, vhat, vnb, wsm, bst, gw = _a_common(z_ref, vg_ref, ws_ref, bst_ref, tm, width)
        dy = _dot(dx_ref[...], wo_ref[...], _NT)
        dws = [jnp.zeros((A_CHUNK, A_CHUNK), F32) for _ in range(A_GROUPS)]
        dbs = [jnp.zeros((A_CHUNK, 1), F32) for _ in range(A_GROUPS)]
        for c in range(tm // A_CHUNK):
            r0 = c * A_CHUNK
            for g in range(A_GROUPS):
                c0 = g * gw
                vn_cg = vnb[r0:r0 + A_CHUNK, c0:c0 + gw]
                vs = _dot(wsm[g], vn_cg) + bst[:, g:g + 1]
                dy_cg = dy[r0:r0 + A_CHUNK, c0:c0 + gw]
                dvs = dy_cg * u[r0:r0 + A_CHUNK, c0:c0 + gw]
                du_scr[r0:r0 + A_CHUNK, c0:c0 + gw] = dy_cg * vs
                dws[g] = dws[g] + _dot(dvs, vn_cg, _NT)
                dbs[g] = dbs[g] + jnp.sum(dvs, axis=1, keepdims=True)
                dvn_scr[r0:r0 + A_CHUNK, c0:c0 + gw] = _dot(wsm[g], dvs, _TN)
        for g in range(A_GROUPS):
            _acc(dws_ref.at[g], dws[g], first)
            _acc(dbs_ref.at[:, g * LANES:(g + 1) * LANES], jnp.broadcast_to(dbs[g], (A_CHUNK, LANES)), first)
        dvn = dvn_scr[...]
        _acc(dvg_ref, _colsum(dvn * vhat), first)
        dvhat = dvn * vg_ref[...]
        dv = rstd * (dvhat - vhat * jnp.mean(dvhat * vhat, axis=-1, keepdims=True))
        gg = _gelu_grad(zp)
        dzu = du_scr[...] * gg[:, :width]
        dzv = dv * gg[:, width:]
        dz_ref[:, :width] = dzu.astype(BF16)
        dz_ref[:, width:] = dzv.astype(BF16)
        _acc(dbin_ref.at[:, :width], _colsum(dzu), first)
        _acc(dbin_ref.at[:, width:], _colsum(dzv), first)

    const2 = lambda i: (0, 0)
    in_specs = [pl.BlockSpec((tm, w2), lambda i: (i, 0)), pl.BlockSpec((tm, dx.shape[1]), lambda i: (i, 0)),
                _resident(w_out.shape), pl.BlockSpec((1, width), const2),
                pl.BlockSpec((A_GROUPS, A_CHUNK, A_CHUNK), lambda i: (0, 0, 0)), pl.BlockSpec((A_CHUNK, A_GROUPS), const2)]
    out_specs = [pl.BlockSpec((tm, w2), lambda i: (i, 0)), pl.BlockSpec((1, w2), const2), pl.BlockSpec((1, width), const2),
                 pl.BlockSpec((A_GROUPS, A_CHUNK, A_CHUNK), lambda i: (0, 0, 0)),
                 pl.BlockSpec((A_CHUNK, A_GROUPS * LANES), const2)]
    out_shape = [_sds((m, w2), BF16), _sds((1, w2)), _sds((1, width)), _sds((A_GROUPS, A_CHUNK, A_CHUNK)),
                 _sds((A_CHUNK, A_GROUPS * LANES))]
    scratch = [pltpu.VMEM((tm, width), F32), pltpu.VMEM((tm, width), F32)]
    return _call(body, name, (m // tm,), in_specs, out_specs, out_shape, scratch, after)(z, dx, w_out, vg, ws, bst)


def _pool_minus_id(ze, i, tm, gw):
    pos = i * tm + lax.broadcasted_iota(jnp.int32, (tm, 1), 0)
    out = []
    for gi, win in enumerate(B_WINDOWS):
        s = ze[:, gi * gw:(gi + 1) * gw]
        step = 1
        while step < win:
            s = s + _down(s, step)
            step *= 2
        inv = 1.0 / jnp.minimum(pos + 1, win).astype(F32)
        out.append(s[POOL_HALO:] * inv - ze[POOL_HALO:, gi * gw:(gi + 1) * gw])
    return out


def _b_specs(tm, width):
    return [pl.BlockSpec((POOL_HALO, width), lambda i: (_prev_halo(tm, POOL_HALO)(i), 0)),
            pl.BlockSpec((tm, width), lambda i: (i, 0))]


def b_mid_fwd(z, wgrp, bgrp, scale, w_out, res, next_g, name):
    m, width = z.shape
    ng = len(B_WINDOWS)
    gw = width // ng
    tm = _rows(m, 512)

    def body(zp_ref, zm_ref, w_ref, b_ref, s_ref, *proj):
        i = pl.program_id(0)
        y_ref = proj[3]
        ze = jnp.concatenate([zp_ref[...] * (i > 0).astype(F32), zm_ref[...]], axis=0)
        p = _pool_minus_id(ze, i, tm, gw)
        for g in range(ng):
            cs = slice(g * gw, (g + 1) * gw)
            y = (_dot(p[g], w_ref[g]) + b_ref[:, cs]) * s_ref[:, cs]
            y_ref[:, cs] = y.astype(BF16)
        _project_out(y_ref, *proj[:3], *proj[4:])

    vec = pl.BlockSpec((1, width), lambda i: (0, 0))
    in_specs = _b_specs(tm, width) + [pl.BlockSpec((ng, gw, gw), lambda i: (0, 0, 0)), vec, vec]
    return _call_projected(body, name, m, tm, in_specs, (z, z, wgrp, bgrp, scale), [(width, BF16)], w_out, res, next_g)


def b_mid_bwd(z, dx, w_out, wgrp, bgrp, scale, name):
    m, width = z.shape
    ng = len(B_WINDOWS)
    gw = width // ng
    tm = _rows(m, 512)

    def body(zp_ref, zm_ref, dx_ref, wo_ref, w_ref, b_ref, s_ref, dp_ref, dw_ref, db_ref, ds_ref):
        i = pl.program_id(0)
        first = i == 0
        ze = jnp.concatenate([zp_ref[...] * (i > 0).astype(F32), zm_ref[...]], axis=0)
        p = _pool_minus_id(ze, i, tm, gw)
        dy = _dot(dx_ref[...], wo_ref[...], _NT)
        for g in range(ng):
            cs = slice(g * gw, (g + 1) * gw)
            dyg = dy[:, cs]
            ypre = _dot(p[g], w_ref[g]) + b_ref[:, cs]
            dyp = dyg * s_ref[:, cs]
            _acc(ds_ref.at[:, cs], _colsum(dyg * ypre), first)
            _acc(db_ref.at[:, cs], _colsum(dyp), first)
            _acc(dw_ref.at[g], _dot(p[g], dyp, _TN), first)
            dp_ref[:, cs] = _dot(dyp, w_ref[g], _NT)

    vec = pl.BlockSpec((1, width), lambda i: (0, 0))
    row = pl.BlockSpec((tm, width), lambda i: (i, 0))
    wsp = pl.BlockSpec((ng, gw, gw), lambda i: (0, 0, 0))
    in_specs = _b_specs(tm, width) + [pl.BlockSpec((tm, dx.shape[1]), lambda i: (i, 0)), _resident(w_out.shape), wsp, vec, vec]
    return _call(body, name, (m // tm,), in_specs, [row, wsp, vec, vec],
                 [_sds((m, width)), _sds((ng, gw, gw)), _sds((1, width)), _sds((1, width))])(z, z, dx, w_out, wgrp, bgrp, scale)


def b_pool_bwd(dp, name):
    m, width = dp.shape
    gw = width // len(B_WINDOWS)
    tm = _rows(m, 512)
    n_i = m // tm

    def body(dm_ref, dn_ref, dz_ref):
        i = pl.program_id(0)
        de = jnp.concatenate([dm_ref[...], dn_ref[...] * (i < n_i - 1).astype(F32)], axis=0)
        pos = i * tm + lax.broadcasted_iota(jnp.int32, (tm + POOL_HALO, 1), 0)
        for gi, win in enumerate(B_WINDOWS):
            cs = slice(gi * gw, (gi + 1) * gw)
            d = de[:, cs]
            s = d * (1.0 / jnp.minimum(pos + 1, win).astype(F32))
            step = 1
            while step < win:
                s = s + _up(s, step)
                step *= 2
            dz_ref[:, cs] = (s[:tm] - d[:tm]).astype(BF16)

    in_specs = [pl.BlockSpec((tm, width), lambda i: (i, 0)),
                pl.BlockSpec((POOL_HALO, width), lambda i: (_next_halo(tm, m, POOL_HALO)(i), 0))]
    return _call(body, name, (n_i,), in_specs, pl.BlockSpec((tm, width), lambda i: (i, 0)), _sds((m, width), BF16))(dp, dp)


def _c_gates(xr, wa_ref, ba_ref, wi_ref, bi_ref, lam_ref, heads, hw):
    xb = xr.astype(BF16)
    ra = jnp.concatenate([_dot(xb[:, h * hw:(h + 1) * hw], wa_ref[h]) for h in range(heads)], axis=1) + ba_ref[...]
    ia = jnp.concatenate([_dot(xb[:, h * hw:(h + 1) * hw], wi_ref[h]) for h in range(heads)], axis=1) + bi_ref[...]
    r, ig = _sigmoid(ra), _sigmoid(ia)
    sp = _softplus(-lam_ref[...])
    log_a = (-C_GATE_C * r) * sp
    a = jnp.exp(log_a)
    mult = jnp.sqrt(-_expm1(2.0 * log_a))
    return xb, r, ig, sp, a, mult


def c_mid_fwd(z, cw, cb, wa, ba, wi, bi, lam, name):
    m, w2 = z.shape
    width = w2 // 2
    heads, hw = wa.shape[0], wa.shape[1]
    taps = cw.shape[0]
    tm = _rows(m, 512)

    def body(zp_ref, zm_ref, cw_ref, cb_ref, wa_ref, ba_ref, wi_ref, bi_ref, lam_ref, a_ref, b_ref, xr_ref):
        i = pl.program_id(0)
        xe = jnp.concatenate([zp_ref[...] * (i > 0).astype(F32), zm_ref[...]], axis=0)
        xr = _conv_ext(xe, cw_ref[...], taps)[HALO:] + cb_ref[...]
        _, _, ig, _, a, mult = _c_gates(xr, wa_ref, ba_ref, wi_ref, bi_ref, lam_ref, heads, hw)
        a_ref[...] = a
        b_ref[...] = mult * (ig * xr)
        xr_ref[...] = xr

    vec = pl.BlockSpec((1, width), lambda i: (0, 0))
    row = pl.BlockSpec((tm, width), lambda i: (i, 0))
    wsp = pl.BlockSpec((heads, hw, hw), lambda i: (0, 0, 0))
    in_specs = [pl.BlockSpec((HALO, width), lambda i: (_prev_halo(tm)(i), 1)), pl.BlockSpec((tm, width), lambda i: (i, 1)),
                pl.BlockSpec((taps, width), lambda i: (0, 0)), vec, wsp, vec, wsp, vec, vec]
    return _call(body, name, (m // tm,), in_specs, [row, row, row], [_sds((m, width))] * 3)(
        z, z, cw, cb, wa, ba, wi, bi, lam)


_SCAN_ROWS = 512


def c_scan_fwd(a, b, z, w_out, res, next_g, name):
    m, width = a.shape
    tm = _rows(m, _SCAN_ROWS)

    def body(a_ref, b_ref, g_ref, wo_ref, res_ref, ng_ref, hs_ref, y_ref, x_ref, h_ref, h_carry):
        @pl.when(pl.program_id(0) == 0)
        def _():
            h_carry[...] = jnp.zeros_like(h_carry)

        def step(t, h):
            h = a_ref[pl.ds(t, 1), :] * h + b_ref[pl.ds(t, 1), :]
            hs_ref[pl.ds(t, 1), :] = h
            return h

        h_carry[...] = lax.fori_loop(0, tm, step, h_carry[...], unroll=8)
        y_ref[...] = (hs_ref[...] * _gelu(g_ref[...])).astype(BF16)
        _project_out(y_ref, wo_ref, res_ref, ng_ref, x_ref, h_ref)

    row = pl.BlockSpec((tm, width), lambda i: (i, 0))
    return _call_projected(body, name, m, tm, [row, row, row], (a, b, z), [(width, F32), (width, BF16)], w_out, res, next_g,
                           [pltpu.VMEM((1, width), F32)])


def c_scan_bwd(dx, w_out, z, hs, a, name):
    m, width = a.shape
    tm = _rows(m, _SCAN_ROWS)
    n_i = m // tm

    def body(dx_ref, wo_ref, g_ref, hs_ref, hp_ref, a_ref, lam_ref, da_ref, dg_ref, dgs_ref, lam_carry, a_carry):
        i = pl.program_id(0)
        first = i == 0

        @pl.when(first)
        def _():
            lam_carry[...] = jnp.zeros_like(lam_carry)
            a_carry[...] = jnp.zeros_like(a_carry)

        gp, dyv, hsv = g_ref[...], _dot(dx_ref[...], wo_ref[...], _NT), hs_ref[...]
        dgate = dyv * hsv * _gelu_grad(gp)
        dg_ref[...] = dgate.astype(BF16)
        _acc(dgs_ref, _colsum(dgate), first)
        lam_ref[...] = dyv * _gelu(gp)

        def step(k, carry):
            lam_next, a_next = carry
            t = tm - 1 - k
            lam_t = lam_ref[pl.ds(t, 1), :] + a_next * lam_next
            lam_ref[pl.ds(t, 1), :] = lam_t
            return lam_t, a_ref[pl.ds(t, 1), :]

        lam_c, a_c = lax.fori_loop(0, tm, step, (lam_carry[...], a_carry[...]), unroll=8)
        lam_carry[...] = lam_c
        a_carry[...] = a_c
        h_before = hp_ref[HALO - 1:HALO, :] * (i < n_i - 1).astype(F32)
        t_i = lax.broadcasted_iota(jnp.int32, (tm, 1), 0)
        da_ref[...] = lam_ref[...] * jnp.where(t_i == 0, h_before, _down(hsv, 1))

    row = pl.BlockSpec((tm, width), lambda i: (n_i - 1 - i, 0))
    halo = pl.BlockSpec((HALO, width), lambda i: (_prev_halo(tm)(n_i - 1 - i), 0))
    vec = pl.BlockSpec((1, width), lambda i: (0, 0))
    in_specs = [pl.BlockSpec((tm, dx.shape[1]), lambda i: (n_i - 1 - i, 0)), _resident(w_out.shape), row, row, halo, row]
    return _call(body, name, (n_i,), in_specs, [row, row, row, vec],
                 [_sds((m, width)), _sds((m, width)), _sds((m, width), BF16), _sds((1, width))],
                 [pltpu.VMEM((1, width), F32), pltpu.VMEM((1, width), F32)])(dx, w_out, z, hs, hs, a)


def c_mid_bwd(lam_seq, da, xr, wa, ba, wi, bi, lam, name):
    m, width = xr.shape
    heads, hw = wa.shape[0], wa.shape[1]
    tm = _rows(m, 512)

    def body(l_ref, da_ref, xr_ref, wa_ref, ba_ref, wi_ref, bi_ref, lam_ref,
             dxr_ref, dwa_ref, dwi_ref, dba_ref, dbi_ref, dlam_ref):
        first = pl.program_id(0) == 0
        xr_v, lmb = xr_ref[...], l_ref[...]
        xb, r, ig, sp, a, mult = _c_gates(xr_v, wa_ref, ba_ref, wi_ref, bi_ref, lam_ref, heads, hw)
        dmult = lmb * (ig * xr_v)
        dig = lmb * mult * xr_v
        dxr = lmb * mult * ig
        dla = da_ref[...] * a - dmult * (a * a) / mult
        dr = dla * (-C_GATE_C * sp)
        dsp = _colsum(dla * (-C_GATE_C * r))
        _acc(dlam_ref, dsp * (-_sigmoid(-lam_ref[...])), first)
        dra = dr * r * (1.0 - r)
        dia = dig * ig * (1.0 - ig)
        _acc(dba_ref, _colsum(dra), first)
        _acc(dbi_ref, _colsum(dia), first)
        for h in range(heads):
            cs = slice(h * hw, (h + 1) * hw)
            _acc(dwa_ref.at[h], _dot(xb[:, cs], dra[:, cs], _TN), first)
            _acc(dwi_ref.at[h], _dot(xb[:, cs], dia[:, cs], _TN), first)
            dxr_ref[:, cs] = dxr[:, cs] + _dot(dra[:, cs], wa_ref[h], _NT) + _dot(dia[:, cs], wi_ref[h], _NT)

    vec = pl.BlockSpec((1, width), lambda i: (0, 0))
    row = pl.BlockSpec((tm, width), lambda i: (i, 0))
    wsp = pl.BlockSpec((heads, hw, hw), lambda i: (0, 0, 0))
    return _call(body, name, (m // tm,), [row, row, row, wsp, vec, wsp, vec, vec], [row, wsp, wsp, vec, vec, vec],
                 [_sds((m, width)), _sds((heads, hw, hw)), _sds((heads, hw, hw)), _sds((1, width)), _sds((1, width)),
                  _sds((1, width))])(lam_seq, da, xr, wa, ba, wi, bi, lam)


def conv_bwd(dy, x_src, col_block, cw, name):
    m, width = dy.shape
    taps = cw.shape[0]
    tm = _rows(m, 512)
    n_i = m // tm

    def body(dm_ref, dn_ref, xp_ref, xm_ref, cw_ref, dx_ref, dw_ref, db_ref, dxs_ref):
        i = pl.program_id(0)
        first = i == 0
        de = jnp.concatenate([jnp.zeros((HALO, width), F32), dm_ref[...], dn_ref[...] * (i < n_i - 1).astype(F32)], axis=0)
        xe = jnp.concatenate([xp_ref[...] * (i > 0).astype(F32), xm_ref[...], jnp.zeros((HALO, width), F32)], axis=0)
        w = cw_ref[...]
        dx = de * w[taps - 1:taps]
        for s in range(1, taps):
            dx = dx + _up(de, s) * w[taps - 1 - s:taps - s]
        dx_ref[...] = dx[HALO:HALO + tm].astype(BF16)
        _acc(dxs_ref, _colsum(dx[HALO:HALO + tm]), first)
        dm = dm_ref[...]
        for s in range(taps):
            _acc(dw_ref.at[taps - 1 - s:taps - s, :], _colsum(dm * _down(xe, s)[HALO:HALO + tm]), first)
        _acc(db_ref, _colsum(dm), first)

    row = pl.BlockSpec((tm, width), lambda i: (i, 0))
    vec = pl.BlockSpec((1, width), lambda i: (0, 0))
    tsp = pl.BlockSpec((taps, width), lambda i: (0, 0))
    in_specs = [row, pl.BlockSpec((HALO, width), lambda i: (_next_halo(tm, m)(i), 0)),
                pl.BlockSpec((HALO, width), lambda i: (_prev_halo(tm)(i), col_block)),
                pl.BlockSpec((tm, width), lambda i: (i, col_block)), tsp]
    return _call(body, name, (n_i,), in_specs, [row, tsp, vec, vec],
                 [_sds((m, width), BF16), _sds((taps, width)), _sds((1, width)), _sds((1, width))])(dy, dy, x_src, x_src, cw)


def d_mid_fwd(z, cw, w_out, res, next_g, name):
    m, w3 = z.shape
    width = w3 // 3
    taps = cw.shape[0]
    tm = _rows(m, 512)

    def body(bm_ref, cp_ref, cm_ref, xp_ref, xm_ref, cw_ref, wo_ref, res_ref, ng_ref, y_ref, x_ref, h_ref):
        keep = (pl.program_id(0) > 0).astype(F32)
        qe = (jnp.concatenate([cp_ref[...] * keep, cm_ref[...]], axis=0)
              * jnp.concatenate([xp_ref[...], xm_ref[...]], axis=0))
        y_ref[...] = (bm_ref[...] * _conv_ext(qe, cw_ref[...], taps)[HALO:]).astype(BF16)
        _project_out(y_ref, wo_ref, res_ref, ng_ref, x_ref, h_ref)

    main = lambda c: pl.BlockSpec((tm, width), lambda i: (i, c))
    prev = lambda c: pl.BlockSpec((HALO, width), lambda i: (_prev_halo(tm)(i), c))
    in_specs = [main(0), prev(1), main(1), prev(2), main(2), pl.BlockSpec((taps, width), lambda i: (0, 0))]
    return _call_projected(body, name, m, tm, in_specs, (z, z, z, z, z, cw), [(width, BF16)], w_out, res, next_g)


def d_mid_bwd(z, dy, cw, name):
    m, w3 = z.shape
    width = w3 // 3
    taps = cw.shape[0]
    tm = _rows(m, 512)
    n_i = m // tm

    def body(bm_ref, bn_ref, cp_ref, cm_ref, cn_ref, xp_ref, xm_ref, xn_ref, dm_ref, dn_ref, cw_ref, dz_ref, dw_ref):
        i = pl.program_id(0)
        first = i == 0
        kp, kn = (i > 0).astype(F32), (i < n_i - 1).astype(F32)
        zeros = jnp.zeros((HALO, width), F32)
        ce = jnp.concatenate([cp_ref[...] * kp, cm_ref[...], cn_ref[...] * kn], axis=0)
        xe = jnp.concatenate([xp_ref[...], xm_ref[...], xn_ref[...]], axis=0)
        qe = ce * xe
        be = jnp.concatenate([zeros, bm_ref[...], bn_ref[...]], axis=0)
        dye = jnp.concatenate([zeros, dm_ref[...], dn_ref[...] * kn], axis=0)
        w = cw_ref[...]
        cq = _conv_ext(qe, w, taps)
        dcq = dye * be
        dq = dcq * w[taps - 1:taps]
        for s in range(1, taps):
            dq = dq + _up(dcq, s) * w[taps - 1 - s:taps - s]
        ms = slice(HALO, HALO + tm)
        dz_ref[:, :width] = (dye * cq)[ms].astype(BF16)
        dz_ref[:, width:2 * width] = (dq * xe)[ms].astype(BF16)
        dz_ref[:, 2 * width:] = (dq * ce)[ms].astype(BF16)
        for s in range(taps):
            _acc(dw_ref.at[taps - 1 - s:taps - s, :], _colsum(dcq[ms] * _down(qe, s)[ms]), first)

    main = lambda c: pl.BlockSpec((tm, width), lambda i: (i, c))
    prev = lambda c: pl.BlockSpec((HALO, width), lambda i: (_prev_halo(tm)(i), c))
    nxt = lambda c: pl.BlockSpec((HALO, width), lambda i: (_next_halo(tm, m)(i), c))
    tsp = pl.BlockSpec((taps, width), lambda i: (0, 0))
    in_specs = [main(0), nxt(0), prev(1), main(1), nxt(1), prev(2), main(2), nxt(2), main(0), nxt(0), tsp]
    return _call(body, name, (n_i,), in_specs, [pl.BlockSpec((tm, w3), lambda i: (i, 0)), tsp],
                 [_sds((m, w3), BF16), _sds((taps, width))])(z, z, z, z, z, z, z, z, dy, dy, cw)


def _halo_rows(dtype):
    return HALO * (4 // jnp.dtype(dtype).itemsize)


def ffn_gate_down(z, cw, cb, w_down, res, next_g, name):
    _, nj, m, c = z.shape
    n = w_down.shape[1]
    taps = cw.shape[2]
    tm = _rows(m, 256)
    hz = _halo_rows(z.dtype)

    def body(zp_ref, zm_ref, cw_ref, cb_ref, w_ref, res_ref, *rest):
        keep = (pl.program_id(0) > 0).astype(F32)
        zc_ref, act_ref, x_ref = rest[-4:-1] if next_g is not None else rest[-3:]
        acc = res_ref[...]
        for j in range(nj):
            zc = []
            for s in range(2):
                xe = jnp.concatenate([zp_ref[s, j].astype(F32) * keep, zm_ref[s, j].astype(F32)], axis=0)
                zc.append(_conv_ext(xe, cw_ref[s, j], taps)[hz:] + cb_ref[s, j])
                zc_ref[s, j] = zc[s].astype(BF16)
            act = (zc[0] * _sigmoid(zc[0]) * zc[1]).astype(BF16)
            act_ref[j] = act
            acc = acc + _dot(act, w_ref[j])
        x_ref[...] = acc
        if next_g is not None:
            rest[-1][...] = _norm_rows(acc, rest[0][...])

    row = pl.BlockSpec((tm, n), lambda i: (i, 0))
    in_specs = [pl.BlockSpec((2, nj, hz, c), lambda i: (0, 0, _prev_halo(tm, hz)(i), 0)),
                pl.BlockSpec((2, nj, tm, c), lambda i: (0, 0, i, 0)),
                _resident(cw.shape), _resident(cb.shape), _resident((nj, c, n)), row]
    out_specs = [pl.BlockSpec((2, nj, tm, c), lambda i: (0, 0, i, 0)), pl.BlockSpec((nj, tm, c), lambda i: (0, i, 0)), row]
    out_shape = [_sds((2, nj, m, c), BF16), _sds((nj, m, c), BF16), _sds((m, n))]
    ins = [z, z, cw, cb, w_down.reshape(nj, c, n), res]
    if next_g is not None:
        in_specs.append(pl.BlockSpec((1, n), lambda i: (0, 0)))
        out_specs.append(row)
        out_shape.append(_sds((m, n), BF16))
        ins.append(next_g)
    return _call(body, name, (m // tm,), in_specs, out_specs, out_shape)(*ins)


def ffn_gate_bwd(z, zc, dx, w_down, cw, name, after=()):
    _, nj, m, c = z.shape
    n = w_down.shape[1]
    taps = cw.shape[2]
    tm = _rows(m, 512)
    n_i = m // tm
    hz = _halo_rows(z.dtype)
    assert _halo_rows(dx.dtype) == hz and zc.dtype == z.dtype, (z.dtype, zc.dtype, dx.dtype)

    def body(zp_ref, zm_ref, zn_ref, cm_ref, cn_ref, dm_ref, dn_ref, wd_ref, cw_ref, dz_ref, dw_ref, db_ref):
        i = pl.program_id(1)
        first = i == 0
        kp, kn = (i > 0).astype(F32), (i < n_i - 1).astype(F32)
        xe = [jnp.concatenate([zp_ref[s].astype(F32) * kp, zm_ref[s].astype(F32), zn_ref[s].astype(F32) * kn], axis=0)
              for s in range(2)]
        zc = [jnp.concatenate([jnp.zeros((hz, c), F32), cm_ref[s].astype(F32), cn_ref[s].astype(F32)], axis=0)
              for s in range(2)]
        dact = _dot(jnp.concatenate([dm_ref[...], dn_ref[...]], axis=0), wd_ref[...], _NT)
        dae = jnp.concatenate([jnp.zeros((hz, c), F32), dact[:tm], dact[tm:] * kn], axis=0)
        sg = _sigmoid(zc[0])
        dzc = [dae * zc[1] * (sg * (1.0 + zc[0] * (1.0 - sg))), dae * (zc[0] * sg)]
        ms = slice(hz, hz + tm)
        for s in range(2):
            w = cw_ref[s]
            ups = [dzc[s]] + [_up(dzc[s], u) for u in range(1, taps)]
            dxs = ups[0] * w[taps - 1:taps]
            for u in range(1, taps):
                dxs = dxs + ups[u] * w[taps - 1 - u:taps - u]
            dz_ref[s] = dxs[ms].astype(BF16)
            tail = dzc[s][hz + tm:]
            x_end = xe[s][tm:]
            for u in range(taps):
                total = _colsum(ups[u] * xe[s]) - _colsum(tail * _down(x_end, u)[hz:])
                _acc(dw_ref.at[s, taps - 1 - u:taps - u, :], total, first)
            _acc(db_ref.at[s], _colsum(dzc[s][ms]), first)

    in_specs = [pl.BlockSpec((2, None, hz, c), lambda j, i: (0, j, _prev_halo(tm, hz)(i), 0)),
                pl.BlockSpec((2, None, tm, c), lambda j, i: (0, j, i, 0)),
                pl.BlockSpec((2, None, hz, c), lambda j, i: (0, j, _next_halo(tm, m, hz)(i), 0)),
                pl.BlockSpec((2, None, tm, c), lambda j, i: (0, j, i, 0)),
                pl.BlockSpec((2, None, hz, c), lambda j, i: (0, j, _next_halo(tm, m, hz)(i), 0)),
                pl.BlockSpec((tm, n), lambda j, i: (i, 0)),
                pl.BlockSpec((hz, n), lambda j, i: (_next_halo(tm, m, hz)(i), 0)),
                pl.BlockSpec((None, c, n), lambda j, i: (j, 0, 0)),
                pl.BlockSpec((2, None, taps, c), lambda j, i: (0, j, 0, 0))]
    out_specs = [pl.BlockSpec((2, None, tm, c), lambda j, i: (0, j, i, 0)),
                 pl.BlockSpec((2, None, taps, c), lambda j, i: (0, j, 0, 0)),
                 pl.BlockSpec((2, None, 1, c), lambda j, i: (0, j, 0, 0))]
    return _call(body, name, (nj, n_i), in_specs, out_specs,
                 [_sds((2, nj, m, c), BF16), _sds((2, nj, taps, c)), _sds((2, nj, 1, c))], (), after)(
        z, z, z, zc, zc, dx, dx, w_down.reshape(nj, c, n), cw)


_STREAM_TILE_BYTES = 2 << 20


def _tile_rows(r, c):
    if r * c * 4 <= _STREAM_TILE_BYTES:
        return r
    fits = [d for d in range(16, r, 16) if r % d == 0 and d * c * 4 <= _STREAM_TILE_BYTES]
    return max(fits) if fits else r


def _as2d(a, lead):
    shape = a.shape
    return a.reshape((lead, -1, shape[-1]) if lead else (-1, shape[-1]))


def add_pairs(g, l1, own, name):
    shape = l1.shape
    g3, l3 = _as2d(g, N_DEV), _as2d(l1, 4)
    _, r, c = l3.shape
    tr = _tile_rows(r, c)

    def body(own_ref, a_ref, b_ref, o_ref):
        o_ref[...] = (a_ref[...].astype(F32) + b_ref[...].astype(F32)).astype(o_ref.dtype)

    spec = pl.BlockSpec((None, tr, c), lambda k, i, own_ref: (k, i, 0))
    grid_spec = pltpu.PrefetchScalarGridSpec(
        num_scalar_prefetch=1, grid=(4, r // tr),
        in_specs=[pl.BlockSpec((None, tr, c), lambda k, i, own_ref: (own_ref[k], i, 0)), spec], out_specs=spec)
    out = pl.pallas_call(
        body, name=name, grid_spec=grid_spec, out_shape=_sds(l3.shape, l1.dtype),
        compiler_params=pltpu.CompilerParams(dimension_semantics=("arbitrary", "arbitrary"),
                                             vmem_limit_bytes=V7X_VMEM_LIMIT_BYTES))(own, g3, l3)
    return out.reshape(shape)


def _grad_sum(p_ref, l_ref):
    return ((p_ref[...].astype(F32) + l_ref[0].astype(F32)) + l_ref[1].astype(F32)) + l_ref[2].astype(F32)


def sum_parts(p, l2, name):
    _, r, c = p.shape

    def body(p_ref, l_ref, o_ref):
        o_ref[...] = _grad_sum(p_ref, l_ref)

    return _call(body, name, (1,), [pl.BlockSpec((None, r, c), lambda i: (0, 0, 0)), pl.BlockSpec((3, r, c), lambda i: (0, 0, 0))],
                 pl.BlockSpec((r, c), lambda i: (0, 0)), _sds((r, c)))(p, l2)


def _adamw_math(w, g, m, v):
    m = ADAM_B1 * m + (1.0 - ADAM_B1) * g
    v = ADAM_B2 * v + (1.0 - ADAM_B2) * (g * g)
    m_hat = m / (1.0 - ADAM_B1 ** ADAM_STEP)
    v_hat = v / (1.0 - ADAM_B2 ** ADAM_STEP)
    delta = -ADAM_LR * (m_hat / (jnp.sqrt(v_hat) + ADAM_EPS) + ADAM_WD * w)
    return delta, m, v


def adamw(w, m, v, name, g=None, p=None, l2=None):
    shape = w.shape
    w2, m2, v2 = (_as2d(t, 0) for t in (w, m, v))
    r, c = w2.shape
    tr = _tile_rows(r, c)
    row = pl.BlockSpec((tr, c), lambda i: (i, 0))
    if g is None:
        p3, l3 = _as2d(p, 4), _as2d(l2, 3)
        gin = (p3, l3)
        gspecs = [pl.BlockSpec((None, tr, c), lambda i: (0, i, 0)), pl.BlockSpec((3, tr, c), lambda i: (0, i, 0))]
    else:
        gin, gspecs = (_as2d(g, 0),), [row]

    def body(*refs):
        n_g = len(gin)
        w_ref, m_ref, v_ref, g_ref, d_ref, nm_ref, nv_ref = refs[n_g:]
        grad = refs[0][...] if n_g == 1 else _grad_sum(refs[0], refs[1])
        delta, nm, nv = _adamw_math(w_ref[...], grad, m_ref[...], v_ref[...])
        g_ref[...] = grad
        d_ref[...] = delta
        nm_ref[...] = nm
        nv_ref[...] = nv

    outs = _call(body, name, (r // tr,), gspecs + [row, row, row], [row] * 4, [_sds((r, c))] * 4)(*gin, w2, m2, v2)
    return tuple(o.reshape(shape) for o in outs)


def adamw_layer(w, m, v, layer, prev, name, p, l2):
    n_l, r, c = w.shape
    tr = _tile_rows(r, c)
    slab = pl.BlockSpec((None, tr, c), lambda i: (layer, i, 0))
    in_specs = [pl.BlockSpec((None, tr, c), lambda i: (0, i, 0)), pl.BlockSpec((3, tr, c), lambda i: (0, i, 0)), slab, slab, slab]
    n_in = len(in_specs)
    prev = () if prev is None else tuple(prev)

    def body(p_ref, l_ref, w_ref, m_ref, v_ref, *rest):
        g_ref, d_ref, nm_ref, nv_ref = rest[len(prev):]
        grad = _grad_sum(p_ref, l_ref)
        delta, nm, nv = _adamw_math(w_ref[...], grad, m_ref[...], v_ref[...])
        g_ref[...] = grad
        d_ref[...] = delta
        nm_ref[...] = nm
        nv_ref[...] = nv

    return pl.pallas_call(
        body, name=name, grid=(r // tr,), in_specs=in_specs + [pl.BlockSpec(memory_space=pl.ANY)] * len(prev),
        out_specs=[slab] * 4, out_shape=[_sds((n_l, r, c))] * 4,
        input_output_aliases={n_in + q: q for q in range(len(prev))},
        compiler_params=pltpu.CompilerParams(dimension_semantics=("arbitrary",), vmem_limit_bytes=V7X_VMEM_LIMIT_BYTES),
    )(_as2d(p, 4), _as2d(l2, 3), w, m, v, *prev)


def _comm_call(body, name, ins, out_shape, n_sems):
    any_spec = pl.BlockSpec(memory_space=pl.ANY)
    return pl.pallas_call(
        body, name=name, in_specs=[any_spec] * len(ins), out_specs=[any_spec] * len(out_shape), out_shape=out_shape,
        scratch_shapes=[pltpu.SemaphoreType.DMA((n,)) for n in n_sems],
        compiler_params=pltpu.CompilerParams(has_side_effects=True))(*ins)


def _place():
    return lax.axis_index("x"), lax.axis_index("y"), lax.axis_index("c")


def _dev_index(px, py, pc):
    return 4 * px + 2 * py + pc


def all_gather(blocks, name):
    n_t = len(blocks)

    def body(*refs):
        ins, outs = refs[:n_t], refs[n_t:2 * n_t]
        send_sems, recv_sems, local_sems = refs[2 * n_t:]
        x, y, c = _place()
        me, sibling = (x, y, c), (x, y, 1 - c)
        chips = [(1 - x, y), (x, 1 - y), (1 - x, 1 - y)]

        def copy(t, k, block, to, src=None):
            dst = outs[t].at[_dev_index(*block)]
            return pltpu.make_async_remote_copy(
                src_ref=dst if src is None else src, dst_ref=dst, send_sem=send_sems.at[t * 7 + k],
                recv_sem=recv_sems.at[t * 7 + k], device_id=to, device_id_type=MESH_ID)

        mine = [pltpu.make_async_copy(ins[t], outs[t].at[_dev_index(*me)], local_sems.at[t]) for t in range(n_t)]
        for cp in mine:
            cp.start()
        first = []
        for t in range(n_t):
            first.append(copy(t, 0, me, sibling, src=ins[t]))
            first += [copy(t, 1 + j, me, (*chip, c), src=ins[t]) for j, chip in enumerate(chips)]
        for cp in first:
            cp.start()
        passed = []
        for t in range(n_t):
            for j, chip in enumerate(chips):
                copy(t, 1 + j, (*chip, c), me).wait_recv()
                cp = copy(t, 4 + j, (*chip, c), sibling)
                cp.start()
                passed.append(cp)
        for t in range(n_t):
            copy(t, 0, sibling, me).wait_recv()
            for j, chip in enumerate(chips):
                copy(t, 4 + j, (*chip, 1 - c), me).wait_recv()
        for cp in first + passed:
            cp.wait_send()
        for cp in mine:
            cp.wait()

    out_shape = [_sds((N_DEV,) + b.shape, b.dtype) for b in blocks]
    return _comm_call(body, name, blocks, out_shape, (7 * n_t, 7 * n_t, n_t))


def _chip_of(x, y, k):
    return (x if k % 2 == 0 else 1 - x), (y if k // 2 == 0 else 1 - y)


_HBM_SPEC = pl.BlockSpec(memory_space=pltpu.HBM)
_SEM_SPEC = pl.BlockSpec(memory_space=pltpu.SEMAPHORE)
_DATAFLOW = pltpu.SideEffectType.DATAFLOW_SIDE_EFFECTING


def _in_hbm(a):
    return pltpu.with_memory_space_constraint(a, pltpu.HBM)


def _split_start(name, issue, srcs, land_shapes, sem_counts, after=()):
    n_buf, n_sem, n_after = len(srcs) + len(land_shapes), len(sem_counts), len(after)

    def body(*refs):
        issue(refs[:len(srcs)], refs[len(srcs):n_buf], refs[n_buf + n_after:n_buf + n_after + n_sem])
        refs[-1][...] = jnp.zeros_like(refs[-1])

    bufs = [pltpu.HBM(s.shape, s.dtype) for s in list(srcs) + list(land_shapes)]
    outs = pl.pallas_call(
        body, name=name, in_specs=(*[_HBM_SPEC] * n_buf, *[pl.BlockSpec(memory_space=pl.ANY)] * n_after),
        out_shape=(*[pltpu.SemaphoreType.DMA((n,)) for n in sem_counts], *bufs, _sds((8, LANES))),
        out_specs=(*[_SEM_SPEC] * n_sem, *[_HBM_SPEC] * n_buf, pl.BlockSpec(memory_space=pltpu.VMEM)),
        input_output_aliases={i: n_sem + i for i in range(n_buf)},
        compiler_params=pltpu.CompilerParams(has_side_effects=_DATAFLOW),
    )(*[_in_hbm(s) for s in srcs], *[_in_hbm(lax.empty(s.shape, s.dtype)) for s in land_shapes], *after)
    return outs[:n_sem], outs[n_sem:n_sem + len(srcs)], outs[n_sem + len(srcs):n_sem + n_buf], outs[-1]


def _split_wait(name, finish, sems, srcs, lands, after):
    n_buf, n_sem = len(srcs) + len(lands), len(sems)

    def body(*refs):
        finish(refs[:len(srcs)], refs[len(srcs):n_buf], refs[n_buf:n_buf + n_sem])

    bufs = [pltpu.HBM(s.shape, s.dtype) for s in list(srcs) + list(lands)]
    outs = pl.pallas_call(
        body, name=name, in_specs=(*[_HBM_SPEC] * n_buf, *[_SEM_SPEC] * n_sem, *[pl.BlockSpec(memory_space=pl.ANY)] * len(after)),
        out_shape=tuple(bufs), out_specs=(_HBM_SPEC,) * n_buf, input_output_aliases={i: i for i in range(n_buf)},
        compiler_params=pltpu.CompilerParams(has_side_effects=_DATAFLOW),
    )(*srcs, *lands, *sems, *after)
    return outs[:len(srcs)], outs[len(srcs):]


def _peer(x, y, c, r):
    return (1 - x if r & 4 else x), (1 - y if r & 2 else y), (1 - c if r & 1 else c)


ALL_PEERS = tuple(range(1, N_DEV))
SAME_CORE_PEERS = (2, 4, 6)


def _gather_copies(src_refs, land_refs, sem_refs, arrivals, peers):
    send_sems, recv_sems, local_sems = sem_refs
    x, y, c = _place()
    me = _dev_index(x, y, c)
    local, sends, recvs = [], [], []
    for j, (src, land) in enumerate(zip(src_refs, land_refs)):
        local.append(pltpu.make_async_copy(src, land.at[me], local_sems.at[j]))
        for p, r in enumerate(peers):
            peer = _peer(x, y, c, r)
            q = len(peers) * j + p
            sends.append(pltpu.make_async_remote_copy(src_ref=src, dst_ref=land.at[me], send_sem=send_sems.at[q],
                                                      recv_sem=recv_sems.at[q], device_id=peer, device_id_type=MESH_ID))
            if arrivals:
                recvs.append(pltpu.make_async_remote_copy(
                    src_ref=src, dst_ref=land.at[_dev_index(*peer)], send_sem=send_sems.at[q], recv_sem=recv_sems.at[q],
                    device_id=peer, device_id_type=MESH_ID))
    return local, sends, recvs


def gather_start(groups, peers, name, after=()):
    flat = [b for g in groups for b in g]
    bounds = [sum(len(g) for g in groups[:i]) for i in range(len(groups) + 1)]

    def issue(src_refs, land_refs, sem_refs):
        for i in range(len(groups)):
            lo, hi = bounds[i], bounds[i + 1]
            local, sends, _ = _gather_copies(src_refs[lo:hi], land_refs[lo:hi], sem_refs[3 * i:3 * i + 3], False, peers[i])
            for cp in local + sends:
                cp.start()

    sem_counts = [n for g, p in zip(groups, peers) for n in (len(p) * len(g), len(p) * len(g), len(g))]
    sems, srcs, lands, _ = _split_start(name, issue, flat, [_sds((N_DEV,) + b.shape, b.dtype) for b in flat], sem_counts, after)
    return [(sems[3 * i:3 * i + 3], srcs[bounds[i]:bounds[i + 1]], lands[bounds[i]:bounds[i + 1]], peers[i])
            for i in range(len(groups))]


def gather_wait(group, after, name):
    sems, srcs, lands, peers = group

    def finish(src_refs, land_refs, sem_refs):
        local, sends, recvs = _gather_copies(src_refs, land_refs, sem_refs, True, peers)
        for cp in local:
            cp.wait()
        for cp in recvs:
            cp.wait_recv()
        for cp in sends:
            cp.wait_send()

    return _split_wait(name, finish, sems, srcs, lands, after)[1]


def _forward_copies(land_refs, sem_refs, arrivals):
    send_sems, recv_sems = sem_refs
    x, y, c = _place()
    sends, recvs = [], []
    for t, land in enumerate(land_refs):
        for k in range(4):
            cx, cy = _chip_of(x, y, k)
            mine, theirs = land.at[_dev_index(cx, cy, c)], land.at[_dev_index(cx, cy, 1 - c)]
            sems = dict(send_sem=send_sems.at[4 * t + k], recv_sem=recv_sems.at[4 * t + k], device_id=(x, y, 1 - c),
                        device_id_type=MESH_ID)
            sends.append(pltpu.make_async_remote_copy(src_ref=mine, dst_ref=mine, **sems))
            if arrivals:
                recvs.append(pltpu.make_async_remote_copy(src_ref=theirs, dst_ref=theirs, **sems))
    return sends, recvs


def forward_start(lands, name):
    def issue(land_refs, _, sem_refs):
        for cp in _forward_copies(land_refs, sem_refs, False)[0]:
            cp.start()

    n = 4 * len(lands)
    sems, lands, _, token = _split_start(name, issue, lands, [], (n, n))
    return sems, lands, token


def forward_wait(started, after, name):
    sems, lands, _ = started

    def finish(land_refs, _, sem_refs):
        sends, recvs = _forward_copies(land_refs, sem_refs, True)
        for cp in recvs:
            cp.wait_recv()
        for cp in sends:
            cp.wait_send()

    return _split_wait(name, finish, sems, lands, [], after)[0]


def _sibling_copies(src_refs, land_refs, sem_refs):
    send_sems, recv_sems = sem_refs
    x, y, c = _place()
    copies = []
    for t, (src, land) in enumerate(zip(src_refs, land_refs)):
        for k in range(4):
            cx, cy = _chip_of(x, y, k)
            copies.append(pltpu.make_async_remote_copy(
                src_ref=src.at[_dev_index(cx, cy, 1 - c)], dst_ref=land.at[k], send_sem=send_sems.at[4 * t + k],
                recv_sem=recv_sems.at[4 * t + k], device_id=(x, y, 1 - c), device_id_type=MESH_ID))
    return copies


def _chip_copies(src_refs, land_refs, sem_refs):
    send_sems, recv_sems = sem_refs
    x, y, c = _place()
    copies = []
    for t, (src, land) in enumerate(zip(src_refs, land_refs)):
        for k in range(1, 4):
            cx, cy = _chip_of(x, y, k)
            copies.append(pltpu.make_async_remote_copy(
                src_ref=src.at[k], dst_ref=land.at[k - 1], send_sem=send_sems.at[3 * t + k - 1],
                recv_sem=recv_sems.at[3 * t + k - 1], device_id=(cx, cy, c), device_id_type=MESH_ID))
    return copies


def _exchange_start(copies_of, n_land, per_array, arrays, name):
    def issue(src_refs, land_refs, sem_refs):
        for cp in copies_of(src_refs, land_refs, sem_refs):
            cp.start()

    n = per_array * len(arrays)
    lands = [_sds((n_land,) + a.shape[1:], a.dtype) for a in arrays]
    return _split_start(name, issue, arrays, lands, (n, n))


def _exchange_wait(copies_of, started, after, name):
    sems, srcs, lands, _ = started

    def finish(src_refs, land_refs, sem_refs):
        copies = copies_of(src_refs, land_refs, sem_refs)
        for cp in copies:
            cp.wait_recv()
        for cp in copies:
            cp.wait_send()

    return _split_wait(name, finish, sems, srcs, lands, after)


def sibling_start(grads, name):
    return _exchange_start(_sibling_copies, 4, 4, grads, name)


def sibling_wait(started, after, name):
    return _exchange_wait(_sibling_copies, started, after, name)


def chips_start(parts, name):
    return _exchange_start(_chip_copies, 3, 3, parts, name)


def chips_wait(started, after, name):
    return _exchange_wait(_chip_copies, started, after, name)


def _pack(arrays, rows):
    flat = jnp.concatenate([a.reshape(-1) for a in arrays])
    return jnp.pad(flat, (0, rows * LANES - flat.shape[0])).reshape(rows, LANES)


def _pack_stacked(arrays, rows):
    flat = jnp.concatenate([a.reshape(N_DEV, -1) for a in arrays], axis=1)
    return jnp.pad(flat, ((0, 0), (0, rows * LANES - flat.shape[1]))).reshape(N_DEV, rows, LANES)


def _unpack(buf, shapes, lead=()):
    flat = buf.reshape(lead + (-1,))
    out, off = [], 0
    for s in shapes:
        n = 1
        for d in s:
            n *= d
        out.append(flat[..., off:off + n].reshape(lead + tuple(s)))
        off += n
    return out


def _padded_rows(shapes, multiple):
    n = sum(functools.reduce(lambda a, b: a * b, s, 1) for s in shapes)
    rows = -(-n // LANES)
    return -(-rows // multiple) * multiple


def _to_full(stacked, axis):
    t = jnp.moveaxis(stacked, 0, axis)
    return t.reshape(t.shape[:axis] + (t.shape[axis] * t.shape[axis + 1],) + t.shape[axis + 2:])


def _to_stacked(full, axis):
    s = full.shape
    t = full.reshape(s[:axis] + (N_DEV, s[axis] // N_DEV) + s[axis + 1:])
    return jnp.moveaxis(t, axis, 0)


def _ffn_forward(x, h, w_up, cw, cb, w_down, next_g, tag):
    z = mm_in(h, w_up, f"ffn{tag}_up", stacked_out=True, out_dtype=BF16, rows=2048)
    nb, m, c = z.shape
    z4 = z.reshape(2, nb // 2, m, c)
    zc, act, *out = ffn_gate_down(z4, cw, cb, w_down, x, next_g, f"ffn{tag}_gate_down")
    return (out[0], out[1] if next_g is not None else None), (x, h, z4, zc, act)


def _ffn_backward(dx, dxb, saved, norm_g, w_up, cw, w_down, tag, after=()):
    x, h, z4, zc, act = saved
    nj = act.shape[0]
    dz4, dcw, dcb = ffn_gate_bwd(z4, zc, dxb, w_down, cw, f"ffn{tag}_gate_bwd", after)
    dw_down = mm_dw_out(act, dxb, f"ffn{tag}_down_dw")
    dz = dz4.reshape((2 * nj,) + dz4.shape[2:])
    dx, dxb, dg = mm_dx_in(dz, w_up, f"ffn{tag}_up_dx", norm=(x, norm_g, dx))
    dw_up = mm_dw_in(h, dz, 2 * nj, f"ffn{tag}_up_dw", transposed=True)
    return dx, dxb, dict(norm_g=dg, w_up=dw_up, conv_w=dcw, conv_b=dcb, w_down=dw_down)


def kernel(x, a_norm_g, a_w_in, a_b_in, a_v_norm_g, a_w_s, a_b_s, a_w_out, b_norm_g, b_w_in, b_w_grp, b_b_grp, b_scale, b_w_out, c_norm_g, c_w_in, c_b_in, c_conv_w, c_conv_b, c_w_a, c_b_a, c_w_i, c_b_i, c_lambda, c_w_out, d_norm_g, d_w_in, d_conv_w, d_w_out, ffn_norm_g, ffn_w_up, ffn_conv_w, ffn_conv_b, ffn_w_down, final_norm_g, loss_target, m_a_norm_g, m_a_w_in, m_a_b_in, m_a_v_norm_g, m_a_w_s, m_a_b_s, m_a_w_out, m_b_norm_g, m_b_w_in, m_b_w_grp, m_b_b_grp, m_b_scale, m_b_w_out, m_c_norm_g, m_c_w_in, m_c_b_in, m_c_conv_w, m_c_conv_b, m_c_w_a, m_c_b_a, m_c_w_i, m_c_b_i, m_c_lambda, m_c_w_out, m_d_norm_g, m_d_w_in, m_d_conv_w, m_d_w_out, m_ffn_norm_g, m_ffn_w_up, m_ffn_conv_w, m_ffn_conv_b, m_ffn_w_down, m_final_norm_g, v_a_norm_g, v_a_w_in, v_a_b_in, v_a_v_norm_g, v_a_w_s, v_a_b_s, v_a_w_out, v_b_norm_g, v_b_w_in, v_b_w_grp, v_b_b_grp, v_b_scale, v_b_w_out, v_c_norm_g, v_c_w_in, v_c_b_in, v_c_conv_w, v_c_conv_b, v_c_w_a, v_c_b_a, v_c_w_i, v_c_b_i, v_c_lambda, v_c_w_out, v_d_norm_g, v_d_w_in, v_d_conv_w, v_d_w_out, v_ffn_norm_g, v_ffn_w_up, v_ffn_conv_w, v_ffn_conv_b, v_ffn_w_down, v_final_norm_g):
    args = locals()
    w_loc = {n: args[n] for n in WEIGHTS}
    m_loc = {n: args["m_" + n] for n in WEIGHTS}
    v_loc = {n: args["v_" + n] for n in WEIGHTS}
    depth = ffn_w_up.shape[0]
    xs = x[0]
    target = loss_target[0]

    small_names = list(SMALL_SHARDED)
    small_shapes = [w_loc[n].shape for n in small_names]
    small_rows = _padded_rows(small_shapes, 8)
    small_packed = _pack([w_loc[n] for n in small_names], small_rows)
    early_names = ['ffn_conv_w', 'b_norm_g', 'c_norm_g', 'd_norm_g']
    late_names = [n for n in small_names if n not in early_names]
    packs = []
    for names in (early_names, late_names):
        shapes = [w_loc[n].shape for n in names]
        packs.append((names, shapes, _pack([w_loc[n] for n in names], _padded_rows(shapes, 8))))
    wa_in, wa_out = all_gather([a_w_in[0].astype(BF16), a_w_out[0].astype(BF16)], "gather_a")
    mixers = [None, (b_w_in, b_w_out), (c_w_in, c_w_out), (d_w_in, d_w_out)]
    groups = {}
    for l in range(depth):
        if l > 0:
            groups[f'mixer{l}'] = [mixers[l][0][0].astype(BF16), mixers[l][1][0].astype(BF16)] + ([packs[1][2]] if l == 1 else [])
        groups[f'ffn{l}'] = [ffn_w_up[l].astype(BF16), ffn_w_down[l].astype(BF16)] + ([packs[0][2]] if l == 0 else [])
    two_level = ('ffn0', 'mixer1', 'ffn1')
    in_flight = dict(zip(groups, gather_start(list(groups.values()),
                                              [SAME_CORE_PEERS if k in two_level else ALL_PEERS for k in groups],
                                              "gather_start", (wa_in, wa_out))))
    forwarding = {}

    def forward_on(key, after):
        forwarding[key] = forward_start(gather_wait(in_flight[key], after, f"gather_wait_{key}"), f"forward_start_{key}")
        return (forwarding[key][2],)

    def gathered(key, after):
        if key in two_level:
            return forward_wait(forwarding[key], after, f"forward_wait_{key}")
        return gather_wait(in_flight[key], after, f"gather_wait_{key}")

    sm = {}

    def unpack_small(pack, gathered):
        names, shapes, _ = pack
        sm.update((n, _to_full(s, SMALL_SHARDED[n])) for n, s in zip(names, _unpack(gathered, shapes, (N_DEV,))))

    def rows_full(st):
        return st.reshape((st.shape[0] * st.shape[1],) + st.shape[2:])

    nb = N_DEV
    ffn_cb = [ffn_conv_b[l].reshape(2, nb // 2, 1, -1) for l in range(depth)]
    ffn_g = [ffn_norm_g[l:l + 1] for l in range(depth)]
    a_bst = a_b_s[0].T
    w_up, w_down, saved = [None] * depth, [None] * depth, {}

    def ffn_forward(xl, hl, l, next_g):
        up, down, *early = gathered(f'ffn{l}', (xl,))
        if early:
            unpack_small(packs[0], early[0])
            ffn_cw.extend(sm['ffn_conv_w'][k].reshape(ffn_conv_w.shape[1], 2, nb // 2, -1).transpose(1, 2, 0, 3)
                          for k in range(depth))
        w_up[l], w_down[l] = up, rows_full(down)
        return _ffn_forward(xl, hl, w_up[l], ffn_cw[l], ffn_cb[l], w_down[l], next_g(), l)

    ffn_cw = []
    wa_out = rows_full(wa_out)
    h = rms_fwd(xs, a_norm_g, "a_norm")
    z = mm_in(h, wa_in, "a_in", bias=a_b_in)
    y = a_mid_fwd(z, a_v_norm_g, a_w_s[0], a_bst, "a_mid")
    x1, hf = mm_out(y, wa_out, xs, ffn_g[0], "a_out", forward_on('ffn0', (y,)))
    saved['a'] = (xs, h, z, y)
    forward_on('mixer1', (x1,))
    (x1, h), saved['f0'] = ffn_forward(x1, hf, 0, lambda: sm['b_norm_g'])

    forward_on('ffn1', (x1,))
    wb_in, wb_out, late = gathered('mixer1', (x1,))
    unpack_small(packs[1], late)
    b_wgrp, c_wa, c_wi = (sm[n][0].astype(BF16) for n in ('b_w_grp', 'c_w_a', 'c_w_i'))
    b_bgrp, c_ba, c_bi = (sm[n][0].reshape(1, -1) for n in ('b_b_grp', 'c_b_a', 'c_b_i'))
    wb_in, wb_out = rows_full(wb_in)[None], rows_full(wb_out)
    z = mm_in(h, wb_in, "b_in")
    y, x2, hf = b_mid_fwd(z, b_wgrp, b_bgrp, sm['b_scale'], wb_out, x1, ffn_g[1], "b_mid_out")
    saved['b'] = (x1, h, z, y)
    (x2, h), saved['f1'] = ffn_forward(x2, hf, 1, lambda: sm['c_norm_g'])

    wc_in, wc_out = gathered('mixer2', (x2,))
    wc_out = rows_full(wc_out)
    z = mm_in(h, wc_in, "c_in", bias=sm['c_b_in'])
    c_cw = sm['c_conv_w'][0]
    a_seq, b_seq, xr = c_mid_fwd(z, c_cw, sm['c_conv_b'], c_wa, c_ba, c_wi, c_bi, sm['c_lambda'], "c_mid")
    hs, y, x3, hf = c_scan_fwd(a_seq, b_seq, z, wc_out, x2, ffn_g[2], "c_scan_out")
    saved['c'] = (x2, h, z, y, a_seq, xr, hs)
    (x3, h), saved['f2'] = ffn_forward(x3, hf, 2, lambda: sm['d_norm_g'])

    wd_in, wd_out = gathered('mixer3', (x3,))
    wd_out = rows_full(wd_out)
    z = mm_in(h, wd_in, "d_in")
    d_cw = sm['d_conv_w'][0]
    y, x4, hf = d_mid_fwd(z, d_cw, wd_out, x3, ffn_g[3], "d_mid_out")
    saved['d'] = (x3, h, z, y)
    (x4, _), saved['f3'] = ffn_forward(x4, hf, 3, lambda: None)

    loss_part, dx, dxb, d_final_g = final_loss(x4, final_norm_g.reshape(1, -1), target, "final_loss")
    loss = lax.psum(loss_part[0, 0], ("x", "y", "c"))

    def rows_stacked(full):
        return full.reshape((N_DEV, full.shape[0] // N_DEV) + full.shape[1:])

    mx, my, mc = _place()
    own = jnp.stack([_dev_index(*_chip_of(mx, my, k), mc) for k in range(4)]).astype(jnp.int32)
    repl_shapes = [w_loc[n].shape for n in REPLICATED]
    repl_rows = _padded_rows(repl_shapes, 8 * N_DEV)
    shard_of = {n: (w_loc[n][0], m_loc[n][0], v_loc[n][0])
                for n in ('a_w_in', 'a_w_out', 'b_w_in', 'b_w_out', 'c_w_in', 'c_w_out', 'd_w_in', 'd_w_out')}
    shard_of['small'] = (small_packed, _pack([m_loc[n] for n in small_names], small_rows),
                         _pack([v_loc[n] for n in small_names], small_rows))
    updated = {}

    def finish(n, part, others):
        if n == 'repl':
            chunk = sum_parts(part, others, "rs_sum_repl")
            repl_g = all_gather([chunk], "gather_repl")[0].reshape(repl_rows, LANES)
            updated[n] = adamw(*(_pack([src[k] for k in REPLICATED], repl_rows) for src in (w_loc, m_loc, v_loc)),
                               "adamw_repl", g=repl_g)
        elif n.startswith('ffn_w'):
            base, l = n[:-1], int(n[-1])
            turn = (lambda t: jnp.swapaxes(t, 1, 2)) if base == 'ffn_w_up' else (lambda t: t)
            updated[base] = adamw_layer(turn(w_loc[base]), turn(m_loc[base]), turn(v_loc[base]), l, updated.get(base),
                                        f"adamw_{n}", part, others)
            return updated[base][1]
        else:
            updated[n] = adamw(*shard_of[n], f"adamw_{n}", p=part, l2=others)
        return updated[n][1]

    def light(arrays):
        return tuple(a for a in arrays if a.size <= 8 * LANES)

    stages = [None, []]

    def advance(tag, new, after, finish_older=True):
        behind = []
        first = None
        if new:
            first = ([n for n, _ in new], sibling_start([g for _, g in new], f"rs_sibling_start{tag}"))
            behind.append(first[1][3])
        older = stages[1]
        if finish_older:
            for k, (names, started) in enumerate(older):
                parts, others = chips_wait(started, after, f"rs_chips_wait{tag}_{k}")
                behind += [finish(n, p, o) for n, p, o in zip(names, parts, others)]
            older = []
        if stages[0] is not None:
            names, started = stages[0]
            grads, got = sibling_wait(started, after, f"rs_sibling_wait{tag}")
            parts = [add_pairs(g, l, own, f"rs_add_{n}") for n, g, l in zip(names, grads, got)]
            older = older + [(names, chips_start(parts, f"rs_chips_start{tag}"))]
            behind.append(older[-1][1][3])
        stages[:] = [first, older]
        return tuple(behind)

    gf = [None] * depth
    dx, dxb, gf[3] = _ffn_backward(dx, dxb, saved['f3'], ffn_g[3], w_up[3], ffn_cw[3], w_down[3], 3)
    xin, h, z, y = saved['d']
    dy = mm_dx_out(dxb, wd_out, "d_out_dx")
    g_d_w_out = mm_dw_out(y, dxb, "d_out_dw")
    dz, g_d_conv_w = d_mid_bwd(z, dy, d_cw, "d_mid_bwd")
    dx, dxb, g_d_norm_g = mm_dx_in(dz, wd_in, "d_in_dx", norm=(xin, sm['d_norm_g'], dx))
    g_d_w_in = mm_dw_in(h, dz, nb, "d_in_dw")
    behind = advance(0, [('ffn_w_up3', gf[3]['w_up']), ('ffn_w_down3', rows_stacked(gf[3]['w_down'])), ('d_w_in', g_d_w_in),
                         ('d_w_out', rows_stacked(g_d_w_out))], (g_d_norm_g,))

    dx, dxb, gf[2] = _ffn_backward(dx, dxb, saved['f2'], ffn_g[2], w_up[2], ffn_cw[2], w_down[2], 2, behind)
    xin, h, z, y, a_seq, xr, hs = saved['c']
    g_c_w_out = mm_dw_out(y, dxb, "c_out_dw")
    lam_seq, da_seq, dgate, dgate_sum = c_scan_bwd(dxb, wc_out, z, hs, a_seq, "c_scan_bwd")
    dxr, g_c_w_a, g_c_w_i, g_c_b_a, g_c_b_i, g_c_lambda = c_mid_bwd(
        lam_seq, da_seq, xr, c_wa, c_ba, c_wi, c_bi, sm['c_lambda'], "c_mid_bwd")
    dxr_pre, g_c_conv_w, g_c_conv_b, dxr_pre_sum = conv_bwd(dxr, z, 1, c_cw, "c_conv_bwd")
    dz = jnp.concatenate([dgate, dxr_pre], axis=1)
    g_c_b_in = jnp.concatenate([dgate_sum, dxr_pre_sum], axis=1)
    dx, dxb, g_c_norm_g = mm_dx_in(dz, wc_in, "c_in_dx", norm=(xin, sm['c_norm_g'], dx))
    g_c_w_in = mm_dw_in(h, dz, nb, "c_in_dw")
    behind = advance(1, [('ffn_w_up2', gf[2]['w_up']), ('ffn_w_down2', rows_stacked(gf[2]['w_down'])), ('c_w_in', g_c_w_in),
                         ('c_w_out', rows_stacked(g_c_w_out))], (g_c_norm_g,))

    dx, dxb, gf[1] = _ffn_backward(dx, dxb, saved['f1'], ffn_g[1], w_up[1], ffn_cw[1], w_down[1], 1, behind)
    xin, h, z, y = saved['b']
    g_b_w_out = mm_dw_out(y, dxb, "b_out_dw")
    dp, g_b_w_grp, g_b_b_grp, g_b_scale = b_mid_bwd(z, dxb, wb_out, b_wgrp, b_bgrp, sm['b_scale'], "b_mid_bwd")
    dz = b_pool_bwd(dp, "b_pool_bwd")
    dx, dxb, g_b_norm_g = mm_dx_in(dz, wb_in, "b_in_dx", norm=(xin, sm['b_norm_g'], dx))
    g_b_w_in = mm_dw_in(h, dz, 1, "b_in_dw")
    behind = advance(2, [('ffn_w_up1', gf[1]['w_up']), ('ffn_w_down1', rows_stacked(gf[1]['w_down'])),
                         ('b_w_in', rows_stacked(g_b_w_in[0])), ('b_w_out', rows_stacked(g_b_w_out))], (g_b_norm_g,))

    dx, dxb, gf[0] = _ffn_backward(dx, dxb, saved['f0'], ffn_g[0], w_up[0], ffn_cw[0], w_down[0], 0, behind)
    full_small = {
        'b_norm_g': g_b_norm_g, 'b_w_grp': g_b_w_grp[None], 'b_b_grp': g_b_b_grp.reshape(b_b_grp.shape[:2] + (-1,)),
        'b_scale': g_b_scale, 'c_norm_g': g_c_norm_g, 'c_b_in': g_c_b_in, 'c_conv_w': g_c_conv_w[None],
        'c_conv_b': g_c_conv_b, 'c_w_a': g_c_w_a[None], 'c_b_a': g_c_b_a.reshape(c_b_a.shape[:2] + (-1,)),
        'c_w_i': g_c_w_i[None], 'c_b_i': g_c_b_i.reshape(c_b_i.shape[:2] + (-1,)), 'c_lambda': g_c_lambda,
        'd_norm_g': g_d_norm_g, 'd_conv_w': g_d_conv_w[None],
        'ffn_conv_w': jnp.stack([gf[l]['conv_w'].transpose(2, 0, 1, 3).reshape(ffn_conv_w.shape[1], -1) for l in range(depth)])}
    small_grads = _pack_stacked([_to_stacked(full_small[n], SMALL_SHARDED[n]) for n in small_names], small_rows)
    behind = advance(3, [('ffn_w_up0', gf[0]['w_up']), ('ffn_w_down0', rows_stacked(gf[0]['w_down'])), ('small', small_grads)],
                     (gf[0]['norm_g'],))

    xin, h, z, y = saved['a']
    g_a_w_out = mm_dw_out(y, dxb, "a_out_dw")
    behind += advance(4, [('a_w_out', rows_stacked(g_a_w_out))], light(behind))
    dz, g_a_b_in, g_a_v_norm_g, g_a_w_s, g_a_b_s = a_mid_bwd(z, dxb, wa_out, a_v_norm_g, a_w_s[0], a_bst, "a_mid_bwd", behind)
    g_a_w_in = mm_dw_in(h, dz, nb, "a_in_dw")
    behind = advance(5, [('a_w_in', g_a_w_in)], (g_a_b_in,))
    behind = advance('5b', [], (g_a_b_in, *light(behind)), finish_older=False)
    dx, _, g_a_norm_g = mm_dx_in(dz, wa_in, "a_in_dx", behind, norm=(xin, a_norm_g, dx))
    grad_x = dx[None]

    tril = jnp.tril(jnp.ones((A_CHUNK, A_CHUNK), bool))
    repl_full = {
        'a_norm_g': g_a_norm_g, 'a_b_in': g_a_b_in, 'a_v_norm_g': g_a_v_norm_g,
        'a_w_s': jnp.where(tril, g_a_w_s, 0.0)[None], 'a_b_s': g_a_b_s[:, ::LANES].T[None],
        'ffn_norm_g': jnp.concatenate([gf[l]['norm_g'] for l in range(depth)], axis=0),
        'ffn_conv_b': jnp.stack([gf[l]['conv_b'].reshape(-1) for l in range(depth)]), 'final_norm_g': d_final_g.reshape(-1)}
    repl_grads = _pack([repl_full[n] for n in REPLICATED], repl_rows).reshape(N_DEV, repl_rows // N_DEV, LANES)
    behind = advance(6, [('repl', repl_grads)], (g_a_norm_g,))
    behind = advance(7, [], (g_a_norm_g, *light(behind)))
    advance(8, [], (g_a_norm_g, *light(behind)))

    outs = [{}, {}, {}, {}]
    for i, dst in enumerate(outs):
        for n in ('a_w_in', 'a_w_out', 'b_w_in', 'b_w_out', 'c_w_in', 'c_w_out', 'd_w_in', 'd_w_out'):
            dst[n] = updated[n][i][None]
        dst['ffn_w_up'] = jnp.swapaxes(updated['ffn_w_up'][i], 1, 2)
        dst['ffn_w_down'] = updated['ffn_w_down'][i]
        dst.update(zip(small_names, _unpack(updated['small'][i], small_shapes)))
        dst.update(zip(REPLICATED, _unpack(updated['repl'][i], repl_shapes)))
    out_g, out_d, out_m, out_v = outs

    return (loss, grad_x, *[out_g[n] for n in WEIGHTS], *[out_d[n] for n in WEIGHTS], *[out_m[n] for n in WEIGHTS],
            *[out_v[n] for n in WEIGHTS])
```

```python
import functools

import jax
import jax.numpy as jnp
from jax import lax
from jax.experimental import pallas as pl
from jax.experimental.pallas import tpu as pltpu

F32, BF16 = jnp.float32, jnp.bfloat16
MESH_ID = pl.DeviceIdType.MESH
N_DEV = 8
V7X_VMEM_LIMIT_BYTES = 56 << 20
LANES = 128
HALO = 8
POOL_HALO = 16

EPS = 1e-6
A_CHUNK, A_GROUPS = 128, 4
B_WINDOWS = (2, 4, 8, 16)
C_GATE_C = 8.0
ADAM_LR, ADAM_B1, ADAM_B2, ADAM_EPS, ADAM_WD, ADAM_STEP = 0.001, 0.9, 0.999, 1e-08, 0.01, 10

WEIGHTS = ['a_norm_g', 'a_w_in', 'a_b_in', 'a_v_norm_g', 'a_w_s', 'a_b_s', 'a_w_out', 'b_norm_g', 'b_w_in', 'b_w_grp',
           'b_b_grp', 'b_scale', 'b_w_out', 'c_norm_g', 'c_w_in', 'c_b_in', 'c_conv_w', 'c_conv_b', 'c_w_a', 'c_b_a',
           'c_w_i', 'c_b_i', 'c_lambda', 'c_w_out', 'd_norm_g', 'd_w_in', 'd_conv_w', 'd_w_out', 'ffn_norm_g',
           'ffn_w_up', 'ffn_conv_w', 'ffn_conv_b', 'ffn_w_down', 'final_norm_g']
SMALL_SHARDED = {'b_norm_g': 1, 'b_w_grp': 2, 'b_b_grp': 2, 'b_scale': 1, 'c_norm_g': 1, 'c_b_in': 1, 'c_conv_w': 2,
                 'c_conv_b': 1, 'c_w_a': 2, 'c_b_a': 2, 'c_w_i': 2, 'c_b_i': 2, 'c_lambda': 1, 'd_norm_g': 1,
                 'd_conv_w': 2, 'ffn_conv_w': 2}
REPLICATED = ['a_norm_g', 'a_b_in', 'a_v_norm_g', 'a_w_s', 'a_b_s', 'ffn_norm_g', 'ffn_conv_b', 'final_norm_g']


_GELU_C0, _GELU_C1 = 0.7978845608028654, 0.044715


def _gelu(x):
    return 0.5 * x * (1.0 + jnp.tanh(_GELU_C0 * (x + _GELU_C1 * (x * x * x))))


def _gelu_grad(x):
    t = jnp.tanh(_GELU_C0 * (x + _GELU_C1 * (x * x * x)))
    return 0.5 * (1.0 + t) + 0.5 * x * (1.0 - t * t) * (_GELU_C0 * (1.0 + 3.0 * _GELU_C1 * (x * x)))


def _sigmoid(x):
    return jax.nn.sigmoid(x)


def _log1p(x):
    u = 1.0 + x
    return jnp.where(u == 1.0, x, jnp.log(u) * (x / (u - 1.0)))


def _softplus(x):
    return jnp.maximum(x, 0.0) + _log1p(jnp.exp(-jnp.abs(x)))


def _expm1(x):
    poly = x * (1.0 + x * (1 / 2) * (1.0 + x * (1 / 3) * (1.0 + x * (1 / 4) * (1.0 + x * (1 / 5) * (
        1.0 + x * (1 / 6) * (1.0 + x * (1 / 7) * (1.0 + x * (1 / 8))))))))
    return jnp.where(jnp.abs(x) < 0.35, poly, jnp.exp(x) - 1.0)


def _down(xe, s):
    return xe if s == 0 else pltpu.roll(xe, s, 0)


def _up(xe, s):
    return xe if s == 0 else pltpu.roll(xe, xe.shape[0] - s, 0)


def _conv_ext(xe, w, taps):
    y = xe * w[taps - 1:taps]
    for s in range(1, taps):
        y = y + _down(xe, s) * w[taps - 1 - s:taps - s]
    return y


def _acc(ref, val, first):
    @pl.when(first)
    def _():
        ref[...] = val

    @pl.when(jnp.logical_not(first))
    def _():
        ref[...] += val


def _colsum(v):
    return jnp.sum(v, axis=0, keepdims=True)


def _dot(a, b, dims=((1,), (0,))):
    return lax.dot_general(a.astype(BF16), b.astype(BF16), (dims, ((), ())), preferred_element_type=F32)


_NN, _NT, _TN = ((1,), (0,)), ((1,), (1,)), ((0,), (0,))


def _call(body, name, grid, in_specs, out_specs, out_shape, scratch=(), after=()):
    n_in, n_after = len(in_specs), len(after)

    def ordered_body(*refs):
        return body(*refs[:n_in], *refs[n_in + n_after:])

    call = pl.pallas_call(
        ordered_body if n_after else body, name=name, grid=grid,
        in_specs=list(in_specs) + [pl.BlockSpec(memory_space=pl.ANY)] * n_after, out_specs=out_specs,
        out_shape=out_shape, scratch_shapes=list(scratch),
        compiler_params=pltpu.CompilerParams(dimension_semantics=("arbitrary",) * len(grid),
                                             vmem_limit_bytes=V7X_VMEM_LIMIT_BYTES))
    return lambda *args: call(*args, *after)


def _rows(m, t):
    t = min(m, t)
    assert m % t == 0, (m, t)
    return t


def _sds(shape, dtype=F32):
    return jax.ShapeDtypeStruct(tuple(shape), dtype)


def _prev_halo(tm, halo=HALO):
    return lambda i: jnp.maximum(i * (tm // halo) - 1, 0)


def _next_halo(tm, m, halo=HALO):
    return lambda i: jnp.minimum((i + 1) * (tm // halo), m // halo - 1)


def _matmul(name, ins, in_specs, out_shape, o_spec, grid, compute, after=()):
    def body(*refs):
        refs[-1][...] = compute(*refs[:-1]).astype(refs[-1].dtype)

    return _call(body, name, grid, in_specs, o_spec, out_shape, (), after)(*ins)


def mm_in(h, w_st, name, bias=None, stacked_out=False, out_dtype=F32, rows=1024):
    m, k = h.shape
    nb, _, n = w_st.shape
    tm = _rows(m, rows)
    in_specs = [pl.BlockSpec((tm, k), lambda i, j: (i, 0)), pl.BlockSpec((None, k, n), lambda i, j: (j, 0, 0))]
    if stacked_out:
        out, o_spec = _sds((nb, m, n), out_dtype), pl.BlockSpec((None, tm, n), lambda i, j: (j, i, 0))
    else:
        out, o_spec = _sds((m, nb * n), out_dtype), pl.BlockSpec((tm, n), lambda i, j: (i, j))
    if bias is None:
        return _matmul(name, (h, w_st), in_specs, out, o_spec, (m // tm, nb), lambda a, b: _dot(a[...], b[...]))
    in_specs.append(pl.BlockSpec((1, n), lambda i, j: (0, j)))
    return _matmul(name, (h, w_st, bias), in_specs, out, o_spec, (m // tm, nb),
                   lambda a, b, c: _dot(a[...], b[...]) + c[...])


def _split_rows(kf):
    g = max(1, kf // 1024)
    return g, kf // g


def _resident(shape):
    return pl.BlockSpec(shape, lambda *_: (0,) * len(shape), pipeline_mode=pl.Buffered(1))


def _norm_rows(xv, g):
    return (xv * lax.rsqrt(jnp.mean(xv * xv, axis=-1, keepdims=True) + EPS) * g).astype(BF16)


def _project_out(y_ref, w_ref, res_ref, g_ref, x_ref, h_ref):
    xv = res_ref[...] + _dot(y_ref[...], w_ref[...])
    x_ref[...] = xv
    h_ref[...] = _norm_rows(xv, g_ref[...])


def _call_projected(body, name, m, tm, in_specs, ins, own_outs, w_out, res, next_g, scratch=()):
    n = w_out.shape[1]
    row = lambda width: pl.BlockSpec((tm, width), lambda i: (i, 0))
    specs = list(in_specs) + [_resident(w_out.shape), row(n), pl.BlockSpec((1, n), lambda i: (0, 0))]
    out_specs = [row(width) for width, _ in own_outs] + [row(n), row(n)]
    out_shape = [_sds((m, width), dtype) for width, dtype in own_outs] + [_sds((m, n)), _sds((m, n), BF16)]
    return _call(body, name, (m // tm,), specs, out_specs, out_shape, scratch)(*ins, w_out, res, next_g)


def mm_out(y, w, res, next_g, name, after=()):
    kf, n = w.shape
    m = y.shape[0]
    tm = _rows(m, 512)
    row = pl.BlockSpec((tm, n), lambda i: (i, 0))

    def body(y_ref, w_ref, res_ref, g_ref, x_ref, h_ref):
        xv = res_ref[...] + _dot(y_ref[...], w_ref[...])
        x_ref[...] = xv
        h_ref[...] = _norm_rows(xv, g_ref[...])

    in_specs = [pl.BlockSpec((tm, kf), lambda i: (i, 0)), _resident((kf, n)), row, pl.BlockSpec((1, n), lambda i: (0, 0))]
    return _call(body, name, (m // tm,), in_specs, [row, row], [_sds((m, n)), _sds((m, n), BF16)], (), after)(y, w, res, next_g)


def mm_dx_in(dz, w_st, name, after=(), norm=None):
    nb, k, n = w_st.shape
    m = dz.shape[-2]
    tm = _rows(m, 512)
    w_spec = _resident((nb, k, n))
    if dz.ndim == 3:
        in_specs = [pl.BlockSpec((None, tm, n), lambda i, r=r: (r, i, 0)) for r in range(nb)] + [w_spec]

        def compute(*refs):
            acc = _dot(refs[0][...], refs[nb][0], _NT)
            for r in range(1, nb):
                acc = acc + _dot(refs[r][...], refs[nb][r], _NT)
            return acc

        ins = (*[dz] * nb, w_st)
    else:
        in_specs = [pl.BlockSpec((tm, nb * n), lambda i: (i, 0)), w_spec]

        def compute(dz_ref, w_ref):
            acc = _dot(dz_ref[:, :n], w_ref[0], _NT)
            for r in range(1, nb):
                acc = acc + _dot(dz_ref[:, r * n:(r + 1) * n], w_ref[r], _NT)
            return acc

        ins = (dz, w_st)
    row = pl.BlockSpec((tm, k), lambda i: (i, 0))
    if norm is None:
        return _matmul(name, ins, in_specs, _sds((m, k)), row, (m // tm,), compute, after)
    n_in = len(in_specs)
    vec = pl.BlockSpec((1, k), lambda i: (0, 0))

    def body(*refs):
        x_ref, g_ref, dr_ref, dx_ref, dxb_ref, dg_ref = refs[n_in:]
        dx, dg = _rms_bwd_math(x_ref[...], g_ref[...], compute(*refs[:n_in]))
        dx = dr_ref[...] + dx
        dx_ref[...] = dx
        dxb_ref[...] = dx.astype(BF16)
        _acc(dg_ref, dg, pl.program_id(0) == 0)

    return _call(body, name, (m // tm,), in_specs + [row, vec, row], [row, row, vec],
                 [_sds((m, k)), _sds((m, k), BF16), _sds((1, k))], (), after)(*ins, *norm)


def mm_dx_out(dout, w, name, groups=None, after=()):
    kf, n = w.shape
    m = dout.shape[0]
    tm = _rows(m, 1024)
    g, k = (groups, kf // groups) if groups else _split_rows(kf)
    in_specs = [pl.BlockSpec((tm, n), lambda i, j: (i, 0)), pl.BlockSpec((None, k, n), lambda i, j: (j, 0, 0))]
    if groups:
        out, o_spec = _sds((g, m, k)), pl.BlockSpec((None, tm, k), lambda i, j: (j, i, 0))
    else:
        out, o_spec = _sds((m, kf)), pl.BlockSpec((tm, k), lambda i, j: (i, j))
    return _matmul(name, (dout, w.reshape(g, k, n)), in_specs, out, o_spec, (m // tm, g),
                   lambda a, b: _dot(a[...], b[...], _NT), after)


def mm_dw_in(h, dz, nb, name, transposed=False):
    m, k = h.shape
    if dz.ndim == 3:
        n = dz.shape[2]
        dz_spec = pl.BlockSpec((None, m, n), lambda j: (j, 0, 0))
    else:
        n = dz.shape[1] // nb
        dz_spec = pl.BlockSpec((m, n), lambda j: (0, j))
    in_specs = [_resident((m, k)), dz_spec]
    if transposed:
        return _matmul(name, (h, dz), in_specs, _sds((nb, n, k), BF16), pl.BlockSpec((None, n, k), lambda j: (j, 0, 0)), (nb,),
                       lambda a, b: _dot(b[...], a[...], _TN))
    return _matmul(name, (h, dz), in_specs, _sds((nb, k, n), BF16), pl.BlockSpec((None, k, n), lambda j: (j, 0, 0)), (nb,),
                   lambda a, b: _dot(a[...], b[...], _TN))


def mm_dw_out(y, dout, name):
    m, n = dout.shape
    if y.ndim == 3:
        g, _, k = y.shape
        y_spec = pl.BlockSpec((None, m, k), lambda j: (j, 0, 0))
    else:
        g, k = _split_rows(y.shape[1])
        y_spec = pl.BlockSpec((m, k), lambda j: (0, j))
    in_specs = [y_spec, _resident((m, n))]
    out = _matmul(name, (y, dout), in_specs, _sds((g, k, n), BF16), pl.BlockSpec((None, k, n), lambda j: (j, 0, 0)), (g,),
                  lambda a, b: _dot(a[...], b[...], _TN))
    return out.reshape(g * k, n)


def rms_fwd(x, g, name):
    m, d = x.shape
    tm = _rows(m, 512)

    def body(x_ref, g_ref, o_ref):
        xv = x_ref[...]
        rstd = lax.rsqrt(jnp.mean(xv * xv, axis=-1, keepdims=True) + EPS)
        o_ref[...] = (xv * rstd * g_ref[...]).astype(BF16)

    row = pl.BlockSpec((tm, d), lambda i: (i, 0))
    vec = pl.BlockSpec((1, d), lambda i: (0, 0))
    return _call(body, name, (m // tm,), [row, vec], row, _sds((m, d), BF16))(x, g)


def _rms_bwd_math(xv, g, dh):
    rstd = lax.rsqrt(jnp.mean(xv * xv, axis=-1, keepdims=True) + EPS)
    xhat = xv * rstd
    dxhat = dh * g
    dx = rstd * (dxhat - xhat * jnp.mean(dxhat * xhat, axis=-1, keepdims=True))
    return dx, _colsum(dh * xhat)


def final_loss(x, g, target, name):
    m, d = x.shape
    tm = _rows(m, 512)

    def body(x_ref, g_ref, t_ref, l_ref, dx_ref, dxb_ref, dg_ref):
        xv, gv = x_ref[...], g_ref[...]
        rstd = lax.rsqrt(jnp.mean(xv * xv, axis=-1, keepdims=True) + EPS)
        err = xv * rstd * gv - t_ref[...]
        part = 0.5 * jnp.sum(jnp.mean(err * err, axis=-1, keepdims=True), axis=0, keepdims=True)
        dx, dg = _rms_bwd_math(xv, gv, err * (1.0 / d))
        dx_ref[...] = dx
        dxb_ref[...] = dx.astype(BF16)
        first = pl.program_id(0) == 0
        _acc(l_ref, jnp.broadcast_to(part, l_ref.shape), first)
        _acc(dg_ref, dg, first)

    row = pl.BlockSpec((tm, d), lambda i: (i, 0))
    vec = pl.BlockSpec((1, d), lambda i: (0, 0))
    lsp = pl.BlockSpec((1, LANES), lambda i: (0, 0))
    return _call(body, name, (m // tm,), [row, vec, row], [lsp, row, row, vec],
                 [_sds((1, LANES)), _sds((m, d)), _sds((m, d), BF16), _sds((1, d))])(x, g, target)


def _a_common(z_ref, vg_ref, ws_ref, bst_ref, tm, width):
    gw = width // A_GROUPS
    zp = z_ref[...]
    z = _gelu(zp)
    u, v = z[:, :width], z[:, width:]
    rstd = lax.rsqrt(jnp.mean(v * v, axis=-1, keepdims=True) + EPS)
    vhat = v * rstd
    vn = vhat * vg_ref[...]
    t_i = lax.broadcasted_iota(jnp.int32, (A_CHUNK, A_CHUNK), 0)
    s_i = lax.broadcasted_iota(jnp.int32, (A_CHUNK, A_CHUNK), 1)
    wsm = [jnp.where(s_i <= t_i, ws_ref[g], 0.0).astype(BF16) for g in range(A_GROUPS)]
    bst = bst_ref[...]
    return zp, u, rstd, vhat, vn.astype(BF16), wsm, bst, gw


def a_mid_fwd(z, vg, ws, bst, name):
    m, w2 = z.shape
    width = w2 // 2
    tm = _rows(m, 256)

    def body(z_ref, vg_ref, ws_ref, bst_ref, y_ref):
        _, u, _, _, vnb, wsm, bst, gw = _a_common(z_ref, vg_ref, ws_ref, bst_ref, tm, width)
        for c in range(tm // A_CHUNK):
            r0 = c * A_CHUNK
            for g in range(A_GROUPS):
                c0 = g * gw
                vs = _dot(wsm[g], vnb[r0:r0 + A_CHUNK, c0:c0 + gw]) + bst[:, g:g + 1]
                y_ref[r0:r0 + A_CHUNK, c0:c0 + gw] = (u[r0:r0 + A_CHUNK, c0:c0 + gw] * vs).astype(BF16)

    in_specs = [pl.BlockSpec((tm, w2), lambda i: (i, 0)), pl.BlockSpec((1, width), lambda i: (0, 0)),
                pl.BlockSpec((A_GROUPS, A_CHUNK, A_CHUNK), lambda i: (0, 0, 0)),
                pl.BlockSpec((A_CHUNK, A_GROUPS), lambda i: (0, 0))]
    return _call(body, name, (m // tm,), in_specs, pl.BlockSpec((tm, width), lambda i: (i, 0)),
                 _sds((m, width), BF16))(z, vg, ws, bst)


def a_mid_bwd(z, dx, w_out, vg, ws, bst, name, after=()):
    m, w2 = z.shape
    width = w2 // 2
    tm = _rows(m, 256)

    def body(z_ref, dx_ref, wo_ref, vg_ref, ws_ref, bst_ref, dz_ref, dbin_ref, dvg_ref, dws_ref, dbs_ref, dvn_scr, du_scr):
        first = pl.program_id(0) == 0
        zp, u, rstd, vhat, vnb, wsm, bst, gw = _a_common(z_ref, vg_ref, ws_ref, bst_ref, tm, width)
        dy = _dot(dx_ref[...], wo_ref[...], _NT)
        dws = [jnp.zeros((A_CHUNK, A_CHUNK), F32) for _ in range(A_GROUPS)]
        dbs = [jnp.zeros((A_CHUNK, 1), F32) for _ in range(A_GROUPS)]
        for c in range(tm // A_CHUNK):
            r0 = c * A_CHUNK
            for g in range(A_GROUPS):
                c0 = g * gw
                vn_cg = vnb[r0:r0 + A_CHUNK, c0:c0 + gw]
                vs = _dot(wsm[g], vn_cg) + bst[:, g:g + 1]
                dy_cg = dy[r0:r0 + A_CHUNK, c0:c0 + gw]
                dvs = dy_cg * u[r0:r0 + A_CHUNK, c0:c0 + gw]
                du_scr[r0:r0 + A_CHUNK, c0:c0 + gw] = dy_cg * vs
                dws[g] = dws[g] + _dot(dvs, vn_cg, _NT)
                dbs[g] = dbs[g] + jnp.sum(dvs, axis=1, keepdims=True)
                dvn_scr[r0:r0 + A_CHUNK, c0:c0 + gw] = _dot(wsm[g], dvs, _TN)
        for g in range(A_GROUPS):
            _acc(dws_ref.at[g], dws[g], first)
            _acc(dbs_ref.at[:, g * LANES:(g + 1) * LANES], jnp.broadcast_to(dbs[g], (A_CHUNK, LANES)), first)
        dvn = dvn_scr[...]
        _acc(dvg_ref, _colsum(dvn * vhat), first)
        dvhat = dvn * vg_ref[...]
        dv = rstd * (dvhat - vhat * jnp.mean(dvhat * vhat, axis=-1, keepdims=True))
        gg = _gelu_grad(zp)
        dzu = du_scr[...] * gg[:, :width]
        dzv = dv * gg[:, width:]
        dz_ref[:, :width] = dzu.astype(BF16)
        dz_ref[:, width:] = dzv.astype(BF16)
        _acc(dbin_ref.at[:, :width], _colsum(dzu), first)
        _acc(dbin_ref.at[:, width:], _colsum(dzv), first)

    const2 = lambda i: (0, 0)
    in_specs = [pl.BlockSpec((tm, w2), lambda i: (i, 0)), pl.BlockSpec((tm, dx.shape[1]), lambda i: (i, 0)),
                _resident(w_out.shape), pl.BlockSpec((1, width), const2),
                pl.BlockSpec((A_GROUPS, A_CHUNK, A_CHUNK), lambda i: (0, 0, 0)), pl.BlockSpec((A_CHUNK, A_GROUPS), const2)]
    out_specs = [pl.BlockSpec((tm, w2), lambda i: (i, 0)), pl.BlockSpec((1, w2), const2), pl.BlockSpec((1, width), const2),
                 pl.BlockSpec((A_GROUPS, A_CHUNK, A_CHUNK), lambda i: (0, 0, 0)),
                 pl.BlockSpec((A_CHUNK, A_GROUPS * LANES), const2)]
    out_shape = [_sds((m, w2), BF16), _sds((1, w2)), _sds((1, width)), _sds((A_GROUPS, A_CHUNK, A_CHUNK)),
                 _sds((A_CHUNK, A_GROUPS * LANES))]
    scratch = [pltpu.VMEM((tm, width), F32), pltpu.VMEM((tm, width), F32)]
    return _call(body, name, (m // tm,), in_specs, out_specs, out_shape, scratch, after)(z, dx, w_out, vg, ws, bst)


def _pool_minus_id(ze, i, tm, gw):
    pos = i * tm + lax.broadcasted_iota(jnp.int32, (tm, 1), 0)
    out = []
    for gi, win in enumerate(B_WINDOWS):
        s = ze[:, gi * gw:(gi + 1) * gw]
        step = 1
        while step < win:
            s = s + _down(s, step)
            step *= 2
        inv = 1.0 / jnp.minimum(pos + 1, win).astype(F32)
        out.append(s[POOL_HALO:] * inv - ze[POOL_HALO:, gi * gw:(gi + 1) * gw])
    return out


def _b_specs(tm, width):
    return [pl.BlockSpec((POOL_HALO, width), lambda i: (_prev_halo(tm, POOL_HALO)(i), 0)),
            pl.BlockSpec((tm, width), lambda i: (i, 0))]


def b_mid_fwd(z, wgrp, bgrp, scale, w_out, res, next_g, name):
    m, width = z.shape
    ng = len(B_WINDOWS)
    gw = width // ng
    tm = _rows(m, 512)

    def body(zp_ref, zm_ref, w_ref, b_ref, s_ref, *proj):
        i = pl.program_id(0)
        y_ref = proj[3]
        ze = jnp.concatenate([zp_ref[...] * (i > 0).astype(F32), zm_ref[...]], axis=0)
        p = _pool_minus_id(ze, i, tm, gw)
        for g in range(ng):
            cs = slice(g * gw, (g + 1) * gw)
            y = (_dot(p[g], w_ref[g]) + b_ref[:, cs]) * s_ref[:, cs]
            y_ref[:, cs] = y.astype(BF16)
        _project_out(y_ref, *proj[:3], *proj[4:])

    vec = pl.BlockSpec((1, width), lambda i: (0, 0))
    in_specs = _b_specs(tm, width) + [pl.BlockSpec((ng, gw, gw), lambda i: (0, 0, 0)), vec, vec]
    return _call_projected(body, name, m, tm, in_specs, (z, z, wgrp, bgrp, scale), [(width, BF16)], w_out, res, next_g)


def b_mid_bwd(z, dx, w_out, wgrp, bgrp, scale, name):
    m, width = z.shape
    ng = len(B_WINDOWS)
    gw = width // ng
    tm = _rows(m, 512)

    def body(zp_ref, zm_ref, dx_ref, wo_ref, w_ref, b_ref, s_ref, dp_ref, dw_ref, db_ref, ds_ref):
        i = pl.program_id(0)
        first = i == 0
        ze = jnp.concatenate([zp_ref[...] * (i > 0).astype(F32), zm_ref[...]], axis=0)
        p = _pool_minus_id(ze, i, tm, gw)
        dy = _dot(dx_ref[...], wo_ref[...], _NT)
        for g in range(ng):
            cs = slice(g * gw, (g + 1) * gw)
            dyg = dy[:, cs]
            ypre = _dot(p[g], w_ref[g]) + b_ref[:, cs]
            dyp = dyg * s_ref[:, cs]
            _acc(ds_ref.at[:, cs], _colsum(dyg * ypre), first)
            _acc(db_ref.at[:, cs], _colsum(dyp), first)
            _acc(dw_ref.at[g], _dot(p[g], dyp, _TN), first)
            dp_ref[:, cs] = _dot(dyp, w_ref[g], _NT)

    vec = pl.BlockSpec((1, width), lambda i: (0, 0))
    row = pl.BlockSpec((tm, width), lambda i: (i, 0))
    wsp = pl.BlockSpec((ng, gw, gw), lambda i: (0, 0, 0))
    in_specs = _b_specs(tm, width) + [pl.BlockSpec((tm, dx.shape[1]), lambda i: (i, 0)), _resident(w_out.shape), wsp, vec, vec]
    return _call(body, name, (m // tm,), in_specs, [row, wsp, vec, vec],
                 [_sds((m, width)), _sds((ng, gw, gw)), _sds((1, width)), _sds((1, width))])(z, z, dx, w_out, wgrp, bgrp, scale)


def b_pool_bwd(dp, name):
    m, width = dp.shape
    gw = width // len(B_WINDOWS)
    tm = _rows(m, 512)
    n_i = m // tm

    def body(dm_ref, dn_ref, dz_ref):
        i = pl.program_id(0)
        de = jnp.concatenate([dm_ref[...], dn_ref[...] * (i < n_i - 1).astype(F32)], axis=0)
        pos = i * tm + lax.broadcasted_iota(jnp.int32, (tm + POOL_HALO, 1), 0)
        for gi, win in enumerate(B_WINDOWS):
            cs = slice(gi * gw, (gi + 1) * gw)
            d = de[:, cs]
            s = d * (1.0 / jnp.minimum(pos + 1, win).astype(F32))
            step = 1
            while step < win:
                s = s + _up(s, step)
                step *= 2
            dz_ref[:, cs] = (s[:tm] - d[:tm]).astype(BF16)

    in_specs = [pl.BlockSpec((tm, width), lambda i: (i, 0)),
                pl.BlockSpec((POOL_HALO, width), lambda i: (_next_halo(tm, m, POOL_HALO)(i), 0))]
    return _call(body, name, (n_i,), in_specs, pl.BlockSpec((tm, width), lambda i: (i, 0)), _sds((m, width), BF16))(dp, dp)


def _c_gates(xr, wa_ref, ba_ref, wi_ref, bi_ref, lam_ref, heads, hw):
    xb = xr.astype(BF16)
    ra = jnp.concatenate([_dot(xb[:, h * hw:(h + 1) * hw], wa_ref[h]) for h in range(heads)], axis=1) + ba_ref[...]
    ia = jnp.concatenate([_dot(xb[:, h * hw:(h + 1) * hw], wi_ref[h]) for h in range(heads)], axis=1) + bi_ref[...]
    r, ig = _sigmoid(ra), _sigmoid(ia)
    sp = _softplus(-lam_ref[...])
    log_a = (-C_GATE_C * r) * sp
    a = jnp.exp(log_a)
    mult = jnp.sqrt(-_expm1(2.0 * log_a))
    return xb, r, ig, sp, a, mult


def c_mid_fwd(z, cw, cb, wa, ba, wi, bi, lam, name):
    m, w2 = z.shape
    width = w2 // 2
    heads, hw = wa.shape[0], wa.shape[1]
    taps = cw.shape[0]
    tm = _rows(m, 512)

    def body(zp_ref, zm_ref, cw_ref, cb_ref, wa_ref, ba_ref, wi_ref, bi_ref, lam_ref, a_ref, b_ref, xr_ref):
        i = pl.program_id(0)
        xe = jnp.concatenate([zp_ref[...] * (i > 0).astype(F32), zm_ref[...]], axis=0)
        xr = _conv_ext(xe, cw_ref[...], taps)[HALO:] + cb_ref[...]
        _, _, ig, _, a, mult = _c_gates(xr, wa_ref, ba_ref, wi_ref, bi_ref, lam_ref, heads, hw)
        a_ref[...] = a
        b_ref[...] = mult * (ig * xr)
        xr_ref[...] = xr

    vec = pl.BlockSpec((1, width), lambda i: (0, 0))
    row = pl.BlockSpec((tm, width), lambda i: (i, 0))
    wsp = pl.BlockSpec((heads, hw, hw), lambda i: (0, 0, 0))
    in_specs = [pl.BlockSpec((HALO, width), lambda i: (_prev_halo(tm)(i), 1)), pl.BlockSpec((tm, width), lambda i: (i, 1)),
                pl.BlockSpec((taps, width), lambda i: (0, 0)), vec, wsp, vec, wsp, vec, vec]
    return _call(body, name, (m // tm,), in_specs, [row, row, row], [_sds((m, width))] * 3)(
        z, z, cw, cb, wa, ba, wi, bi, lam)


_SCAN_ROWS = 512


def c_scan_fwd(a, b, z, w_out, res, next_g, name):
    m, width = a.shape
    tm = _rows(m, _SCAN_ROWS)

    def body(a_ref, b_ref, g_ref, wo_ref, res_ref, ng_ref, hs_ref, y_ref, x_ref, h_ref, h_carry):
        @pl.when(pl.program_id(0) == 0)
        def _():
            h_carry[...] = jnp.zeros_like(h_carry)

        def step(t, h):
            h = a_ref[pl.ds(t, 1), :] * h + b_ref[pl.ds(t, 1), :]
            hs_ref[pl.ds(t, 1), :] = h
            return h

        h_carry[...] = lax.fori_loop(0, tm, step, h_carry[...], unroll=8)
        y_ref[...] = (hs_ref[...] * _gelu(g_ref[...])).astype(BF16)
        _project_out(y_ref, wo_ref, res_ref, ng_ref, x_ref, h_ref)

    row = pl.BlockSpec((tm, width), lambda i: (i, 0))
    return _call_projected(body, name, m, tm, [row, row, row], (a, b, z), [(width, F32), (width, BF16)], w_out, res, next_g,
                           [pltpu.VMEM((1, width), F32)])


def c_scan_bwd(dx, w_out, z, hs, a, name):
    m, width = a.shape
    tm = _rows(m, _SCAN_ROWS)
    n_i = m // tm

    def body(dx_ref, wo_ref, g_ref, hs_ref, hp_ref, a_ref, lam_ref, da_ref, dg_ref, dgs_ref, lam_carry, a_carry):
        i = pl.program_id(0)
        first = i == 0

        @pl.when(first)
        def _():
            lam_carry[...] = jnp.zeros_like(lam_carry)
            a_carry[...] = jnp.zeros_like(a_carry)

        gp, dyv, hsv = g_ref[...], _dot(dx_ref[...], wo_ref[...], _NT), hs_ref[...]
        dgate = dyv * hsv * _gelu_grad(gp)
        dg_ref[...] = dgate.astype(BF16)
        _acc(dgs_ref, _colsum(dgate), first)
        lam_ref[...] = dyv * _gelu(gp)

        def step(k, carry):
            lam_next, a_next = carry
            t = tm - 1 - k
            lam_t = lam_ref[pl.ds(t, 1), :] + a_next * lam_next
            lam_ref[pl.ds(t, 1), :] = lam_t
            return lam_t, a_ref[pl.ds(t, 1), :]

        lam_c, a_c = lax.fori_loop(0, tm, step, (lam_carry[...], a_carry[...]), unroll=8)
        lam_carry[...] = lam_c
        a_carry[...] = a_c
        h_before = hp_ref[HALO - 1:HALO, :] * (i < n_i - 1).astype(F32)
        t_i = lax.broadcasted_iota(jnp.int32, (tm, 1), 0)
        da_ref[...] = lam_ref[...] * jnp.where(t_i == 0, h_before, _down(hsv, 1))

    row = pl.BlockSpec((tm, width), lambda i: (n_i - 1 - i, 0))
    halo = pl.BlockSpec((HALO, width), lambda i: (_prev_halo(tm)(n_i - 1 - i), 0))
    vec = pl.BlockSpec((1, width), lambda i: (0, 0))
    in_specs = [pl.BlockSpec((tm, dx.shape[1]), lambda i: (n_i - 1 - i, 0)), _resident(w_out.shape), row, row, halo, row]
    return _call(body, name, (n_i,), in_specs, [row, row, row, vec],
                 [_sds((m, width)), _sds((m, width)), _sds((m, width), BF16), _sds((1, width))],
                 [pltpu.VMEM((1, width), F32), pltpu.VMEM((1, width), F32)])(dx, w_out, z, hs, hs, a)


def c_mid_bwd(lam_seq, da, xr, wa, ba, wi, bi, lam, name):
    m, width = xr.shape
    heads, hw = wa.shape[0], wa.shape[1]
    tm = _rows(m, 512)

    def body(l_ref, da_ref, xr_ref, wa_ref, ba_ref, wi_ref, bi_ref, lam_ref,
             dxr_ref, dwa_ref, dwi_ref, dba_ref, dbi_ref, dlam_ref):
        first = pl.program_id(0) == 0
        xr_v, lmb = xr_ref[...], l_ref[...]
        xb, r, ig, sp, a, mult = _c_gates(xr_v, wa_ref, ba_ref, wi_ref, bi_ref, lam_ref, heads, hw)
        dmult = lmb * (ig * xr_v)
        dig = lmb * mult * xr_v
        dxr = lmb * mult * ig
        dla = da_ref[...] * a - dmult * (a * a) / mult
        dr = dla * (-C_GATE_C * sp)
        dsp = _colsum(dla * (-C_GATE_C * r))
        _acc(dlam_ref, dsp * (-_sigmoid(-lam_ref[...])), first)
        dra = dr * r * (1.0 - r)
        dia = dig * ig * (1.0 - ig)
        _acc(dba_ref, _colsum(dra), first)
        _acc(dbi_ref, _colsum(dia), first)
        for h in range(heads):
            cs = slice(h * hw, (h + 1) * hw)
            _acc(dwa_ref.at[h], _dot(xb[:, cs], dra[:, cs], _TN), first)
            _acc(dwi_ref.at[h], _dot(xb[:, cs], dia[:, cs], _TN), first)
            dxr_ref[:, cs] = dxr[:, cs] + _dot(dra[:, cs], wa_ref[h], _NT) + _dot(dia[:, cs], wi_ref[h], _NT)

    vec = pl.BlockSpec((1, width), lambda i: (0, 0))
    row = pl.BlockSpec((tm, width), lambda i: (i, 0))
    wsp = pl.BlockSpec((heads, hw, hw), lambda i: (0, 0, 0))
    return _call(body, name, (m // tm,), [row, row, row, wsp, vec, wsp, vec, vec], [row, wsp, wsp, vec, vec, vec],
                 [_sds((m, width)), _sds((heads, hw, hw)), _sds((heads, hw, hw)), _sds((1, width)), _sds((1, width)),
                  _sds((1, width))])(lam_seq, da, xr, wa, ba, wi, bi, lam)


def conv_bwd(dy, x_src, col_block, cw, name):
    m, width = dy.shape
    taps = cw.shape[0]
    tm = _rows(m, 512)
    n_i = m // tm

    def body(dm_ref, dn_ref, xp_ref, xm_ref, cw_ref, dx_ref, dw_ref, db_ref, dxs_ref):
        i = pl.program_id(0)
        first = i == 0
        de = jnp.concatenate([jnp.zeros((HALO, width), F32), dm_ref[...], dn_ref[...] * (i < n_i - 1).astype(F32)], axis=0)
        xe = jnp.concatenate([xp_ref[...] * (i > 0).astype(F32), xm_ref[...], jnp.zeros((HALO, width), F32)], axis=0)
        w = cw_ref[...]
        dx = de * w[taps - 1:taps]
        for s in range(1, taps):
            dx = dx + _up(de, s) * w[taps - 1 - s:taps - s]
        dx_ref[...] = dx[HALO:HALO + tm].astype(BF16)
        _acc(dxs_ref, _colsum(dx[HALO:HALO + tm]), first)
        dm = dm_ref[...]
        for s in range(taps):
            _acc(dw_ref.at[taps - 1 - s:taps - s, :], _colsum(dm * _down(xe, s)[HALO:HALO + tm]), first)
        _acc(db_ref, _colsum(dm), first)

    row = pl.BlockSpec((tm, width), lambda i: (i, 0))
    vec = pl.BlockSpec((1, width), lambda i: (0, 0))
    tsp = pl.BlockSpec((taps, width), lambda i: (0, 0))
    in_specs = [row, pl.BlockSpec((HALO, width), lambda i: (_next_halo(tm, m)(i), 0)),
                pl.BlockSpec((HALO, width), lambda i: (_prev_halo(tm)(i), col_block)),
                pl.BlockSpec((tm, width), lambda i: (i, col_block)), tsp]
    return _call(body, name, (n_i,), in_specs, [row, tsp, vec, vec],
                 [_sds((m, width), BF16), _sds((taps, width)), _sds((1, width)), _sds((1, width))])(dy, dy, x_src, x_src, cw)


def d_mid_fwd(z, cw, w_out, res, next_g, name):
    m, w3 = z.shape
    width = w3 // 3
    taps = cw.shape[0]
    tm = _rows(m, 512)

    def body(bm_ref, cp_ref, cm_ref, xp_ref, xm_ref, cw_ref, wo_ref, res_ref, ng_ref, y_ref, x_ref, h_ref):
        keep = (pl.program_id(0) > 0).astype(F32)
        qe = (jnp.concatenate([cp_ref[...] * keep, cm_ref[...]], axis=0)
              * jnp.concatenate([xp_ref[...], xm_ref[...]], axis=0))
        y_ref[...] = (bm_ref[...] * _conv_ext(qe, cw_ref[...], taps)[HALO:]).astype(BF16)
        _project_out(y_ref, wo_ref, res_ref, ng_ref, x_ref, h_ref)

    main = lambda c: pl.BlockSpec((tm, width), lambda i: (i, c))
    prev = lambda c: pl.BlockSpec((HALO, width), lambda i: (_prev_halo(tm)(i), c))
    in_specs = [main(0), prev(1), main(1), prev(2), main(2), pl.BlockSpec((taps, width), lambda i: (0, 0))]
    return _call_projected(body, name, m, tm, in_specs, (z, z, z, z, z, cw), [(width, BF16)], w_out, res, next_g)


def d_mid_bwd(z, dy, cw, name):
    m, w3 = z.shape
    width = w3 // 3
    taps = cw.shape[0]
    tm = _rows(m, 512)
    n_i = m // tm

    def body(bm_ref, bn_ref, cp_ref, cm_ref, cn_ref, xp_ref, xm_ref, xn_ref, dm_ref, dn_ref, cw_ref, dz_ref, dw_ref):
        i = pl.program_id(0)
        first = i == 0
        kp, kn = (i > 0).astype(F32), (i < n_i - 1).astype(F32)
        zeros = jnp.zeros((HALO, width), F32)
        ce = jnp.concatenate([cp_ref[...] * kp, cm_ref[...], cn_ref[...] * kn], axis=0)
        xe = jnp.concatenate([xp_ref[...], xm_ref[...], xn_ref[...]], axis=0)
        qe = ce * xe
        be = jnp.concatenate([zeros, bm_ref[...], bn_ref[...]], axis=0)
        dye = jnp.concatenate([zeros, dm_ref[...], dn_ref[...] * kn], axis=0)
        w = cw_ref[...]
        cq = _conv_ext(qe, w, taps)
        dcq = dye * be
        dq = dcq * w[taps - 1:taps]
        for s in range(1, taps):
            dq = dq + _up(dcq, s) * w[taps - 1 - s:taps - s]
        ms = slice(HALO, HALO + tm)
        dz_ref[:, :width] = (dye * cq)[ms].astype(BF16)
        dz_ref[:, width:2 * width] = (dq * xe)[ms].astype(BF16)
        dz_ref[:, 2 * width:] = (dq * ce)[ms].astype(BF16)
        for s in range(taps):
            _acc(dw_ref.at[taps - 1 - s:taps - s, :], _colsum(dcq[ms] * _down(qe, s)[ms]), first)

    main = lambda c: pl.BlockSpec((tm, width), lambda i: (i, c))
    prev = lambda c: pl.BlockSpec((HALO, width), lambda i: (_prev_halo(tm)(i), c))
    nxt = lambda c: pl.BlockSpec((HALO, width), lambda i: (_next_halo(tm, m)(i), c))
    tsp = pl.BlockSpec((taps, width), lambda i: (0, 0))
    in_specs = [main(0), nxt(0), prev(1), main(1), nxt(1), prev(2), main(2), nxt(2), main(0), nxt(0), tsp]
    return _call(body, name, (n_i,), in_specs, [pl.BlockSpec((tm, w3), lambda i: (i, 0)), tsp],
                 [_sds((m, w3), BF16), _sds((taps, width))])(z, z, z, z, z, z, z, z, dy, dy, cw)


def _halo_rows(dtype):
    return HALO * (4 // jnp.dtype(dtype).itemsize)


def ffn_gate_down(z, cw, cb, w_down, res, next_g, name):
    _, nj, m, c = z.shape
    n = w_down.shape[1]
    taps = cw.shape[2]
    tm = _rows(m, 256)
    hz = _halo_rows(z.dtype)

    def body(zp_ref, zm_ref, cw_ref, cb_ref, w_ref, res_ref, *rest):
        keep = (pl.program_id(0) > 0).astype(F32)
        zc_ref, act_ref, x_ref = rest[-4:-1] if next_g is not None else rest[-3:]
        acc = res_ref[...]
        for j in range(nj):
            zc = []
            for s in range(2):
                xe = jnp.concatenate([zp_ref[s, j].astype(F32) * keep, zm_ref[s, j].astype(F32)], axis=0)
                zc.append(_conv_ext(xe, cw_ref[s, j], taps)[hz:] + cb_ref[s, j])
                zc_ref[s, j] = zc[s].astype(BF16)
            act = (zc[0] * _sigmoid(zc[0]) * zc[1]).astype(BF16)
            act_ref[j] = act
            acc = acc + _dot(act, w_ref[j])
        x_ref[...] = acc
        if next_g is not None:
            rest[-1][...] = _norm_rows(acc, rest[0][...])

    row = pl.BlockSpec((tm, n), lambda i: (i, 0))
    in_specs = [pl.BlockSpec((2, nj, hz, c), lambda i: (0, 0, _prev_halo(tm, hz)(i), 0)),
                pl.BlockSpec((2, nj, tm, c), lambda i: (0, 0, i, 0)),
                _resident(cw.shape), _resident(cb.shape), _resident((nj, c, n)), row]
    out_specs = [pl.BlockSpec((2, nj, tm, c), lambda i: (0, 0, i, 0)), pl.BlockSpec((nj, tm, c), lambda i: (0, i, 0)), row]
    out_shape = [_sds((2, nj, m, c), BF16), _sds((nj, m, c), BF16), _sds((m, n))]
    ins = [z, z, cw, cb, w_down.reshape(nj, c, n), res]
    if next_g is not None:
        in_specs.append(pl.BlockSpec((1, n), lambda i: (0, 0)))
        out_specs.append(row)
        out_shape.append(_sds((m, n), BF16))
        ins.append(next_g)
    return _call(body, name, (m // tm,), in_specs, out_specs, out_shape)(*ins)


def ffn_gate_bwd(z, zc, dx, w_down, cw, name, after=()):
    _, nj, m, c = z.shape
    n = w_down.shape[1]
    taps = cw.shape[2]
    tm = _rows(m, 512)
    n_i = m // tm
    hz = _halo_rows(z.dtype)
    assert _halo_rows(dx.dtype) == hz and zc.dtype == z.dtype, (z.dtype, zc.dtype, dx.dtype)

    def body(zp_ref, zm_ref, zn_ref, cm_ref, cn_ref, dm_ref, dn_ref, wd_ref, cw_ref, dz_ref, dw_ref, db_ref):
        i = pl.program_id(1)
        first = i == 0
        kp, kn = (i > 0).astype(F32), (i < n_i - 1).astype(F32)
        xe = [jnp.concatenate([zp_ref[s].astype(F32) * kp, zm_ref[s].astype(F32), zn_ref[s].astype(F32) * kn], axis=0)
              for s in range(2)]
        zc = [jnp.concatenate([jnp.zeros((hz, c), F32), cm_ref[s].astype(F32), cn_ref[s].astype(F32)], axis=0)
              for s in range(2)]
        dact = _dot(jnp.concatenate([dm_ref[...], dn_ref[...]], axis=0), wd_ref[...], _NT)
        dae = jnp.concatenate([jnp.zeros((hz, c), F32), dact[:tm], dact[tm:] * kn], axis=0)
        sg = _sigmoid(zc[0])
        dzc = [dae * zc[1] * (sg * (1.0 + zc[0] * (1.0 - sg))), dae * (zc[0] * sg)]
        ms = slice(hz, hz + tm)
        for s in range(2):
            w = cw_ref[s]
            ups = [dzc[s]] + [_up(dzc[s], u) for u in range(1, taps)]
            dxs = ups[0] * w[taps - 1:taps]
            for u in range(1, taps):
                dxs = dxs + ups[u] * w[taps - 1 - u:taps - u]
            dz_ref[s] = dxs[ms].astype(BF16)
            tail = dzc[s][hz + tm:]
            x_end = xe[s][tm:]
            for u in range(taps):
                total = _colsum(ups[u] * xe[s]) - _colsum(tail * _down(x_end, u)[hz:])
                _acc(dw_ref.at[s, taps - 1 - u:taps - u, :], total, first)
            _acc(db_ref.at[s], _colsum(dzc[s][ms]), first)

    in_specs = [pl.BlockSpec((2, None, hz, c), lambda j, i: (0, j, _prev_halo(tm, hz)(i), 0)),
                pl.BlockSpec((2, None, tm, c), lambda j, i: (0, j, i, 0)),
                pl.BlockSpec((2, None, hz, c), lambda j, i: (0, j, _next_halo(tm, m, hz)(i), 0)),
                pl.BlockSpec((2, None, tm, c), lambda j, i: (0, j, i, 0)),
                pl.BlockSpec((2, None, hz, c), lambda j, i: (0, j, _next_halo(tm, m, hz)(i), 0)),
                pl.BlockSpec((tm, n), lambda j, i: (i, 0)),
                pl.BlockSpec((hz, n), lambda j, i: (_next_halo(tm, m, hz)(i), 0)),
                pl.BlockSpec((None, c, n), lambda j, i: (j, 0, 0)),
                pl.BlockSpec((2, None, taps, c), lambda j, i: (0, j, 0, 0))]
    out_specs = [pl.BlockSpec((2, None, tm, c), lambda j, i: (0, j, i, 0)),
                 pl.BlockSpec((2, None, taps, c), lambda j, i: (0, j, 0, 0)),
                 pl.BlockSpec((2, None, 1, c), lambda j, i: (0, j, 0, 0))]
    return _call(body, name, (nj, n_i), in_specs, out_specs,
                 [_sds((2, nj, m, c), BF16), _sds((2, nj, taps, c)), _sds((2, nj, 1, c))], (), after)(
        z, z, z, zc, zc, dx, dx, w_down.reshape(nj, c, n), cw)


_STREAM_TILE_BYTES = 2 << 20


def _tile_rows(r, c):
    if r * c * 4 <= _STREAM_TILE_BYTES:
        return r
    fits = [d for d in range(16, r, 16) if r % d == 0 and d * c * 4 <= _STREAM_TILE_BYTES]
    return max(fits) if fits else r


def _as2d(a, lead):
    shape = a.shape
    return a.reshape((lead, -1, shape[-1]) if lead else (-1, shape[-1]))


def add_pairs(g, l1, own, name):
    shape = l1.shape
    g3, l3 = _as2d(g, N_DEV), _as2d(l1, 4)
    _, r, c = l3.shape
    tr = _tile_rows(r, c)

    def body(own_ref, a_ref, b_ref, o_ref):
        o_ref[...] = (a_ref[...].astype(F32) + b_ref[...].astype(F32)).astype(o_ref.dtype)

    spec = pl.BlockSpec((None, tr, c), lambda k, i, own_ref: (k, i, 0))
    grid_spec = pltpu.PrefetchScalarGridSpec(
        num_scalar_prefetch=1, grid=(4, r // tr),
        in_specs=[pl.BlockSpec((None, tr, c), lambda k, i, own_ref: (own_ref[k], i, 0)), spec], out_specs=spec)
    out = pl.pallas_call(
        body, name=name, grid_spec=grid_spec, out_shape=_sds(l3.shape, l1.dtype),
        compiler_params=pltpu.CompilerParams(dimension_semantics=("arbitrary", "arbitrary"),
                                             vmem_limit_bytes=V7X_VMEM_LIMIT_BYTES))(own, g3, l3)
    return out.reshape(shape)


def _grad_sum(p_ref, l_ref):
    return ((p_ref[...].astype(F32) + l_ref[0].astype(F32)) + l_ref[1].astype(F32)) + l_ref[2].astype(F32)


def sum_parts(p, l2, name):
    _, r, c = p.shape

    def body(p_ref, l_ref, o_ref):
        o_ref[...] = _grad_sum(p_ref, l_ref)

    return _call(body, name, (1,), [pl.BlockSpec((None, r, c), lambda i: (0, 0, 0)), pl.BlockSpec((3, r, c), lambda i: (0, 0, 0))],
                 pl.BlockSpec((r, c), lambda i: (0, 0)), _sds((r, c)))(p, l2)


def _adamw_math(w, g, m, v):
    m = ADAM_B1 * m + (1.0 - ADAM_B1) * g
    v = ADAM_B2 * v + (1.0 - ADAM_B2) * (g * g)
    m_hat = m / (1.0 - ADAM_B1 ** ADAM_STEP)
    v_hat = v / (1.0 - ADAM_B2 ** ADAM_STEP)
    delta = -ADAM_LR * (m_hat / (jnp.sqrt(v_hat) + ADAM_EPS) + ADAM_WD * w)
    return delta, m, v


def adamw(w, m, v, name, g=None, p=None, l2=None):
    shape = w.shape
    w2, m2, v2 = (_as2d(t, 0) for t in (w, m, v))
    r, c = w2.shape
    tr = _tile_rows(r, c)
    row = pl.BlockSpec((tr, c), lambda i: (i, 0))
    if g is None:
        p3, l3 = _as2d(p, 4), _as2d(l2, 3)
        gin = (p3, l3)
        gspecs = [pl.BlockSpec((None, tr, c), lambda i: (0, i, 0)), pl.BlockSpec((3, tr, c), lambda i: (0, i, 0))]
    else:
        gin, gspecs = (_as2d(g, 0),), [row]

    def body(*refs):
        n_g = len(gin)
        w_ref, m_ref, v_ref, g_ref, d_ref, nm_ref, nv_ref = refs[n_g:]
        grad = refs[0][...] if n_g == 1 else _grad_sum(refs[0], refs[1])
        delta, nm, nv = _adamw_math(w_ref[...], grad, m_ref[...], v_ref[...])
        g_ref[...] = grad
        d_ref[...] = delta
        nm_ref[...] = nm
        nv_ref[...] = nv

    outs = _call(body, name, (r // tr,), gspecs + [row, row, row], [row] * 4, [_sds((r, c))] * 4)(*gin, w2, m2, v2)
    return tuple(o.reshape(shape) for o in outs)


def adamw_layer(w, m, v, layer, prev, name, p, l2):
    n_l, r, c = w.shape
    tr = _tile_rows(r, c)
    slab = pl.BlockSpec((None, tr, c), lambda i: (layer, i, 0))
    in_specs = [pl.BlockSpec((None, tr, c), lambda i: (0, i, 0)), pl.BlockSpec((3, tr, c), lambda i: (0, i, 0)), slab, slab, slab]
    n_in = len(in_specs)
    prev = () if prev is None else tuple(prev)

    def body(p_ref, l_ref, w_ref, m_ref, v_ref, *rest):
        g_ref, d_ref, nm_ref, nv_ref = rest[len(prev):]
        grad = _grad_sum(p_ref, l_ref)
        delta, nm, nv = _adamw_math(w_ref[...], grad, m_ref[...], v_ref[...])
        g_ref[...] = grad
        d_ref[...] = delta
        nm_ref[...] = nm
        nv_ref[...] = nv

    return pl.pallas_call(
        body, name=name, grid=(r // tr,), in_specs=in_specs + [pl.BlockSpec(memory_space=pl.ANY)] * len(prev),
        out_specs=[slab] * 4, out_shape=[_sds((n_l, r, c))] * 4,
        input_output_aliases={n_in + q: q for q in range(len(prev))},
        compiler_params=pltpu.CompilerParams(dimension_semantics=("arbitrary",), vmem_limit_bytes=V7X_VMEM_LIMIT_BYTES),
    )(_as2d(p, 4), _as2d(l2, 3), w, m, v, *prev)


def _comm_call(body, name, ins, out_shape, n_sems):
    any_spec = pl.BlockSpec(memory_space=pl.ANY)
    return pl.pallas_call(
        body, name=name, in_specs=[any_spec] * len(ins), out_specs=[any_spec] * len(out_shape), out_shape=out_shape,
        scratch_shapes=[pltpu.SemaphoreType.DMA((n,)) for n in n_sems],
        compiler_params=pltpu.CompilerParams(has_side_effects=True))(*ins)


def _place():
    return lax.axis_index("x"), lax.axis_index("y"), lax.axis_index("c")


def _dev_index(px, py, pc):
    return 4 * px + 2 * py + pc


def all_gather(blocks, name):
    n_t = len(blocks)

    def body(*refs):
        ins, outs = refs[:n_t], refs[n_t:2 * n_t]
        send_sems, recv_sems, local_sems = refs[2 * n_t:]
        x, y, c = _place()
        me, sibling = (x, y, c), (x, y, 1 - c)
        chips = [(1 - x, y), (x, 1 - y), (1 - x, 1 - y)]

        def copy(t, k, block, to, src=None):
            dst = outs[t].at[_dev_index(*block)]
            return pltpu.make_async_remote_copy(
                src_ref=dst if src is None else src, dst_ref=dst, send_sem=send_sems.at[t * 7 + k],
                recv_sem=recv_sems.at[t * 7 + k], device_id=to, device_id_type=MESH_ID)

        mine = [pltpu.make_async_copy(ins[t], outs[t].at[_dev_index(*me)], local_sems.at[t]) for t in range(n_t)]
        for cp in mine:
            cp.start()
        first = []
        for t in range(n_t):
            first.append(copy(t, 0, me, sibling, src=ins[t]))
            first += [copy(t, 1 + j, me, (*chip, c), src=ins[t]) for j, chip in enumerate(chips)]
        for cp in first:
            cp.start()
        passed = []
        for t in range(n_t):
            for j, chip in enumerate(chips):
                copy(t, 1 + j, (*chip, c), me).wait_recv()
                cp = copy(t, 4 + j, (*chip, c), sibling)
                cp.start()
                passed.append(cp)
        for t in range(n_t):
            copy(t, 0, sibling, me).wait_recv()
            for j, chip in enumerate(chips):
                copy(t, 4 + j, (*chip, 1 - c), me).wait_recv()
        for cp in first + passed:
            cp.wait_send()
        for cp in mine:
            cp.wait()

    out_shape = [_sds((N_DEV,) + b.shape, b.dtype) for b in blocks]
    return _comm_call(body, name, blocks, out_shape, (7 * n_t, 7 * n_t, n_t))


def _chip_of(x, y, k):
    return (x if k % 2 == 0 else 1 - x), (y if k // 2 == 0 else 1 - y)


_HBM_SPEC = pl.BlockSpec(memory_space=pltpu.HBM)
_SEM_SPEC = pl.BlockSpec(memory_space=pltpu.SEMAPHORE)
_DATAFLOW = pltpu.SideEffectType.DATAFLOW_SIDE_EFFECTING


def _in_hbm(a):
    return pltpu.with_memory_space_constraint(a, pltpu.HBM)


def _split_start(name, issue, srcs, land_shapes, sem_counts, after=()):
    n_buf, n_sem, n_after = len(srcs) + len(land_shapes), len(sem_counts), len(after)

    def body(*refs):
        issue(refs[:len(srcs)], refs[len(srcs):n_buf], refs[n_buf + n_after:n_buf + n_after + n_sem])
        refs[-1][...] = jnp.zeros_like(refs[-1])

    bufs = [pltpu.HBM(s.shape, s.dtype) for s in list(srcs) + list(land_shapes)]
    outs = pl.pallas_call(
        body, name=name, in_specs=(*[_HBM_SPEC] * n_buf, *[pl.BlockSpec(memory_space=pl.ANY)] * n_after),
        out_shape=(*[pltpu.SemaphoreType.DMA((n,)) for n in sem_counts], *bufs, _sds((8, LANES))),
        out_specs=(*[_SEM_SPEC] * n_sem, *[_HBM_SPEC] * n_buf, pl.BlockSpec(memory_space=pltpu.VMEM)),
        input_output_aliases={i: n_sem + i for i in range(n_buf)},
        compiler_params=pltpu.CompilerParams(has_side_effects=_DATAFLOW),
    )(*[_in_hbm(s) for s in srcs], *[_in_hbm(lax.empty(s.shape, s.dtype)) for s in land_shapes], *after)
    return outs[:n_sem], outs[n_sem:n_sem + len(srcs)], outs[n_sem + len(srcs):n_sem + n_buf], outs[-1]


def _split_wait(name, finish, sems, srcs, lands, after):
    n_buf, n_sem = len(srcs) + len(lands), len(sems)

    def body(*refs):
        finish(refs[:len(srcs)], refs[len(srcs):n_buf], refs[n_buf:n_buf + n_sem])

    bufs = [pltpu.HBM(s.shape, s.dtype) for s in list(srcs) + list(lands)]
    outs = pl.pallas_call(
        body, name=name, in_specs=(*[_HBM_SPEC] * n_buf, *[_SEM_SPEC] * n_sem, *[pl.BlockSpec(memory_space=pl.ANY)] * len(after)),
        out_shape=tuple(bufs), out_specs=(_HBM_SPEC,) * n_buf, input_output_aliases={i: i for i in range(n_buf)},
        compiler_params=pltpu.CompilerParams(has_side_effects=_DATAFLOW),
    )(*srcs, *lands, *sems, *after)
    return outs[:len(srcs)], outs[len(srcs):]


def _peer(x, y, c, r):
    return (1 - x if r & 4 else x), (1 - y if r & 2 else y), (1 - c if r & 1 else c)


ALL_PEERS = tuple(range(1, N_DEV))
SAME_CORE_PEERS = (2, 4, 6)


def _gather_copies(src_refs, land_refs, sem_refs, arrivals, peers):
    send_sems, recv_sems, local_sems = sem_refs
    x, y, c = _place()
    me = _dev_index(x, y, c)
    local, sends, recvs = [], [], []
    for j, (src, land) in enumerate(zip(src_refs, land_refs)):
        local.append(pltpu.make_async_copy(src, land.at[me], local_sems.at[j]))
        for p, r in enumerate(peers):
            peer = _peer(x, y, c, r)
            q = len(peers) * j + p
            sends.append(pltpu.make_async_remote_copy(src_ref=src, dst_ref=land.at[me], send_sem=send_sems.at[q],
                                                      recv_sem=recv_sems.at[q], device_id=peer, device_id_type=MESH_ID))
            if arrivals:
                recvs.append(pltpu.make_async_remote_copy(
                    src_ref=src, dst_ref=land.at[_dev_index(*peer)], send_sem=send_sems.at[q], recv_sem=recv_sems.at[q],
                    device_id=peer, device_id_type=MESH_ID))
    return local, sends, recvs


def gather_start(groups, peers, name, after=()):
    flat = [b for g in groups for b in g]
    bounds = [sum(len(g) for g in groups[:i]) for i in range(len(groups) + 1)]

    def issue(src_refs, land_refs, sem_refs):
        for i in range(len(groups)):
            lo, hi = bounds[i], bounds[i + 1]
            local, sends, _ = _gather_copies(src_refs[lo:hi], land_refs[lo:hi], sem_refs[3 * i:3 * i + 3], False, peers[i])
            for cp in local + sends:
                cp.start()

    sem_counts = [n for g, p in zip(groups, peers) for n in (len(p) * len(g), len(p) * len(g), len(g))]
    sems, srcs, lands, _ = _split_start(name, issue, flat, [_sds((N_DEV,) + b.shape, b.dtype) for b in flat], sem_counts, after)
    return [(sems[3 * i:3 * i + 3], srcs[bounds[i]:bounds[i + 1]], lands[bounds[i]:bounds[i + 1]], peers[i])
            for i in range(len(groups))]


def gather_wait(group, after, name):
    sems, srcs, lands, peers = group

    def finish(src_refs, land_refs, sem_refs):
        local, sends, recvs = _gather_copies(src_refs, land_refs, sem_refs, True, peers)
        for cp in local:
            cp.wait()
        for cp in recvs:
            cp.wait_recv()
        for cp in sends:
            cp.wait_send()

    return _split_wait(name, finish, sems, srcs, lands, after)[1]


def _forward_copies(land_refs, sem_refs, arrivals):
    send_sems, recv_sems = sem_refs
    x, y, c = _place()
    sends, recvs = [], []
    for t, land in enumerate(land_refs):
        for k in range(4):
            cx, cy = _chip_of(x, y, k)
            mine, theirs = land.at[_dev_index(cx, cy, c)], land.at[_dev_index(cx, cy, 1 - c)]
            sems = dict(send_sem=send_sems.at[4 * t + k], recv_sem=recv_sems.at[4 * t + k], device_id=(x, y, 1 - c),
                        device_id_type=MESH_ID)
            sends.append(pltpu.make_async_remote_copy(src_ref=mine, dst_ref=mine, **sems))
            if arrivals:
                recvs.append(pltpu.make_async_remote_copy(src_ref=theirs, dst_ref=theirs, **sems))
    return sends, recvs


def forward_start(lands, name):
    def issue(land_refs, _, sem_refs):
        for cp in _forward_copies(land_refs, sem_refs, False)[0]:
            cp.start()

    n = 4 * len(lands)
    sems, lands, _, token = _split_start(name, issue, lands, [], (n, n))
    return sems, lands, token


def forward_wait(started, after, name):
    sems, lands, _ = started

    def finish(land_refs, _, sem_refs):
        sends, recvs = _forward_copies(land_refs, sem_refs, True)
        for cp in recvs:
            cp.wait_recv()
        for cp in sends:
            cp.wait_send()

    return _split_wait(name, finish, sems, lands, [], after)[0]


def _sibling_copies(src_refs, land_refs, sem_refs):
    send_sems, recv_sems = sem_refs
    x, y, c = _place()
    copies = []
    for t, (src, land) in enumerate(zip(src_refs, land_refs)):
        for k in range(4):
            cx, cy = _chip_of(x, y, k)
            copies.append(pltpu.make_async_remote_copy(
                src_ref=src.at[_dev_index(cx, cy, 1 - c)], dst_ref=land.at[k], send_sem=send_sems.at[4 * t + k],
                recv_sem=recv_sems.at[4 * t + k], device_id=(x, y, 1 - c), device_id_type=MESH_ID))
    return copies


def _chip_copies(src_refs, land_refs, sem_refs):
    send_sems, recv_sems = sem_refs
    x, y, c = _place()
    copies = []
    for t, (src, land) in enumerate(zip(src_refs, land_refs)):
        for k in range(1, 4):
            cx, cy = _chip_of(x, y, k)
            copies.append(pltpu.make_async_remote_copy(
                src_ref=src.at[k], dst_ref=land.at[k - 1], send_sem=send_sems.at[3 * t + k - 1],
                recv_sem=recv_sems.at[3 * t + k - 1], device_id=(cx, cy, c), device_id_type=MESH_ID))
    return copies


def _exchange_start(copies_of, n_land, per_array, arrays, name):
    def issue(src_refs, land_refs, sem_refs):
        for cp in copies_of(src_refs, land_refs, sem_refs):
            cp.start()

    n = per_array * len(arrays)
    lands = [_sds((n_land,) + a.shape[1:], a.dtype) for a in arrays]
    return _split_start(name, issue, arrays, lands, (n, n))


def _exchange_wait(copies_of, started, after, name):
    sems, srcs, lands, _ = started

    def finish(src_refs, land_refs, sem_refs):
        copies = copies_of(src_refs, land_refs, sem_refs)
        for cp in copies:
            cp.wait_recv()
        for cp in copies:
            cp.wait_send()

    return _split_wait(name, finish, sems, srcs, lands, after)


def sibling_start(grads, name):
    return _exchange_start(_sibling_copies, 4, 4, grads, name)


def sibling_wait(started, after, name):
    return _exchange_wait(_sibling_copies, started, after, name)


def chips_start(parts, name):
    return _exchange_start(_chip_copies, 3, 3, parts, name)


def chips_wait(started, after, name):
    return _exchange_wait(_chip_copies, started, after, name)


def _pack(arrays, rows):
    flat = jnp.concatenate([a.reshape(-1) for a in arrays])
    return jnp.pad(flat, (0, rows * LANES - flat.shape[0])).reshape(rows, LANES)


def _pack_stacked(arrays, rows):
    flat = jnp.concatenate([a.reshape(N_DEV, -1) for a in arrays], axis=1)
    return jnp.pad(flat, ((0, 0), (0, rows * LANES - flat.shape[1]))).reshape(N_DEV, rows, LANES)


def _unpack(buf, shapes, lead=()):
    flat = buf.reshape(lead + (-1,))
    out, off = [], 0
    for s in shapes:
        n = 1
        for d in s:
            n *= d
        out.append(flat[..., off:off + n].reshape(lead + tuple(s)))
        off += n
    return out


def _padded_rows(shapes, multiple):
    n = sum(functools.reduce(lambda a, b: a * b, s, 1) for s in shapes)
    rows = -(-n // LANES)
    return -(-rows // multiple) * multiple


def _to_full(stacked, axis):
    t = jnp.moveaxis(stacked, 0, axis)
    return t.reshape(t.shape[:axis] + (t.shape[axis] * t.shape[axis + 1],) + t.shape[axis + 2:])


def _to_stacked(full, axis):
    s = full.shape
    t = full.reshape(s[:axis] + (N_DEV, s[axis] // N_DEV) + s[axis + 1:])
    return jnp.moveaxis(t, axis, 0)


def _ffn_forward(x, h, w_up, cw, cb, w_down, next_g, tag):
    z = mm_in(h, w_up, f"ffn{tag}_up", stacked_out=True, out_dtype=BF16, rows=2048)
    nb, m, c = z.shape
    z4 = z.reshape(2, nb // 2, m, c)
    zc, act, *out = ffn_gate_down(z4, cw, cb, w_down, x, next_g, f"ffn{tag}_gate_down")
    return (out[0], out[1] if next_g is not None else None), (x, h, z4, zc, act)


def _ffn_backward(dx, dxb, saved, norm_g, w_up, cw, w_down, tag, after=()):
    x, h, z4, zc, act = saved
    nj = act.shape[0]
    dz4, dcw, dcb = ffn_gate_bwd(z4, zc, dxb, w_down, cw, f"ffn{tag}_gate_bwd", after)
    dw_down = mm_dw_out(act, dxb, f"ffn{tag}_down_dw")
    dz = dz4.reshape((2 * nj,) + dz4.shape[2:])
    dx, dxb, dg = mm_dx_in(dz, w_up, f"ffn{tag}_up_dx", norm=(x, norm_g, dx))
    dw_up = mm_dw_in(h, dz, 2 * nj, f"ffn{tag}_up_dw", transposed=True)
    return dx, dxb, dict(norm_g=dg, w_up=dw_up, conv_w=dcw, conv_b=dcb, w_down=dw_down)


def kernel(x, a_norm_g, a_w_in, a_b_in, a_v_norm_g, a_w_s, a_b_s, a_w_out, b_norm_g, b_w_in, b_w_grp, b_b_grp, b_scale, b_w_out, c_norm_g, c_w_in, c_b_in, c_conv_w, c_conv_b, c_w_a, c_b_a, c_w_i, c_b_i, c_lambda, c_w_out, d_norm_g, d_w_in, d_conv_w, d_w_out, ffn_norm_g, ffn_w_up, ffn_conv_w, ffn_conv_b, ffn_w_down, final_norm_g, loss_target, m_a_norm_g, m_a_w_in, m_a_b_in, m_a_v_norm_g, m_a_w_s, m_a_b_s, m_a_w_out, m_b_norm_g, m_b_w_in, m_b_w_grp, m_b_b_grp, m_b_scale, m_b_w_out, m_c_norm_g, m_c_w_in, m_c_b_in, m_c_conv_w, m_c_conv_b, m_c_w_a, m_c_b_a, m_c_w_i, m_c_b_i, m_c_lambda, m_c_w_out, m_d_norm_g, m_d_w_in, m_d_conv_w, m_d_w_out, m_ffn_norm_g, m_ffn_w_up, m_ffn_conv_w, m_ffn_conv_b, m_ffn_w_down, m_final_norm_g, v_a_norm_g, v_a_w_in, v_a_b_in, v_a_v_norm_g, v_a_w_s, v_a_b_s, v_a_w_out, v_b_norm_g, v_b_w_in, v_b_w_grp, v_b_b_grp, v_b_scale, v_b_w_out, v_c_norm_g, v_c_w_in, v_c_b_in, v_c_conv_w, v_c_conv_b, v_c_w_a, v_c_b_a, v_c_w_i, v_c_b_i, v_c_lambda, v_c_w_out, v_d_norm_g, v_d_w_in, v_d_conv_w, v_d_w_out, v_ffn_norm_g, v_ffn_w_up, v_ffn_conv_w, v_ffn_conv_b, v_ffn_w_down, v_final_norm_g):
    args = locals()
    w_loc = {n: args[n] for n in WEIGHTS}
    m_loc = {n: args["m_" + n] for n in WEIGHTS}
    v_loc = {n: args["v_" + n] for n in WEIGHTS}
    depth = ffn_w_up.shape[0]
    xs = x[0]
    target = loss_target[0]

    small_names = list(SMALL_SHARDED)
    small_shapes = [w_loc[n].shape for n in small_names]
    small_rows = _padded_rows(small_shapes, 8)
    small_packed = _pack([w_loc[n] for n in small_names], small_rows)
    early_names = ['ffn_conv_w', 'b_norm_g', 'c_norm_g', 'd_norm_g']
    late_names = [n for n in small_names if n not in early_names]
    packs = []
    for names in (early_names, late_names):
        shapes = [w_loc[n].shape for n in names]
        packs.append((names, shapes, _pack([w_loc[n] for n in names], _padded_rows(shapes, 8))))
    mixers = [(a_w_in, a_w_out), (b_w_in, b_w_out), (c_w_in, c_w_out), (d_w_in, d_w_out)]
    groups = {}
    for l in range(depth):
        groups[f'mixer{l}'] = [mixers[l][0][0].astype(BF16), mixers[l][1][0].astype(BF16)] + ([packs[1][2]] if l == 1 else [])
        groups[f'ffn{l}'] = [ffn_w_up[l].astype(BF16), ffn_w_down[l].astype(BF16)] + ([packs[0][2]] if l == 0 else [])
    two_level = ('mixer0', 'ffn0', 'mixer1', 'ffn1')
    in_flight = dict(zip(groups, gather_start(list(groups.values()),
                                              [SAME_CORE_PEERS if k in two_level else ALL_PEERS for k in groups], "gather_start")))
    forwarding = {}

    def forward_on(key, after):
        forwarding[key] = forward_start(gather_wait(in_flight[key], after, f"gather_wait_{key}"), f"forward_start_{key}")
        return (forwarding[key][2],)

    def gathered(key, after):
        if key in two_level:
            return forward_wait(forwarding[key], after, f"forward_wait_{key}")
        return gather_wait(in_flight[key], after, f"gather_wait_{key}")

    sm = {}

    def unpack_small(pack, gathered):
        names, shapes, _ = pack
        sm.update((n, _to_full(s, SMALL_SHARDED[n])) for n, s in zip(names, _unpack(gathered, shapes, (N_DEV,))))

    def rows_full(st):
        return st.reshape((st.shape[0] * st.shape[1],) + st.shape[2:])

    nb = N_DEV
    ffn_cb = [ffn_conv_b[l].reshape(2, nb // 2, 1, -1) for l in range(depth)]
    ffn_g = [ffn_norm_g[l:l + 1] for l in range(depth)]
    a_bst = a_b_s[0].T
    w_up, w_down, saved = [None] * depth, [None] * depth, {}

    def ffn_forward(xl, hl, l, next_g):
        up, down, *early = gathered(f'ffn{l}', (xl,))
        if early:
            unpack_small(packs[0], early[0])
            ffn_cw.extend(sm['ffn_conv_w'][k].reshape(ffn_conv_w.shape[1], 2, nb // 2, -1).transpose(1, 2, 0, 3)
                          for k in range(depth))
        w_up[l], w_down[l] = up, rows_full(down)
        return _ffn_forward(xl, hl, w_up[l], ffn_cw[l], ffn_cb[l], w_down[l], next_g(), l)

    ffn_cw = []
    h = rms_fwd(xs, a_norm_g, "a_norm")
    forward_on('mixer0', (h, small_packed))
    wa_in, wa_out = gathered('mixer0', (h,))
    wa_out = rows_full(wa_out)
    z = mm_in(h, wa_in, "a_in", bias=a_b_in)
    y = a_mid_fwd(z, a_v_norm_g, a_w_s[0], a_bst, "a_mid")
    x1, hf = mm_out(y, wa_out, xs, ffn_g[0], "a_out", forward_on('ffn0', (y,)))
    saved['a'] = (xs, h, z, y)
    forward_on('mixer1', (x1,))
    (x1, h), saved['f0'] = ffn_forward(x1, hf, 0, lambda: sm['b_norm_g'])

    forward_on('ffn1', (x1,))
    wb_in, wb_out, late = gathered('mixer1', (x1,))
    unpack_small(packs[1], late)
    b_wgrp, c_wa, c_wi = (sm[n][0].astype(BF16) for n in ('b_w_grp', 'c_w_a', 'c_w_i'))
    b_bgrp, c_ba, c_bi = (sm[n][0].reshape(1, -1) for n in ('b_b_grp', 'c_b_a', 'c_b_i'))
    wb_in, wb_out = rows_full(wb_in)[None], rows_full(wb_out)
    z = mm_in(h, wb_in, "b_in")
    y, x2, hf = b_mid_fwd(z, b_wgrp, b_bgrp, sm['b_scale'], wb_out, x1, ffn_g[1], "b_mid_out")
    saved['b'] = (x1, h, z, y)
    (x2, h), saved['f1'] = ffn_forward(x2, hf, 1, lambda: sm['c_norm_g'])

    wc_in, wc_out = gathered('mixer2', (x2,))
    wc_out = rows_full(wc_out)
    z = mm_in(h, wc_in, "c_in", bias=sm['c_b_in'])
    c_cw = sm['c_conv_w'][0]
    a_seq, b_seq, xr = c_mid_fwd(z, c_cw, sm['c_conv_b'], c_wa, c_ba, c_wi, c_bi, sm['c_lambda'], "c_mid")
    hs, y, x3, hf = c_scan_fwd(a_seq, b_seq, z, wc_out, x2, ffn_g[2], "c_scan_out")
    saved['c'] = (x2, h, z, y, a_seq, xr, hs)
    (x3, h), saved['f2'] = ffn_forward(x3, hf, 2, lambda: sm['d_norm_g'])

    wd_in, wd_out = gathered('mixer3', (x3,))
    wd_out = rows_full(wd_out)
    z = mm_in(h, wd_in, "d_in")
    d_cw = sm['d_conv_w'][0]
    y, x4, hf = d_mid_fwd(z, d_cw, wd_out, x3, ffn_g[3], "d_mid_out")
    saved['d'] = (x3, h, z, y)
    (x4, _), saved['f3'] = ffn_forward(x4, hf, 3, lambda: None)

    loss_part, dx, dxb, d_final_g = final_loss(x4, final_norm_g.reshape(1, -1), target, "final_loss")
    loss = lax.psum(loss_part[0, 0], ("x", "y", "c"))

    def rows_stacked(full):
        return full.reshape((N_DEV, full.shape[0] // N_DEV) + full.shape[1:])

    mx, my, mc = _place()
    own = jnp.stack([_dev_index(*_chip_of(mx, my, k), mc) for k in range(4)]).astype(jnp.int32)
    repl_shapes = [w_loc[n].shape for n in REPLICATED]
    repl_rows = _padded_rows(repl_shapes, 8 * N_DEV)
    shard_of = {n: (w_loc[n][0], m_loc[n][0], v_loc[n][0])
                for n in ('a_w_in', 'a_w_out', 'b_w_in', 'b_w_out', 'c_w_in', 'c_w_out', 'd_w_in', 'd_w_out')}
    shard_of['small'] = (small_packed, _pack([m_loc[n] for n in small_names], small_rows),
                         _pack([v_loc[n] for n in small_names], small_rows))
    updated = {}

    def finish(n, part, others):
        if n == 'repl':
            chunk = sum_parts(part, others, "rs_sum_repl")
            repl_g = all_gather([chunk], "gather_repl")[0].reshape(repl_rows, LANES)
            updated[n] = adamw(*(_pack([src[k] for k in REPLICATED], repl_rows) for src in (w_loc, m_loc, v_loc)),
                               "adamw_repl", g=repl_g)
        elif n.startswith('ffn_w'):
            base, l = n[:-1], int(n[-1])
            turn = (lambda t: jnp.swapaxes(t, 1, 2)) if base == 'ffn_w_up' else (lambda t: t)
            updated[base] = adamw_layer(turn(w_loc[base]), turn(m_loc[base]), turn(v_loc[base]), l, updated.get(base),
                                        f"adamw_{n}", part, others)
            return updated[base][1]
        else:
            updated[n] = adamw(*shard_of[n], f"adamw_{n}", p=part, l2=others)
        return updated[n][1]

    def light(arrays):
        return tuple(a for a in arrays if a.size <= 8 * LANES)

    stages = [None, []]

    def advance(tag, new, after, finish_older=True):
        behind = []
        first = None
        if new:
            first = ([n for n, _ in new], sibling_start([g for _, g in new], f"rs_sibling_start{tag}"))
            behind.append(first[1][3])
        older = stages[1]
        if finish_older:
            for k, (names, started) in enumerate(older):
                parts, others = chips_wait(started, after, f"rs_chips_wait{tag}_{k}")
                behind += [finish(n, p, o) for n, p, o in zip(names, parts, others)]
            older = []
        if stages[0] is not None:
            names, started = stages[0]
            grads, got = sibling_wait(started, after, f"rs_sibling_wait{tag}")
            parts = [add_pairs(g, l, own, f"rs_add_{n}") for n, g, l in zip(names, grads, got)]
            older = older + [(names, chips_start(parts, f"rs_chips_start{tag}"))]
            behind.append(older[-1][1][3])
        stages[:] = [first, older]
        return tuple(behind)

    gf = [None] * depth
    dx, dxb, gf[3] = _ffn_backward(dx, dxb, saved['f3'], ffn_g[3], w_up[3], ffn_cw[3], w_down[3], 3)
    xin, h, z, y = saved['d']
    dy = mm_dx_out(dxb, wd_out, "d_out_dx")
    g_d_w_out = mm_dw_out(y, dxb, "d_out_dw")
    dz, g_d_conv_w = d_mid_bwd(z, dy, d_cw, "d_mid_bwd")
    dx, dxb, g_d_norm_g = mm_dx_in(dz, wd_in, "d_in_dx", norm=(xin, sm['d_norm_g'], dx))
    g_d_w_in = mm_dw_in(h, dz, nb, "d_in_dw")
    behind = advance(0, [('ffn_w_up3', gf[3]['w_up']), ('ffn_w_down3', rows_stacked(gf[3]['w_down'])), ('d_w_in', g_d_w_in),
                         ('d_w_out', rows_stacked(g_d_w_out))], (g_d_norm_g,))

    dx, dxb, gf[2] = _ffn_backward(dx, dxb, saved['f2'], ffn_g[2], w_up[2], ffn_cw[2], w_down[2], 2, behind)
    xin, h, z, y, a_seq, xr, hs = saved['c']
    g_c_w_out = mm_dw_out(y, dxb, "c_out_dw")
    lam_seq, da_seq, dgate, dgate_sum = c_scan_bwd(dxb, wc_out, z, hs, a_seq, "c_scan_bwd")
    dxr, g_c_w_a, g_c_w_i, g_c_b_a, g_c_b_i, g_c_lambda = c_mid_bwd(
        lam_seq, da_seq, xr, c_wa, c_ba, c_wi, c_bi, sm['c_lambda'], "c_mid_bwd")
    dxr_pre, g_c_conv_w, g_c_conv_b, dxr_pre_sum = conv_bwd(dxr, z, 1, c_cw, "c_conv_bwd")
    dz = jnp.concatenate([dgate, dxr_pre], axis=1)
    g_c_b_in = jnp.concatenate([dgate_sum, dxr_pre_sum], axis=1)
    dx, dxb, g_c_norm_g = mm_dx_in(dz, wc_in, "c_in_dx", norm=(xin, sm['c_norm_g'], dx))
    g_c_w_in = mm_dw_in(h, dz, nb, "c_in_dw")
    behind = advance(1, [('ffn_w_up2', gf[2]['w_up']), ('ffn_w_down2', rows_stacked(gf[2]['w_down'])), ('c_w_in', g_c_w_in),
                         ('c_w_out', rows_stacked(g_c_w_out))], (g_c_norm_g,))

    dx, dxb, gf[1] = _ffn_backward(dx, dxb, saved['f1'], ffn_g[1], w_up[1], ffn_cw[1], w_down[1], 1, behind)
    xin, h, z, y = saved['b']
    g_b_w_out = mm_dw_out(y, dxb, "b_out_dw")
    dp, g_b_w_grp, g_b_b_grp, g_b_scale = b_mid_bwd(z, dxb, wb_out, b_wgrp, b_bgrp, sm['b_scale'], "b_mid_bwd")
    dz = b_pool_bwd(dp, "b_pool_bwd")
    dx, dxb, g_b_norm_g = mm_dx_in(dz, wb_in, "b_in_dx", norm=(xin, sm['b_norm_g'], dx))
    g_b_w_in = mm_dw_in(h, dz, 1, "b_in_dw")
    behind = advance(2, [('ffn_w_up1', gf[1]['w_up']), ('ffn_w_down1', rows_stacked(gf[1]['w_down'])),
                         ('b_w_in', rows_stacked(g_b_w_in[0])), ('b_w_out', rows_stacked(g_b_w_out))], (g_b_norm_g,))

    dx, dxb, gf[0] = _ffn_backward(dx, dxb, saved['f0'], ffn_g[0], w_up[0], ffn_cw[0], w_down[0], 0, behind)
    full_small = {
        'b_norm_g': g_b_norm_g, 'b_w_grp': g_b_w_grp[None], 'b_b_grp': g_b_b_grp.reshape(b_b_grp.shape[:2] + (-1,)),
        'b_scale': g_b_scale, 'c_norm_g': g_c_norm_g, 'c_b_in': g_c_b_in, 'c_conv_w': g_c_conv_w[None],
        'c_conv_b': g_c_conv_b, 'c_w_a': g_c_w_a[None], 'c_b_a': g_c_b_a.reshape(c_b_a.shape[:2] + (-1,)),
        'c_w_i': g_c_w_i[None], 'c_b_i': g_c_b_i.reshape(c_b_i.shape[:2] + (-1,)), 'c_lambda': g_c_lambda,
        'd_norm_g': g_d_norm_g, 'd_conv_w': g_d_conv_w[None],
        'ffn_conv_w': jnp.stack([gf[l]['conv_w'].transpose(2, 0, 1, 3).reshape(ffn_conv_w.shape[1], -1) for l in range(depth)])}
    small_grads = _pack_stacked([_to_stacked(full_small[n], SMALL_SHARDED[n]) for n in small_names], small_rows)
    behind = advance(3, [('ffn_w_up0', gf[0]['w_up']), ('ffn_w_down0', rows_stacked(gf[0]['w_down'])), ('small', small_grads)],
                     (gf[0]['norm_g'],))

    xin, h, z, y = saved['a']
    g_a_w_out = mm_dw_out(y, dxb, "a_out_dw")
    behind += advance(4, [('a_w_out', rows_stacked(g_a_w_out))], light(behind))
    dz, g_a_b_in, g_a_v_norm_g, g_a_w_s, g_a_b_s = a_mid_bwd(z, dxb, wa_out, a_v_norm_g, a_w_s[0], a_bst, "a_mid_bwd", behind)
    g_a_w_in = mm_dw_in(h, dz, nb, "a_in_dw")
    behind = advance(5, [('a_w_in', g_a_w_in)], (g_a_b_in,))
    behind = advance('5b', [], (g_a_b_in, *light(behind)), finish_older=False)
    dx, _, g_a_norm_g = mm_dx_in(dz, wa_in, "a_in_dx", behind, norm=(xin, a_norm_g, dx))
    grad_x = dx[None]

    tril = jnp.tril(jnp.ones((A_CHUNK, A_CHUNK), bool))
    repl_full = {
        'a_norm_g': g_a_norm_g, 'a_b_in': g_a_b_in, 'a_v_norm_g': g_a_v_norm_g,
        'a_w_s': jnp.where(tril, g_a_w_s, 0.0)[None], 'a_b_s': g_a_b_s[:, ::LANES].T[None],
        'ffn_norm_g': jnp.concatenate([gf[l]['norm_g'] for l in range(depth)], axis=0),
        'ffn_conv_b': jnp.stack([gf[l]['conv_b'].reshape(-1) for l in range(depth)]), 'final_norm_g': d_final_g.reshape(-1)}
    repl_grads = _pack([repl_full[n] for n in REPLICATED], repl_rows).reshape(N_DEV, repl_rows // N_DEV, LANES)
    behind = advance(6, [('repl', repl_grads)], (g_a_norm_g,))
    behind = advance(7, [], (g_a_norm_g, *light(behind)))
    advance(8, [], (g_a_norm_g, *light(behind)))

    outs = [{}, {}, {}, {}]
    for i, dst in enumerate(outs):
        for n in ('a_w_in', 'a_w_out', 'b_w_in', 'b_w_out', 'c_w_in', 'c_w_out', 'd_w_in', 'd_w_out'):
            dst[n] = updated[n][i][None]
        dst['ffn_w_up'] = jnp.swapaxes(updated['ffn_w_up'][i], 1, 2)
        dst['ffn_w_down'] = updated['ffn_w_down'][i]
        dst.update(zip(small_names, _unpack(updated['small'][i], small_shapes)))
        dst.update(zip(REPLICATED, _unpack(updated['repl'][i], repl_shapes)))
    out_g, out_d, out_m, out_v = outs

    return (loss, grad_x, *[out_g[n] for n in WEIGHTS], *[out_d[n] for n in WEIGHTS], *[out_m[n] for n in WEIGHTS],
            *[out_v[n] for n in WEIGHTS])
```

```python
import functools

import jax
import jax.numpy as jnp
from jax import lax
from jax.experimental import pallas as pl
from jax.experimental.pallas import tpu as pltpu

F32, BF16 = jnp.float32, jnp.bfloat16
MESH_ID = pl.DeviceIdType.MESH
N_DEV = 8
V7X_VMEM_LIMIT_BYTES = 56 << 20
LANES = 128
HALO = 8
POOL_HALO = 16

EPS = 1e-6
A_CHUNK, A_GROUPS = 128, 4
B_WINDOWS = (2, 4, 8, 16)
C_GATE_C = 8.0
ADAM_LR, ADAM_B1, ADAM_B2, ADAM_EPS, ADAM_WD, ADAM_STEP = 0.001, 0.9, 0.999, 1e-08, 0.01, 10

WEIGHTS = ['a_norm_g', 'a_w_in', 'a_b_in', 'a_v_norm_g', 'a_w_s', 'a_b_s', 'a_w_out', 'b_norm_g', 'b_w_in', 'b_w_grp',
           'b_b_grp', 'b_scale', 'b_w_out', 'c_norm_g', 'c_w_in', 'c_b_in', 'c_conv_w', 'c_conv_b', 'c_w_a', 'c_b_a',
           'c_w_i', 'c_b_i', 'c_lambda', 'c_w_out', 'd_norm_g', 'd_w_in', 'd_conv_w', 'd_w_out', 'ffn_norm_g',
           'ffn_w_up', 'ffn_conv_w', 'ffn_conv_b', 'ffn_w_down', 'final_norm_g']
SMALL_SHARDED = {'b_norm_g': 1, 'b_w_grp': 2, 'b_b_grp': 2, 'b_scale': 1, 'c_norm_g': 1, 'c_b_in': 1, 'c_conv_w': 2,
                 'c_conv_b': 1, 'c_w_a': 2, 'c_b_a': 2, 'c_w_i': 2, 'c_b_i': 2, 'c_lambda': 1, 'd_norm_g': 1,
                 'd_conv_w': 2, 'ffn_conv_w': 2}
REPLICATED = ['a_norm_g', 'a_b_in', 'a_v_norm_g', 'a_w_s', 'a_b_s', 'ffn_norm_g', 'ffn_conv_b', 'final_norm_g']


_GELU_C0, _GELU_C1 = 0.7978845608028654, 0.044715


def _gelu(x):
    return 0.5 * x * (1.0 + jnp.tanh(_GELU_C0 * (x + _GELU_C1 * (x * x * x))))


def _gelu_grad(x):
    t = jnp.tanh(_GELU_C0 * (x + _GELU_C1 * (x * x * x)))
    return 0.5 * (1.0 + t) + 0.5 * x * (1.0 - t * t) * (_GELU_C0 * (1.0 + 3.0 * _GELU_C1 * (x * x)))


def _sigmoid(x):
    return jax.nn.sigmoid(x)


def _log1p(x):
    u = 1.0 + x
    return jnp.where(u == 1.0, x, jnp.log(u) * (x / (u - 1.0)))


def _softplus(x):
    return jnp.maximum(x, 0.0) + _log1p(jnp.exp(-jnp.abs(x)))


def _expm1(x):
    poly = x * (1.0 + x * (1 / 2) * (1.0 + x * (1 / 3) * (1.0 + x * (1 / 4) * (1.0 + x * (1 / 5) * (
        1.0 + x * (1 / 6) * (1.0 + x * (1 / 7) * (1.0 + x * (1 / 8))))))))
    return jnp.where(jnp.abs(x) < 0.35, poly, jnp.exp(x) - 1.0)


def _down(xe, s):
    return xe if s == 0 else pltpu.roll(xe, s, 0)


def _up(xe, s):
    return xe if s == 0 else pltpu.roll(xe, xe.shape[0] - s, 0)


def _conv_ext(xe, w, taps):
    y = xe * w[taps - 1:taps]
    for s in range(1, taps):
        y = y + _down(xe, s) * w[taps - 1 - s:taps - s]
    return y


def _acc(ref, val, first):
    @pl.when(first)
    def _():
        ref[...] = val

    @pl.when(jnp.logical_not(first))
    def _():
        ref[...] += val


def _colsum(v):
    return jnp.sum(v, axis=0, keepdims=True)


def _dot(a, b, dims=((1,), (0,))):
    return lax.dot_general(a.astype(BF16), b.astype(BF16), (dims, ((), ())), preferred_element_type=F32)


_NN, _NT, _TN = ((1,), (0,)), ((1,), (1,)), ((0,), (0,))


def _call(body, name, grid, in_specs, out_specs, out_shape, scratch=(), after=()):
    n_in, n_after = len(in_specs), len(after)

    def ordered_body(*refs):
        return body(*refs[:n_in], *refs[n_in + n_after:])

    call = pl.pallas_call(
        ordered_body if n_after else body, name=name, grid=grid,
        in_specs=list(in_specs) + [pl.BlockSpec(memory_space=pl.ANY)] * n_after, out_specs=out_specs,
        out_shape=out_shape, scratch_shapes=list(scratch),
        compiler_params=pltpu.CompilerParams(dimension_semantics=("arbitrary",) * len(grid),
                                             vmem_limit_bytes=V7X_VMEM_LIMIT_BYTES))
    return lambda *args: call(*args, *after)


def _rows(m, t):
    t = min(m, t)
    assert m % t == 0, (m, t)
    return t


def _sds(shape, dtype=F32):
    return jax.ShapeDtypeStruct(tuple(shape), dtype)


def _prev_halo(tm, halo=HALO):
    return lambda i: jnp.maximum(i * (tm // halo) - 1, 0)


def _next_halo(tm, m, halo=HALO):
    return lambda i: jnp.minimum((i + 1) * (tm // halo), m // halo - 1)


def _matmul(name, ins, in_specs, out_shape, o_spec, grid, compute, after=()):
    def body(*refs):
        refs[-1][...] = compute(*refs[:-1]).astype(refs[-1].dtype)

    return _call(body, name, grid, in_specs, o_spec, out_shape, (), after)(*ins)


def mm_in(h, w_st, name, bias=None, stacked_out=False, out_dtype=F32, rows=1024, after=()):
    m, k = h.shape
    nb, _, n = w_st.shape
    tm = _rows(m, rows)
    in_specs = [pl.BlockSpec((tm, k), lambda i, j: (i, 0)), pl.BlockSpec((None, k, n), lambda i, j: (j, 0, 0))]
    if stacked_out:
        out, o_spec = _sds((nb, m, n), out_dtype), pl.BlockSpec((None, tm, n), lambda i, j: (j, i, 0))
    else:
        out, o_spec = _sds((m, nb * n), out_dtype), pl.BlockSpec((tm, n), lambda i, j: (i, j))
    if bias is None:
        return _matmul(name, (h, w_st), in_specs, out, o_spec, (m // tm, nb), lambda a, b: _dot(a[...], b[...]), after)
    in_specs.append(pl.BlockSpec((1, n), lambda i, j: (0, j)))
    return _matmul(name, (h, w_st, bias), in_specs, out, o_spec, (m // tm, nb),
                   lambda a, b, c: _dot(a[...], b[...]) + c[...], after)


def _split_rows(kf):
    g = max(1, kf // 1024)
    return g, kf // g


def _resident(shape):
    return pl.BlockSpec(shape, lambda *_: (0,) * len(shape), pipeline_mode=pl.Buffered(1))


def _norm_rows(xv, g):
    return (xv * lax.rsqrt(jnp.mean(xv * xv, axis=-1, keepdims=True) + EPS) * g).astype(BF16)


def _project_out(y_ref, w_ref, res_ref, g_ref, x_ref, h_ref):
    xv = res_ref[...] + _dot(y_ref[...], w_ref[...])
    x_ref[...] = xv
    h_ref[...] = _norm_rows(xv, g_ref[...])


def _call_projected(body, name, m, tm, in_specs, ins, own_outs, w_out, res, next_g, scratch=()):
    n = w_out.shape[1]
    row = lambda width: pl.BlockSpec((tm, width), lambda i: (i, 0))
    specs = list(in_specs) + [_resident(w_out.shape), row(n), pl.BlockSpec((1, n), lambda i: (0, 0))]
    out_specs = [row(width) for width, _ in own_outs] + [row(n), row(n)]
    out_shape = [_sds((m, width), dtype) for width, dtype in own_outs] + [_sds((m, n)), _sds((m, n), BF16)]
    return _call(body, name, (m // tm,), specs, out_specs, out_shape, scratch)(*ins, w_out, res, next_g)


def mm_out(y, w, res, next_g, name, after=()):
    kf, n = w.shape
    m = y.shape[0]
    tm = _rows(m, 512)
    row = pl.BlockSpec((tm, n), lambda i: (i, 0))

    def body(y_ref, w_ref, res_ref, g_ref, x_ref, h_ref):
        xv = res_ref[...] + _dot(y_ref[...], w_ref[...])
        x_ref[...] = xv
        h_ref[...] = _norm_rows(xv, g_ref[...])

    in_specs = [pl.BlockSpec((tm, kf), lambda i: (i, 0)), _resident((kf, n)), row, pl.BlockSpec((1, n), lambda i: (0, 0))]
    return _call(body, name, (m // tm,), in_specs, [row, row], [_sds((m, n)), _sds((m, n), BF16)], (), after)(y, w, res, next_g)


def mm_dx_in(dz, w_st, name, after=(), norm=None):
    nb, k, n = w_st.shape
    m = dz.shape[-2]
    tm = _rows(m, 512)
    w_spec = _resident((nb, k, n))
    if dz.ndim == 3:
        in_specs = [pl.BlockSpec((None, tm, n), lambda i, r=r: (r, i, 0)) for r in range(nb)] + [w_spec]

        def compute(*refs):
            acc = _dot(refs[0][...], refs[nb][0], _NT)
            for r in range(1, nb):
                acc = acc + _dot(refs[r][...], refs[nb][r], _NT)
            return acc

        ins = (*[dz] * nb, w_st)
    else:
        in_specs = [pl.BlockSpec((tm, nb * n), lambda i: (i, 0)), w_spec]

        def compute(dz_ref, w_ref):
            acc = _dot(dz_ref[:, :n], w_ref[0], _NT)
            for r in range(1, nb):
                acc = acc + _dot(dz_ref[:, r * n:(r + 1) * n], w_ref[r], _NT)
            return acc

        ins = (dz, w_st)
    row = pl.BlockSpec((tm, k), lambda i: (i, 0))
    if norm is None:
        return _matmul(name, ins, in_specs, _sds((m, k)), row, (m // tm,), compute, after)
    n_in = len(in_specs)
    vec = pl.BlockSpec((1, k), lambda i: (0, 0))

    def body(*refs):
        x_ref, g_ref, dr_ref, dx_ref, dxb_ref, dg_ref = refs[n_in:]
        dx, dg = _rms_bwd_math(x_ref[...], g_ref[...], compute(*refs[:n_in]))
        dx = dr_ref[...] + dx
        dx_ref[...] = dx
        dxb_ref[...] = dx.astype(BF16)
        _acc(dg_ref, dg, pl.program_id(0) == 0)

    return _call(body, name, (m // tm,), in_specs + [row, vec, row], [row, row, vec],
                 [_sds((m, k)), _sds((m, k), BF16), _sds((1, k))], (), after)(*ins, *norm)


def mm_dx_out(dout, w, name, groups=None, after=()):
    kf, n = w.shape
    m = dout.shape[0]
    tm = _rows(m, 1024)
    g, k = (groups, kf // groups) if groups else _split_rows(kf)
    in_specs = [pl.BlockSpec((tm, n), lambda i, j: (i, 0)), pl.BlockSpec((None, k, n), lambda i, j: (j, 0, 0))]
    if groups:
        out, o_spec = _sds((g, m, k)), pl.BlockSpec((None, tm, k), lambda i, j: (j, i, 0))
    else:
        out, o_spec = _sds((m, kf)), pl.BlockSpec((tm, k), lambda i, j: (i, j))
    return _matmul(name, (dout, w.reshape(g, k, n)), in_specs, out, o_spec, (m // tm, g),
                   lambda a, b: _dot(a[...], b[...], _NT), after)


def mm_dw_in(h, dz, nb, name, transposed=False):
    m, k = h.shape
    if dz.ndim == 3:
        n = dz.shape[2]
        dz_spec = pl.BlockSpec((None, m, n), lambda j: (j, 0, 0))
    else:
        n = dz.shape[1] // nb
        dz_spec = pl.BlockSpec((m, n), lambda j: (0, j))
    in_specs = [_resident((m, k)), dz_spec]
    if transposed:
        return _matmul(name, (h, dz), in_specs, _sds((nb, n, k), BF16), pl.BlockSpec((None, n, k), lambda j: (j, 0, 0)), (nb,),
                       lambda a, b: _dot(b[...], a[...], _TN))
    return _matmul(name, (h, dz), in_specs, _sds((nb, k, n), BF16), pl.BlockSpec((None, k, n), lambda j: (j, 0, 0)), (nb,),
                   lambda a, b: _dot(a[...], b[...], _TN))


def mm_dw_out(y, dout, name):
    m, n = dout.shape
    if y.ndim == 3:
        g, _, k = y.shape
        y_spec = pl.BlockSpec((None, m, k), lambda j: (j, 0, 0))
    else:
        g, k = _split_rows(y.shape[1])
        y_spec = pl.BlockSpec((m, k), lambda j: (0, j))
    in_specs = [y_spec, _resident((m, n))]
    out = _matmul(name, (y, dout), in_specs, _sds((g, k, n), BF16), pl.BlockSpec((None, k, n), lambda j: (j, 0, 0)), (g,),
                  lambda a, b: _dot(a[...], b[...], _TN))
    return out.reshape(g * k, n)


def rms_fwd(x, g, name):
    m, d = x.shape
    tm = _rows(m, 512)

    def body(x_ref, g_ref, o_ref):
        xv = x_ref[...]
        rstd = lax.rsqrt(jnp.mean(xv * xv, axis=-1, keepdims=True) + EPS)
        o_ref[...] = (xv * rstd * g_ref[...]).astype(BF16)

    row = pl.BlockSpec((tm, d), lambda i: (i, 0))
    vec = pl.BlockSpec((1, d), lambda i: (0, 0))
    return _call(body, name, (m // tm,), [row, vec], row, _sds((m, d), BF16))(x, g)


def _rms_bwd_math(xv, g, dh):
    rstd = lax.rsqrt(jnp.mean(xv * xv, axis=-1, keepdims=True) + EPS)
    xhat = xv * rstd
    dxhat = dh * g
    dx = rstd * (dxhat - xhat * jnp.mean(dxhat * xhat, axis=-1, keepdims=True))
    return dx, _colsum(dh * xhat)


def final_loss(x, g, target, name):
    m, d = x.shape
    tm = _rows(m, 512)

    def body(x_ref, g_ref, t_ref, l_ref, dx_ref, dxb_ref, dg_ref):
        xv, gv = x_ref[...], g_ref[...]
        rstd = lax.rsqrt(jnp.mean(xv * xv, axis=-1, keepdims=True) + EPS)
        err = xv * rstd * gv - t_ref[...]
        part = 0.5 * jnp.sum(jnp.mean(err * err, axis=-1, keepdims=True), axis=0, keepdims=True)
        dx, dg = _rms_bwd_math(xv, gv, err * (1.0 / d))
        dx_ref[...] = dx
        dxb_ref[...] = dx.astype(BF16)
        first = pl.program_id(0) == 0
        _acc(l_ref, jnp.broadcast_to(part, l_ref.shape), first)
        _acc(dg_ref, dg, first)

    row = pl.BlockSpec((tm, d), lambda i: (i, 0))
    vec = pl.BlockSpec((1, d), lambda i: (0, 0))
    lsp = pl.BlockSpec((1, LANES), lambda i: (0, 0))
    return _call(body, name, (m // tm,), [row, vec, row], [lsp, row, row, vec],
                 [_sds((1, LANES)), _sds((m, d)), _sds((m, d), BF16), _sds((1, d))])(x, g, target)


def _a_common(z_ref, vg_ref, ws_ref, bst_ref, tm, width):
    gw = width // A_GROUPS
    zp = z_ref[...]
    z = _gelu(zp)
    u, v = z[:, :width], z[:, width:]
    rstd = lax.rsqrt(jnp.mean(v * v, axis=-1, keepdims=True) + EPS)
    vhat = v * rstd
    vn = vhat * vg_ref[...]
    t_i = lax.broadcasted_iota(jnp.int32, (A_CHUNK, A_CHUNK), 0)
    s_i = lax.broadcasted_iota(jnp.int32, (A_CHUNK, A_CHUNK), 1)
    wsm = [jnp.where(s_i <= t_i, ws_ref[g], 0.0).astype(BF16) for g in range(A_GROUPS)]
    bst = bst_ref[...]
    return zp, u, rstd, vhat, vn.astype(BF16), wsm, bst, gw


def a_mid_fwd(z, vg, ws, bst, name):
    m, w2 = z.shape
    width = w2 // 2
    tm = _rows(m, 256)

    def body(z_ref, vg_ref, ws_ref, bst_ref, y_ref):
        _, u, _, _, vnb, wsm, bst, gw = _a_common(z_ref, vg_ref, ws_ref, bst_ref, tm, width)
        for c in range(tm // A_CHUNK):
            r0 = c * A_CHUNK
            for g in range(A_GROUPS):
                c0 = g * gw
                vs = _dot(wsm[g], vnb[r0:r0 + A_CHUNK, c0:c0 + gw]) + bst[:, g:g + 1]
                y_ref[r0:r0 + A_CHUNK, c0:c0 + gw] = (u[r0:r0 + A_CHUNK, c0:c0 + gw] * vs).astype(BF16)

    in_specs = [pl.BlockSpec((tm, w2), lambda i: (i, 0)), pl.BlockSpec((1, width), lambda i: (0, 0)),
                pl.BlockSpec((A_GROUPS, A_CHUNK, A_CHUNK), lambda i: (0, 0, 0)),
                pl.BlockSpec((A_CHUNK, A_GROUPS), lambda i: (0, 0))]
    return _call(body, name, (m // tm,), in_specs, pl.BlockSpec((tm, width), lambda i: (i, 0)),
                 _sds((m, width), BF16))(z, vg, ws, bst)


def a_mid_bwd(z, dx, w_out, vg, ws, bst, name, after=()):
    m, w2 = z.shape
    width = w2 // 2
    tm = _rows(m, 256)

    def body(z_ref, dx_ref, wo_ref, vg_ref, ws_ref, bst_ref, dz_ref, dbin_ref, dvg_ref, dws_ref, dbs_ref, dvn_scr, du_scr):
        first = pl.program_id(0) == 0
        zp, u, rstd, vhat, vnb, wsm, bst, gw = _a_common(z_ref, vg_ref, ws_ref, bst_ref, tm, width)
        dy = _dot(dx_ref[...], wo_ref[...], _NT)
        dws = [jnp.zeros((A_CHUNK, A_CHUNK), F32) for _ in range(A_GROUPS)]
        dbs = [jnp.zeros((A_CHUNK, 1), F32) for _ in range(A_GROUPS)]
        for c in range(tm // A_CHUNK):
            r0 = c * A_CHUNK
            for g in range(A_GROUPS):
                c0 = g * gw
                vn_cg = vnb[r0:r0 + A_CHUNK, c0:c0 + gw]
                vs = _dot(wsm[g], vn_cg) + bst[:, g:g + 1]
                dy_cg = dy[r0:r0 + A_CHUNK, c0:c0 + gw]
                dvs = dy_cg * u[r0:r0 + A_CHUNK, c0:c0 + gw]
                du_scr[r0:r0 + A_CHUNK, c0:c0 + gw] = dy_cg * vs
                dws[g] = dws[g] + _dot(dvs, vn_cg, _NT)
                dbs[g] = dbs[g] + jnp.sum(dvs, axis=1, keepdims=True)
                dvn_scr[r0:r0 + A_CHUNK, c0:c0 + gw] = _dot(wsm[g], dvs, _TN)
        for g in range(A_GROUPS):
            _acc(dws_ref.at[g], dws[g], first)
            _acc(dbs_ref.at[:, g * LANES:(g + 1) * LANES], jnp.broadcast_to(dbs[g], (A_CHUNK, LANES)), first)
        dvn = dvn_scr[...]
        _acc(dvg_ref, _colsum(dvn * vhat), first)
        dvhat = dvn * vg_ref[...]
        dv = rstd * (dvhat - vhat * jnp.mean(dvhat * vhat, axis=-1, keepdims=True))
        gg = _gelu_grad(zp)
        dzu = du_scr[...] * gg[:, :width]
        dzv = dv * gg[:, width:]
        dz_ref[:, :width] = dzu.astype(BF16)
        dz_ref[:, width:] = dzv.astype(BF16)
        _acc(dbin_ref.at[:, :width], _colsum(dzu), first)
        _acc(dbin_ref.at[:, width:], _colsum(dzv), first)

    const2 = lambda i: (0, 0)
    in_specs = [pl.BlockSpec((tm, w2), lambda i: (i, 0)), pl.BlockSpec((tm, dx.shape[1]), lambda i: (i, 0)),
                _resident(w_out.shape), pl.BlockSpec((1, width), const2),
                pl.BlockSpec((A_GROUPS, A_CHUNK, A_CHUNK), lambda i: (0, 0, 0)), pl.BlockSpec((A_CHUNK, A_GROUPS), const2)]
    out_specs = [pl.BlockSpec((tm, w2), lambda i: (i, 0)), pl.BlockSpec((1, w2), const2), pl.BlockSpec((1, width), const2),
                 pl.BlockSpec((A_GROUPS, A_CHUNK, A_CHUNK), lambda i: (0, 0, 0)),
                 pl.BlockSpec((A_CHUNK, A_GROUPS * LANES), const2)]
    out_shape = [_sds((m, w2), BF16), _sds((1, w2)), _sds((1, width)), _sds((A_GROUPS, A_CHUNK, A_CHUNK)),
                 _sds((A_CHUNK, A_GROUPS * LANES))]
    scratch = [pltpu.VMEM((tm, width), F32), pltpu.VMEM((tm, width), F32)]
    return _call(body, name, (m // tm,), in_specs, out_specs, out_shape, scratch, after)(z, dx, w_out, vg, ws, bst)


def _pool_minus_id(ze, i, tm, gw):
    pos = i * tm + lax.broadcasted_iota(jnp.int32, (tm, 1), 0)
    out = []
    for gi, win in enumerate(B_WINDOWS):
        s = ze[:, gi * gw:(gi + 1) * gw]
        step = 1
        while step < win:
            s = s + _down(s, step)
            step *= 2
        inv = 1.0 / jnp.minimum(pos + 1, win).astype(F32)
        out.append(s[POOL_HALO:] * inv - ze[POOL_HALO:, gi * gw:(gi + 1) * gw])
    return out


def _b_specs(tm, width):
    return [pl.BlockSpec((POOL_HALO, width), lambda i: (_prev_halo(tm, POOL_HALO)(i), 0)),
            pl.BlockSpec((tm, width), lambda i: (i, 0))]


def b_mid_fwd(z, wgrp, bgrp, scale, w_out, res, next_g, name):
    m, width = z.shape
    ng = len(B_WINDOWS)
    gw = width // ng
    tm = _rows(m, 512)

    def body(zp_ref, zm_ref, w_ref, b_ref, s_ref, *proj):
        i = pl.program_id(0)
        y_ref = proj[3]
        ze = jnp.concatenate([zp_ref[...] * (i > 0).astype(F32), zm_ref[...]], axis=0)
        p = _pool_minus_id(ze, i, tm, gw)
        for g in range(ng):
            cs = slice(g * gw, (g + 1) * gw)
            y = (_dot(p[g], w_ref[g]) + b_ref[:, cs]) * s_ref[:, cs]
            y_ref[:, cs] = y.astype(BF16)
        _project_out(y_ref, *proj[:3], *proj[4:])

    vec = pl.BlockSpec((1, width), lambda i: (0, 0))
    in_specs = _b_specs(tm, width) + [pl.BlockSpec((ng, gw, gw), lambda i: (0, 0, 0)), vec, vec]
    return _call_projected(body, name, m, tm, in_specs, (z, z, wgrp, bgrp, scale), [(width, BF16)], w_out, res, next_g)


def b_mid_bwd(z, dx, w_out, wgrp, bgrp, scale, name):
    m, width = z.shape
    ng = len(B_WINDOWS)
    gw = width // ng
    tm = _rows(m, 512)

    def body(zp_ref, zm_ref, dx_ref, wo_ref, w_ref, b_ref, s_ref, dp_ref, dw_ref, db_ref, ds_ref):
        i = pl.program_id(0)
        first = i == 0
        ze = jnp.concatenate([zp_ref[...] * (i > 0).astype(F32), zm_ref[...]], axis=0)
        p = _pool_minus_id(ze, i, tm, gw)
        dy = _dot(dx_ref[...], wo_ref[...], _NT)
        for g in range(ng):
            cs = slice(g * gw, (g + 1) * gw)
            dyg = dy[:, cs]
            ypre = _dot(p[g], w_ref[g]) + b_ref[:, cs]
            dyp = dyg * s_ref[:, cs]
            _acc(ds_ref.at[:, cs], _colsum(dyg * ypre), first)
            _acc(db_ref.at[:, cs], _colsum(dyp), first)
            _acc(dw_ref.at[g], _dot(p[g], dyp, _TN), first)
            dp_ref[:, cs] = _dot(dyp, w_ref[g], _NT)

    vec = pl.BlockSpec((1, width), lambda i: (0, 0))
    row = pl.BlockSpec((tm, width), lambda i: (i, 0))
    wsp = pl.BlockSpec((ng, gw, gw), lambda i: (0, 0, 0))
    in_specs = _b_specs(tm, width) + [pl.BlockSpec((tm, dx.shape[1]), lambda i: (i, 0)), _resident(w_out.shape), wsp, vec, vec]
    return _call(body, name, (m // tm,), in_specs, [row, wsp, vec, vec],
                 [_sds((m, width)), _sds((ng, gw, gw)), _sds((1, width)), _sds((1, width))])(z, z, dx, w_out, wgrp, bgrp, scale)


def b_pool_bwd(dp, name):
    m, width = dp.shape
    gw = width // len(B_WINDOWS)
    tm = _rows(m, 512)
    n_i = m // tm

    def body(dm_ref, dn_ref, dz_ref):
        i = pl.program_id(0)
        de = jnp.concatenate([dm_ref[...], dn_ref[...] * (i < n_i - 1).astype(F32)], axis=0)
        pos = i * tm + lax.broadcasted_iota(jnp.int32, (tm + POOL_HALO, 1), 0)
        for gi, win in enumerate(B_WINDOWS):
            cs = slice(gi * gw, (gi + 1) * gw)
            d = de[:, cs]
            s = d * (1.0 / jnp.minimum(pos + 1, win).astype(F32))
            step = 1
            while step < win:
                s = s + _up(s, step)
                step *= 2
            dz_ref[:, cs] = (s[:tm] - d[:tm]).astype(BF16)

    in_specs = [pl.BlockSpec((tm, width), lambda i: (i, 0)),
                pl.BlockSpec((POOL_HALO, width), lambda i: (_next_halo(tm, m, POOL_HALO)(i), 0))]
    return _call(body, name, (n_i,), in_specs, pl.BlockSpec((tm, width), lambda i: (i, 0)), _sds((m, width), BF16))(dp, dp)


def _c_gates(xr, wa_ref, ba_ref, wi_ref, bi_ref, lam_ref, heads, hw):
    xb = xr.astype(BF16)
    ra = jnp.concatenate([_dot(xb[:, h * hw:(h + 1) * hw], wa_ref[h]) for h in range(heads)], axis=1) + ba_ref[...]
    ia = jnp.concatenate([_dot(xb[:, h * hw:(h + 1) * hw], wi_ref[h]) for h in range(heads)], axis=1) + bi_ref[...]
    r, ig = _sigmoid(ra), _sigmoid(ia)
    sp = _softplus(-lam_ref[...])
    log_a = (-C_GATE_C * r) * sp
    a = jnp.exp(log_a)
    mult = jnp.sqrt(-_expm1(2.0 * log_a))
    return xb, r, ig, sp, a, mult


def c_mid_fwd(z, cw, cb, wa, ba, wi, bi, lam, name):
    m, w2 = z.shape
    width = w2 // 2
    heads, hw = wa.shape[0], wa.shape[1]
    taps = cw.shape[0]
    tm = _rows(m, 512)

    def body(zp_ref, zm_ref, cw_ref, cb_ref, wa_ref, ba_ref, wi_ref, bi_ref, lam_ref, a_ref, b_ref, xr_ref):
        i = pl.program_id(0)
        xe = jnp.concatenate([zp_ref[...] * (i > 0).astype(F32), zm_ref[...]], axis=0)
        xr = _conv_ext(xe, cw_ref[...], taps)[HALO:] + cb_ref[...]
        _, _, ig, _, a, mult = _c_gates(xr, wa_ref, ba_ref, wi_ref, bi_ref, lam_ref, heads, hw)
        a_ref[...] = a
        b_ref[...] = mult * (ig * xr)
        xr_ref[...] = xr

    vec = pl.BlockSpec((1, width), lambda i: (0, 0))
    row = pl.BlockSpec((tm, width), lambda i: (i, 0))
    wsp = pl.BlockSpec((heads, hw, hw), lambda i: (0, 0, 0))
    in_specs = [pl.BlockSpec((HALO, width), lambda i: (_prev_halo(tm)(i), 1)), pl.BlockSpec((tm, width), lambda i: (i, 1)),
                pl.BlockSpec((taps, width), lambda i: (0, 0)), vec, wsp, vec, wsp, vec, vec]
    return _call(body, name, (m // tm,), in_specs, [row, row, row], [_sds((m, width))] * 3)(
        z, z, cw, cb, wa, ba, wi, bi, lam)


_SCAN_ROWS = 512


def c_scan_fwd(a, b, z, w_out, res, next_g, name):
    m, width = a.shape
    tm = _rows(m, _SCAN_ROWS)

    def body(a_ref, b_ref, g_ref, wo_ref, res_ref, ng_ref, hs_ref, y_ref, x_ref, h_ref, h_carry):
        @pl.when(pl.program_id(0) == 0)
        def _():
            h_carry[...] = jnp.zeros_like(h_carry)

        def step(t, h):
            h = a_ref[pl.ds(t, 1), :] * h + b_ref[pl.ds(t, 1), :]
            hs_ref[pl.ds(t, 1), :] = h
            return h

        h_carry[...] = lax.fori_loop(0, tm, step, h_carry[...], unroll=8)
        y_ref[...] = (hs_ref[...] * _gelu(g_ref[...])).astype(BF16)
        _project_out(y_ref, wo_ref, res_ref, ng_ref, x_ref, h_ref)

    row = pl.BlockSpec((tm, width), lambda i: (i, 0))
    return _call_projected(body, name, m, tm, [row, row, row], (a, b, z), [(width, F32), (width, BF16)], w_out, res, next_g,
                           [pltpu.VMEM((1, width), F32)])


def c_scan_bwd(dx, w_out, z, hs, a, name):
    m, width = a.shape
    tm = _rows(m, _SCAN_ROWS)
    n_i = m // tm

    def body(dx_ref, wo_ref, g_ref, hs_ref, hp_ref, a_ref, lam_ref, da_ref, dg_ref, dgs_ref, lam_carry, a_carry):
        i = pl.program_id(0)
        first = i == 0

        @pl.when(first)
        def _():
            lam_carry[...] = jnp.zeros_like(lam_carry)
            a_carry[...] = jnp.zeros_like(a_carry)

        gp, dyv, hsv = g_ref[...], _dot(dx_ref[...], wo_ref[...], _NT), hs_ref[...]
        dgate = dyv * hsv * _gelu_grad(gp)
        dg_ref[...] = dgate.astype(BF16)
        _acc(dgs_ref, _colsum(dgate), first)
        lam_ref[...] = dyv * _gelu(gp)

        def step(k, carry):
            lam_next, a_next = carry
            t = tm - 1 - k
            lam_t = lam_ref[pl.ds(t, 1), :] + a_next * lam_next
            lam_ref[pl.ds(t, 1), :] = lam_t
            return lam_t, a_ref[pl.ds(t, 1), :]

        lam_c, a_c = lax.fori_loop(0, tm, step, (lam_carry[...], a_carry[...]), unroll=8)
        lam_carry[...] = lam_c
        a_carry[...] = a_c
        h_before = hp_ref[HALO - 1:HALO, :] * (i < n_i - 1).astype(F32)
        t_i = lax.broadcasted_iota(jnp.int32, (tm, 1), 0)
        da_ref[...] = lam_ref[...] * jnp.where(t_i == 0, h_before, _down(hsv, 1))

    row = pl.BlockSpec((tm, width), lambda i: (n_i - 1 - i, 0))
    halo = pl.BlockSpec((HALO, width), lambda i: (_prev_halo(tm)(n_i - 1 - i), 0))
    vec = pl.BlockSpec((1, width), lambda i: (0, 0))
    in_specs = [pl.BlockSpec((tm, dx.shape[1]), lambda i: (n_i - 1 - i, 0)), _resident(w_out.shape), row, row, halo, row]
    return _call(body, name, (n_i,), in_specs, [row, row, row, vec],
                 [_sds((m, width)), _sds((m, width)), _sds((m, width), BF16), _sds((1, width))],
                 [pltpu.VMEM((1, width), F32), pltpu.VMEM((1, width), F32)])(dx, w_out, z, hs, hs, a)


def c_mid_bwd(lam_seq, da, xr, wa, ba, wi, bi, lam, name):
    m, width = xr.shape
    heads, hw = wa.shape[0], wa.shape[1]
    tm = _rows(m, 512)

    def body(l_ref, da_ref, xr_ref, wa_ref, ba_ref, wi_ref, bi_ref, lam_ref,
             dxr_ref, dwa_ref, dwi_ref, dba_ref, dbi_ref, dlam_ref):
        first = pl.program_id(0) == 0
        xr_v, lmb = xr_ref[...], l_ref[...]
        xb, r, ig, sp, a, mult = _c_gates(xr_v, wa_ref, ba_ref, wi_ref, bi_ref, lam_ref, heads, hw)
        dmult = lmb * (ig * xr_v)
        dig = lmb * mult * xr_v
        dxr = lmb * mult * ig
        dla = da_ref[...] * a - dmult * (a * a) / mult
        dr = dla * (-C_GATE_C * sp)
        dsp = _colsum(dla * (-C_GATE_C * r))
        _acc(dlam_ref, dsp * (-_sigmoid(-lam_ref[...])), first)
        dra = dr * r * (1.0 - r)
        dia = dig * ig * (1.0 - ig)
        _acc(dba_ref, _colsum(dra), first)
        _acc(dbi_ref, _colsum(dia), first)
        for h in range(heads):
            cs = slice(h * hw, (h + 1) * hw)
            _acc(dwa_ref.at[h], _dot(xb[:, cs], dra[:, cs], _TN), first)
            _acc(dwi_ref.at[h], _dot(xb[:, cs], dia[:, cs], _TN), first)
            dxr_ref[:, cs] = dxr[:, cs] + _dot(dra[:, cs], wa_ref[h], _NT) + _dot(dia[:, cs], wi_ref[h], _NT)

    vec = pl.BlockSpec((1, width), lambda i: (0, 0))
    row = pl.BlockSpec((tm, width), lambda i: (i, 0))
    wsp = pl.BlockSpec((heads, hw, hw), lambda i: (0, 0, 0))
    return _call(body, name, (m // tm,), [row, row, row, wsp, vec, wsp, vec, vec], [row, wsp, wsp, vec, vec, vec],
                 [_sds((m, width)), _sds((heads, hw, hw)), _sds((heads, hw, hw)), _sds((1, width)), _sds((1, width)),
                  _sds((1, width))])(lam_seq, da, xr, wa, ba, wi, bi, lam)


def conv_bwd(dy, x_src, col_block, cw, name):
    m, width = dy.shape
    taps = cw.shape[0]
    tm = _rows(m, 512)
    n_i = m // tm

    def body(dm_ref, dn_ref, xp_ref, xm_ref, cw_ref, dx_ref, dw_ref, db_ref, dxs_ref):
        i = pl.program_id(0)
        first = i == 0
        de = jnp.concatenate([jnp.zeros((HALO, width), F32), dm_ref[...], dn_ref[...] * (i < n_i - 1).astype(F32)], axis=0)
        xe = jnp.concatenate([xp_ref[...] * (i > 0).astype(F32), xm_ref[...], jnp.zeros((HALO, width), F32)], axis=0)
        w = cw_ref[...]
        dx = de * w[taps - 1:taps]
        for s in range(1, taps):
            dx = dx + _up(de, s) * w[taps - 1 - s:taps - s]
        dx_ref[...] = dx[HALO:HALO + tm].astype(BF16)
        _acc(dxs_ref, _colsum(dx[HALO:HALO + tm]), first)
        dm = dm_ref[...]
        for s in range(taps):
            _acc(dw_ref.at[taps - 1 - s:taps - s, :], _colsum(dm * _down(xe, s)[HALO:HALO + tm]), first)
        _acc(db_ref, _colsum(dm), first)

    row = pl.BlockSpec((tm, width), lambda i: (i, 0))
    vec = pl.BlockSpec((1, width), lambda i: (0, 0))
    tsp = pl.BlockSpec((taps, width), lambda i: (0, 0))
    in_specs = [row, pl.BlockSpec((HALO, width), lambda i: (_next_halo(tm, m)(i), 0)),
                pl.BlockSpec((HALO, width), lambda i: (_prev_halo(tm)(i), col_block)),
                pl.BlockSpec((tm, width), lambda i: (i, col_block)), tsp]
    return _call(body, name, (n_i,), in_specs, [row, tsp, vec, vec],
                 [_sds((m, width), BF16), _sds((taps, width)), _sds((1, width)), _sds((1, width))])(dy, dy, x_src, x_src, cw)


def d_mid_fwd(z, cw, w_out, res, next_g, name):
    m, w3 = z.shape
    width = w3 // 3
    taps = cw.shape[0]
    tm = _rows(m, 512)

    def body(bm_ref, cp_ref, cm_ref, xp_ref, xm_ref, cw_ref, wo_ref, res_ref, ng_ref, y_ref, x_ref, h_ref):
        keep = (pl.program_id(0) > 0).astype(F32)
        qe = (jnp.concatenate([cp_ref[...] * keep, cm_ref[...]], axis=0)
              * jnp.concatenate([xp_ref[...], xm_ref[...]], axis=0))
        y_ref[...] = (bm_ref[...] * _conv_ext(qe, cw_ref[...], taps)[HALO:]).astype(BF16)
        _project_out(y_ref, wo_ref, res_ref, ng_ref, x_ref, h_ref)

    main = lambda c: pl.BlockSpec((tm, width), lambda i: (i, c))
    prev = lambda c: pl.BlockSpec((HALO, width), lambda i: (_prev_halo(tm)(i), c))
    in_specs = [main(0), prev(1), main(1), prev(2), main(2), pl.BlockSpec((taps, width), lambda i: (0, 0))]
    return _call_projected(body, name, m, tm, in_specs, (z, z, z, z, z, cw), [(width, BF16)], w_out, res, next_g)


def d_mid_bwd(z, dy, cw, name):
    m, w3 = z.shape
    width = w3 // 3
    taps = cw.shape[0]
    tm = _rows(m, 512)
    n_i = m // tm

    def body(bm_ref, bn_ref, cp_ref, cm_ref, cn_ref, xp_ref, xm_ref, xn_ref, dm_ref, dn_ref, cw_ref, dz_ref, dw_ref):
        i = pl.program_id(0)
        first = i == 0
        kp, kn = (i > 0).astype(F32), (i < n_i - 1).astype(F32)
        zeros = jnp.zeros((HALO, width), F32)
        ce = jnp.concatenate([cp_ref[...] * kp, cm_ref[...], cn_ref[...] * kn], axis=0)
        xe = jnp.concatenate([xp_ref[...], xm_ref[...], xn_ref[...]], axis=0)
        qe = ce * xe
        be = jnp.concatenate([zeros, bm_ref[...], bn_ref[...]], axis=0)
        dye = jnp.concatenate([zeros, dm_ref[...], dn_ref[...] * kn], axis=0)
        w = cw_ref[...]
        cq = _conv_ext(qe, w, taps)
        dcq = dye * be
        dq = dcq * w[taps - 1:taps]
        for s in range(1, taps):
            dq = dq + _up(dcq, s) * w[taps - 1 - s:taps - s]
        ms = slice(HALO, HALO + tm)
        dz_ref[:, :width] = (dye * cq)[ms].astype(BF16)
        dz_ref[:, width:2 * width] = (dq * xe)[ms].astype(BF16)
        dz_ref[:, 2 * width:] = (dq * ce)[ms].astype(BF16)
        for s in range(taps):
            _acc(dw_ref.at[taps - 1 - s:taps - s, :], _colsum(dcq[ms] * _down(qe, s)[ms]), first)

    main = lambda c: pl.BlockSpec((tm, width), lambda i: (i, c))
    prev = lambda c: pl.BlockSpec((HALO, width), lambda i: (_prev_halo(tm)(i), c))
    nxt = lambda c: pl.BlockSpec((HALO, width), lambda i: (_next_halo(tm, m)(i), c))
    tsp = pl.BlockSpec((taps, width), lambda i: (0, 0))
    in_specs = [main(0), nxt(0), prev(1), main(1), nxt(1), prev(2), main(2), nxt(2), main(0), nxt(0), tsp]
    return _call(body, name, (n_i,), in_specs, [pl.BlockSpec((tm, w3), lambda i: (i, 0)), tsp],
                 [_sds((m, w3), BF16), _sds((taps, width))])(z, z, z, z, z, z, z, z, dy, dy, cw)


def _halo_rows(dtype):
    return HALO * (4 // jnp.dtype(dtype).itemsize)


def ffn_gate_down(z, cw, cb, w_down, res, next_g, name):
    _, nj, m, c = z.shape
    n = w_down.shape[1]
    taps = cw.shape[2]
    tm = _rows(m, 256)
    hz = _halo_rows(z.dtype)

    def body(zp_ref, zm_ref, cw_ref, cb_ref, w_ref, res_ref, *rest):
        keep = (pl.program_id(0) > 0).astype(F32)
        zc_ref, act_ref, x_ref = rest[-4:-1] if next_g is not None else rest[-3:]
        acc = res_ref[...]
        for j in range(nj):
            zc = []
            for s in range(2):
                xe = jnp.concatenate([zp_ref[s, j].astype(F32) * keep, zm_ref[s, j].astype(F32)], axis=0)
                zc.append(_conv_ext(xe, cw_ref[s, j], taps)[hz:] + cb_ref[s, j])
                zc_ref[s, j] = zc[s].astype(BF16)
            act = (zc[0] * _sigmoid(zc[0]) * zc[1]).astype(BF16)
            act_ref[j] = act
            acc = acc + _dot(act, w_ref[j])
        x_ref[...] = acc
        if next_g is not None:
            rest[-1][...] = _norm_rows(acc, rest[0][...])

    row = pl.BlockSpec((tm, n), lambda i: (i, 0))
    in_specs = [pl.BlockSpec((2, nj, hz, c), lambda i: (0, 0, _prev_halo(tm, hz)(i), 0)),
                pl.BlockSpec((2, nj, tm, c), lambda i: (0, 0, i, 0)),
                _resident(cw.shape), _resident(cb.shape), _resident((nj, c, n)), row]
    out_specs = [pl.BlockSpec((2, nj, tm, c), lambda i: (0, 0, i, 0)), pl.BlockSpec((nj, tm, c), lambda i: (0, i, 0)), row]
    out_shape = [_sds((2, nj, m, c), BF16), _sds((nj, m, c), BF16), _sds((m, n))]
    ins = [z, z, cw, cb, w_down.reshape(nj, c, n), res]
    if next_g is not None:
        in_specs.append(pl.BlockSpec((1, n), lambda i: (0, 0)))
        out_specs.append(row)
        out_shape.append(_sds((m, n), BF16))
        ins.append(next_g)
    return _call(body, name, (m // tm,), in_specs, out_specs, out_shape)(*ins)


def ffn_gate_bwd(z, zc, dx, w_down, cw, name, after=()):
    _, nj, m, c = z.shape
    n = w_down.shape[1]
    taps = cw.shape[2]
    tm = _rows(m, 512)
    n_i = m // tm
    hz = _halo_rows(z.dtype)
    assert _halo_rows(dx.dtype) == hz and zc.dtype == z.dtype, (z.dtype, zc.dtype, dx.dtype)

    def body(zp_ref, zm_ref, zn_ref, cm_ref, cn_ref, dm_ref, dn_ref, wd_ref, cw_ref, dz_ref, dw_ref, db_ref):
        i = pl.program_id(1)
        first = i == 0
        kp, kn = (i > 0).astype(F32), (i < n_i - 1).astype(F32)
        xe = [jnp.concatenate([zp_ref[s].astype(F32) * kp, zm_ref[s].astype(F32), zn_ref[s].astype(F32) * kn], axis=0)
              for s in range(2)]
        zc = [jnp.concatenate([jnp.zeros((hz, c), F32), cm_ref[s].astype(F32), cn_ref[s].astype(F32)], axis=0)
              for s in range(2)]
        dact = _dot(jnp.concatenate([dm_ref[...], dn_ref[...]], axis=0), wd_ref[...], _NT)
        dae = jnp.concatenate([jnp.zeros((hz, c), F32), dact[:tm], dact[tm:] * kn], axis=0)
        sg = _sigmoid(zc[0])
        dzc = [dae * zc[1] * (sg * (1.0 + zc[0] * (1.0 - sg))), dae * (zc[0] * sg)]
        ms = slice(hz, hz + tm)
        for s in range(2):
            w = cw_ref[s]
            ups = [dzc[s]] + [_up(dzc[s], u) for u in range(1, taps)]
            dxs = ups[0] * w[taps - 1:taps]
            for u in range(1, taps):
                dxs = dxs + ups[u] * w[taps - 1 - u:taps - u]
            dz_ref[s] = dxs[ms].astype(BF16)
            tail = dzc[s][hz + tm:]
            x_end = xe[s][tm:]
            for u in range(taps):
                total = _colsum(ups[u] * xe[s]) - _colsum(tail * _down(x_end, u)[hz:])
                _acc(dw_ref.at[s, taps - 1 - u:taps - u, :], total, first)
            _acc(db_ref.at[s], _colsum(dzc[s][ms]), first)

    in_specs = [pl.BlockSpec((2, None, hz, c), lambda j, i: (0, j, _prev_halo(tm, hz)(i), 0)),
                pl.BlockSpec((2, None, tm, c), lambda j, i: (0, j, i, 0)),
                pl.BlockSpec((2, None, hz, c), lambda j, i: (0, j, _next_halo(tm, m, hz)(i), 0)),
                pl.BlockSpec((2, None, tm, c), lambda j, i: (0, j, i, 0)),
                pl.BlockSpec((2, None, hz, c), lambda j, i: (0, j, _next_halo(tm, m, hz)(i), 0)),
                pl.BlockSpec((tm, n), lambda j, i: (i, 0)),
                pl.BlockSpec((hz, n), lambda j, i: (_next_halo(tm, m, hz)(i), 0)),
                pl.BlockSpec((None, c, n), lambda j, i: (j, 0, 0)),
                pl.BlockSpec((2, None, taps, c), lambda j, i: (0, j, 0, 0))]
    out_specs = [pl.BlockSpec((2, None, tm, c), lambda j, i: (0, j, i, 0)),
                 pl.BlockSpec((2, None, taps, c), lambda j, i: (0, j, 0, 0)),
                 pl.BlockSpec((2, None, 1, c), lambda j, i: (0, j, 0, 0))]
    return _call(body, name, (nj, n_i), in_specs, out_specs,
                 [_sds((2, nj, m, c), BF16), _sds((2, nj, taps, c)), _sds((2, nj, 1, c))], (), after)(
        z, z, z, zc, zc, dx, dx, w_down.reshape(nj, c, n), cw)


_STREAM_TILE_BYTES = 2 << 20


def _tile_rows(r, c):
    if r * c * 4 <= _STREAM_TILE_BYTES:
        return r
    fits = [d for d in range(16, r, 16) if r % d == 0 and d * c * 4 <= _STREAM_TILE_BYTES]
    return max(fits) if fits else r


def _as2d(a, lead):
    shape = a.shape
    return a.reshape((lead, -1, shape[-1]) if lead else (-1, shape[-1]))


def add_pairs(g, l1, own, name):
    shape = l1.shape
    g3, l3 = _as2d(g, N_DEV), _as2d(l1, 4)
    _, r, c = l3.shape
    tr = _tile_rows(r, c)

    def body(own_ref, a_ref, b_ref, o_ref):
        o_ref[...] = (a_ref[...].astype(F32) + b_ref[...].astype(F32)).astype(o_ref.dtype)

    spec = pl.BlockSpec((None, tr, c), lambda k, i, own_ref: (k, i, 0))
    grid_spec = pltpu.PrefetchScalarGridSpec(
        num_scalar_prefetch=1, grid=(4, r // tr),
        in_specs=[pl.BlockSpec((None, tr, c), lambda k, i, own_ref: (own_ref[k], i, 0)), spec], out_specs=spec)
    out = pl.pallas_call(
        body, name=name, grid_spec=grid_spec, out_shape=_sds(l3.shape, l1.dtype),
        compiler_params=pltpu.CompilerParams(dimension_semantics=("arbitrary", "arbitrary"),
                                             vmem_limit_bytes=V7X_VMEM_LIMIT_BYTES))(own, g3, l3)
    return out.reshape(shape)


def _grad_sum(p_ref, l_ref):
    return ((p_ref[...].astype(F32) + l_ref[0].astype(F32)) + l_ref[1].astype(F32)) + l_ref[2].astype(F32)


def sum_parts(p, l2, name):
    _, r, c = p.shape

    def body(p_ref, l_ref, o_ref):
        o_ref[...] = _grad_sum(p_ref, l_ref)

    return _call(body, name, (1,), [pl.BlockSpec((None, r, c), lambda i: (0, 0, 0)), pl.BlockSpec((3, r, c), lambda i: (0, 0, 0))],
                 pl.BlockSpec((r, c), lambda i: (0, 0)), _sds((r, c)))(p, l2)


def _adamw_math(w, g, m, v):
    m = ADAM_B1 * m + (1.0 - ADAM_B1) * g
    v = ADAM_B2 * v + (1.0 - ADAM_B2) * (g * g)
    m_hat = m / (1.0 - ADAM_B1 ** ADAM_STEP)
    v_hat = v / (1.0 - ADAM_B2 ** ADAM_STEP)
    delta = -ADAM_LR * (m_hat / (jnp.sqrt(v_hat) + ADAM_EPS) + ADAM_WD * w)
    return delta, m, v


def adamw(w, m, v, name, g=None, p=None, l2=None):
    shape = w.shape
    w2, m2, v2 = (_as2d(t, 0) for t in (w, m, v))
    r, c = w2.shape
    tr = _tile_rows(r, c)
    row = pl.BlockSpec((tr, c), lambda i: (i, 0))
    if g is None:
        p3, l3 = _as2d(p, 4), _as2d(l2, 3)
        gin = (p3, l3)
        gspecs = [pl.BlockSpec((None, tr, c), lambda i: (0, i, 0)), pl.BlockSpec((3, tr, c), lambda i: (0, i, 0))]
    else:
        gin, gspecs = (_as2d(g, 0),), [row]

    def body(*refs):
        n_g = len(gin)
        w_ref, m_ref, v_ref, g_ref, d_ref, nm_ref, nv_ref = refs[n_g:]
        grad = refs[0][...] if n_g == 1 else _grad_sum(refs[0], refs[1])
        delta, nm, nv = _adamw_math(w_ref[...], grad, m_ref[...], v_ref[...])
        g_ref[...] = grad
        d_ref[...] = delta
        nm_ref[...] = nm
        nv_ref[...] = nv

    outs = _call(body, name, (r // tr,), gspecs + [row, row, row], [row] * 4, [_sds((r, c))] * 4)(*gin, w2, m2, v2)
    return tuple(o.reshape(shape) for o in outs)


def adamw_layer(w, m, v, layer, prev, name, p, l2):
    n_l, r, c = w.shape
    tr = _tile_rows(r, c)
    slab = pl.BlockSpec((None, tr, c), lambda i: (layer, i, 0))
    in_specs = [pl.BlockSpec((None, tr, c), lambda i: (0, i, 0)), pl.BlockSpec((3, tr, c), lambda i: (0, i, 0)), slab, slab, slab]
    n_in = len(in_specs)
    prev = () if prev is None else tuple(prev)

    def body(p_ref, l_ref, w_ref, m_ref, v_ref, *rest):
        g_ref, d_ref, nm_ref, nv_ref = rest[len(prev):]
        grad = _grad_sum(p_ref, l_ref)
        delta, nm, nv = _adamw_math(w_ref[...], grad, m_ref[...], v_ref[...])
        g_ref[...] = grad
        d_ref[...] = delta
        nm_ref[...] = nm
        nv_ref[...] = nv

    return pl.pallas_call(
        body, name=name, grid=(r // tr,), in_specs=in_specs + [pl.BlockSpec(memory_space=pl.ANY)] * len(prev),
        out_specs=[slab] * 4, out_shape=[_sds((n_l, r, c))] * 4,
        input_output_aliases={n_in + q: q for q in range(len(prev))},
        compiler_params=pltpu.CompilerParams(dimension_semantics=("arbitrary",), vmem_limit_bytes=V7X_VMEM_LIMIT_BYTES),
    )(_as2d(p, 4), _as2d(l2, 3), w, m, v, *prev)


def _comm_call(body, name, ins, out_shape, n_sems):
    any_spec = pl.BlockSpec(memory_space=pl.ANY)
    return pl.pallas_call(
        body, name=name, in_specs=[any_spec] * len(ins), out_specs=[any_spec] * len(out_shape), out_shape=out_shape,
        scratch_shapes=[pltpu.SemaphoreType.DMA((n,)) for n in n_sems],
        compiler_params=pltpu.CompilerParams(has_side_effects=True))(*ins)


def _place():
    return lax.axis_index("x"), lax.axis_index("y"), lax.axis_index("c")


def _dev_index(px, py, pc):
    return 4 * px + 2 * py + pc


def all_gather(blocks, name):
    n_t = len(blocks)

    def body(*refs):
        ins, outs = refs[:n_t], refs[n_t:2 * n_t]
        send_sems, recv_sems, local_sems = refs[2 * n_t:]
        x, y, c = _place()
        me, sibling = (x, y, c), (x, y, 1 - c)
        chips = [(1 - x, y), (x, 1 - y), (1 - x, 1 - y)]

        def copy(t, k, block, to, src=None):
            dst = outs[t].at[_dev_index(*block)]
            return pltpu.make_async_remote_copy(
                src_ref=dst if src is None else src, dst_ref=dst, send_sem=send_sems.at[t * 7 + k],
                recv_sem=recv_sems.at[t * 7 + k], device_id=to, device_id_type=MESH_ID)

        mine = [pltpu.make_async_copy(ins[t], outs[t].at[_dev_index(*me)], local_sems.at[t]) for t in range(n_t)]
        for cp in mine:
            cp.start()
        first = []
        for t in range(n_t):
            first.append(copy(t, 0, me, sibling, src=ins[t]))
            first += [copy(t, 1 + j, me, (*chip, c), src=ins[t]) for j, chip in enumerate(chips)]
        for cp in first:
            cp.start()
        passed = []
        for t in range(n_t):
            for j, chip in enumerate(chips):
                copy(t, 1 + j, (*chip, c), me).wait_recv()
                cp = copy(t, 4 + j, (*chip, c), sibling)
                cp.start()
                passed.append(cp)
        for t in range(n_t):
            copy(t, 0, sibling, me).wait_recv()
            for j, chip in enumerate(chips):
                copy(t, 4 + j, (*chip, 1 - c), me).wait_recv()
        for cp in first + passed:
            cp.wait_send()
        for cp in mine:
            cp.wait()

    out_shape = [_sds((N_DEV,) + b.shape, b.dtype) for b in blocks]
    return _comm_call(body, name, blocks, out_shape, (7 * n_t, 7 * n_t, n_t))


def _chip_of(x, y, k):
    return (x if k % 2 == 0 else 1 - x), (y if k // 2 == 0 else 1 - y)


_HBM_SPEC = pl.BlockSpec(memory_space=pltpu.HBM)
_SEM_SPEC = pl.BlockSpec(memory_space=pltpu.SEMAPHORE)
_DATAFLOW = pltpu.SideEffectType.DATAFLOW_SIDE_EFFECTING


def _in_hbm(a):
    return pltpu.with_memory_space_constraint(a, pltpu.HBM)


def _split_start(name, issue, srcs, land_shapes, sem_counts, after=()):
    n_buf, n_sem, n_after = len(srcs) + len(land_shapes), len(sem_counts), len(after)

    def body(*refs):
        issue(refs[:len(srcs)], refs[len(srcs):n_buf], refs[n_buf + n_after:n_buf + n_after + n_sem])
        refs[-1][...] = jnp.zeros_like(refs[-1])

    bufs = [pltpu.HBM(s.shape, s.dtype) for s in list(srcs) + list(land_shapes)]
    outs = pl.pallas_call(
        body, name=name, in_specs=(*[_HBM_SPEC] * n_buf, *[pl.BlockSpec(memory_space=pl.ANY)] * n_after),
        out_shape=(*[pltpu.SemaphoreType.DMA((n,)) for n in sem_counts], *bufs, _sds((8, LANES))),
        out_specs=(*[_SEM_SPEC] * n_sem, *[_HBM_SPEC] * n_buf, pl.BlockSpec(memory_space=pltpu.VMEM)),
        input_output_aliases={i: n_sem + i for i in range(n_buf)},
        compiler_params=pltpu.CompilerParams(has_side_effects=_DATAFLOW),
    )(*[_in_hbm(s) for s in srcs], *[_in_hbm(lax.empty(s.shape, s.dtype)) for s in land_shapes], *after)
    return outs[:n_sem], outs[n_sem:n_sem + len(srcs)], outs[n_sem + len(srcs):n_sem + n_buf], outs[-1]


def _split_wait(name, finish, sems, srcs, lands, after):
    n_buf, n_sem = len(srcs) + len(lands), len(sems)

    def body(*refs):
        finish(refs[:len(srcs)], refs[len(srcs):n_buf], refs[n_buf:n_buf + n_sem])

    bufs = [pltpu.HBM(s.shape, s.dtype) for s in list(srcs) + list(lands)]
    outs = pl.pallas_call(
        body, name=name, in_specs=(*[_HBM_SPEC] * n_buf, *[_SEM_SPEC] * n_sem, *[pl.BlockSpec(memory_space=pl.ANY)] * len(after)),
        out_shape=tuple(bufs), out_specs=(_HBM_SPEC,) * n_buf, input_output_aliases={i: i for i in range(n_buf)},
        compiler_params=pltpu.CompilerParams(has_side_effects=_DATAFLOW),
    )(*srcs, *lands, *sems, *after)
    return outs[:len(srcs)], outs[len(srcs):]


def _peer(x, y, c, r):
    return (1 - x if r & 4 else x), (1 - y if r & 2 else y), (1 - c if r & 1 else c)


ALL_PEERS = tuple(range(1, N_DEV))
SAME_CORE_PEERS = (2, 4, 6)


def _gather_copies(src_refs, land_refs, sem_refs, arrivals, peers):
    send_sems, recv_sems, local_sems = sem_refs
    x, y, c = _place()
    me = _dev_index(x, y, c)
    local, sends, recvs = [], [], []
    for j, (src, land) in enumerate(zip(src_refs, land_refs)):
        local.append(pltpu.make_async_copy(src, land.at[me], local_sems.at[j]))
        for p, r in enumerate(peers):
            peer = _peer(x, y, c, r)
            q = len(peers) * j + p
            sends.append(pltpu.make_async_remote_copy(src_ref=src, dst_ref=land.at[me], send_sem=send_sems.at[q],
                                                      recv_sem=recv_sems.at[q], device_id=peer, device_id_type=MESH_ID))
            if arrivals:
                recvs.append(pltpu.make_async_remote_copy(
                    src_ref=src, dst_ref=land.at[_dev_index(*peer)], send_sem=send_sems.at[q], recv_sem=recv_sems.at[q],
                    device_id=peer, device_id_type=MESH_ID))
    return local, sends, recvs


def gather_start(groups, peers, name, after=()):
    flat = [b for g in groups for b in g]
    bounds = [sum(len(g) for g in groups[:i]) for i in range(len(groups) + 1)]

    def issue(src_refs, land_refs, sem_refs):
        for i in range(len(groups)):
            lo, hi = bounds[i], bounds[i + 1]
            local, sends, _ = _gather_copies(src_refs[lo:hi], land_refs[lo:hi], sem_refs[3 * i:3 * i + 3], False, peers[i])
            for cp in local + sends:
                cp.start()

    sem_counts = [n for g, p in zip(groups, peers) for n in (len(p) * len(g), len(p) * len(g), len(g))]
    sems, srcs, lands, _ = _split_start(name, issue, flat, [_sds((N_DEV,) + b.shape, b.dtype) for b in flat], sem_counts, after)
    return [(sems[3 * i:3 * i + 3], srcs[bounds[i]:bounds[i + 1]], lands[bounds[i]:bounds[i + 1]], peers[i])
            for i in range(len(groups))]


def gather_wait(group, after, name):
    sems, srcs, lands, peers = group

    def finish(src_refs, land_refs, sem_refs):
        local, sends, recvs = _gather_copies(src_refs, land_refs, sem_refs, True, peers)
        for cp in local:
            cp.wait()
        for cp in recvs:
            cp.wait_recv()
        for cp in sends:
            cp.wait_send()

    return _split_wait(name, finish, sems, srcs, lands, after)[1]


def _forward_copies(land_refs, sem_refs, arrivals):
    send_sems, recv_sems = sem_refs
    x, y, c = _place()
    sends, recvs = [], []
    for t, land in enumerate(land_refs):
        for k in range(4):
            cx, cy = _chip_of(x, y, k)
            mine, theirs = land.at[_dev_index(cx, cy, c)], land.at[_dev_index(cx, cy, 1 - c)]
            sems = dict(send_sem=send_sems.at[4 * t + k], recv_sem=recv_sems.at[4 * t + k], device_id=(x, y, 1 - c),
                        device_id_type=MESH_ID)
            sends.append(pltpu.make_async_remote_copy(src_ref=mine, dst_ref=mine, **sems))
            if arrivals:
                recvs.append(pltpu.make_async_remote_copy(src_ref=theirs, dst_ref=theirs, **sems))
    return sends, recvs


def forward_start(lands, name):
    def issue(land_refs, _, sem_refs):
        for cp in _forward_copies(land_refs, sem_refs, False)[0]:
            cp.start()

    n = 4 * len(lands)
    sems, lands, _, token = _split_start(name, issue, lands, [], (n, n))
    return sems, lands, token


def forward_wait(started, after, name):
    sems, lands, _ = started

    def finish(land_refs, _, sem_refs):
        sends, recvs = _forward_copies(land_refs, sem_refs, True)
        for cp in recvs:
            cp.wait_recv()
        for cp in sends:
            cp.wait_send()

    return _split_wait(name, finish, sems, lands, [], after)[0]


def _sibling_copies(src_refs, land_refs, sem_refs):
    send_sems, recv_sems = sem_refs
    x, y, c = _place()
    copies = []
    for t, (src, land) in enumerate(zip(src_refs, land_refs)):
        for k in range(4):
            cx, cy = _chip_of(x, y, k)
            copies.append(pltpu.make_async_remote_copy(
                src_ref=src.at[_dev_index(cx, cy, 1 - c)], dst_ref=land.at[k], send_sem=send_sems.at[4 * t + k],
                recv_sem=recv_sems.at[4 * t + k], device_id=(x, y, 1 - c), device_id_type=MESH_ID))
    return copies


def _chip_copies(src_refs, land_refs, sem_refs):
    send_sems, recv_sems = sem_refs
    x, y, c = _place()
    copies = []
    for t, (src, land) in enumerate(zip(src_refs, land_refs)):
        for k in range(1, 4):
            cx, cy = _chip_of(x, y, k)
            copies.append(pltpu.make_async_remote_copy(
                src_ref=src.at[k], dst_ref=land.at[k - 1], send_sem=send_sems.at[3 * t + k - 1],
                recv_sem=recv_sems.at[3 * t + k - 1], device_id=(cx, cy, c), device_id_type=MESH_ID))
    return copies


def _exchange_start(copies_of, n_land, per_array, arrays, name):
    def issue(src_refs, land_refs, sem_refs):
        for cp in copies_of(src_refs, land_refs, sem_refs):
            cp.start()

    n = per_array * len(arrays)
    lands = [_sds((n_land,) + a.shape[1:], a.dtype) for a in arrays]
    return _split_start(name, issue, arrays, lands, (n, n))


def _exchange_wait(copies_of, started, after, name):
    sems, srcs, lands, _ = started

    def finish(src_refs, land_refs, sem_refs):
        copies = copies_of(src_refs, land_refs, sem_refs)
        for cp in copies:
            cp.wait_recv()
        for cp in copies:
            cp.wait_send()

    return _split_wait(name, finish, sems, srcs, lands, after)


def sibling_start(grads, name):
    return _exchange_start(_sibling_copies, 4, 4, grads, name)


def sibling_wait(started, after, name):
    return _exchange_wait(_sibling_copies, started, after, name)


def chips_start(parts, name):
    return _exchange_start(_chip_copies, 3, 3, parts, name)


def chips_wait(started, after, name):
    return _exchange_wait(_chip_copies, started, after, name)


def _pack(arrays, rows):
    flat = jnp.concatenate([a.reshape(-1) for a in arrays])
    return jnp.pad(flat, (0, rows * LANES - flat.shape[0])).reshape(rows, LANES)


def _pack_stacked(arrays, rows):
    flat = jnp.concatenate([a.reshape(N_DEV, -1) for a in arrays], axis=1)
    return jnp.pad(flat, ((0, 0), (0, rows * LANES - flat.shape[1]))).reshape(N_DEV, rows, LANES)


def _unpack(buf, shapes, lead=()):
    flat = buf.reshape(lead + (-1,))
    out, off = [], 0
    for s in shapes:
        n = 1
        for d in s:
            n *= d
        out.append(flat[..., off:off + n].reshape(lead + tuple(s)))
        off += n
    return out


def _padded_rows(shapes, multiple):
    n = sum(functools.reduce(lambda a, b: a * b, s, 1) for s in shapes)
    rows = -(-n // LANES)
    return -(-rows // multiple) * multiple


def _to_full(stacked, axis):
    t = jnp.moveaxis(stacked, 0, axis)
    return t.reshape(t.shape[:axis] + (t.shape[axis] * t.shape[axis + 1],) + t.shape[axis + 2:])


def _to_stacked(full, axis):
    s = full.shape
    t = full.reshape(s[:axis] + (N_DEV, s[axis] // N_DEV) + s[axis + 1:])
    return jnp.moveaxis(t, axis, 0)


def _ffn_forward(x, h, w_up, cw, cb, w_down, next_g, tag, after=()):
    z = mm_in(h, w_up, f"ffn{tag}_up", stacked_out=True, out_dtype=BF16, rows=2048, after=after)
    nb, m, c = z.shape
    z4 = z.reshape(2, nb // 2, m, c)
    zc, act, *out = ffn_gate_down(z4, cw, cb, w_down, x, next_g, f"ffn{tag}_gate_down")
    return (out[0], out[1] if next_g is not None else None), (x, h, z4, zc, act)


def _ffn_backward(dx, dxb, saved, norm_g, w_up, cw, w_down, tag, after=()):
    x, h, z4, zc, act = saved
    nj = act.shape[0]
    dz4, dcw, dcb = ffn_gate_bwd(z4, zc, dxb, w_down, cw, f"ffn{tag}_gate_bwd", after)
    dw_down = mm_dw_out(act, dxb, f"ffn{tag}_down_dw")
    dz = dz4.reshape((2 * nj,) + dz4.shape[2:])
    dx, dxb, dg = mm_dx_in(dz, w_up, f"ffn{tag}_up_dx", norm=(x, norm_g, dx))
    dw_up = mm_dw_in(h, dz, 2 * nj, f"ffn{tag}_up_dw", transposed=True)
    return dx, dxb, dict(norm_g=dg, w_up=dw_up, conv_w=dcw, conv_b=dcb, w_down=dw_down)


def kernel(x, a_norm_g, a_w_in, a_b_in, a_v_norm_g, a_w_s, a_b_s, a_w_out, b_norm_g, b_w_in, b_w_grp, b_b_grp, b_scale, b_w_out, c_norm_g, c_w_in, c_b_in, c_conv_w, c_conv_b, c_w_a, c_b_a, c_w_i, c_b_i, c_lambda, c_w_out, d_norm_g, d_w_in, d_conv_w, d_w_out, ffn_norm_g, ffn_w_up, ffn_conv_w, ffn_conv_b, ffn_w_down, final_norm_g, loss_target, m_a_norm_g, m_a_w_in, m_a_b_in, m_a_v_norm_g, m_a_w_s, m_a_b_s, m_a_w_out, m_b_norm_g, m_b_w_in, m_b_w_grp, m_b_b_grp, m_b_scale, m_b_w_out, m_c_norm_g, m_c_w_in, m_c_b_in, m_c_conv_w, m_c_conv_b, m_c_w_a, m_c_b_a, m_c_w_i, m_c_b_i, m_c_lambda, m_c_w_out, m_d_norm_g, m_d_w_in, m_d_conv_w, m_d_w_out, m_ffn_norm_g, m_ffn_w_up, m_ffn_conv_w, m_ffn_conv_b, m_ffn_w_down, m_final_norm_g, v_a_norm_g, v_a_w_in, v_a_b_in, v_a_v_norm_g, v_a_w_s, v_a_b_s, v_a_w_out, v_b_norm_g, v_b_w_in, v_b_w_grp, v_b_b_grp, v_b_scale, v_b_w_out, v_c_norm_g, v_c_w_in, v_c_b_in, v_c_conv_w, v_c_conv_b, v_c_w_a, v_c_b_a, v_c_w_i, v_c_b_i, v_c_lambda, v_c_w_out, v_d_norm_g, v_d_w_in, v_d_conv_w, v_d_w_out, v_ffn_norm_g, v_ffn_w_up, v_ffn_conv_w, v_ffn_conv_b, v_ffn_w_down, v_final_norm_g):
    args = locals()
    w_loc = {n: args[n] for n in WEIGHTS}
    m_loc = {n: args["m_" + n] for n in WEIGHTS}
    v_loc = {n: args["v_" + n] for n in WEIGHTS}
    depth = ffn_w_up.shape[0]
    xs = x[0]
    target = loss_target[0]

    small_names = list(SMALL_SHARDED)
    small_shapes = [w_loc[n].shape for n in small_names]
    small_rows = _padded_rows(small_shapes, 8)
    small_packed = _pack([w_loc[n] for n in small_names], small_rows)
    early_names = ['ffn_conv_w', 'b_norm_g', 'c_norm_g', 'd_norm_g']
    late_names = [n for n in small_names if n not in early_names]
    packs = []
    for names in (early_names, late_names):
        shapes = [w_loc[n].shape for n in names]
        packs.append((names, shapes, _pack([w_loc[n] for n in names], _padded_rows(shapes, 8))))
    mixers = [(a_w_in, a_w_out), (b_w_in, b_w_out), (c_w_in, c_w_out), (d_w_in, d_w_out)]
    groups = {}
    for l in range(depth):
        groups[f'mixer{l}'] = [mixers[l][0][0].astype(BF16), mixers[l][1][0].astype(BF16)] + ([packs[1][2]] if l == 1 else [])
        groups[f'ffn{l}'] = [ffn_w_up[l].astype(BF16), ffn_w_down[l].astype(BF16)] + ([packs[0][2]] if l == 0 else [])
    two_level = ('mixer0', 'ffn0', 'mixer1', 'ffn1')
    in_flight = dict(zip(groups, gather_start(list(groups.values()),
                                              [SAME_CORE_PEERS if k in two_level else ALL_PEERS for k in groups], "gather_start")))
    forwarding = {}

    def forward_on(key, after):
        forwarding[key] = forward_start(gather_wait(in_flight[key], after, f"gather_wait_{key}"), f"forward_start_{key}")
        return (forwarding[key][2],)

    def gathered(key, after):
        if key in two_level:
            return forward_wait(forwarding[key], after, f"forward_wait_{key}")
        return gather_wait(in_flight[key], after, f"gather_wait_{key}")

    sm = {}

    def unpack_small(pack, gathered):
        names, shapes, _ = pack
        sm.update((n, _to_full(s, SMALL_SHARDED[n])) for n, s in zip(names, _unpack(gathered, shapes, (N_DEV,))))

    def rows_full(st):
        return st.reshape((st.shape[0] * st.shape[1],) + st.shape[2:])

    nb = N_DEV
    ffn_cb = [ffn_conv_b[l].reshape(2, nb // 2, 1, -1) for l in range(depth)]
    ffn_g = [ffn_norm_g[l:l + 1] for l in range(depth)]
    a_bst = a_b_s[0].T
    w_up, w_down, saved = [None] * depth, [None] * depth, {}

    def ffn_forward(xl, hl, l, next_g, after=()):
        up, down, *early = gathered(f'ffn{l}', (xl,))
        if early:
            unpack_small(packs[0], early[0])
            ffn_cw.extend(sm['ffn_conv_w'][k].reshape(ffn_conv_w.shape[1], 2, nb // 2, -1).transpose(1, 2, 0, 3)
                          for k in range(depth))
        w_up[l], w_down[l] = up, rows_full(down)
        return _ffn_forward(xl, hl, w_up[l], ffn_cw[l], ffn_cb[l], w_down[l], next_g(), l, after)

    ffn_cw = []
    h = rms_fwd(xs, a_norm_g, "a_norm")
    forward_on('mixer0', (h, small_packed))
    wa_in, wa_out = gathered('mixer0', (h,))
    wa_out = rows_full(wa_out)
    z = mm_in(h, wa_in, "a_in", bias=a_b_in)
    y = a_mid_fwd(z, a_v_norm_g, a_w_s[0], a_bst, "a_mid")
    x1, hf = mm_out(y, wa_out, xs, ffn_g[0], "a_out", forward_on('ffn0', (y,)))
    saved['a'] = (xs, h, z, y)
    (x1, h), saved['f0'] = ffn_forward(x1, hf, 0, lambda: sm['b_norm_g'], forward_on('mixer1', (x1,)))

    sent = forward_on('ffn1', (x1,))
    wb_in, wb_out, late = gathered('mixer1', (x1,))
    unpack_small(packs[1], late)
    b_wgrp, c_wa, c_wi = (sm[n][0].astype(BF16) for n in ('b_w_grp', 'c_w_a', 'c_w_i'))
    b_bgrp, c_ba, c_bi = (sm[n][0].reshape(1, -1) for n in ('b_b_grp', 'c_b_a', 'c_b_i'))
    wb_in, wb_out = rows_full(wb_in)[None], rows_full(wb_out)
    z = mm_in(h, wb_in, "b_in", after=sent)
    y, x2, hf = b_mid_fwd(z, b_wgrp, b_bgrp, sm['b_scale'], wb_out, x1, ffn_g[1], "b_mid_out")
    saved['b'] = (x1, h, z, y)
    (x2, h), saved['f1'] = ffn_forward(x2, hf, 1, lambda: sm['c_norm_g'])

    wc_in, wc_out = gathered('mixer2', (x2,))
    wc_out = rows_full(wc_out)
    z = mm_in(h, wc_in, "c_in", bias=sm['c_b_in'])
    c_cw = sm['c_conv_w'][0]
    a_seq, b_seq, xr = c_mid_fwd(z, c_cw, sm['c_conv_b'], c_wa, c_ba, c_wi, c_bi, sm['c_lambda'], "c_mid")
    hs, y, x3, hf = c_scan_fwd(a_seq, b_seq, z, wc_out, x2, ffn_g[2], "c_scan_out")
    saved['c'] = (x2, h, z, y, a_seq, xr, hs)
    (x3, h), saved['f2'] = ffn_forward(x3, hf, 2, lambda: sm['d_norm_g'])

    wd_in, wd_out = gathered('mixer3', (x3,))
    wd_out = rows_full(wd_out)
    z = mm_in(h, wd_in, "d_in")
    d_cw = sm['d_conv_w'][0]
    y, x4, hf = d_mid_fwd(z, d_cw, wd_out, x3, ffn_g[3], "d_mid_out")
    saved['d'] = (x3, h, z, y)
    (x4, _), saved['f3'] = ffn_forward(x4, hf, 3, lambda: None)

    loss_part, dx, dxb, d_final_g = final_loss(x4, final_norm_g.reshape(1, -1), target, "final_loss")
    loss = lax.psum(loss_part[0, 0], ("x", "y", "c"))

    def rows_stacked(full):
        return full.reshape((N_DEV, full.shape[0] // N_DEV) + full.shape[1:])

    mx, my, mc = _place()
    own = jnp.stack([_dev_index(*_chip_of(mx, my, k), mc) for k in range(4)]).astype(jnp.int32)
    repl_shapes = [w_loc[n].shape for n in REPLICATED]
    repl_rows = _padded_rows(repl_shapes, 8 * N_DEV)
    shard_of = {n: (w_loc[n][0], m_loc[n][0], v_loc[n][0])
                for n in ('a_w_in', 'a_w_out', 'b_w_in', 'b_w_out', 'c_w_in', 'c_w_out', 'd_w_in', 'd_w_out')}
    shard_of['small'] = (small_packed, _pack([m_loc[n] for n in small_names], small_rows),
                         _pack([v_loc[n] for n in small_names], small_rows))
    updated = {}

    def finish(n, part, others):
        if n == 'repl':
            chunk = sum_parts(part, others, "rs_sum_repl")
            repl_g = all_gather([chunk], "gather_repl")[0].reshape(repl_rows, LANES)
            updated[n] = adamw(*(_pack([src[k] for k in REPLICATED], repl_rows) for src in (w_loc, m_loc, v_loc)),
                               "adamw_repl", g=repl_g)
        elif n.startswith('ffn_w'):
            base, l = n[:-1], int(n[-1])
            turn = (lambda t: jnp.swapaxes(t, 1, 2)) if base == 'ffn_w_up' else (lambda t: t)
            updated[base] = adamw_layer(turn(w_loc[base]), turn(m_loc[base]), turn(v_loc[base]), l, updated.get(base),
                                        f"adamw_{n}", part, others)
            return updated[base][1]
        else:
            updated[n] = adamw(*shard_of[n], f"adamw_{n}", p=part, l2=others)
        return updated[n][1]

    def light(arrays):
        return tuple(a for a in arrays if a.size <= 8 * LANES)

    stages = [None, []]

    def advance(tag, new, after, finish_older=True):
        behind = []
        first = None
        if new:
            first = ([n for n, _ in new], sibling_start([g for _, g in new], f"rs_sibling_start{tag}"))
            behind.append(first[1][3])
        older = stages[1]
        if finish_older:
            for k, (names, started) in enumerate(older):
                parts, others = chips_wait(started, after, f"rs_chips_wait{tag}_{k}")
                behind += [finish(n, p, o) for n, p, o in zip(names, parts, others)]
            older = []
        if stages[0] is not None:
            names, started = stages[0]
            grads, got = sibling_wait(started, after, f"rs_sibling_wait{tag}")
            parts = [add_pairs(g, l, own, f"rs_add_{n}") for n, g, l in zip(names, grads, got)]
            older = older + [(names, chips_start(parts, f"rs_chips_start{tag}"))]
            behind.append(older[-1][1][3])
        stages[:] = [first, older]
        return tuple(behind)

    gf = [None] * depth
    dx, dxb, gf[3] = _ffn_backward(dx, dxb, saved['f3'], ffn_g[3], w_up[3], ffn_cw[3], w_down[3], 3)
    xin, h, z, y = saved['d']
    dy = mm_dx_out(dxb, wd_out, "d_out_dx")
    g_d_w_out = mm_dw_out(y, dxb, "d_out_dw")
    dz, g_d_conv_w = d_mid_bwd(z, dy, d_cw, "d_mid_bwd")
    dx, dxb, g_d_norm_g = mm_dx_in(dz, wd_in, "d_in_dx", norm=(xin, sm['d_norm_g'], dx))
    g_d_w_in = mm_dw_in(h, dz, nb, "d_in_dw")
    behind = advance(0, [('ffn_w_up3', gf[3]['w_up']), ('ffn_w_down3', rows_stacked(gf[3]['w_down'])), ('d_w_in', g_d_w_in),
                         ('d_w_out', rows_stacked(g_d_w_out))], (g_d_norm_g,))

    dx, dxb, gf[2] = _ffn_backward(dx, dxb, saved['f2'], ffn_g[2], w_up[2], ffn_cw[2], w_down[2], 2, behind)
    xin, h, z, y, a_seq, xr, hs = saved['c']
    g_c_w_out = mm_dw_out(y, dxb, "c_out_dw")
    lam_seq, da_seq, dgate, dgate_sum = c_scan_bwd(dxb, wc_out, z, hs, a_seq, "c_scan_bwd")
    dxr, g_c_w_a, g_c_w_i, g_c_b_a, g_c_b_i, g_c_lambda = c_mid_bwd(
        lam_seq, da_seq, xr, c_wa, c_ba, c_wi, c_bi, sm['c_lambda'], "c_mid_bwd")
    dxr_pre, g_c_conv_w, g_c_conv_b, dxr_pre_sum = conv_bwd(dxr, z, 1, c_cw, "c_conv_bwd")
    dz = jnp.concatenate([dgate, dxr_pre], axis=1)
    g_c_b_in = jnp.concatenate([dgate_sum, dxr_pre_sum], axis=1)
    dx, dxb, g_c_norm_g = mm_dx_in(dz, wc_in, "c_in_dx", norm=(xin, sm['c_norm_g'], dx))
    g_c_w_in = mm_dw_in(h, dz, nb, "c_in_dw")
    behind = advance(1, [('ffn_w_up2', gf[2]['w_up']), ('ffn_w_down2', rows_stacked(gf[2]['w_down'])), ('c_w_in', g_c_w_in),
                         ('c_w_out', rows_stacked(g_c_w_out))], (g_c_norm_g,))

    dx, dxb, gf[1] = _ffn_backward(dx, dxb, saved['f1'], ffn_g[1], w_up[1], ffn_cw[1], w_down[1], 1, behind)
    xin, h, z, y = saved['b']
    g_b_w_out = mm_dw_out(y, dxb, "b_out_dw")
    dp, g_b_w_grp, g_b_b_grp, g_b_scale = b_mid_bwd(z, dxb, wb_out, b_wgrp, b_bgrp, sm['b_scale'], "b_mid_bwd")
    dz = b_pool_bwd(dp, "b_pool_bwd")
    dx, dxb, g_b_norm_g = mm_dx_in(dz, wb_in, "b_in_dx", norm=(xin, sm['b_norm_g'], dx))
    g_b_w_in = mm_dw_in(h, dz, 1, "b_in_dw")
    behind = advance(2, [('ffn_w_up1', gf[1]['w_up']), ('ffn_w_down1', rows_stacked(gf[1]['w_down'])),
                         ('b_w_in', rows_stacked(g_b_w_in[0])), ('b_w_out', rows_stacked(g_b_w_out))], (g_b_norm_g,))

    dx, dxb, gf[0] = _ffn_backward(dx, dxb, saved['f0'], ffn_g[0], w_up[0], ffn_cw[0], w_down[0], 0, behind)
    full_small = {
        'b_norm_g': g_b_norm_g, 'b_w_grp': g_b_w_grp[None], 'b_b_grp': g_b_b_grp.reshape(b_b_grp.shape[:2] + (-1,)),
        'b_scale': g_b_scale, 'c_norm_g': g_c_norm_g, 'c_b_in': g_c_b_in, 'c_conv_w': g_c_conv_w[None],
        'c_conv_b': g_c_conv_b, 'c_w_a': g_c_w_a[None], 'c_b_a': g_c_b_a.reshape(c_b_a.shape[:2] + (-1,)),
        'c_w_i': g_c_w_i[None], 'c_b_i': g_c_b_i.reshape(c_b_i.shape[:2] + (-1,)), 'c_lambda': g_c_lambda,
        'd_norm_g': g_d_norm_g, 'd_conv_w': g_d_conv_w[None],
        'ffn_conv_w': jnp.stack([gf[l]['conv_w'].transpose(2, 0, 1, 3).reshape(ffn_conv_w.shape[1], -1) for l in range(depth)])}
    small_grads = _pack_stacked([_to_stacked(full_small[n], SMALL_SHARDED[n]) for n in small_names], small_rows)
    behind = advance(3, [('ffn_w_up0', gf[0]['w_up']), ('ffn_w_down0', rows_stacked(gf[0]['w_down'])), ('small', small_grads)],
                     (gf[0]['norm_g'],))

    xin, h, z, y = saved['a']
    g_a_w_out = mm_dw_out(y, dxb, "a_out_dw")
    behind += advance(4, [('a_w_out', rows_stacked(g_a_w_out))], light(behind))
    dz, g_a_b_in, g_a_v_norm_g, g_a_w_s, g_a_b_s = a_mid_bwd(z, dxb, wa_out, a_v_norm_g, a_w_s[0], a_bst, "a_mid_bwd", behind)
    g_a_w_in = mm_dw_in(h, dz, nb, "a_in_dw")
    behind = advance(5, [('a_w_in', g_a_w_in)], (g_a_b_in,))
    behind = advance('5b', [], (g_a_b_in, *light(behind)), finish_older=False)
    dx, _, g_a_norm_g = mm_dx_in(dz, wa_in, "a_in_dx", behind, norm=(xin, a_norm_g, dx))
    grad_x = dx[None]

    tril = jnp.tril(jnp.ones((A_CHUNK, A_CHUNK), bool))
    repl_full = {
        'a_norm_g': g_a_norm_g, 'a_b_in': g_a_b_in, 'a_v_norm_g': g_a_v_norm_g,
        'a_w_s': jnp.where(tril, g_a_w_s, 0.0)[None], 'a_b_s': g_a_b_s[:, ::LANES].T[None],
        'ffn_norm_g': jnp.concatenate([gf[l]['norm_g'] for l in range(depth)], axis=0),
        'ffn_conv_b': jnp.stack([gf[l]['conv_b'].reshape(-1) for l in range(depth)]), 'final_norm_g': d_final_g.reshape(-1)}
    repl_grads = _pack([repl_full[n] for n in REPLICATED], repl_rows).reshape(N_DEV, repl_rows // N_DEV, LANES)
    behind = advance(6, [('repl', repl_grads)], (g_a_norm_g,))
    behind = advance(7, [], (g_a_norm_g, *light(behind)))
    advance(8, [], (g_a_norm_g, *light(behind)))

    outs = [{}, {}, {}, {}]
    for i, dst in enumerate(outs):
        for n in ('a_w_in', 'a_w_out', 'b_w_in', 'b_w_out', 'c_w_in', 'c_w_out', 'd_w_in', 'd_w_out'):
            dst[n] = updated[n][i][None]
        dst['ffn_w_up'] = jnp.swapaxes(updated['ffn_w_up'][i], 1, 2)
        dst['ffn_w_down'] = updated['ffn_w_down'][i]
        dst.update(zip(small_names, _unpack(updated['small'][i], small_shapes)))
        dst.update(zip(REPLICATED, _unpack(updated['repl'][i], repl_shapes)))
    out_g, out_d, out_m, out_v = outs

    return (loss, grad_x, *[out_g[n] for n in WEIGHTS], *[out_d[n] for n in WEIGHTS], *[out_m[n] for n in WEIGHTS],
            *[out_v[n] for n in WEIGHTS])
```

```python
import functools

import jax
import jax.numpy as jnp
from jax import lax
from jax.experimental import pallas as pl
from jax.experimental.pallas import tpu as pltpu

F32, BF16 = jnp.float32, jnp.bfloat16
MESH_ID = pl.DeviceIdType.MESH
N_DEV = 8
V7X_VMEM_LIMIT_BYTES = 56 << 20
LANES = 128
HALO = 8
POOL_HALO = 16

EPS = 1e-6
A_CHUNK, A_GROUPS = 128, 4
B_WINDOWS = (2, 4, 8, 16)
C_GATE_C = 8.0
ADAM_LR, ADAM_B1, ADAM_B2, ADAM_EPS, ADAM_WD, ADAM_STEP = 0.001, 0.9, 0.999, 1e-08, 0.01, 10

WEIGHTS = ['a_norm_g', 'a_w_in', 'a_b_in', 'a_v_norm_g', 'a_w_s', 'a_b_s', 'a_w_out', 'b_norm_g', 'b_w_in', 'b_w_grp',
           'b_b_grp', 'b_scale', 'b_w_out', 'c_norm_g', 'c_w_in', 'c_b_in', 'c_conv_w', 'c_conv_b', 'c_w_a', 'c_b_a',
           'c_w_i', 'c_b_i', 'c_lambda', 'c_w_out', 'd_norm_g', 'd_w_in', 'd_conv_w', 'd_w_out', 'ffn_norm_g',
           'ffn_w_up', 'ffn_conv_w', 'ffn_conv_b', 'ffn_w_down', 'final_norm_g']
SMALL_SHARDED = {'b_norm_g': 1, 'b_w_grp': 2, 'b_b_grp': 2, 'b_scale': 1, 'c_norm_g': 1, 'c_b_in': 1, 'c_conv_w': 2,
                 'c_conv_b': 1, 'c_w_a': 2, 'c_b_a': 2, 'c_w_i': 2, 'c_b_i': 2, 'c_lambda': 1, 'd_norm_g': 1,
                 'd_conv_w': 2, 'ffn_conv_w': 2}
REPLICATED = ['a_norm_g', 'a_b_in', 'a_v_norm_g', 'a_w_s', 'a_b_s', 'ffn_norm_g', 'ffn_conv_b', 'final_norm_g']


_GELU_C0, _GELU_C1 = 0.7978845608028654, 0.044715


def _gelu(x):
    return 0.5 * x * (1.0 + jnp.tanh(_GELU_C0 * (x + _GELU_C1 * (x * x * x))))


def _gelu_grad(x):
    t = jnp.tanh(_GELU_C0 * (x + _GELU_C1 * (x * x * x)))
    return 0.5 * (1.0 + t) + 0.5 * x * (1.0 - t * t) * (_GELU_C0 * (1.0 + 3.0 * _GELU_C1 * (x * x)))


def _sigmoid(x):
    return jax.nn.sigmoid(x)


def _log1p(x):
    u = 1.0 + x
    return jnp.where(u == 1.0, x, jnp.log(u) * (x / (u - 1.0)))


def _softplus(x):
    return jnp.maximum(x, 0.0) + _log1p(jnp.exp(-jnp.abs(x)))


def _expm1(x):
    poly = x * (1.0 + x * (1 / 2) * (1.0 + x * (1 / 3) * (1.0 + x * (1 / 4) * (1.0 + x * (1 / 5) * (
        1.0 + x * (1 / 6) * (1.0 + x * (1 / 7) * (1.0 + x * (1 / 8))))))))
    return jnp.where(jnp.abs(x) < 0.35, poly, jnp.exp(x) - 1.0)


def _down(xe, s):
    return xe if s == 0 else pltpu.roll(xe, s, 0)


def _up(xe, s):
    return xe if s == 0 else pltpu.roll(xe, xe.shape[0] - s, 0)


def _conv_ext(xe, w, taps):
    y = xe * w[taps - 1:taps]
    for s in range(1, taps):
        y = y + _down(xe, s) * w[taps - 1 - s:taps - s]
    return y


def _acc(ref, val, first):
    @pl.when(first)
    def _():
        ref[...] = val

    @pl.when(jnp.logical_not(first))
    def _():
        ref[...] += val


def _colsum(v):
    return jnp.sum(v, axis=0, keepdims=True)


def _dot(a, b, dims=((1,), (0,))):
    return lax.dot_general(a.astype(BF16), b.astype(BF16), (dims, ((), ())), preferred_element_type=F32)


_NN, _NT, _TN = ((1,), (0,)), ((1,), (1,)), ((0,), (0,))


def _call(body, name, grid, in_specs, out_specs, out_shape, scratch=(), after=()):
    n_in, n_after = len(in_specs), len(after)

    def ordered_body(*refs):
        return body(*refs[:n_in], *refs[n_in + n_after:])

    call = pl.pallas_call(
        ordered_body if n_after else body, name=name, grid=grid,
        in_specs=list(in_specs) + [pl.BlockSpec(memory_space=pl.ANY)] * n_after, out_specs=out_specs,
        out_shape=out_shape, scratch_shapes=list(scratch),
        compiler_params=pltpu.CompilerParams(dimension_semantics=("arbitrary",) * len(grid),
                                             vmem_limit_bytes=V7X_VMEM_LIMIT_BYTES))
    return lambda *args: call(*args, *after)


def _rows(m, t):
    t = min(m, t)
    assert m % t == 0, (m, t)
    return t


def _sds(shape, dtype=F32):
    return jax.ShapeDtypeStruct(tuple(shape), dtype)


def _prev_halo(tm, halo=HALO):
    return lambda i: jnp.maximum(i * (tm // halo) - 1, 0)


def _next_halo(tm, m, halo=HALO):
    return lambda i: jnp.minimum((i + 1) * (tm // halo), m // halo - 1)


def _matmul(name, ins, in_specs, out_shape, o_spec, grid, compute, after=()):
    def body(*refs):
        refs[-1][...] = compute(*refs[:-1]).astype(refs[-1].dtype)

    return _call(body, name, grid, in_specs, o_spec, out_shape, (), after)(*ins)


def mm_in(h, w_st, name, bias=None, stacked_out=False, out_dtype=F32, rows=1024, after=()):
    m, k = h.shape
    nb, _, n = w_st.shape
    tm = _rows(m, rows)
    in_specs = [pl.BlockSpec((tm, k), lambda i, j: (i, 0)), pl.BlockSpec((None, k, n), lambda i, j: (j, 0, 0))]
    if stacked_out:
        out, o_spec = _sds((nb, m, n), out_dtype), pl.BlockSpec((None, tm, n), lambda i, j: (j, i, 0))
    else:
        out, o_spec = _sds((m, nb * n), out_dtype), pl.BlockSpec((tm, n), lambda i, j: (i, j))
    if bias is None:
        return _matmul(name, (h, w_st), in_specs, out, o_spec, (m // tm, nb), lambda a, b: _dot(a[...], b[...]), after)
    in_specs.append(pl.BlockSpec((1, n), lambda i, j: (0, j)))
    return _matmul(name, (h, w_st, bias), in_specs, out, o_spec, (m // tm, nb),
                   lambda a, b, c: _dot(a[...], b[...]) + c[...], after)


def _split_rows(kf):
    g = max(1, kf // 1024)
    return g, kf // g


def _resident(shape):
    return pl.BlockSpec(shape, lambda *_: (0,) * len(shape), pipeline_mode=pl.Buffered(1))


def _norm_rows(xv, g):
    return (xv * lax.rsqrt(jnp.mean(xv * xv, axis=-1, keepdims=True) + EPS) * g).astype(BF16)


def _project_out(y_ref, w_ref, res_ref, g_ref, x_ref, h_ref):
    xv = res_ref[...] + _dot(y_ref[...], w_ref[...])
    x_ref[...] = xv
    h_ref[...] = _norm_rows(xv, g_ref[...])


def _call_projected(body, name, m, tm, in_specs, ins, own_outs, w_out, res, next_g, scratch=()):
    n = w_out.shape[1]
    row = lambda width: pl.BlockSpec((tm, width), lambda i: (i, 0))
    specs = list(in_specs) + [_resident(w_out.shape), row(n), pl.BlockSpec((1, n), lambda i: (0, 0))]
    out_specs = [row(width) for width, _ in own_outs] + [row(n), row(n)]
    out_shape = [_sds((m, width), dtype) for width, dtype in own_outs] + [_sds((m, n)), _sds((m, n), BF16)]
    return _call(body, name, (m // tm,), specs, out_specs, out_shape, scratch)(*ins, w_out, res, next_g)


def mm_out(y, w, res, next_g, name, after=()):
    kf, n = w.shape
    m = y.shape[0]
    tm = _rows(m, 512)
    row = pl.BlockSpec((tm, n), lambda i: (i, 0))

    def body(y_ref, w_ref, res_ref, g_ref, x_ref, h_ref):
        xv = res_ref[...] + _dot(y_ref[...], w_ref[...])
        x_ref[...] = xv
        h_ref[...] = _norm_rows(xv, g_ref[...])

    in_specs = [pl.BlockSpec((tm, kf), lambda i: (i, 0)), _resident((kf, n)), row, pl.BlockSpec((1, n), lambda i: (0, 0))]
    return _call(body, name, (m // tm,), in_specs, [row, row], [_sds((m, n)), _sds((m, n), BF16)], (), after)(y, w, res, next_g)


def mm_dx_in(dz, w_st, name, after=(), norm=None):
    nb, k, n = w_st.shape
    m = dz.shape[-2]
    tm = _rows(m, 512)
    w_spec = _resident((nb, k, n))
    if dz.ndim == 3:
        in_specs = [pl.BlockSpec((None, tm, n), lambda i, r=r: (r, i, 0)) for r in range(nb)] + [w_spec]

        def compute(*refs):
            acc = _dot(refs[0][...], refs[nb][0], _NT)
            for r in range(1, nb):
                acc = acc + _dot(refs[r][...], refs[nb][r], _NT)
            return acc

        ins = (*[dz] * nb, w_st)
    else:
        in_specs = [pl.BlockSpec((tm, nb * n), lambda i: (i, 0)), w_spec]

        def compute(dz_ref, w_ref):
            acc = _dot(dz_ref[:, :n], w_ref[0], _NT)
            for r in range(1, nb):
                acc = acc + _dot(dz_ref[:, r * n:(r + 1) * n], w_ref[r], _NT)
            return acc

        ins = (dz, w_st)
    row = pl.BlockSpec((tm, k), lambda i: (i, 0))
    if norm is None:
        return _matmul(name, ins, in_specs, _sds((m, k)), row, (m // tm,), compute, after)
    n_in = len(in_specs)
    vec = pl.BlockSpec((1, k), lambda i: (0, 0))

    def body(*refs):
        x_ref, g_ref, dr_ref, dx_ref, dxb_ref, dg_ref = refs[n_in:]
        dx, dg = _rms_bwd_math(x_ref[...], g_ref[...], compute(*refs[:n_in]))
        dx = dr_ref[...] + dx
        dx_ref[...] = dx
        dxb_ref[...] = dx.astype(BF16)
        _acc(dg_ref, dg, pl.program_id(0) == 0)

    return _call(body, name, (m // tm,), in_specs + [row, vec, row], [row, row, vec],
                 [_sds((m, k)), _sds((m, k), BF16), _sds((1, k))], (), after)(*ins, *norm)


def mm_dx_out(dout, w, name, groups=None, after=()):
    kf, n = w.shape
    m = dout.shape[0]
    tm = _rows(m, 1024)
    g, k = (groups, kf // groups) if groups else _split_rows(kf)
    in_specs = [pl.BlockSpec((tm, n), lambda i, j: (i, 0)), pl.BlockSpec((None, k, n), lambda i, j: (j, 0, 0))]
    if groups:
        out, o_spec = _sds((g, m, k)), pl.BlockSpec((None, tm, k), lambda i, j: (j, i, 0))
    else:
        out, o_spec = _sds((m, kf)), pl.BlockSpec((tm, k), lambda i, j: (i, j))
    return _matmul(name, (dout, w.reshape(g, k, n)), in_specs, out, o_spec, (m // tm, g),
                   lambda a, b: _dot(a[...], b[...], _NT), after)


def mm_dw_in(h, dz, nb, name, transposed=False):
    m, k = h.shape
    if dz.ndim == 3:
        n = dz.shape[2]
        dz_spec = pl.BlockSpec((None, m, n), lambda j: (j, 0, 0))
    else:
        n = dz.shape[1] // nb
        dz_spec = pl.BlockSpec((m, n), lambda j: (0, j))
    in_specs = [_resident((m, k)), dz_spec]
    if transposed:
        return _matmul(name, (h, dz), in_specs, _sds((nb, n, k), BF16), pl.BlockSpec((None, n, k), lambda j: (j, 0, 0)), (nb,),
                       lambda a, b: _dot(b[...], a[...], _TN))
    return _matmul(name, (h, dz), in_specs, _sds((nb, k, n), BF16), pl.BlockSpec((None, k, n), lambda j: (j, 0, 0)), (nb,),
                   lambda a, b: _dot(a[...], b[...], _TN))


def mm_dw_out(y, dout, name):
    m, n = dout.shape
    if y.ndim == 3:
        g, _, k = y.shape
        y_spec = pl.BlockSpec((None, m, k), lambda j: (j, 0, 0))
    else:
        g, k = _split_rows(y.shape[1])
        y_spec = pl.BlockSpec((m, k), lambda j: (0, j))
    in_specs = [y_spec, _resident((m, n))]
    out = _matmul(name, (y, dout), in_specs, _sds((g, k, n), BF16), pl.BlockSpec((None, k, n), lambda j: (j, 0, 0)), (g,),
                  lambda a, b: _dot(a[...], b[...], _TN))
    return out.reshape(g * k, n)


def rms_fwd(x, g, name):
    m, d = x.shape
    tm = _rows(m, 512)

    def body(x_ref, g_ref, o_ref):
        xv = x_ref[...]
        rstd = lax.rsqrt(jnp.mean(xv * xv, axis=-1, keepdims=True) + EPS)
        o_ref[...] = (xv * rstd * g_ref[...]).astype(BF16)

    row = pl.BlockSpec((tm, d), lambda i: (i, 0))
    vec = pl.BlockSpec((1, d), lambda i: (0, 0))
    return _call(body, name, (m // tm,), [row, vec], row, _sds((m, d), BF16))(x, g)


def _rms_bwd_math(xv, g, dh):
    rstd = lax.rsqrt(jnp.mean(xv * xv, axis=-1, keepdims=True) + EPS)
    xhat = xv * rstd
    dxhat = dh * g
    dx = rstd * (dxhat - xhat * jnp.mean(dxhat * xhat, axis=-1, keepdims=True))
    return dx, _colsum(dh * xhat)


def final_loss(x, g, target, name):
    m, d = x.shape
    tm = _rows(m, 512)

    def body(x_ref, g_ref, t_ref, l_ref, dx_ref, dxb_ref, dg_ref):
        xv, gv = x_ref[...], g_ref[...]
        rstd = lax.rsqrt(jnp.mean(xv * xv, axis=-1, keepdims=True) + EPS)
        err = xv * rstd * gv - t_ref[...]
        part = 0.5 * jnp.sum(jnp.mean(err * err, axis=-1, keepdims=True), axis=0, keepdims=True)
        dx, dg = _rms_bwd_math(xv, gv, err * (1.0 / d))
        dx_ref[...] = dx
        dxb_ref[...] = dx.astype(BF16)
        first = pl.program_id(0) == 0
        _acc(l_ref, jnp.broadcast_to(part, l_ref.shape), first)
        _acc(dg_ref, dg, first)

    row = pl.BlockSpec((tm, d), lambda i: (i, 0))
    vec = pl.BlockSpec((1, d), lambda i: (0, 0))
    lsp = pl.BlockSpec((1, LANES), lambda i: (0, 0))
    return _call(body, name, (m // tm,), [row, vec, row], [lsp, row, row, vec],
                 [_sds((1, LANES)), _sds((m, d)), _sds((m, d), BF16), _sds((1, d))])(x, g, target)


def _a_common(z_ref, vg_ref, ws_ref, bst_ref, tm, width):
    gw = width // A_GROUPS
    zp = z_ref[...]
    z = _gelu(zp)
    u, v = z[:, :width], z[:, width:]
    rstd = lax.rsqrt(jnp.mean(v * v, axis=-1, keepdims=True) + EPS)
    vhat = v * rstd
    vn = vhat * vg_ref[...]
    t_i = lax.broadcasted_iota(jnp.int32, (A_CHUNK, A_CHUNK), 0)
    s_i = lax.broadcasted_iota(jnp.int32, (A_CHUNK, A_CHUNK), 1)
    wsm = [jnp.where(s_i <= t_i, ws_ref[g], 0.0).astype(BF16) for g in range(A_GROUPS)]
    bst = bst_ref[...]
    return zp, u, rstd, vhat, vn.astype(BF16), wsm, bst, gw


def a_mid_fwd(z, vg, ws, bst, name):
    m, w2 = z.shape
    width = w2 // 2
    tm = _rows(m, 256)

    def body(z_ref, vg_ref, ws_ref, bst_ref, y_ref):
        _, u, _, _, vnb, wsm, bst, gw = _a_common(z_ref, vg_ref, ws_ref, bst_ref, tm, width)
        for c in range(tm // A_CHUNK):
            r0 = c * A_CHUNK
            for g in range(A_GROUPS):
                c0 = g * gw
                vs = _dot(wsm[g], vnb[r0:r0 + A_CHUNK, c0:c0 + gw]) + bst[:, g:g + 1]
                y_ref[r0:r0 + A_CHUNK, c0:c0 + gw] = (u[r0:r0 + A_CHUNK, c0:c0 + gw] * vs).astype(BF16)

    in_specs = [pl.BlockSpec((tm, w2), lambda i: (i, 0)), pl.BlockSpec((1, width), lambda i: (0, 0)),
                pl.BlockSpec((A_GROUPS, A_CHUNK, A_CHUNK), lambda i: (0, 0, 0)),
                pl.BlockSpec((A_CHUNK, A_GROUPS), lambda i: (0, 0))]
    return _call(body, name, (m // tm,), in_specs, pl.BlockSpec((tm, width), lambda i: (i, 0)),
                 _sds((m, width), BF16))(z, vg, ws, bst)


def a_mid_bwd(z, dx, w_out, vg, ws, bst, name, after=()):
    m, w2 = z.shape
    width = w2 // 2
    tm = _rows(m, 256)

    def body(z_ref, dx_ref, wo_ref, vg_ref, ws_ref, bst_ref, dz_ref, dbin_ref, dvg_ref, dws_ref, dbs_ref, dvn_scr, du_scr):
        first = pl.program_id(0) == 0
        zp, u, rstd, vhat, vnb, wsm, bst, gw = _a_common(z_ref, vg_ref, ws_ref, bst_ref, tm, width)
        dy = _dot(dx_ref[...], wo_ref[...], _NT)
        dws = [jnp.zeros((A_CHUNK, A_CHUNK), F32) for _ in range(A_GROUPS)]
        dbs = [jnp.zeros((A_CHUNK, 1), F32) for _ in range(A_GROUPS)]
        for c in range(tm // A_CHUNK):
            r0 = c * A_CHUNK
            for g in range(A_GROUPS):
                c0 = g * gw
                vn_cg = vnb[r0:r0 + A_CHUNK, c0:c0 + gw]
                vs = _dot(wsm[g], vn_cg) + bst[:, g:g + 1]
                dy_cg = dy[r0:r0 + A_CHUNK, c0:c0 + gw]
                dvs = dy_cg * u[r0:r0 + A_CHUNK, c0:c0 + gw]
                du_scr[r0:r0 + A_CHUNK, c0:c0 + gw] = dy_cg * vs
                dws[g] = dws[g] + _dot(dvs, vn_cg, _NT)
                dbs[g] = dbs[g] + jnp.sum(dvs, axis=1, keepdims=True)
                dvn_scr[r0:r0 + A_CHUNK, c0:c0 + gw] = _dot(wsm[g], dvs, _TN)
        for g in range(A_GROUPS):
            _acc(dws_ref.at[g], dws[g], first)
            _acc(dbs_ref.at[:, g * LANES:(g + 1) * LANES], jnp.broadcast_to(dbs[g], (A_CHUNK, LANES)), first)
        dvn = dvn_scr[...]
        _acc(dvg_ref, _colsum(dvn * vhat), first)
        dvhat = dvn * vg_ref[...]
        dv = rstd * (dvhat - vhat * jnp.mean(dvhat * vhat, axis=-1, keepdims=True))
        gg = _gelu_grad(zp)
        dzu = du_scr[...] * gg[:, :width]
        dzv = dv * gg[:, width:]
        dz_ref[:, :width] = dzu.astype(BF16)
        dz_ref[:, width:] = dzv.astype(BF16)
        _acc(dbin_ref.at[:, :width], _colsum(dzu), first)
        _acc(dbin_ref.at[:, width:], _colsum(dzv), first)

    const2 = lambda i: (0, 0)
    in_specs = [pl.BlockSpec((tm, w2), lambda i: (i, 0)), pl.BlockSpec((tm, dx.shape[1]), lambda i: (i, 0)),
                _resident(w_out.shape), pl.BlockSpec((1, width), const2),
                pl.BlockSpec((A_GROUPS, A_CHUNK, A_CHUNK), lambda i: (0, 0, 0)), pl.BlockSpec((A_CHUNK, A_GROUPS), const2)]
    out_specs = [pl.BlockSpec((tm, w2), lambda i: (i, 0)), pl.BlockSpec((1, w2), const2), pl.BlockSpec((1, width), const2),
                 pl.BlockSpec((A_GROUPS, A_CHUNK, A_CHUNK), lambda i: (0, 0, 0)),
                 pl.BlockSpec((A_CHUNK, A_GROUPS * LANES), const2)]
    out_shape = [_sds((m, w2), BF16), _sds((1, w2)), _sds((1, width)), _sds((A_GROUPS, A_CHUNK, A_CHUNK)),
                 _sds((A_CHUNK, A_GROUPS * LANES))]
    scratch = [pltpu.VMEM((tm, width), F32), pltpu.VMEM((tm, width), F32)]
    return _call(body, name, (m // tm,), in_specs, out_specs, out_shape, scratch, after)(z, dx, w_out, vg, ws, bst)


def _pool_minus_id(ze, i, tm, gw):
    pos = i * tm + lax.broadcasted_iota(jnp.int32, (tm, 1), 0)
    out = []
    for gi, win in enumerate(B_WINDOWS):
        s = ze[:, gi * gw:(gi + 1) * gw]
        step = 1
        while step < win:
            s = s + _down(s, step)
            step *= 2
        inv = 1.0 / jnp.minimum(pos + 1, win).astype(F32)
        out.append(s[POOL_HALO:] * inv - ze[POOL_HALO:, gi * gw:(gi + 1) * gw])
    return out


def _b_specs(tm, width):
    return [pl.BlockSpec((POOL_HALO, width), lambda i: (_prev_halo(tm, POOL_HALO)(i), 0)),
            pl.BlockSpec((tm, width), lambda i: (i, 0))]


def b_mid_fwd(z, wgrp, bgrp, scale, w_out, res, next_g, name):
    m, width = z.shape
    ng = len(B_WINDOWS)
    gw = width // ng
    tm = _rows(m, 512)

    def body(zp_ref, zm_ref, w_ref, b_ref, s_ref, *proj):
        i = pl.program_id(0)
        y_ref = proj[3]
        ze = jnp.concatenate([zp_ref[...] * (i > 0).astype(F32), zm_ref[...]], axis=0)
        p = _pool_minus_id(ze, i, tm, gw)
        for g in range(ng):
            cs = slice(g * gw, (g + 1) * gw)
            y = (_dot(p[g], w_ref[g]) + b_ref[:, cs]) * s_ref[:, cs]
            y_ref[:, cs] = y.astype(BF16)
        _project_out(y_ref, *proj[:3], *proj[4:])

    vec = pl.BlockSpec((1, width), lambda i: (0, 0))
    in_specs = _b_specs(tm, width) + [pl.BlockSpec((ng, gw, gw), lambda i: (0, 0, 0)), vec, vec]
    return _call_projected(body, name, m, tm, in_specs, (z, z, wgrp, bgrp, scale), [(width, BF16)], w_out, res, next_g)


def b_mid_bwd(z, dx, w_out, wgrp, bgrp, scale, name):
    m, width = z.shape
    ng = len(B_WINDOWS)
    gw = width // ng
    tm = _rows(m, 512)

    def body(zp_ref, zm_ref, dx_ref, wo_ref, w_ref, b_ref, s_ref, dp_ref, dw_ref, db_ref, ds_ref):
        i = pl.program_id(0)
        first = i == 0
        ze = jnp.concatenate([zp_ref[...] * (i > 0).astype(F32), zm_ref[...]], axis=0)
        p = _pool_minus_id(ze, i, tm, gw)
        dy = _dot(dx_ref[...], wo_ref[...], _NT)
        for g in range(ng):
            cs = slice(g * gw, (g + 1) * gw)
            dyg = dy[:, cs]
            ypre = _dot(p[g], w_ref[g]) + b_ref[:, cs]
            dyp = dyg * s_ref[:, cs]
            _acc(ds_ref.at[:, cs], _colsum(dyg * ypre), first)
            _acc(db_ref.at[:, cs], _colsum(dyp), first)
            _acc(dw_ref.at[g], _dot(p[g], dyp, _TN), first)
            dp_ref[:, cs] = _dot(dyp, w_ref[g], _NT)

    vec = pl.BlockSpec((1, width), lambda i: (0, 0))
    row = pl.BlockSpec((tm, width), lambda i: (i, 0))
    wsp = pl.BlockSpec((ng, gw, gw), lambda i: (0, 0, 0))
    in_specs = _b_specs(tm, width) + [pl.BlockSpec((tm, dx.shape[1]), lambda i: (i, 0)), _resident(w_out.shape), wsp, vec, vec]
    return _call(body, name, (m // tm,), in_specs, [row, wsp, vec, vec],
                 [_sds((m, width)), _sds((ng, gw, gw)), _sds((1, width)), _sds((1, width))])(z, z, dx, w_out, wgrp, bgrp, scale)


def b_pool_bwd(dp, name):
    m, width = dp.shape
    gw = width // len(B_WINDOWS)
    tm = _rows(m, 512)
    n_i = m // tm

    def body(dm_ref, dn_ref, dz_ref):
        i = pl.program_id(0)
        de = jnp.concatenate([dm_ref[...], dn_ref[...] * (i < n_i - 1).astype(F32)], axis=0)
        pos = i * tm + lax.broadcasted_iota(jnp.int32, (tm + POOL_HALO, 1), 0)
        for gi, win in enumerate(B_WINDOWS):
            cs = slice(gi * gw, (gi + 1) * gw)
            d = de[:, cs]
            s = d * (1.0 / jnp.minimum(pos + 1, win).astype(F32))
            step = 1
            while step < win:
                s = s + _up(s, step)
                step *= 2
            dz_ref[:, cs] = (s[:tm] - d[:tm]).astype(BF16)

    in_specs = [pl.BlockSpec((tm, width), lambda i: (i, 0)),
                pl.BlockSpec((POOL_HALO, width), lambda i: (_next_halo(tm, m, POOL_HALO)(i), 0))]
    return _call(body, name, (n_i,), in_specs, pl.BlockSpec((tm, width), lambda i: (i, 0)), _sds((m, width), BF16))(dp, dp)


def _c_gates(xr, wa_ref, ba_ref, wi_ref, bi_ref, lam_ref, heads, hw):
    xb = xr.astype(BF16)
    ra = jnp.concatenate([_dot(xb[:, h * hw:(h + 1) * hw], wa_ref[h]) for h in range(heads)], axis=1) + ba_ref[...]
    ia = jnp.concatenate([_dot(xb[:, h * hw:(h + 1) * hw], wi_ref[h]) for h in range(heads)], axis=1) + bi_ref[...]
    r, ig = _sigmoid(ra), _sigmoid(ia)
    sp = _softplus(-lam_ref[...])
    log_a = (-C_GATE_C * r) * sp
    a = jnp.exp(log_a)
    mult = jnp.sqrt(-_expm1(2.0 * log_a))
    return xb, r, ig, sp, a, mult


def c_mid_fwd(z, cw, cb, wa, ba, wi, bi, lam, name):
    m, w2 = z.shape
    width = w2 // 2
    heads, hw = wa.shape[0], wa.shape[1]
    taps = cw.shape[0]
    tm = _rows(m, 512)

    def body(zp_ref, zm_ref, cw_ref, cb_ref, wa_ref, ba_ref, wi_ref, bi_ref, lam_ref, a_ref, b_ref, xr_ref):
        i = pl.program_id(0)
        xe = jnp.concatenate([zp_ref[...] * (i > 0).astype(F32), zm_ref[...]], axis=0)
        xr = _conv_ext(xe, cw_ref[...], taps)[HALO:] + cb_ref[...]
        _, _, ig, _, a, mult = _c_gates(xr, wa_ref, ba_ref, wi_ref, bi_ref, lam_ref, heads, hw)
        a_ref[...] = a
        b_ref[...] = mult * (ig * xr)
        xr_ref[...] = xr

    vec = pl.BlockSpec((1, width), lambda i: (0, 0))
    row = pl.BlockSpec((tm, width), lambda i: (i, 0))
    wsp = pl.BlockSpec((heads, hw, hw), lambda i: (0, 0, 0))
    in_specs = [pl.BlockSpec((HALO, width), lambda i: (_prev_halo(tm)(i), 1)), pl.BlockSpec((tm, width), lambda i: (i, 1)),
                pl.BlockSpec((taps, width), lambda i: (0, 0)), vec, wsp, vec, wsp, vec, vec]
    return _call(body, name, (m // tm,), in_specs, [row, row, row], [_sds((m, width))] * 3)(
        z, z, cw, cb, wa, ba, wi, bi, lam)


_SCAN_ROWS = 512


def c_scan_fwd(a, b, z, w_out, res, next_g, name):
    m, width = a.shape
    tm = _rows(m, _SCAN_ROWS)

    def body(a_ref, b_ref, g_ref, wo_ref, res_ref, ng_ref, hs_ref, y_ref, x_ref, h_ref, h_carry):
        @pl.when(pl.program_id(0) == 0)
        def _():
            h_carry[...] = jnp.zeros_like(h_carry)

        def step(t, h):
            h = a_ref[pl.ds(t, 1), :] * h + b_ref[pl.ds(t, 1), :]
            hs_ref[pl.ds(t, 1), :] = h
            return h

        h_carry[...] = lax.fori_loop(0, tm, step, h_carry[...], unroll=8)
        y_ref[...] = (hs_ref[...] * _gelu(g_ref[...])).astype(BF16)
        _project_out(y_ref, wo_ref, res_ref, ng_ref, x_ref, h_ref)

    row = pl.BlockSpec((tm, width), lambda i: (i, 0))
    return _call_projected(body, name, m, tm, [row, row, row], (a, b, z), [(width, F32), (width, BF16)], w_out, res, next_g,
                           [pltpu.VMEM((1, width), F32)])


def c_scan_bwd(dx, w_out, z, hs, a, name):
    m, width = a.shape
    tm = _rows(m, _SCAN_ROWS)
    n_i = m // tm

    def body(dx_ref, wo_ref, g_ref, hs_ref, hp_ref, a_ref, lam_ref, da_ref, dg_ref, dgs_ref, lam_carry, a_carry):
        i = pl.program_id(0)
        first = i == 0

        @pl.when(first)
        def _():
            lam_carry[...] = jnp.zeros_like(lam_carry)
            a_carry[...] = jnp.zeros_like(a_carry)

        gp, dyv, hsv = g_ref[...], _dot(dx_ref[...], wo_ref[...], _NT), hs_ref[...]
        dgate = dyv * hsv * _gelu_grad(gp)
        dg_ref[...] = dgate.astype(BF16)
        _acc(dgs_ref, _colsum(dgate), first)
        lam_ref[...] = dyv * _gelu(gp)

        def step(k, carry):
            lam_next, a_next = carry
            t = tm - 1 - k
            lam_t = lam_ref[pl.ds(t, 1), :] + a_next * lam_next
            lam_ref[pl.ds(t, 1), :] = lam_t
            return lam_t, a_ref[pl.ds(t, 1), :]

        lam_c, a_c = lax.fori_loop(0, tm, step, (lam_carry[...], a_carry[...]), unroll=8)
        lam_carry[...] = lam_c
        a_carry[...] = a_c
        h_before = hp_ref[HALO - 1:HALO, :] * (i < n_i - 1).astype(F32)
        t_i = lax.broadcasted_iota(jnp.int32, (tm, 1), 0)
        da_ref[...] = lam_ref[...] * jnp.where(t_i == 0, h_before, _down(hsv, 1))

    row = pl.BlockSpec((tm, width), lambda i: (n_i - 1 - i, 0))
    halo = pl.BlockSpec((HALO, width), lambda i: (_prev_halo(tm)(n_i - 1 - i), 0))
    vec = pl.BlockSpec((1, width), lambda i: (0, 0))
    in_specs = [pl.BlockSpec((tm, dx.shape[1]), lambda i: (n_i - 1 - i, 0)), _resident(w_out.shape), row, row, halo, row]
    return _call(body, name, (n_i,), in_specs, [row, row, row, vec],
                 [_sds((m, width)), _sds((m, width)), _sds((m, width), BF16), _sds((1, width))],
                 [pltpu.VMEM((1, width), F32), pltpu.VMEM((1, width), F32)])(dx, w_out, z, hs, hs, a)


def c_mid_bwd(lam_seq, da, xr, wa, ba, wi, bi, lam, name):
    m, width = xr.shape
    heads, hw = wa.shape[0], wa.shape[1]
    tm = _rows(m, 512)

    def body(l_ref, da_ref, xr_ref, wa_ref, ba_ref, wi_ref, bi_ref, lam_ref,
             dxr_ref, dwa_ref, dwi_ref, dba_ref, dbi_ref, dlam_ref):
        first = pl.program_id(0) == 0
        xr_v, lmb = xr_ref[...], l_ref[...]
        xb, r, ig, sp, a, mult = _c_gates(xr_v, wa_ref, ba_ref, wi_ref, bi_ref, lam_ref, heads, hw)
        dmult = lmb * (ig * xr_v)
        dig = lmb * mult * xr_v
        dxr = lmb * mult * ig
        dla = da_ref[...] * a - dmult * (a * a) / mult
        dr = dla * (-C_GATE_C * sp)
        dsp = _colsum(dla * (-C_GATE_C * r))
        _acc(dlam_ref, dsp * (-_sigmoid(-lam_ref[...])), first)
        dra = dr * r * (1.0 - r)
        dia = dig * ig * (1.0 - ig)
        _acc(dba_ref, _colsum(dra), first)
        _acc(dbi_ref, _colsum(dia), first)
        for h in range(heads):
            cs = slice(h * hw, (h + 1) * hw)
            _acc(dwa_ref.at[h], _dot(xb[:, cs], dra[:, cs], _TN), first)
            _acc(dwi_ref.at[h], _dot(xb[:, cs], dia[:, cs], _TN), first)
            dxr_ref[:, cs] = dxr[:, cs] + _dot(dra[:, cs], wa_ref[h], _NT) + _dot(dia[:, cs], wi_ref[h], _NT)

    vec = pl.BlockSpec((1, width), lambda i: (0, 0))
    row = pl.BlockSpec((tm, width), lambda i: (i, 0))
    wsp = pl.BlockSpec((heads, hw, hw), lambda i: (0, 0, 0))
    return _call(body, name, (m // tm,), [row, row, row, wsp, vec, wsp, vec, vec], [row, wsp, wsp, vec, vec, vec],
                 [_sds((m, width)), _sds((heads, hw, hw)), _sds((heads, hw, hw)), _sds((1, width)), _sds((1, width)),
                  _sds((1, width))])(lam_seq, da, xr, wa, ba, wi, bi, lam)


def conv_bwd(dy, x_src, col_block, cw, name):
    m, width = dy.shape
    taps = cw.shape[0]
    tm = _rows(m, 512)
    n_i = m // tm

    def body(dm_ref, dn_ref, xp_ref, xm_ref, cw_ref, dx_ref, dw_ref, db_ref, dxs_ref):
        i = pl.program_id(0)
        first = i == 0
        de = jnp.concatenate([jnp.zeros((HALO, width), F32), dm_ref[...], dn_ref[...] * (i < n_i - 1).astype(F32)], axis=0)
        xe = jnp.concatenate([xp_ref[...] * (i > 0).astype(F32), xm_ref[...], jnp.zeros((HALO, width), F32)], axis=0)
        w = cw_ref[...]
        dx = de * w[taps - 1:taps]
        for s in range(1, taps):
            dx = dx + _up(de, s) * w[taps - 1 - s:taps - s]
        dx_ref[...] = dx[HALO:HALO + tm].astype(BF16)
        _acc(dxs_ref, _colsum(dx[HALO:HALO + tm]), first)
        dm = dm_ref[...]
        for s in range(taps):
            _acc(dw_ref.at[taps - 1 - s:taps - s, :], _colsum(dm * _down(xe, s)[HALO:HALO + tm]), first)
        _acc(db_ref, _colsum(dm), first)

    row = pl.BlockSpec((tm, width), lambda i: (i, 0))
    vec = pl.BlockSpec((1, width), lambda i: (0, 0))
    tsp = pl.BlockSpec((taps, width), lambda i: (0, 0))
    in_specs = [row, pl.BlockSpec((HALO, width), lambda i: (_next_halo(tm, m)(i), 0)),
                pl.BlockSpec((HALO, width), lambda i: (_prev_halo(tm)(i), col_block)),
                pl.BlockSpec((tm, width), lambda i: (i, col_block)), tsp]
    return _call(body, name, (n_i,), in_specs, [row, tsp, vec, vec],
                 [_sds((m, width), BF16), _sds((taps, width)), _sds((1, width)), _sds((1, width))])(dy, dy, x_src, x_src, cw)


def d_mid_fwd(z, cw, w_out, res, next_g, name):
    m, w3 = z.shape
    width = w3 // 3
    taps = cw.shape[0]
    tm = _rows(m, 512)

    def body(bm_ref, cp_ref, cm_ref, xp_ref, xm_ref, cw_ref, wo_ref, res_ref, ng_ref, y_ref, x_ref, h_ref):
        keep = (pl.program_id(0) > 0).astype(F32)
        qe = (jnp.concatenate([cp_ref[...] * keep, cm_ref[...]], axis=0)
              * jnp.concatenate([xp_ref[...], xm_ref[...]], axis=0))
        y_ref[...] = (bm_ref[...] * _conv_ext(qe, cw_ref[...], taps)[HALO:]).astype(BF16)
        _project_out(y_ref, wo_ref, res_ref, ng_ref, x_ref, h_ref)

    main = lambda c: pl.BlockSpec((tm, width), lambda i: (i, c))
    prev = lambda c: pl.BlockSpec((HALO, width), lambda i: (_prev_halo(tm)(i), c))
    in_specs = [main(0), prev(1), main(1), prev(2), main(2), pl.BlockSpec((taps, width), lambda i: (0, 0))]
    return _call_projected(body, name, m, tm, in_specs, (z, z, z, z, z, cw), [(width, BF16)], w_out, res, next_g)


def d_mid_bwd(z, dy, cw, name):
    m, w3 = z.shape
    width = w3 // 3
    taps = cw.shape[0]
    tm = _rows(m, 512)
    n_i = m // tm

    def body(bm_ref, bn_ref, cp_ref, cm_ref, cn_ref, xp_ref, xm_ref, xn_ref, dm_ref, dn_ref, cw_ref, dz_ref, dw_ref):
        i = pl.program_id(0)
        first = i == 0
        kp, kn = (i > 0).astype(F32), (i < n_i - 1).astype(F32)
        zeros = jnp.zeros((HALO, width), F32)
        ce = jnp.concatenate([cp_ref[...] * kp, cm_ref[...], cn_ref[...] * kn], axis=0)
        xe = jnp.concatenate([xp_ref[...], xm_ref[...], xn_ref[...]], axis=0)
        qe = ce * xe
        be = jnp.concatenate([zeros, bm_ref[...], bn_ref[...]], axis=0)
        dye = jnp.concatenate([zeros, dm_ref[...], dn_ref[...] * kn], axis=0)
        w = cw_ref[...]
        cq = _conv_ext(qe, w, taps)
        dcq = dye * be
        dq = dcq * w[taps - 1:taps]
        for s in range(1, taps):
            dq = dq + _up(dcq, s) * w[taps - 1 - s:taps - s]
        ms = slice(HALO, HALO + tm)
        dz_ref[:, :width] = (dye * cq)[ms].astype(BF16)
        dz_ref[:, width:2 * width] = (dq * xe)[ms].astype(BF16)
        dz_ref[:, 2 * width:] = (dq * ce)[ms].astype(BF16)
        for s in range(taps):
            _acc(dw_ref.at[taps - 1 - s:taps - s, :], _colsum(dcq[ms] * _down(qe, s)[ms]), first)

    main = lambda c: pl.BlockSpec((tm, width), lambda i: (i, c))
    prev = lambda c: pl.BlockSpec((HALO, width), lambda i: (_prev_halo(tm)(i), c))
    nxt = lambda c: pl.BlockSpec((HALO, width), lambda i: (_next_halo(tm, m)(i), c))
    tsp = pl.BlockSpec((taps, width), lambda i: (0, 0))
    in_specs = [main(0), nxt(0), prev(1), main(1), nxt(1), prev(2), main(2), nxt(2), main(0), nxt(0), tsp]
    return _call(body, name, (n_i,), in_specs, [pl.BlockSpec((tm, w3), lambda i: (i, 0)), tsp],
                 [_sds((m, w3), BF16), _sds((taps, width))])(z, z, z, z, z, z, z, z, dy, dy, cw)


def _halo_rows(dtype):
    return HALO * (4 // jnp.dtype(dtype).itemsize)


def ffn_gate_down(z, cw, cb, w_down, res, next_g, name):
    _, nj, m, c = z.shape
    n = w_down.shape[1]
    taps = cw.shape[2]
    tm = _rows(m, 256)
    hz = _halo_rows(z.dtype)

    def body(zp_ref, zm_ref, cw_ref, cb_ref, w_ref, res_ref, *rest):
        keep = (pl.program_id(0) > 0).astype(F32)
        zc_ref, act_ref, x_ref = rest[-4:-1] if next_g is not None else rest[-3:]
        acc = res_ref[...]
        for j in range(nj):
            zc = []
            for s in range(2):
                xe = jnp.concatenate([zp_ref[s, j].astype(F32) * keep, zm_ref[s, j].astype(F32)], axis=0)
                zc.append(_conv_ext(xe, cw_ref[s, j], taps)[hz:] + cb_ref[s, j])
                zc_ref[s, j] = zc[s].astype(BF16)
            act = (zc[0] * _sigmoid(zc[0]) * zc[1]).astype(BF16)
            act_ref[j] = act
            acc = acc + _dot(act, w_ref[j])
        x_ref[...] = acc
        if next_g is not None:
            rest[-1][...] = _norm_rows(acc, rest[0][...])

    row = pl.BlockSpec((tm, n), lambda i: (i, 0))
    in_specs = [pl.BlockSpec((2, nj, hz, c), lambda i: (0, 0, _prev_halo(tm, hz)(i), 0)),
                pl.BlockSpec((2, nj, tm, c), lambda i: (0, 0, i, 0)),
                _resident(cw.shape), _resident(cb.shape), _resident((nj, c, n)), row]
    out_specs = [pl.BlockSpec((2, nj, tm, c), lambda i: (0, 0, i, 0)), pl.BlockSpec((nj, tm, c), lambda i: (0, i, 0)), row]
    out_shape = [_sds((2, nj, m, c), BF16), _sds((nj, m, c), BF16), _sds((m, n))]
    ins = [z, z, cw, cb, w_down.reshape(nj, c, n), res]
    if next_g is not None:
        in_specs.append(pl.BlockSpec((1, n), lambda i: (0, 0)))
        out_specs.append(row)
        out_shape.append(_sds((m, n), BF16))
        ins.append(next_g)
    return _call(body, name, (m // tm,), in_specs, out_specs, out_shape)(*ins)


def ffn_gate_bwd(z, zc, dx, w_down, cw, name, after=()):
    _, nj, m, c = z.shape
    n = w_down.shape[1]
    taps = cw.shape[2]
    tm = _rows(m, 256)
    n_i = m // tm
    hz = _halo_rows(z.dtype)
    assert _halo_rows(dx.dtype) == hz and zc.dtype == z.dtype, (z.dtype, zc.dtype, dx.dtype)

    def body(zp_ref, zm_ref, zn_ref, cm_ref, cn_ref, dm_ref, dn_ref, wd_ref, cw_ref, dz_ref, dw_ref, db_ref):
        i = pl.program_id(1)
        first = i == 0
        kp, kn = (i > 0).astype(F32), (i < n_i - 1).astype(F32)
        xe = [jnp.concatenate([zp_ref[s].astype(F32) * kp, zm_ref[s].astype(F32), zn_ref[s].astype(F32) * kn], axis=0)
              for s in range(2)]
        zc = [jnp.concatenate([jnp.zeros((hz, c), F32), cm_ref[s].astype(F32), cn_ref[s].astype(F32)], axis=0)
              for s in range(2)]
        dact = _dot(jnp.concatenate([dm_ref[...], dn_ref[...]], axis=0), wd_ref[...], _NT)
        dae = jnp.concatenate([jnp.zeros((hz, c), F32), dact[:tm], dact[tm:] * kn], axis=0)
        sg = _sigmoid(zc[0])
        dzc = [dae * zc[1] * (sg * (1.0 + zc[0] * (1.0 - sg))), dae * (zc[0] * sg)]
        ms = slice(hz, hz + tm)
        for s in range(2):
            w = cw_ref[s]
            ups = [dzc[s]] + [_up(dzc[s], u) for u in range(1, taps)]
            dxs = ups[0] * w[taps - 1:taps]
            for u in range(1, taps):
                dxs = dxs + ups[u] * w[taps - 1 - u:taps - u]
            dz_ref[s] = dxs[ms].astype(BF16)
            tail = dzc[s][hz + tm:]
            x_end = xe[s][tm:]
            for u in range(taps):
                total = _colsum(ups[u] * xe[s]) - _colsum(tail * _down(x_end, u)[hz:])
                _acc(dw_ref.at[s, taps - 1 - u:taps - u, :], total, first)
            _acc(db_ref.at[s], _colsum(dzc[s][ms]), first)

    in_specs = [pl.BlockSpec((2, None, hz, c), lambda j, i: (0, j, _prev_halo(tm, hz)(i), 0)),
                pl.BlockSpec((2, None, tm, c), lambda j, i: (0, j, i, 0)),
                pl.BlockSpec((2, None, hz, c), lambda j, i: (0, j, _next_halo(tm, m, hz)(i), 0)),
                pl.BlockSpec((2, None, tm, c), lambda j, i: (0, j, i, 0)),
                pl.BlockSpec((2, None, hz, c), lambda j, i: (0, j, _next_halo(tm, m, hz)(i), 0)),
                pl.BlockSpec((tm, n), lambda j, i: (i, 0)),
                pl.BlockSpec((hz, n), lambda j, i: (_next_halo(tm, m, hz)(i), 0)),
                pl.BlockSpec((None, c, n), lambda j, i: (j, 0, 0)),
                pl.BlockSpec((2, None, taps, c), lambda j, i: (0, j, 0, 0))]
    out_specs = [pl.BlockSpec((2, None, tm, c), lambda j, i: (0, j, i, 0)),
                 pl.BlockSpec((2, None, taps, c), lambda j, i: (0, j, 0, 0)),
                 pl.BlockSpec((2, None, 1, c), lambda j, i: (0, j, 0, 0))]
    return _call(body, name, (nj, n_i), in_specs, out_specs,
                 [_sds((2, nj, m, c), BF16), _sds((2, nj, taps, c)), _sds((2, nj, 1, c))], (), after)(
        z, z, z, zc, zc, dx, dx, w_down.reshape(nj, c, n), cw)


_STREAM_TILE_BYTES = 2 << 20


def _tile_rows(r, c):
    if r * c * 4 <= _STREAM_TILE_BYTES:
        return r
    fits = [d for d in range(16, r, 16) if r % d == 0 and d * c * 4 <= _STREAM_TILE_BYTES]
    return max(fits) if fits else r


def _as2d(a, lead):
    shape = a.shape
    return a.reshape((lead, -1, shape[-1]) if lead else (-1, shape[-1]))


def add_pairs(g, l1, own, name):
    shape = l1.shape
    g3, l3 = _as2d(g, N_DEV), _as2d(l1, 4)
    _, r, c = l3.shape
    tr = _tile_rows(r, c)

    def body(own_ref, a_ref, b_ref, o_ref):
        o_ref[...] = (a_ref[...].astype(F32) + b_ref[...].astype(F32)).astype(o_ref.dtype)

    spec = pl.BlockSpec((None, tr, c), lambda k, i, own_ref: (k, i, 0))
    grid_spec = pltpu.PrefetchScalarGridSpec(
        num_scalar_prefetch=1, grid=(4, r // tr),
        in_specs=[pl.BlockSpec((None, tr, c), lambda k, i, own_ref: (own_ref[k], i, 0)), spec], out_specs=spec)
    out = pl.pallas_call(
        body, name=name, grid_spec=grid_spec, out_shape=_sds(l3.shape, l1.dtype),
        compiler_params=pltpu.CompilerParams(dimension_semantics=("arbitrary", "arbitrary"),
                                             vmem_limit_bytes=V7X_VMEM_LIMIT_BYTES))(own, g3, l3)
    return out.reshape(shape)


def _grad_sum(p_ref, l_ref):
    return ((p_ref[...].astype(F32) + l_ref[0].astype(F32)) + l_ref[1].astype(F32)) + l_ref[2].astype(F32)


def sum_parts(p, l2, name):
    _, r, c = p.shape

    def body(p_ref, l_ref, o_ref):
        o_ref[...] = _grad_sum(p_ref, l_ref)

    return _call(body, name, (1,), [pl.BlockSpec((None, r, c), lambda i: (0, 0, 0)), pl.BlockSpec((3, r, c), lambda i: (0, 0, 0))],
                 pl.BlockSpec((r, c), lambda i: (0, 0)), _sds((r, c)))(p, l2)


def _adamw_math(w, g, m, v):
    m = ADAM_B1 * m + (1.0 - ADAM_B1) * g
    v = ADAM_B2 * v + (1.0 - ADAM_B2) * (g * g)
    m_hat = m / (1.0 - ADAM_B1 ** ADAM_STEP)
    v_hat = v / (1.0 - ADAM_B2 ** ADAM_STEP)
    delta = -ADAM_LR * (m_hat / (jnp.sqrt(v_hat) + ADAM_EPS) + ADAM_WD * w)
    return delta, m, v


def adamw(w, m, v, name, g=None, p=None, l2=None):
    shape = w.shape
    w2, m2, v2 = (_as2d(t, 0) for t in (w, m, v))
    r, c = w2.shape
    tr = _tile_rows(r, c)
    row = pl.BlockSpec((tr, c), lambda i: (i, 0))
    if g is None:
        p3, l3 = _as2d(p, 4), _as2d(l2, 3)
        gin = (p3, l3)
        gspecs = [pl.BlockSpec((None, tr, c), lambda i: (0, i, 0)), pl.BlockSpec((3, tr, c), lambda i: (0, i, 0))]
    else:
        gin, gspecs = (_as2d(g, 0),), [row]

    def body(*refs):
        n_g = len(gin)
        w_ref, m_ref, v_ref, g_ref, d_ref, nm_ref, nv_ref = refs[n_g:]
        grad = refs[0][...] if n_g == 1 else _grad_sum(refs[0], refs[1])
        delta, nm, nv = _adamw_math(w_ref[...], grad, m_ref[...], v_ref[...])
        g_ref[...] = grad
        d_ref[...] = delta
        nm_ref[...] = nm
        nv_ref[...] = nv

    outs = _call(body, name, (r // tr,), gspecs + [row, row, row], [row] * 4, [_sds((r, c))] * 4)(*gin, w2, m2, v2)
    return tuple(o.reshape(shape) for o in outs)


def adamw_layer(w, m, v, layer, prev, name, p, l2):
    n_l, r, c = w.shape
    tr = _tile_rows(r, c)
    slab = pl.BlockSpec((None, tr, c), lambda i: (layer, i, 0))
    in_specs = [pl.BlockSpec((None, tr, c), lambda i: (0, i, 0)), pl.BlockSpec((3, tr, c), lambda i: (0, i, 0)), slab, slab, slab]
    n_in = len(in_specs)
    prev = () if prev is None else tuple(prev)

    def body(p_ref, l_ref, w_ref, m_ref, v_ref, *rest):
        g_ref, d_ref, nm_ref, nv_ref = rest[len(prev):]
        grad = _grad_sum(p_ref, l_ref)
        delta, nm, nv = _adamw_math(w_ref[...], grad, m_ref[...], v_ref[...])
        g_ref[...] = grad
        d_ref[...] = delta
        nm_ref[...] = nm
        nv_ref[...] = nv

    return pl.pallas_call(
        body, name=name, grid=(r // tr,), in_specs=in_specs + [pl.BlockSpec(memory_space=pl.ANY)] * len(prev),
        out_specs=[slab] * 4, out_shape=[_sds((n_l, r, c))] * 4,
        input_output_aliases={n_in + q: q for q in range(len(prev))},
        compiler_params=pltpu.CompilerParams(dimension_semantics=("arbitrary",), vmem_limit_bytes=V7X_VMEM_LIMIT_BYTES),
    )(_as2d(p, 4), _as2d(l2, 3), w, m, v, *prev)


def _comm_call(body, name, ins, out_shape, n_sems):
    any_spec = pl.BlockSpec(memory_space=pl.ANY)
    return pl.pallas_call(
        body, name=name, in_specs=[any_spec] * len(ins), out_specs=[any_spec] * len(out_shape), out_shape=out_shape,
        scratch_shapes=[pltpu.SemaphoreType.DMA((n,)) for n in n_sems],
        compiler_params=pltpu.CompilerParams(has_side_effects=True))(*ins)


def _place():
    return lax.axis_index("x"), lax.axis_index("y"), lax.axis_index("c")


def _dev_index(px, py, pc):
    return 4 * px + 2 * py + pc


def all_gather(blocks, name):
    n_t = len(blocks)

    def body(*refs):
        ins, outs = refs[:n_t], refs[n_t:2 * n_t]
        send_sems, recv_sems, local_sems = refs[2 * n_t:]
        x, y, c = _place()
        me, sibling = (x, y, c), (x, y, 1 - c)
        chips = [(1 - x, y), (x, 1 - y), (1 - x, 1 - y)]

        def copy(t, k, block, to, src=None):
            dst = outs[t].at[_dev_index(*block)]
            return pltpu.make_async_remote_copy(
                src_ref=dst if src is None else src, dst_ref=dst, send_sem=send_sems.at[t * 7 + k],
                recv_sem=recv_sems.at[t * 7 + k], device_id=to, device_id_type=MESH_ID)

        mine = [pltpu.make_async_copy(ins[t], outs[t].at[_dev_index(*me)], local_sems.at[t]) for t in range(n_t)]
        for cp in mine:
            cp.start()
        first = []
        for t in range(n_t):
            first.append(copy(t, 0, me, sibling, src=ins[t]))
            first += [copy(t, 1 + j, me, (*chip, c), src=ins[t]) for j, chip in enumerate(chips)]
        for cp in first:
            cp.start()
        passed = []
        for t in range(n_t):
            for j, chip in enumerate(chips):
                copy(t, 1 + j, (*chip, c), me).wait_recv()
                cp = copy(t, 4 + j, (*chip, c), sibling)
                cp.start()
                passed.append(cp)
        for t in range(n_t):
            copy(t, 0, sibling, me).wait_recv()
            for j, chip in enumerate(chips):
                copy(t, 4 + j, (*chip, 1 - c), me).wait_recv()
        for cp in first + passed:
            cp.wait_send()
        for cp in mine:
            cp.wait()

    out_shape = [_sds((N_DEV,) + b.shape, b.dtype) for b in blocks]
    return _comm_call(body, name, blocks, out_shape, (7 * n_t, 7 * n_t, n_t))


def _chip_of(x, y, k):
    return (x if k % 2 == 0 else 1 - x), (y if k // 2 == 0 else 1 - y)


_HBM_SPEC = pl.BlockSpec(memory_space=pltpu.HBM)
_SEM_SPEC = pl.BlockSpec(memory_space=pltpu.SEMAPHORE)
_DATAFLOW = pltpu.SideEffectType.DATAFLOW_SIDE_EFFECTING


def _in_hbm(a):
    return pltpu.with_memory_space_constraint(a, pltpu.HBM)


def _split_start(name, issue, srcs, land_shapes, sem_counts, after=()):
    n_buf, n_sem, n_after = len(srcs) + len(land_shapes), len(sem_counts), len(after)

    def body(*refs):
        issue(refs[:len(srcs)], refs[len(srcs):n_buf], refs[n_buf + n_after:n_buf + n_after + n_sem])
        refs[-1][...] = jnp.zeros_like(refs[-1])

    bufs = [pltpu.HBM(s.shape, s.dtype) for s in list(srcs) + list(land_shapes)]
    outs = pl.pallas_call(
        body, name=name, in_specs=(*[_HBM_SPEC] * n_buf, *[pl.BlockSpec(memory_space=pl.ANY)] * n_after),
        out_shape=(*[pltpu.SemaphoreType.DMA((n,)) for n in sem_counts], *bufs, _sds((8, LANES))),
        out_specs=(*[_SEM_SPEC] * n_sem, *[_HBM_SPEC] * n_buf, pl.BlockSpec(memory_space=pltpu.VMEM)),
        input_output_aliases={i: n_sem + i for i in range(n_buf)},
        compiler_params=pltpu.CompilerParams(has_side_effects=_DATAFLOW),
    )(*[_in_hbm(s) for s in srcs], *[_in_hbm(lax.empty(s.shape, s.dtype)) for s in land_shapes], *after)
    return outs[:n_sem], outs[n_sem:n_sem + len(srcs)], outs[n_sem + len(srcs):n_sem + n_buf], outs[-1]


def _split_wait(name, finish, sems, srcs, lands, after):
    n_buf, n_sem = len(srcs) + len(lands), len(sems)

    def body(*refs):
        finish(refs[:len(srcs)], refs[len(srcs):n_buf], refs[n_buf:n_buf + n_sem])

    bufs = [pltpu.HBM(s.shape, s.dtype) for s in list(srcs) + list(lands)]
    outs = pl.pallas_call(
        body, name=name, in_specs=(*[_HBM_SPEC] * n_buf, *[_SEM_SPEC] * n_sem, *[pl.BlockSpec(memory_space=pl.ANY)] * len(after)),
        out_shape=tuple(bufs), out_specs=(_HBM_SPEC,) * n_buf, input_output_aliases={i: i for i in range(n_buf)},
        compiler_params=pltpu.CompilerParams(has_side_effects=_DATAFLOW),
    )(*srcs, *lands, *sems, *after)
    return outs[:len(srcs)], outs[len(srcs):]


def _peer(x, y, c, r):
    return (1 - x if r & 4 else x), (1 - y if r & 2 else y), (1 - c if r & 1 else c)


ALL_PEERS = tuple(range(1, N_DEV))
SAME_CORE_PEERS = (2, 4, 6)


def _gather_copies(src_refs, land_refs, sem_refs, arrivals, peers):
    send_sems, recv_sems, local_sems = sem_refs
    x, y, c = _place()
    me = _dev_index(x, y, c)
    local, sends, recvs = [], [], []
    for j, (src, land) in enumerate(zip(src_refs, land_refs)):
        local.append(pltpu.make_async_copy(src, land.at[me], local_sems.at[j]))
        for p, r in enumerate(peers):
            peer = _peer(x, y, c, r)
            q = len(peers) * j + p
            sends.append(pltpu.make_async_remote_copy(src_ref=src, dst_ref=land.at[me], send_sem=send_sems.at[q],
                                                      recv_sem=recv_sems.at[q], device_id=peer, device_id_type=MESH_ID))
            if arrivals:
                recvs.append(pltpu.make_async_remote_copy(
                    src_ref=src, dst_ref=land.at[_dev_index(*peer)], send_sem=send_sems.at[q], recv_sem=recv_sems.at[q],
                    device_id=peer, device_id_type=MESH_ID))
    return local, sends, recvs


def gather_start(groups, peers, name, after=()):
    flat = [b for g in groups for b in g]
    bounds = [sum(len(g) for g in groups[:i]) for i in range(len(groups) + 1)]

    def issue(src_refs, land_refs, sem_refs):
        for i in range(len(groups)):
            lo, hi = bounds[i], bounds[i + 1]
            local, sends, _ = _gather_copies(src_refs[lo:hi], land_refs[lo:hi], sem_refs[3 * i:3 * i + 3], False, peers[i])
            for cp in local + sends:
                cp.start()

    sem_counts = [n for g, p in zip(groups, peers) for n in (len(p) * len(g), len(p) * len(g), len(g))]
    sems, srcs, lands, _ = _split_start(name, issue, flat, [_sds((N_DEV,) + b.shape, b.dtype) for b in flat], sem_counts, after)
    return [(sems[3 * i:3 * i + 3], srcs[bounds[i]:bounds[i + 1]], lands[bounds[i]:bounds[i + 1]], peers[i])
            for i in range(len(groups))]


def gather_wait(group, after, name):
    sems, srcs, lands, peers = group

    def finish(src_refs, land_refs, sem_refs):
        local, sends, recvs = _gather_copies(src_refs, land_refs, sem_refs, True, peers)
        for cp in local:
            cp.wait()
        for cp in recvs:
            cp.wait_recv()
        for cp in sends:
            cp.wait_send()

    return _split_wait(name, finish, sems, srcs, lands, after)[1]


def _forward_copies(land_refs, sem_refs, arrivals):
    send_sems, recv_sems = sem_refs
    x, y, c = _place()
    sends, recvs = [], []
    for t, land in enumerate(land_refs):
        for k in range(4):
            cx, cy = _chip_of(x, y, k)
            mine, theirs = land.at[_dev_index(cx, cy, c)], land.at[_dev_index(cx, cy, 1 - c)]
            sems = dict(send_sem=send_sems.at[4 * t + k], recv_sem=recv_sems.at[4 * t + k], device_id=(x, y, 1 - c),
                        device_id_type=MESH_ID)
            sends.append(pltpu.make_async_remote_copy(src_ref=mine, dst_ref=mine, **sems))
            if arrivals:
                recvs.append(pltpu.make_async_remote_copy(src_ref=theirs, dst_ref=theirs, **sems))
    return sends, recvs


def forward_start(lands, name):
    def issue(land_refs, _, sem_refs):
        for cp in _forward_copies(land_refs, sem_refs, False)[0]:
            cp.start()

    n = 4 * len(lands)
    sems, lands, _, token = _split_start(name, issue, lands, [], (n, n))
    return sems, lands, token


def forward_wait(started, after, name):
    sems, lands, _ = started

    def finish(land_refs, _, sem_refs):
        sends, recvs = _forward_copies(land_refs, sem_refs, True)
        for cp in recvs:
            cp.wait_recv()
        for cp in sends:
            cp.wait_send()

    return _split_wait(name, finish, sems, lands, [], after)[0]


def _sibling_copies(src_refs, land_refs, sem_refs):
    send_sems, recv_sems = sem_refs
    x, y, c = _place()
    copies = []
    for t, (src, land) in enumerate(zip(src_refs, land_refs)):
        for k in range(4):
            cx, cy = _chip_of(x, y, k)
            copies.append(pltpu.make_async_remote_copy(
                src_ref=src.at[_dev_index(cx, cy, 1 - c)], dst_ref=land.at[k], send_sem=send_sems.at[4 * t + k],
                recv_sem=recv_sems.at[4 * t + k], device_id=(x, y, 1 - c), device_id_type=MESH_ID))
    return copies


def _chip_copies(src_refs, land_refs, sem_refs):
    send_sems, recv_sems = sem_refs
    x, y, c = _place()
    copies = []
    for t, (src, land) in enumerate(zip(src_refs, land_refs)):
        for k in range(1, 4):
            cx, cy = _chip_of(x, y, k)
            copies.append(pltpu.make_async_remote_copy(
                src_ref=src.at[k], dst_ref=land.at[k - 1], send_sem=send_sems.at[3 * t + k - 1],
                recv_sem=recv_sems.at[3 * t + k - 1], device_id=(cx, cy, c), device_id_type=MESH_ID))
    return copies


def _exchange_start(copies_of, n_land, per_array, arrays, name):
    def issue(src_refs, land_refs, sem_refs):
        for cp in copies_of(src_refs, land_refs, sem_refs):
            cp.start()

    n = per_array * len(arrays)
    lands = [_sds((n_land,) + a.shape[1:], a.dtype) for a in arrays]
    return _split_start(name, issue, arrays, lands, (n, n))


def _exchange_wait(copies_of, started, after, name):
    sems, srcs, lands, _ = started

    def finish(src_refs, land_refs, sem_refs):
        copies = copies_of(src_refs, land_refs, sem_refs)
        for cp in copies:
            cp.wait_recv()
        for cp in copies:
            cp.wait_send()

    return _split_wait(name, finish, sems, srcs, lands, after)


def sibling_start(grads, name):
    return _exchange_start(_sibling_copies, 4, 4, grads, name)


def sibling_wait(started, after, name):
    return _exchange_wait(_sibling_copies, started, after, name)


def chips_start(parts, name):
    return _exchange_start(_chip_copies, 3, 3, parts, name)


def chips_wait(started, after, name):
    return _exchange_wait(_chip_copies, started, after, name)


def _pack(arrays, rows):
    flat = jnp.concatenate([a.reshape(-1) for a in arrays])
    return jnp.pad(flat, (0, rows * LANES - flat.shape[0])).reshape(rows, LANES)


def _pack_stacked(arrays, rows):
    flat = jnp.concatenate([a.reshape(N_DEV, -1) for a in arrays], axis=1)
    return jnp.pad(flat, ((0, 0), (0, rows * LANES - flat.shape[1]))).reshape(N_DEV, rows, LANES)


def _unpack(buf, shapes, lead=()):
    flat = buf.reshape(lead + (-1,))
    out, off = [], 0
    for s in shapes:
        n = 1
        for d in s:
            n *= d
        out.append(flat[..., off:off + n].reshape(lead + tuple(s)))
        off += n
    return out


def _padded_rows(shapes, multiple):
    n = sum(functools.reduce(lambda a, b: a * b, s, 1) for s in shapes)
    rows = -(-n // LANES)
    return -(-rows // multiple) * multiple


def _to_full(stacked, axis):
    t = jnp.moveaxis(stacked, 0, axis)
    return t.reshape(t.shape[:axis] + (t.shape[axis] * t.shape[axis + 1],) + t.shape[axis + 2:])


def _to_stacked(full, axis):
    s = full.shape
    t = full.reshape(s[:axis] + (N_DEV, s[axis] // N_DEV) + s[axis + 1:])
    return jnp.moveaxis(t, axis, 0)


def _ffn_forward(x, h, w_up, cw, cb, w_down, next_g, tag, after=()):
    z = mm_in(h, w_up, f"ffn{tag}_up", stacked_out=True, out_dtype=BF16, rows=2048, after=after)
    nb, m, c = z.shape
    z4 = z.reshape(2, nb // 2, m, c)
    zc, act, *out = ffn_gate_down(z4, cw, cb, w_down, x, next_g, f"ffn{tag}_gate_down")
    return (out[0], out[1] if next_g is not None else None), (x, h, z4, zc, act)


def _ffn_backward(dx, dxb, saved, norm_g, w_up, cw, w_down, tag, after=()):
    x, h, z4, zc, act = saved
    nj = act.shape[0]
    dz4, dcw, dcb = ffn_gate_bwd(z4, zc, dxb, w_down, cw, f"ffn{tag}_gate_bwd", after)
    dw_down = mm_dw_out(act, dxb, f"ffn{tag}_down_dw")
    dz = dz4.reshape((2 * nj,) + dz4.shape[2:])
    dx, dxb, dg = mm_dx_in(dz, w_up, f"ffn{tag}_up_dx", norm=(x, norm_g, dx))
    dw_up = mm_dw_in(h, dz, 2 * nj, f"ffn{tag}_up_dw", transposed=True)
    return dx, dxb, dict(norm_g=dg, w_up=dw_up, conv_w=dcw, conv_b=dcb, w_down=dw_down)


def kernel(x, a_norm_g, a_w_in, a_b_in, a_v_norm_g, a_w_s, a_b_s, a_w_out, b_norm_g, b_w_in, b_w_grp, b_b_grp, b_scale, b_w_out, c_norm_g, c_w_in, c_b_in, c_conv_w, c_conv_b, c_w_a, c_b_a, c_w_i, c_b_i, c_lambda, c_w_out, d_norm_g, d_w_in, d_conv_w, d_w_out, ffn_norm_g, ffn_w_up, ffn_conv_w, ffn_conv_b, ffn_w_down, final_norm_g, loss_target, m_a_norm_g, m_a_w_in, m_a_b_in, m_a_v_norm_g, m_a_w_s, m_a_b_s, m_a_w_out, m_b_norm_g, m_b_w_in, m_b_w_grp, m_b_b_grp, m_b_scale, m_b_w_out, m_c_norm_g, m_c_w_in, m_c_b_in, m_c_conv_w, m_c_conv_b, m_c_w_a, m_c_b_a, m_c_w_i, m_c_b_i, m_c_lambda, m_c_w_out, m_d_norm_g, m_d_w_in, m_d_conv_w, m_d_w_out, m_ffn_norm_g, m_ffn_w_up, m_ffn_conv_w, m_ffn_conv_b, m_ffn_w_down, m_final_norm_g, v_a_norm_g, v_a_w_in, v_a_b_in, v_a_v_norm_g, v_a_w_s, v_a_b_s, v_a_w_out, v_b_norm_g, v_b_w_in, v_b_w_grp, v_b_b_grp, v_b_scale, v_b_w_out, v_c_norm_g, v_c_w_in, v_c_b_in, v_c_conv_w, v_c_conv_b, v_c_w_a, v_c_b_a, v_c_w_i, v_c_b_i, v_c_lambda, v_c_w_out, v_d_norm_g, v_d_w_in, v_d_conv_w, v_d_w_out, v_ffn_norm_g, v_ffn_w_up, v_ffn_conv_w, v_ffn_conv_b, v_ffn_w_down, v_final_norm_g):
    args = locals()
    w_loc = {n: args[n] for n in WEIGHTS}
    m_loc = {n: args["m_" + n] for n in WEIGHTS}
    v_loc = {n: args["v_" + n] for n in WEIGHTS}
    depth = ffn_w_up.shape[0]
    xs = x[0]
    target = loss_target[0]

    small_names = list(SMALL_SHARDED)
    small_shapes = [w_loc[n].shape for n in small_names]
    small_rows = _padded_rows(small_shapes, 8)
    small_packed = _pack([w_loc[n] for n in small_names], small_rows)
    early_names = ['ffn_conv_w', 'b_norm_g', 'c_norm_g', 'd_norm_g']
    late_names = [n for n in small_names if n not in early_names]
    packs = []
    for names in (early_names, late_names):
        shapes = [w_loc[n].shape for n in names]
        packs.append((names, shapes, _pack([w_loc[n] for n in names], _padded_rows(shapes, 8))))
    mixers = [(a_w_in, a_w_out), (b_w_in, b_w_out), (c_w_in, c_w_out), (d_w_in, d_w_out)]
    groups = {}
    for l in range(depth):
        groups[f'mixer{l}'] = [mixers[l][0][0].astype(BF16), mixers[l][1][0].astype(BF16)] + ([packs[1][2]] if l == 1 else [])
        groups[f'ffn{l}'] = [ffn_w_up[l].astype(BF16), ffn_w_down[l].astype(BF16)] + ([packs[0][2]] if l == 0 else [])
    two_level = ('mixer0', 'ffn0', 'mixer1', 'ffn1')
    in_flight = dict(zip(groups, gather_start(list(groups.values()),
                                              [SAME_CORE_PEERS if k in two_level else ALL_PEERS for k in groups], "gather_start")))
    forwarding = {}

    def forward_on(key, after):
        forwarding[key] = forward_start(gather_wait(in_flight[key], after, f"gather_wait_{key}"), f"forward_start_{key}")
        return (forwarding[key][2],)

    def gathered(key, after):
        if key in two_level:
            return forward_wait(forwarding[key], after, f"forward_wait_{key}")
        return gather_wait(in_flight[key], after, f"gather_wait_{key}")

    sm = {}

    def unpack_small(pack, gathered):
        names, shapes, _ = pack
        sm.update((n, _to_full(s, SMALL_SHARDED[n])) for n, s in zip(names, _unpack(gathered, shapes, (N_DEV,))))

    def rows_full(st):
        return st.reshape((st.shape[0] * st.shape[1],) + st.shape[2:])

    nb = N_DEV
    ffn_cb = [ffn_conv_b[l].reshape(2, nb // 2, 1, -1) for l in range(depth)]
    ffn_g = [ffn_norm_g[l:l + 1] for l in range(depth)]
    a_bst = a_b_s[0].T
    w_up, w_down, saved = [None] * depth, [None] * depth, {}

    def ffn_forward(xl, hl, l, next_g, after=()):
        up, down, *early = gathered(f'ffn{l}', (xl,))
        if early:
            unpack_small(packs[0], early[0])
            ffn_cw.extend(sm['ffn_conv_w'][k].reshape(ffn_conv_w.shape[1], 2, nb // 2, -1).transpose(1, 2, 0, 3)
                          for k in range(depth))
        w_up[l], w_down[l] = up, rows_full(down)
        return _ffn_forward(xl, hl, w_up[l], ffn_cw[l], ffn_cb[l], w_down[l], next_g(), l, after)

    ffn_cw = []
    h = rms_fwd(xs, a_norm_g, "a_norm")
    forward_on('mixer0', (h, small_packed))
    wa_in, wa_out = gathered('mixer0', (h,))
    wa_out = rows_full(wa_out)
    z = mm_in(h, wa_in, "a_in", bias=a_b_in)
    y = a_mid_fwd(z, a_v_norm_g, a_w_s[0], a_bst, "a_mid")
    x1, hf = mm_out(y, wa_out, xs, ffn_g[0], "a_out", forward_on('ffn0', (y,)))
    saved['a'] = (xs, h, z, y)
    (x1, h), saved['f0'] = ffn_forward(x1, hf, 0, lambda: sm['b_norm_g'], forward_on('mixer1', (x1,)))

    sent = forward_on('ffn1', (x1,))
    wb_in, wb_out, late = gathered('mixer1', (x1,))
    unpack_small(packs[1], late)
    b_wgrp, c_wa, c_wi = (sm[n][0].astype(BF16) for n in ('b_w_grp', 'c_w_a', 'c_w_i'))
    b_bgrp, c_ba, c_bi = (sm[n][0].reshape(1, -1) for n in ('b_b_grp', 'c_b_a', 'c_b_i'))
    wb_in, wb_out = rows_full(wb_in)[None], rows_full(wb_out)
    z = mm_in(h, wb_in, "b_in", after=sent)
    y, x2, hf = b_mid_fwd(z, b_wgrp, b_bgrp, sm['b_scale'], wb_out, x1, ffn_g[1], "b_mid_out")
    saved['b'] = (x1, h, z, y)
    (x2, h), saved['f1'] = ffn_forward(x2, hf, 1, lambda: sm['c_norm_g'])

    wc_in, wc_out = gathered('mixer2', (x2,))
    wc_out = rows_full(wc_out)
    z = mm_in(h, wc_in, "c_in", bias=sm['c_b_in'])
    c_cw = sm['c_conv_w'][0]
    a_seq, b_seq, xr = c_mid_fwd(z, c_cw, sm['c_conv_b'], c_wa, c_ba, c_wi, c_bi, sm['c_lambda'], "c_mid")
    hs, y, x3, hf = c_scan_fwd(a_seq, b_seq, z, wc_out, x2, ffn_g[2], "c_scan_out")
    saved['c'] = (x2, h, z, y, a_seq, xr, hs)
    (x3, h), saved['f2'] = ffn_forward(x3, hf, 2, lambda: sm['d_norm_g'])

    wd_in, wd_out = gathered('mixer3', (x3,))
    wd_out = rows_full(wd_out)
    z = mm_in(h, wd_in, "d_in")
    d_cw = sm['d_conv_w'][0]
    y, x4, hf = d_mid_fwd(z, d_cw, wd_out, x3, ffn_g[3], "d_mid_out")
    saved['d'] = (x3, h, z, y)
    (x4, _), saved['f3'] = ffn_forward(x4, hf, 3, lambda: None)

    loss_part, dx, dxb, d_final_g = final_loss(x4, final_norm_g.reshape(1, -1), target, "final_loss")
    loss = lax.psum(loss_part[0, 0], ("x", "y", "c"))

    def rows_stacked(full):
        return full.reshape((N_DEV, full.shape[0] // N_DEV) + full.shape[1:])

    mx, my, mc = _place()
    own = jnp.stack([_dev_index(*_chip_of(mx, my, k), mc) for k in range(4)]).astype(jnp.int32)
    repl_shapes = [w_loc[n].shape for n in REPLICATED]
    repl_rows = _padded_rows(repl_shapes, 8 * N_DEV)
    shard_of = {n: (w_loc[n][0], m_loc[n][0], v_loc[n][0])
                for n in ('a_w_in', 'a_w_out', 'b_w_in', 'b_w_out', 'c_w_in', 'c_w_out', 'd_w_in', 'd_w_out')}
    shard_of['small'] = (small_packed, _pack([m_loc[n] for n in small_names], small_rows),
                         _pack([v_loc[n] for n in small_names], small_rows))
    updated = {}

    def finish(n, part, others):
        if n == 'repl':
            chunk = sum_parts(part, others, "rs_sum_repl")
            repl_g = all_gather([chunk], "gather_repl")[0].reshape(repl_rows, LANES)
            updated[n] = adamw(*(_pack([src[k] for k in REPLICATED], repl_rows) for src in (w_loc, m_loc, v_loc)),
                               "adamw_repl", g=repl_g)
        elif n.startswith('ffn_w'):
            base, l = n[:-1], int(n[-1])
            turn = (lambda t: jnp.swapaxes(t, 1, 2)) if base == 'ffn_w_up' else (lambda t: t)
            updated[base] = adamw_layer(turn(w_loc[base]), turn(m_loc[base]), turn(v_loc[base]), l, updated.get(base),
                                        f"adamw_{n}", part, others)
            return updated[base][1]
        else:
            updated[n] = adamw(*shard_of[n], f"adamw_{n}", p=part, l2=others)
        return updated[n][1]

    def light(arrays):
        return tuple(a for a in arrays if a.size <= 8 * LANES)

    stages = [None, []]

    def advance(tag, new, after, finish_older=True):
        behind = []
        first = None
        if new:
            first = ([n for n, _ in new], sibling_start([g for _, g in new], f"rs_sibling_start{tag}"))
            behind.append(first[1][3])
        older = stages[1]
        if finish_older:
            for k, (names, started) in enumerate(older):
                parts, others = chips_wait(started, after, f"rs_chips_wait{tag}_{k}")
                behind += [finish(n, p, o) for n, p, o in zip(names, parts, others)]
            older = []
        if stages[0] is not None:
            names, started = stages[0]
            grads, got = sibling_wait(started, after, f"rs_sibling_wait{tag}")
            parts = [add_pairs(g, l, own, f"rs_add_{n}") for n, g, l in zip(names, grads, got)]
            older = older + [(names, chips_start(parts, f"rs_chips_start{tag}"))]
            behind.append(older[-1][1][3])
        stages[:] = [first, older]
        return tuple(behind)

    gf = [None] * depth
    dx, dxb, gf[3] = _ffn_backward(dx, dxb, saved['f3'], ffn_g[3], w_up[3], ffn_cw[3], w_down[3], 3)
    xin, h, z, y = saved['d']
    dy = mm_dx_out(dxb, wd_out, "d_out_dx")
    g_d_w_out = mm_dw_out(y, dxb, "d_out_dw")
    dz, g_d_conv_w = d_mid_bwd(z, dy, d_cw, "d_mid_bwd")
    dx, dxb, g_d_norm_g = mm_dx_in(dz, wd_in, "d_in_dx", norm=(xin, sm['d_norm_g'], dx))
    g_d_w_in = mm_dw_in(h, dz, nb, "d_in_dw")
    behind = advance(0, [('ffn_w_up3', gf[3]['w_up']), ('ffn_w_down3', rows_stacked(gf[3]['w_down'])), ('d_w_in', g_d_w_in),
                         ('d_w_out', rows_stacked(g_d_w_out))], (g_d_norm_g,))

    dx, dxb, gf[2] = _ffn_backward(dx, dxb, saved['f2'], ffn_g[2], w_up[2], ffn_cw[2], w_down[2], 2, behind)
    xin, h, z, y, a_seq, xr, hs = saved['c']
    g_c_w_out = mm_dw_out(y, dxb, "c_out_dw")
    lam_seq, da_seq, dgate, dgate_sum = c_scan_bwd(dxb, wc_out, z, hs, a_seq, "c_scan_bwd")
    dxr, g_c_w_a, g_c_w_i, g_c_b_a, g_c_b_i, g_c_lambda = c_mid_bwd(
        lam_seq, da_seq, xr, c_wa, c_ba, c_wi, c_bi, sm['c_lambda'], "c_mid_bwd")
    dxr_pre, g_c_conv_w, g_c_conv_b, dxr_pre_sum = conv_bwd(dxr, z, 1, c_cw, "c_conv_bwd")
    dz = jnp.concatenate([dgate, dxr_pre], axis=1)
    g_c_b_in = jnp.concatenate([dgate_sum, dxr_pre_sum], axis=1)
    dx, dxb, g_c_norm_g = mm_dx_in(dz, wc_in, "c_in_dx", norm=(xin, sm['c_norm_g'], dx))
    g_c_w_in = mm_dw_in(h, dz, nb, "c_in_dw")
    behind = advance(1, [('ffn_w_up2', gf[2]['w_up']), ('ffn_w_down2', rows_stacked(gf[2]['w_down'])), ('c_w_in', g_c_w_in),
                         ('c_w_out', rows_stacked(g_c_w_out))], (g_c_norm_g,))

    dx, dxb, gf[1] = _ffn_backward(dx, dxb, saved['f1'], ffn_g[1], w_up[1], ffn_cw[1], w_down[1], 1, behind)
    xin, h, z, y = saved['b']
    g_b_w_out = mm_dw_out(y, dxb, "b_out_dw")
    dp, g_b_w_grp, g_b_b_grp, g_b_scale = b_mid_bwd(z, dxb, wb_out, b_wgrp, b_bgrp, sm['b_scale'], "b_mid_bwd")
    dz = b_pool_bwd(dp, "b_pool_bwd")
    dx, dxb, g_b_norm_g = mm_dx_in(dz, wb_in, "b_in_dx", norm=(xin, sm['b_norm_g'], dx))
    g_b_w_in = mm_dw_in(h, dz, 1, "b_in_dw")
    behind = advance(2, [('ffn_w_up1', gf[1]['w_up']), ('ffn_w_down1', rows_stacked(gf[1]['w_down'])),
                         ('b_w_in', rows_stacked(g_b_w_in[0])), ('b_w_out', rows_stacked(g_b_w_out))], (g_b_norm_g,))

    dx, dxb, gf[0] = _ffn_backward(dx, dxb, saved['f0'], ffn_g[0], w_up[0], ffn_cw[0], w_down[0], 0, behind)
    full_small = {
        'b_norm_g': g_b_norm_g, 'b_w_grp': g_b_w_grp[None], 'b_b_grp': g_b_b_grp.reshape(b_b_grp.shape[:2] + (-1,)),
        'b_scale': g_b_scale, 'c_norm_g': g_c_norm_g, 'c_b_in': g_c_b_in, 'c_conv_w': g_c_conv_w[None],
        'c_conv_b': g_c_conv_b, 'c_w_a': g_c_w_a[None], 'c_b_a': g_c_b_a.reshape(c_b_a.shape[:2] + (-1,)),
        'c_w_i': g_c_w_i[None], 'c_b_i': g_c_b_i.reshape(c_b_i.shape[:2] + (-1,)), 'c_lambda': g_c_lambda,
        'd_norm_g': g_d_norm_g, 'd_conv_w': g_d_conv_w[None],
        'ffn_conv_w': jnp.stack([gf[l]['conv_w'].transpose(2, 0, 1, 3).reshape(ffn_conv_w.shape[1], -1) for l in range(depth)])}
    small_grads = _pack_stacked([_to_stacked(full_small[n], SMALL_SHARDED[n]) for n in small_names], small_rows)
    behind = advance(3, [('ffn_w_up0', gf[0]['w_up']), ('ffn_w_down0', rows_stacked(gf[0]['w_down'])), ('small', small_grads)],
                     (gf[0]['norm_g'],))

    xin, h, z, y = saved['a']
    g_a_w_out = mm_dw_out(y, dxb, "a_out_dw")
    behind += advance(4, [('a_w_out', rows_stacked(g_a_w_out))], light(behind))
    dz, g_a_b_in, g_a_v_norm_g, g_a_w_s, g_a_b_s = a_mid_bwd(z, dxb, wa_out, a_v_norm_g, a_w_s[0], a_bst, "a_mid_bwd", behind)
    g_a_w_in = mm_dw_in(h, dz, nb, "a_in_dw")
    behind = advance(5, [('a_w_in', g_a_w_in)], (g_a_b_in,))
    behind = advance('5b', [], (g_a_b_in, *light(behind)), finish_older=False)
    dx, _, g_a_norm_g = mm_dx_in(dz, wa_in, "a_in_dx", behind, norm=(xin, a_norm_g, dx))
    grad_x = dx[None]

    tril = jnp.tril(jnp.ones((A_CHUNK, A_CHUNK), bool))
    repl_full = {
        'a_norm_g': g_a_norm_g, 'a_b_in': g_a_b_in, 'a_v_norm_g': g_a_v_norm_g,
        'a_w_s': jnp.where(tril, g_a_w_s, 0.0)[None], 'a_b_s': g_a_b_s[:, ::LANES].T[None],
        'ffn_norm_g': jnp.concatenate([gf[l]['norm_g'] for l in range(depth)], axis=0),
        'ffn_conv_b': jnp.stack([gf[l]['conv_b'].reshape(-1) for l in range(depth)]), 'final_norm_g': d_final_g.reshape(-1)}
    repl_grads = _pack([repl_full[n] for n in REPLICATED], repl_rows).reshape(N_DEV, repl_rows // N_DEV, LANES)
    behind = advance(6, [('repl', repl_grads)], (g_a_norm_g,))
    behind = advance(7, [], (g_a_norm_g, *light(behind)))
    advance(8, [], (g_a_norm_g, *light(behind)))

    outs = [{}, {}, {}, {}]
    for i, dst in enumerate(outs):
        for n in ('a_w_in', 'a_w_out', 'b_w_in', 'b_w_out', 'c_w_in', 'c_w_out', 'd_w_in', 'd_w_out'):
            dst[n] = updated[n][i][None]
        dst['ffn_w_up'] = jnp.swapaxes(updated['ffn_w_up'][i], 1, 2)
        dst['ffn_w_down'] = updated['ffn_w_down'][i]
        dst.update(zip(small_names, _unpack(updated['small'][i], small_shapes)))
        dst.update(zip(REPLICATED, _unpack(updated['repl'][i], repl_shapes)))
    out_g, out_d, out_m, out_v = outs

    return (loss, grad_x, *[out_g[n] for n in WEIGHTS], *[out_d[n] for n in WEIGHTS], *[out_m[n] for n in WEIGHTS],
            *[out_v[n] for n in WEIGHTS])
```
